```python
import math
import jax, jax.numpy as jnp
from jax import lax
import numpy as np

D_MODEL = 1024
BATCH = 16
SEQ = 2048
DEPTH = 2

BLOCK = 128
ROPE_THETA = 500000.0
ROPE_FRACTION = 4
EPS = 1e-6
CONV_CH = 512
CONV_K = 3
DIFF_HEADS = 4
DIFF_DK = 64
DIFF_DV = 128
SWA_Q_HEADS = 16
SWA_KV_HEADS = 4
SWA_HEAD_DIM = 64
SWA_WINDOW = 128
MEM_LEN = 256
XATTN_HEADS = 4
XATTN_HEAD_DIM = 128
N_GROUPS = 4
EXPERTS_PER_GROUP = 8
N_EXPERTS = N_GROUPS * EXPERTS_PER_GROUP
TOP_K = 2
EXPERT_HIDDEN = 512

A_IN = 3 * CONV_CH
DIFF_QK = DIFF_HEADS * 2 * DIFF_DK
B_IN = 2 * DIFF_QK + DIFF_HEADS * DIFF_DV
HYB_IN = A_IN + B_IN
HYB_OUT = CONV_CH + DIFF_HEADS * DIFF_DV
SWA_QW = SWA_Q_HEADS * SWA_HEAD_DIM
SWA_KVW = SWA_KV_HEADS * SWA_HEAD_DIM
SWA_IN = SWA_QW + 2 * SWA_KVW
XATTN_W = XATTN_HEADS * XATTN_HEAD_DIM

kernel_name = "hybrid_conv_diffattn_swa_sink_xmem_hmoe"


def rmsnorm(x, g):
    xf = x.astype(jnp.float32)
    y = xf * lax.rsqrt(jnp.mean(xf * xf, axis=-1, keepdims=True) + EPS)
    return (y * g.astype(jnp.float32)).astype(x.dtype)


def rotary_tables(seq, head_dim):
    rot = head_dim // ROPE_FRACTION
    pos = jnp.arange(seq, dtype=jnp.float32)
    inv = ROPE_THETA ** (-jnp.arange(0, rot, 2, dtype=jnp.float32) / rot)
    ang = pos[:, None] * inv[None, :]
    return jnp.cos(ang), jnp.sin(ang)


def rope_partial(x, cos, sin):
    half = cos.shape[-1]
    extra = x.ndim - 3
    c = cos.reshape((cos.shape[0],) + (1,) * extra + (half,)).astype(x.dtype)
    s = sin.reshape((sin.shape[0],) + (1,) * extra + (half,)).astype(x.dtype)
    x1, x2, rest = x[..., :half], x[..., half:2 * half], x[..., 2 * half:]
    return jnp.concatenate([x1 * c - x2 * s, x2 * c + x1 * s, rest], axis=-1)


def short_conv(u, w):
    C = u.shape[-1]
    return lax.conv_general_dilated(
        u, w.astype(u.dtype)[:, None, :], window_strides=(1,), padding=[(CONV_K - 1, 0)],
        dimension_numbers=('NWC', 'WIO', 'NWC'), feature_group_count=C)


def diff_attention(q, k, v, lam):
    B, S, H, _, D = q.shape
    nb = S // BLOCK
    scale = D ** -0.5
    qb = q.reshape(B, nb, BLOCK, H, 2, D).swapaxes(0, 1)
    kpos = jnp.arange(S)

    def block(args):
        qblk, n = args
        qpos = n * BLOCK + jnp.arange(BLOCK)
        s = jnp.einsum('bqhmd,bkhmd->bhmqk', qblk, k).astype(jnp.float32) * scale
        s = jnp.where(kpos[None, :] <= qpos[:, None], s, -jnp.inf)
        p = jax.nn.softmax(s, axis=-1)
        a = p[:, :, 0] - lam * p[:, :, 1]
        return jnp.einsum('bhqk,bkhe->bqhe', a.astype(v.dtype), v)

    o = lax.map(block, (qb, jnp.arange(nb)))
    return o.swapaxes(0, 1).reshape(B, S, H, v.shape[-1])


def hybrid_mixer(h, w_in, conv_w, lam_vecs, subln_g, w_out, lambda_init, cos, sin):
    B, S, _ = h.shape
    p = h @ w_in
    a_in, b_in = p[..., :A_IN], p[..., A_IN:]
    gate_b, gate_c, u = jnp.split(a_in, 3, axis=-1)
    y_a = gate_b * short_conv(gate_c * u, conv_w)
    q = b_in[..., :DIFF_QK].reshape(B, S, DIFF_HEADS, 2, DIFF_DK)
    k = b_in[..., DIFF_QK:2 * DIFF_QK].reshape(B, S, DIFF_HEADS, 2, DIFF_DK)
    v = b_in[..., 2 * DIFF_QK:].reshape(B, S, DIFF_HEADS, DIFF_DV)
    q = rope_partial(q, cos, sin)
    k = rope_partial(k, cos, sin)
    lf = lam_vecs.astype(jnp.float32)
    lam = jnp.exp(jnp.sum(lf[0] * lf[1])) - jnp.exp(jnp.sum(lf[2] * lf[3])) + lambda_init
    o = diff_attention(q, k, v, lam)
    o = rmsnorm(o, subln_g) * (1.0 - lambda_init)
    y_b = o.reshape(B, S, DIFF_HEADS * DIFF_DV)
    return jnp.concatenate([y_a, y_b], axis=-1) @ w_out


def swa_mixer(h, w_qkv, b_qkv, sinks, w_out, b_out, cos, sin):
    B, S, _ = h.shape
    G = SWA_Q_HEADS // SWA_KV_HEADS
    nb = S // BLOCK
    p = h @ w_qkv + b_qkv
    q = p[..., :SWA_QW].reshape(B, S, SWA_Q_HEADS, SWA_HEAD_DIM)
    k = p[..., SWA_QW:SWA_QW + SWA_KVW].reshape(B, S, SWA_KV_HEADS, SWA_HEAD_DIM)
    v = p[..., SWA_QW + SWA_KVW:].reshape(B, S, SWA_KV_HEADS, SWA_HEAD_DIM)
    q = rope_partial(q, cos, sin)
    k = rope_partial(k, cos, sin)
    qb = q.reshape(B, nb, BLOCK, SWA_KV_HEADS, G, SWA_HEAD_DIM)

    def band(t):
        tb = t.reshape(B, nb, BLOCK, SWA_KV_HEADS, SWA_HEAD_DIM)
        prev = jnp.concatenate([jnp.zeros_like(tb[:, :1]), tb[:, :-1]], axis=1)
        return jnp.concatenate([prev, tb], axis=2)

    kb, vb = band(k), band(v)
    s = jnp.einsum('bnqkgd,bnskd->bnkgqs', qb, kb).astype(jnp.float32) * (SWA_HEAD_DIM ** -0.5)
    qi = jnp.arange(BLOCK)[:, None]
    si = jnp.arange(2 * BLOCK)[None, :]
    rel = si - BLOCK - qi
    in_win = (rel <= 0) & (rel > -SWA_WINDOW)
    kabs = jnp.arange(nb)[:, None, None] * BLOCK + si[None] - BLOCK
    mask = in_win[None] & (kabs >= 0)
    s = jnp.where(mask[None, :, None, None], s, -jnp.inf)
    sink = jnp.broadcast_to(sinks.astype(jnp.float32).reshape(1, 1, SWA_KV_HEADS, G, 1, 1),
                            s.shape[:-1] + (1,))
    pr = jax.nn.softmax(jnp.concatenate([s, sink], axis=-1), axis=-1)[..., :-1]
    o = jnp.einsum('bnkgqs,bnskd->bnqkgd', pr.astype(vb.dtype), vb).reshape(B, S, SWA_QW)
    return o @ w_out + b_out


def cross_attention(h, mem_k, mem_v, w_q, w_o):
    B, S, _ = h.shape
    q = (h @ w_q).reshape(B, S, XATTN_HEADS, XATTN_HEAD_DIM)
    s = jnp.einsum('bshd,bmhd->bhsm', q, mem_k).astype(jnp.float32) * (XATTN_HEAD_DIM ** -0.5)
    p = jax.nn.softmax(s, axis=-1)
    o = jnp.einsum('bhsm,bmhd->bshd', p.astype(mem_v.dtype), mem_v).reshape(B, S, XATTN_W)
    return o @ w_o


def hier_moe(h, w_group, b_group, w_router, b_router, w_gate_up, w_down):
    B, S, D = h.shape
    t = h.reshape(-1, D)
    T = t.shape[0]
    p_group = jax.nn.softmax((t @ w_group + b_group).astype(jnp.float32), axis=-1)
    g_sel = jnp.argmax(p_group, axis=-1)
    p_g = jnp.take_along_axis(p_group, g_sel[:, None], axis=1)[:, 0]
    e_logits = (t @ w_router + b_router).astype(jnp.float32).reshape(T, N_GROUPS, EXPERTS_PER_GROUP)
    e_logits = jnp.take_along_axis(e_logits, g_sel[:, None, None], axis=1)[:, 0]
    top_w, top_i = lax.top_k(jax.nn.softmax(e_logits, axis=-1), TOP_K)
    top_w = top_w / jnp.sum(top_w, axis=-1, keepdims=True) * p_g[:, None]
    expert_id = (g_sel[:, None] * EXPERTS_PER_GROUP + top_i).reshape(-1)
    flat_w = top_w.reshape(-1)
    flat_tok = jnp.repeat(jnp.arange(T), TOP_K)
    order = jnp.argsort(expert_id)
    tok_sorted = flat_tok[order]
    xs = t[tok_sorted]
    group_sizes = jnp.bincount(expert_id, length=N_EXPERTS).astype(jnp.int32)
    gu = lax.ragged_dot(xs, w_gate_up, group_sizes)
    hid = jax.nn.silu(gu[:, :EXPERT_HIDDEN]) * gu[:, EXPERT_HIDDEN:]
    y = lax.ragged_dot(hid, w_down, group_sizes) * flat_w[order][:, None].astype(hid.dtype)
    out = jax.ops.segment_sum(y, tok_sorted, num_segments=T)
    return out.reshape(B, S, D)


def setup_inputs(seed: int = 0) -> dict:
    key = jax.random.key(seed)
    ks = iter(jax.random.split(key, 32))
    n_even = (DEPTH + 1) // 2
    n_odd = DEPTH // 2
    f32 = jnp.float32

    def nrm(shape, scale):
        return jax.random.normal(next(ks), shape, f32) * scale

    def gain(shape):
        return 1.0 + 0.05 * jax.random.normal(next(ks), shape, f32)

    D = D_MODEL
    return {
        "x": nrm((BATCH, SEQ, D), 1.0),
        "mem": nrm((BATCH, MEM_LEN, D), 1.0),
        "mem_norm": gain((D,)),
        "mem_w_kv": nrm((D, 2 * XATTN_W), D ** -0.5),
        "norm_mix": gain((DEPTH, D)),
        "norm_xattn": gain((DEPTH, D)),
        "norm_ffn": gain((DEPTH, D)),
        "hyb_w_in": nrm((n_even, D, HYB_IN), D ** -0.5),
        "hyb_conv_w": nrm((n_even, CONV_K, CONV_CH), CONV_K ** -0.5),
        "diff_lambda": nrm((n_even, 4, DIFF_DK), 0.1),
        "diff_subln": gain((n_even, DIFF_DV)),
        "hyb_w_out": nrm((n_even, HYB_OUT, D), HYB_OUT ** -0.5),
        "swa_w_qkv": nrm((n_odd, D, SWA_IN), D ** -0.5),
        "swa_b_qkv": nrm((n_odd, SWA_IN), 0.02),
        "swa_sinks": nrm((n_odd, SWA_Q_HEADS), 0.5),
        "swa_w_out": nrm((n_odd, SWA_QW, D), SWA_QW ** -0.5),
        "swa_b_out": nrm((n_odd, D), 0.02),
        "xattn_w_q": nrm((DEPTH, D, XATTN_W), D ** -0.5),
        "xattn_w_o": nrm((DEPTH, XATTN_W, D), XATTN_W ** -0.5),
        "moe_w_group": nrm((DEPTH, D, N_GROUPS), D ** -0.5),
        "moe_b_group": nrm((DEPTH, N_GROUPS), 0.01),
        "moe_w_router": nrm((DEPTH, D, N_EXPERTS), D ** -0.5),
        "moe_b_router": nrm((DEPTH, N_EXPERTS), 0.01),
        "moe_w_gate_up": nrm((DEPTH, N_EXPERTS, D, 2 * EXPERT_HIDDEN), D ** -0.5),
        "moe_w_down": nrm((DEPTH, N_EXPERTS, EXPERT_HIDDEN, D), EXPERT_HIDDEN ** -0.5),
        "final_norm": gain((D,)),
    }


def reference(x, mem, mem_norm, mem_w_kv, norm_mix, norm_xattn, norm_ffn,
              hyb_w_in, hyb_conv_w, diff_lambda, diff_subln, hyb_w_out,
              swa_w_qkv, swa_b_qkv, swa_sinks, swa_w_out, swa_b_out,
              xattn_w_q, xattn_w_o, moe_w_group, moe_b_group, moe_w_router, moe_b_router,
              moe_w_gate_up, moe_w_down, final_norm):
    B, S, _ = x.shape
    M = mem.shape[1]
    mkv = rmsnorm(mem, mem_norm) @ mem_w_kv
    mem_k = mkv[..., :XATTN_W].reshape(B, M, XATTN_HEADS, XATTN_HEAD_DIM)
    mem_v = mkv[..., XATTN_W:].reshape(B, M, XATTN_HEADS, XATTN_HEAD_DIM)
    cos_d, sin_d = rotary_tables(S, DIFF_DK)
    cos_s, sin_s = rotary_tables(S, SWA_HEAD_DIM)
    for l in range(DEPTH):
        h = rmsnorm(x, norm_mix[l])
        if l % 2 == 0:
            e = l // 2
            lambda_init = 0.8 - 0.6 * math.exp(-0.3 * l)
            x = x + hybrid_mixer(h, hyb_w_in[e], hyb_conv_w[e], diff_lambda[e], diff_subln[e],
                                 hyb_w_out[e], lambda_init, cos_d, sin_d)
        else:
            o = l // 2
            x = x + swa_mixer(h, swa_w_qkv[o], swa_b_qkv[o], swa_sinks[o], swa_w_out[o],
                              swa_b_out[o], cos_s, sin_s)
        x = x + cross_attention(rmsnorm(x, norm_xattn[l]), mem_k, mem_v, xattn_w_q[l], xattn_w_o[l])
        x = x + hier_moe(rmsnorm(x, norm_ffn[l]), moe_w_group[l], moe_b_group[l], moe_w_router[l],
                         moe_b_router[l], moe_w_gate_up[l], moe_w_down[l])
    return rmsnorm(x, final_norm)
```

```python
import functools
import math

import jax
import jax.numpy as jnp
from jax import lax
from jax.experimental import pallas as pl
from jax.experimental.pallas import tpu as pltpu

F32 = jnp.float32
BF16 = jnp.bfloat16
I32 = jnp.int32

EPS = 1e-6
LANES = 128
VMEM_LIMIT = 48 * 1024 * 1024

ROPE_THETA = 500000.0
ROPE_FRACTION = 4
BLOCK = 128
CONV_CH = 512
CONV_K = 3
DIFF_HEADS = 4
DIFF_DK = 64
DIFF_DV = 128
SWA_Q_HEADS = 16
SWA_KV_HEADS = 4
SWA_HEAD_DIM = 64
SWA_WINDOW = 128
XATTN_HEADS = 4
XATTN_HEAD_DIM = 128
N_GROUPS = 4
EXPERTS_PER_GROUP = 8
N_EXPERTS = N_GROUPS * EXPERTS_PER_GROUP
TOP_K = 2
EXPERT_HIDDEN = 512

ROW_TILE = 256
EXPERT_TILE = 512
NEG_INF = float("-inf")


def _cparams(sem):
    return pltpu.CompilerParams(dimension_semantics=sem, vmem_limit_bytes=VMEM_LIMIT)


def _rms(x, g):
    return x * lax.rsqrt(jnp.mean(x * x, axis=-1, keepdims=True) + EPS) * g


def _dot(a, b):
    return jnp.dot(a, b, preferred_element_type=F32)


def _dot_nt(a, b):
    return lax.dot_general(a, b, (((1,), (1,)), ((), ())), preferred_element_type=F32)


def _rope_lane_tables(seq, head_dim, scale):
    rot = head_dim // ROPE_FRACTION
    half = rot // 2
    pos = jnp.arange(seq, dtype=F32)
    inv = ROPE_THETA ** (-jnp.arange(0, rot, 2, dtype=F32) / rot)
    ang = pos[:, None] * inv[None, :]
    cos, sin = jnp.cos(ang), jnp.sin(ang)
    idx = jnp.arange(LANES) % head_dim
    cl = jnp.take(cos, idx % half, axis=1)
    sl = jnp.take(sin, idx % half, axis=1)
    c = jnp.where(idx < rot, cl, 1.0) * scale
    s_up = jnp.where(idx < half, -sl, 0.0) * scale
    s_dn = jnp.where((idx >= half) & (idx < rot), sl, 0.0) * scale
    return c.astype(F32), s_up.astype(F32), s_dn.astype(F32), half


def _rope_chunk(xc, c, s_up, s_dn, half):
    return (xc * c + pltpu.roll(xc, LANES - half, 1) * s_up + pltpu.roll(xc, half, 1) * s_dn)


def _norm_proj_kernel(x_ref, g_ref, w_ref, o_ref):
    h = _rms(x_ref[...], g_ref[...]).astype(BF16)
    o_ref[...] = _dot(h, w_ref[...]).astype(o_ref.dtype)


def _norm_proj(x2d, g, w_bf16, tm):
    m, d = x2d.shape
    n = w_bf16.shape[1]
    return pl.pallas_call(
        _norm_proj_kernel,
        grid=(m // tm,),
        in_specs=[pl.BlockSpec((tm, d), lambda i: (i, 0)),
                  pl.BlockSpec((1, d), lambda i: (0, 0)),
                  pl.BlockSpec((d, n), lambda i: (0, 0))],
        out_specs=pl.BlockSpec((tm, n), lambda i: (i, 0)),
        out_shape=jax.ShapeDtypeStruct((m, n), BF16),
        compiler_params=_cparams(("arbitrary",)),
        name="mem_kv_proj",
    )(x2d, g.reshape(1, d), w_bf16)


def _hyb_front_kernel(x_ref, g_ref, w_ref, cw_ref, cq_ref, suq_ref, sdq_ref, ck_ref, suk_ref, sdk_ref,
                      ya_ref, q_ref, k_ref, v_ref, cbuf, *, tm, half):
    s = pl.program_id(1)
    h = _rms(x_ref[0], g_ref[...]).astype(BF16)
    p = _dot(h, w_ref[...])
    c = CONV_CH
    gate_b, gate_c, u = p[:, 0:c], p[:, c:2 * c], p[:, 2 * c:3 * c]
    cu = gate_c * u

    @pl.when(s == 0)
    def _():
        cbuf[0:8, :] = jnp.zeros((8, c), F32)

    cbuf[8:8 + tm, :] = cu
    cw = cw_ref[...]
    conv = (cw[0:1, :] * cbuf[6:6 + tm, :] + cw[1:2, :] * cbuf[7:7 + tm, :] + cw[2:3, :] * cu)
    ya_ref[0] = (gate_b * conv).astype(BF16)
    cbuf[0:8, :] = cbuf[tm:tm + 8, :]

    base = 3 * c
    nq = DIFF_HEADS * 2 * DIFF_DK
    cq, suq, sdq = cq_ref[...], suq_ref[...], sdq_ref[...]
    ck, suk, sdk = ck_ref[...], suk_ref[...], sdk_ref[...]
    for j in range(nq // LANES):
        lo = base + j * LANES
        q_ref[0, :, j * LANES:(j + 1) * LANES] = _rope_chunk(p[:, lo:lo + LANES], cq, suq, sdq, half).astype(BF16)
        lo = base + nq + j * LANES
        k_ref[0, :, j * LANES:(j + 1) * LANES] = _rope_chunk(p[:, lo:lo + LANES], ck, suk, sdk, half).astype(BF16)
    v_ref[0] = p[:, base + 2 * nq:].astype(BF16)


def _hyb_front(x, g, w_in, conv_w, tabs_q, tabs_k, half):
    b, s, d = x.shape
    tm = ROW_TILE
    n = w_in.shape[1]
    nq = DIFF_HEADS * 2 * DIFF_DK
    nv = DIFF_HEADS * DIFF_DV
    tab_spec = pl.BlockSpec((tm, LANES), lambda bi, si: (si, 0))
    row = lambda width: pl.BlockSpec((1, tm, width), lambda bi, si: (bi, si, 0))
    return pl.pallas_call(
        functools.partial(_hyb_front_kernel, tm=tm, half=half),
        grid=(b, s // tm),
        in_specs=[row(d),
                  pl.BlockSpec((1, d), lambda bi, si: (0, 0)),
                  pl.BlockSpec((d, n), lambda bi, si: (0, 0)),
                  pl.BlockSpec((CONV_K, CONV_CH), lambda bi, si: (0, 0)),
                  tab_spec, tab_spec, tab_spec, tab_spec, tab_spec, tab_spec],
        out_specs=[row(CONV_CH), row(nq), row(nq), row(nv)],
        out_shape=[jax.ShapeDtypeStruct((b, s, CONV_CH), BF16),
                   jax.ShapeDtypeStruct((b, s, nq), BF16),
                   jax.ShapeDtypeStruct((b, s, nq), BF16),
                   jax.ShapeDtypeStruct((b, s, nv), BF16)],
        scratch_shapes=[pltpu.VMEM((tm + 8, CONV_CH), F32)],
        compiler_params=_cparams(("arbitrary", "arbitrary")),
        name="hyb_front",
    )(x, g.reshape(1, d), w_in, conv_w, *tabs_q, *tabs_k)


def _diff_attn_kernel(q_ref, k_ref, v_ref, lam_ref, g_ref, o_ref, acc0, acc1, *, tq, lambda_init):
    i = pl.program_id(2)
    q = q_ref[0]
    lane = lax.broadcasted_iota(I32, (1, LANES), 1)
    zero = jnp.zeros_like(q)
    q0 = jnp.where(lane < DIFF_DK, q, zero)
    q1 = jnp.where(lane >= DIFF_DK, q, zero)
    acc0[...] = jnp.zeros_like(acc0)
    acc1[...] = jnp.zeros_like(acc1)

    def step(j, carry, masked):
        m0, l0, m1, l1 = carry
        kb = k_ref[0, pl.ds(pl.multiple_of(j * tq, tq), tq), :]
        vb = v_ref[0, pl.ds(pl.multiple_of(j * tq, tq), tq), :]
        out = []
        for qm, m, l, acc in ((q0, m0, l0, acc0), (q1, m1, l1, acc1)):
            sc = _dot_nt(qm, kb)
            if masked:
                r = lax.broadcasted_iota(I32, (tq, tq), 0)
                cc = lax.broadcasted_iota(I32, (tq, tq), 1)
                sc = jnp.where(cc <= r, sc, NEG_INF)
            m_new = jnp.maximum(m, jnp.max(sc, axis=-1, keepdims=True))
            alpha = jnp.exp(m - m_new)
            pr = jnp.exp(sc - m_new)
            l_new = alpha * l + jnp.sum(pr, axis=-1, keepdims=True)
            acc[...] = alpha * acc[...] + _dot(pr.astype(BF16), vb)
            out += [m_new, l_new]
        return tuple(out)

    init = (jnp.full((tq, 1), NEG_INF, F32), jnp.zeros((tq, 1), F32),
            jnp.full((tq, 1), NEG_INF, F32), jnp.zeros((tq, 1), F32))
    carry = lax.fori_loop(0, i, lambda j, c: step(j, c, False), init)
    m0, l0, m1, l1 = step(i, carry, True)

    lf = lam_ref[...]
    lam = (jnp.exp(jnp.sum(lf[0:1] * lf[1:2], keepdims=True))
           - jnp.exp(jnp.sum(lf[2:3] * lf[3:4], keepdims=True)) + lambda_init)
    o = acc0[...] / l0 - lam * (acc1[...] / l1)
    o_ref[0] = (_rms(o, g_ref[...]) * (1.0 - lambda_init)).astype(BF16)


def _diff_attn(q, k, v, lam_vecs, subln_g, lambda_init):
    b, s, _ = q.shape
    tq = 256
    blk = lambda bi, hi, qi: (bi, qi, hi)
    full = lambda bi, hi, qi: (bi, 0, hi)
    return pl.pallas_call(
        functools.partial(_diff_attn_kernel, tq=tq, lambda_init=lambda_init),
        grid=(b, DIFF_HEADS, s // tq),
        in_specs=[pl.BlockSpec((1, tq, LANES), blk),
                  pl.BlockSpec((1, s, LANES), full),
                  pl.BlockSpec((1, s, LANES), full),
                  pl.BlockSpec((4, DIFF_DK), lambda bi, hi, qi: (0, 0)),
                  pl.BlockSpec((1, DIFF_DV), lambda bi, hi, qi: (0, 0))],
        out_specs=pl.BlockSpec((1, tq, LANES), blk),
        out_shape=jax.ShapeDtypeStruct((b, s, DIFF_HEADS * DIFF_DV), BF16),
        scratch_shapes=[pltpu.VMEM((tq, DIFF_DV), F32), pltpu.VMEM((tq, DIFF_DV), F32)],
        compiler_params=_cparams(("arbitrary", "arbitrary", "arbitrary")),
        name="diff_attn",
    )(q, k, v, lam_vecs, subln_g.reshape(1, DIFF_DV))


def _swa_front_kernel(x_ref, g_ref, w_ref, b_ref, cq_ref, suq_ref, sdq_ref, ck_ref, suk_ref, sdk_ref,
                      q_ref, kv_ref, *, half):
    h = _rms(x_ref[0], g_ref[...]).astype(BF16)
    p = _dot(h, w_ref[...]) + b_ref[...]
    nq = SWA_Q_HEADS * SWA_HEAD_DIM
    nkv = SWA_KV_HEADS * SWA_HEAD_DIM
    cq, suq, sdq = cq_ref[...], suq_ref[...], sdq_ref[...]
    ck, suk, sdk = ck_ref[...], suk_ref[...], sdk_ref[...]
    for j in range(nq // LANES):
        lo = j * LANES
        q_ref[0, :, lo:lo + LANES] = _rope_chunk(p[:, lo:lo + LANES], cq, suq, sdq, half).astype(BF16)
    for j in range(nkv // LANES):
        lo = j * LANES
        kv_ref[0, :, lo:lo + LANES] = _rope_chunk(p[:, nq + lo:nq + lo + LANES], ck, suk, sdk, half).astype(BF16)
    kv_ref[0, :, nkv:] = p[:, nq + nkv:].astype(BF16)


def _swa_front(x, g, w_qkv, b_qkv, tabs_q, tabs_k, half):
    b, s, d = x.shape
    tm = ROW_TILE
    n = w_qkv.shape[1]
    nq = SWA_Q_HEADS * SWA_HEAD_DIM
    nkv = SWA_KV_HEADS * SWA_HEAD_DIM
    tab_spec = pl.BlockSpec((tm, LANES), lambda bi, si: (si, 0))
    row = lambda width: pl.BlockSpec((1, tm, width), lambda bi, si: (bi, si, 0))
    return pl.pallas_call(
        functools.partial(_swa_front_kernel, half=half),
        grid=(b, s // tm),
        in_specs=[row(d),
                  pl.BlockSpec((1, d), lambda bi, si: (0, 0)),
                  pl.BlockSpec((d, n), lambda bi, si: (0, 0)),
                  pl.BlockSpec((1, n), lambda bi, si: (0, 0)),
                  tab_spec, tab_spec, tab_spec, tab_spec, tab_spec, tab_spec],
        out_specs=[row(nq), row(2 * nkv)],
        out_shape=[jax.ShapeDtypeStruct((b, s, nq), BF16),
                   jax.ShapeDtypeStruct((b, s, 2 * nkv), BF16)],
        compiler_params=_cparams(("arbitrary", "arbitrary")),
        name="swa_front",
    )(x, g.reshape(1, d), w_qkv, b_qkv.reshape(1, n), *tabs_q, *tabs_k)


def _swa_attn_kernel(sink_ref, q_ref, kvp_ref, kvc_ref, o_ref):
    n = pl.program_id(1)
    hd = SWA_HEAD_DIM
    g_sz = SWA_Q_HEADS // SWA_KV_HEADS
    nkv = SWA_KV_HEADS * hd
    q = q_ref[0]
    kv = jnp.concatenate([kvp_ref[0], kvc_ref[0]], axis=0)
    r = lax.broadcasted_iota(I32, (BLOCK, 2 * BLOCK), 0)
    c = lax.broadcasted_iota(I32, (BLOCK, 2 * BLOCK), 1)
    rel = c - BLOCK - r
    mask = (rel <= 0) & (rel > -SWA_WINDOW) & ((c >= BLOCK) | (n > 0))
    outs = []
    for kh in range(SWA_KV_HEADS):
        kg = kv[:, kh * hd:(kh + 1) * hd]
        vg = kv[:, nkv + kh * hd:nkv + (kh + 1) * hd]
        for j in range(g_sz):
            h = kh * g_sz + j
            sc = _dot_nt(q[:, h * hd:(h + 1) * hd], kg)
            sc = jnp.where(mask, sc, NEG_INF)
            sink = sink_ref[h]
            m = jnp.maximum(jnp.max(sc, axis=-1, keepdims=True), sink)
            pr = jnp.exp(sc - m)
            den = jnp.sum(pr, axis=-1, keepdims=True) + jnp.exp(sink - m)
            outs.append(_dot(pr.astype(BF16), vg) / den)
    o_ref[0] = jnp.concatenate(outs, axis=-1).astype(BF16)


def _swa_attn(q, kv, sinks):
    b, s, nq = q.shape
    nb = s // BLOCK
    return pl.pallas_call(
        _swa_attn_kernel,
        grid_spec=pltpu.PrefetchScalarGridSpec(
            num_scalar_prefetch=1,
            grid=(b, nb),
            in_specs=[pl.BlockSpec((1, BLOCK, nq), lambda bi, ni, sk: (bi, ni, 0)),
                      pl.BlockSpec((1, BLOCK, kv.shape[2]), lambda bi, ni, sk: (bi, jnp.maximum(ni - 1, 0), 0)),
                      pl.BlockSpec((1, BLOCK, kv.shape[2]), lambda bi, ni, sk: (bi, ni, 0))],
            out_specs=pl.BlockSpec((1, BLOCK, nq), lambda bi, ni, sk: (bi, ni, 0))),
        out_shape=jax.ShapeDtypeStruct((b, s, nq), BF16),
        compiler_params=_cparams(("arbitrary", "arbitrary")),
        name="swa_attn",
    )(sinks, q, kv, kv)


def _post_kernel(*refs, tm, n_a, has_bias):
    x_ref = refs[0]
    a_refs = refs[1:1 + n_a]
    k = 1 + n_a
    wo_refs = refs[k:k + n_a]
    k += n_a
    if has_bias:
        bo_ref = refs[k]
        k += 1
    (gx_ref, wq_ref, mkv_ref, wxo_ref, gf_ref, wr_ref, br_ref,
     x2_ref, h2_ref, mi_ref, mw_ref, cnt_ref, cnt_acc) = refs[k:]
    first = (pl.program_id(0) == 0) & (pl.program_id(1) == 0)

    @pl.when(first)
    def _():
        cnt_acc[...] = jnp.zeros_like(cnt_acc)

    acc = _dot(a_refs[0][0], wo_refs[0][...])
    for a_ref, w_ref in zip(a_refs[1:], wo_refs[1:]):
        acc = acc + _dot(a_ref[0], w_ref[...])
    if has_bias:
        acc = acc + bo_ref[...]
    x1 = x_ref[0] + acc

    hx = _rms(x1, gx_ref[...]).astype(BF16)
    qx = (_dot(hx, wq_ref[...]) * (XATTN_HEAD_DIM ** -0.5)).astype(BF16)
    xw = XATTN_HEADS * XATTN_HEAD_DIM
    outs = []
    for hh in range(XATTN_HEADS):
        lo = hh * XATTN_HEAD_DIM
        mk = mkv_ref[0, :, lo:lo + XATTN_HEAD_DIM]
        mv = mkv_ref[0, :, xw + lo:xw + lo + XATTN_HEAD_DIM]
        sc = _dot_nt(qx[:, lo:lo + XATTN_HEAD_DIM], mk)
        m = jnp.max(sc, axis=-1, keepdims=True)
        pr = jnp.exp(sc - m)
        den = jnp.sum(pr, axis=-1, keepdims=True)
        outs.append((_dot(pr.astype(BF16), mv) / den).astype(BF16))
    ox = jnp.concatenate(outs, axis=-1)
    x2 = x1 + _dot(ox, wxo_ref[...])
    x2_ref[0] = x2

    h2 = _rms(x2, gf_ref[...])
    h2_ref[0] = h2
    logits = jnp.dot(h2, wr_ref[...], precision=lax.Precision.HIGHEST,
                     preferred_element_type=F32) + br_ref[...]
    lane = lax.broadcasted_iota(I32, (tm, LANES), 1)
    big = jnp.int32(LANES)
    g_lo = N_EXPERTS
    lg = jnp.where((lane >= g_lo) & (lane < g_lo + N_GROUPS), logits, NEG_INF)
    mg = jnp.max(lg, axis=-1, keepdims=True)
    g_lane = jnp.min(jnp.where(lg == mg, lane, big), axis=-1, keepdims=True)
    p_g = 1.0 / jnp.sum(jnp.exp(lg - mg), axis=-1, keepdims=True)
    e_lo = (g_lane - g_lo) * EXPERTS_PER_GROUP
    le = jnp.where((lane >= e_lo) & (lane < e_lo + EXPERTS_PER_GROUP), logits, NEG_INF)
    m1 = jnp.max(le, axis=-1, keepdims=True)
    i1 = jnp.min(jnp.where(le == m1, lane, big), axis=-1, keepdims=True)
    le2 = jnp.where(lane == i1, NEG_INF, le)
    m2 = jnp.max(le2, axis=-1, keepdims=True)
    i2 = jnp.min(jnp.where(le2 == m2, lane, big), axis=-1, keepdims=True)
    t = jnp.exp(m2 - m1)
    w1 = p_g / (1.0 + t)
    w2 = p_g * t / (1.0 + t)

    oh1 = lane == i1
    oh2 = lane == i2
    oh = jnp.where(oh1 | oh2, 1.0, 0.0)
    rr = lax.broadcasted_iota(I32, (tm, tm), 0)
    cc = lax.broadcasted_iota(I32, (tm, tm), 1)
    tri = jnp.where(cc < rr, 1.0, 0.0).astype(BF16)
    before = _dot(tri, oh.astype(BF16)) + cnt_acc[...]
    r1 = jnp.sum(jnp.where(oh1, before, 0.0), axis=-1, keepdims=True)
    r2 = jnp.sum(jnp.where(oh2, before, 0.0), axis=-1, keepdims=True)
    cnt_new = cnt_acc[...] + jnp.sum(oh, axis=0, keepdims=True)
    cnt_acc[...] = cnt_new
    cnt_ref[...] = cnt_new

    mw_ref[...] = jnp.where(lane == 0, w1, jnp.where(lane == 1, w2, 0.0))
    ints = jnp.where(lane == 0, i1.astype(F32),
                     jnp.where(lane == 1, i2.astype(F32),
                               jnp.where(lane == 2, r1, jnp.where(lane == 3, r2, 0.0))))
    mi_ref[...] = ints.T[0:8, :].astype(I32)


def _post_mixer(x, a_list, wo_list, b_out, g_x, w_q, mkv, w_xo, g_f, w_rt, b_rt):
    b, s, d = x.shape
    tm = ROW_TILE
    ns = s // tm
    t = b * s
    n_a = len(a_list)
    has_bias = b_out is not None
    const2 = lambda bi, si: (0, 0)
    row = lambda width: pl.BlockSpec((1, tm, width), lambda bi, si: (bi, si, 0))
    in_specs = [row(d)] + [row(a.shape[2]) for a in a_list]
    in_specs += [pl.BlockSpec(w.shape, const2) for w in wo_list]
    args = [x, *a_list, *wo_list]
    if has_bias:
        in_specs.append(pl.BlockSpec((1, d), const2))
        args.append(b_out.reshape(1, d))
    xw = w_q.shape[1]
    in_specs += [pl.BlockSpec((1, d), const2),
                 pl.BlockSpec((d, xw), const2),
                 pl.BlockSpec((1, mkv.shape[1], mkv.shape[2]), lambda bi, si: (bi, 0, 0)),
                 pl.BlockSpec((xw, d), const2),
                 pl.BlockSpec((1, d), const2),
                 pl.BlockSpec((d, LANES), const2),
                 pl.BlockSpec((1, LANES), const2)]
    args += [g_x.reshape(1, d), w_q, mkv, w_xo, g_f.reshape(1, d), w_rt, b_rt]
    out_specs = [row(d), row(d),
                 pl.BlockSpec((8, tm), lambda bi, si: (0, bi * ns + si)),
                 pl.BlockSpec((tm, LANES), lambda bi, si: (bi * ns + si, 0)),
                 pl.BlockSpec((1, LANES), const2)]
    out_shape = [jax.ShapeDtypeStruct((b, s, d), F32),
                 jax.ShapeDtypeStruct((b, s, d), F32),
                 jax.ShapeDtypeStruct((8, t), I32),
                 jax.ShapeDtypeStruct((t, LANES), F32),
                 jax.ShapeDtypeStruct((1, LANES), F32)]
    return pl.pallas_call(
        functools.partial(_post_kernel, tm=tm, n_a=n_a, has_bias=has_bias),
        grid=(b, ns),
        in_specs=in_specs,
        out_specs=out_specs,
        out_shape=out_shape,
        scratch_shapes=[pltpu.VMEM((1, LANES), F32)],
        compiler_params=_cparams(("arbitrary", "arbitrary")),
        name="post_mixer",
    )(*args)


def _row_copy(src, dst, sem):
    return pltpu.make_async_copy(src, dst, sem)


def _dispatch_kernel(off_ref, mi_ref, h_ref, xs_ref, sem, *, tm):
    def issue(r, _):
        for kk in range(TOP_K):
            pos = off_ref[mi_ref[kk, r]] + mi_ref[TOP_K + kk, r]
            _row_copy(h_ref.at[pl.ds(r, 1)], xs_ref.at[pl.ds(pos, 1)], sem).start()
        return 0

    lax.fori_loop(0, tm, issue, 0)

    def drain(r, _):
        for kk in range(TOP_K):
            _row_copy(h_ref.at[pl.ds(0, 1)], xs_ref.at[pl.ds(0, 1)], sem).wait()
        return 0

    lax.fori_loop(0, tm, drain, 0)


def _dispatch(off, meta_i, h2d, n_rows):
    t, d = h2d.shape
    tm = ROW_TILE
    return pl.pallas_call(
        functools.partial(_dispatch_kernel, tm=tm),
        grid_spec=pltpu.PrefetchScalarGridSpec(
            num_scalar_prefetch=1,
            grid=(t // tm,),
            in_specs=[pl.BlockSpec((8, tm), lambda i, off: (0, i), memory_space=pltpu.SMEM),
                      pl.BlockSpec((tm, d), lambda i, off: (i, 0))],
            out_specs=pl.BlockSpec(memory_space=pl.ANY),
            scratch_shapes=[pltpu.SemaphoreType.DMA]),
        out_shape=jax.ShapeDtypeStruct((n_rows, d), F32),
        compiler_params=_cparams(("arbitrary",)),
        name="moe_dispatch",
    )(off, meta_i, h2d)


def _expert_kernel(tblk_ref, texp_ref, tn_ref, xs_ref, wgu_ref, wdn_ref, ys_ref, *, tm):
    i = pl.program_id(0)
    n_valid = tn_ref[i]

    @pl.when(n_valid > 0)
    def _():
        row = lax.broadcasted_iota(I32, (tm, 1), 0)
        xs = jnp.where(row < n_valid, xs_ref[...], 0.0).astype(BF16)
        gu = _dot(xs, wgu_ref[0].astype(BF16))
        gate, up = gu[:, :EXPERT_HIDDEN], gu[:, EXPERT_HIDDEN:]
        hid = (gate / (1.0 + jnp.exp(-gate)) * up).astype(BF16)
        ys_ref[...] = _dot(hid, wdn_ref[0].astype(BF16))


def _experts(tile_blk, tile_exp, tile_n, xs, w_gu, w_dn):
    n_rows, d = xs.shape
    tm = EXPERT_TILE
    nt = n_rows // tm
    return pl.pallas_call(
        functools.partial(_expert_kernel, tm=tm),
        grid_spec=pltpu.PrefetchScalarGridSpec(
            num_scalar_prefetch=3,
            grid=(nt,),
            in_specs=[pl.BlockSpec((tm, d), lambda i, tb, te, tn: (tb[i], 0)),
                      pl.BlockSpec((1,) + w_gu.shape[1:], lambda i, tb, te, tn: (te[i], 0, 0)),
                      pl.BlockSpec((1,) + w_dn.shape[1:], lambda i, tb, te, tn: (te[i], 0, 0))],
            out_specs=pl.BlockSpec((tm, d), lambda i, tb, te, tn: (tb[i], 0))),
        out_shape=jax.ShapeDtypeStruct((n_rows, d), F32),
        compiler_params=_cparams(("arbitrary",)),
        name="moe_experts",
    )(tile_blk, tile_exp, tile_n, xs, w_gu, w_dn)


def _combine_kernel(off_ref, mi_ref, mw_ref, x_ref, ys_ref, *rest, tm, final):
    if final:
        g_ref, o_ref, ybuf, sem = rest
    else:
        o_ref, ybuf, sem = rest

    def issue(r, _):
        for kk in range(TOP_K):
            pos = off_ref[mi_ref[kk, r]] + mi_ref[TOP_K + kk, r]
            _row_copy(ys_ref.at[pl.ds(pos, 1)], ybuf.at[kk, pl.ds(r, 1)], sem).start()
        return 0

    lax.fori_loop(0, tm, issue, 0)

    def drain(r, _):
        for kk in range(TOP_K):
            _row_copy(ys_ref.at[pl.ds(0, 1)], ybuf.at[kk, pl.ds(0, 1)], sem).wait()
        return 0

    lax.fori_loop(0, tm, drain, 0)
    mw = mw_ref[...]
    x3 = x_ref[...] + mw[:, 0:1] * ybuf[0] + mw[:, 1:2] * ybuf[1]
    if final:
        x3 = _rms(x3, g_ref[...])
    o_ref[...] = x3


def _combine(off, meta_i, meta_w, x2d, ys, g_final):
    t, d = x2d.shape
    tm = ROW_TILE
    final = g_final is not None
    in_specs = [pl.BlockSpec((8, tm), lambda i, off: (0, i), memory_space=pltpu.SMEM),
                pl.BlockSpec((tm, LANES), lambda i, off: (i, 0)),
                pl.BlockSpec((tm, d), lambda i, off: (i, 0)),
                pl.BlockSpec(memory_space=pl.ANY)]
    args = [off, meta_i, meta_w, x2d, ys]
    if final:
        in_specs.append(pl.BlockSpec((1, d), lambda i, off: (0, 0)))
        args.append(g_final.reshape(1, d))
    return pl.pallas_call(
        functools.partial(_combine_kernel, tm=tm, final=final),
        grid_spec=pltpu.PrefetchScalarGridSpec(
            num_scalar_prefetch=1,
            grid=(t // tm,),
            in_specs=in_specs,
            out_specs=pl.BlockSpec((tm, d), lambda i, off: (i, 0)),
            scratch_shapes=[pltpu.VMEM((TOP_K, tm, d), F32), pltpu.SemaphoreType.DMA]),
        out_shape=jax.ShapeDtypeStruct((t, d), F32),
        compiler_params=_cparams(("arbitrary",)),
        name="moe_combine",
    )(*args)


def _moe_plan(counts, n_tiles):
    tm = EXPERT_TILE
    cnt = counts[0, :N_EXPERTS].astype(I32)
    padded = ((cnt + tm - 1) // tm) * tm
    ends = jnp.cumsum(padded)
    off = ends - padded
    start = jnp.arange(n_tiles, dtype=I32) * tm
    exp = jnp.minimum(jnp.sum((start[:, None] >= ends[None, :]).astype(I32), axis=1), N_EXPERTS - 1)
    n_valid = jnp.clip(off[exp] + cnt[exp] - start, 0, tm)
    active = start < ends[-1]
    n_valid = jnp.where(active, n_valid, 0)
    last = jnp.maximum(ends[-1] // tm - 1, 0)
    blk = jnp.where(active, jnp.arange(n_tiles, dtype=I32), last)
    exp = jnp.where(active, exp, exp[last])
    return off.astype(I32), blk.astype(I32), exp.astype(I32), n_valid.astype(I32)


def _moe(x2, h2, meta_i, meta_w, counts, w_gu, w_dn, g_final):
    b, s, d = x2.shape
    t = b * s
    n_rows = t * TOP_K + N_EXPERTS * EXPERT_TILE
    off, blk, exp, n_valid = _moe_plan(counts, n_rows // EXPERT_TILE)
    xs = _dispatch(off, meta_i, h2.reshape(t, d), n_rows)
    ys = _experts(blk, exp, n_valid, xs, w_gu, w_dn)
    out = _combine(off, meta_i, meta_w, x2.reshape(t, d), ys, g_final)
    return out.reshape(b, s, d)


def _router_weights(w_group, b_group, w_router, b_router):
    d = w_group.shape[0]
    pad = LANES - N_EXPERTS - N_GROUPS
    w = jnp.concatenate([w_router, w_group, jnp.zeros((d, pad), F32)], axis=1)
    bias = jnp.concatenate([b_router, b_group, jnp.zeros((pad,), F32)]).reshape(1, LANES)
    return w, bias


def kernel(x, mem, mem_norm, mem_w_kv, norm_mix, norm_xattn, norm_ffn, hyb_w_in, hyb_conv_w, diff_lambda, diff_subln, hyb_w_out, swa_w_qkv, swa_b_qkv, swa_sinks, swa_w_out, swa_b_out, xattn_w_q, xattn_w_o, moe_w_group, moe_b_group, moe_w_router, moe_b_router, moe_w_gate_up, moe_w_down, final_norm):
    b, s, d = x.shape
    m = mem.shape[1]
    depth = norm_mix.shape[0]
    mkv = _norm_proj(mem.reshape(b * m, d), mem_norm, mem_w_kv.astype(BF16), ROW_TILE).reshape(b, m, -1)

    scale = DIFF_DK ** -0.5
    cq, suq, sdq, half = _rope_lane_tables(s, DIFF_DK, scale)
    ck, suk, sdk, _ = _rope_lane_tables(s, DIFF_DK, 1.0)
    tabs_q, tabs_k = (cq, suq, sdq), (ck, suk, sdk)

    for l in range(depth):
        if l % 2 == 0:
            e = l // 2
            lambda_init = 0.8 - 0.6 * math.exp(-0.3 * l)
            ya, q, k, v = _hyb_front(x, norm_mix[l], hyb_w_in[e].astype(BF16), hyb_conv_w[e], tabs_q, tabs_k, half)
            o = _diff_attn(q, k, v, diff_lambda[e], diff_subln[e], lambda_init)
            w_out = hyb_w_out[e].astype(BF16)
            a_list, wo_list, b_out = [ya, o], [w_out[:CONV_CH], w_out[CONV_CH:]], None
        else:
            e = l // 2
            q, kv = _swa_front(x, norm_mix[l], swa_w_qkv[e].astype(BF16), swa_b_qkv[e], tabs_q, tabs_k, half)
            o = _swa_attn(q, kv, swa_sinks[e])
            a_list, wo_list, b_out = [o], [swa_w_out[e].astype(BF16)], swa_b_out[e]
        w_rt, b_rt = _router_weights(moe_w_group[l], moe_b_group[l], moe_w_router[l], moe_b_router[l])
        x2, h2, meta_i, meta_w, counts = _post_mixer(
            x, a_list, wo_list, b_out, norm_xattn[l], xattn_w_q[l].astype(BF16), mkv,
            xattn_w_o[l].astype(BF16), norm_ffn[l], w_rt, b_rt)
        x = _moe(x2, h2, meta_i, meta_w, counts, moe_w_gate_up[l], moe_w_down[l],
                 final_norm if l == depth - 1 else None)
    return x
```

```python
import functools
import math

import jax
import jax.numpy as jnp
from jax import lax
from jax.experimental import pallas as pl
from jax.experimental.pallas import tpu as pltpu

F32 = jnp.float32
BF16 = jnp.bfloat16
I32 = jnp.int32

EPS = 1e-6
LANES = 128
SUBLANES = 8
DMA_UNROLL = 8
VMEM_LIMIT = 48 * 1024 * 1024

ROPE_THETA = 500000.0
ROPE_FRACTION = 4
BLOCK = 128
CONV_CH = 512
CONV_K = 3
DIFF_HEADS = 4
DIFF_DK = 64
DIFF_DV = 128
SWA_Q_HEADS = 16
SWA_KV_HEADS = 4
SWA_HEAD_DIM = 64
SWA_WINDOW = 128
XATTN_HEADS = 4
XATTN_HEAD_DIM = 128
N_GROUPS = 4
EXPERTS_PER_GROUP = 8
N_EXPERTS = N_GROUPS * EXPERTS_PER_GROUP
TOP_K = 2
EXPERT_HIDDEN = 512

ROW_TILE = 256
EXPERT_TILE = 512
NEG_INF = float("-inf")


def _cparams(sem):
    return pltpu.CompilerParams(dimension_semantics=sem, vmem_limit_bytes=VMEM_LIMIT)


def _rms(x, g):
    return x * lax.rsqrt(jnp.mean(x * x, axis=-1, keepdims=True) + EPS) * g


def _dot(a, b):
    return jnp.dot(a, b, preferred_element_type=F32)


def _dot_nt(a, b):
    return lax.dot_general(a, b, (((1,), (1,)), ((), ())), preferred_element_type=F32)


def _rope_lane_tables(seq, head_dim, scale):
    rot = head_dim // ROPE_FRACTION
    half = rot // 2
    pos = jnp.arange(seq, dtype=F32)
    inv = ROPE_THETA ** (-jnp.arange(0, rot, 2, dtype=F32) / rot)
    ang = pos[:, None] * inv[None, :]
    cos, sin = jnp.cos(ang), jnp.sin(ang)
    idx = jnp.arange(LANES) % head_dim
    cl = jnp.take(cos, idx % half, axis=1)
    sl = jnp.take(sin, idx % half, axis=1)
    c = jnp.where(idx < rot, cl, 1.0) * scale
    s_up = jnp.where(idx < half, -sl, 0.0) * scale
    s_dn = jnp.where((idx >= half) & (idx < rot), sl, 0.0) * scale
    return c.astype(F32), s_up.astype(F32), s_dn.astype(F32), half


def _rope_chunk(xc, c, s_up, s_dn, half):
    return (xc * c + pltpu.roll(xc, LANES - half, 1) * s_up + pltpu.roll(xc, half, 1) * s_dn)


def _norm_proj_kernel(x_ref, g_ref, w_ref, o_ref):
    h = _rms(x_ref[...], g_ref[...]).astype(BF16)
    o_ref[...] = _dot(h, w_ref[...]).astype(o_ref.dtype)


def _norm_proj(x2d, g, w_bf16, tm):
    m, d = x2d.shape
    n = w_bf16.shape[1]
    return pl.pallas_call(
        _norm_proj_kernel,
        grid=(m // tm,),
        in_specs=[pl.BlockSpec((tm, d), lambda i: (i, 0)),
                  pl.BlockSpec((1, d), lambda i: (0, 0)),
                  pl.BlockSpec((d, n), lambda i: (0, 0))],
        out_specs=pl.BlockSpec((tm, n), lambda i: (i, 0)),
        out_shape=jax.ShapeDtypeStruct((m, n), BF16),
        compiler_params=_cparams(("arbitrary",)),
        name="mem_kv_proj",
    )(x2d, g.reshape(1, d), w_bf16)


def _hyb_front_kernel(x_ref, g_ref, w_ref, cw_ref, cq_ref, suq_ref, sdq_ref, ck_ref, suk_ref, sdk_ref,
                      ya_ref, q_ref, k_ref, v_ref, cbuf, *, tm, half):
    s = pl.program_id(1)
    h = _rms(x_ref[0], g_ref[...]).astype(BF16)
    p = _dot(h, w_ref[...])
    c = CONV_CH
    gate_b, gate_c, u = p[:, 0:c], p[:, c:2 * c], p[:, 2 * c:3 * c]
    cu = gate_c * u

    @pl.when(s == 0)
    def _():
        cbuf[0:8, :] = jnp.zeros((8, c), F32)

    cbuf[8:8 + tm, :] = cu
    cw = cw_ref[...]
    conv = (cw[0:1, :] * cbuf[6:6 + tm, :] + cw[1:2, :] * cbuf[7:7 + tm, :] + cw[2:3, :] * cu)
    ya_ref[0] = (gate_b * conv).astype(BF16)
    cbuf[0:8, :] = cbuf[tm:tm + 8, :]

    base = 3 * c
    nq = DIFF_HEADS * 2 * DIFF_DK
    cq, suq, sdq = cq_ref[...], suq_ref[...], sdq_ref[...]
    ck, suk, sdk = ck_ref[...], suk_ref[...], sdk_ref[...]
    for j in range(nq // LANES):
        lo = base + j * LANES
        q_ref[0, :, j * LANES:(j + 1) * LANES] = _rope_chunk(p[:, lo:lo + LANES], cq, suq, sdq, half).astype(BF16)
        lo = base + nq + j * LANES
        k_ref[0, :, j * LANES:(j + 1) * LANES] = _rope_chunk(p[:, lo:lo + LANES], ck, suk, sdk, half).astype(BF16)
    v_ref[0] = p[:, base + 2 * nq:].astype(BF16)


def _hyb_front(x, g, w_in, conv_w, tabs_q, tabs_k, half):
    b, s, d = x.shape
    tm = ROW_TILE
    n = w_in.shape[1]
    nq = DIFF_HEADS * 2 * DIFF_DK
    nv = DIFF_HEADS * DIFF_DV
    tab_spec = pl.BlockSpec((tm, LANES), lambda bi, si: (si, 0))
    row = lambda width: pl.BlockSpec((1, tm, width), lambda bi, si: (bi, si, 0))
    return pl.pallas_call(
        functools.partial(_hyb_front_kernel, tm=tm, half=half),
        grid=(b, s // tm),
        in_specs=[row(d),
                  pl.BlockSpec((1, d), lambda bi, si: (0, 0)),
                  pl.BlockSpec((d, n), lambda bi, si: (0, 0)),
                  pl.BlockSpec((CONV_K, CONV_CH), lambda bi, si: (0, 0)),
                  tab_spec, tab_spec, tab_spec, tab_spec, tab_spec, tab_spec],
        out_specs=[row(CONV_CH), row(nq), row(nq), row(nv)],
        out_shape=[jax.ShapeDtypeStruct((b, s, CONV_CH), BF16),
                   jax.ShapeDtypeStruct((b, s, nq), BF16),
                   jax.ShapeDtypeStruct((b, s, nq), BF16),
                   jax.ShapeDtypeStruct((b, s, nv), BF16)],
        scratch_shapes=[pltpu.VMEM((tm + 8, CONV_CH), F32)],
        compiler_params=_cparams(("arbitrary", "arbitrary")),
        name="hyb_front",
    )(x, g.reshape(1, d), w_in, conv_w, *tabs_q, *tabs_k)


def _lane_fold(x, op):
    r = x[:, 0:LANES]
    for c in range(1, x.shape[1] // LANES):
        r = op(r, x[:, c * LANES:(c + 1) * LANES])
    return r


def _diff_attn_kernel(q_ref, k_ref, v_ref, lam_ref, g_ref, o_ref, sbuf, stat, acc, *, tq, hp, lambda_init):
    i = pl.program_id(2)
    lane = lax.broadcasted_iota(I32, (1, LANES), 1)
    qs = []
    for h in range(hp):
        q = q_ref[0, :, h * LANES:(h + 1) * LANES]
        zero = jnp.zeros_like(q)
        qs += [jnp.where(lane < DIFF_DK, q, zero), jnp.where(lane >= DIFF_DK, q, zero)]
    nc = 2 * hp

    def scores(j, masked):
        for c in range(nc):
            h = c // 2
            kb = k_ref[0, pl.ds(pl.multiple_of(j * tq, tq), tq), h * LANES:(h + 1) * LANES]
            sc = _dot_nt(qs[c], kb)
            if masked:
                r = lax.broadcasted_iota(I32, (tq, tq), 0)
                cc = lax.broadcasted_iota(I32, (tq, tq), 1)
                sc = jnp.where(cc <= r, sc, NEG_INF)
            sbuf[c, j] = sc
            stat[c] = jnp.maximum(stat[c], _lane_fold(sc, jnp.maximum))

    stat[...] = jnp.full(stat.shape, NEG_INF, F32)

    def pass1(j, carry):
        scores(j, False)
        return carry

    lax.fori_loop(0, i, pass1, 0)
    scores(i, True)
    ms = [jnp.max(stat[c], axis=-1, keepdims=True) for c in range(nc)]

    acc[...] = jnp.zeros_like(acc)
    stat[...] = jnp.zeros(stat.shape, F32)

    def pass2(j, carry):
        for h in range(hp):
            vb = v_ref[0, pl.ds(pl.multiple_of(j * tq, tq), tq), h * LANES:(h + 1) * LANES]
            p0 = jnp.exp(sbuf[2 * h, j] - ms[2 * h])
            p1 = jnp.exp(sbuf[2 * h + 1, j] - ms[2 * h + 1])
            acc[h] += _dot(jnp.concatenate([p0, p1], axis=0).astype(BF16), vb)
            stat[2 * h] += _lane_fold(p0, jnp.add)
            stat[2 * h + 1] += _lane_fold(p1, jnp.add)
        return carry

    lax.fori_loop(0, i + 1, pass2, 0)

    lf = lam_ref[...]
    lam = (jnp.exp(jnp.sum(lf[0:1] * lf[1:2], keepdims=True))
           - jnp.exp(jnp.sum(lf[2:3] * lf[3:4], keepdims=True)) + lambda_init)
    for h in range(hp):
        l0 = jnp.sum(stat[2 * h], axis=-1, keepdims=True)
        l1 = jnp.sum(stat[2 * h + 1], axis=-1, keepdims=True)
        o = acc[h, 0:tq, :] / l0 - lam * (acc[h, tq:2 * tq, :] / l1)
        o_ref[0, :, h * LANES:(h + 1) * LANES] = (_rms(o, g_ref[...]) * (1.0 - lambda_init)).astype(BF16)


def _diff_attn(q, k, v, lam_vecs, subln_g, lambda_init):
    b, s, _ = q.shape
    tq = 256
    hp = 4
    blk = lambda bi, hi, qi: (bi, qi, hi)
    full = lambda bi, hi, qi: (bi, 0, hi)
    return pl.pallas_call(
        functools.partial(_diff_attn_kernel, tq=tq, hp=hp, lambda_init=lambda_init),
        grid=(b, DIFF_HEADS // hp, s // tq),
        in_specs=[pl.BlockSpec((1, tq, hp * LANES), blk),
                  pl.BlockSpec((1, s, hp * LANES), full),
                  pl.BlockSpec((1, s, hp * LANES), full),
                  pl.BlockSpec((4, DIFF_DK), lambda bi, hi, qi: (0, 0)),
                  pl.BlockSpec((1, DIFF_DV), lambda bi, hi, qi: (0, 0))],
        out_specs=pl.BlockSpec((1, tq, hp * LANES), blk),
        out_shape=jax.ShapeDtypeStruct((b, s, DIFF_HEADS * DIFF_DV), BF16),
        scratch_shapes=[pltpu.VMEM((2 * hp, s // tq, tq, tq), F32),
                        pltpu.VMEM((2 * hp, tq, LANES), F32),
                        pltpu.VMEM((hp, 2 * tq, DIFF_DV), F32)],
        compiler_params=_cparams(("arbitrary", "arbitrary", "arbitrary")),
        name="diff_attn",
    )(q, k, v, lam_vecs, subln_g.reshape(1, DIFF_DV))


def _swa_front_kernel(x_ref, g_ref, w_ref, b_ref, cq_ref, suq_ref, sdq_ref, ck_ref, suk_ref, sdk_ref,
                      q_ref, kv_ref, *, half):
    h = _rms(x_ref[0], g_ref[...]).astype(BF16)
    p = _dot(h, w_ref[...]) + b_ref[...]
    nq = SWA_Q_HEADS * SWA_HEAD_DIM
    nkv = SWA_KV_HEADS * SWA_HEAD_DIM
    cq, suq, sdq = cq_ref[...], suq_ref[...], sdq_ref[...]
    ck, suk, sdk = ck_ref[...], suk_ref[...], sdk_ref[...]
    for j in range(nq // LANES):
        lo = j * LANES
        q_ref[0, :, lo:lo + LANES] = _rope_chunk(p[:, lo:lo + LANES], cq, suq, sdq, half).astype(BF16)
    for j in range(nkv // LANES):
        lo = j * LANES
        kv_ref[0, :, lo:lo + LANES] = _rope_chunk(p[:, nq + lo:nq + lo + LANES], ck, suk, sdk, half).astype(BF16)
    kv_ref[0, :, nkv:] = p[:, nq + nkv:].astype(BF16)


def _swa_front(x, g, w_qkv, b_qkv, tabs_q, tabs_k, half):
    b, s, d = x.shape
    tm = ROW_TILE
    n = w_qkv.shape[1]
    nq = SWA_Q_HEADS * SWA_HEAD_DIM
    nkv = SWA_KV_HEADS * SWA_HEAD_DIM
    tab_spec = pl.BlockSpec((tm, LANES), lambda bi, si: (si, 0))
    row = lambda width: pl.BlockSpec((1, tm, width), lambda bi, si: (bi, si, 0))
    return pl.pallas_call(
        functools.partial(_swa_front_kernel, half=half),
        grid=(b, s // tm),
        in_specs=[row(d),
                  pl.BlockSpec((1, d), lambda bi, si: (0, 0)),
                  pl.BlockSpec((d, n), lambda bi, si: (0, 0)),
                  pl.BlockSpec((1, n), lambda bi, si: (0, 0)),
                  tab_spec, tab_spec, tab_spec, tab_spec, tab_spec, tab_spec],
        out_specs=[row(nq), row(2 * nkv)],
        out_shape=[jax.ShapeDtypeStruct((b, s, nq), BF16),
                   jax.ShapeDtypeStruct((b, s, 2 * nkv), BF16)],
        compiler_params=_cparams(("arbitrary", "arbitrary")),
        name="swa_front",
    )(x, g.reshape(1, d), w_qkv, b_qkv.reshape(1, n), *tabs_q, *tabs_k)


def _swa_attn_kernel(sink_ref, q_ref, kvp_ref, kvc_ref, o_ref):
    n = pl.program_id(1)
    hd = SWA_HEAD_DIM
    g_sz = SWA_Q_HEADS // SWA_KV_HEADS
    nkv = SWA_KV_HEADS * hd
    q = q_ref[0]
    kv = jnp.concatenate([kvp_ref[0], kvc_ref[0]], axis=0)
    r = lax.broadcasted_iota(I32, (BLOCK, 2 * BLOCK), 0)
    c = lax.broadcasted_iota(I32, (BLOCK, 2 * BLOCK), 1)
    rel = c - BLOCK - r
    mask = (rel <= 0) & (rel > -SWA_WINDOW) & ((c >= BLOCK) | (n > 0))
    outs = []
    for kh in range(SWA_KV_HEADS):
        kg = kv[:, kh * hd:(kh + 1) * hd]
        vg = kv[:, nkv + kh * hd:nkv + (kh + 1) * hd]
        for j in range(g_sz):
            h = kh * g_sz + j
            sc = _dot_nt(q[:, h * hd:(h + 1) * hd], kg)
            sc = jnp.where(mask, sc, NEG_INF)
            sink = sink_ref[h]
            m = jnp.maximum(jnp.max(sc, axis=-1, keepdims=True), sink)
            pr = jnp.exp(sc - m)
            den = jnp.sum(pr, axis=-1, keepdims=True) + jnp.exp(sink - m)
            outs.append(_dot(pr.astype(BF16), vg) / den)
    o_ref[0] = jnp.concatenate(outs, axis=-1).astype(BF16)


def _swa_attn(q, kv, sinks):
    b, s, nq = q.shape
    nb = s // BLOCK
    return pl.pallas_call(
        _swa_attn_kernel,
        grid_spec=pltpu.PrefetchScalarGridSpec(
            num_scalar_prefetch=1,
            grid=(b, nb),
            in_specs=[pl.BlockSpec((1, BLOCK, nq), lambda bi, ni, sk: (bi, ni, 0)),
                      pl.BlockSpec((1, BLOCK, kv.shape[2]), lambda bi, ni, sk: (bi, jnp.maximum(ni - 1, 0), 0)),
                      pl.BlockSpec((1, BLOCK, kv.shape[2]), lambda bi, ni, sk: (bi, ni, 0))],
            out_specs=pl.BlockSpec((1, BLOCK, nq), lambda bi, ni, sk: (bi, ni, 0))),
        out_shape=jax.ShapeDtypeStruct((b, s, nq), BF16),
        compiler_params=_cparams(("arbitrary", "arbitrary")),
        name="swa_attn",
    )(sinks, q, kv, kv)


def _post_kernel(*refs, tm, n_a, has_bias):
    x_ref = refs[0]
    a_refs = refs[1:1 + n_a]
    k = 1 + n_a
    wo_refs = refs[k:k + n_a]
    k += n_a
    if has_bias:
        bo_ref = refs[k]
        k += 1
    (gx_ref, wq_ref, mkv_ref, wxo_ref, gf_ref, wr_ref, br_ref,
     x2_ref, h2_ref, mi_ref, mw_ref, cnt_ref, cnt_acc) = refs[k:]
    first = (pl.program_id(0) == 0) & (pl.program_id(1) == 0)

    @pl.when(first)
    def _():
        cnt_acc[...] = jnp.zeros_like(cnt_acc)

    acc = _dot(a_refs[0][0], wo_refs[0][...])
    for a_ref, w_ref in zip(a_refs[1:], wo_refs[1:]):
        acc = acc + _dot(a_ref[0], w_ref[...])
    if has_bias:
        acc = acc + bo_ref[...]
    x1 = x_ref[0] + acc

    hx = _rms(x1, gx_ref[...]).astype(BF16)
    qx = (_dot(hx, wq_ref[...]) * (XATTN_HEAD_DIM ** -0.5)).astype(BF16)
    xw = XATTN_HEADS * XATTN_HEAD_DIM
    outs = []
    for hh in range(XATTN_HEADS):
        lo = hh * XATTN_HEAD_DIM
        mk = mkv_ref[0, :, lo:lo + XATTN_HEAD_DIM]
        mv = mkv_ref[0, :, xw + lo:xw + lo + XATTN_HEAD_DIM]
        sc = _dot_nt(qx[:, lo:lo + XATTN_HEAD_DIM], mk)
        m = jnp.max(sc, axis=-1, keepdims=True)
        pr = jnp.exp(sc - m)
        den = jnp.sum(pr, axis=-1, keepdims=True)
        outs.append((_dot(pr.astype(BF16), mv) / den).astype(BF16))
    ox = jnp.concatenate(outs, axis=-1)
    x2 = x1 + _dot(ox, wxo_ref[...])
    x2_ref[0] = x2

    h2 = _rms(x2, gf_ref[...])
    _to_token_tiles(h2_ref, h2, tm)
    h_hi = h2.astype(BF16)
    h_lo = (h2 - h_hi.astype(F32)).astype(BF16)
    logits = (_dot(h_hi, wr_ref[0]) + _dot(h_lo, wr_ref[0]) + _dot(h_hi, wr_ref[1])
              + br_ref[...])
    lane = lax.broadcasted_iota(I32, (tm, LANES), 1)
    big = jnp.int32(LANES)
    g_lo = N_EXPERTS
    lg = jnp.where((lane >= g_lo) & (lane < g_lo + N_GROUPS), logits, NEG_INF)
    mg = jnp.max(lg, axis=-1, keepdims=True)
    g_lane = jnp.min(jnp.where(lg == mg, lane, big), axis=-1, keepdims=True)
    p_g = 1.0 / jnp.sum(jnp.exp(lg - mg), axis=-1, keepdims=True)
    e_lo = (g_lane - g_lo) * EXPERTS_PER_GROUP
    le = jnp.where((lane >= e_lo) & (lane < e_lo + EXPERTS_PER_GROUP), logits, NEG_INF)
    m1 = jnp.max(le, axis=-1, keepdims=True)
    i1 = jnp.min(jnp.where(le == m1, lane, big), axis=-1, keepdims=True)
    le2 = jnp.where(lane == i1, NEG_INF, le)
    m2 = jnp.max(le2, axis=-1, keepdims=True)
    i2 = jnp.min(jnp.where(le2 == m2, lane, big), axis=-1, keepdims=True)
    t = jnp.exp(m2 - m1)
    w1 = p_g / (1.0 + t)
    w2 = p_g * t / (1.0 + t)

    oh1 = lane == i1
    oh2 = lane == i2
    oh = jnp.where(oh1 | oh2, 1.0, 0.0)
    rr = lax.broadcasted_iota(I32, (tm, tm), 0)
    cc = lax.broadcasted_iota(I32, (tm, tm), 1)
    tri = jnp.where(cc < rr, 1.0, 0.0).astype(BF16)
    before = _dot(tri, oh.astype(BF16)) + cnt_acc[...]
    r1 = jnp.sum(jnp.where(oh1, before, 0.0), axis=-1, keepdims=True)
    r2 = jnp.sum(jnp.where(oh2, before, 0.0), axis=-1, keepdims=True)
    cnt_new = cnt_acc[...] + jnp.sum(oh, axis=0, keepdims=True)
    cnt_acc[...] = cnt_new
    cnt_ref[...] = cnt_new

    mw_ref[...] = jnp.where(lane == 0, w1, jnp.where(lane == 1, w2, 0.0))
    ints = jnp.where(lane == 0, i1.astype(F32),
                     jnp.where(lane == 1, i2.astype(F32),
                               jnp.where(lane == 2, r1, jnp.where(lane == 3, r2, 0.0))))
    mi_ref[...] = ints.T[0:8, :].astype(I32)


def _post_mixer(x, a_list, wo_list, b_out, g_x, w_q, mkv, w_xo, g_f, w_rt, b_rt):
    b, s, d = x.shape
    tm = ROW_TILE
    ns = s // tm
    t = b * s
    n_a = len(a_list)
    has_bias = b_out is not None
    const2 = lambda bi, si: (0, 0)
    row = lambda width: pl.BlockSpec((1, tm, width), lambda bi, si: (bi, si, 0))
    in_specs = [row(d)] + [row(a.shape[2]) for a in a_list]
    in_specs += [pl.BlockSpec(w.shape, const2) for w in wo_list]
    args = [x, *a_list, *wo_list]
    if has_bias:
        in_specs.append(pl.BlockSpec((1, d), const2))
        args.append(b_out.reshape(1, d))
    xw = w_q.shape[1]
    in_specs += [pl.BlockSpec((1, d), const2),
                 pl.BlockSpec((d, xw), const2),
                 pl.BlockSpec((1, mkv.shape[1], mkv.shape[2]), lambda bi, si: (bi, 0, 0)),
                 pl.BlockSpec((xw, d), const2),
                 pl.BlockSpec((1, d), const2),
                 pl.BlockSpec((2, d, LANES), lambda bi, si: (0, 0, 0)),
                 pl.BlockSpec((1, LANES), const2)]
    args += [g_x.reshape(1, d), w_q, mkv, w_xo, g_f.reshape(1, d), w_rt, b_rt]
    out_specs = [row(d),
                 pl.BlockSpec((tm * SUBLANES, LANES), lambda bi, si: (bi * ns + si, 0)),
                 pl.BlockSpec((8, tm), lambda bi, si: (0, bi * ns + si)),
                 pl.BlockSpec((tm, LANES), lambda bi, si: (bi * ns + si, 0)),
                 pl.BlockSpec((1, LANES), const2)]
    out_shape = [jax.ShapeDtypeStruct((b, s, d), F32),
                 jax.ShapeDtypeStruct((t * SUBLANES, LANES), F32),
                 jax.ShapeDtypeStruct((8, t), I32),
                 jax.ShapeDtypeStruct((t, LANES), F32),
                 jax.ShapeDtypeStruct((1, LANES), F32)]
    return pl.pallas_call(
        functools.partial(_post_kernel, tm=tm, n_a=n_a, has_bias=has_bias),
        grid=(b, ns),
        in_specs=in_specs,
        out_specs=out_specs,
        out_shape=out_shape,
        scratch_shapes=[pltpu.VMEM((1, LANES), F32)],
        compiler_params=_cparams(("arbitrary", "arbitrary")),
        name="post_mixer",
    )(*args)


def _to_token_tiles(dst_ref, val, rows):
    for j in range(SUBLANES):
        dst_ref[pl.ds(j, rows, stride=SUBLANES), :] = val[:, j * LANES:(j + 1) * LANES]


def _from_token_tiles(src_ref, rows):
    return jnp.concatenate([src_ref[pl.ds(j, rows, stride=SUBLANES), :] for j in range(SUBLANES)], axis=1)


def _token_slice(ref, tok):
    return ref.at[pl.ds(pl.multiple_of(tok * SUBLANES, SUBLANES), SUBLANES)]


def _sorted_pos(off_ref, mi_ref, kk, r):
    return off_ref[mi_ref[kk, r]] + mi_ref[TOP_K + kk, r]


def _dispatch_kernel(off_ref, mi_ref, h_ref, xs_ref, sem, *, tm):
    def issue(g, _):
        for u in range(DMA_UNROLL):
            r = g * DMA_UNROLL + u
            for kk in range(TOP_K):
                pos = _sorted_pos(off_ref, mi_ref, kk, r)
                pltpu.make_async_copy(_token_slice(h_ref, r), _token_slice(xs_ref, pos), sem).start(priority=kk)
        return 0

    lax.fori_loop(0, tm // DMA_UNROLL, issue, 0)
    for _ in range(TOP_K):
        pltpu.make_async_copy(h_ref, xs_ref.at[pl.ds(0, tm * SUBLANES)], sem).wait()


def _dispatch(off, meta_i, h2t, n_rows):
    tm = ROW_TILE
    t = h2t.shape[0] // SUBLANES
    return pl.pallas_call(
        functools.partial(_dispatch_kernel, tm=tm),
        grid_spec=pltpu.PrefetchScalarGridSpec(
            num_scalar_prefetch=1,
            grid=(t // tm,),
            in_specs=[pl.BlockSpec((8, tm), lambda i, off: (0, i), memory_space=pltpu.SMEM),
                      pl.BlockSpec((tm * SUBLANES, LANES), lambda i, off: (i, 0))],
            out_specs=pl.BlockSpec(memory_space=pl.ANY),
            scratch_shapes=[pltpu.SemaphoreType.DMA]),
        out_shape=jax.ShapeDtypeStruct((n_rows * SUBLANES, LANES), F32),
        compiler_params=_cparams(("arbitrary",)),
        name="moe_dispatch",
    )(off, meta_i, h2t)


def _expert_kernel(tblk_ref, texp_ref, tn_ref, xs_ref, wgu_ref, wdn_ref, ys_ref, wgu_bf, wdn_bf, *, tm):
    i = pl.program_id(0)
    n_valid = tn_ref[i]
    new_expert = (i == 0) | (texp_ref[i] != texp_ref[jnp.maximum(i - 1, 0)])

    @pl.when(new_expert)
    def _():
        wgu_bf[...] = wgu_ref[0, 0].astype(BF16)
        wdn_bf[...] = wdn_ref[0, 0].astype(BF16)

    @pl.when(n_valid > 0)
    def _():
        row = lax.broadcasted_iota(I32, (tm, 1), 0)
        xs = jnp.where(row < n_valid, _from_token_tiles(xs_ref, tm), 0.0).astype(BF16)
        gu = _dot(xs, wgu_bf[...])
        gate, up = gu[:, :EXPERT_HIDDEN], gu[:, EXPERT_HIDDEN:]
        hid = (gate / (1.0 + jnp.exp(-gate)) * up).astype(BF16)
        _to_token_tiles(ys_ref, _dot(hid, wdn_bf[...]), tm)


def _experts(tile_blk, tile_exp, tile_n, xs, w_gu, w_dn, layer):
    tm = EXPERT_TILE
    nt = xs.shape[0] // (tm * SUBLANES)
    gu_shape, dn_shape = w_gu.shape[2:], w_dn.shape[2:]
    return pl.pallas_call(
        functools.partial(_expert_kernel, tm=tm),
        grid_spec=pltpu.PrefetchScalarGridSpec(
            num_scalar_prefetch=3,
            grid=(nt,),
            in_specs=[pl.BlockSpec((tm * SUBLANES, LANES), lambda i, tb, te, tn: (tb[i], 0)),
                      pl.BlockSpec((1, 1) + gu_shape, lambda i, tb, te, tn: (layer, te[i], 0, 0)),
                      pl.BlockSpec((1, 1) + dn_shape, lambda i, tb, te, tn: (layer, te[i], 0, 0))],
            out_specs=pl.BlockSpec((tm * SUBLANES, LANES), lambda i, tb, te, tn: (tb[i], 0)),
            scratch_shapes=[pltpu.VMEM(gu_shape, BF16), pltpu.VMEM(dn_shape, BF16)]),
        out_shape=jax.ShapeDtypeStruct(xs.shape, F32),
        compiler_params=_cparams(("arbitrary",)),
        name="moe_experts",
    )(tile_blk, tile_exp, tile_n, xs, w_gu, w_dn)


def _combine_kernel(off_ref, mi_ref, mw_ref, x_ref, ys_ref, *rest, tm, final):
    if final:
        g_ref, o_ref, ybuf0, ybuf1, sem = rest
    else:
        o_ref, ybuf0, ybuf1, sem = rest
    ybufs = (ybuf0, ybuf1)

    def issue(g, _):
        for u in range(DMA_UNROLL):
            r = g * DMA_UNROLL + u
            for kk in range(TOP_K):
                pos = _sorted_pos(off_ref, mi_ref, kk, r)
                pltpu.make_async_copy(_token_slice(ys_ref, pos), _token_slice(ybufs[kk], r), sem).start(priority=kk)
        return 0

    lax.fori_loop(0, tm // DMA_UNROLL, issue, 0)
    for kk in range(TOP_K):
        pltpu.make_async_copy(ys_ref.at[pl.ds(0, tm * SUBLANES)], ybufs[kk], sem).wait()
    mw = mw_ref[...]
    x3 = x_ref[...] + mw[:, 0:1] * _from_token_tiles(ybuf0, tm) + mw[:, 1:2] * _from_token_tiles(ybuf1, tm)
    if final:
        x3 = _rms(x3, g_ref[...])
    o_ref[...] = x3


def _combine(off, meta_i, meta_w, x2d, ys, g_final):
    t, d = x2d.shape
    tm = ROW_TILE
    final = g_final is not None
    in_specs = [pl.BlockSpec((8, tm), lambda i, off: (0, i), memory_space=pltpu.SMEM),
                pl.BlockSpec((tm, LANES), lambda i, off: (i, 0)),
                pl.BlockSpec((tm, d), lambda i, off: (i, 0)),
                pl.BlockSpec(memory_space=pl.ANY)]
    args = [off, meta_i, meta_w, x2d, ys]
    if final:
        in_specs.append(pl.BlockSpec((1, d), lambda i, off: (0, 0)))
        args.append(g_final.reshape(1, d))
    return pl.pallas_call(
        functools.partial(_combine_kernel, tm=tm, final=final),
        grid_spec=pltpu.PrefetchScalarGridSpec(
            num_scalar_prefetch=1,
            grid=(t // tm,),
            in_specs=in_specs,
            out_specs=pl.BlockSpec((tm, d), lambda i, off: (i, 0)),
            scratch_shapes=[pltpu.VMEM((tm * SUBLANES, LANES), F32), pltpu.VMEM((tm * SUBLANES, LANES), F32),
                            pltpu.SemaphoreType.DMA]),
        out_shape=jax.ShapeDtypeStruct((t, d), F32),
        compiler_params=_cparams(("arbitrary",)),
        name="moe_combine",
    )(*args)


def _moe_plan(counts, n_tiles):
    tm = EXPERT_TILE
    cnt = counts[0, :N_EXPERTS].astype(I32)
    padded = ((cnt + tm - 1) // tm) * tm
    ends = jnp.cumsum(padded)
    off = ends - padded
    start = jnp.arange(n_tiles, dtype=I32) * tm
    exp = jnp.minimum(jnp.sum((start[:, None] >= ends[None, :]).astype(I32), axis=1), N_EXPERTS - 1)
    n_valid = jnp.clip(off[exp] + cnt[exp] - start, 0, tm)
    active = start < ends[-1]
    n_valid = jnp.where(active, n_valid, 0)
    last = jnp.maximum(ends[-1] // tm - 1, 0)
    blk = jnp.where(active, jnp.arange(n_tiles, dtype=I32), last)
    exp = jnp.where(active, exp, exp[last])
    return off.astype(I32), blk.astype(I32), exp.astype(I32), n_valid.astype(I32)


def _moe(x2, h2t, meta_i, meta_w, counts, w_gu, w_dn, layer, g_final):
    b, s, d = x2.shape
    assert d == SUBLANES * LANES
    t = b * s
    n_rows = t * TOP_K + N_EXPERTS * EXPERT_TILE
    off, blk, exp, n_valid = _moe_plan(counts, n_rows // EXPERT_TILE)
    xs = _dispatch(off, meta_i, h2t, n_rows)
    ys = _experts(blk, exp, n_valid, xs, w_gu, w_dn, layer)
    out = _combine(off, meta_i, meta_w, x2.reshape(t, d), ys, g_final)
    return out.reshape(b, s, d)


def _router_weights(w_group, b_group, w_router, b_router):
    d = w_group.shape[0]
    pad = LANES - N_EXPERTS - N_GROUPS
    w = jnp.concatenate([w_router, w_group, jnp.zeros((d, pad), F32)], axis=1)
    bias = jnp.concatenate([b_router, b_group, jnp.zeros((pad,), F32)]).reshape(1, LANES)
    w_hi = w.astype(BF16)
    w_lo = (w - w_hi.astype(F32)).astype(BF16)
    return jnp.stack([w_hi, w_lo]), bias


def kernel(x, mem, mem_norm, mem_w_kv, norm_mix, norm_xattn, norm_ffn, hyb_w_in, hyb_conv_w, diff_lambda, diff_subln, hyb_w_out, swa_w_qkv, swa_b_qkv, swa_sinks, swa_w_out, swa_b_out, xattn_w_q, xattn_w_o, moe_w_group, moe_b_group, moe_w_router, moe_b_router, moe_w_gate_up, moe_w_down, final_norm):
    b, s, d = x.shape
    m = mem.shape[1]
    depth = norm_mix.shape[0]
    mkv = _norm_proj(mem.reshape(b * m, d), mem_norm, mem_w_kv.astype(BF16), ROW_TILE).reshape(b, m, -1)

    scale = DIFF_DK ** -0.5
    cq, suq, sdq, half = _rope_lane_tables(s, DIFF_DK, scale)
    ck, suk, sdk, _ = _rope_lane_tables(s, DIFF_DK, 1.0)
    tabs_q, tabs_k = (cq, suq, sdq), (ck, suk, sdk)

    for l in range(depth):
        if l % 2 == 0:
            e = l // 2
            lambda_init = 0.8 - 0.6 * math.exp(-0.3 * l)
            ya, q, k, v = _hyb_front(x, norm_mix[l], hyb_w_in[e].astype(BF16), hyb_conv_w[e], tabs_q, tabs_k, half)
            o = _diff_attn(q, k, v, diff_lambda[e], diff_subln[e], lambda_init)
            w_out = hyb_w_out[e].astype(BF16)
            a_list, wo_list, b_out = [ya, o], [w_out[:CONV_CH], w_out[CONV_CH:]], None
        else:
            e = l // 2
            q, kv = _swa_front(x, norm_mix[l], swa_w_qkv[e].astype(BF16), swa_b_qkv[e], tabs_q, tabs_k, half)
            o = _swa_attn(q, kv, swa_sinks[e])
            a_list, wo_list, b_out = [o], [swa_w_out[e].astype(BF16)], swa_b_out[e]
        w_rt, b_rt = _router_weights(moe_w_group[l], moe_b_group[l], moe_w_router[l], moe_b_router[l])
        x2, h2, meta_i, meta_w, counts = _post_mixer(
            x, a_list, wo_list, b_out, norm_xattn[l], xattn_w_q[l].astype(BF16), mkv,
            xattn_w_o[l].astype(BF16), norm_ffn[l], w_rt, b_rt)
        x = _moe(x2, h2, meta_i, meta_w, counts, moe_w_gate_up, moe_w_down, l,
                 final_norm if l == depth - 1 else None)
    return x
```

```python
import functools
import math

import jax
import jax.numpy as jnp
from jax import lax
from jax.experimental import pallas as pl
from jax.experimental.pallas import tpu as pltpu

F32 = jnp.float32
BF16 = jnp.bfloat16
I32 = jnp.int32

EPS = 1e-6
LANES = 128
SUBLANES = 8
DMA_UNROLL = 8
VMEM_LIMIT = 56 * 1024 * 1024

ROPE_THETA = 500000.0
ROPE_FRACTION = 4
BLOCK = 128
CONV_CH = 512
CONV_K = 3
DIFF_HEADS = 4
DIFF_DK = 64
DIFF_DV = 128
SWA_Q_HEADS = 16
SWA_KV_HEADS = 4
SWA_HEAD_DIM = 64
SWA_WINDOW = 128
XATTN_HEADS = 4
XATTN_HEAD_DIM = 128
N_GROUPS = 4
EXPERTS_PER_GROUP = 8
N_EXPERTS = N_GROUPS * EXPERTS_PER_GROUP
TOP_K = 2
EXPERT_HIDDEN = 512

ROW_TILE = 512
EXPERT_TILE = 512
DISPATCH_TILE = 1024
COMBINE_TILE = 512
NEG_INF = float("-inf")


def _cparams(sem):
    return pltpu.CompilerParams(dimension_semantics=sem, vmem_limit_bytes=VMEM_LIMIT)


def _rms(x, g):
    return x * lax.rsqrt(jnp.mean(x * x, axis=-1, keepdims=True) + EPS) * g


def _dot(a, b):
    return jnp.dot(a, b, preferred_element_type=F32)


def _dot_nt(a, b):
    return lax.dot_general(a, b, (((1,), (1,)), ((), ())), preferred_element_type=F32)


def _rope_lane_tables(seq, head_dim, scale):
    rot = head_dim // ROPE_FRACTION
    half = rot // 2
    pos = jnp.arange(seq, dtype=F32)
    inv = ROPE_THETA ** (-jnp.arange(0, rot, 2, dtype=F32) / rot)
    ang = pos[:, None] * inv[None, :]
    cos, sin = jnp.cos(ang), jnp.sin(ang)
    idx = jnp.arange(LANES) % head_dim
    cl = jnp.take(cos, idx % half, axis=1)
    sl = jnp.take(sin, idx % half, axis=1)
    c = jnp.where(idx < rot, cl, 1.0) * scale
    s_up = jnp.where(idx < half, -sl, 0.0) * scale
    s_dn = jnp.where((idx >= half) & (idx < rot), sl, 0.0) * scale
    return c.astype(F32), s_up.astype(F32), s_dn.astype(F32), half


def _rope_chunk(xc, c, s_up, s_dn, half):
    return (xc * c + pltpu.roll(xc, LANES - half, 1) * s_up + pltpu.roll(xc, half, 1) * s_dn)


def _norm_proj_kernel(x_ref, g_ref, w_ref, o_ref):
    h = _rms(x_ref[...], g_ref[...]).astype(BF16)
    o_ref[...] = _dot(h, w_ref[...]).astype(o_ref.dtype)


def _norm_proj(x2d, g, w_bf16, tm):
    m, d = x2d.shape
    n = w_bf16.shape[1]
    return pl.pallas_call(
        _norm_proj_kernel,
        grid=(m // tm,),
        in_specs=[pl.BlockSpec((tm, d), lambda i: (i, 0)),
                  pl.BlockSpec((1, d), lambda i: (0, 0)),
                  pl.BlockSpec((d, n), lambda i: (0, 0))],
        out_specs=pl.BlockSpec((tm, n), lambda i: (i, 0)),
        out_shape=jax.ShapeDtypeStruct((m, n), BF16),
        compiler_params=_cparams(("arbitrary",)),
        name="mem_kv_proj",
    )(x2d, g.reshape(1, d), w_bf16)


def _hyb_front_kernel(x_ref, g_ref, w_ref, cw_ref, cq_ref, suq_ref, sdq_ref, ck_ref, suk_ref, sdk_ref,
                      ya_ref, q_ref, k_ref, v_ref, cbuf, *, tm, half):
    s = pl.program_id(1)
    h = _rms(x_ref[0], g_ref[...]).astype(BF16)
    p = _dot(h, w_ref[...])
    c = CONV_CH
    gate_b, gate_c, u = p[:, 0:c], p[:, c:2 * c], p[:, 2 * c:3 * c]
    cu = gate_c * u

    @pl.when(s == 0)
    def _():
        cbuf[0:8, :] = jnp.zeros((8, c), F32)

    cbuf[8:8 + tm, :] = cu
    cw = cw_ref[...]
    conv = (cw[0:1, :] * cbuf[6:6 + tm, :] + cw[1:2, :] * cbuf[7:7 + tm, :] + cw[2:3, :] * cu)
    ya_ref[0] = (gate_b * conv).astype(BF16)
    cbuf[0:8, :] = cbuf[tm:tm + 8, :]

    base = 3 * c
    nq = DIFF_HEADS * 2 * DIFF_DK
    cq, suq, sdq = cq_ref[...], suq_ref[...], sdq_ref[...]
    ck, suk, sdk = ck_ref[...], suk_ref[...], sdk_ref[...]
    for j in range(nq // LANES):
        lo = base + j * LANES
        q_ref[0, :, j * LANES:(j + 1) * LANES] = _rope_chunk(p[:, lo:lo + LANES], cq, suq, sdq, half).astype(BF16)
        lo = base + nq + j * LANES
        k_ref[0, :, j * LANES:(j + 1) * LANES] = _rope_chunk(p[:, lo:lo + LANES], ck, suk, sdk, half).astype(BF16)
    v_ref[0] = p[:, base + 2 * nq:].astype(BF16)


def _hyb_front(x, g, w_in, conv_w, tabs_q, tabs_k, half):
    b, s, d = x.shape
    tm = ROW_TILE
    n = w_in.shape[1]
    nq = DIFF_HEADS * 2 * DIFF_DK
    nv = DIFF_HEADS * DIFF_DV
    tab_spec = pl.BlockSpec((tm, LANES), lambda bi, si: (si, 0))
    row = lambda width: pl.BlockSpec((1, tm, width), lambda bi, si: (bi, si, 0))
    return pl.pallas_call(
        functools.partial(_hyb_front_kernel, tm=tm, half=half),
        grid=(b, s // tm),
        in_specs=[row(d),
                  pl.BlockSpec((1, d), lambda bi, si: (0, 0)),
                  pl.BlockSpec((d, n), lambda bi, si: (0, 0)),
                  pl.BlockSpec((CONV_K, CONV_CH), lambda bi, si: (0, 0)),
                  tab_spec, tab_spec, tab_spec, tab_spec, tab_spec, tab_spec],
        out_specs=[row(CONV_CH), row(nq), row(nq), row(nv)],
        out_shape=[jax.ShapeDtypeStruct((b, s, CONV_CH), BF16),
                   jax.ShapeDtypeStruct((b, s, nq), BF16),
                   jax.ShapeDtypeStruct((b, s, nq), BF16),
                   jax.ShapeDtypeStruct((b, s, nv), BF16)],
        scratch_shapes=[pltpu.VMEM((tm + 8, CONV_CH), F32)],
        compiler_params=_cparams(("arbitrary", "arbitrary")),
        name="hyb_front",
    )(x, g.reshape(1, d), w_in, conv_w, *tabs_q, *tabs_k)


def _lane_fold(x, op):
    r = x[:, 0:LANES]
    for c in range(1, x.shape[1] // LANES):
        r = op(r, x[:, c * LANES:(c + 1) * LANES])
    return r


def _diff_attn_kernel(q_ref, k_ref, v_ref, lam_ref, g_ref, o_ref, sbuf, stat, acc, *, tq, hp, lambda_init):
    i = pl.program_id(2)
    lane = lax.broadcasted_iota(I32, (1, LANES), 1)
    qs = []
    for h in range(hp):
        q = q_ref[0, :, h * LANES:(h + 1) * LANES]
        zero = jnp.zeros_like(q)
        qs += [jnp.where(lane < DIFF_DK, q, zero), jnp.where(lane >= DIFF_DK, q, zero)]
    nc = 2 * hp

    def scores(j, masked):
        for c in range(nc):
            h = c // 2
            kb = k_ref[0, pl.ds(pl.multiple_of(j * tq, tq), tq), h * LANES:(h + 1) * LANES]
            sc = _dot_nt(qs[c], kb)
            if masked:
                r = lax.broadcasted_iota(I32, (tq, tq), 0)
                cc = lax.broadcasted_iota(I32, (tq, tq), 1)
                sc = jnp.where(cc <= r, sc, NEG_INF)
            sbuf[c, j] = sc
            stat[c] = jnp.maximum(stat[c], _lane_fold(sc, jnp.maximum))

    stat[...] = jnp.full(stat.shape, NEG_INF, F32)

    def pass1(j, carry):
        scores(j, False)
        return carry

    lax.fori_loop(0, i, pass1, 0)
    scores(i, True)
    ms = [jnp.max(stat[c], axis=-1, keepdims=True) for c in range(nc)]

    acc[...] = jnp.zeros_like(acc)
    stat[...] = jnp.zeros(stat.shape, F32)

    def pass2(j, carry):
        for h in range(hp):
            vb = v_ref[0, pl.ds(pl.multiple_of(j * tq, tq), tq), h * LANES:(h + 1) * LANES]
            p0 = jnp.exp(sbuf[2 * h, j] - ms[2 * h])
            p1 = jnp.exp(sbuf[2 * h + 1, j] - ms[2 * h + 1])
            acc[h] += _dot(jnp.concatenate([p0, p1], axis=0).astype(BF16), vb)
            stat[2 * h] += _lane_fold(p0, jnp.add)
            stat[2 * h + 1] += _lane_fold(p1, jnp.add)
        return carry

    lax.fori_loop(0, i + 1, pass2, 0)

    lf = lam_ref[...]
    lam = (jnp.exp(jnp.sum(lf[0:1] * lf[1:2], keepdims=True))
           - jnp.exp(jnp.sum(lf[2:3] * lf[3:4], keepdims=True)) + lambda_init)
    for h in range(hp):
        l0 = jnp.sum(stat[2 * h], axis=-1, keepdims=True)
        l1 = jnp.sum(stat[2 * h + 1], axis=-1, keepdims=True)
        o = acc[h, 0:tq, :] / l0 - lam * (acc[h, tq:2 * tq, :] / l1)
        o_ref[0, :, h * LANES:(h + 1) * LANES] = (_rms(o, g_ref[...]) * (1.0 - lambda_init)).astype(BF16)


def _diff_attn(q, k, v, lam_vecs, subln_g, lambda_init):
    b, s, _ = q.shape
    tq = 256
    hp = 4
    blk = lambda bi, hi, qi: (bi, qi, hi)
    full = lambda bi, hi, qi: (bi, 0, hi)
    return pl.pallas_call(
        functools.partial(_diff_attn_kernel, tq=tq, hp=hp, lambda_init=lambda_init),
        grid=(b, DIFF_HEADS // hp, s // tq),
        in_specs=[pl.BlockSpec((1, tq, hp * LANES), blk),
                  pl.BlockSpec((1, s, hp * LANES), full),
                  pl.BlockSpec((1, s, hp * LANES), full),
                  pl.BlockSpec((4, DIFF_DK), lambda bi, hi, qi: (0, 0)),
                  pl.BlockSpec((1, DIFF_DV), lambda bi, hi, qi: (0, 0))],
        out_specs=pl.BlockSpec((1, tq, hp * LANES), blk),
        out_shape=jax.ShapeDtypeStruct((b, s, DIFF_HEADS * DIFF_DV), BF16),
        scratch_shapes=[pltpu.VMEM((2 * hp, s // tq, tq, tq), F32),
                        pltpu.VMEM((2 * hp, tq, LANES), F32),
                        pltpu.VMEM((hp, 2 * tq, DIFF_DV), F32)],
        compiler_params=_cparams(("arbitrary", "arbitrary", "arbitrary")),
        name="diff_attn",
    )(q, k, v, lam_vecs, subln_g.reshape(1, DIFF_DV))


def _swa_front_kernel(x_ref, g_ref, w_ref, b_ref, cq_ref, suq_ref, sdq_ref, ck_ref, suk_ref, sdk_ref,
                      q_ref, kv_ref, *, half):
    h = _rms(x_ref[0], g_ref[...]).astype(BF16)
    p = _dot(h, w_ref[...]) + b_ref[...]
    nq = SWA_Q_HEADS * SWA_HEAD_DIM
    nkv = SWA_KV_HEADS * SWA_HEAD_DIM
    cq, suq, sdq = cq_ref[...], suq_ref[...], sdq_ref[...]
    ck, suk, sdk = ck_ref[...], suk_ref[...], sdk_ref[...]
    for j in range(nq // LANES):
        lo = j * LANES
        q_ref[0, :, lo:lo + LANES] = _rope_chunk(p[:, lo:lo + LANES], cq, suq, sdq, half).astype(BF16)
    for j in range(nkv // LANES):
        lo = j * LANES
        kv_ref[0, :, lo:lo + LANES] = _rope_chunk(p[:, nq + lo:nq + lo + LANES], ck, suk, sdk, half).astype(BF16)
    kv_ref[0, :, nkv:] = p[:, nq + nkv:].astype(BF16)


def _swa_front(x, g, w_qkv, b_qkv, tabs_q, tabs_k, half):
    b, s, d = x.shape
    tm = ROW_TILE
    n = w_qkv.shape[1]
    nq = SWA_Q_HEADS * SWA_HEAD_DIM
    nkv = SWA_KV_HEADS * SWA_HEAD_DIM
    tab_spec = pl.BlockSpec((tm, LANES), lambda bi, si: (si, 0))
    row = lambda width: pl.BlockSpec((1, tm, width), lambda bi, si: (bi, si, 0))
    return pl.pallas_call(
        functools.partial(_swa_front_kernel, half=half),
        grid=(b, s // tm),
        in_specs=[row(d),
                  pl.BlockSpec((1, d), lambda bi, si: (0, 0)),
                  pl.BlockSpec((d, n), lambda bi, si: (0, 0)),
                  pl.BlockSpec((1, n), lambda bi, si: (0, 0)),
                  tab_spec, tab_spec, tab_spec, tab_spec, tab_spec, tab_spec],
        out_specs=[row(nq), row(2 * nkv)],
        out_shape=[jax.ShapeDtypeStruct((b, s, nq), BF16),
                   jax.ShapeDtypeStruct((b, s, 2 * nkv), BF16)],
        compiler_params=_cparams(("arbitrary", "arbitrary")),
        name="swa_front",
    )(x, g.reshape(1, d), w_qkv, b_qkv.reshape(1, n), *tabs_q, *tabs_k)


def _swa_attn_kernel(sink_ref, q_ref, kvp_ref, kvc_ref, o_ref):
    n = pl.program_id(1)
    hd = SWA_HEAD_DIM
    g_sz = SWA_Q_HEADS // SWA_KV_HEADS
    nkv = SWA_KV_HEADS * hd
    q = q_ref[0]
    kv = jnp.concatenate([kvp_ref[0], kvc_ref[0]], axis=0)
    r = lax.broadcasted_iota(I32, (BLOCK, 2 * BLOCK), 0)
    c = lax.broadcasted_iota(I32, (BLOCK, 2 * BLOCK), 1)
    rel = c - BLOCK - r
    mask = (rel <= 0) & (rel > -SWA_WINDOW) & ((c >= BLOCK) | (n > 0))
    outs = []
    for kh in range(SWA_KV_HEADS):
        kg = kv[:, kh * hd:(kh + 1) * hd]
        vg = kv[:, nkv + kh * hd:nkv + (kh + 1) * hd]
        for j in range(g_sz):
            h = kh * g_sz + j
            sc = _dot_nt(q[:, h * hd:(h + 1) * hd], kg)
            sc = jnp.where(mask, sc, NEG_INF)
            sink = sink_ref[h]
            m = jnp.maximum(jnp.max(sc, axis=-1, keepdims=True), sink)
            pr = jnp.exp(sc - m)
            den = jnp.sum(pr, axis=-1, keepdims=True) + jnp.exp(sink - m)
            outs.append(_dot(pr.astype(BF16), vg) / den)
    o_ref[0] = jnp.concatenate(outs, axis=-1).astype(BF16)


def _swa_attn(q, kv, sinks):
    b, s, nq = q.shape
    nb = s // BLOCK
    return pl.pallas_call(
        _swa_attn_kernel,
        grid_spec=pltpu.PrefetchScalarGridSpec(
            num_scalar_prefetch=1,
            grid=(b, nb),
            in_specs=[pl.BlockSpec((1, BLOCK, nq), lambda bi, ni, sk: (bi, ni, 0)),
                      pl.BlockSpec((1, BLOCK, kv.shape[2]), lambda bi, ni, sk: (bi, jnp.maximum(ni - 1, 0), 0)),
                      pl.BlockSpec((1, BLOCK, kv.shape[2]), lambda bi, ni, sk: (bi, ni, 0))],
            out_specs=pl.BlockSpec((1, BLOCK, nq), lambda bi, ni, sk: (bi, ni, 0))),
        out_shape=jax.ShapeDtypeStruct((b, s, nq), BF16),
        compiler_params=_cparams(("arbitrary", "arbitrary")),
        name="swa_attn",
    )(sinks, q, kv, kv)


def _post_kernel(*refs, tm, n_a, has_bias):
    x_ref = refs[0]
    a_refs = refs[1:1 + n_a]
    k = 1 + n_a
    wo_refs = refs[k:k + n_a]
    k += n_a
    if has_bias:
        bo_ref = refs[k]
        k += 1
    (gx_ref, wq_ref, mkv_ref, wxo_ref, gf_ref, wr_ref, br_ref,
     x2_ref, h2_ref, mi_ref, mw_ref, cnt_ref, cnt_acc) = refs[k:]
    first = (pl.program_id(0) == 0) & (pl.program_id(1) == 0)

    @pl.when(first)
    def _():
        cnt_acc[...] = jnp.zeros_like(cnt_acc)

    acc = _dot(a_refs[0][0], wo_refs[0][...])
    for a_ref, w_ref in zip(a_refs[1:], wo_refs[1:]):
        acc = acc + _dot(a_ref[0], w_ref[...])
    if has_bias:
        acc = acc + bo_ref[...]
    x1 = x_ref[0] + acc

    hx = _rms(x1, gx_ref[...]).astype(BF16)
    qx = (_dot(hx, wq_ref[...]) * (XATTN_HEAD_DIM ** -0.5)).astype(BF16)
    xw = XATTN_HEADS * XATTN_HEAD_DIM
    outs = []
    for hh in range(XATTN_HEADS):
        lo = hh * XATTN_HEAD_DIM
        mk = mkv_ref[0, :, lo:lo + XATTN_HEAD_DIM]
        mv = mkv_ref[0, :, xw + lo:xw + lo + XATTN_HEAD_DIM]
        sc = _dot_nt(qx[:, lo:lo + XATTN_HEAD_DIM], mk)
        m = jnp.max(sc, axis=-1, keepdims=True)
        pr = jnp.exp(sc - m)
        den = jnp.sum(pr, axis=-1, keepdims=True)
        outs.append((_dot(pr.astype(BF16), mv) / den).astype(BF16))
    ox = jnp.concatenate(outs, axis=-1)
    x2 = x1 + _dot(ox, wxo_ref[...])
    x2_ref[0] = x2

    h2 = _rms(x2, gf_ref[...])
    _to_token_tiles(h2_ref, h2, tm)
    h_hi = h2.astype(BF16)
    h_lo = (h2 - h_hi.astype(F32)).astype(BF16)
    logits = (_dot(h_hi, wr_ref[0]) + _dot(h_lo, wr_ref[0]) + _dot(h_hi, wr_ref[1])
              + br_ref[...])
    lane = lax.broadcasted_iota(I32, (tm, LANES), 1).astype(F32)
    big = float(LANES)
    g_lo = float(N_EXPERTS)
    lg = jnp.where((lane >= g_lo) & (lane < g_lo + N_GROUPS), logits, NEG_INF)
    mg = jnp.max(lg, axis=-1, keepdims=True)
    g_lane = jnp.min(jnp.where(lg == mg, lane, big), axis=-1, keepdims=True)
    p_g = 1.0 / jnp.sum(jnp.exp(lg - mg), axis=-1, keepdims=True)
    e_lo = (g_lane - g_lo) * EXPERTS_PER_GROUP
    le = jnp.where((lane >= e_lo) & (lane < e_lo + EXPERTS_PER_GROUP), logits, NEG_INF)
    m1 = jnp.max(le, axis=-1, keepdims=True)
    i1 = jnp.min(jnp.where(le == m1, lane, big), axis=-1, keepdims=True)
    le2 = jnp.where(lane == i1, NEG_INF, le)
    m2 = jnp.max(le2, axis=-1, keepdims=True)
    i2 = jnp.min(jnp.where(le2 == m2, lane, big), axis=-1, keepdims=True)
    t = jnp.exp(m2 - m1)
    w1 = p_g / (1.0 + t)
    w2 = p_g * t / (1.0 + t)

    oh1 = lane == i1
    oh2 = lane == i2
    oh = jnp.where(oh1 | oh2, 1.0, 0.0)
    rr = lax.broadcasted_iota(I32, (tm, tm), 0)
    cc = lax.broadcasted_iota(I32, (tm, tm), 1)
    tri = jnp.where(cc < rr, 1.0, 0.0).astype(BF16)
    before = _dot(tri, oh.astype(BF16)) + cnt_acc[...]
    r1 = jnp.sum(jnp.where(oh1, before, 0.0), axis=-1, keepdims=True)
    r2 = jnp.sum(jnp.where(oh2, before, 0.0), axis=-1, keepdims=True)
    cnt_new = cnt_acc[...] + jnp.sum(oh, axis=0, keepdims=True)
    cnt_acc[...] = cnt_new
    cnt_ref[...] = cnt_new

    mw_ref[...] = jnp.where(lane == 0, w1, jnp.where(lane == 1, w2, 0.0))
    ints = jnp.where(lane == 0, i1,
                     jnp.where(lane == 1, i2,
                               jnp.where(lane == 2, r1, jnp.where(lane == 3, r2, 0.0))))
    mi_ref[...] = ints.T[0:8, :].astype(I32)


def _post_mixer(x, a_list, wo_list, b_out, g_x, w_q, mkv, w_xo, g_f, w_rt, b_rt):
    b, s, d = x.shape
    tm = ROW_TILE
    ns = s // tm
    t = b * s
    n_a = len(a_list)
    has_bias = b_out is not None
    const2 = lambda bi, si: (0, 0)
    row = lambda width: pl.BlockSpec((1, tm, width), lambda bi, si: (bi, si, 0))
    in_specs = [row(d)] + [row(a.shape[2]) for a in a_list]
    in_specs += [pl.BlockSpec(w.shape, const2) for w in wo_list]
    args = [x, *a_list, *wo_list]
    if has_bias:
        in_specs.append(pl.BlockSpec((1, d), const2))
        args.append(b_out.reshape(1, d))
    xw = w_q.shape[1]
    in_specs += [pl.BlockSpec((1, d), const2),
                 pl.BlockSpec((d, xw), const2),
                 pl.BlockSpec((1, mkv.shape[1], mkv.shape[2]), lambda bi, si: (bi, 0, 0)),
                 pl.BlockSpec((xw, d), const2),
                 pl.BlockSpec((1, d), const2),
                 pl.BlockSpec((2, d, LANES), lambda bi, si: (0, 0, 0)),
                 pl.BlockSpec((1, LANES), const2)]
    args += [g_x.reshape(1, d), w_q, mkv, w_xo, g_f.reshape(1, d), w_rt, b_rt]
    out_specs = [row(d),
                 pl.BlockSpec((tm * SUBLANES, LANES), lambda bi, si: (bi * ns + si, 0)),
                 pl.BlockSpec((8, tm), lambda bi, si: (0, bi * ns + si)),
                 pl.BlockSpec((tm, LANES), lambda bi, si: (bi * ns + si, 0)),
                 pl.BlockSpec((1, LANES), const2)]
    out_shape = [jax.ShapeDtypeStruct((b, s, d), F32),
                 jax.ShapeDtypeStruct((t * SUBLANES, LANES), F32),
                 jax.ShapeDtypeStruct((8, t), I32),
                 jax.ShapeDtypeStruct((t, LANES), F32),
                 jax.ShapeDtypeStruct((1, LANES), F32)]
    return pl.pallas_call(
        functools.partial(_post_kernel, tm=tm, n_a=n_a, has_bias=has_bias),
        grid=(b, ns),
        in_specs=in_specs,
        out_specs=out_specs,
        out_shape=out_shape,
        scratch_shapes=[pltpu.VMEM((1, LANES), F32)],
        compiler_params=_cparams(("arbitrary", "arbitrary")),
        name="post_mixer",
    )(*args)


def _to_token_tiles(dst_ref, val, rows):
    for j in range(SUBLANES):
        dst_ref[pl.ds(j, rows, stride=SUBLANES), :] = val[:, j * LANES:(j + 1) * LANES]


def _from_token_tiles(src_ref, rows):
    return jnp.concatenate([src_ref[pl.ds(j, rows, stride=SUBLANES), :] for j in range(SUBLANES)], axis=1)


def _token_slice(ref, tok):
    return ref.at[pl.ds(pl.multiple_of(tok * SUBLANES, SUBLANES), SUBLANES)]


def _pos_kernel(off_ref, mi_ref, pos_ref):
    e = mi_ref[0:TOP_K, :]
    pos = mi_ref[TOP_K:2 * TOP_K, :]
    for j in range(N_EXPERTS):
        pos = pos + jnp.where(e == j, off_ref[j], 0)
    pos_ref[...] = pos


def _sorted_positions(off, meta_i):
    t = meta_i.shape[1]
    return pl.pallas_call(
        _pos_kernel,
        grid_spec=pltpu.PrefetchScalarGridSpec(
            num_scalar_prefetch=1,
            grid=(1,),
            in_specs=[pl.BlockSpec((8, t), lambda i, off: (0, 0))],
            out_specs=pl.BlockSpec((TOP_K, t), lambda i, off: (0, 0))),
        out_shape=jax.ShapeDtypeStruct((TOP_K, t), I32),
        compiler_params=_cparams(("arbitrary",)),
        name="moe_positions",
    )(off, meta_i)


def _tile_major(pos, tm):
    t = pos.shape[1]
    return pos.reshape(TOP_K, t // tm, tm).transpose(1, 0, 2).reshape(-1)


def _dispatch_kernel(pos_ref, h_ref, xs_ref, sem, *, tm):
    def issue(g, _):
        for u in range(DMA_UNROLL):
            r = g * DMA_UNROLL + u
            for kk in range(TOP_K):
                pos = pos_ref[kk * tm + r]
                pltpu.make_async_copy(_token_slice(h_ref, r), _token_slice(xs_ref, pos), sem).start(priority=kk)
        return 0

    lax.fori_loop(0, tm // DMA_UNROLL, issue, 0)
    for _ in range(TOP_K):
        pltpu.make_async_copy(h_ref, xs_ref.at[pl.ds(0, tm * SUBLANES)], sem).wait()


def _dispatch(pos, h2t, n_rows):
    tm = DISPATCH_TILE
    t = h2t.shape[0] // SUBLANES
    return pl.pallas_call(
        functools.partial(_dispatch_kernel, tm=tm),
        grid=(t // tm,),
        in_specs=[pl.BlockSpec((TOP_K * tm,), lambda i: (i,), memory_space=pltpu.SMEM),
                  pl.BlockSpec((tm * SUBLANES, LANES), lambda i: (i, 0))],
        out_specs=pl.BlockSpec(memory_space=pl.ANY),
        scratch_shapes=[pltpu.SemaphoreType.DMA],
        out_shape=jax.ShapeDtypeStruct((n_rows * SUBLANES, LANES), F32),
        compiler_params=_cparams(("arbitrary",)),
        name="moe_dispatch",
    )(_tile_major(pos, tm), h2t)


def _expert_kernel(tblk_ref, texp_ref, tn_ref, xs_ref, wgu_ref, wdn_ref, ys_ref, wgu_bf, wdn_bf, *, tm):
    i = pl.program_id(0)
    n_valid = tn_ref[i]
    new_expert = (i == 0) | (texp_ref[i] != texp_ref[jnp.maximum(i - 1, 0)])

    @pl.when(new_expert)
    def _():
        wgu_bf[...] = wgu_ref[0, 0].astype(BF16)
        wdn_bf[...] = wdn_ref[0, 0].astype(BF16)

    @pl.when(n_valid > 0)
    def _():
        row = lax.broadcasted_iota(I32, (tm, 1), 0)
        xs = jnp.where(row < n_valid, _from_token_tiles(xs_ref, tm), 0.0).astype(BF16)
        gu = _dot(xs, wgu_bf[...])
        gate, up = gu[:, :EXPERT_HIDDEN], gu[:, EXPERT_HIDDEN:]
        hid = (gate / (1.0 + jnp.exp(-gate)) * up).astype(BF16)
        _to_token_tiles(ys_ref, _dot(hid, wdn_bf[...]), tm)


def _experts(tile_blk, tile_exp, tile_n, xs, w_gu, w_dn, layer):
    tm = EXPERT_TILE
    nt = xs.shape[0] // (tm * SUBLANES)
    gu_shape, dn_shape = w_gu.shape[2:], w_dn.shape[2:]
    return pl.pallas_call(
        functools.partial(_expert_kernel, tm=tm),
        grid_spec=pltpu.PrefetchScalarGridSpec(
            num_scalar_prefetch=3,
            grid=(nt,),
            in_specs=[pl.BlockSpec((tm * SUBLANES, LANES), lambda i, tb, te, tn: (tb[i], 0)),
                      pl.BlockSpec((1, 1) + gu_shape, lambda i, tb, te, tn: (layer, te[i], 0, 0)),
                      pl.BlockSpec((1, 1) + dn_shape, lambda i, tb, te, tn: (layer, te[i], 0, 0))],
            out_specs=pl.BlockSpec((tm * SUBLANES, LANES), lambda i, tb, te, tn: (tb[i], 0)),
            scratch_shapes=[pltpu.VMEM(gu_shape, BF16), pltpu.VMEM(dn_shape, BF16)]),
        out_shape=jax.ShapeDtypeStruct(xs.shape, F32),
        compiler_params=_cparams(("arbitrary",)),
        name="moe_experts",
    )(tile_blk, tile_exp, tile_n, xs, w_gu, w_dn)


def _combine_kernel(pos_cur, pos_nxt, mw_ref, x_ref, ys_ref, *rest, tm, final):
    if final:
        g_ref, o_ref, ybuf, sems = rest
    else:
        o_ref, ybuf, sems = rest
    i = pl.program_id(0)
    n = pl.num_programs(0)

    def gather(pos_ref, slot):
        def issue(g, _):
            for u in range(DMA_UNROLL):
                r = g * DMA_UNROLL + u
                for kk in range(TOP_K):
                    pos = pos_ref[kk * tm + r]
                    pltpu.make_async_copy(_token_slice(ys_ref, pos), _token_slice(ybuf.at[slot, kk], r),
                                          sems.at[slot]).start(priority=kk)
            return 0

        lax.fori_loop(0, tm // DMA_UNROLL, issue, 0)

    @pl.when(i == 0)
    def _():
        gather(pos_cur, 0)

    @pl.when(i + 1 < n)
    def _():
        gather(pos_nxt, (i + 1) % 2)

    slot = i % 2
    for kk in range(TOP_K):
        pltpu.make_async_copy(ys_ref.at[pl.ds(0, tm * SUBLANES)], ybuf.at[slot, kk], sems.at[slot]).wait()
    mw = mw_ref[...]
    x3 = (x_ref[...] + mw[:, 0:1] * _from_token_tiles(ybuf.at[slot, 0], tm)
          + mw[:, 1:2] * _from_token_tiles(ybuf.at[slot, 1], tm))
    if final:
        x3 = _rms(x3, g_ref[...])
    o_ref[...] = x3


def _combine(pos, meta_w, x2d, ys, g_final):
    t, d = x2d.shape
    tm = COMBINE_TILE
    nt = t // tm
    final = g_final is not None
    in_specs = [pl.BlockSpec((TOP_K * tm,), lambda i: (i,), memory_space=pltpu.SMEM),
                pl.BlockSpec((TOP_K * tm,), lambda i: (jnp.minimum(i + 1, nt - 1),), memory_space=pltpu.SMEM),
                pl.BlockSpec((tm, LANES), lambda i: (i, 0)),
                pl.BlockSpec((tm, d), lambda i: (i, 0)),
                pl.BlockSpec(memory_space=pl.ANY)]
    pos_flat = _tile_major(pos, tm)
    args = [pos_flat, pos_flat, meta_w, x2d, ys]
    if final:
        in_specs.append(pl.BlockSpec((1, d), lambda i: (0, 0)))
        args.append(g_final.reshape(1, d))
    return pl.pallas_call(
        functools.partial(_combine_kernel, tm=tm, final=final),
        grid=(nt,),
        in_specs=in_specs,
        out_specs=pl.BlockSpec((tm, d), lambda i: (i, 0)),
        scratch_shapes=[pltpu.VMEM((2, TOP_K, tm * SUBLANES, LANES), F32), pltpu.SemaphoreType.DMA((2,))],
        out_shape=jax.ShapeDtypeStruct((t, d), F32),
        compiler_params=_cparams(("arbitrary",)),
        name="moe_combine",
    )(*args)


def _moe_plan(counts, n_tiles):
    tm = EXPERT_TILE
    cnt = counts[0, :N_EXPERTS].astype(I32)
    padded = ((cnt + tm - 1) // tm) * tm
    ends = jnp.cumsum(padded)
    off = ends - padded
    start = jnp.arange(n_tiles, dtype=I32) * tm
    exp = jnp.minimum(jnp.sum((start[:, None] >= ends[None, :]).astype(I32), axis=1), N_EXPERTS - 1)
    n_valid = jnp.clip(off[exp] + cnt[exp] - start, 0, tm)
    active = start < ends[-1]
    n_valid = jnp.where(active, n_valid, 0)
    last = jnp.maximum(ends[-1] // tm - 1, 0)
    blk = jnp.where(active, jnp.arange(n_tiles, dtype=I32), last)
    exp = jnp.where(active, exp, exp[last])
    return off.astype(I32), blk.astype(I32), exp.astype(I32), n_valid.astype(I32)


def _moe(x2, h2t, meta_i, meta_w, counts, w_gu, w_dn, layer, g_final):
    b, s, d = x2.shape
    assert d == SUBLANES * LANES
    t = b * s
    n_rows = t * TOP_K + N_EXPERTS * EXPERT_TILE
    off, blk, exp, n_valid = _moe_plan(counts, n_rows // EXPERT_TILE)
    pos = _sorted_positions(off, meta_i)
    xs = _dispatch(pos, h2t, n_rows)
    ys = _experts(blk, exp, n_valid, xs, w_gu, w_dn, layer)
    out = _combine(pos, meta_w, x2.reshape(t, d), ys, g_final)
    return out.reshape(b, s, d)


def _router_weights(w_group, b_group, w_router, b_router):
    d = w_group.shape[0]
    pad = LANES - N_EXPERTS - N_GROUPS
    w = jnp.concatenate([w_router, w_group, jnp.zeros((d, pad), F32)], axis=1)
    bias = jnp.concatenate([b_router, b_group, jnp.zeros((pad,), F32)]).reshape(1, LANES)
    w_hi = w.astype(BF16)
    w_lo = (w - w_hi.astype(F32)).astype(BF16)
    return jnp.stack([w_hi, w_lo]), bias


def kernel(x, mem, mem_norm, mem_w_kv, norm_mix, norm_xattn, norm_ffn, hyb_w_in, hyb_conv_w, diff_lambda, diff_subln, hyb_w_out, swa_w_qkv, swa_b_qkv, swa_sinks, swa_w_out, swa_b_out, xattn_w_q, xattn_w_o, moe_w_group, moe_b_group, moe_w_router, moe_b_router, moe_w_gate_up, moe_w_down, final_norm):
    b, s, d = x.shape
    m = mem.shape[1]
    depth = norm_mix.shape[0]
    mkv = _norm_proj(mem.reshape(b * m, d), mem_norm, mem_w_kv.astype(BF16), m).reshape(b, m, -1)

    scale = DIFF_DK ** -0.5
    cq, suq, sdq, half = _rope_lane_tables(s, DIFF_DK, scale)
    ck, suk, sdk, _ = _rope_lane_tables(s, DIFF_DK, 1.0)
    tabs_q, tabs_k = (cq, suq, sdq), (ck, suk, sdk)

    for l in range(depth):
        if l % 2 == 0:
            e = l // 2
            lambda_init = 0.8 - 0.6 * math.exp(-0.3 * l)
            ya, q, k, v = _hyb_front(x, norm_mix[l], hyb_w_in[e].astype(BF16), hyb_conv_w[e], tabs_q, tabs_k, half)
            o = _diff_attn(q, k, v, diff_lambda[e], diff_subln[e], lambda_init)
            w_out = hyb_w_out[e].astype(BF16)
            a_list, wo_list, b_out = [ya, o], [w_out[:CONV_CH], w_out[CONV_CH:]], None
        else:
            e = l // 2
            q, kv = _swa_front(x, norm_mix[l], swa_w_qkv[e].astype(BF16), swa_b_qkv[e], tabs_q, tabs_k, half)
            o = _swa_attn(q, kv, swa_sinks[e])
            a_list, wo_list, b_out = [o], [swa_w_out[e].astype(BF16)], swa_b_out[e]
        w_rt, b_rt = _router_weights(moe_w_group[l], moe_b_group[l], moe_w_router[l], moe_b_router[l])
        x2, h2, meta_i, meta_w, counts = _post_mixer(
            x, a_list, wo_list, b_out, norm_xattn[l], xattn_w_q[l].astype(BF16), mkv,
            xattn_w_o[l].astype(BF16), norm_ffn[l], w_rt, b_rt)
        x = _moe(x2, h2, meta_i, meta_w, counts, moe_w_gate_up, moe_w_down, l,
                 final_norm if l == depth - 1 else None)
    return x
```

```python
import functools
import math

import jax
import jax.numpy as jnp
from jax import lax
from jax.experimental import pallas as pl
from jax.experimental.pallas import tpu as pltpu

F32 = jnp.float32
BF16 = jnp.bfloat16
I32 = jnp.int32

EPS = 1e-6
LANES = 128
SUBLANES = 8
DMA_UNROLL = 8
VMEM_LIMIT = 56 * 1024 * 1024

ROPE_THETA = 500000.0
ROPE_FRACTION = 4
BLOCK = 128
CONV_CH = 512
CONV_K = 3
DIFF_HEADS = 4
DIFF_DK = 64
DIFF_DV = 128
SWA_Q_HEADS = 16
SWA_KV_HEADS = 4
SWA_HEAD_DIM = 64
SWA_WINDOW = 128
XATTN_HEADS = 4
XATTN_HEAD_DIM = 128
N_GROUPS = 4
EXPERTS_PER_GROUP = 8
N_EXPERTS = N_GROUPS * EXPERTS_PER_GROUP
TOP_K = 2
EXPERT_HIDDEN = 512

ROW_TILE = 512
EXPERT_TILE = 512
DISPATCH_TILE = 1024
COMBINE_TILE = 512
NEG_INF = float("-inf")


def _cparams(sem):
    return pltpu.CompilerParams(dimension_semantics=sem, vmem_limit_bytes=VMEM_LIMIT)


def _rms(x, g):
    return x * lax.rsqrt(jnp.mean(x * x, axis=-1, keepdims=True) + EPS) * g


def _dot(a, b):
    return jnp.dot(a, b, preferred_element_type=F32)


def _dot_nt(a, b):
    return lax.dot_general(a, b, (((1,), (1,)), ((), ())), preferred_element_type=F32)


def _rope_lane_tables(seq, head_dim, scale):
    rot = head_dim // ROPE_FRACTION
    half = rot // 2
    pos = jnp.arange(seq, dtype=F32)
    inv = ROPE_THETA ** (-jnp.arange(0, rot, 2, dtype=F32) / rot)
    ang = pos[:, None] * inv[None, :]
    cos, sin = jnp.cos(ang), jnp.sin(ang)
    idx = jnp.arange(LANES) % head_dim
    cl = jnp.take(cos, idx % half, axis=1)
    sl = jnp.take(sin, idx % half, axis=1)
    c = jnp.where(idx < rot, cl, 1.0) * scale
    s_up = jnp.where(idx < half, -sl, 0.0) * scale
    s_dn = jnp.where((idx >= half) & (idx < rot), sl, 0.0) * scale
    return c.astype(F32), s_up.astype(F32), s_dn.astype(F32), half


def _rope_chunk(xc, c, s_up, s_dn, half):
    return (xc * c + pltpu.roll(xc, LANES - half, 1) * s_up + pltpu.roll(xc, half, 1) * s_dn)


def _norm_proj_kernel(x_ref, g_ref, w_ref, o_ref):
    h = _rms(x_ref[...], g_ref[...]).astype(BF16)
    o_ref[...] = _dot(h, w_ref[...]).astype(o_ref.dtype)


def _norm_proj(x2d, g, w_bf16, tm):
    m, d = x2d.shape
    n = w_bf16.shape[1]
    return pl.pallas_call(
        _norm_proj_kernel,
        grid=(m // tm,),
        in_specs=[pl.BlockSpec((tm, d), lambda i: (i, 0)),
                  pl.BlockSpec((1, d), lambda i: (0, 0)),
                  pl.BlockSpec((d, n), lambda i: (0, 0))],
        out_specs=pl.BlockSpec((tm, n), lambda i: (i, 0)),
        out_shape=jax.ShapeDtypeStruct((m, n), BF16),
        compiler_params=_cparams(("arbitrary",)),
        name="mem_kv_proj",
    )(x2d, g.reshape(1, d), w_bf16)


def _hyb_front_kernel(x_ref, g_ref, w_ref, cw_ref, cq_ref, suq_ref, sdq_ref, ck_ref, suk_ref, sdk_ref,
                      ya_ref, q_ref, k_ref, v_ref, cbuf, *, tm, half):
    s = pl.program_id(1)
    h = _rms(x_ref[0], g_ref[...]).astype(BF16)
    p = _dot(h, w_ref[...])
    c = CONV_CH
    gate_b, gate_c, u = p[:, 0:c], p[:, c:2 * c], p[:, 2 * c:3 * c]
    cu = gate_c * u

    @pl.when(s == 0)
    def _():
        cbuf[0:8, :] = jnp.zeros((8, c), F32)

    cbuf[8:8 + tm, :] = cu
    cw = cw_ref[...]
    conv = (cw[0:1, :] * cbuf[6:6 + tm, :] + cw[1:2, :] * cbuf[7:7 + tm, :] + cw[2:3, :] * cu)
    ya_ref[0] = (gate_b * conv).astype(BF16)
    cbuf[0:8, :] = cbuf[tm:tm + 8, :]

    base = 3 * c
    nq = DIFF_HEADS * 2 * DIFF_DK
    cq, suq, sdq = cq_ref[...], suq_ref[...], sdq_ref[...]
    ck, suk, sdk = ck_ref[...], suk_ref[...], sdk_ref[...]
    for j in range(nq // LANES):
        lo = base + j * LANES
        q_ref[0, :, j * LANES:(j + 1) * LANES] = _rope_chunk(p[:, lo:lo + LANES], cq, suq, sdq, half).astype(BF16)
        lo = base + nq + j * LANES
        k_ref[0, :, j * LANES:(j + 1) * LANES] = _rope_chunk(p[:, lo:lo + LANES], ck, suk, sdk, half).astype(BF16)
    v_ref[0] = p[:, base + 2 * nq:].astype(BF16)


def _hyb_front(x, g, w_in, conv_w, tabs_q, tabs_k, half):
    b, s, d = x.shape
    tm = ROW_TILE
    n = w_in.shape[1]
    nq = DIFF_HEADS * 2 * DIFF_DK
    nv = DIFF_HEADS * DIFF_DV
    tab_spec = pl.BlockSpec((tm, LANES), lambda bi, si: (si, 0))
    row = lambda width: pl.BlockSpec((1, tm, width), lambda bi, si: (bi, si, 0))
    return pl.pallas_call(
        functools.partial(_hyb_front_kernel, tm=tm, half=half),
        grid=(b, s // tm),
        in_specs=[row(d),
                  pl.BlockSpec((1, d), lambda bi, si: (0, 0)),
                  pl.BlockSpec((d, n), lambda bi, si: (0, 0)),
                  pl.BlockSpec((CONV_K, CONV_CH), lambda bi, si: (0, 0)),
                  tab_spec, tab_spec, tab_spec, tab_spec, tab_spec, tab_spec],
        out_specs=[row(CONV_CH), row(nq), row(nq), row(nv)],
        out_shape=[jax.ShapeDtypeStruct((b, s, CONV_CH), BF16),
                   jax.ShapeDtypeStruct((b, s, nq), BF16),
                   jax.ShapeDtypeStruct((b, s, nq), BF16),
                   jax.ShapeDtypeStruct((b, s, nv), BF16)],
        scratch_shapes=[pltpu.VMEM((tm + 8, CONV_CH), F32)],
        compiler_params=_cparams(("arbitrary", "arbitrary")),
        name="hyb_front",
    )(x, g.reshape(1, d), w_in, conv_w, *tabs_q, *tabs_k)


def _lane_fold(x, op):
    r = x[:, 0:LANES]
    for c in range(1, x.shape[1] // LANES):
        r = op(r, x[:, c * LANES:(c + 1) * LANES])
    return r


def _diff_attn_kernel(q_ref, k_ref, v_ref, lam_ref, g_ref, o_ref, sbuf, stat, acc, *, tq, hp, lambda_init):
    i = pl.program_id(2)
    lane = lax.broadcasted_iota(I32, (1, LANES), 1)
    qs = []
    for h in range(hp):
        q = q_ref[0, :, h * LANES:(h + 1) * LANES]
        zero = jnp.zeros_like(q)
        qs += [jnp.where(lane < DIFF_DK, q, zero), jnp.where(lane >= DIFF_DK, q, zero)]
    nc = 2 * hp

    def scores(j, masked):
        for c in range(nc):
            h = c // 2
            kb = k_ref[0, pl.ds(pl.multiple_of(j * tq, tq), tq), h * LANES:(h + 1) * LANES]
            sc = _dot_nt(qs[c], kb)
            if masked:
                r = lax.broadcasted_iota(I32, (tq, tq), 0)
                cc = lax.broadcasted_iota(I32, (tq, tq), 1)
                sc = jnp.where(cc <= r, sc, NEG_INF)
            sbuf[c, j] = sc
            stat[c] = jnp.maximum(stat[c], _lane_fold(sc, jnp.maximum))

    stat[...] = jnp.full(stat.shape, NEG_INF, F32)

    def pass1(j, carry):
        scores(j, False)
        return carry

    lax.fori_loop(0, i, pass1, 0)
    scores(i, True)
    ms = [jnp.max(stat[c], axis=-1, keepdims=True) for c in range(nc)]

    acc[...] = jnp.zeros_like(acc)
    stat[...] = jnp.zeros(stat.shape, F32)

    def pass2(j, carry):
        for h in range(hp):
            vb = v_ref[0, pl.ds(pl.multiple_of(j * tq, tq), tq), h * LANES:(h + 1) * LANES]
            p0 = jnp.exp(sbuf[2 * h, j] - ms[2 * h])
            p1 = jnp.exp(sbuf[2 * h + 1, j] - ms[2 * h + 1])
            acc[h] += _dot(jnp.concatenate([p0, p1], axis=0).astype(BF16), vb)
            stat[2 * h] += _lane_fold(p0, jnp.add)
            stat[2 * h + 1] += _lane_fold(p1, jnp.add)
        return carry

    lax.fori_loop(0, i + 1, pass2, 0)

    lf = lam_ref[...]
    lam = (jnp.exp(jnp.sum(lf[0:1] * lf[1:2], keepdims=True))
           - jnp.exp(jnp.sum(lf[2:3] * lf[3:4], keepdims=True)) + lambda_init)
    for h in range(hp):
        l0 = jnp.sum(stat[2 * h], axis=-1, keepdims=True)
        l1 = jnp.sum(stat[2 * h + 1], axis=-1, keepdims=True)
        o = acc[h, 0:tq, :] / l0 - lam * (acc[h, tq:2 * tq, :] / l1)
        o_ref[0, :, h * LANES:(h + 1) * LANES] = (_rms(o, g_ref[...]) * (1.0 - lambda_init)).astype(BF16)


def _diff_attn(q, k, v, lam_vecs, subln_g, lambda_init):
    b, s, _ = q.shape
    tq = 256
    hp = 4
    blk = lambda bi, hi, qi: (bi, qi, hi)
    full = lambda bi, hi, qi: (bi, 0, hi)
    return pl.pallas_call(
        functools.partial(_diff_attn_kernel, tq=tq, hp=hp, lambda_init=lambda_init),
        grid=(b, DIFF_HEADS // hp, s // tq),
        in_specs=[pl.BlockSpec((1, tq, hp * LANES), blk),
                  pl.BlockSpec((1, s, hp * LANES), full),
                  pl.BlockSpec((1, s, hp * LANES), full),
                  pl.BlockSpec((4, DIFF_DK), lambda bi, hi, qi: (0, 0)),
                  pl.BlockSpec((1, DIFF_DV), lambda bi, hi, qi: (0, 0))],
        out_specs=pl.BlockSpec((1, tq, hp * LANES), blk),
        out_shape=jax.ShapeDtypeStruct((b, s, DIFF_HEADS * DIFF_DV), BF16),
        scratch_shapes=[pltpu.VMEM((2 * hp, s // tq, tq, tq), F32),
                        pltpu.VMEM((2 * hp, tq, LANES), F32),
                        pltpu.VMEM((hp, 2 * tq, DIFF_DV), F32)],
        compiler_params=_cparams(("arbitrary", "arbitrary", "arbitrary")),
        name="diff_attn",
    )(q, k, v, lam_vecs, subln_g.reshape(1, DIFF_DV))


def _swa_front_kernel(x_ref, g_ref, w_ref, b_ref, cq_ref, suq_ref, sdq_ref, ck_ref, suk_ref, sdk_ref,
                      q_ref, kv_ref, *, half):
    h = _rms(x_ref[0], g_ref[...]).astype(BF16)
    p = _dot(h, w_ref[...]) + b_ref[...]
    nq = SWA_Q_HEADS * SWA_HEAD_DIM
    nkv = SWA_KV_HEADS * SWA_HEAD_DIM
    cq, suq, sdq = cq_ref[...], suq_ref[...], sdq_ref[...]
    ck, suk, sdk = ck_ref[...], suk_ref[...], sdk_ref[...]
    for j in range(nq // LANES):
        lo = j * LANES
        q_ref[0, :, lo:lo + LANES] = _rope_chunk(p[:, lo:lo + LANES], cq, suq, sdq, half).astype(BF16)
    for j in range(nkv // LANES):
        lo = j * LANES
        kv_ref[0, :, lo:lo + LANES] = _rope_chunk(p[:, nq + lo:nq + lo + LANES], ck, suk, sdk, half).astype(BF16)
    kv_ref[0, :, nkv:] = p[:, nq + nkv:].astype(BF16)


def _swa_front(x, g, w_qkv, b_qkv, tabs_q, tabs_k, half):
    b, s, d = x.shape
    tm = ROW_TILE
    n = w_qkv.shape[1]
    nq = SWA_Q_HEADS * SWA_HEAD_DIM
    nkv = SWA_KV_HEADS * SWA_HEAD_DIM
    tab_spec = pl.BlockSpec((tm, LANES), lambda bi, si: (si, 0))
    row = lambda width: pl.BlockSpec((1, tm, width), lambda bi, si: (bi, si, 0))
    return pl.pallas_call(
        functools.partial(_swa_front_kernel, half=half),
        grid=(b, s // tm),
        in_specs=[row(d),
                  pl.BlockSpec((1, d), lambda bi, si: (0, 0)),
                  pl.BlockSpec((d, n), lambda bi, si: (0, 0)),
                  pl.BlockSpec((1, n), lambda bi, si: (0, 0)),
                  tab_spec, tab_spec, tab_spec, tab_spec, tab_spec, tab_spec],
        out_specs=[row(nq), row(2 * nkv)],
        out_shape=[jax.ShapeDtypeStruct((b, s, nq), BF16),
                   jax.ShapeDtypeStruct((b, s, 2 * nkv), BF16)],
        compiler_params=_cparams(("arbitrary", "arbitrary")),
        name="swa_front",
    )(x, g.reshape(1, d), w_qkv, b_qkv.reshape(1, n), *tabs_q, *tabs_k)


def _swa_head_order():
    g_sz = SWA_Q_HEADS // SWA_KV_HEADS
    order = []
    for slab in range(SWA_Q_HEADS // 2):
        pair, j = slab // g_sz, slab % g_sz
        order += [(2 * pair) * g_sz + j, (2 * pair + 1) * g_sz + j]
    return order


def _swa_attn_kernel(sink_ref, q_ref, kvp_ref, kvc_ref, o_ref, *, nblk):
    i = pl.program_id(1)
    hd = SWA_HEAD_DIM
    nkv = SWA_KV_HEADS * hd
    g_sz = SWA_Q_HEADS // SWA_KV_HEADS
    kv = jnp.concatenate([kvp_ref[0], kvc_ref[0]], axis=0)
    r = lax.broadcasted_iota(I32, (BLOCK, 2 * BLOCK), 0)
    c = lax.broadcasted_iota(I32, (BLOCK, 2 * BLOCK), 1)
    rel = c - BLOCK - r
    in_win = (rel <= 0) & (rel > -SWA_WINDOW)
    lane = lax.broadcasted_iota(I32, (1, LANES), 1)
    lo_half = lane < hd
    ones = jnp.ones((2 * BLOCK, LANES), BF16)
    for n in range(nblk):
        mask = in_win & ((c >= BLOCK) | (i > 0)) if n == 0 else in_win
        keys = kv[n * BLOCK:(n + 2) * BLOCK]
        for pair in range(SWA_KV_HEADS // 2):
            k2 = keys[:, pair * LANES:(pair + 1) * LANES]
            v2 = keys[:, nkv + pair * LANES:nkv + (pair + 1) * LANES]
            v_ext = jnp.concatenate([v2, ones], axis=1)
            pieces = []
            for j in range(g_sz):
                slab = pair * g_sz + j
                qs = q_ref[0, n * BLOCK:(n + 1) * BLOCK, slab * LANES:(slab + 1) * LANES]
                zero = jnp.zeros_like(qs)
                pieces += [jnp.where(lo_half, qs, zero), jnp.where(lo_half, zero, qs)]
            sc = _dot_nt(jnp.concatenate(pieces, axis=0), k2)
            probs, tails = [], []
            for pc in range(2 * g_sz):
                scp = jnp.where(mask, sc[pc * BLOCK:(pc + 1) * BLOCK], NEG_INF)
                sink = sink_ref[2 * g_sz * pair + pc]
                m = jnp.maximum(jnp.max(scp, axis=-1, keepdims=True), sink)
                probs.append(jnp.exp(scp - m).astype(BF16))
                tails.append(jnp.exp(sink - m))
            pv = _dot(jnp.concatenate(probs, axis=0), v_ext)
            for j in range(g_sz):
                slab = pair * g_sz + j
                halves = []
                for hf in range(2):
                    pc = 2 * j + hf
                    blk = pv[pc * BLOCK:(pc + 1) * BLOCK]
                    halves.append(blk[:, :LANES] / (blk[:, LANES:] + tails[pc]))
                o_ref[0, n * BLOCK:(n + 1) * BLOCK, slab * LANES:(slab + 1) * LANES] = (
                    jnp.where(lo_half, halves[0], halves[1]).astype(BF16))


def _swa_attn(q, kv, sinks):
    b, s, nq = q.shape
    nblk = 2
    tq = nblk * BLOCK
    return pl.pallas_call(
        functools.partial(_swa_attn_kernel, nblk=nblk),
        grid_spec=pltpu.PrefetchScalarGridSpec(
            num_scalar_prefetch=1,
            grid=(b, s // tq),
            in_specs=[pl.BlockSpec((1, tq, nq), lambda bi, ni, sk: (bi, ni, 0)),
                      pl.BlockSpec((1, BLOCK, kv.shape[2]), lambda bi, ni, sk: (bi, jnp.maximum(ni * nblk - 1, 0), 0)),
                      pl.BlockSpec((1, tq, kv.shape[2]), lambda bi, ni, sk: (bi, ni, 0))],
            out_specs=pl.BlockSpec((1, tq, nq), lambda bi, ni, sk: (bi, ni, 0))),
        out_shape=jax.ShapeDtypeStruct((b, s, nq), BF16),
        compiler_params=_cparams(("arbitrary", "arbitrary")),
        name="swa_attn",
    )(sinks, q, kv, kv)


def _post_kernel(*refs, tm, n_a, has_bias):
    x_ref = refs[0]
    a_refs = refs[1:1 + n_a]
    k = 1 + n_a
    wo_refs = refs[k:k + n_a]
    k += n_a
    if has_bias:
        bo_ref = refs[k]
        k += 1
    (gx_ref, wq_ref, mkv_ref, wxo_ref, gf_ref, wr_ref, br_ref,
     x2_ref, h2_ref, mi_ref, mw_ref, cnt_ref, cnt_acc) = refs[k:]
    first = (pl.program_id(0) == 0) & (pl.program_id(1) == 0)

    @pl.when(first)
    def _():
        cnt_acc[...] = jnp.zeros_like(cnt_acc)

    acc = _dot(a_refs[0][0], wo_refs[0][...])
    for a_ref, w_ref in zip(a_refs[1:], wo_refs[1:]):
        acc = acc + _dot(a_ref[0], w_ref[...])
    if has_bias:
        acc = acc + bo_ref[...]
    x1 = x_ref[0] + acc

    hx = _rms(x1, gx_ref[...]).astype(BF16)
    qx = (_dot(hx, wq_ref[...]) * (XATTN_HEAD_DIM ** -0.5)).astype(BF16)
    xw = XATTN_HEADS * XATTN_HEAD_DIM
    outs = []
    for hh in range(XATTN_HEADS):
        lo = hh * XATTN_HEAD_DIM
        mk = mkv_ref[0, :, lo:lo + XATTN_HEAD_DIM]
        mv = mkv_ref[0, :, xw + lo:xw + lo + XATTN_HEAD_DIM]
        sc = _dot_nt(qx[:, lo:lo + XATTN_HEAD_DIM], mk)
        m = jnp.max(sc, axis=-1, keepdims=True)
        pr = jnp.exp(sc - m)
        den = jnp.sum(pr, axis=-1, keepdims=True)
        outs.append((_dot(pr.astype(BF16), mv) / den).astype(BF16))
    ox = jnp.concatenate(outs, axis=-1)
    x2 = x1 + _dot(ox, wxo_ref[...])
    x2_ref[0] = x2

    h2 = _rms(x2, gf_ref[...])
    _to_token_tiles(h2_ref, h2, tm)
    h_hi = h2.astype(BF16)
    h_lo = (h2 - h_hi.astype(F32)).astype(BF16)
    logits = (_dot(h_hi, wr_ref[0]) + _dot(h_lo, wr_ref[0]) + _dot(h_hi, wr_ref[1])
              + br_ref[...])
    lane = lax.broadcasted_iota(I32, (tm, LANES), 1).astype(F32)
    big = float(LANES)
    g_lo = float(N_EXPERTS)
    lg = jnp.where((lane >= g_lo) & (lane < g_lo + N_GROUPS), logits, NEG_INF)
    mg = jnp.max(lg, axis=-1, keepdims=True)
    g_lane = jnp.min(jnp.where(lg == mg, lane, big), axis=-1, keepdims=True)
    p_g = 1.0 / jnp.sum(jnp.exp(lg - mg), axis=-1, keepdims=True)
    e_lo = (g_lane - g_lo) * EXPERTS_PER_GROUP
    le = jnp.where((lane >= e_lo) & (lane < e_lo + EXPERTS_PER_GROUP), logits, NEG_INF)
    m1 = jnp.max(le, axis=-1, keepdims=True)
    i1 = jnp.min(jnp.where(le == m1, lane, big), axis=-1, keepdims=True)
    le2 = jnp.where(lane == i1, NEG_INF, le)
    m2 = jnp.max(le2, axis=-1, keepdims=True)
    i2 = jnp.min(jnp.where(le2 == m2, lane, big), axis=-1, keepdims=True)
    t = jnp.exp(m2 - m1)
    w1 = p_g / (1.0 + t)
    w2 = p_g * t / (1.0 + t)

    oh1 = lane == i1
    oh2 = lane == i2
    oh = jnp.where(oh1 | oh2, 1.0, 0.0)
    rr = lax.broadcasted_iota(I32, (tm, tm), 0)
    cc = lax.broadcasted_iota(I32, (tm, tm), 1)
    tri = jnp.where(cc < rr, 1.0, 0.0).astype(BF16)
    before = _dot(tri, oh.astype(BF16)) + cnt_acc[...]
    r1 = jnp.sum(jnp.where(oh1, before, 0.0), axis=-1, keepdims=True)
    r2 = jnp.sum(jnp.where(oh2, before, 0.0), axis=-1, keepdims=True)
    cnt_new = cnt_acc[...] + jnp.sum(oh, axis=0, keepdims=True)
    cnt_acc[...] = cnt_new
    cnt_ref[...] = cnt_new

    mw_ref[...] = jnp.where(lane == 0, w1, jnp.where(lane == 1, w2, 0.0))
    ints = jnp.where(lane == 0, i1,
                     jnp.where(lane == 1, i2,
                               jnp.where(lane == 2, r1, jnp.where(lane == 3, r2, 0.0))))
    mi_ref[...] = ints.T[0:8, :].astype(I32)


def _post_mixer(x, a_list, wo_list, b_out, g_x, w_q, mkv, w_xo, g_f, w_rt, b_rt):
    b, s, d = x.shape
    tm = ROW_TILE
    ns = s // tm
    t = b * s
    n_a = len(a_list)
    has_bias = b_out is not None
    const2 = lambda bi, si: (0, 0)
    row = lambda width: pl.BlockSpec((1, tm, width), lambda bi, si: (bi, si, 0))
    in_specs = [row(d)] + [row(a.shape[2]) for a in a_list]
    in_specs += [pl.BlockSpec(w.shape, const2) for w in wo_list]
    args = [x, *a_list, *wo_list]
    if has_bias:
        in_specs.append(pl.BlockSpec((1, d), const2))
        args.append(b_out.reshape(1, d))
    xw = w_q.shape[1]
    in_specs += [pl.BlockSpec((1, d), const2),
                 pl.BlockSpec((d, xw), const2),
                 pl.BlockSpec((1, mkv.shape[1], mkv.shape[2]), lambda bi, si: (bi, 0, 0)),
                 pl.BlockSpec((xw, d), const2),
                 pl.BlockSpec((1, d), const2),
                 pl.BlockSpec((2, d, LANES), lambda bi, si: (0, 0, 0)),
                 pl.BlockSpec((1, LANES), const2)]
    args += [g_x.reshape(1, d), w_q, mkv, w_xo, g_f.reshape(1, d), w_rt, b_rt]
    out_specs = [row(d),
                 pl.BlockSpec((tm * SUBLANES, LANES), lambda bi, si: (bi * ns + si, 0)),
                 pl.BlockSpec((8, tm), lambda bi, si: (0, bi * ns + si)),
                 pl.BlockSpec((tm, LANES), lambda bi, si: (bi * ns + si, 0)),
                 pl.BlockSpec((1, LANES), const2)]
    out_shape = [jax.ShapeDtypeStruct((b, s, d), F32),
                 jax.ShapeDtypeStruct((t * SUBLANES, LANES), F32),
                 jax.ShapeDtypeStruct((8, t), I32),
                 jax.ShapeDtypeStruct((t, LANES), F32),
                 jax.ShapeDtypeStruct((1, LANES), F32)]
    return pl.pallas_call(
        functools.partial(_post_kernel, tm=tm, n_a=n_a, has_bias=has_bias),
        grid=(b, ns),
        in_specs=in_specs,
        out_specs=out_specs,
        out_shape=out_shape,
        scratch_shapes=[pltpu.VMEM((1, LANES), F32)],
        compiler_params=_cparams(("arbitrary", "arbitrary")),
        name="post_mixer",
    )(*args)


def _to_token_tiles(dst_ref, val, rows):
    for j in range(SUBLANES):
        dst_ref[pl.ds(j, rows, stride=SUBLANES), :] = val[:, j * LANES:(j + 1) * LANES]


def _from_token_tiles(src_ref, rows):
    return jnp.concatenate([src_ref[pl.ds(j, rows, stride=SUBLANES), :] for j in range(SUBLANES)], axis=1)


def _token_slice(ref, tok):
    return ref.at[pl.ds(pl.multiple_of(tok * SUBLANES, SUBLANES), SUBLANES)]


def _pos_kernel(off_ref, mi_ref, pos_ref):
    e = mi_ref[0:TOP_K, :]
    pos = mi_ref[TOP_K:2 * TOP_K, :]
    for j in range(N_EXPERTS):
        pos = pos + jnp.where(e == j, off_ref[j], 0)
    pos_ref[...] = pos


def _sorted_positions(off, meta_i):
    t = meta_i.shape[1]
    return pl.pallas_call(
        _pos_kernel,
        grid_spec=pltpu.PrefetchScalarGridSpec(
            num_scalar_prefetch=1,
            grid=(1,),
            in_specs=[pl.BlockSpec((8, t), lambda i, off: (0, 0))],
            out_specs=pl.BlockSpec((TOP_K, t), lambda i, off: (0, 0))),
        out_shape=jax.ShapeDtypeStruct((TOP_K, t), I32),
        compiler_params=_cparams(("arbitrary",)),
        name="moe_positions",
    )(off, meta_i)


def _tile_major(pos, tm):
    t = pos.shape[1]
    return pos.reshape(TOP_K, t // tm, tm).transpose(1, 0, 2).reshape(-1)


def _dispatch_kernel(pos_ref, h_ref, xs_ref, sem, *, tm):
    def issue(g, _):
        for u in range(DMA_UNROLL):
            r = g * DMA_UNROLL + u
            for kk in range(TOP_K):
                pos = pos_ref[kk * tm + r]
                pltpu.make_async_copy(_token_slice(h_ref, r), _token_slice(xs_ref, pos), sem).start(priority=kk)
        return 0

    lax.fori_loop(0, tm // DMA_UNROLL, issue, 0)
    for _ in range(TOP_K):
        pltpu.make_async_copy(h_ref, xs_ref.at[pl.ds(0, tm * SUBLANES)], sem).wait()


def _dispatch(pos, h2t, n_rows):
    tm = DISPATCH_TILE
    t = h2t.shape[0] // SUBLANES
    return pl.pallas_call(
        functools.partial(_dispatch_kernel, tm=tm),
        grid=(t // tm,),
        in_specs=[pl.BlockSpec((TOP_K * tm,), lambda i: (i,), memory_space=pltpu.SMEM),
                  pl.BlockSpec((tm * SUBLANES, LANES), lambda i: (i, 0))],
        out_specs=pl.BlockSpec(memory_space=pl.ANY),
        scratch_shapes=[pltpu.SemaphoreType.DMA],
        out_shape=jax.ShapeDtypeStruct((n_rows * SUBLANES, LANES), F32),
        compiler_params=_cparams(("arbitrary",)),
        name="moe_dispatch",
    )(_tile_major(pos, tm), h2t)


def _expert_kernel(tblk_ref, texp_ref, tn_ref, xs_ref, wgu_ref, wdn_ref, ys_ref, wgu_bf, wdn_bf, *, tm):
    i = pl.program_id(0)
    n_valid = tn_ref[i]
    new_expert = (i == 0) | (texp_ref[i] != texp_ref[jnp.maximum(i - 1, 0)])

    @pl.when(new_expert)
    def _():
        wgu_bf[...] = wgu_ref[0, 0].astype(BF16)
        wdn_bf[...] = wdn_ref[0, 0].astype(BF16)

    @pl.when(n_valid > 0)
    def _():
        row = lax.broadcasted_iota(I32, (tm, 1), 0)
        xs = jnp.where(row < n_valid, _from_token_tiles(xs_ref, tm), 0.0).astype(BF16)
        gu = _dot(xs, wgu_bf[...])
        gate, up = gu[:, :EXPERT_HIDDEN], gu[:, EXPERT_HIDDEN:]
        hid = (gate / (1.0 + jnp.exp(-gate)) * up).astype(BF16)
        _to_token_tiles(ys_ref, _dot(hid, wdn_bf[...]), tm)


def _experts(tile_blk, tile_exp, tile_n, xs, w_gu, w_dn, layer):
    tm = EXPERT_TILE
    nt = xs.shape[0] // (tm * SUBLANES)
    gu_shape, dn_shape = w_gu.shape[2:], w_dn.shape[2:]
    return pl.pallas_call(
        functools.partial(_expert_kernel, tm=tm),
        grid_spec=pltpu.PrefetchScalarGridSpec(
            num_scalar_prefetch=3,
            grid=(nt,),
            in_specs=[pl.BlockSpec((tm * SUBLANES, LANES), lambda i, tb, te, tn: (tb[i], 0)),
                      pl.BlockSpec((1, 1) + gu_shape, lambda i, tb, te, tn: (layer, te[i], 0, 0)),
                      pl.BlockSpec((1, 1) + dn_shape, lambda i, tb, te, tn: (layer, te[i], 0, 0))],
            out_specs=pl.BlockSpec((tm * SUBLANES, LANES), lambda i, tb, te, tn: (tb[i], 0)),
            scratch_shapes=[pltpu.VMEM(gu_shape, BF16), pltpu.VMEM(dn_shape, BF16)]),
        out_shape=jax.ShapeDtypeStruct(xs.shape, F32),
        compiler_params=_cparams(("arbitrary",)),
        name="moe_experts",
    )(tile_blk, tile_exp, tile_n, xs, w_gu, w_dn)


def _combine_kernel(pos_cur, pos_nxt, mw_ref, x_ref, ys_ref, *rest, tm, final):
    if final:
        g_ref, o_ref, ybuf, sems = rest
    else:
        o_ref, ybuf, sems = rest
    i = pl.program_id(0)
    n = pl.num_programs(0)

    def gather(pos_ref, slot):
        def issue(g, _):
            for u in range(DMA_UNROLL):
                r = g * DMA_UNROLL + u
                for kk in range(TOP_K):
                    pos = pos_ref[kk * tm + r]
                    pltpu.make_async_copy(_token_slice(ys_ref, pos), _token_slice(ybuf.at[slot, kk], r),
                                          sems.at[slot]).start(priority=kk)
            return 0

        lax.fori_loop(0, tm // DMA_UNROLL, issue, 0)

    @pl.when(i == 0)
    def _():
        gather(pos_cur, 0)

    @pl.when(i + 1 < n)
    def _():
        gather(pos_nxt, (i + 1) % 2)

    slot = i % 2
    for kk in range(TOP_K):
        pltpu.make_async_copy(ys_ref.at[pl.ds(0, tm * SUBLANES)], ybuf.at[slot, kk], sems.at[slot]).wait()
    mw = mw_ref[...]
    x3 = (x_ref[...] + mw[:, 0:1] * _from_token_tiles(ybuf.at[slot, 0], tm)
          + mw[:, 1:2] * _from_token_tiles(ybuf.at[slot, 1], tm))
    if final:
        x3 = _rms(x3, g_ref[...])
    o_ref[...] = x3


def _combine(pos, meta_w, x2d, ys, g_final):
    t, d = x2d.shape
    tm = COMBINE_TILE
    nt = t // tm
    final = g_final is not None
    in_specs = [pl.BlockSpec((TOP_K * tm,), lambda i: (i,), memory_space=pltpu.SMEM),
                pl.BlockSpec((TOP_K * tm,), lambda i: (jnp.minimum(i + 1, nt - 1),), memory_space=pltpu.SMEM),
                pl.BlockSpec((tm, LANES), lambda i: (i, 0)),
                pl.BlockSpec((tm, d), lambda i: (i, 0)),
                pl.BlockSpec(memory_space=pl.ANY)]
    pos_flat = _tile_major(pos, tm)
    args = [pos_flat, pos_flat, meta_w, x2d, ys]
    if final:
        in_specs.append(pl.BlockSpec((1, d), lambda i: (0, 0)))
        args.append(g_final.reshape(1, d))
    return pl.pallas_call(
        functools.partial(_combine_kernel, tm=tm, final=final),
        grid=(nt,),
        in_specs=in_specs,
        out_specs=pl.BlockSpec((tm, d), lambda i: (i, 0)),
        scratch_shapes=[pltpu.VMEM((2, TOP_K, tm * SUBLANES, LANES), F32), pltpu.SemaphoreType.DMA((2,))],
        out_shape=jax.ShapeDtypeStruct((t, d), F32),
        compiler_params=_cparams(("arbitrary",)),
        name="moe_combine",
    )(*args)


def _moe_plan(counts, n_tiles):
    tm = EXPERT_TILE
    cnt = counts[0, :N_EXPERTS].astype(I32)
    padded = ((cnt + tm - 1) // tm) * tm
    ends = jnp.cumsum(padded)
    off = ends - padded
    start = jnp.arange(n_tiles, dtype=I32) * tm
    exp = jnp.minimum(jnp.sum((start[:, None] >= ends[None, :]).astype(I32), axis=1), N_EXPERTS - 1)
    n_valid = jnp.clip(off[exp] + cnt[exp] - start, 0, tm)
    active = start < ends[-1]
    n_valid = jnp.where(active, n_valid, 0)
    last = jnp.maximum(ends[-1] // tm - 1, 0)
    blk = jnp.where(active, jnp.arange(n_tiles, dtype=I32), last)
    exp = jnp.where(active, exp, exp[last])
    return off.astype(I32), blk.astype(I32), exp.astype(I32), n_valid.astype(I32)


def _moe(x2, h2t, meta_i, meta_w, counts, w_gu, w_dn, layer, g_final):
    b, s, d = x2.shape
    assert d == SUBLANES * LANES
    t = b * s
    n_rows = t * TOP_K + N_EXPERTS * EXPERT_TILE
    off, blk, exp, n_valid = _moe_plan(counts, n_rows // EXPERT_TILE)
    pos = _sorted_positions(off, meta_i)
    xs = _dispatch(pos, h2t, n_rows)
    ys = _experts(blk, exp, n_valid, xs, w_gu, w_dn, layer)
    out = _combine(pos, meta_w, x2.reshape(t, d), ys, g_final)
    return out.reshape(b, s, d)


def _router_weights(w_group, b_group, w_router, b_router):
    d = w_group.shape[0]
    pad = LANES - N_EXPERTS - N_GROUPS
    w = jnp.concatenate([w_router, w_group, jnp.zeros((d, pad), F32)], axis=1)
    bias = jnp.concatenate([b_router, b_group, jnp.zeros((pad,), F32)]).reshape(1, LANES)
    w_hi = w.astype(BF16)
    w_lo = (w - w_hi.astype(F32)).astype(BF16)
    return jnp.stack([w_hi, w_lo]), bias


def kernel(x, mem, mem_norm, mem_w_kv, norm_mix, norm_xattn, norm_ffn, hyb_w_in, hyb_conv_w, diff_lambda, diff_subln, hyb_w_out, swa_w_qkv, swa_b_qkv, swa_sinks, swa_w_out, swa_b_out, xattn_w_q, xattn_w_o, moe_w_group, moe_b_group, moe_w_router, moe_b_router, moe_w_gate_up, moe_w_down, final_norm):
    b, s, d = x.shape
    m = mem.shape[1]
    depth = norm_mix.shape[0]
    mkv = _norm_proj(mem.reshape(b * m, d), mem_norm, mem_w_kv.astype(BF16), m).reshape(b, m, -1)

    scale = DIFF_DK ** -0.5
    cq, suq, sdq, half = _rope_lane_tables(s, DIFF_DK, scale)
    ck, suk, sdk, _ = _rope_lane_tables(s, DIFF_DK, 1.0)
    tabs_q, tabs_k = (cq, suq, sdq), (ck, suk, sdk)

    for l in range(depth):
        if l % 2 == 0:
            e = l // 2
            lambda_init = 0.8 - 0.6 * math.exp(-0.3 * l)
            ya, q, k, v = _hyb_front(x, norm_mix[l], hyb_w_in[e].astype(BF16), hyb_conv_w[e], tabs_q, tabs_k, half)
            o = _diff_attn(q, k, v, diff_lambda[e], diff_subln[e], lambda_init)
            w_out = hyb_w_out[e].astype(BF16)
            a_list, wo_list, b_out = [ya, o], [w_out[:CONV_CH], w_out[CONV_CH:]], None
        else:
            e = l // 2
            order = jnp.asarray(_swa_head_order(), I32)
            nq = SWA_Q_HEADS * SWA_HEAD_DIM
            cols = (order[:, None] * SWA_HEAD_DIM + jnp.arange(SWA_HEAD_DIM, dtype=I32)[None, :]).reshape(-1)
            cols_all = jnp.concatenate([cols, jnp.arange(nq, swa_w_qkv.shape[2], dtype=I32)])
            w_qkv = jnp.take(swa_w_qkv[e], cols_all, axis=1).astype(BF16)
            b_qkv = jnp.take(swa_b_qkv[e], cols_all)
            q, kv = _swa_front(x, norm_mix[l], w_qkv, b_qkv, tabs_q, tabs_k, half)
            o = _swa_attn(q, kv, jnp.take(swa_sinks[e], order))
            a_list, wo_list, b_out = [o], [jnp.take(swa_w_out[e], cols, axis=0).astype(BF16)], swa_b_out[e]
        w_rt, b_rt = _router_weights(moe_w_group[l], moe_b_group[l], moe_w_router[l], moe_b_router[l])
        x2, h2, meta_i, meta_w, counts = _post_mixer(
            x, a_list, wo_list, b_out, norm_xattn[l], xattn_w_q[l].astype(BF16), mkv,
            xattn_w_o[l].astype(BF16), norm_ffn[l], w_rt, b_rt)
        x = _moe(x2, h2, meta_i, meta_w, counts, moe_w_gate_up, moe_w_down, l,
                 final_norm if l == depth - 1 else None)
    return x
```

```python
import functools
import math

import jax
import jax.numpy as jnp
from jax import lax
from jax.experimental import pallas as pl
from jax.experimental.pallas import tpu as pltpu

F32 = jnp.float32
BF16 = jnp.bfloat16
I32 = jnp.int32

EPS = 1e-6
LANES = 128
SUBLANES = 8
DMA_UNROLL = 8
VMEM_LIMIT = 56 * 1024 * 1024

ROPE_THETA = 500000.0
ROPE_FRACTION = 4
BLOCK = 128
CONV_CH = 512
CONV_K = 3
DIFF_HEADS = 4
DIFF_DK = 64
DIFF_DV = 128
SWA_Q_HEADS = 16
SWA_KV_HEADS = 4
SWA_HEAD_DIM = 64
SWA_WINDOW = 128
XATTN_HEADS = 4
XATTN_HEAD_DIM = 128
N_GROUPS = 4
EXPERTS_PER_GROUP = 8
N_EXPERTS = N_GROUPS * EXPERTS_PER_GROUP
TOP_K = 2
EXPERT_HIDDEN = 512

ROW_TILE = 512
EXPERT_TILE = 512
DISPATCH_TILE = 1024
COMBINE_TILE = 512
NEG_INF = float("-inf")
LOG2E = math.log2(math.e)


def _cparams(sem):
    return pltpu.CompilerParams(dimension_semantics=sem, vmem_limit_bytes=VMEM_LIMIT)


def _rms(x, g):
    return x * lax.rsqrt(jnp.mean(x * x, axis=-1, keepdims=True) + EPS) * g


def _dot(a, b):
    return jnp.dot(a, b, preferred_element_type=F32)


def _dot_nt(a, b):
    return lax.dot_general(a, b, (((1,), (1,)), ((), ())), preferred_element_type=F32)


def _rope_lane_tables(seq, head_dim, scale):
    rot = head_dim // ROPE_FRACTION
    half = rot // 2
    pos = jnp.arange(seq, dtype=F32)
    inv = ROPE_THETA ** (-jnp.arange(0, rot, 2, dtype=F32) / rot)
    ang = pos[:, None] * inv[None, :]
    cos, sin = jnp.cos(ang), jnp.sin(ang)
    idx = jnp.arange(LANES) % head_dim
    cl = jnp.take(cos, idx % half, axis=1)
    sl = jnp.take(sin, idx % half, axis=1)
    c = jnp.where(idx < rot, cl, 1.0) * scale
    s_up = jnp.where(idx < half, -sl, 0.0) * scale
    s_dn = jnp.where((idx >= half) & (idx < rot), sl, 0.0) * scale
    return c.astype(F32), s_up.astype(F32), s_dn.astype(F32), half


def _rope_chunk(xc, c, s_up, s_dn, half):
    return (xc * c + pltpu.roll(xc, LANES - half, 1) * s_up + pltpu.roll(xc, half, 1) * s_dn)


def _norm_proj_kernel(x_ref, g_ref, w_ref, o_ref):
    h = _rms(x_ref[...], g_ref[...]).astype(BF16)
    o_ref[...] = _dot(h, w_ref[...]).astype(o_ref.dtype)


def _norm_proj(x2d, g, w_bf16, tm):
    m, d = x2d.shape
    n = w_bf16.shape[1]
    return pl.pallas_call(
        _norm_proj_kernel,
        grid=(m // tm,),
        in_specs=[pl.BlockSpec((tm, d), lambda i: (i, 0)),
                  pl.BlockSpec((1, d), lambda i: (0, 0)),
                  pl.BlockSpec((d, n), lambda i: (0, 0))],
        out_specs=pl.BlockSpec((tm, n), lambda i: (i, 0)),
        out_shape=jax.ShapeDtypeStruct((m, n), BF16),
        compiler_params=_cparams(("arbitrary",)),
        name="mem_kv_proj",
    )(x2d, g.reshape(1, d), w_bf16)


def _hyb_front_kernel(x_ref, g_ref, w_ref, cw_ref, cq_ref, suq_ref, sdq_ref, ck_ref, suk_ref, sdk_ref,
                      ya_ref, q_ref, k_ref, v_ref, cbuf, *, tm, half):
    s = pl.program_id(1)
    h = _rms(x_ref[0], g_ref[...]).astype(BF16)
    p = _dot(h, w_ref[...])
    c = CONV_CH
    gate_b, gate_c, u = p[:, 0:c], p[:, c:2 * c], p[:, 2 * c:3 * c]
    cu = gate_c * u

    @pl.when(s == 0)
    def _():
        cbuf[0:8, :] = jnp.zeros((8, c), F32)

    cbuf[8:8 + tm, :] = cu
    cw = cw_ref[...]
    conv = (cw[0:1, :] * cbuf[6:6 + tm, :] + cw[1:2, :] * cbuf[7:7 + tm, :] + cw[2:3, :] * cu)
    ya_ref[0] = (gate_b * conv).astype(BF16)
    cbuf[0:8, :] = cbuf[tm:tm + 8, :]

    base = 3 * c
    nq = DIFF_HEADS * 2 * DIFF_DK
    cq, suq, sdq = cq_ref[...], suq_ref[...], sdq_ref[...]
    ck, suk, sdk = ck_ref[...], suk_ref[...], sdk_ref[...]
    for j in range(nq // LANES):
        lo = base + j * LANES
        q_ref[0, :, j * LANES:(j + 1) * LANES] = _rope_chunk(p[:, lo:lo + LANES], cq, suq, sdq, half).astype(BF16)
        lo = base + nq + j * LANES
        k_ref[0, :, j * LANES:(j + 1) * LANES] = _rope_chunk(p[:, lo:lo + LANES], ck, suk, sdk, half).astype(BF16)
    v_ref[0] = p[:, base + 2 * nq:].astype(BF16)


def _hyb_front(x, g, w_in, conv_w, tabs_q, tabs_k, half):
    b, s, d = x.shape
    tm = ROW_TILE
    n = w_in.shape[1]
    nq = DIFF_HEADS * 2 * DIFF_DK
    nv = DIFF_HEADS * DIFF_DV
    tab_spec = pl.BlockSpec((tm, LANES), lambda bi, si: (si, 0))
    row = lambda width: pl.BlockSpec((1, tm, width), lambda bi, si: (bi, si, 0))
    return pl.pallas_call(
        functools.partial(_hyb_front_kernel, tm=tm, half=half),
        grid=(b, s // tm),
        in_specs=[row(d),
                  pl.BlockSpec((1, d), lambda bi, si: (0, 0)),
                  pl.BlockSpec((d, n), lambda bi, si: (0, 0)),
                  pl.BlockSpec((CONV_K, CONV_CH), lambda bi, si: (0, 0)),
                  tab_spec, tab_spec, tab_spec, tab_spec, tab_spec, tab_spec],
        out_specs=[row(CONV_CH), row(nq), row(nq), row(nv)],
        out_shape=[jax.ShapeDtypeStruct((b, s, CONV_CH), BF16),
                   jax.ShapeDtypeStruct((b, s, nq), BF16),
                   jax.ShapeDtypeStruct((b, s, nq), BF16),
                   jax.ShapeDtypeStruct((b, s, nv), BF16)],
        scratch_shapes=[pltpu.VMEM((tm + 8, CONV_CH), F32)],
        compiler_params=_cparams(("arbitrary", "arbitrary")),
        name="hyb_front",
    )(x, g.reshape(1, d), w_in, conv_w, *tabs_q, *tabs_k)


def _lane_fold(x, op):
    r = x[:, 0:LANES]
    for c in range(1, x.shape[1] // LANES):
        r = op(r, x[:, c * LANES:(c + 1) * LANES])
    return r


def _diff_attn_kernel(q_ref, k_ref, v_ref, lam_ref, g_ref, o_ref, sbuf, stat, acc, *, tq, hp, lambda_init):
    i = pl.program_id(2)
    lane = lax.broadcasted_iota(I32, (1, LANES), 1)
    qs = []
    for h in range(hp):
        q = q_ref[0, :, h * LANES:(h + 1) * LANES]
        zero = jnp.zeros_like(q)
        qs += [jnp.where(lane < DIFF_DK, q, zero), jnp.where(lane >= DIFF_DK, q, zero)]
    nc = 2 * hp

    def scores(j, masked):
        for c in range(nc):
            h = c // 2
            kb = k_ref[0, pl.ds(pl.multiple_of(j * tq, tq), tq), h * LANES:(h + 1) * LANES]
            sc = _dot_nt(qs[c], kb)
            if masked:
                r = lax.broadcasted_iota(I32, (tq, tq), 0)
                cc = lax.broadcasted_iota(I32, (tq, tq), 1)
                sc = jnp.where(cc <= r, sc, NEG_INF)
            sbuf[c, j] = sc
            stat[c] = jnp.maximum(stat[c], _lane_fold(sc, jnp.maximum))

    stat[...] = jnp.full(stat.shape, NEG_INF, F32)

    def pass1(j, carry):
        scores(j, False)
        return carry

    lax.fori_loop(0, i, pass1, 0)
    scores(i, True)
    ms = [jnp.max(stat[c], axis=-1, keepdims=True) for c in range(nc)]

    acc[...] = jnp.zeros_like(acc)
    ones = jnp.ones((tq, LANES), BF16)

    def pass2(j, carry):
        for h in range(hp):
            vb = v_ref[0, pl.ds(pl.multiple_of(j * tq, tq), tq), h * LANES:(h + 1) * LANES]
            v_ext = jnp.concatenate([vb, ones], axis=1)
            p0 = jnp.exp2(sbuf[2 * h, j] - ms[2 * h])
            p1 = jnp.exp2(sbuf[2 * h + 1, j] - ms[2 * h + 1])
            acc[h] += _dot(jnp.concatenate([p0, p1], axis=0).astype(BF16), v_ext)
        return carry

    lax.fori_loop(0, i + 1, pass2, 0)

    lf = lam_ref[...]
    lam = (jnp.exp(jnp.sum(lf[0:1] * lf[1:2], keepdims=True))
           - jnp.exp(jnp.sum(lf[2:3] * lf[3:4], keepdims=True)) + lambda_init)
    for h in range(hp):
        a0 = acc[h, 0:tq, :]
        a1 = acc[h, tq:2 * tq, :]
        o = a0[:, :LANES] / a0[:, LANES:] - lam * (a1[:, :LANES] / a1[:, LANES:])
        o_ref[0, :, h * LANES:(h + 1) * LANES] = (_rms(o, g_ref[...]) * (1.0 - lambda_init)).astype(BF16)


def _diff_attn(q, k, v, lam_vecs, subln_g, lambda_init):
    b, s, _ = q.shape
    tq = 256
    hp = 4
    blk = lambda bi, hi, qi: (bi, qi, hi)
    full = lambda bi, hi, qi: (bi, 0, hi)
    return pl.pallas_call(
        functools.partial(_diff_attn_kernel, tq=tq, hp=hp, lambda_init=lambda_init),
        grid=(b, DIFF_HEADS // hp, s // tq),
        in_specs=[pl.BlockSpec((1, tq, hp * LANES), blk),
                  pl.BlockSpec((1, s, hp * LANES), full),
                  pl.BlockSpec((1, s, hp * LANES), full),
                  pl.BlockSpec((4, DIFF_DK), lambda bi, hi, qi: (0, 0)),
                  pl.BlockSpec((1, DIFF_DV), lambda bi, hi, qi: (0, 0))],
        out_specs=pl.BlockSpec((1, tq, hp * LANES), blk),
        out_shape=jax.ShapeDtypeStruct((b, s, DIFF_HEADS * DIFF_DV), BF16),
        scratch_shapes=[pltpu.VMEM((2 * hp, s // tq, tq, tq), F32),
                        pltpu.VMEM((2 * hp, tq, LANES), F32),
                        pltpu.VMEM((hp, 2 * tq, DIFF_DV + LANES), F32)],
        compiler_params=_cparams(("arbitrary", "arbitrary", "arbitrary")),
        name="diff_attn",
    )(q, k, v, lam_vecs, subln_g.reshape(1, DIFF_DV))


def _swa_front_kernel(pos_cur, pos_nxt, mw_ref, x_ref, ys_ref, g_ref, w_ref, b_ref,
                      cq_ref, suq_ref, sdq_ref, ck_ref, suk_ref, sdk_ref,
                      x3_ref, q_ref, kv_ref, ybuf, sems, *, half, tm):
    lin = pl.program_id(0) * pl.num_programs(1) + pl.program_id(1)
    n_tiles = pl.num_programs(0) * pl.num_programs(1)
    x3 = _moe_combine_tile(pos_cur, pos_nxt, mw_ref, x_ref[0], ys_ref, ybuf, sems, lin, n_tiles, tm)
    x3_ref[0] = x3
    h = _rms(x3, g_ref[...]).astype(BF16)
    p = _dot(h, w_ref[...]) + b_ref[...]
    nq = SWA_Q_HEADS * SWA_HEAD_DIM
    nkv = SWA_KV_HEADS * SWA_HEAD_DIM
    cq, suq, sdq = cq_ref[...], suq_ref[...], sdq_ref[...]
    ck, suk, sdk = ck_ref[...], suk_ref[...], sdk_ref[...]
    for j in range(nq // LANES):
        lo = j * LANES
        q_ref[0, :, lo:lo + LANES] = _rope_chunk(p[:, lo:lo + LANES], cq, suq, sdq, half).astype(BF16)
    for j in range(nkv // LANES):
        lo = j * LANES
        kv_ref[0, :, lo:lo + LANES] = _rope_chunk(p[:, nq + lo:nq + lo + LANES], ck, suk, sdk, half).astype(BF16)
    kv_ref[0, :, nkv:] = p[:, nq + nkv:].astype(BF16)


def _swa_front(moe, g, w_qkv, b_qkv, tabs_q, tabs_k, half):
    pos, meta_w, x2, ys = moe
    b, s, d = x2.shape
    tm = COMBINE_TILE
    ns = s // tm
    nt = b * ns
    n = w_qkv.shape[1]
    nq = SWA_Q_HEADS * SWA_HEAD_DIM
    nkv = SWA_KV_HEADS * SWA_HEAD_DIM
    tab_spec = pl.BlockSpec((tm, LANES), lambda bi, si: (si, 0))
    row = lambda width: pl.BlockSpec((1, tm, width), lambda bi, si: (bi, si, 0))
    pos_flat = _tile_major(pos, tm)
    return pl.pallas_call(
        functools.partial(_swa_front_kernel, half=half, tm=tm),
        grid=(b, ns),
        in_specs=[pl.BlockSpec((TOP_K * tm,), lambda bi, si: (bi * ns + si,), memory_space=pltpu.SMEM),
                  pl.BlockSpec((TOP_K * tm,), lambda bi, si: (jnp.minimum(bi * ns + si + 1, nt - 1),),
                               memory_space=pltpu.SMEM),
                  pl.BlockSpec((tm, LANES), lambda bi, si: (bi * ns + si, 0)),
                  row(d),
                  pl.BlockSpec(memory_space=pl.ANY),
                  pl.BlockSpec((1, d), lambda bi, si: (0, 0)),
                  pl.BlockSpec((d, n), lambda bi, si: (0, 0)),
                  pl.BlockSpec((1, n), lambda bi, si: (0, 0)),
                  tab_spec, tab_spec, tab_spec, tab_spec, tab_spec, tab_spec],
        out_specs=[row(d), row(nq), row(2 * nkv)],
        out_shape=[jax.ShapeDtypeStruct((b, s, d), F32),
                   jax.ShapeDtypeStruct((b, s, nq), BF16),
                   jax.ShapeDtypeStruct((b, s, 2 * nkv), BF16)],
        scratch_shapes=[pltpu.VMEM((2, TOP_K, tm * SUBLANES, LANES), F32), pltpu.SemaphoreType.DMA((2,))],
        compiler_params=_cparams(("arbitrary", "arbitrary")),
        name="swa_front",
    )(pos_flat, pos_flat, meta_w, x2, ys, g.reshape(1, d), w_qkv, b_qkv.reshape(1, n), *tabs_q, *tabs_k)


def _swa_head_order():
    g_sz = SWA_Q_HEADS // SWA_KV_HEADS
    order = []
    for slab in range(SWA_Q_HEADS // 2):
        pair, j = slab // g_sz, slab % g_sz
        order += [(2 * pair) * g_sz + j, (2 * pair + 1) * g_sz + j]
    return order


def _swa_attn_kernel(sink_ref, q_ref, kvp_ref, kvc_ref, o_ref, *, nblk):
    i = pl.program_id(1)
    hd = SWA_HEAD_DIM
    nkv = SWA_KV_HEADS * hd
    g_sz = SWA_Q_HEADS // SWA_KV_HEADS
    kv = jnp.concatenate([kvp_ref[0], kvc_ref[0]], axis=0)
    r = lax.broadcasted_iota(I32, (BLOCK, 2 * BLOCK), 0)
    c = lax.broadcasted_iota(I32, (BLOCK, 2 * BLOCK), 1)
    rel = c - BLOCK - r
    in_win = (rel <= 0) & (rel > -SWA_WINDOW)
    lane = lax.broadcasted_iota(I32, (1, LANES), 1)
    lo_half = lane < hd
    ones = jnp.ones((2 * BLOCK, LANES), BF16)
    for n in range(nblk):
        mask = in_win & ((c >= BLOCK) | (i > 0)) if n == 0 else in_win
        keys = kv[n * BLOCK:(n + 2) * BLOCK]
        for pair in range(SWA_KV_HEADS // 2):
            k2 = keys[:, pair * LANES:(pair + 1) * LANES]
            v2 = keys[:, nkv + pair * LANES:nkv + (pair + 1) * LANES]
            v_ext = jnp.concatenate([v2, ones], axis=1)
            pieces = []
            for j in range(g_sz):
                slab = pair * g_sz + j
                qs = q_ref[0, n * BLOCK:(n + 1) * BLOCK, slab * LANES:(slab + 1) * LANES]
                zero = jnp.zeros_like(qs)
                pieces += [jnp.where(lo_half, qs, zero), jnp.where(lo_half, zero, qs)]
            sc = _dot_nt(jnp.concatenate(pieces, axis=0), k2)
            probs, tails = [], []
            for pc in range(2 * g_sz):
                scp = jnp.where(mask, sc[pc * BLOCK:(pc + 1) * BLOCK], NEG_INF)
                sink = sink_ref[2 * g_sz * pair + pc] * LOG2E
                m = jnp.maximum(jnp.max(scp, axis=-1, keepdims=True), sink)
                probs.append(jnp.exp2(scp - m).astype(BF16))
                tails.append(jnp.exp2(sink - m))
            pv = _dot(jnp.concatenate(probs, axis=0), v_ext)
            for j in range(g_sz):
                slab = pair * g_sz + j
                halves = []
                for hf in range(2):
                    pc = 2 * j + hf
                    blk = pv[pc * BLOCK:(pc + 1) * BLOCK]
                    halves.append(blk[:, :LANES] / (blk[:, LANES:] + tails[pc]))
                o_ref[0, n * BLOCK:(n + 1) * BLOCK, slab * LANES:(slab + 1) * LANES] = (
                    jnp.where(lo_half, halves[0], halves[1]).astype(BF16))


def _swa_attn(q, kv, sinks):
    b, s, nq = q.shape
    nblk = 2
    tq = nblk * BLOCK
    return pl.pallas_call(
        functools.partial(_swa_attn_kernel, nblk=nblk),
        grid_spec=pltpu.PrefetchScalarGridSpec(
            num_scalar_prefetch=1,
            grid=(b, s // tq),
            in_specs=[pl.BlockSpec((1, tq, nq), lambda bi, ni, sk: (bi, ni, 0)),
                      pl.BlockSpec((1, BLOCK, kv.shape[2]), lambda bi, ni, sk: (bi, jnp.maximum(ni * nblk - 1, 0), 0)),
                      pl.BlockSpec((1, tq, kv.shape[2]), lambda bi, ni, sk: (bi, ni, 0))],
            out_specs=pl.BlockSpec((1, tq, nq), lambda bi, ni, sk: (bi, ni, 0))),
        out_shape=jax.ShapeDtypeStruct((b, s, nq), BF16),
        compiler_params=_cparams(("arbitrary", "arbitrary")),
        name="swa_attn",
    )(sinks, q, kv, kv)


def _post_kernel(*refs, tm, n_a, has_bias):
    x_ref = refs[0]
    a_refs = refs[1:1 + n_a]
    k = 1 + n_a
    wo_refs = refs[k:k + n_a]
    k += n_a
    if has_bias:
        bo_ref = refs[k]
        k += 1
    (gx_ref, wq_ref, mkv_ref, wxo_ref, gf_ref, wr_ref, br_ref,
     x2_ref, h2_ref, mi_ref, mw_ref, cnt_ref, cnt_acc) = refs[k:]
    first = (pl.program_id(0) == 0) & (pl.program_id(1) == 0)

    @pl.when(first)
    def _():
        cnt_acc[...] = jnp.zeros_like(cnt_acc)

    acc = _dot(a_refs[0][0], wo_refs[0][...])
    for a_ref, w_ref in zip(a_refs[1:], wo_refs[1:]):
        acc = acc + _dot(a_ref[0], w_ref[...])
    if has_bias:
        acc = acc + bo_ref[...]
    x1 = x_ref[0] + acc

    hx = _rms(x1, gx_ref[...]).astype(BF16)
    qx = (_dot(hx, wq_ref[...]) * (XATTN_HEAD_DIM ** -0.5 * LOG2E)).astype(BF16)
    xw = XATTN_HEADS * XATTN_HEAD_DIM
    ones = jnp.ones((mkv_ref.shape[1], LANES), BF16)
    outs = []
    for hh in range(XATTN_HEADS):
        lo = hh * XATTN_HEAD_DIM
        mk = mkv_ref[0, :, lo:lo + XATTN_HEAD_DIM]
        mv = jnp.concatenate([mkv_ref[0, :, xw + lo:xw + lo + XATTN_HEAD_DIM], ones], axis=1)
        sc = _dot_nt(qx[:, lo:lo + XATTN_HEAD_DIM], mk)
        m = jnp.max(sc, axis=-1, keepdims=True)
        pv = _dot(jnp.exp2(sc - m).astype(BF16), mv)
        outs.append((pv[:, :LANES] / pv[:, LANES:]).astype(BF16))
    ox = jnp.concatenate(outs, axis=-1)
    x2 = x1 + _dot(ox, wxo_ref[...])
    x2_ref[0] = x2

    h2 = _rms(x2, gf_ref[...])
    _to_token_tiles(h2_ref, h2, tm)
    h_hi = h2.astype(BF16)
    h_lo = (h2 - h_hi.astype(F32)).astype(BF16)
    logits = (_dot(h_hi, wr_ref[0]) + _dot(h_lo, wr_ref[0]) + _dot(h_hi, wr_ref[1])
              + br_ref[...])
    lane = lax.broadcasted_iota(I32, (tm, LANES), 1).astype(F32)
    big = float(LANES)
    g_lo = float(N_EXPERTS)
    lg = jnp.where((lane >= g_lo) & (lane < g_lo + N_GROUPS), logits, NEG_INF)
    mg = jnp.max(lg, axis=-1, keepdims=True)
    g_lane = jnp.min(jnp.where(lg == mg, lane, big), axis=-1, keepdims=True)
    p_g = 1.0 / jnp.sum(jnp.exp(lg - mg), axis=-1, keepdims=True)
    e_lo = (g_lane - g_lo) * EXPERTS_PER_GROUP
    le = jnp.where((lane >= e_lo) & (lane < e_lo + EXPERTS_PER_GROUP), logits, NEG_INF)
    m1 = jnp.max(le, axis=-1, keepdims=True)
    i1 = jnp.min(jnp.where(le == m1, lane, big), axis=-1, keepdims=True)
    le2 = jnp.where(lane == i1, NEG_INF, le)
    m2 = jnp.max(le2, axis=-1, keepdims=True)
    i2 = jnp.min(jnp.where(le2 == m2, lane, big), axis=-1, keepdims=True)
    t = jnp.exp(m2 - m1)
    w1 = p_g / (1.0 + t)
    w2 = p_g * t / (1.0 + t)

    oh1 = lane == i1
    oh2 = lane == i2
    oh = jnp.where(oh1 | oh2, 1.0, 0.0)
    rr = lax.broadcasted_iota(I32, (tm, tm), 0)
    cc = lax.broadcasted_iota(I32, (tm, tm), 1)
    tri = jnp.where(cc < rr, 1.0, 0.0).astype(BF16)
    before = _dot(tri, oh.astype(BF16)) + cnt_acc[...]
    r1 = jnp.sum(jnp.where(oh1, before, 0.0), axis=-1, keepdims=True)
    r2 = jnp.sum(jnp.where(oh2, before, 0.0), axis=-1, keepdims=True)
    cnt_new = cnt_acc[...] + jnp.sum(oh, axis=0, keepdims=True)
    cnt_acc[...] = cnt_new
    cnt_ref[...] = cnt_new

    mw_ref[...] = jnp.where(lane == 0, w1, jnp.where(lane == 1, w2, 0.0))
    ints = jnp.where(lane == 0, i1,
                     jnp.where(lane == 1, i2,
                               jnp.where(lane == 2, r1, jnp.where(lane == 3, r2, 0.0))))
    mi_ref[...] = ints.T[0:8, :].astype(I32)


def _post_mixer(x, a_list, wo_list, b_out, g_x, w_q, mkv, w_xo, g_f, w_rt, b_rt):
    b, s, d = x.shape
    tm = ROW_TILE
    ns = s // tm
    t = b * s
    n_a = len(a_list)
    has_bias = b_out is not None
    const2 = lambda bi, si: (0, 0)
    row = lambda width: pl.BlockSpec((1, tm, width), lambda bi, si: (bi, si, 0))
    in_specs = [row(d)] + [row(a.shape[2]) for a in a_list]
    in_specs += [pl.BlockSpec(w.shape, const2) for w in wo_list]
    args = [x, *a_list, *wo_list]
    if has_bias:
        in_specs.append(pl.BlockSpec((1, d), const2))
        args.append(b_out.reshape(1, d))
    xw = w_q.shape[1]
    in_specs += [pl.BlockSpec((1, d), const2),
                 pl.BlockSpec((d, xw), const2),
                 pl.BlockSpec((1, mkv.shape[1], mkv.shape[2]), lambda bi, si: (bi, 0, 0)),
                 pl.BlockSpec((xw, d), const2),
                 pl.BlockSpec((1, d), const2),
                 pl.BlockSpec((2, d, LANES), lambda bi, si: (0, 0, 0)),
                 pl.BlockSpec((1, LANES), const2)]
    args += [g_x.reshape(1, d), w_q, mkv, w_xo, g_f.reshape(1, d), w_rt, b_rt]
    out_specs = [row(d),
                 pl.BlockSpec((tm * SUBLANES, LANES), lambda bi, si: (bi * ns + si, 0)),
                 pl.BlockSpec((8, tm), lambda bi, si: (0, bi * ns + si)),
                 pl.BlockSpec((tm, LANES), lambda bi, si: (bi * ns + si, 0)),
                 pl.BlockSpec((1, LANES), const2)]
    out_shape = [jax.ShapeDtypeStruct((b, s, d), F32),
                 jax.ShapeDtypeStruct((t * SUBLANES, LANES), F32),
                 jax.ShapeDtypeStruct((8, t), I32),
                 jax.ShapeDtypeStruct((t, LANES), F32),
                 jax.ShapeDtypeStruct((1, LANES), F32)]
    return pl.pallas_call(
        functools.partial(_post_kernel, tm=tm, n_a=n_a, has_bias=has_bias),
        grid=(b, ns),
        in_specs=in_specs,
        out_specs=out_specs,
        out_shape=out_shape,
        scratch_shapes=[pltpu.VMEM((1, LANES), F32)],
        compiler_params=_cparams(("arbitrary", "arbitrary")),
        name="post_mixer",
    )(*args)


def _to_token_tiles(dst_ref, val, rows):
    for j in range(SUBLANES):
        dst_ref[pl.ds(j, rows, stride=SUBLANES), :] = val[:, j * LANES:(j + 1) * LANES]


def _from_token_tiles(src_ref, rows):
    return jnp.concatenate([src_ref[pl.ds(j, rows, stride=SUBLANES), :] for j in range(SUBLANES)], axis=1)


def _token_slice(ref, tok):
    return ref.at[pl.ds(pl.multiple_of(tok * SUBLANES, SUBLANES), SUBLANES)]


def _pos_kernel(off_ref, mi_ref, pos_ref):
    e = mi_ref[0:TOP_K, :]
    pos = mi_ref[TOP_K:2 * TOP_K, :]
    for j in range(N_EXPERTS):
        pos = pos + jnp.where(e == j, off_ref[j], 0)
    pos_ref[...] = pos


def _sorted_positions(off, meta_i):
    t = meta_i.shape[1]
    return pl.pallas_call(
        _pos_kernel,
        grid_spec=pltpu.PrefetchScalarGridSpec(
            num_scalar_prefetch=1,
            grid=(1,),
            in_specs=[pl.BlockSpec((8, t), lambda i, off: (0, 0))],
            out_specs=pl.BlockSpec((TOP_K, t), lambda i, off: (0, 0))),
        out_shape=jax.ShapeDtypeStruct((TOP_K, t), I32),
        compiler_params=_cparams(("arbitrary",)),
        name="moe_positions",
    )(off, meta_i)


def _tile_major(pos, tm):
    t = pos.shape[1]
    return pos.reshape(TOP_K, t // tm, tm).transpose(1, 0, 2).reshape(-1)


def _dispatch_kernel(pos_ref, h_ref, xs_ref, sem, *, tm):
    def issue(g, _):
        for u in range(DMA_UNROLL):
            r = g * DMA_UNROLL + u
            for kk in range(TOP_K):
                pos = pos_ref[kk * tm + r]
                pltpu.make_async_copy(_token_slice(h_ref, r), _token_slice(xs_ref, pos), sem).start(priority=kk)
        return 0

    lax.fori_loop(0, tm // DMA_UNROLL, issue, 0)
    for _ in range(TOP_K):
        pltpu.make_async_copy(h_ref, xs_ref.at[pl.ds(0, tm * SUBLANES)], sem).wait()


def _dispatch(pos, h2t, n_rows):
    tm = DISPATCH_TILE
    t = h2t.shape[0] // SUBLANES
    return pl.pallas_call(
        functools.partial(_dispatch_kernel, tm=tm),
        grid=(t // tm,),
        in_specs=[pl.BlockSpec((TOP_K * tm,), lambda i: (i,), memory_space=pltpu.SMEM),
                  pl.BlockSpec((tm * SUBLANES, LANES), lambda i: (i, 0))],
        out_specs=pl.BlockSpec(memory_space=pl.ANY),
        scratch_shapes=[pltpu.SemaphoreType.DMA],
        out_shape=jax.ShapeDtypeStruct((n_rows * SUBLANES, LANES), F32),
        compiler_params=_cparams(("arbitrary",)),
        name="moe_dispatch",
    )(_tile_major(pos, tm), h2t)


def _expert_kernel(tblk_ref, texp_ref, tn_ref, xs_ref, wgu_ref, wdn_ref, ys_ref, wgu_bf, wdn_bf, *, tm):
    i = pl.program_id(0)
    n_valid = tn_ref[i]
    new_expert = (i == 0) | (texp_ref[i] != texp_ref[jnp.maximum(i - 1, 0)])

    @pl.when(new_expert)
    def _():
        wgu_bf[...] = wgu_ref[0, 0].astype(BF16)
        wdn_bf[...] = wdn_ref[0, 0].astype(BF16)

    @pl.when(n_valid > 0)
    def _():
        row = lax.broadcasted_iota(I32, (tm, 1), 0)
        xs = jnp.where(row < n_valid, _from_token_tiles(xs_ref, tm), 0.0).astype(BF16)
        gu = _dot(xs, wgu_bf[...])
        gate, up = gu[:, :EXPERT_HIDDEN], gu[:, EXPERT_HIDDEN:]
        hid = (gate / (1.0 + jnp.exp(-gate)) * up).astype(BF16)
        _to_token_tiles(ys_ref, _dot(hid, wdn_bf[...]), tm)


def _experts(tile_blk, tile_exp, tile_n, xs, w_gu, w_dn, layer):
    tm = EXPERT_TILE
    nt = xs.shape[0] // (tm * SUBLANES)
    gu_shape, dn_shape = w_gu.shape[2:], w_dn.shape[2:]
    return pl.pallas_call(
        functools.partial(_expert_kernel, tm=tm),
        grid_spec=pltpu.PrefetchScalarGridSpec(
            num_scalar_prefetch=3,
            grid=(nt,),
            in_specs=[pl.BlockSpec((tm * SUBLANES, LANES), lambda i, tb, te, tn: (tb[i], 0)),
                      pl.BlockSpec((1, 1) + gu_shape, lambda i, tb, te, tn: (layer, te[i], 0, 0)),
                      pl.BlockSpec((1, 1) + dn_shape, lambda i, tb, te, tn: (layer, te[i], 0, 0))],
            out_specs=pl.BlockSpec((tm * SUBLANES, LANES), lambda i, tb, te, tn: (tb[i], 0)),
            scratch_shapes=[pltpu.VMEM(gu_shape, BF16), pltpu.VMEM(dn_shape, BF16)]),
        out_shape=jax.ShapeDtypeStruct(xs.shape, F32),
        compiler_params=_cparams(("arbitrary",)),
        name="moe_experts",
    )(tile_blk, tile_exp, tile_n, xs, w_gu, w_dn)


def _moe_combine_tile(pos_cur, pos_nxt, mw_ref, x, ys_ref, ybuf, sems, i, n, tm):
    def gather(pos_ref, slot):
        def issue(g, _):
            for u in range(DMA_UNROLL):
                r = g * DMA_UNROLL + u
                for kk in range(TOP_K):
                    pos = pos_ref[kk * tm + r]
                    pltpu.make_async_copy(_token_slice(ys_ref, pos), _token_slice(ybuf.at[slot, kk], r),
                                          sems.at[slot]).start(priority=kk)
            return 0

        lax.fori_loop(0, tm // DMA_UNROLL, issue, 0)

    @pl.when(i == 0)
    def _():
        gather(pos_cur, 0)

    @pl.when(i + 1 < n)
    def _():
        gather(pos_nxt, (i + 1) % 2)

    slot = i % 2
    for kk in range(TOP_K):
        pltpu.make_async_copy(ys_ref.at[pl.ds(0, tm * SUBLANES)], ybuf.at[slot, kk], sems.at[slot]).wait()
    mw = mw_ref[...]
    return (x + mw[:, 0:1] * _from_token_tiles(ybuf.at[slot, 0], tm)
            + mw[:, 1:2] * _from_token_tiles(ybuf.at[slot, 1], tm))


def _combine_kernel(pos_cur, pos_nxt, mw_ref, x_ref, ys_ref, *rest, tm, final):
    if final:
        g_ref, o_ref, ybuf, sems = rest
    else:
        o_ref, ybuf, sems = rest
    x3 = _moe_combine_tile(pos_cur, pos_nxt, mw_ref, x_ref[...], ys_ref, ybuf, sems,
                           pl.program_id(0), pl.num_programs(0), tm)
    if final:
        x3 = _rms(x3, g_ref[...])
    o_ref[...] = x3


def _combine(moe, g_final):
    pos, meta_w, x2, ys = moe
    d = x2.shape[-1]
    x2d = x2.reshape(-1, d)
    t = x2d.shape[0]
    tm = COMBINE_TILE
    nt = t // tm
    final = g_final is not None
    in_specs = [pl.BlockSpec((TOP_K * tm,), lambda i: (i,), memory_space=pltpu.SMEM),
                pl.BlockSpec((TOP_K * tm,), lambda i: (jnp.minimum(i + 1, nt - 1),), memory_space=pltpu.SMEM),
                pl.BlockSpec((tm, LANES), lambda i: (i, 0)),
                pl.BlockSpec((tm, d), lambda i: (i, 0)),
                pl.BlockSpec(memory_space=pl.ANY)]
    pos_flat = _tile_major(pos, tm)
    args = [pos_flat, pos_flat, meta_w, x2d, ys]
    if final:
        in_specs.append(pl.BlockSpec((1, d), lambda i: (0, 0)))
        args.append(g_final.reshape(1, d))
    return pl.pallas_call(
        functools.partial(_combine_kernel, tm=tm, final=final),
        grid=(nt,),
        in_specs=in_specs,
        out_specs=pl.BlockSpec((tm, d), lambda i: (i, 0)),
        scratch_shapes=[pltpu.VMEM((2, TOP_K, tm * SUBLANES, LANES), F32), pltpu.SemaphoreType.DMA((2,))],
        out_shape=jax.ShapeDtypeStruct((t, d), F32),
        compiler_params=_cparams(("arbitrary",)),
        name="moe_combine",
    )(*args)


def _moe_plan(counts, n_tiles):
    tm = EXPERT_TILE
    cnt = counts[0, :N_EXPERTS].astype(I32)
    padded = ((cnt + tm - 1) // tm) * tm
    ends = jnp.cumsum(padded)
    off = ends - padded
    start = jnp.arange(n_tiles, dtype=I32) * tm
    exp = jnp.minimum(jnp.sum((start[:, None] >= ends[None, :]).astype(I32), axis=1), N_EXPERTS - 1)
    n_valid = jnp.clip(off[exp] + cnt[exp] - start, 0, tm)
    active = start < ends[-1]
    n_valid = jnp.where(active, n_valid, 0)
    last = jnp.maximum(ends[-1] // tm - 1, 0)
    blk = jnp.where(active, jnp.arange(n_tiles, dtype=I32), last)
    exp = jnp.where(active, exp, exp[last])
    return off.astype(I32), blk.astype(I32), exp.astype(I32), n_valid.astype(I32)


def _moe(x2, h2t, meta_i, meta_w, counts, w_gu, w_dn, layer):
    b, s, d = x2.shape
    assert d == SUBLANES * LANES
    n_rows = b * s * TOP_K + N_EXPERTS * EXPERT_TILE
    off, blk, exp, n_valid = _moe_plan(counts, n_rows // EXPERT_TILE)
    pos = _sorted_positions(off, meta_i)
    xs = _dispatch(pos, h2t, n_rows)
    ys = _experts(blk, exp, n_valid, xs, w_gu, w_dn, layer)
    return pos, meta_w, x2, ys


def _router_weights(w_group, b_group, w_router, b_router):
    d = w_group.shape[0]
    pad = LANES - N_EXPERTS - N_GROUPS
    w = jnp.concatenate([w_router, w_group, jnp.zeros((d, pad), F32)], axis=1)
    bias = jnp.concatenate([b_router, b_group, jnp.zeros((pad,), F32)]).reshape(1, LANES)
    w_hi = w.astype(BF16)
    w_lo = (w - w_hi.astype(F32)).astype(BF16)
    return jnp.stack([w_hi, w_lo]), bias


def kernel(x, mem, mem_norm, mem_w_kv, norm_mix, norm_xattn, norm_ffn, hyb_w_in, hyb_conv_w, diff_lambda, diff_subln, hyb_w_out, swa_w_qkv, swa_b_qkv, swa_sinks, swa_w_out, swa_b_out, xattn_w_q, xattn_w_o, moe_w_group, moe_b_group, moe_w_router, moe_b_router, moe_w_gate_up, moe_w_down, final_norm):
    b, s, d = x.shape
    m = mem.shape[1]
    depth = norm_mix.shape[0]
    mkv = _norm_proj(mem.reshape(b * m, d), mem_norm, mem_w_kv.astype(BF16), m).reshape(b, m, -1)

    scale = DIFF_DK ** -0.5 * LOG2E
    cq, suq, sdq, half = _rope_lane_tables(s, DIFF_DK, scale)
    ck, suk, sdk, _ = _rope_lane_tables(s, DIFF_DK, 1.0)
    tabs_q, tabs_k = (cq, suq, sdq), (ck, suk, sdk)

    moe = None
    for l in range(depth):
        if l % 2 == 0:
            e = l // 2
            lambda_init = 0.8 - 0.6 * math.exp(-0.3 * l)
            if moe is not None:
                x = _combine(moe, None).reshape(b, s, d)
            ya, q, k, v = _hyb_front(x, norm_mix[l], hyb_w_in[e].astype(BF16), hyb_conv_w[e], tabs_q, tabs_k, half)
            o = _diff_attn(q, k, v, diff_lambda[e], diff_subln[e], lambda_init)
            w_out = hyb_w_out[e].astype(BF16)
            a_list, wo_list, b_out = [ya, o], [w_out[:CONV_CH], w_out[CONV_CH:]], None
        else:
            e = l // 2
            order = jnp.asarray(_swa_head_order(), I32)
            nq = SWA_Q_HEADS * SWA_HEAD_DIM
            cols = (order[:, None] * SWA_HEAD_DIM + jnp.arange(SWA_HEAD_DIM, dtype=I32)[None, :]).reshape(-1)
            cols_all = jnp.concatenate([cols, jnp.arange(nq, swa_w_qkv.shape[2], dtype=I32)])
            w_qkv = jnp.take(swa_w_qkv[e], cols_all, axis=1).astype(BF16)
            b_qkv = jnp.take(swa_b_qkv[e], cols_all)
            x, q, kv = _swa_front(moe, norm_mix[l], w_qkv, b_qkv, tabs_q, tabs_k, half)
            o = _swa_attn(q, kv, jnp.take(swa_sinks[e], order))
            a_list, wo_list, b_out = [o], [jnp.take(swa_w_out[e], cols, axis=0).astype(BF16)], swa_b_out[e]
        w_rt, b_rt = _router_weights(moe_w_group[l], moe_b_group[l], moe_w_router[l], moe_b_router[l])
        x2, h2, meta_i, meta_w, counts = _post_mixer(
            x, a_list, wo_list, b_out, norm_xattn[l], xattn_w_q[l].astype(BF16), mkv,
            xattn_w_o[l].astype(BF16), norm_ffn[l], w_rt, b_rt)
        moe = _moe(x2, h2, meta_i, meta_w, counts, moe_w_gate_up, moe_w_down, l)
    return _combine(moe, final_norm).reshape(b, s, d)
```

```python
import functools
import math

import jax
import jax.numpy as jnp
from jax import lax
from jax.experimental import pallas as pl
from jax.experimental.pallas import tpu as pltpu

F32 = jnp.float32
BF16 = jnp.bfloat16
I32 = jnp.int32

EPS = 1e-6
LANES = 128
SUBLANES = 8
DMA_UNROLL = 8
VMEM_LIMIT = 56 * 1024 * 1024

ROPE_THETA = 500000.0
ROPE_FRACTION = 4
BLOCK = 128
CONV_CH = 512
CONV_K = 3
DIFF_HEADS = 4
DIFF_DK = 64
DIFF_DV = 128
SWA_Q_HEADS = 16
SWA_KV_HEADS = 4
SWA_HEAD_DIM = 64
SWA_WINDOW = 128
XATTN_HEADS = 4
XATTN_HEAD_DIM = 128
N_GROUPS = 4
EXPERTS_PER_GROUP = 8
N_EXPERTS = N_GROUPS * EXPERTS_PER_GROUP
TOP_K = 2
EXPERT_HIDDEN = 512

ROW_TILE = 512
EXPERT_TILE = 512
DISPATCH_TILE = 1024
COMBINE_TILE = 512
POST_CHAINS = 1
NEG_INF = float("-inf")
LOG2E = math.log2(math.e)


def _cparams(sem):
    return pltpu.CompilerParams(dimension_semantics=sem, vmem_limit_bytes=VMEM_LIMIT)


def _rms(x, g):
    return x * lax.rsqrt(jnp.mean(x * x, axis=-1, keepdims=True) + EPS) * g


def _dot(a, b):
    return jnp.dot(a, b, preferred_element_type=F32)


def _dot_nt(a, b):
    return lax.dot_general(a, b, (((1,), (1,)), ((), ())), preferred_element_type=F32)


def _rope_lane_tables(seq, head_dim, scale):
    rot = head_dim // ROPE_FRACTION
    half = rot // 2
    pos = jnp.arange(seq, dtype=F32)
    inv = ROPE_THETA ** (-jnp.arange(0, rot, 2, dtype=F32) / rot)
    ang = pos[:, None] * inv[None, :]
    cos, sin = jnp.cos(ang), jnp.sin(ang)
    idx = jnp.arange(LANES) % head_dim
    cl = jnp.take(cos, idx % half, axis=1)
    sl = jnp.take(sin, idx % half, axis=1)
    c = jnp.where(idx < rot, cl, 1.0) * scale
    s_up = jnp.where(idx < half, -sl, 0.0) * scale
    s_dn = jnp.where((idx >= half) & (idx < rot), sl, 0.0) * scale
    return c.astype(F32), s_up.astype(F32), s_dn.astype(F32), half


def _rope_chunk(xc, c, s_up, s_dn, half):
    return (xc * c + pltpu.roll(xc, LANES - half, 1) * s_up + pltpu.roll(xc, half, 1) * s_dn)


def _norm_proj_kernel(x_ref, g_ref, w_ref, o_ref):
    h = _rms(x_ref[...], g_ref[...]).astype(BF16)
    o_ref[...] = _dot(h, w_ref[...]).astype(o_ref.dtype)


def _norm_proj(x2d, g, w_bf16, tm):
    m, d = x2d.shape
    n = w_bf16.shape[1]
    return pl.pallas_call(
        _norm_proj_kernel,
        grid=(m // tm,),
        in_specs=[pl.BlockSpec((tm, d), lambda i: (i, 0)),
                  pl.BlockSpec((1, d), lambda i: (0, 0)),
                  pl.BlockSpec((d, n), lambda i: (0, 0))],
        out_specs=pl.BlockSpec((tm, n), lambda i: (i, 0)),
        out_shape=jax.ShapeDtypeStruct((m, n), BF16),
        compiler_params=_cparams(("arbitrary",)),
        name="mem_kv_proj",
    )(x2d, g.reshape(1, d), w_bf16)


def _hyb_front_kernel(x_ref, g_ref, w_ref, cw_ref, cq_ref, suq_ref, sdq_ref, ck_ref, suk_ref, sdk_ref,
                      ya_ref, q_ref, k_ref, v_ref, cbuf, *, tm, half):
    s = pl.program_id(1)
    h = _rms(x_ref[0], g_ref[...]).astype(BF16)
    c = CONV_CH
    base = 3 * c
    nq = DIFF_HEADS * 2 * DIFF_DK
    gate_b = _dot(h, w_ref[:, 0:c])
    cu = _dot(h, w_ref[:, c:2 * c]) * _dot(h, w_ref[:, 2 * c:3 * c])

    @pl.when(s == 0)
    def _():
        cbuf[0:8, :] = jnp.zeros((8, c), F32)

    cbuf[8:8 + tm, :] = cu
    cw = cw_ref[...]
    conv = (cw[0:1, :] * cbuf[6:6 + tm, :] + cw[1:2, :] * cbuf[7:7 + tm, :] + cw[2:3, :] * cu)
    ya_ref[0] = (gate_b * conv).astype(BF16)
    cbuf[0:8, :] = cbuf[tm:tm + 8, :]

    cq, suq, sdq = cq_ref[...], suq_ref[...], sdq_ref[...]
    ck, suk, sdk = ck_ref[...], suk_ref[...], sdk_ref[...]
    pq = _dot(h, w_ref[:, base:base + nq])
    for j in range(nq // LANES):
        q_ref[0, :, j * LANES:(j + 1) * LANES] = _rope_chunk(pq[:, j * LANES:(j + 1) * LANES], cq, suq, sdq, half).astype(BF16)
    pk = _dot(h, w_ref[:, base + nq:base + 2 * nq])
    for j in range(nq // LANES):
        k_ref[0, :, j * LANES:(j + 1) * LANES] = _rope_chunk(pk[:, j * LANES:(j + 1) * LANES], ck, suk, sdk, half).astype(BF16)
    v_ref[0] = _dot(h, w_ref[:, base + 2 * nq:]).astype(BF16)


def _hyb_front(x, g, w_in, conv_w, tabs_q, tabs_k, half):
    b, s, d = x.shape
    tm = ROW_TILE
    n = w_in.shape[1]
    nq = DIFF_HEADS * 2 * DIFF_DK
    nv = DIFF_HEADS * DIFF_DV
    tab_spec = pl.BlockSpec((tm, LANES), lambda bi, si: (si, 0))
    row = lambda width: pl.BlockSpec((1, tm, width), lambda bi, si: (bi, si, 0))
    return pl.pallas_call(
        functools.partial(_hyb_front_kernel, tm=tm, half=half),
        grid=(b, s // tm),
        in_specs=[row(d),
                  pl.BlockSpec((1, d), lambda bi, si: (0, 0)),
                  pl.BlockSpec((d, n), lambda bi, si: (0, 0)),
                  pl.BlockSpec((CONV_K, CONV_CH), lambda bi, si: (0, 0)),
                  tab_spec, tab_spec, tab_spec, tab_spec, tab_spec, tab_spec],
        out_specs=[row(CONV_CH), row(nq), row(nq), row(nv)],
        out_shape=[jax.ShapeDtypeStruct((b, s, CONV_CH), BF16),
                   jax.ShapeDtypeStruct((b, s, nq), BF16),
                   jax.ShapeDtypeStruct((b, s, nq), BF16),
                   jax.ShapeDtypeStruct((b, s, nv), BF16)],
        scratch_shapes=[pltpu.VMEM((tm + 8, CONV_CH), F32)],
        compiler_params=_cparams(("arbitrary", "arbitrary")),
        name="hyb_front",
    )(x, g.reshape(1, d), w_in, conv_w, *tabs_q, *tabs_k)


def _lane_fold(x, op):
    r = x[:, 0:LANES]
    for c in range(1, x.shape[1] // LANES):
        r = op(r, x[:, c * LANES:(c + 1) * LANES])
    return r


def _diff_attn_kernel(q_ref, k_ref, v_ref, lam_ref, g_ref, o_ref, sbuf, stat, acc, *, tq, hp, lambda_init):
    i = pl.program_id(2)
    lane = lax.broadcasted_iota(I32, (1, LANES), 1)
    qs = []
    for h in range(hp):
        q = q_ref[0, :, h * LANES:(h + 1) * LANES]
        zero = jnp.zeros_like(q)
        qs += [jnp.where(lane < DIFF_DK, q, zero), jnp.where(lane >= DIFF_DK, q, zero)]
    nc = 2 * hp

    def scores(j, masked):
        for c in range(nc):
            h = c // 2
            kb = k_ref[0, pl.ds(pl.multiple_of(j * tq, tq), tq), h * LANES:(h + 1) * LANES]
            sc = _dot_nt(qs[c], kb)
            if masked:
                r = lax.broadcasted_iota(I32, (tq, tq), 0)
                cc = lax.broadcasted_iota(I32, (tq, tq), 1)
                sc = jnp.where(cc <= r, sc, NEG_INF)
            sbuf[c, j] = sc
            stat[c] = jnp.maximum(stat[c], _lane_fold(sc, jnp.maximum))

    stat[...] = jnp.full(stat.shape, NEG_INF, F32)

    def pass1(j, carry):
        scores(j, False)
        return carry

    lax.fori_loop(0, i, pass1, 0)
    scores(i, True)
    ms = [jnp.max(stat[c], axis=-1, keepdims=True) for c in range(nc)]

    acc[...] = jnp.zeros_like(acc)
    ones = jnp.ones((tq, LANES), BF16)

    def pass2(j, carry):
        for h in range(hp):
            vb = v_ref[0, pl.ds(pl.multiple_of(j * tq, tq), tq), h * LANES:(h + 1) * LANES]
            v_ext = jnp.concatenate([vb, ones], axis=1)
            p0 = jnp.exp2(sbuf[2 * h, j] - ms[2 * h])
            p1 = jnp.exp2(sbuf[2 * h + 1, j] - ms[2 * h + 1])
            acc[h] += _dot(jnp.concatenate([p0, p1], axis=0).astype(BF16), v_ext)
        return carry

    lax.fori_loop(0, i + 1, pass2, 0)

    lf = lam_ref[...]
    lam = (jnp.exp(jnp.sum(lf[0:1] * lf[1:2], keepdims=True))
           - jnp.exp(jnp.sum(lf[2:3] * lf[3:4], keepdims=True)) + lambda_init)
    for h in range(hp):
        a0 = acc[h, 0:tq, :]
        a1 = acc[h, tq:2 * tq, :]
        o = a0[:, :LANES] / a0[:, LANES:] - lam * (a1[:, :LANES] / a1[:, LANES:])
        o_ref[0, :, h * LANES:(h + 1) * LANES] = (_rms(o, g_ref[...]) * (1.0 - lambda_init)).astype(BF16)


def _diff_attn(q, k, v, lam_vecs, subln_g, lambda_init):
    b, s, _ = q.shape
    tq = 256
    hp = 4
    blk = lambda bi, hi, qi: (bi, qi, hi)
    full = lambda bi, hi, qi: (bi, 0, hi)
    return pl.pallas_call(
        functools.partial(_diff_attn_kernel, tq=tq, hp=hp, lambda_init=lambda_init),
        grid=(b, DIFF_HEADS // hp, s // tq),
        in_specs=[pl.BlockSpec((1, tq, hp * LANES), blk),
                  pl.BlockSpec((1, s, hp * LANES), full),
                  pl.BlockSpec((1, s, hp * LANES), full),
                  pl.BlockSpec((4, DIFF_DK), lambda bi, hi, qi: (0, 0)),
                  pl.BlockSpec((1, DIFF_DV), lambda bi, hi, qi: (0, 0))],
        out_specs=pl.BlockSpec((1, tq, hp * LANES), blk),
        out_shape=jax.ShapeDtypeStruct((b, s, DIFF_HEADS * DIFF_DV), BF16),
        scratch_shapes=[pltpu.VMEM((2 * hp, s // tq, tq, tq), F32),
                        pltpu.VMEM((2 * hp, tq, LANES), F32),
                        pltpu.VMEM((hp, 2 * tq, DIFF_DV + LANES), F32)],
        compiler_params=_cparams(("arbitrary", "arbitrary", "arbitrary")),
        name="diff_attn",
    )(q, k, v, lam_vecs, subln_g.reshape(1, DIFF_DV))


def _swa_front_kernel(x_ref, g_ref, w_ref, b_ref, cq_ref, suq_ref, sdq_ref, ck_ref, suk_ref, sdk_ref,
                      q_ref, kv_ref, *, half):
    h = _rms(x_ref[0], g_ref[...]).astype(BF16)
    nq = SWA_Q_HEADS * SWA_HEAD_DIM
    nkv = SWA_KV_HEADS * SWA_HEAD_DIM
    cq, suq, sdq = cq_ref[...], suq_ref[...], sdq_ref[...]
    ck, suk, sdk = ck_ref[...], suk_ref[...], sdk_ref[...]
    pq = _dot(h, w_ref[:, 0:nq]) + b_ref[:, 0:nq]
    for j in range(nq // LANES):
        lo = j * LANES
        q_ref[0, :, lo:lo + LANES] = _rope_chunk(pq[:, lo:lo + LANES], cq, suq, sdq, half).astype(BF16)
    pkv = _dot(h, w_ref[:, nq:]) + b_ref[:, nq:]
    for j in range(nkv // LANES):
        lo = j * LANES
        kv_ref[0, :, lo:lo + LANES] = _rope_chunk(pkv[:, lo:lo + LANES], ck, suk, sdk, half).astype(BF16)
    kv_ref[0, :, nkv:] = pkv[:, nkv:].astype(BF16)


def _swa_front(x, g, w_qkv, b_qkv, tabs_q, tabs_k, half):
    b, s, d = x.shape
    tm = ROW_TILE
    n = w_qkv.shape[1]
    nq = SWA_Q_HEADS * SWA_HEAD_DIM
    nkv = SWA_KV_HEADS * SWA_HEAD_DIM
    tab_spec = pl.BlockSpec((tm, LANES), lambda bi, si: (si, 0))
    row = lambda width: pl.BlockSpec((1, tm, width), lambda bi, si: (bi, si, 0))
    return pl.pallas_call(
        functools.partial(_swa_front_kernel, half=half),
        grid=(b, s // tm),
        in_specs=[row(d),
                  pl.BlockSpec((1, d), lambda bi, si: (0, 0)),
                  pl.BlockSpec((d, n), lambda bi, si: (0, 0)),
                  pl.BlockSpec((1, n), lambda bi, si: (0, 0)),
                  tab_spec, tab_spec, tab_spec, tab_spec, tab_spec, tab_spec],
        out_specs=[row(nq), row(2 * nkv)],
        out_shape=[jax.ShapeDtypeStruct((b, s, nq), BF16),
                   jax.ShapeDtypeStruct((b, s, 2 * nkv), BF16)],
        compiler_params=_cparams(("arbitrary", "arbitrary")),
        name="swa_front",
    )(x, g.reshape(1, d), w_qkv, b_qkv.reshape(1, n), *tabs_q, *tabs_k)


def _swa_head_order():
    g_sz = SWA_Q_HEADS // SWA_KV_HEADS
    order = []
    for slab in range(SWA_Q_HEADS // 2):
        pair, j = slab // g_sz, slab % g_sz
        order += [(2 * pair) * g_sz + j, (2 * pair + 1) * g_sz + j]
    return order


def _swa_attn_kernel(sink_ref, q_ref, kvp_ref, kvc_ref, o_ref, *, nblk):
    i = pl.program_id(1)
    hd = SWA_HEAD_DIM
    nkv = SWA_KV_HEADS * hd
    g_sz = SWA_Q_HEADS // SWA_KV_HEADS
    kv = jnp.concatenate([kvp_ref[0], kvc_ref[0]], axis=0)
    r = lax.broadcasted_iota(I32, (BLOCK, 2 * BLOCK), 0)
    c = lax.broadcasted_iota(I32, (BLOCK, 2 * BLOCK), 1)
    rel = c - BLOCK - r
    in_win = (rel <= 0) & (rel > -SWA_WINDOW)
    lane = lax.broadcasted_iota(I32, (1, LANES), 1)
    lo_half = lane < hd
    ones = jnp.ones((2 * BLOCK, LANES), BF16)
    for n in range(nblk):
        mask = in_win & ((c >= BLOCK) | (i > 0)) if n == 0 else in_win
        keys = kv[n * BLOCK:(n + 2) * BLOCK]
        for pair in range(SWA_KV_HEADS // 2):
            k2 = keys[:, pair * LANES:(pair + 1) * LANES]
            v2 = keys[:, nkv + pair * LANES:nkv + (pair + 1) * LANES]
            v_ext = jnp.concatenate([v2, ones], axis=1)
            pieces = []
            for j in range(g_sz):
                slab = pair * g_sz + j
                qs = q_ref[0, n * BLOCK:(n + 1) * BLOCK, slab * LANES:(slab + 1) * LANES]
                zero = jnp.zeros_like(qs)
                pieces += [jnp.where(lo_half, qs, zero), jnp.where(lo_half, zero, qs)]
            sc = _dot_nt(jnp.concatenate(pieces, axis=0), k2)
            probs, tails = [], []
            for pc in range(2 * g_sz):
                scp = jnp.where(mask, sc[pc * BLOCK:(pc + 1) * BLOCK], NEG_INF)
                sink = sink_ref[2 * g_sz * pair + pc] * LOG2E
                m = jnp.maximum(jnp.max(scp, axis=-1, keepdims=True), sink)
                probs.append(jnp.exp2(scp - m).astype(BF16))
                tails.append(jnp.exp2(sink - m))
            pv = _dot(jnp.concatenate(probs, axis=0), v_ext)
            for j in range(g_sz):
                slab = pair * g_sz + j
                halves = []
                for hf in range(2):
                    pc = 2 * j + hf
                    blk = pv[pc * BLOCK:(pc + 1) * BLOCK]
                    halves.append(blk[:, :LANES] / (blk[:, LANES:] + tails[pc]))
                o_ref[0, n * BLOCK:(n + 1) * BLOCK, slab * LANES:(slab + 1) * LANES] = (
                    jnp.where(lo_half, halves[0], halves[1]).astype(BF16))


def _swa_attn(q, kv, sinks):
    b, s, nq = q.shape
    nblk = 2
    tq = nblk * BLOCK
    return pl.pallas_call(
        functools.partial(_swa_attn_kernel, nblk=nblk),
        grid_spec=pltpu.PrefetchScalarGridSpec(
            num_scalar_prefetch=1,
            grid=(b, s // tq),
            in_specs=[pl.BlockSpec((1, tq, nq), lambda bi, ni, sk: (bi, ni, 0)),
                      pl.BlockSpec((1, BLOCK, kv.shape[2]), lambda bi, ni, sk: (bi, jnp.maximum(ni * nblk - 1, 0), 0)),
                      pl.BlockSpec((1, tq, kv.shape[2]), lambda bi, ni, sk: (bi, ni, 0))],
            out_specs=pl.BlockSpec((1, tq, nq), lambda bi, ni, sk: (bi, ni, 0))),
        out_shape=jax.ShapeDtypeStruct((b, s, nq), BF16),
        compiler_params=_cparams(("arbitrary", "arbitrary")),
        name="swa_attn",
    )(sinks, q, kv, kv)


def _post_kernel(*refs, tm, n_a, has_bias):
    x_ref = refs[0]
    a_refs = refs[1:1 + n_a]
    k = 1 + n_a
    wo_refs = refs[k:k + n_a]
    k += n_a
    if has_bias:
        bo_ref = refs[k]
        k += 1
    (gx_ref, wq_ref, mkv_ref, wxo_ref, gf_ref, wr_ref, br_ref,
     x2_ref, h2_ref, mi_ref, mw_ref, cnt_ref, cnt_acc) = refs[k:]
    first = (pl.program_id(0) == 0) & (pl.program_id(1) == 0)

    @pl.when(first)
    def _():
        cnt_acc[...] = jnp.zeros_like(cnt_acc)

    xw = XATTN_HEADS * XATTN_HEAD_DIM
    ones = jnp.ones((mkv_ref.shape[1], LANES), BF16)

    def row_chain(r0, rh):
        acc = _dot(a_refs[0][0, r0:r0 + rh, :], wo_refs[0][...])
        for a_ref, w_ref in zip(a_refs[1:], wo_refs[1:]):
            acc = acc + _dot(a_ref[0, r0:r0 + rh, :], w_ref[...])
        if has_bias:
            acc = acc + bo_ref[...]
        x1 = x_ref[0, r0:r0 + rh, :] + acc

        hx = _rms(x1, gx_ref[...]).astype(BF16)
        qx = (_dot(hx, wq_ref[...]) * (XATTN_HEAD_DIM ** -0.5 * LOG2E)).astype(BF16)
        outs = []
        for hh in range(XATTN_HEADS):
            lo = hh * XATTN_HEAD_DIM
            mk = mkv_ref[0, :, lo:lo + XATTN_HEAD_DIM]
            mv = jnp.concatenate([mkv_ref[0, :, xw + lo:xw + lo + XATTN_HEAD_DIM], ones], axis=1)
            sc = _dot_nt(qx[:, lo:lo + XATTN_HEAD_DIM], mk)
            m = jnp.max(sc, axis=-1, keepdims=True)
            pv = _dot(jnp.exp2(sc - m).astype(BF16), mv)
            outs.append((pv[:, :LANES] / pv[:, LANES:]).astype(BF16))
        x2 = x1 + _dot(jnp.concatenate(outs, axis=-1), wxo_ref[...])
        x2_ref[0, r0:r0 + rh, :] = x2

        h2 = _rms(x2, gf_ref[...])
        _to_token_tiles(h2_ref, h2, rh, r0)
        h_hi = h2.astype(BF16)
        h_lo = (h2 - h_hi.astype(F32)).astype(BF16)
        return _dot(h_hi, wr_ref[0]) + _dot(h_lo, wr_ref[0]) + _dot(h_hi, wr_ref[1]) + br_ref[...]

    rh = tm // POST_CHAINS
    logits = jnp.concatenate([row_chain(c * rh, rh) for c in range(POST_CHAINS)], axis=0)
    lane = lax.broadcasted_iota(I32, (tm, LANES), 1).astype(F32)
    big = float(LANES)
    g_lo = float(N_EXPERTS)
    lg = jnp.where((lane >= g_lo) & (lane < g_lo + N_GROUPS), logits, NEG_INF)
    mg = jnp.max(lg, axis=-1, keepdims=True)
    g_lane = jnp.min(jnp.where(lg == mg, lane, big), axis=-1, keepdims=True)
    p_g = 1.0 / jnp.sum(jnp.exp(lg - mg), axis=-1, keepdims=True)
    e_lo = (g_lane - g_lo) * EXPERTS_PER_GROUP
    le = jnp.where((lane >= e_lo) & (lane < e_lo + EXPERTS_PER_GROUP), logits, NEG_INF)
    m1 = jnp.max(le, axis=-1, keepdims=True)
    i1 = jnp.min(jnp.where(le == m1, lane, big), axis=-1, keepdims=True)
    le2 = jnp.where(lane == i1, NEG_INF, le)
    m2 = jnp.max(le2, axis=-1, keepdims=True)
    i2 = jnp.min(jnp.where(le2 == m2, lane, big), axis=-1, keepdims=True)
    t = jnp.exp(m2 - m1)
    w1 = p_g / (1.0 + t)
    w2 = p_g * t / (1.0 + t)

    oh1 = lane == i1
    oh2 = lane == i2
    oh = jnp.where(oh1 | oh2, 1.0, 0.0)
    rr = lax.broadcasted_iota(I32, (tm, tm), 0)
    cc = lax.broadcasted_iota(I32, (tm, tm), 1)
    tri = jnp.where(cc < rr, 1.0, 0.0).astype(BF16)
    before = _dot(tri, oh.astype(BF16)) + cnt_acc[...]
    r1 = jnp.sum(jnp.where(oh1, before, 0.0), axis=-1, keepdims=True)
    r2 = jnp.sum(jnp.where(oh2, before, 0.0), axis=-1, keepdims=True)
    cnt_new = cnt_acc[...] + jnp.sum(oh, axis=0, keepdims=True)
    cnt_acc[...] = cnt_new
    cnt_ref[...] = cnt_new.astype(I32)

    mw_ref[...] = jnp.where(lane == 0, w1, jnp.where(lane == 1, w2, 0.0))
    ints = jnp.where(lane == 0, i1,
                     jnp.where(lane == 1, i2,
                               jnp.where(lane == 2, r1, jnp.where(lane == 3, r2, 0.0))))
    mi_ref[...] = ints.T[0:8, :].astype(I32)


def _post_mixer(x, a_list, wo_list, b_out, g_x, w_q, mkv, w_xo, g_f, w_rt, b_rt):
    b, s, d = x.shape
    tm = ROW_TILE
    ns = s // tm
    t = b * s
    n_a = len(a_list)
    has_bias = b_out is not None
    const2 = lambda bi, si: (0, 0)
    row = lambda width: pl.BlockSpec((1, tm, width), lambda bi, si: (bi, si, 0))
    in_specs = [row(d)] + [row(a.shape[2]) for a in a_list]
    in_specs += [pl.BlockSpec(w.shape, const2) for w in wo_list]
    args = [x, *a_list, *wo_list]
    if has_bias:
        in_specs.append(pl.BlockSpec((1, d), const2))
        args.append(b_out.reshape(1, d))
    xw = w_q.shape[1]
    in_specs += [pl.BlockSpec((1, d), const2),
                 pl.BlockSpec((d, xw), const2),
                 pl.BlockSpec((1, mkv.shape[1], mkv.shape[2]), lambda bi, si: (bi, 0, 0)),
                 pl.BlockSpec((xw, d), const2),
                 pl.BlockSpec((1, d), const2),
                 pl.BlockSpec((2, d, LANES), lambda bi, si: (0, 0, 0)),
                 pl.BlockSpec((1, LANES), const2)]
    args += [g_x.reshape(1, d), w_q, mkv, w_xo, g_f.reshape(1, d), w_rt, b_rt]
    out_specs = [row(d),
                 pl.BlockSpec((tm * SUBLANES, LANES), lambda bi, si: (bi * ns + si, 0)),
                 pl.BlockSpec((8, tm), lambda bi, si: (0, bi * ns + si)),
                 pl.BlockSpec((tm, LANES), lambda bi, si: (bi * ns + si, 0)),
                 pl.BlockSpec((1, LANES), const2)]
    out_shape = [jax.ShapeDtypeStruct((b, s, d), F32),
                 jax.ShapeDtypeStruct((t * SUBLANES, LANES), F32),
                 jax.ShapeDtypeStruct((8, t), I32),
                 jax.ShapeDtypeStruct((t, LANES), F32),
                 jax.ShapeDtypeStruct((1, LANES), I32)]
    return pl.pallas_call(
        functools.partial(_post_kernel, tm=tm, n_a=n_a, has_bias=has_bias),
        grid=(b, ns),
        in_specs=in_specs,
        out_specs=out_specs,
        out_shape=out_shape,
        scratch_shapes=[pltpu.VMEM((1, LANES), F32)],
        compiler_params=_cparams(("arbitrary", "arbitrary")),
        name="post_mixer",
    )(*args)


def _to_token_tiles(dst_ref, val, rows, tok0=0):
    for j in range(SUBLANES):
        dst_ref[pl.ds(tok0 * SUBLANES + j, rows, stride=SUBLANES), :] = val[:, j * LANES:(j + 1) * LANES]


def _from_token_tiles(src_ref, rows):
    return jnp.concatenate([src_ref[pl.ds(j, rows, stride=SUBLANES), :] for j in range(SUBLANES)], axis=1)


def _token_slice(ref, tok):
    return ref.at[pl.ds(pl.multiple_of(tok * SUBLANES, SUBLANES), SUBLANES)]


def _pos_kernel(off_ref, mi_ref, pos_ref):
    e = mi_ref[0:TOP_K, :]
    pos = mi_ref[TOP_K:2 * TOP_K, :]
    for j in range(N_EXPERTS):
        pos = pos + jnp.where(e == j, off_ref[j], 0)
    pos_ref[...] = pos


def _sorted_positions(off, meta_i):
    t = meta_i.shape[1]
    return pl.pallas_call(
        _pos_kernel,
        grid_spec=pltpu.PrefetchScalarGridSpec(
            num_scalar_prefetch=1,
            grid=(1,),
            in_specs=[pl.BlockSpec((8, t), lambda i, off: (0, 0))],
            out_specs=pl.BlockSpec((TOP_K, t), lambda i, off: (0, 0))),
        out_shape=jax.ShapeDtypeStruct((TOP_K, t), I32),
        compiler_params=_cparams(("arbitrary",)),
        name="moe_positions",
    )(off, meta_i)


def _tile_major(pos, tm):
    t = pos.shape[1]
    return pos.reshape(TOP_K, t // tm, tm).transpose(1, 0, 2).reshape(-1)


def _dispatch_kernel(pos_ref, h_ref, xs_ref, sem, *, tm):
    def issue(g, _):
        for u in range(DMA_UNROLL):
            r = g * DMA_UNROLL + u
            for kk in range(TOP_K):
                pos = pos_ref[kk * tm + r]
                pltpu.make_async_copy(_token_slice(h_ref, r), _token_slice(xs_ref, pos), sem).start(priority=kk)
        return 0

    lax.fori_loop(0, tm // DMA_UNROLL, issue, 0)
    for _ in range(TOP_K):
        pltpu.make_async_copy(h_ref, xs_ref.at[pl.ds(0, tm * SUBLANES)], sem).wait()


def _dispatch(pos, h2t, n_rows):
    tm = DISPATCH_TILE
    t = h2t.shape[0] // SUBLANES
    return pl.pallas_call(
        functools.partial(_dispatch_kernel, tm=tm),
        grid=(t // tm,),
        in_specs=[pl.BlockSpec((TOP_K * tm,), lambda i: (i,), memory_space=pltpu.SMEM),
                  pl.BlockSpec((tm * SUBLANES, LANES), lambda i: (i, 0))],
        out_specs=pl.BlockSpec(memory_space=pl.ANY),
        scratch_shapes=[pltpu.SemaphoreType.DMA],
        out_shape=jax.ShapeDtypeStruct((n_rows * SUBLANES, LANES), F32),
        compiler_params=_cparams(("arbitrary",)),
        name="moe_dispatch",
    )(_tile_major(pos, tm), h2t)


def _expert_kernel(tblk_ref, texp_ref, tn_ref, xs_ref, wgu_ref, wdn_ref, ys_ref, wgu_bf, wdn_bf, *, tm):
    i = pl.program_id(0)
    n_valid = tn_ref[i]
    new_expert = (i == 0) | (texp_ref[i] != texp_ref[jnp.maximum(i - 1, 0)])

    @pl.when(new_expert)
    def _():
        wgu_bf[...] = wgu_ref[0, 0].astype(BF16)
        wdn_bf[...] = wdn_ref[0, 0].astype(BF16)

    @pl.when(n_valid > 0)
    def _():
        row = lax.broadcasted_iota(I32, (tm, 1), 0)
        xs = jnp.where(row < n_valid, _from_token_tiles(xs_ref, tm), 0.0).astype(BF16)
        gu = _dot(xs, wgu_bf[...])
        gate, up = gu[:, :EXPERT_HIDDEN], gu[:, EXPERT_HIDDEN:]
        hid = (gate / (1.0 + jnp.exp(-gate)) * up).astype(BF16)
        _to_token_tiles(ys_ref, _dot(hid, wdn_bf[...]), tm)


def _experts(tile_blk, tile_exp, tile_n, xs, w_gu, w_dn, layer):
    tm = EXPERT_TILE
    nt = xs.shape[0] // (tm * SUBLANES)
    gu_shape, dn_shape = w_gu.shape[2:], w_dn.shape[2:]
    return pl.pallas_call(
        functools.partial(_expert_kernel, tm=tm),
        grid_spec=pltpu.PrefetchScalarGridSpec(
            num_scalar_prefetch=3,
            grid=(nt,),
            in_specs=[pl.BlockSpec((tm * SUBLANES, LANES), lambda i, tb, te, tn: (tb[i], 0)),
                      pl.BlockSpec((1, 1) + gu_shape, lambda i, tb, te, tn: (layer, te[i], 0, 0)),
                      pl.BlockSpec((1, 1) + dn_shape, lambda i, tb, te, tn: (layer, te[i], 0, 0))],
            out_specs=pl.BlockSpec((tm * SUBLANES, LANES), lambda i, tb, te, tn: (tb[i], 0)),
            scratch_shapes=[pltpu.VMEM(gu_shape, BF16), pltpu.VMEM(dn_shape, BF16)]),
        out_shape=jax.ShapeDtypeStruct(xs.shape, F32),
        compiler_params=_cparams(("arbitrary",)),
        name="moe_experts",
    )(tile_blk, tile_exp, tile_n, xs, w_gu, w_dn)


def _moe_combine_tile(pos_cur, pos_nxt, mw_ref, x, ys_ref, ybuf, sems, i, n, tm):
    def gather(pos_ref, slot):
        def issue(g, _):
            for u in range(DMA_UNROLL):
                r = g * DMA_UNROLL + u
                for kk in range(TOP_K):
                    pos = pos_ref[kk * tm + r]
                    pltpu.make_async_copy(_token_slice(ys_ref, pos), _token_slice(ybuf.at[slot, kk], r),
                                          sems.at[slot]).start(priority=kk)
            return 0

        lax.fori_loop(0, tm // DMA_UNROLL, issue, 0)

    @pl.when(i == 0)
    def _():
        gather(pos_cur, 0)

    @pl.when(i + 1 < n)
    def _():
        gather(pos_nxt, (i + 1) % 2)

    slot = i % 2
    for kk in range(TOP_K):
        pltpu.make_async_copy(ys_ref.at[pl.ds(0, tm * SUBLANES)], ybuf.at[slot, kk], sems.at[slot]).wait()
    mw = mw_ref[...]
    return (x + mw[:, 0:1] * _from_token_tiles(ybuf.at[slot, 0], tm)
            + mw[:, 1:2] * _from_token_tiles(ybuf.at[slot, 1], tm))


def _combine_kernel(pos_cur, pos_nxt, mw_ref, x_ref, ys_ref, *rest, tm, final):
    if final:
        g_ref, o_ref, ybuf, sems = rest
    else:
        o_ref, ybuf, sems = rest
    x3 = _moe_combine_tile(pos_cur, pos_nxt, mw_ref, x_ref[...], ys_ref, ybuf, sems,
                           pl.program_id(0), pl.num_programs(0), tm)
    if final:
        x3 = _rms(x3, g_ref[...])
    o_ref[...] = x3


def _combine(moe, g_final):
    pos, meta_w, x2, ys = moe
    d = x2.shape[-1]
    x2d = x2.reshape(-1, d)
    t = x2d.shape[0]
    tm = COMBINE_TILE
    nt = t // tm
    final = g_final is not None
    in_specs = [pl.BlockSpec((TOP_K * tm,), lambda i: (i,), memory_space=pltpu.SMEM),
                pl.BlockSpec((TOP_K * tm,), lambda i: (jnp.minimum(i + 1, nt - 1),), memory_space=pltpu.SMEM),
                pl.BlockSpec((tm, LANES), lambda i: (i, 0)),
                pl.BlockSpec((tm, d), lambda i: (i, 0)),
                pl.BlockSpec(memory_space=pl.ANY)]
    pos_flat = _tile_major(pos, tm)
    args = [pos_flat, pos_flat, meta_w, x2d, ys]
    if final:
        in_specs.append(pl.BlockSpec((1, d), lambda i: (0, 0)))
        args.append(g_final.reshape(1, d))
    return pl.pallas_call(
        functools.partial(_combine_kernel, tm=tm, final=final),
        grid=(nt,),
        in_specs=in_specs,
        out_specs=pl.BlockSpec((tm, d), lambda i: (i, 0)),
        scratch_shapes=[pltpu.VMEM((2, TOP_K, tm * SUBLANES, LANES), F32), pltpu.SemaphoreType.DMA((2,))],
        out_shape=jax.ShapeDtypeStruct((t, d), F32),
        compiler_params=_cparams(("arbitrary",)),
        name="moe_combine",
    )(*args)


def _plan_kernel(cnt_ref, off_ref, blk_ref, exp_ref, nv_ref, *, tm, n_tiles):
    shift = tm.bit_length() - 1

    def per_expert(e, row0):
        c = cnt_ref[0, e]
        ntile = lax.shift_right_logical(c + (tm - 1), shift)
        off_ref[e] = row0
        t0 = lax.shift_right_logical(row0, shift)

        def fill(j, _):
            blk_ref[t0 + j] = t0 + j
            exp_ref[t0 + j] = e
            nv_ref[t0 + j] = jnp.minimum(c - j * tm, tm)
            return 0

        lax.fori_loop(0, ntile, fill, 0)
        return row0 + lax.shift_left(ntile, shift)

    total = lax.fori_loop(0, N_EXPERTS, per_expert, jnp.int32(0))
    used = lax.shift_right_logical(total, shift)
    last = jnp.maximum(used - 1, 0)

    def tail(i, _):
        blk_ref[i] = last
        exp_ref[i] = exp_ref[last]
        nv_ref[i] = 0
        return 0

    lax.fori_loop(used, n_tiles, tail, 0)


def _moe_plan(counts, n_tiles):
    tm = EXPERT_TILE
    assert tm & (tm - 1) == 0
    smem = pl.BlockSpec(memory_space=pltpu.SMEM)
    return pl.pallas_call(
        functools.partial(_plan_kernel, tm=tm, n_tiles=n_tiles),
        in_specs=[smem],
        out_specs=[smem, smem, smem, smem],
        out_shape=[jax.ShapeDtypeStruct((N_EXPERTS,), I32)] + [jax.ShapeDtypeStruct((n_tiles,), I32)] * 3,
        name="moe_plan",
    )(counts)


def _moe(x2, h2t, meta_i, meta_w, counts, w_gu, w_dn, layer):
    b, s, d = x2.shape
    assert d == SUBLANES * LANES
    n_rows = b * s * TOP_K + N_EXPERTS * EXPERT_TILE
    off, blk, exp, n_valid = _moe_plan(counts, n_rows // EXPERT_TILE)
    pos = _sorted_positions(off, meta_i)
    xs = _dispatch(pos, h2t, n_rows)
    ys = _experts(blk, exp, n_valid, xs, w_gu, w_dn, layer)
    return pos, meta_w, x2, ys


def _router_weights(w_group, b_group, w_router, b_router):
    d = w_group.shape[0]
    pad = LANES - N_EXPERTS - N_GROUPS
    w = jnp.concatenate([w_router, w_group, jnp.zeros((d, pad), F32)], axis=1)
    bias = jnp.concatenate([b_router, b_group, jnp.zeros((pad,), F32)]).reshape(1, LANES)
    w_hi = w.astype(BF16)
    w_lo = (w - w_hi.astype(F32)).astype(BF16)
    return jnp.stack([w_hi, w_lo]), bias


def kernel(x, mem, mem_norm, mem_w_kv, norm_mix, norm_xattn, norm_ffn, hyb_w_in, hyb_conv_w, diff_lambda, diff_subln, hyb_w_out, swa_w_qkv, swa_b_qkv, swa_sinks, swa_w_out, swa_b_out, xattn_w_q, xattn_w_o, moe_w_group, moe_b_group, moe_w_router, moe_b_router, moe_w_gate_up, moe_w_down, final_norm):
    b, s, d = x.shape
    m = mem.shape[1]
    depth = norm_mix.shape[0]
    mkv = _norm_proj(mem.reshape(b * m, d), mem_norm, mem_w_kv.astype(BF16), m).reshape(b, m, -1)

    scale = DIFF_DK ** -0.5 * LOG2E
    cq, suq, sdq, half = _rope_lane_tables(s, DIFF_DK, scale)
    ck, suk, sdk, _ = _rope_lane_tables(s, DIFF_DK, 1.0)
    tabs_q, tabs_k = (cq, suq, sdq), (ck, suk, sdk)

    moe = None
    for l in range(depth):
        if moe is not None:
            x = _combine(moe, None).reshape(b, s, d)
        if l % 2 == 0:
            e = l // 2
            lambda_init = 0.8 - 0.6 * math.exp(-0.3 * l)
            ya, q, k, v = _hyb_front(x, norm_mix[l], hyb_w_in[e].astype(BF16), hyb_conv_w[e], tabs_q, tabs_k, half)
            o = _diff_attn(q, k, v, diff_lambda[e], diff_subln[e], lambda_init)
            w_out = hyb_w_out[e].astype(BF16)
            a_list, wo_list, b_out = [ya, o], [w_out[:CONV_CH], w_out[CONV_CH:]], None
        else:
            e = l // 2
            order = jnp.asarray(_swa_head_order(), I32)
            nq = SWA_Q_HEADS * SWA_HEAD_DIM
            cols = (order[:, None] * SWA_HEAD_DIM + jnp.arange(SWA_HEAD_DIM, dtype=I32)[None, :]).reshape(-1)
            cols_all = jnp.concatenate([cols, jnp.arange(nq, swa_w_qkv.shape[2], dtype=I32)])
            w_qkv = jnp.take(swa_w_qkv[e], cols_all, axis=1).astype(BF16)
            b_qkv = jnp.take(swa_b_qkv[e], cols_all)
            q, kv = _swa_front(x, norm_mix[l], w_qkv, b_qkv, tabs_q, tabs_k, half)
            o = _swa_attn(q, kv, jnp.take(swa_sinks[e], order))
            a_list, wo_list, b_out = [o], [jnp.take(swa_w_out[e], cols, axis=0).astype(BF16)], swa_b_out[e]
        w_rt, b_rt = _router_weights(moe_w_group[l], moe_b_group[l], moe_w_router[l], moe_b_router[l])
        x2, h2, meta_i, meta_w, counts = _post_mixer(
            x, a_list, wo_list, b_out, norm_xattn[l], xattn_w_q[l].astype(BF16), mkv,
            xattn_w_o[l].astype(BF16), norm_ffn[l], w_rt, b_rt)
        moe = _moe(x2, h2, meta_i, meta_w, counts, moe_w_gate_up, moe_w_down, l)
    return _combine(moe, final_norm).reshape(b, s, d)
```

```python
import functools
import math

import jax
import jax.numpy as jnp
from jax import lax
from jax.experimental import pallas as pl
from jax.experimental.pallas import tpu as pltpu

F32 = jnp.float32
BF16 = jnp.bfloat16
I32 = jnp.int32

EPS = 1e-6
LANES = 128
SUBLANES = 8
DMA_UNROLL = 8
VMEM_LIMIT = 56 * 1024 * 1024

ROPE_THETA = 500000.0
ROPE_FRACTION = 4
BLOCK = 128
CONV_CH = 512
CONV_K = 3
DIFF_HEADS = 4
DIFF_DK = 64
DIFF_DV = 128
SWA_Q_HEADS = 16
SWA_KV_HEADS = 4
SWA_HEAD_DIM = 64
SWA_WINDOW = 128
XATTN_HEADS = 4
XATTN_HEAD_DIM = 128
N_GROUPS = 4
EXPERTS_PER_GROUP = 8
N_EXPERTS = N_GROUPS * EXPERTS_PER_GROUP
TOP_K = 2
EXPERT_HIDDEN = 512

ROW_TILE = 512
EXPERT_TILE = 512
DISPATCH_TILE = 1024
COMBINE_TILE = 512
POST_CHAINS = 2
EXPERT_CHAINS = 2
NEG_INF = float("-inf")
LOG2E = math.log2(math.e)


def _cparams(sem):
    return pltpu.CompilerParams(dimension_semantics=sem, vmem_limit_bytes=VMEM_LIMIT)


def _rms(x, g):
    return x * lax.rsqrt(jnp.mean(x * x, axis=-1, keepdims=True) + EPS) * g


def _dot(a, b):
    return jnp.dot(a, b, preferred_element_type=F32)


def _dot_nt(a, b):
    return lax.dot_general(a, b, (((1,), (1,)), ((), ())), preferred_element_type=F32)


def _rope_lane_tables(seq, head_dim, scale):
    rot = head_dim // ROPE_FRACTION
    half = rot // 2
    pos = jnp.arange(seq, dtype=F32)
    inv = ROPE_THETA ** (-jnp.arange(0, rot, 2, dtype=F32) / rot)
    ang = pos[:, None] * inv[None, :]
    cos, sin = jnp.cos(ang), jnp.sin(ang)
    idx = jnp.arange(LANES) % head_dim
    cl = jnp.take(cos, idx % half, axis=1)
    sl = jnp.take(sin, idx % half, axis=1)
    c = jnp.where(idx < rot, cl, 1.0) * scale
    s_up = jnp.where(idx < half, -sl, 0.0) * scale
    s_dn = jnp.where((idx >= half) & (idx < rot), sl, 0.0) * scale
    return c.astype(F32), s_up.astype(F32), s_dn.astype(F32), half


def _rope_chunk(xc, c, s_up, s_dn, half):
    return (xc * c + pltpu.roll(xc, LANES - half, 1) * s_up + pltpu.roll(xc, half, 1) * s_dn)


def _norm_proj_kernel(x_ref, g_ref, w_ref, o_ref):
    h = _rms(x_ref[...], g_ref[...]).astype(BF16)
    o_ref[...] = _dot(h, w_ref[...]).astype(o_ref.dtype)


def _norm_proj(x2d, g, w_bf16, tm):
    m, d = x2d.shape
    n = w_bf16.shape[1]
    return pl.pallas_call(
        _norm_proj_kernel,
        grid=(m // tm,),
        in_specs=[pl.BlockSpec((tm, d), lambda i: (i, 0)),
                  pl.BlockSpec((1, d), lambda i: (0, 0)),
                  pl.BlockSpec((d, n), lambda i: (0, 0))],
        out_specs=pl.BlockSpec((tm, n), lambda i: (i, 0)),
        out_shape=jax.ShapeDtypeStruct((m, n), BF16),
        compiler_params=_cparams(("arbitrary",)),
        name="mem_kv_proj",
    )(x2d, g.reshape(1, d), w_bf16)


def _hyb_front_kernel(x_ref, g_ref, w_ref, cw_ref, cq_ref, suq_ref, sdq_ref, ck_ref, suk_ref, sdk_ref,
                      ya_ref, q_ref, k_ref, v_ref, cbuf, *, tm, half):
    s = pl.program_id(1)
    h = _rms(x_ref[0], g_ref[...]).astype(BF16)
    c = CONV_CH
    base = 3 * c
    nq = DIFF_HEADS * 2 * DIFF_DK
    gate_b = _dot(h, w_ref[:, 0:c])
    cu = _dot(h, w_ref[:, c:2 * c]) * _dot(h, w_ref[:, 2 * c:3 * c])

    @pl.when(s == 0)
    def _():
        cbuf[0:8, :] = jnp.zeros((8, c), F32)

    cbuf[8:8 + tm, :] = cu
    cw = cw_ref[...]
    conv = (cw[0:1, :] * cbuf[6:6 + tm, :] + cw[1:2, :] * cbuf[7:7 + tm, :] + cw[2:3, :] * cu)
    ya_ref[0] = (gate_b * conv).astype(BF16)
    cbuf[0:8, :] = cbuf[tm:tm + 8, :]

    cq, suq, sdq = cq_ref[...], suq_ref[...], sdq_ref[...]
    ck, suk, sdk = ck_ref[...], suk_ref[...], sdk_ref[...]
    pq = _dot(h, w_ref[:, base:base + nq])
    for j in range(nq // LANES):
        q_ref[0, :, j * LANES:(j + 1) * LANES] = _rope_chunk(pq[:, j * LANES:(j + 1) * LANES], cq, suq, sdq, half).astype(BF16)
    pk = _dot(h, w_ref[:, base + nq:base + 2 * nq])
    for j in range(nq // LANES):
        k_ref[0, :, j * LANES:(j + 1) * LANES] = _rope_chunk(pk[:, j * LANES:(j + 1) * LANES], ck, suk, sdk, half).astype(BF16)
    v_ref[0] = _dot(h, w_ref[:, base + 2 * nq:]).astype(BF16)


def _hyb_front(x, g, w_in, conv_w, tabs_q, tabs_k, half):
    b, s, d = x.shape
    tm = ROW_TILE
    n = w_in.shape[1]
    nq = DIFF_HEADS * 2 * DIFF_DK
    nv = DIFF_HEADS * DIFF_DV
    tab_spec = pl.BlockSpec((tm, LANES), lambda bi, si: (si, 0))
    row = lambda width: pl.BlockSpec((1, tm, width), lambda bi, si: (bi, si, 0))
    return pl.pallas_call(
        functools.partial(_hyb_front_kernel, tm=tm, half=half),
        grid=(b, s // tm),
        in_specs=[row(d),
                  pl.BlockSpec((1, d), lambda bi, si: (0, 0)),
                  pl.BlockSpec((d, n), lambda bi, si: (0, 0)),
                  pl.BlockSpec((CONV_K, CONV_CH), lambda bi, si: (0, 0)),
                  tab_spec, tab_spec, tab_spec, tab_spec, tab_spec, tab_spec],
        out_specs=[row(CONV_CH), row(nq), row(nq), row(nv)],
        out_shape=[jax.ShapeDtypeStruct((b, s, CONV_CH), BF16),
                   jax.ShapeDtypeStruct((b, s, nq), BF16),
                   jax.ShapeDtypeStruct((b, s, nq), BF16),
                   jax.ShapeDtypeStruct((b, s, nv), BF16)],
        scratch_shapes=[pltpu.VMEM((tm + 8, CONV_CH), F32)],
        compiler_params=_cparams(("arbitrary", "arbitrary")),
        name="hyb_front",
    )(x, g.reshape(1, d), w_in, conv_w, *tabs_q, *tabs_k)


def _lane_fold(x, op):
    r = x[:, 0:LANES]
    for c in range(1, x.shape[1] // LANES):
        r = op(r, x[:, c * LANES:(c + 1) * LANES])
    return r


def _diff_attn_kernel(q_ref, k_ref, v_ref, lam_ref, g_ref, o_ref, sbuf, stat, acc, *, tq, hp, lambda_init):
    i = pl.program_id(2)
    lane = lax.broadcasted_iota(I32, (1, LANES), 1)
    qs = []
    for h in range(hp):
        q = q_ref[0, :, h * LANES:(h + 1) * LANES]
        zero = jnp.zeros_like(q)
        qs += [jnp.where(lane < DIFF_DK, q, zero), jnp.where(lane >= DIFF_DK, q, zero)]
    nc = 2 * hp

    def scores(j, masked):
        for c in range(nc):
            h = c // 2
            kb = k_ref[0, pl.ds(pl.multiple_of(j * tq, tq), tq), h * LANES:(h + 1) * LANES]
            sc = _dot_nt(qs[c], kb)
            if masked:
                r = lax.broadcasted_iota(I32, (tq, tq), 0)
                cc = lax.broadcasted_iota(I32, (tq, tq), 1)
                sc = jnp.where(cc <= r, sc, NEG_INF)
            sbuf[c, j] = sc
            stat[c] = jnp.maximum(stat[c], _lane_fold(sc, jnp.maximum))

    stat[...] = jnp.full(stat.shape, NEG_INF, F32)

    def pass1(j, carry):
        scores(j, False)
        return carry

    lax.fori_loop(0, i, pass1, 0)
    scores(i, True)
    ms = [jnp.max(stat[c], axis=-1, keepdims=True) for c in range(nc)]

    ones = jnp.ones((tq, LANES), BF16)

    def pass2(j, first):
        for h in range(hp):
            vb = v_ref[0, pl.ds(pl.multiple_of(j * tq, tq), tq), h * LANES:(h + 1) * LANES]
            v_ext = jnp.concatenate([vb, ones], axis=1)
            p0 = jnp.exp2(sbuf[2 * h, j] - ms[2 * h])
            p1 = jnp.exp2(sbuf[2 * h + 1, j] - ms[2 * h + 1])
            upd = _dot(jnp.concatenate([p0, p1], axis=0).astype(BF16), v_ext)
            if first:
                acc[h] = upd
            else:
                acc[h] += upd
        return 0

    pass2(0, True)
    lax.fori_loop(1, i + 1, lambda j, c: pass2(j, False), 0)

    lf = lam_ref[...]
    lam = (jnp.exp(jnp.sum(lf[0:1] * lf[1:2], keepdims=True))
           - jnp.exp(jnp.sum(lf[2:3] * lf[3:4], keepdims=True)) + lambda_init)
    for h in range(hp):
        a0 = acc[h, 0:tq, :]
        a1 = acc[h, tq:2 * tq, :]
        o = a0[:, :LANES] / a0[:, LANES:] - lam * (a1[:, :LANES] / a1[:, LANES:])
        o_ref[0, :, h * LANES:(h + 1) * LANES] = (_rms(o, g_ref[...]) * (1.0 - lambda_init)).astype(BF16)


def _diff_attn(q, k, v, lam_vecs, subln_g, lambda_init):
    b, s, _ = q.shape
    tq = 512
    hp = 2
    blk = lambda bi, hi, qi: (bi, qi, hi)
    full = lambda bi, hi, qi: (bi, 0, hi)
    return pl.pallas_call(
        functools.partial(_diff_attn_kernel, tq=tq, hp=hp, lambda_init=lambda_init),
        grid=(b, DIFF_HEADS // hp, s // tq),
        in_specs=[pl.BlockSpec((1, tq, hp * LANES), blk),
                  pl.BlockSpec((1, s, hp * LANES), full),
                  pl.BlockSpec((1, s, hp * LANES), full),
                  pl.BlockSpec((4, DIFF_DK), lambda bi, hi, qi: (0, 0)),
                  pl.BlockSpec((1, DIFF_DV), lambda bi, hi, qi: (0, 0))],
        out_specs=pl.BlockSpec((1, tq, hp * LANES), blk),
        out_shape=jax.ShapeDtypeStruct((b, s, DIFF_HEADS * DIFF_DV), BF16),
        scratch_shapes=[pltpu.VMEM((2 * hp, s // tq, tq, tq), F32),
                        pltpu.VMEM((2 * hp, tq, LANES), F32),
                        pltpu.VMEM((hp, 2 * tq, DIFF_DV + LANES), F32)],
        compiler_params=_cparams(("arbitrary", "arbitrary", "arbitrary")),
        name="diff_attn",
    )(q, k, v, lam_vecs, subln_g.reshape(1, DIFF_DV))


def _swa_front_kernel(x_ref, g_ref, w_ref, b_ref, cq_ref, suq_ref, sdq_ref, ck_ref, suk_ref, sdk_ref,
                      q_ref, kv_ref, *, half):
    h = _rms(x_ref[0], g_ref[...]).astype(BF16)
    nq = SWA_Q_HEADS * SWA_HEAD_DIM
    nkv = SWA_KV_HEADS * SWA_HEAD_DIM
    cq, suq, sdq = cq_ref[...], suq_ref[...], sdq_ref[...]
    ck, suk, sdk = ck_ref[...], suk_ref[...], sdk_ref[...]
    pq = _dot(h, w_ref[:, 0:nq]) + b_ref[:, 0:nq]
    for j in range(nq // LANES):
        lo = j * LANES
        q_ref[0, :, lo:lo + LANES] = _rope_chunk(pq[:, lo:lo + LANES], cq, suq, sdq, half).astype(BF16)
    pkv = _dot(h, w_ref[:, nq:]) + b_ref[:, nq:]
    for j in range(nkv // LANES):
        lo = j * LANES
        kv_ref[0, :, lo:lo + LANES] = _rope_chunk(pkv[:, lo:lo + LANES], ck, suk, sdk, half).astype(BF16)
    kv_ref[0, :, nkv:] = pkv[:, nkv:].astype(BF16)


def _swa_front(x, g, w_qkv, b_qkv, tabs_q, tabs_k, half):
    b, s, d = x.shape
    tm = ROW_TILE
    n = w_qkv.shape[1]
    nq = SWA_Q_HEADS * SWA_HEAD_DIM
    nkv = SWA_KV_HEADS * SWA_HEAD_DIM
    tab_spec = pl.BlockSpec((tm, LANES), lambda bi, si: (si, 0))
    row = lambda width: pl.BlockSpec((1, tm, width), lambda bi, si: (bi, si, 0))
    return pl.pallas_call(
        functools.partial(_swa_front_kernel, half=half),
        grid=(b, s // tm),
        in_specs=[row(d),
                  pl.BlockSpec((1, d), lambda bi, si: (0, 0)),
                  pl.BlockSpec((d, n), lambda bi, si: (0, 0)),
                  pl.BlockSpec((1, n), lambda bi, si: (0, 0)),
                  tab_spec, tab_spec, tab_spec, tab_spec, tab_spec, tab_spec],
        out_specs=[row(nq), row(2 * nkv)],
        out_shape=[jax.ShapeDtypeStruct((b, s, nq), BF16),
                   jax.ShapeDtypeStruct((b, s, 2 * nkv), BF16)],
        compiler_params=_cparams(("arbitrary", "arbitrary")),
        name="swa_front",
    )(x, g.reshape(1, d), w_qkv, b_qkv.reshape(1, n), *tabs_q, *tabs_k)


def _swa_head_order():
    g_sz = SWA_Q_HEADS // SWA_KV_HEADS
    order = []
    for slab in range(SWA_Q_HEADS // 2):
        pair, j = slab // g_sz, slab % g_sz
        order += [(2 * pair) * g_sz + j, (2 * pair + 1) * g_sz + j]
    return order


def _swa_attn_kernel(sink_ref, q_ref, kvp_ref, kvc_ref, o_ref, *, nblk):
    i = pl.program_id(1)
    hd = SWA_HEAD_DIM
    nkv = SWA_KV_HEADS * hd
    g_sz = SWA_Q_HEADS // SWA_KV_HEADS
    kv = jnp.concatenate([kvp_ref[0], kvc_ref[0]], axis=0)
    r = lax.broadcasted_iota(I32, (BLOCK, 2 * BLOCK), 0)
    c = lax.broadcasted_iota(I32, (BLOCK, 2 * BLOCK), 1)
    rel = c - BLOCK - r
    in_win = (rel <= 0) & (rel > -SWA_WINDOW)
    lane = lax.broadcasted_iota(I32, (1, LANES), 1)
    lo_half = lane < hd
    ones = jnp.ones((2 * BLOCK, LANES), BF16)
    for n in range(nblk):
        mask = in_win & ((c >= BLOCK) | (i > 0)) if n == 0 else in_win
        keys = kv[n * BLOCK:(n + 2) * BLOCK]
        for pair in range(SWA_KV_HEADS // 2):
            k2 = keys[:, pair * LANES:(pair + 1) * LANES]
            v2 = keys[:, nkv + pair * LANES:nkv + (pair + 1) * LANES]
            v_ext = jnp.concatenate([v2, ones], axis=1)
            pieces = []
            for j in range(g_sz):
                slab = pair * g_sz + j
                qs = q_ref[0, n * BLOCK:(n + 1) * BLOCK, slab * LANES:(slab + 1) * LANES]
                zero = jnp.zeros_like(qs)
                pieces += [jnp.where(lo_half, qs, zero), jnp.where(lo_half, zero, qs)]
            sc = _dot_nt(jnp.concatenate(pieces, axis=0), k2)
            probs, tails = [], []
            for pc in range(2 * g_sz):
                scp = jnp.where(mask, sc[pc * BLOCK:(pc + 1) * BLOCK], NEG_INF)
                sink = sink_ref[2 * g_sz * pair + pc] * LOG2E
                m = jnp.maximum(jnp.max(scp, axis=-1, keepdims=True), sink)
                probs.append(jnp.exp2(scp - m).astype(BF16))
                tails.append(jnp.exp2(sink - m))
            pv = _dot(jnp.concatenate(probs, axis=0), v_ext)
            for j in range(g_sz):
                slab = pair * g_sz + j
                halves = []
                for hf in range(2):
                    pc = 2 * j + hf
                    blk = pv[pc * BLOCK:(pc + 1) * BLOCK]
                    halves.append(blk[:, :LANES] / (blk[:, LANES:] + tails[pc]))
                o_ref[0, n * BLOCK:(n + 1) * BLOCK, slab * LANES:(slab + 1) * LANES] = (
                    jnp.where(lo_half, halves[0], halves[1]).astype(BF16))


def _swa_attn(q, kv, sinks):
    b, s, nq = q.shape
    nblk = 2
    tq = nblk * BLOCK
    return pl.pallas_call(
        functools.partial(_swa_attn_kernel, nblk=nblk),
        grid_spec=pltpu.PrefetchScalarGridSpec(
            num_scalar_prefetch=1,
            grid=(b, s // tq),
            in_specs=[pl.BlockSpec((1, tq, nq), lambda bi, ni, sk: (bi, ni, 0)),
                      pl.BlockSpec((1, BLOCK, kv.shape[2]), lambda bi, ni, sk: (bi, jnp.maximum(ni * nblk - 1, 0), 0)),
                      pl.BlockSpec((1, tq, kv.shape[2]), lambda bi, ni, sk: (bi, ni, 0))],
            out_specs=pl.BlockSpec((1, tq, nq), lambda bi, ni, sk: (bi, ni, 0))),
        out_shape=jax.ShapeDtypeStruct((b, s, nq), BF16),
        compiler_params=_cparams(("arbitrary", "arbitrary")),
        name="swa_attn",
    )(sinks, q, kv, kv)


def _post_kernel(*refs, tm, n_a, has_bias):
    x_ref = refs[0]
    a_refs = refs[1:1 + n_a]
    k = 1 + n_a
    wo_refs = refs[k:k + n_a]
    k += n_a
    if has_bias:
        bo_ref = refs[k]
        k += 1
    (gx_ref, wq_ref, mkv_ref, wxo_ref, gf_ref, wr_ref, br_ref,
     x2_ref, h2_ref, mi_ref, mw_ref, cnt_ref, cnt_acc) = refs[k:]
    first = (pl.program_id(0) == 0) & (pl.program_id(1) == 0)

    @pl.when(first)
    def _():
        cnt_acc[...] = jnp.zeros_like(cnt_acc)

    xw = XATTN_HEADS * XATTN_HEAD_DIM
    ones = jnp.ones((mkv_ref.shape[1], LANES), BF16)

    rh = tm // POST_CHAINS

    def out_proj(r0):
        acc = _dot(a_refs[0][0, r0:r0 + rh, :], wo_refs[0][...])
        for a_ref, w_ref in zip(a_refs[1:], wo_refs[1:]):
            acc = acc + _dot(a_ref[0, r0:r0 + rh, :], w_ref[...])
        if has_bias:
            acc = acc + bo_ref[...]
        return x_ref[0, r0:r0 + rh, :] + acc

    def q_proj(x1):
        hx = _rms(x1, gx_ref[...]).astype(BF16)
        return (_dot(hx, wq_ref[...]) * (XATTN_HEAD_DIM ** -0.5 * LOG2E)).astype(BF16)

    def mem_attn(qx):
        outs = []
        for hh in range(XATTN_HEADS):
            lo = hh * XATTN_HEAD_DIM
            mk = mkv_ref[0, :, lo:lo + XATTN_HEAD_DIM]
            mv = jnp.concatenate([mkv_ref[0, :, xw + lo:xw + lo + XATTN_HEAD_DIM], ones], axis=1)
            sc = _dot_nt(qx[:, lo:lo + XATTN_HEAD_DIM], mk)
            m = jnp.max(sc, axis=-1, keepdims=True)
            pv = _dot(jnp.exp2(sc - m).astype(BF16), mv)
            outs.append((pv[:, :LANES] / pv[:, LANES:]).astype(BF16))
        return jnp.concatenate(outs, axis=-1)

    def o_proj(r0, x1, ox):
        x2 = x1 + _dot(ox, wxo_ref[...])
        x2_ref[0, r0:r0 + rh, :] = x2
        return x2

    def router(r0, x2):
        h2 = _rms(x2, gf_ref[...])
        _to_token_tiles(h2_ref, h2, rh, r0)
        h_hi = h2.astype(BF16)
        h_lo = (h2 - h_hi.astype(F32)).astype(BF16)
        return _dot(h_hi, wr_ref[0]) + _dot(h_lo, wr_ref[0]) + _dot(h_hi, wr_ref[1]) + br_ref[...]

    starts = [c * rh for c in range(POST_CHAINS)]
    x1s = [out_proj(r0) for r0 in starts]
    qxs = [q_proj(x1) for x1 in x1s]
    oxs = [mem_attn(qx) for qx in qxs]
    x2s = [o_proj(r0, x1, ox) for r0, x1, ox in zip(starts, x1s, oxs)]
    logits = jnp.concatenate([router(r0, x2) for r0, x2 in zip(starts, x2s)], axis=0)
    lane = lax.broadcasted_iota(I32, (tm, LANES), 1).astype(F32)
    big = float(LANES)
    g_lo = float(N_EXPERTS)
    lg = jnp.where((lane >= g_lo) & (lane < g_lo + N_GROUPS), logits, NEG_INF)
    mg = jnp.max(lg, axis=-1, keepdims=True)
    g_lane = jnp.min(jnp.where(lg == mg, lane, big), axis=-1, keepdims=True)
    p_g = 1.0 / jnp.sum(jnp.exp(lg - mg), axis=-1, keepdims=True)
    e_lo = (g_lane - g_lo) * EXPERTS_PER_GROUP
    le = jnp.where((lane >= e_lo) & (lane < e_lo + EXPERTS_PER_GROUP), logits, NEG_INF)
    m1 = jnp.max(le, axis=-1, keepdims=True)
    i1 = jnp.min(jnp.where(le == m1, lane, big), axis=-1, keepdims=True)
    le2 = jnp.where(lane == i1, NEG_INF, le)
    m2 = jnp.max(le2, axis=-1, keepdims=True)
    i2 = jnp.min(jnp.where(le2 == m2, lane, big), axis=-1, keepdims=True)
    t = jnp.exp(m2 - m1)
    w1 = p_g / (1.0 + t)
    w2 = p_g * t / (1.0 + t)

    oh1 = lane == i1
    oh2 = lane == i2
    oh = jnp.where(oh1 | oh2, 1.0, 0.0)
    rr = lax.broadcasted_iota(I32, (tm, tm), 0)
    cc = lax.broadcasted_iota(I32, (tm, tm), 1)
    tri = jnp.where(cc < rr, 1.0, 0.0).astype(BF16)
    before = _dot(tri, oh.astype(BF16)) + cnt_acc[...]
    r1 = jnp.sum(jnp.where(oh1, before, 0.0), axis=-1, keepdims=True)
    r2 = jnp.sum(jnp.where(oh2, before, 0.0), axis=-1, keepdims=True)
    cnt_new = cnt_acc[...] + jnp.sum(oh, axis=0, keepdims=True)
    cnt_acc[...] = cnt_new
    cnt_ref[...] = cnt_new.astype(I32)

    mw_ref[...] = jnp.where(lane == 0, w1, jnp.where(lane == 1, w2, 0.0))
    ints = jnp.where(lane == 0, i1,
                     jnp.where(lane == 1, i2,
                               jnp.where(lane == 2, r1, jnp.where(lane == 3, r2, 0.0))))
    mi_ref[...] = ints.T[0:8, :].astype(I32)


def _post_mixer(x, a_list, wo_list, b_out, g_x, w_q, mkv, w_xo, g_f, w_rt, b_rt):
    b, s, d = x.shape
    tm = ROW_TILE
    ns = s // tm
    t = b * s
    n_a = len(a_list)
    has_bias = b_out is not None
    const2 = lambda bi, si: (0, 0)
    row = lambda width: pl.BlockSpec((1, tm, width), lambda bi, si: (bi, si, 0))
    in_specs = [row(d)] + [row(a.shape[2]) for a in a_list]
    in_specs += [pl.BlockSpec(w.shape, const2) for w in wo_list]
    args = [x, *a_list, *wo_list]
    if has_bias:
        in_specs.append(pl.BlockSpec((1, d), const2))
        args.append(b_out.reshape(1, d))
    xw = w_q.shape[1]
    in_specs += [pl.BlockSpec((1, d), const2),
                 pl.BlockSpec((d, xw), const2),
                 pl.BlockSpec((1, mkv.shape[1], mkv.shape[2]), lambda bi, si: (bi, 0, 0)),
                 pl.BlockSpec((xw, d), const2),
                 pl.BlockSpec((1, d), const2),
                 pl.BlockSpec((2, d, LANES), lambda bi, si: (0, 0, 0)),
                 pl.BlockSpec((1, LANES), const2)]
    args += [g_x.reshape(1, d), w_q, mkv, w_xo, g_f.reshape(1, d), w_rt, b_rt]
    out_specs = [row(d),
                 pl.BlockSpec((tm * SUBLANES, LANES), lambda bi, si: (bi * ns + si, 0)),
                 pl.BlockSpec((8, tm), lambda bi, si: (0, bi * ns + si)),
                 pl.BlockSpec((tm, LANES), lambda bi, si: (bi * ns + si, 0)),
                 pl.BlockSpec((1, LANES), const2)]
    out_shape = [jax.ShapeDtypeStruct((b, s, d), F32),
                 jax.ShapeDtypeStruct((t * SUBLANES, LANES), F32),
                 jax.ShapeDtypeStruct((8, t), I32),
                 jax.ShapeDtypeStruct((t, LANES), F32),
                 jax.ShapeDtypeStruct((1, LANES), I32)]
    return pl.pallas_call(
        functools.partial(_post_kernel, tm=tm, n_a=n_a, has_bias=has_bias),
        grid=(b, ns),
        in_specs=in_specs,
        out_specs=out_specs,
        out_shape=out_shape,
        scratch_shapes=[pltpu.VMEM((1, LANES), F32)],
        compiler_params=_cparams(("arbitrary", "arbitrary")),
        name="post_mixer",
    )(*args)


def _to_token_tiles(dst_ref, val, rows, tok0=0):
    for j in range(SUBLANES):
        dst_ref[pl.ds(tok0 * SUBLANES + j, rows, stride=SUBLANES), :] = val[:, j * LANES:(j + 1) * LANES]


def _from_token_tiles(src_ref, rows, tok0=0):
    return jnp.concatenate([src_ref[pl.ds(tok0 * SUBLANES + j, rows, stride=SUBLANES), :]
                            for j in range(SUBLANES)], axis=1)


def _token_slice(ref, tok):
    return ref.at[pl.ds(pl.multiple_of(tok * SUBLANES, SUBLANES), SUBLANES)]


def _pos_kernel(off_ref, mi_ref, pos_ref):
    e = mi_ref[0:TOP_K, :]
    pos = mi_ref[TOP_K:2 * TOP_K, :]
    for j in range(N_EXPERTS):
        pos = pos + jnp.where(e == j, off_ref[j], 0)
    pos_ref[...] = pos


def _sorted_positions(off, meta_i):
    t = meta_i.shape[1]
    return pl.pallas_call(
        _pos_kernel,
        grid_spec=pltpu.PrefetchScalarGridSpec(
            num_scalar_prefetch=1,
            grid=(1,),
            in_specs=[pl.BlockSpec((8, t), lambda i, off: (0, 0))],
            out_specs=pl.BlockSpec((TOP_K, t), lambda i, off: (0, 0))),
        out_shape=jax.ShapeDtypeStruct((TOP_K, t), I32),
        compiler_params=_cparams(("arbitrary",)),
        name="moe_positions",
    )(off, meta_i)


def _tile_major(pos, tm):
    t = pos.shape[1]
    return pos.reshape(TOP_K, t // tm, tm).transpose(1, 0, 2).reshape(-1)


def _dispatch_kernel(pos_ref, h_ref, xs_ref, sem, *, tm):
    def issue(g, _):
        for u in range(DMA_UNROLL):
            r = g * DMA_UNROLL + u
            for kk in range(TOP_K):
                pos = pos_ref[kk * tm + r]
                pltpu.make_async_copy(_token_slice(h_ref, r), _token_slice(xs_ref, pos), sem).start(priority=kk)
        return 0

    lax.fori_loop(0, tm // DMA_UNROLL, issue, 0)
    for _ in range(TOP_K):
        pltpu.make_async_copy(h_ref, xs_ref.at[pl.ds(0, tm * SUBLANES)], sem).wait()


def _dispatch(pos, h2t, n_rows):
    tm = DISPATCH_TILE
    t = h2t.shape[0] // SUBLANES
    return pl.pallas_call(
        functools.partial(_dispatch_kernel, tm=tm),
        grid=(t // tm,),
        in_specs=[pl.BlockSpec((TOP_K * tm,), lambda i: (i,), memory_space=pltpu.SMEM),
                  pl.BlockSpec((tm * SUBLANES, LANES), lambda i: (i, 0))],
        out_specs=pl.BlockSpec(memory_space=pl.ANY),
        scratch_shapes=[pltpu.SemaphoreType.DMA],
        out_shape=jax.ShapeDtypeStruct((n_rows * SUBLANES, LANES), F32),
        compiler_params=_cparams(("arbitrary",)),
        name="moe_dispatch",
    )(_tile_major(pos, tm), h2t)


def _expert_kernel(tblk_ref, texp_ref, tn_ref, xs_ref, wgu_ref, wdn_ref, ys_ref, wgu_bf, wdn_bf, *, tm):
    i = pl.program_id(0)
    n_valid = tn_ref[i]
    new_expert = (i == 0) | (texp_ref[i] != texp_ref[jnp.maximum(i - 1, 0)])

    @pl.when(new_expert)
    def _():
        wgu_bf[...] = wgu_ref[0, 0].astype(BF16)
        wdn_bf[...] = wdn_ref[0, 0].astype(BF16)

    @pl.when(n_valid > 0)
    def _():
        rh = tm // EXPERT_CHAINS
        starts = [c * rh for c in range(EXPERT_CHAINS)]
        row = lax.broadcasted_iota(I32, (rh, 1), 0)
        gus = []
        for r0 in starts:
            x = jnp.where(row + r0 < n_valid, _from_token_tiles(xs_ref, rh, r0), 0.0).astype(BF16)
            gus.append(_dot(x, wgu_bf[...]))
        ys = []
        for gu in gus:
            hid = (gu[:, :EXPERT_HIDDEN] / (1.0 + jnp.exp(-gu[:, :EXPERT_HIDDEN])) * gu[:, EXPERT_HIDDEN:]).astype(BF16)
            ys.append(_dot(hid, wdn_bf[...]))
        for r0, y in zip(starts, ys):
            _to_token_tiles(ys_ref, y, rh, r0)


def _experts(tile_blk, tile_exp, tile_n, xs, w_gu, w_dn, layer):
    tm = EXPERT_TILE
    nt = xs.shape[0] // (tm * SUBLANES)
    gu_shape, dn_shape = w_gu.shape[2:], w_dn.shape[2:]
    return pl.pallas_call(
        functools.partial(_expert_kernel, tm=tm),
        grid_spec=pltpu.PrefetchScalarGridSpec(
            num_scalar_prefetch=3,
            grid=(nt,),
            in_specs=[pl.BlockSpec((tm * SUBLANES, LANES), lambda i, tb, te, tn: (tb[i], 0)),
                      pl.BlockSpec((1, 1) + gu_shape, lambda i, tb, te, tn: (layer, te[i], 0, 0)),
                      pl.BlockSpec((1, 1) + dn_shape, lambda i, tb, te, tn: (layer, te[i], 0, 0))],
            out_specs=pl.BlockSpec((tm * SUBLANES, LANES), lambda i, tb, te, tn: (tb[i], 0)),
            scratch_shapes=[pltpu.VMEM(gu_shape, BF16), pltpu.VMEM(dn_shape, BF16)]),
        out_shape=jax.ShapeDtypeStruct(xs.shape, F32),
        compiler_params=_cparams(("arbitrary",)),
        name="moe_experts",
    )(tile_blk, tile_exp, tile_n, xs, w_gu, w_dn)


def _moe_combine_tile(pos_cur, pos_nxt, mw_ref, x, ys_ref, ybuf, sems, i, n, tm):
    def gather(pos_ref, slot):
        def issue(g, _):
            for u in range(DMA_UNROLL):
                r = g * DMA_UNROLL + u
                for kk in range(TOP_K):
                    pos = pos_ref[kk * tm + r]
                    pltpu.make_async_copy(_token_slice(ys_ref, pos), _token_slice(ybuf.at[slot, kk], r),
                                          sems.at[slot]).start(priority=kk)
            return 0

        lax.fori_loop(0, tm // DMA_UNROLL, issue, 0)

    @pl.when(i == 0)
    def _():
        gather(pos_cur, 0)

    @pl.when(i + 1 < n)
    def _():
        gather(pos_nxt, (i + 1) % 2)

    slot = i % 2
    for kk in range(TOP_K):
        pltpu.make_async_copy(ys_ref.at[pl.ds(0, tm * SUBLANES)], ybuf.at[slot, kk], sems.at[slot]).wait()
    mw = mw_ref[...]
    return (x + mw[:, 0:1] * _from_token_tiles(ybuf.at[slot, 0], tm)
            + mw[:, 1:2] * _from_token_tiles(ybuf.at[slot, 1], tm))


def _combine_kernel(pos_cur, pos_nxt, mw_ref, x_ref, ys_ref, *rest, tm, final):
    if final:
        g_ref, o_ref, ybuf, sems = rest
    else:
        o_ref, ybuf, sems = rest
    x3 = _moe_combine_tile(pos_cur, pos_nxt, mw_ref, x_ref[...], ys_ref, ybuf, sems,
                           pl.program_id(0), pl.num_programs(0), tm)
    if final:
        x3 = _rms(x3, g_ref[...])
    o_ref[...] = x3


def _combine(moe, g_final):
    pos, meta_w, x2, ys = moe
    d = x2.shape[-1]
    x2d = x2.reshape(-1, d)
    t = x2d.shape[0]
    tm = COMBINE_TILE
    nt = t // tm
    final = g_final is not None
    in_specs = [pl.BlockSpec((TOP_K * tm,), lambda i: (i,), memory_space=pltpu.SMEM),
                pl.BlockSpec((TOP_K * tm,), lambda i: (jnp.minimum(i + 1, nt - 1),), memory_space=pltpu.SMEM),
                pl.BlockSpec((tm, LANES), lambda i: (i, 0)),
                pl.BlockSpec((tm, d), lambda i: (i, 0)),
                pl.BlockSpec(memory_space=pl.ANY)]
    pos_flat = _tile_major(pos, tm)
    args = [pos_flat, pos_flat, meta_w, x2d, ys]
    if final:
        in_specs.append(pl.BlockSpec((1, d), lambda i: (0, 0)))
        args.append(g_final.reshape(1, d))
    return pl.pallas_call(
        functools.partial(_combine_kernel, tm=tm, final=final),
        grid=(nt,),
        in_specs=in_specs,
        out_specs=pl.BlockSpec((tm, d), lambda i: (i, 0)),
        scratch_shapes=[pltpu.VMEM((2, TOP_K, tm * SUBLANES, LANES), F32), pltpu.SemaphoreType.DMA((2,))],
        out_shape=jax.ShapeDtypeStruct((t, d), F32),
        compiler_params=_cparams(("arbitrary",)),
        name="moe_combine",
    )(*args)


def _plan_kernel(cnt_ref, off_ref, blk_ref, exp_ref, nv_ref, *, tm, n_tiles):
    shift = tm.bit_length() - 1

    def per_expert(e, row0):
        c = cnt_ref[0, e]
        ntile = lax.shift_right_logical(c + (tm - 1), shift)
        off_ref[e] = row0
        t0 = lax.shift_right_logical(row0, shift)

        def fill(j, _):
            blk_ref[t0 + j] = t0 + j
            exp_ref[t0 + j] = e
            nv_ref[t0 + j] = jnp.minimum(c - j * tm, tm)
            return 0

        lax.fori_loop(0, ntile, fill, 0)
        return row0 + lax.shift_left(ntile, shift)

    total = lax.fori_loop(0, N_EXPERTS, per_expert, jnp.int32(0))
    used = lax.shift_right_logical(total, shift)
    last = jnp.maximum(used - 1, 0)

    def tail(i, _):
        blk_ref[i] = last
        exp_ref[i] = exp_ref[last]
        nv_ref[i] = 0
        return 0

    lax.fori_loop(used, n_tiles, tail, 0)


def _moe_plan(counts, n_tiles):
    tm = EXPERT_TILE
    assert tm & (tm - 1) == 0
    smem = pl.BlockSpec(memory_space=pltpu.SMEM)
    return pl.pallas_call(
        functools.partial(_plan_kernel, tm=tm, n_tiles=n_tiles),
        in_specs=[smem],
        out_specs=[smem, smem, smem, smem],
        out_shape=[jax.ShapeDtypeStruct((N_EXPERTS,), I32)] + [jax.ShapeDtypeStruct((n_tiles,), I32)] * 3,
        name="moe_plan",
    )(counts)


def _moe(x2, h2t, meta_i, meta_w, counts, w_gu, w_dn, layer):
    b, s, d = x2.shape
    assert d == SUBLANES * LANES
    n_rows = b * s * TOP_K + N_EXPERTS * EXPERT_TILE
    off, blk, exp, n_valid = _moe_plan(counts, n_rows // EXPERT_TILE)
    pos = _sorted_positions(off, meta_i)
    xs = _dispatch(pos, h2t, n_rows)
    ys = _experts(blk, exp, n_valid, xs, w_gu, w_dn, layer)
    return pos, meta_w, x2, ys


def _router_weights(w_group, b_group, w_router, b_router):
    d = w_group.shape[0]
    pad = LANES - N_EXPERTS - N_GROUPS
    w = jnp.concatenate([w_router, w_group, jnp.zeros((d, pad), F32)], axis=1)
    bias = jnp.concatenate([b_router, b_group, jnp.zeros((pad,), F32)]).reshape(1, LANES)
    w_hi = w.astype(BF16)
    w_lo = (w - w_hi.astype(F32)).astype(BF16)
    return jnp.stack([w_hi, w_lo]), bias


def kernel(x, mem, mem_norm, mem_w_kv, norm_mix, norm_xattn, norm_ffn, hyb_w_in, hyb_conv_w, diff_lambda, diff_subln, hyb_w_out, swa_w_qkv, swa_b_qkv, swa_sinks, swa_w_out, swa_b_out, xattn_w_q, xattn_w_o, moe_w_group, moe_b_group, moe_w_router, moe_b_router, moe_w_gate_up, moe_w_down, final_norm):
    b, s, d = x.shape
    m = mem.shape[1]
    depth = norm_mix.shape[0]
    mkv = _norm_proj(mem.reshape(b * m, d), mem_norm, mem_w_kv.astype(BF16), m).reshape(b, m, -1)

    scale = DIFF_DK ** -0.5 * LOG2E
    cq, suq, sdq, half = _rope_lane_tables(s, DIFF_DK, scale)
    ck, suk, sdk, _ = _rope_lane_tables(s, DIFF_DK, 1.0)
    tabs_q, tabs_k = (cq, suq, sdq), (ck, suk, sdk)

    moe = None
    for l in range(depth):
        if moe is not None:
            x = _combine(moe, None).reshape(b, s, d)
        if l % 2 == 0:
            e = l // 2
            lambda_init = 0.8 - 0.6 * math.exp(-0.3 * l)
            ya, q, k, v = _hyb_front(x, norm_mix[l], hyb_w_in[e].astype(BF16), hyb_conv_w[e], tabs_q, tabs_k, half)
            o = _diff_attn(q, k, v, diff_lambda[e], diff_subln[e], lambda_init)
            w_out = hyb_w_out[e].astype(BF16)
            a_list, wo_list, b_out = [ya, o], [w_out[:CONV_CH], w_out[CONV_CH:]], None
        else:
            e = l // 2
            order = jnp.asarray(_swa_head_order(), I32)
            nq = SWA_Q_HEADS * SWA_HEAD_DIM
            cols = (order[:, None] * SWA_HEAD_DIM + jnp.arange(SWA_HEAD_DIM, dtype=I32)[None, :]).reshape(-1)
            cols_all = jnp.concatenate([cols, jnp.arange(nq, swa_w_qkv.shape[2], dtype=I32)])
            w_qkv = jnp.take(swa_w_qkv[e], cols_all, axis=1).astype(BF16)
            b_qkv = jnp.take(swa_b_qkv[e], cols_all)
            q, kv = _swa_front(x, norm_mix[l], w_qkv, b_qkv, tabs_q, tabs_k, half)
            o = _swa_attn(q, kv, jnp.take(swa_sinks[e], order))
            a_list, wo_list, b_out = [o], [jnp.take(swa_w_out[e], cols, axis=0).astype(BF16)], swa_b_out[e]
        w_rt, b_rt = _router_weights(moe_w_group[l], moe_b_group[l], moe_w_router[l], moe_b_router[l])
        x2, h2, meta_i, meta_w, counts = _post_mixer(
            x, a_list, wo_list, b_out, norm_xattn[l], xattn_w_q[l].astype(BF16), mkv,
            xattn_w_o[l].astype(BF16), norm_ffn[l], w_rt, b_rt)
        moe = _moe(x2, h2, meta_i, meta_w, counts, moe_w_gate_up, moe_w_down, l)
    return _combine(moe, final_norm).reshape(b, s, d)
```

```python
import functools
import math

import jax
import jax.numpy as jnp
from jax import lax
from jax.experimental import pallas as pl
from jax.experimental.pallas import tpu as pltpu

F32 = jnp.float32
BF16 = jnp.bfloat16
I32 = jnp.int32
U32 = jnp.uint32

EPS = 1e-6
LANES = 128
SUBLANES = 8
TOKEN_ROWS = 4
DMA_UNROLL = 8
VMEM_LIMIT = 56 * 1024 * 1024

ROPE_THETA = 500000.0
ROPE_FRACTION = 4
BLOCK = 128
CONV_CH = 512
CONV_K = 3
DIFF_HEADS = 4
DIFF_DK = 64
DIFF_DV = 128
SWA_Q_HEADS = 16
SWA_KV_HEADS = 4
SWA_HEAD_DIM = 64
SWA_WINDOW = 128
XATTN_HEADS = 4
XATTN_HEAD_DIM = 128
N_GROUPS = 4
EXPERTS_PER_GROUP = 8
N_EXPERTS = N_GROUPS * EXPERTS_PER_GROUP
TOP_K = 2
EXPERT_HIDDEN = 512

ROW_TILE = 512
EXPERT_TILE = 512
DISPATCH_TILE = 1024
COMBINE_TILE = 512
POST_CHAINS = 2
EXPERT_CHAINS = 2
FRONT_CHAINS = 2
NEG_INF = float("-inf")
LOG2E = math.log2(math.e)


def _cparams(sem):
    return pltpu.CompilerParams(dimension_semantics=sem, vmem_limit_bytes=VMEM_LIMIT)


def _rms(x, g):
    return x * lax.rsqrt(jnp.mean(x * x, axis=-1, keepdims=True) + EPS) * g


def _dot(a, b):
    return jnp.dot(a, b, preferred_element_type=F32)


def _dot_nt(a, b):
    return lax.dot_general(a, b, (((1,), (1,)), ((), ())), preferred_element_type=F32)


def _rope_lane_tables(seq, head_dim, scale):
    rot = head_dim // ROPE_FRACTION
    half = rot // 2
    pos = jnp.arange(seq, dtype=F32)
    inv = ROPE_THETA ** (-jnp.arange(0, rot, 2, dtype=F32) / rot)
    ang = pos[:, None] * inv[None, :]
    cos, sin = jnp.cos(ang), jnp.sin(ang)
    idx = jnp.arange(LANES) % head_dim
    cl = jnp.take(cos, idx % half, axis=1)
    sl = jnp.take(sin, idx % half, axis=1)
    c = jnp.where(idx < rot, cl, 1.0) * scale
    s_up = jnp.where(idx < half, -sl, 0.0) * scale
    s_dn = jnp.where((idx >= half) & (idx < rot), sl, 0.0) * scale
    return c.astype(F32), s_up.astype(F32), s_dn.astype(F32), half


def _rope_chunk(xc, c, s_up, s_dn, half):
    return (xc * c + pltpu.roll(xc, LANES - half, 1) * s_up + pltpu.roll(xc, half, 1) * s_dn)


def _norm_proj_kernel(x_ref, g_ref, w_ref, o_ref):
    h = _rms(x_ref[...], g_ref[...]).astype(BF16)
    o_ref[...] = _dot(h, w_ref[...]).astype(o_ref.dtype)


def _norm_proj(x2d, g, w_bf16, tm):
    m, d = x2d.shape
    n = w_bf16.shape[1]
    return pl.pallas_call(
        _norm_proj_kernel,
        grid=(m // tm,),
        in_specs=[pl.BlockSpec((tm, d), lambda i: (i, 0)),
                  pl.BlockSpec((1, d), lambda i: (0, 0)),
                  pl.BlockSpec((d, n), lambda i: (0, 0))],
        out_specs=pl.BlockSpec((tm, n), lambda i: (i, 0)),
        out_shape=jax.ShapeDtypeStruct((m, n), BF16),
        compiler_params=_cparams(("arbitrary",)),
        name="mem_kv_proj",
    )(x2d, g.reshape(1, d), w_bf16)


def _hyb_front_kernel(x_ref, g_ref, w_ref, cw_ref, cq_ref, suq_ref, sdq_ref, ck_ref, suk_ref, sdk_ref,
                      ya_ref, q_ref, k_ref, v_ref, cbuf, *, tm, half):
    s = pl.program_id(1)
    c = CONV_CH
    base = 3 * c
    nq = DIFF_HEADS * 2 * DIFF_DK
    rh = tm // FRONT_CHAINS
    starts = [ch * rh for ch in range(FRONT_CHAINS)]
    cw = cw_ref[...]

    @pl.when(s == 0)
    def _():
        cbuf[0:8, :] = jnp.zeros((8, c), F32)

    hs = [_rms(x_ref[0, r0:r0 + rh, :], g_ref[...]).astype(BF16) for r0 in starts]
    for r0, h in zip(starts, hs):
        gate_b = _dot(h, w_ref[:, 0:c])
        cu = _dot(h, w_ref[:, c:2 * c]) * _dot(h, w_ref[:, 2 * c:3 * c])
        cbuf[8 + r0:8 + r0 + rh, :] = cu
        conv = (cw[0:1, :] * cbuf[6 + r0:6 + r0 + rh, :] + cw[1:2, :] * cbuf[7 + r0:7 + r0 + rh, :] + cw[2:3, :] * cu)
        ya_ref[0, r0:r0 + rh, :] = (gate_b * conv).astype(BF16)
    cbuf[0:8, :] = cbuf[tm:tm + 8, :]

    for r0, h in zip(starts, hs):
        cq, suq, sdq = cq_ref[r0:r0 + rh, :], suq_ref[r0:r0 + rh, :], sdq_ref[r0:r0 + rh, :]
        pq = _dot(h, w_ref[:, base:base + nq])
        for j in range(nq // LANES):
            q_ref[0, r0:r0 + rh, j * LANES:(j + 1) * LANES] = _rope_chunk(
                pq[:, j * LANES:(j + 1) * LANES], cq, suq, sdq, half).astype(BF16)
    for r0, h in zip(starts, hs):
        ck, suk, sdk = ck_ref[r0:r0 + rh, :], suk_ref[r0:r0 + rh, :], sdk_ref[r0:r0 + rh, :]
        pk = _dot(h, w_ref[:, base + nq:base + 2 * nq])
        for j in range(nq // LANES):
            k_ref[0, r0:r0 + rh, j * LANES:(j + 1) * LANES] = _rope_chunk(
                pk[:, j * LANES:(j + 1) * LANES], ck, suk, sdk, half).astype(BF16)
    for r0, h in zip(starts, hs):
        v_ref[0, r0:r0 + rh, :] = _dot(h, w_ref[:, base + 2 * nq:]).astype(BF16)


def _hyb_front(x, g, w_in, conv_w, tabs_q, tabs_k, half):
    b, s, d = x.shape
    tm = ROW_TILE
    n = w_in.shape[1]
    nq = DIFF_HEADS * 2 * DIFF_DK
    nv = DIFF_HEADS * DIFF_DV
    tab_spec = pl.BlockSpec((tm, LANES), lambda bi, si: (si, 0))
    row = lambda width: pl.BlockSpec((1, tm, width), lambda bi, si: (bi, si, 0))
    return pl.pallas_call(
        functools.partial(_hyb_front_kernel, tm=tm, half=half),
        grid=(b, s // tm),
        in_specs=[row(d),
                  pl.BlockSpec((1, d), lambda bi, si: (0, 0)),
                  pl.BlockSpec((d, n), lambda bi, si: (0, 0)),
                  pl.BlockSpec((CONV_K, CONV_CH), lambda bi, si: (0, 0)),
                  tab_spec, tab_spec, tab_spec, tab_spec, tab_spec, tab_spec],
        out_specs=[row(CONV_CH), row(nq), row(nq), row(nv)],
        out_shape=[jax.ShapeDtypeStruct((b, s, CONV_CH), BF16),
                   jax.ShapeDtypeStruct((b, s, nq), BF16),
                   jax.ShapeDtypeStruct((b, s, nq), BF16),
                   jax.ShapeDtypeStruct((b, s, nv), BF16)],
        scratch_shapes=[pltpu.VMEM((tm + 8, CONV_CH), F32)],
        compiler_params=_cparams(("arbitrary", "arbitrary")),
        name="hyb_front",
    )(x, g.reshape(1, d), w_in, conv_w, *tabs_q, *tabs_k)


def _lane_fold(x, op):
    r = x[:, 0:LANES]
    for c in range(1, x.shape[1] // LANES):
        r = op(r, x[:, c * LANES:(c + 1) * LANES])
    return r


def _diff_attn_kernel(q_ref, k_ref, v_ref, lam_ref, g_ref, o_ref, sbuf, stat, acc, *, tq, hp, lambda_init):
    i = pl.program_id(2)
    lane = lax.broadcasted_iota(I32, (1, LANES), 1)
    qs = []
    for h in range(hp):
        q = q_ref[0, :, h * LANES:(h + 1) * LANES]
        zero = jnp.zeros_like(q)
        qs += [jnp.where(lane < DIFF_DK, q, zero), jnp.where(lane >= DIFF_DK, q, zero)]
    nc = 2 * hp

    def scores(j, masked):
        for c in range(nc):
            h = c // 2
            kb = k_ref[0, pl.ds(pl.multiple_of(j * tq, tq), tq), h * LANES:(h + 1) * LANES]
            sc = _dot_nt(qs[c], kb)
            if masked:
                r = lax.broadcasted_iota(I32, (tq, tq), 0)
                cc = lax.broadcasted_iota(I32, (tq, tq), 1)
                sc = jnp.where(cc <= r, sc, NEG_INF)
            sbuf[c, j] = sc
            stat[c] = jnp.maximum(stat[c], _lane_fold(sc, jnp.maximum))

    stat[...] = jnp.full(stat.shape, NEG_INF, F32)

    def pass1(j, carry):
        scores(j, False)
        return carry

    lax.fori_loop(0, i, pass1, 0)
    scores(i, True)
    ms = [jnp.max(stat[c], axis=-1, keepdims=True) for c in range(nc)]

    ones = jnp.ones((tq, LANES), BF16)

    def pass2(j, first):
        for h in range(hp):
            vb = v_ref[0, pl.ds(pl.multiple_of(j * tq, tq), tq), h * LANES:(h + 1) * LANES]
            v_ext = jnp.concatenate([vb, ones], axis=1)
            p0 = jnp.exp2(sbuf[2 * h, j] - ms[2 * h])
            p1 = jnp.exp2(sbuf[2 * h + 1, j] - ms[2 * h + 1])
            upd = _dot(jnp.concatenate([p0, p1], axis=0).astype(BF16), v_ext)
            if first:
                acc[h] = upd
            else:
                acc[h] += upd
        return 0

    pass2(0, True)
    lax.fori_loop(1, i + 1, lambda j, c: pass2(j, False), 0)

    lf = lam_ref[...]
    lam = (jnp.exp(jnp.sum(lf[0:1] * lf[1:2], keepdims=True))
           - jnp.exp(jnp.sum(lf[2:3] * lf[3:4], keepdims=True)) + lambda_init)
    for h in range(hp):
        a0 = acc[h, 0:tq, :]
        a1 = acc[h, tq:2 * tq, :]
        o = a0[:, :LANES] / a0[:, LANES:] - lam * (a1[:, :LANES] / a1[:, LANES:])
        o_ref[0, :, h * LANES:(h + 1) * LANES] = (_rms(o, g_ref[...]) * (1.0 - lambda_init)).astype(BF16)


def _diff_attn(q, k, v, lam_vecs, subln_g, lambda_init):
    b, s, _ = q.shape
    tq = 512
    hp = 2
    blk = lambda bi, hi, qi: (bi, qi, hi)
    full = lambda bi, hi, qi: (bi, 0, hi)
    return pl.pallas_call(
        functools.partial(_diff_attn_kernel, tq=tq, hp=hp, lambda_init=lambda_init),
        grid=(b, DIFF_HEADS // hp, s // tq),
        in_specs=[pl.BlockSpec((1, tq, hp * LANES), blk),
                  pl.BlockSpec((1, s, hp * LANES), full),
                  pl.BlockSpec((1, s, hp * LANES), full),
                  pl.BlockSpec((4, DIFF_DK), lambda bi, hi, qi: (0, 0)),
                  pl.BlockSpec((1, DIFF_DV), lambda bi, hi, qi: (0, 0))],
        out_specs=pl.BlockSpec((1, tq, hp * LANES), blk),
        out_shape=jax.ShapeDtypeStruct((b, s, DIFF_HEADS * DIFF_DV), BF16),
        scratch_shapes=[pltpu.VMEM((2 * hp, s // tq, tq, tq), F32),
                        pltpu.VMEM((2 * hp, tq, LANES), F32),
                        pltpu.VMEM((hp, 2 * tq, DIFF_DV + LANES), F32)],
        compiler_params=_cparams(("arbitrary", "arbitrary", "arbitrary")),
        name="diff_attn",
    )(q, k, v, lam_vecs, subln_g.reshape(1, DIFF_DV))


def _swa_front_kernel(x_ref, g_ref, w_ref, b_ref, cq_ref, suq_ref, sdq_ref, ck_ref, suk_ref, sdk_ref,
                      q_ref, kv_ref, *, half):
    nq = SWA_Q_HEADS * SWA_HEAD_DIM
    nkv = SWA_KV_HEADS * SWA_HEAD_DIM
    tm = x_ref.shape[1]
    rh = tm // FRONT_CHAINS
    starts = [ch * rh for ch in range(FRONT_CHAINS)]
    hs = [_rms(x_ref[0, r0:r0 + rh, :], g_ref[...]).astype(BF16) for r0 in starts]
    for r0, h in zip(starts, hs):
        cq, suq, sdq = cq_ref[r0:r0 + rh, :], suq_ref[r0:r0 + rh, :], sdq_ref[r0:r0 + rh, :]
        pq = _dot(h, w_ref[:, 0:nq]) + b_ref[:, 0:nq]
        for j in range(nq // LANES):
            lo = j * LANES
            q_ref[0, r0:r0 + rh, lo:lo + LANES] = _rope_chunk(pq[:, lo:lo + LANES], cq, suq, sdq, half).astype(BF16)
    for r0, h in zip(starts, hs):
        ck, suk, sdk = ck_ref[r0:r0 + rh, :], suk_ref[r0:r0 + rh, :], sdk_ref[r0:r0 + rh, :]
        pkv = _dot(h, w_ref[:, nq:]) + b_ref[:, nq:]
        for j in range(nkv // LANES):
            lo = j * LANES
            kv_ref[0, r0:r0 + rh, lo:lo + LANES] = _rope_chunk(pkv[:, lo:lo + LANES], ck, suk, sdk, half).astype(BF16)
        kv_ref[0, r0:r0 + rh, nkv:] = pkv[:, nkv:].astype(BF16)


def _swa_front(x, g, w_qkv, b_qkv, tabs_q, tabs_k, half):
    b, s, d = x.shape
    tm = ROW_TILE
    n = w_qkv.shape[1]
    nq = SWA_Q_HEADS * SWA_HEAD_DIM
    nkv = SWA_KV_HEADS * SWA_HEAD_DIM
    tab_spec = pl.BlockSpec((tm, LANES), lambda bi, si: (si, 0))
    row = lambda width: pl.BlockSpec((1, tm, width), lambda bi, si: (bi, si, 0))
    return pl.pallas_call(
        functools.partial(_swa_front_kernel, half=half),
        grid=(b, s // tm),
        in_specs=[row(d),
                  pl.BlockSpec((1, d), lambda bi, si: (0, 0)),
                  pl.BlockSpec((d, n), lambda bi, si: (0, 0)),
                  pl.BlockSpec((1, n), lambda bi, si: (0, 0)),
                  tab_spec, tab_spec, tab_spec, tab_spec, tab_spec, tab_spec],
        out_specs=[row(nq), row(2 * nkv)],
        out_shape=[jax.ShapeDtypeStruct((b, s, nq), BF16),
                   jax.ShapeDtypeStruct((b, s, 2 * nkv), BF16)],
        compiler_params=_cparams(("arbitrary", "arbitrary")),
        name="swa_front",
    )(x, g.reshape(1, d), w_qkv, b_qkv.reshape(1, n), *tabs_q, *tabs_k)


def _swa_head_order():
    g_sz = SWA_Q_HEADS // SWA_KV_HEADS
    order = []
    for slab in range(SWA_Q_HEADS // 2):
        pair, j = slab // g_sz, slab % g_sz
        order += [(2 * pair) * g_sz + j, (2 * pair + 1) * g_sz + j]
    return order


def _swa_attn_kernel(sink_ref, q_ref, kvp_ref, kvc_ref, o_ref, *, nblk):
    i = pl.program_id(1)
    hd = SWA_HEAD_DIM
    nkv = SWA_KV_HEADS * hd
    g_sz = SWA_Q_HEADS // SWA_KV_HEADS
    kv = jnp.concatenate([kvp_ref[0], kvc_ref[0]], axis=0)
    r = lax.broadcasted_iota(I32, (BLOCK, 2 * BLOCK), 0)
    c = lax.broadcasted_iota(I32, (BLOCK, 2 * BLOCK), 1)
    rel = c - BLOCK - r
    in_win = (rel <= 0) & (rel > -SWA_WINDOW)
    lane = lax.broadcasted_iota(I32, (1, LANES), 1)
    lo_half = lane < hd
    ones = jnp.ones((2 * BLOCK, LANES), BF16)
    for n in range(nblk):
        mask = in_win & ((c >= BLOCK) | (i > 0)) if n == 0 else in_win
        keys = kv[n * BLOCK:(n + 2) * BLOCK]
        for pair in range(SWA_KV_HEADS // 2):
            k2 = keys[:, pair * LANES:(pair + 1) * LANES]
            v2 = keys[:, nkv + pair * LANES:nkv + (pair + 1) * LANES]
            v_ext = jnp.concatenate([v2, ones], axis=1)
            pieces = []
            for j in range(g_sz):
                slab = pair * g_sz + j
                qs = q_ref[0, n * BLOCK:(n + 1) * BLOCK, slab * LANES:(slab + 1) * LANES]
                zero = jnp.zeros_like(qs)
                pieces += [jnp.where(lo_half, qs, zero), jnp.where(lo_half, zero, qs)]
            sc = _dot_nt(jnp.concatenate(pieces, axis=0), k2)
            probs, tails = [], []
            for pc in range(2 * g_sz):
                scp = jnp.where(mask, sc[pc * BLOCK:(pc + 1) * BLOCK], NEG_INF)
                sink = sink_ref[2 * g_sz * pair + pc] * LOG2E
                m = jnp.maximum(jnp.max(scp, axis=-1, keepdims=True), sink)
                probs.append(jnp.exp2(scp - m).astype(BF16))
                tails.append(jnp.exp2(sink - m))
            pv = _dot(jnp.concatenate(probs, axis=0), v_ext)
            for j in range(g_sz):
                slab = pair * g_sz + j
                halves = []
                for hf in range(2):
                    pc = 2 * j + hf
                    blk = pv[pc * BLOCK:(pc + 1) * BLOCK]
                    halves.append(blk[:, :LANES] / (blk[:, LANES:] + tails[pc]))
                o_ref[0, n * BLOCK:(n + 1) * BLOCK, slab * LANES:(slab + 1) * LANES] = (
                    jnp.where(lo_half, halves[0], halves[1]).astype(BF16))


def _swa_attn(q, kv, sinks):
    b, s, nq = q.shape
    nblk = 2
    tq = nblk * BLOCK
    return pl.pallas_call(
        functools.partial(_swa_attn_kernel, nblk=nblk),
        grid_spec=pltpu.PrefetchScalarGridSpec(
            num_scalar_prefetch=1,
            grid=(b, s // tq),
            in_specs=[pl.BlockSpec((1, tq, nq), lambda bi, ni, sk: (bi, ni, 0)),
                      pl.BlockSpec((1, BLOCK, kv.shape[2]), lambda bi, ni, sk: (bi, jnp.maximum(ni * nblk - 1, 0), 0)),
                      pl.BlockSpec((1, tq, kv.shape[2]), lambda bi, ni, sk: (bi, ni, 0))],
            out_specs=pl.BlockSpec((1, tq, nq), lambda bi, ni, sk: (bi, ni, 0))),
        out_shape=jax.ShapeDtypeStruct((b, s, nq), BF16),
        compiler_params=_cparams(("arbitrary", "arbitrary")),
        name="swa_attn",
    )(sinks, q, kv, kv)


def _post_kernel(*refs, tm, n_a, has_bias):
    x_ref = refs[0]
    a_refs = refs[1:1 + n_a]
    k = 1 + n_a
    wo_refs = refs[k:k + n_a]
    k += n_a
    if has_bias:
        bo_ref = refs[k]
        k += 1
    (gx_ref, wq_ref, mkv_ref, wxo_ref, gf_ref, wr_ref, br_ref,
     x2_ref, h2_ref, mi_ref, mw_ref, cnt_ref, cnt_acc) = refs[k:]
    first = (pl.program_id(0) == 0) & (pl.program_id(1) == 0)

    @pl.when(first)
    def _():
        cnt_acc[...] = jnp.zeros_like(cnt_acc)

    xw = XATTN_HEADS * XATTN_HEAD_DIM
    ones = jnp.ones((mkv_ref.shape[1], LANES), BF16)

    rh = tm // POST_CHAINS

    def out_proj(r0):
        acc = _dot(a_refs[0][0, r0:r0 + rh, :], wo_refs[0][...])
        for a_ref, w_ref in zip(a_refs[1:], wo_refs[1:]):
            acc = acc + _dot(a_ref[0, r0:r0 + rh, :], w_ref[...])
        if has_bias:
            acc = acc + bo_ref[...]
        return x_ref[0, r0:r0 + rh, :] + acc

    def q_proj(x1):
        hx = _rms(x1, gx_ref[...]).astype(BF16)
        return (_dot(hx, wq_ref[...]) * (XATTN_HEAD_DIM ** -0.5 * LOG2E)).astype(BF16)

    def mem_attn(qx):
        outs = []
        for hh in range(XATTN_HEADS):
            lo = hh * XATTN_HEAD_DIM
            mk = mkv_ref[0, :, lo:lo + XATTN_HEAD_DIM]
            mv = jnp.concatenate([mkv_ref[0, :, xw + lo:xw + lo + XATTN_HEAD_DIM], ones], axis=1)
            sc = _dot_nt(qx[:, lo:lo + XATTN_HEAD_DIM], mk)
            m = jnp.max(sc, axis=-1, keepdims=True)
            pv = _dot(jnp.exp2(sc - m).astype(BF16), mv)
            outs.append((pv[:, :LANES] / pv[:, LANES:]).astype(BF16))
        return jnp.concatenate(outs, axis=-1)

    def o_proj(r0, x1, ox):
        x2 = x1 + _dot(ox, wxo_ref[...])
        x2_ref[0, r0:r0 + rh, :] = x2
        return x2

    def router(r0, x2):
        h2 = _rms(x2, gf_ref[...])
        _to_token_tiles(h2_ref, h2, rh, r0)
        h_hi = h2.astype(BF16)
        h_lo = (h2 - h_hi.astype(F32)).astype(BF16)
        return _dot(h_hi, wr_ref[0]) + _dot(h_lo, wr_ref[0]) + _dot(h_hi, wr_ref[1]) + br_ref[...]

    starts = [c * rh for c in range(POST_CHAINS)]
    x1s = [out_proj(r0) for r0 in starts]
    qxs = [q_proj(x1) for x1 in x1s]
    oxs = [mem_attn(qx) for qx in qxs]
    x2s = [o_proj(r0, x1, ox) for r0, x1, ox in zip(starts, x1s, oxs)]
    logits = jnp.concatenate([router(r0, x2) for r0, x2 in zip(starts, x2s)], axis=0)
    lane = lax.broadcasted_iota(I32, (tm, LANES), 1).astype(F32)
    big = float(LANES)
    g_lo = float(N_EXPERTS)
    lg = jnp.where((lane >= g_lo) & (lane < g_lo + N_GROUPS), logits, NEG_INF)
    mg = jnp.max(lg, axis=-1, keepdims=True)
    g_lane = jnp.min(jnp.where(lg == mg, lane, big), axis=-1, keepdims=True)
    p_g = 1.0 / jnp.sum(jnp.exp(lg - mg), axis=-1, keepdims=True)
    e_lo = (g_lane - g_lo) * EXPERTS_PER_GROUP
    le = jnp.where((lane >= e_lo) & (lane < e_lo + EXPERTS_PER_GROUP), logits, NEG_INF)
    m1 = jnp.max(le, axis=-1, keepdims=True)
    i1 = jnp.min(jnp.where(le == m1, lane, big), axis=-1, keepdims=True)
    le2 = jnp.where(lane == i1, NEG_INF, le)
    m2 = jnp.max(le2, axis=-1, keepdims=True)
    i2 = jnp.min(jnp.where(le2 == m2, lane, big), axis=-1, keepdims=True)
    t = jnp.exp(m2 - m1)
    w1 = p_g / (1.0 + t)
    w2 = p_g * t / (1.0 + t)

    oh1 = lane == i1
    oh2 = lane == i2
    oh = jnp.where(oh1 | oh2, 1.0, 0.0)
    rr = lax.broadcasted_iota(I32, (tm, tm), 0)
    cc = lax.broadcasted_iota(I32, (tm, tm), 1)
    tri = jnp.where(cc < rr, 1.0, 0.0).astype(BF16)
    before = _dot(tri, oh.astype(BF16)) + cnt_acc[...]
    r1 = jnp.sum(jnp.where(oh1, before, 0.0), axis=-1, keepdims=True)
    r2 = jnp.sum(jnp.where(oh2, before, 0.0), axis=-1, keepdims=True)
    cnt_new = cnt_acc[...] + jnp.sum(oh, axis=0, keepdims=True)
    cnt_acc[...] = cnt_new
    cnt_ref[...] = cnt_new.astype(I32)

    mw_ref[...] = jnp.where(lane == 0, w1, jnp.where(lane == 1, w2, 0.0))
    ints = jnp.where(lane == 0, i1,
                     jnp.where(lane == 1, i2,
                               jnp.where(lane == 2, r1, jnp.where(lane == 3, r2, 0.0))))
    mi_ref[...] = ints.T[0:8, :].astype(I32)


def _post_mixer(x, a_list, wo_list, b_out, g_x, w_q, mkv, w_xo, g_f, w_rt, b_rt):
    b, s, d = x.shape
    tm = ROW_TILE
    ns = s // tm
    t = b * s
    n_a = len(a_list)
    has_bias = b_out is not None
    const2 = lambda bi, si: (0, 0)
    row = lambda width: pl.BlockSpec((1, tm, width), lambda bi, si: (bi, si, 0))
    in_specs = [row(d)] + [row(a.shape[2]) for a in a_list]
    in_specs += [pl.BlockSpec(w.shape, const2) for w in wo_list]
    args = [x, *a_list, *wo_list]
    if has_bias:
        in_specs.append(pl.BlockSpec((1, d), const2))
        args.append(b_out.reshape(1, d))
    xw = w_q.shape[1]
    in_specs += [pl.BlockSpec((1, d), const2),
                 pl.BlockSpec((d, xw), const2),
                 pl.BlockSpec((1, mkv.shape[1], mkv.shape[2]), lambda bi, si: (bi, 0, 0)),
                 pl.BlockSpec((xw, d), const2),
                 pl.BlockSpec((1, d), const2),
                 pl.BlockSpec((2, d, LANES), lambda bi, si: (0, 0, 0)),
                 pl.BlockSpec((1, LANES), const2)]
    args += [g_x.reshape(1, d), w_q, mkv, w_xo, g_f.reshape(1, d), w_rt, b_rt]
    out_specs = [row(d),
                 pl.BlockSpec((tm * TOKEN_ROWS, LANES), lambda bi, si: (bi * ns + si, 0)),
                 pl.BlockSpec((8, tm), lambda bi, si: (0, bi * ns + si)),
                 pl.BlockSpec((tm, LANES), lambda bi, si: (bi * ns + si, 0)),
                 pl.BlockSpec((1, LANES), const2)]
    out_shape = [jax.ShapeDtypeStruct((b, s, d), F32),
                 jax.ShapeDtypeStruct((t * TOKEN_ROWS, LANES), U32),
                 jax.ShapeDtypeStruct((8, t), I32),
                 jax.ShapeDtypeStruct((t, LANES), F32),
                 jax.ShapeDtypeStruct((1, LANES), I32)]
    return pl.pallas_call(
        functools.partial(_post_kernel, tm=tm, n_a=n_a, has_bias=has_bias),
        grid=(b, ns),
        in_specs=in_specs,
        out_specs=out_specs,
        out_shape=out_shape,
        scratch_shapes=[pltpu.VMEM((1, LANES), F32)],
        compiler_params=_cparams(("arbitrary", "arbitrary")),
        name="post_mixer",
    )(*args)


def _to_token_tiles(dst_ref, val, rows, tok0=0):
    half = TOKEN_ROWS * LANES
    for j in range(TOKEN_ROWS):
        hi = val[:, j * LANES:(j + 1) * LANES].astype(BF16).astype(F32)
        lo = val[:, half + j * LANES:half + (j + 1) * LANES].astype(BF16).astype(F32)
        word = lax.bitcast_convert_type(hi, U32) | (lax.bitcast_convert_type(lo, U32) >> 16)
        dst_ref[pl.ds(tok0 * TOKEN_ROWS + j, rows, stride=TOKEN_ROWS), :] = word


def _from_token_tiles(src_ref, rows, tok0=0):
    his, los = [], []
    for j in range(TOKEN_ROWS):
        word = src_ref[pl.ds(tok0 * TOKEN_ROWS + j, rows, stride=TOKEN_ROWS), :]
        his.append(lax.bitcast_convert_type(word & jnp.uint32(0xFFFF0000), F32))
        los.append(lax.bitcast_convert_type(word << 16, F32))
    return jnp.concatenate(his + los, axis=1)


def _token_slice(ref, tok):
    return ref.at[pl.ds(pl.multiple_of(tok * TOKEN_ROWS, TOKEN_ROWS), TOKEN_ROWS)]


def _pos_kernel(off_ref, mi_ref, pos_ref):
    e = mi_ref[0:TOP_K, :]
    pos = mi_ref[TOP_K:2 * TOP_K, :]
    for j in range(N_EXPERTS):
        pos = pos + jnp.where(e == j, off_ref[j], 0)
    pos_ref[...] = pos


def _sorted_positions(off, meta_i):
    t = meta_i.shape[1]
    return pl.pallas_call(
        _pos_kernel,
        grid_spec=pltpu.PrefetchScalarGridSpec(
            num_scalar_prefetch=1,
            grid=(1,),
            in_specs=[pl.BlockSpec((8, t), lambda i, off: (0, 0))],
            out_specs=pl.BlockSpec((TOP_K, t), lambda i, off: (0, 0))),
        out_shape=jax.ShapeDtypeStruct((TOP_K, t), I32),
        compiler_params=_cparams(("arbitrary",)),
        name="moe_positions",
    )(off, meta_i)


def _tile_major(pos, tm):
    t = pos.shape[1]
    return pos.reshape(TOP_K, t // tm, tm).transpose(1, 0, 2).reshape(-1)


def _dispatch_kernel(pos_ref, h_ref, xs_ref, sem, *, tm):
    def issue(g, _):
        for u in range(DMA_UNROLL):
            r = g * DMA_UNROLL + u
            for kk in range(TOP_K):
                pos = pos_ref[kk * tm + r]
                pltpu.make_async_copy(_token_slice(h_ref, r), _token_slice(xs_ref, pos), sem).start(priority=kk)
        return 0

    lax.fori_loop(0, tm // DMA_UNROLL, issue, 0)
    for _ in range(TOP_K):
        pltpu.make_async_copy(h_ref, xs_ref.at[pl.ds(0, tm * TOKEN_ROWS)], sem).wait()


def _dispatch(pos, h2t, n_rows):
    tm = DISPATCH_TILE
    t = h2t.shape[0] // TOKEN_ROWS
    return pl.pallas_call(
        functools.partial(_dispatch_kernel, tm=tm),
        grid=(t // tm,),
        in_specs=[pl.BlockSpec((TOP_K * tm,), lambda i: (i,), memory_space=pltpu.SMEM),
                  pl.BlockSpec((tm * TOKEN_ROWS, LANES), lambda i: (i, 0))],
        out_specs=pl.BlockSpec(memory_space=pl.ANY),
        scratch_shapes=[pltpu.SemaphoreType.DMA],
        out_shape=jax.ShapeDtypeStruct((n_rows * TOKEN_ROWS, LANES), U32),
        compiler_params=_cparams(("arbitrary",)),
        name="moe_dispatch",
    )(_tile_major(pos, tm), h2t)


def _expert_kernel(tblk_ref, texp_ref, tn_ref, xs_ref, wgu_ref, wdn_ref, ys_ref, wgu_bf, wdn_bf, *, tm):
    i = pl.program_id(0)
    n_valid = tn_ref[i]
    new_expert = (i == 0) | (texp_ref[i] != texp_ref[jnp.maximum(i - 1, 0)])

    @pl.when(new_expert)
    def _():
        wgu_bf[...] = wgu_ref[0, 0].astype(BF16)
        wdn_bf[...] = wdn_ref[0, 0].astype(BF16)

    @pl.when(n_valid > 0)
    def _():
        rh = tm // EXPERT_CHAINS
        starts = [c * rh for c in range(EXPERT_CHAINS)]
        row = lax.broadcasted_iota(I32, (rh, 1), 0)
        gus = []
        for r0 in starts:
            x = jnp.where(row + r0 < n_valid, _from_token_tiles(xs_ref, rh, r0), 0.0).astype(BF16)
            gus.append(_dot(x, wgu_bf[...]))
        ys = []
        for gu in gus:
            hid = (gu[:, :EXPERT_HIDDEN] / (1.0 + jnp.exp(-gu[:, :EXPERT_HIDDEN])) * gu[:, EXPERT_HIDDEN:]).astype(BF16)
            ys.append(_dot(hid, wdn_bf[...]))
        for r0, y in zip(starts, ys):
            _to_token_tiles(ys_ref, y, rh, r0)


def _experts(tile_blk, tile_exp, tile_n, xs, w_gu, w_dn, layer):
    tm = EXPERT_TILE
    nt = xs.shape[0] // (tm * TOKEN_ROWS)
    gu_shape, dn_shape = w_gu.shape[2:], w_dn.shape[2:]
    return pl.pallas_call(
        functools.partial(_expert_kernel, tm=tm),
        grid_spec=pltpu.PrefetchScalarGridSpec(
            num_scalar_prefetch=3,
            grid=(nt,),
            in_specs=[pl.BlockSpec((tm * TOKEN_ROWS, LANES), lambda i, tb, te, tn: (tb[i], 0)),
                      pl.BlockSpec((1, 1) + gu_shape, lambda i, tb, te, tn: (layer, te[i], 0, 0)),
                      pl.BlockSpec((1, 1) + dn_shape, lambda i, tb, te, tn: (layer, te[i], 0, 0))],
            out_specs=pl.BlockSpec((tm * TOKEN_ROWS, LANES), lambda i, tb, te, tn: (tb[i], 0)),
            scratch_shapes=[pltpu.VMEM(gu_shape, BF16), pltpu.VMEM(dn_shape, BF16)]),
        out_shape=jax.ShapeDtypeStruct(xs.shape, U32),
        compiler_params=_cparams(("arbitrary",)),
        name="moe_experts",
    )(tile_blk, tile_exp, tile_n, xs, w_gu, w_dn)


def _moe_combine_tile(pos_cur, pos_nxt, mw_ref, x, ys_ref, ybuf, sems, i, n, tm):
    def gather(pos_ref, slot):
        def issue(g, _):
            for u in range(DMA_UNROLL):
                r = g * DMA_UNROLL + u
                for kk in range(TOP_K):
                    pos = pos_ref[kk * tm + r]
                    pltpu.make_async_copy(_token_slice(ys_ref, pos), _token_slice(ybuf.at[slot, kk], r),
                                          sems.at[slot]).start(priority=kk)
            return 0

        lax.fori_loop(0, tm // DMA_UNROLL, issue, 0)

    @pl.when(i == 0)
    def _():
        gather(pos_cur, 0)

    @pl.when(i + 1 < n)
    def _():
        gather(pos_nxt, (i + 1) % 2)

    slot = i % 2
    for kk in range(TOP_K):
        pltpu.make_async_copy(ys_ref.at[pl.ds(0, tm * TOKEN_ROWS)], ybuf.at[slot, kk], sems.at[slot]).wait()
    mw = mw_ref[...]
    return (x + mw[:, 0:1] * _from_token_tiles(ybuf.at[slot, 0], tm)
            + mw[:, 1:2] * _from_token_tiles(ybuf.at[slot, 1], tm))


def _combine_kernel(pos_cur, pos_nxt, mw_ref, x_ref, ys_ref, *rest, tm, final):
    if final:
        g_ref, o_ref, ybuf, sems = rest
    else:
        o_ref, ybuf, sems = rest
    x3 = _moe_combine_tile(pos_cur, pos_nxt, mw_ref, x_ref[...], ys_ref, ybuf, sems,
                           pl.program_id(0), pl.num_programs(0), tm)
    if final:
        x3 = _rms(x3, g_ref[...])
    o_ref[...] = x3


def _combine(moe, g_final):
    pos, meta_w, x2, ys = moe
    d = x2.shape[-1]
    x2d = x2.reshape(-1, d)
    t = x2d.shape[0]
    tm = COMBINE_TILE
    nt = t // tm
    final = g_final is not None
    in_specs = [pl.BlockSpec((TOP_K * tm,), lambda i: (i,), memory_space=pltpu.SMEM),
                pl.BlockSpec((TOP_K * tm,), lambda i: (jnp.minimum(i + 1, nt - 1),), memory_space=pltpu.SMEM),
                pl.BlockSpec((tm, LANES), lambda i: (i, 0)),
                pl.BlockSpec((tm, d), lambda i: (i, 0)),
                pl.BlockSpec(memory_space=pl.ANY)]
    pos_flat = _tile_major(pos, tm)
    args = [pos_flat, pos_flat, meta_w, x2d, ys]
    if final:
        in_specs.append(pl.BlockSpec((1, d), lambda i: (0, 0)))
        args.append(g_final.reshape(1, d))
    return pl.pallas_call(
        functools.partial(_combine_kernel, tm=tm, final=final),
        grid=(nt,),
        in_specs=in_specs,
        out_specs=pl.BlockSpec((tm, d), lambda i: (i, 0)),
        scratch_shapes=[pltpu.VMEM((2, TOP_K, tm * TOKEN_ROWS, LANES), U32), pltpu.SemaphoreType.DMA((2,))],
        out_shape=jax.ShapeDtypeStruct((t, d), F32),
        compiler_params=_cparams(("arbitrary",)),
        name="moe_combine",
    )(*args)


def _plan_kernel(cnt_ref, off_ref, blk_ref, exp_ref, nv_ref, *, tm, n_tiles):
    shift = tm.bit_length() - 1

    def per_expert(e, row0):
        c = cnt_ref[0, e]
        ntile = lax.shift_right_logical(c + (tm - 1), shift)
        off_ref[e] = row0
        t0 = lax.shift_right_logical(row0, shift)

        def fill(j, _):
            blk_ref[t0 + j] = t0 + j
            exp_ref[t0 + j] = e
            nv_ref[t0 + j] = jnp.minimum(c - j * tm, tm)
            return 0

        lax.fori_loop(0, ntile, fill, 0)
        return row0 + lax.shift_left(ntile, shift)

    total = lax.fori_loop(0, N_EXPERTS, per_expert, jnp.int32(0))
    used = lax.shift_right_logical(total, shift)
    last = jnp.maximum(used - 1, 0)

    def tail(i, _):
        blk_ref[i] = last
        exp_ref[i] = exp_ref[last]
        nv_ref[i] = 0
        return 0

    lax.fori_loop(used, n_tiles, tail, 0)


def _moe_plan(counts, n_tiles):
    tm = EXPERT_TILE
    assert tm & (tm - 1) == 0
    smem = pl.BlockSpec(memory_space=pltpu.SMEM)
    return pl.pallas_call(
        functools.partial(_plan_kernel, tm=tm, n_tiles=n_tiles),
        in_specs=[smem],
        out_specs=[smem, smem, smem, smem],
        out_shape=[jax.ShapeDtypeStruct((N_EXPERTS,), I32)] + [jax.ShapeDtypeStruct((n_tiles,), I32)] * 3,
        name="moe_plan",
    )(counts)


def _moe(x2, h2t, meta_i, meta_w, counts, w_gu, w_dn, layer):
    b, s, d = x2.shape
    assert d == 2 * TOKEN_ROWS * LANES
    n_rows = b * s * TOP_K + N_EXPERTS * EXPERT_TILE
    off, blk, exp, n_valid = _moe_plan(counts, n_rows // EXPERT_TILE)
    pos = _sorted_positions(off, meta_i)
    xs = _dispatch(pos, h2t, n_rows)
    ys = _experts(blk, exp, n_valid, xs, w_gu, w_dn, layer)
    return pos, meta_w, x2, ys


def _router_weights(w_group, b_group, w_router, b_router):
    d = w_group.shape[0]
    pad = LANES - N_EXPERTS - N_GROUPS
    w = jnp.concatenate([w_router, w_group, jnp.zeros((d, pad), F32)], axis=1)
    bias = jnp.concatenate([b_router, b_group, jnp.zeros((pad,), F32)]).reshape(1, LANES)
    w_hi = w.astype(BF16)
    w_lo = (w - w_hi.astype(F32)).astype(BF16)
    return jnp.stack([w_hi, w_lo]), bias


def kernel(x, mem, mem_norm, mem_w_kv, norm_mix, norm_xattn, norm_ffn, hyb_w_in, hyb_conv_w, diff_lambda, diff_subln, hyb_w_out, swa_w_qkv, swa_b_qkv, swa_sinks, swa_w_out, swa_b_out, xattn_w_q, xattn_w_o, moe_w_group, moe_b_group, moe_w_router, moe_b_router, moe_w_gate_up, moe_w_down, final_norm):
    b, s, d = x.shape
    m = mem.shape[1]
    depth = norm_mix.shape[0]
    mkv = _norm_proj(mem.reshape(b * m, d), mem_norm, mem_w_kv.astype(BF16), m).reshape(b, m, -1)

    scale = DIFF_DK ** -0.5 * LOG2E
    cq, suq, sdq, half = _rope_lane_tables(s, DIFF_DK, scale)
    ck, suk, sdk, _ = _rope_lane_tables(s, DIFF_DK, 1.0)
    tabs_q, tabs_k = (cq, suq, sdq), (ck, suk, sdk)

    moe = None
    for l in range(depth):
        if moe is not None:
            x = _combine(moe, None).reshape(b, s, d)
        if l % 2 == 0:
            e = l // 2
            lambda_init = 0.8 - 0.6 * math.exp(-0.3 * l)
            ya, q, k, v = _hyb_front(x, norm_mix[l], hyb_w_in[e].astype(BF16), hyb_conv_w[e], tabs_q, tabs_k, half)
            o = _diff_attn(q, k, v, diff_lambda[e], diff_subln[e], lambda_init)
            w_out = hyb_w_out[e].astype(BF16)
            a_list, wo_list, b_out = [ya, o], [w_out[:CONV_CH], w_out[CONV_CH:]], None
        else:
            e = l // 2
            order = jnp.asarray(_swa_head_order(), I32)
            nq = SWA_Q_HEADS * SWA_HEAD_DIM
            cols = (order[:, None] * SWA_HEAD_DIM + jnp.arange(SWA_HEAD_DIM, dtype=I32)[None, :]).reshape(-1)
            cols_all = jnp.concatenate([cols, jnp.arange(nq, swa_w_qkv.shape[2], dtype=I32)])
            w_qkv = jnp.take(swa_w_qkv[e], cols_all, axis=1).astype(BF16)
            b_qkv = jnp.take(swa_b_qkv[e], cols_all)
            q, kv = _swa_front(x, norm_mix[l], w_qkv, b_qkv, tabs_q, tabs_k, half)
            o = _swa_attn(q, kv, jnp.take(swa_sinks[e], order))
            a_list, wo_list, b_out = [o], [jnp.take(swa_w_out[e], cols, axis=0).astype(BF16)], swa_b_out[e]
        w_rt, b_rt = _router_weights(moe_w_group[l], moe_b_group[l], moe_w_router[l], moe_b_router[l])
        x2, h2, meta_i, meta_w, counts = _post_mixer(
            x, a_list, wo_list, b_out, norm_xattn[l], xattn_w_q[l].astype(BF16), mkv,
            xattn_w_o[l].astype(BF16), norm_ffn[l], w_rt, b_rt)
        moe = _moe(x2, h2, meta_i, meta_w, counts, moe_w_gate_up, moe_w_down, l)
    return _combine(moe, final_norm).reshape(b, s, d)
```

```python
import functools
import math

import jax
import jax.numpy as jnp
from jax import lax
from jax.experimental import pallas as pl
from jax.experimental.pallas import tpu as pltpu

F32 = jnp.float32
BF16 = jnp.bfloat16
I32 = jnp.int32
U32 = jnp.uint32

EPS = 1e-6
LANES = 128
SUBLANES = 8
TOKEN_ROWS = 4
DMA_UNROLL = 8
VMEM_LIMIT = 56 * 1024 * 1024

ROPE_THETA = 500000.0
ROPE_FRACTION = 4
BLOCK = 128
CONV_CH = 512
CONV_K = 3
DIFF_HEADS = 4
DIFF_DK = 64
DIFF_DV = 128
SWA_Q_HEADS = 16
SWA_KV_HEADS = 4
SWA_HEAD_DIM = 64
SWA_WINDOW = 128
XATTN_HEADS = 4
XATTN_HEAD_DIM = 128
N_GROUPS = 4
EXPERTS_PER_GROUP = 8
N_EXPERTS = N_GROUPS * EXPERTS_PER_GROUP
TOP_K = 2
EXPERT_HIDDEN = 512

ROW_TILE = 512
EXPERT_TILE = 512
DISPATCH_TILE = 1024
COMBINE_TILE = 512
POST_CHAINS = 2
EXPERT_CHAINS = 2
FRONT_CHAINS = 2
NEG_INF = float("-inf")
LOG2E = math.log2(math.e)


def _cparams(sem):
    return pltpu.CompilerParams(dimension_semantics=sem, vmem_limit_bytes=VMEM_LIMIT)


def _rms(x, g):
    return x * lax.rsqrt(jnp.mean(x * x, axis=-1, keepdims=True) + EPS) * g


def _dot(a, b):
    return jnp.dot(a, b, preferred_element_type=F32)


def _dot_nt(a, b):
    return lax.dot_general(a, b, (((1,), (1,)), ((), ())), preferred_element_type=F32)


def _rope_lane_tables(seq, head_dim, scale):
    rot = head_dim // ROPE_FRACTION
    half = rot // 2
    pos = jnp.arange(seq, dtype=F32)
    inv = ROPE_THETA ** (-jnp.arange(0, rot, 2, dtype=F32) / rot)
    ang = pos[:, None] * inv[None, :]
    cos, sin = jnp.cos(ang), jnp.sin(ang)
    idx = jnp.arange(LANES) % head_dim
    cl = jnp.take(cos, idx % half, axis=1)
    sl = jnp.take(sin, idx % half, axis=1)
    c = jnp.where(idx < rot, cl, 1.0) * scale
    s_up = jnp.where(idx < half, -sl, 0.0) * scale
    s_dn = jnp.where((idx >= half) & (idx < rot), sl, 0.0) * scale
    return c.astype(F32), s_up.astype(F32), s_dn.astype(F32), half


def _rope_chunk(xc, c, s_up, s_dn, half):
    return (xc * c + pltpu.roll(xc, LANES - half, 1) * s_up + pltpu.roll(xc, half, 1) * s_dn)


def _norm_proj_kernel(x_ref, g_ref, w_ref, o_ref):
    h = _rms(x_ref[...], g_ref[...]).astype(BF16)
    o_ref[...] = _dot(h, w_ref[...]).astype(o_ref.dtype)


def _norm_proj(x2d, g, w_bf16, tm):
    m, d = x2d.shape
    n = w_bf16.shape[1]
    return pl.pallas_call(
        _norm_proj_kernel,
        grid=(m // tm,),
        in_specs=[pl.BlockSpec((tm, d), lambda i: (i, 0)),
                  pl.BlockSpec((1, d), lambda i: (0, 0)),
                  pl.BlockSpec((d, n), lambda i: (0, 0))],
        out_specs=pl.BlockSpec((tm, n), lambda i: (i, 0)),
        out_shape=jax.ShapeDtypeStruct((m, n), BF16),
        compiler_params=_cparams(("arbitrary",)),
        name="mem_kv_proj",
    )(x2d, g.reshape(1, d), w_bf16)


def _hyb_front_kernel(x_ref, g_ref, w_ref, cw_ref, cq_ref, suq_ref, sdq_ref, ck_ref, suk_ref, sdk_ref,
                      ya_ref, q_ref, k_ref, v_ref, cbuf, *, tm, half):
    s = pl.program_id(1)
    c = CONV_CH
    base = 3 * c
    nq = DIFF_HEADS * 2 * DIFF_DK
    rh = tm // FRONT_CHAINS
    starts = [ch * rh for ch in range(FRONT_CHAINS)]
    cw = cw_ref[...]

    @pl.when(s == 0)
    def _():
        cbuf[0:8, :] = jnp.zeros((8, c), F32)

    hs = [_rms(x_ref[0, r0:r0 + rh, :], g_ref[...]).astype(BF16) for r0 in starts]
    for r0, h in zip(starts, hs):
        gate_b = _dot(h, w_ref[:, 0:c])
        cu = _dot(h, w_ref[:, c:2 * c]) * _dot(h, w_ref[:, 2 * c:3 * c])
        cbuf[8 + r0:8 + r0 + rh, :] = cu
        conv = (cw[0:1, :] * cbuf[6 + r0:6 + r0 + rh, :] + cw[1:2, :] * cbuf[7 + r0:7 + r0 + rh, :] + cw[2:3, :] * cu)
        ya_ref[0, r0:r0 + rh, :] = (gate_b * conv).astype(BF16)
    cbuf[0:8, :] = cbuf[tm:tm + 8, :]

    for r0, h in zip(starts, hs):
        cq, suq, sdq = cq_ref[r0:r0 + rh, :], suq_ref[r0:r0 + rh, :], sdq_ref[r0:r0 + rh, :]
        pq = _dot(h, w_ref[:, base:base + nq])
        for j in range(nq // LANES):
            q_ref[0, r0:r0 + rh, j * LANES:(j + 1) * LANES] = _rope_chunk(
                pq[:, j * LANES:(j + 1) * LANES], cq, suq, sdq, half).astype(BF16)
    for r0, h in zip(starts, hs):
        ck, suk, sdk = ck_ref[r0:r0 + rh, :], suk_ref[r0:r0 + rh, :], sdk_ref[r0:r0 + rh, :]
        pk = _dot(h, w_ref[:, base + nq:base + 2 * nq])
        for j in range(nq // LANES):
            k_ref[0, r0:r0 + rh, j * LANES:(j + 1) * LANES] = _rope_chunk(
                pk[:, j * LANES:(j + 1) * LANES], ck, suk, sdk, half).astype(BF16)
    for r0, h in zip(starts, hs):
        v_ref[0, r0:r0 + rh, :] = _dot(h, w_ref[:, base + 2 * nq:]).astype(BF16)


def _hyb_front(x, g, w_in, conv_w, tabs_q, tabs_k, half):
    b, s, d = x.shape
    tm = ROW_TILE
    n = w_in.shape[1]
    nq = DIFF_HEADS * 2 * DIFF_DK
    nv = DIFF_HEADS * DIFF_DV
    tab_spec = pl.BlockSpec((tm, LANES), lambda bi, si: (si, 0))
    row = lambda width: pl.BlockSpec((1, tm, width), lambda bi, si: (bi, si, 0))
    return pl.pallas_call(
        functools.partial(_hyb_front_kernel, tm=tm, half=half),
        grid=(b, s // tm),
        in_specs=[row(d),
                  pl.BlockSpec((1, d), lambda bi, si: (0, 0)),
                  pl.BlockSpec((d, n), lambda bi, si: (0, 0)),
                  pl.BlockSpec((CONV_K, CONV_CH), lambda bi, si: (0, 0)),
                  tab_spec, tab_spec, tab_spec, tab_spec, tab_spec, tab_spec],
        out_specs=[row(CONV_CH), row(nq), row(nq), row(nv)],
        out_shape=[jax.ShapeDtypeStruct((b, s, CONV_CH), BF16),
                   jax.ShapeDtypeStruct((b, s, nq), BF16),
                   jax.ShapeDtypeStruct((b, s, nq), BF16),
                   jax.ShapeDtypeStruct((b, s, nv), BF16)],
        scratch_shapes=[pltpu.VMEM((tm + 8, CONV_CH), F32)],
        compiler_params=_cparams(("arbitrary", "arbitrary")),
        name="hyb_front",
    )(x, g.reshape(1, d), w_in, conv_w, *tabs_q, *tabs_k)


def _lane_fold(x, op):
    r = x[:, 0:LANES]
    for c in range(1, x.shape[1] // LANES):
        r = op(r, x[:, c * LANES:(c + 1) * LANES])
    return r


def _diff_attn_kernel(q_ref, k_ref, v_ref, lam_ref, g_ref, o_ref, sbuf, stat, acc, *, tq, hp, lambda_init):
    i = pl.program_id(2)
    lane = lax.broadcasted_iota(I32, (1, LANES), 1)
    qs = []
    for h in range(hp):
        q = q_ref[0, :, h * LANES:(h + 1) * LANES]
        zero = jnp.zeros_like(q)
        qs += [jnp.where(lane < DIFF_DK, q, zero), jnp.where(lane >= DIFF_DK, q, zero)]
    nc = 2 * hp

    def scores(j, masked):
        for c in range(nc):
            h = c // 2
            kb = k_ref[0, pl.ds(pl.multiple_of(j * tq, tq), tq), h * LANES:(h + 1) * LANES]
            sc = _dot_nt(qs[c], kb)
            if masked:
                r = lax.broadcasted_iota(I32, (tq, tq), 0)
                cc = lax.broadcasted_iota(I32, (tq, tq), 1)
                sc = jnp.where(cc <= r, sc, NEG_INF)
            sbuf[c, j] = sc
            stat[c] = jnp.maximum(stat[c], _lane_fold(sc, jnp.maximum))

    stat[...] = jnp.full(stat.shape, NEG_INF, F32)

    def pass1(j, carry):
        scores(j, False)
        return carry

    lax.fori_loop(0, i, pass1, 0)
    scores(i, True)
    ms = [jnp.max(stat[c], axis=-1, keepdims=True) for c in range(nc)]

    ones = jnp.ones((tq, LANES), BF16)

    def pass2(j, first):
        for h in range(hp):
            vb = v_ref[0, pl.ds(pl.multiple_of(j * tq, tq), tq), h * LANES:(h + 1) * LANES]
            v_ext = jnp.concatenate([vb, ones], axis=1)
            p0 = jnp.exp2(sbuf[2 * h, j] - ms[2 * h])
            p1 = jnp.exp2(sbuf[2 * h + 1, j] - ms[2 * h + 1])
            upd = _dot(jnp.concatenate([p0, p1], axis=0).astype(BF16), v_ext)
            if first:
                acc[h] = upd
            else:
                acc[h] += upd
        return 0

    pass2(0, True)
    lax.fori_loop(1, i + 1, lambda j, c: pass2(j, False), 0)

    lf = lam_ref[...]
    lam = (jnp.exp(jnp.sum(lf[0:1] * lf[1:2], keepdims=True))
           - jnp.exp(jnp.sum(lf[2:3] * lf[3:4], keepdims=True)) + lambda_init)
    for h in range(hp):
        a0 = acc[h, 0:tq, :]
        a1 = acc[h, tq:2 * tq, :]
        o = a0[:, :LANES] / a0[:, LANES:] - lam * (a1[:, :LANES] / a1[:, LANES:])
        o_ref[0, :, h * LANES:(h + 1) * LANES] = (_rms(o, g_ref[...]) * (1.0 - lambda_init)).astype(BF16)


def _diff_attn(q, k, v, lam_vecs, subln_g, lambda_init):
    b, s, _ = q.shape
    tq = 512
    hp = 4
    blk = lambda bi, hi, qi: (bi, qi, hi)
    full = lambda bi, hi, qi: (bi, 0, hi)
    return pl.pallas_call(
        functools.partial(_diff_attn_kernel, tq=tq, hp=hp, lambda_init=lambda_init),
        grid=(b, DIFF_HEADS // hp, s // tq),
        in_specs=[pl.BlockSpec((1, tq, hp * LANES), blk),
                  pl.BlockSpec((1, s, hp * LANES), full),
                  pl.BlockSpec((1, s, hp * LANES), full),
                  pl.BlockSpec((4, DIFF_DK), lambda bi, hi, qi: (0, 0)),
                  pl.BlockSpec((1, DIFF_DV), lambda bi, hi, qi: (0, 0))],
        out_specs=pl.BlockSpec((1, tq, hp * LANES), blk),
        out_shape=jax.ShapeDtypeStruct((b, s, DIFF_HEADS * DIFF_DV), BF16),
        scratch_shapes=[pltpu.VMEM((2 * hp, s // tq, tq, tq), F32),
                        pltpu.VMEM((2 * hp, tq, LANES), F32),
                        pltpu.VMEM((hp, 2 * tq, DIFF_DV + LANES), F32)],
        compiler_params=_cparams(("arbitrary", "arbitrary", "arbitrary")),
        name="diff_attn",
    )(q, k, v, lam_vecs, subln_g.reshape(1, DIFF_DV))


def _swa_front_kernel(x_ref, g_ref, w_ref, b_ref, cq_ref, suq_ref, sdq_ref, ck_ref, suk_ref, sdk_ref,
                      q_ref, kv_ref, *, half):
    nq = SWA_Q_HEADS * SWA_HEAD_DIM
    nkv = SWA_KV_HEADS * SWA_HEAD_DIM
    tm = x_ref.shape[1]
    rh = tm // FRONT_CHAINS
    starts = [ch * rh for ch in range(FRONT_CHAINS)]
    hs = [_rms(x_ref[0, r0:r0 + rh, :], g_ref[...]).astype(BF16) for r0 in starts]
    for r0, h in zip(starts, hs):
        cq, suq, sdq = cq_ref[r0:r0 + rh, :], suq_ref[r0:r0 + rh, :], sdq_ref[r0:r0 + rh, :]
        pq = _dot(h, w_ref[:, 0:nq]) + b_ref[:, 0:nq]
        for j in range(nq // LANES):
            lo = j * LANES
            q_ref[0, r0:r0 + rh, lo:lo + LANES] = _rope_chunk(pq[:, lo:lo + LANES], cq, suq, sdq, half).astype(BF16)
    for r0, h in zip(starts, hs):
        ck, suk, sdk = ck_ref[r0:r0 + rh, :], suk_ref[r0:r0 + rh, :], sdk_ref[r0:r0 + rh, :]
        pkv = _dot(h, w_ref[:, nq:]) + b_ref[:, nq:]
        for j in range(nkv // LANES):
            lo = j * LANES
            kv_ref[0, r0:r0 + rh, lo:lo + LANES] = _rope_chunk(pkv[:, lo:lo + LANES], ck, suk, sdk, half).astype(BF16)
        kv_ref[0, r0:r0 + rh, nkv:] = pkv[:, nkv:].astype(BF16)


def _swa_front(x, g, w_qkv, b_qkv, tabs_q, tabs_k, half):
    b, s, d = x.shape
    tm = ROW_TILE
    n = w_qkv.shape[1]
    nq = SWA_Q_HEADS * SWA_HEAD_DIM
    nkv = SWA_KV_HEADS * SWA_HEAD_DIM
    tab_spec = pl.BlockSpec((tm, LANES), lambda bi, si: (si, 0))
    row = lambda width: pl.BlockSpec((1, tm, width), lambda bi, si: (bi, si, 0))
    return pl.pallas_call(
        functools.partial(_swa_front_kernel, half=half),
        grid=(b, s // tm),
        in_specs=[row(d),
                  pl.BlockSpec((1, d), lambda bi, si: (0, 0)),
                  pl.BlockSpec((d, n), lambda bi, si: (0, 0)),
                  pl.BlockSpec((1, n), lambda bi, si: (0, 0)),
                  tab_spec, tab_spec, tab_spec, tab_spec, tab_spec, tab_spec],
        out_specs=[row(nq), row(2 * nkv)],
        out_shape=[jax.ShapeDtypeStruct((b, s, nq), BF16),
                   jax.ShapeDtypeStruct((b, s, 2 * nkv), BF16)],
        compiler_params=_cparams(("arbitrary", "arbitrary")),
        name="swa_front",
    )(x, g.reshape(1, d), w_qkv, b_qkv.reshape(1, n), *tabs_q, *tabs_k)


def _swa_head_order():
    g_sz = SWA_Q_HEADS // SWA_KV_HEADS
    order = []
    for slab in range(SWA_Q_HEADS // 2):
        pair, j = slab // g_sz, slab % g_sz
        order += [(2 * pair) * g_sz + j, (2 * pair + 1) * g_sz + j]
    return order


def _swa_attn_kernel(sink_ref, q_ref, kvp_ref, kvc_ref, o_ref, *, nblk):
    i = pl.program_id(1)
    hd = SWA_HEAD_DIM
    nkv = SWA_KV_HEADS * hd
    g_sz = SWA_Q_HEADS // SWA_KV_HEADS
    kv = jnp.concatenate([kvp_ref[0], kvc_ref[0]], axis=0)
    r = lax.broadcasted_iota(I32, (BLOCK, 2 * BLOCK), 0)
    c = lax.broadcasted_iota(I32, (BLOCK, 2 * BLOCK), 1)
    rel = c - BLOCK - r
    in_win = (rel <= 0) & (rel > -SWA_WINDOW)
    lane = lax.broadcasted_iota(I32, (1, LANES), 1)
    lo_half = lane < hd
    ones = jnp.ones((2 * BLOCK, LANES), BF16)
    for n in range(nblk):
        mask = in_win & ((c >= BLOCK) | (i > 0)) if n == 0 else in_win
        keys = kv[n * BLOCK:(n + 2) * BLOCK]
        for pair in range(SWA_KV_HEADS // 2):
            k2 = keys[:, pair * LANES:(pair + 1) * LANES]
            v2 = keys[:, nkv + pair * LANES:nkv + (pair + 1) * LANES]
            v_ext = jnp.concatenate([v2, ones], axis=1)
            pieces = []
            for j in range(g_sz):
                slab = pair * g_sz + j
                qs = q_ref[0, n * BLOCK:(n + 1) * BLOCK, slab * LANES:(slab + 1) * LANES]
                zero = jnp.zeros_like(qs)
                pieces += [jnp.where(lo_half, qs, zero), jnp.where(lo_half, zero, qs)]
            sc = _dot_nt(jnp.concatenate(pieces, axis=0), k2)
            probs, tails = [], []
            for pc in range(2 * g_sz):
                scp = jnp.where(mask, sc[pc * BLOCK:(pc + 1) * BLOCK], NEG_INF)
                sink = sink_ref[2 * g_sz * pair + pc] * LOG2E
                m = jnp.maximum(jnp.max(scp, axis=-1, keepdims=True), sink)
                probs.append(jnp.exp2(scp - m).astype(BF16))
                tails.append(jnp.exp2(sink - m))
            pv = _dot(jnp.concatenate(probs, axis=0), v_ext)
            for j in range(g_sz):
                slab = pair * g_sz + j
                halves = []
                for hf in range(2):
                    pc = 2 * j + hf
                    blk = pv[pc * BLOCK:(pc + 1) * BLOCK]
                    halves.append(blk[:, :LANES] / (blk[:, LANES:] + tails[pc]))
                o_ref[0, n * BLOCK:(n + 1) * BLOCK, slab * LANES:(slab + 1) * LANES] = (
                    jnp.where(lo_half, halves[0], halves[1]).astype(BF16))


def _swa_attn(q, kv, sinks):
    b, s, nq = q.shape
    nblk = 2
    tq = nblk * BLOCK
    return pl.pallas_call(
        functools.partial(_swa_attn_kernel, nblk=nblk),
        grid_spec=pltpu.PrefetchScalarGridSpec(
            num_scalar_prefetch=1,
            grid=(b, s // tq),
            in_specs=[pl.BlockSpec((1, tq, nq), lambda bi, ni, sk: (bi, ni, 0)),
                      pl.BlockSpec((1, BLOCK, kv.shape[2]), lambda bi, ni, sk: (bi, jnp.maximum(ni * nblk - 1, 0), 0)),
                      pl.BlockSpec((1, tq, kv.shape[2]), lambda bi, ni, sk: (bi, ni, 0))],
            out_specs=pl.BlockSpec((1, tq, nq), lambda bi, ni, sk: (bi, ni, 0))),
        out_shape=jax.ShapeDtypeStruct((b, s, nq), BF16),
        compiler_params=_cparams(("arbitrary", "arbitrary")),
        name="swa_attn",
    )(sinks, q, kv, kv)


def _post_kernel(*refs, tm, n_a, has_bias):
    x_ref = refs[0]
    a_refs = refs[1:1 + n_a]
    k = 1 + n_a
    wo_refs = refs[k:k + n_a]
    k += n_a
    if has_bias:
        bo_ref = refs[k]
        k += 1
    (gx_ref, wq_ref, mkv_ref, wxo_ref, gf_ref, wr_ref, br_ref,
     x2_ref, h2_ref, mi_ref, mw_ref, cnt_ref, cnt_acc) = refs[k:]
    first = (pl.program_id(0) == 0) & (pl.program_id(1) == 0)

    @pl.when(first)
    def _():
        cnt_acc[...] = jnp.zeros_like(cnt_acc)

    xw = XATTN_HEADS * XATTN_HEAD_DIM
    ones = jnp.ones((mkv_ref.shape[1], LANES), BF16)

    rh = tm // POST_CHAINS

    def out_proj(r0):
        acc = _dot(a_refs[0][0, r0:r0 + rh, :], wo_refs[0][...])
        for a_ref, w_ref in zip(a_refs[1:], wo_refs[1:]):
            acc = acc + _dot(a_ref[0, r0:r0 + rh, :], w_ref[...])
        if has_bias:
            acc = acc + bo_ref[...]
        return x_ref[0, r0:r0 + rh, :] + acc

    def q_proj(x1):
        hx = _rms(x1, gx_ref[...]).astype(BF16)
        return (_dot(hx, wq_ref[...]) * (XATTN_HEAD_DIM ** -0.5 * LOG2E)).astype(BF16)

    def mem_attn(qx):
        outs = []
        for hh in range(XATTN_HEADS):
            lo = hh * XATTN_HEAD_DIM
            mk = mkv_ref[0, :, lo:lo + XATTN_HEAD_DIM]
            mv = jnp.concatenate([mkv_ref[0, :, xw + lo:xw + lo + XATTN_HEAD_DIM], ones], axis=1)
            sc = _dot_nt(qx[:, lo:lo + XATTN_HEAD_DIM], mk)
            m = jnp.max(sc, axis=-1, keepdims=True)
            pv = _dot(jnp.exp2(sc - m).astype(BF16), mv)
            outs.append((pv[:, :LANES] / pv[:, LANES:]).astype(BF16))
        return jnp.concatenate(outs, axis=-1)

    def o_proj(r0, x1, ox):
        x2 = x1 + _dot(ox, wxo_ref[...])
        x2_ref[0, r0:r0 + rh, :] = x2
        return x2

    def router(r0, x2):
        h2 = _rms(x2, gf_ref[...])
        _to_token_tiles(h2_ref, h2, rh, r0)
        h_hi = h2.astype(BF16)
        h_lo = (h2 - h_hi.astype(F32)).astype(BF16)
        prod = _dot(jnp.concatenate([h_hi, h_lo], axis=0), wr_ref[...])
        return prod[:rh, :LANES] + prod[:rh, LANES:] + prod[rh:, :LANES] + br_ref[...]

    starts = [c * rh for c in range(POST_CHAINS)]
    x1s = [out_proj(r0) for r0 in starts]
    qxs = [q_proj(x1) for x1 in x1s]
    oxs = [mem_attn(qx) for qx in qxs]
    x2s = [o_proj(r0, x1, ox) for r0, x1, ox in zip(starts, x1s, oxs)]
    logits = jnp.concatenate([router(r0, x2) for r0, x2 in zip(starts, x2s)], axis=0)
    lane = lax.broadcasted_iota(I32, (tm, LANES), 1).astype(F32)
    big = float(LANES)
    g_lo = float(N_EXPERTS)
    lg = jnp.where((lane >= g_lo) & (lane < g_lo + N_GROUPS), logits, NEG_INF)
    mg = jnp.max(lg, axis=-1, keepdims=True)
    g_lane = jnp.min(jnp.where(lg == mg, lane, big), axis=-1, keepdims=True)
    p_g = 1.0 / jnp.sum(jnp.exp(lg - mg), axis=-1, keepdims=True)
    e_lo = (g_lane - g_lo) * EXPERTS_PER_GROUP
    le = jnp.where((lane >= e_lo) & (lane < e_lo + EXPERTS_PER_GROUP), logits, NEG_INF)
    m1 = jnp.max(le, axis=-1, keepdims=True)
    i1 = jnp.min(jnp.where(le == m1, lane, big), axis=-1, keepdims=True)
    le2 = jnp.where(lane == i1, NEG_INF, le)
    m2 = jnp.max(le2, axis=-1, keepdims=True)
    i2 = jnp.min(jnp.where(le2 == m2, lane, big), axis=-1, keepdims=True)
    t = jnp.exp(m2 - m1)
    w1 = p_g / (1.0 + t)
    w2 = p_g * t / (1.0 + t)

    oh1 = lane == i1
    oh2 = lane == i2
    oh = jnp.where(oh1 | oh2, 1.0, 0.0)
    rr = lax.broadcasted_iota(I32, (tm, tm), 0)
    cc = lax.broadcasted_iota(I32, (tm, tm), 1)
    tri = jnp.where(cc < rr, 1.0, 0.0).astype(BF16)
    before = _dot(tri, oh.astype(BF16)) + cnt_acc[...]
    r1 = jnp.sum(jnp.where(oh1, before, 0.0), axis=-1, keepdims=True)
    r2 = jnp.sum(jnp.where(oh2, before, 0.0), axis=-1, keepdims=True)
    cnt_new = cnt_acc[...] + jnp.sum(oh, axis=0, keepdims=True)
    cnt_acc[...] = cnt_new
    cnt_ref[...] = cnt_new.astype(I32)

    mw_ref[...] = jnp.where(lane == 0, w1, jnp.where(lane == 1, w2, 0.0))
    ints = jnp.where(lane == 0, i1,
                     jnp.where(lane == 1, i2,
                               jnp.where(lane == 2, r1, jnp.where(lane == 3, r2, 0.0))))
    mi_ref[...] = ints.T[0:8, :].astype(I32)


def _post_mixer(x, a_list, wo_list, b_out, g_x, w_q, mkv, w_xo, g_f, w_rt, b_rt):
    b, s, d = x.shape
    tm = ROW_TILE
    ns = s // tm
    t = b * s
    n_a = len(a_list)
    has_bias = b_out is not None
    const2 = lambda bi, si: (0, 0)
    row = lambda width: pl.BlockSpec((1, tm, width), lambda bi, si: (bi, si, 0))
    in_specs = [row(d)] + [row(a.shape[2]) for a in a_list]
    in_specs += [pl.BlockSpec(w.shape, const2) for w in wo_list]
    args = [x, *a_list, *wo_list]
    if has_bias:
        in_specs.append(pl.BlockSpec((1, d), const2))
        args.append(b_out.reshape(1, d))
    xw = w_q.shape[1]
    in_specs += [pl.BlockSpec((1, d), const2),
                 pl.BlockSpec((d, xw), const2),
                 pl.BlockSpec((1, mkv.shape[1], mkv.shape[2]), lambda bi, si: (bi, 0, 0)),
                 pl.BlockSpec((xw, d), const2),
                 pl.BlockSpec((1, d), const2),
                 pl.BlockSpec((d, 2 * LANES), const2),
                 pl.BlockSpec((1, LANES), const2)]
    args += [g_x.reshape(1, d), w_q, mkv, w_xo, g_f.reshape(1, d), w_rt, b_rt]
    out_specs = [row(d),
                 pl.BlockSpec((tm * TOKEN_ROWS, LANES), lambda bi, si: (bi * ns + si, 0)),
                 pl.BlockSpec((8, tm), lambda bi, si: (0, bi * ns + si)),
                 pl.BlockSpec((tm, LANES), lambda bi, si: (bi * ns + si, 0)),
                 pl.BlockSpec((1, LANES), const2)]
    out_shape = [jax.ShapeDtypeStruct((b, s, d), F32),
                 jax.ShapeDtypeStruct((t * TOKEN_ROWS, LANES), U32),
                 jax.ShapeDtypeStruct((8, t), I32),
                 jax.ShapeDtypeStruct((t, LANES), F32),
                 jax.ShapeDtypeStruct((1, LANES), I32)]
    return pl.pallas_call(
        functools.partial(_post_kernel, tm=tm, n_a=n_a, has_bias=has_bias),
        grid=(b, ns),
        in_specs=in_specs,
        out_specs=out_specs,
        out_shape=out_shape,
        scratch_shapes=[pltpu.VMEM((1, LANES), F32)],
        compiler_params=_cparams(("arbitrary", "arbitrary")),
        name="post_mixer",
    )(*args)


def _to_token_tiles(dst_ref, val, rows, tok0=0):
    half = TOKEN_ROWS * LANES
    for j in range(TOKEN_ROWS):
        hi = val[:, j * LANES:(j + 1) * LANES].astype(BF16).astype(F32)
        lo = val[:, half + j * LANES:half + (j + 1) * LANES].astype(BF16).astype(F32)
        word = lax.bitcast_convert_type(hi, U32) | (lax.bitcast_convert_type(lo, U32) >> 16)
        dst_ref[pl.ds(tok0 * TOKEN_ROWS + j, rows, stride=TOKEN_ROWS), :] = word


def _from_token_tiles(src_ref, rows, tok0=0):
    his, los = [], []
    for j in range(TOKEN_ROWS):
        word = src_ref[pl.ds(tok0 * TOKEN_ROWS + j, rows, stride=TOKEN_ROWS), :]
        his.append(lax.bitcast_convert_type(word & jnp.uint32(0xFFFF0000), F32))
        los.append(lax.bitcast_convert_type(word << 16, F32))
    return jnp.concatenate(his + los, axis=1)


def _token_slice(ref, tok):
    return ref.at[pl.ds(pl.multiple_of(tok * TOKEN_ROWS, TOKEN_ROWS), TOKEN_ROWS)]


def _pos_kernel(off_ref, mi_ref, pos_ref):
    e = mi_ref[0:TOP_K, :]
    pos = mi_ref[TOP_K:2 * TOP_K, :]
    for j in range(N_EXPERTS):
        pos = pos + jnp.where(e == j, off_ref[j], 0)
    pos_ref[...] = pos


def _sorted_positions(off, meta_i):
    t = meta_i.shape[1]
    return pl.pallas_call(
        _pos_kernel,
        grid_spec=pltpu.PrefetchScalarGridSpec(
            num_scalar_prefetch=1,
            grid=(1,),
            in_specs=[pl.BlockSpec((8, t), lambda i, off: (0, 0))],
            out_specs=pl.BlockSpec((TOP_K, t), lambda i, off: (0, 0))),
        out_shape=jax.ShapeDtypeStruct((TOP_K, t), I32),
        compiler_params=_cparams(("arbitrary",)),
        name="moe_positions",
    )(off, meta_i)


def _tile_major(pos, tm):
    t = pos.shape[1]
    return pos.reshape(TOP_K, t // tm, tm).transpose(1, 0, 2).reshape(-1)


def _dispatch_kernel(pos_ref, h_ref, xs_ref, sem, *, tm):
    def issue(g, _):
        for u in range(DMA_UNROLL):
            r = g * DMA_UNROLL + u
            for kk in range(TOP_K):
                pos = pos_ref[kk * tm + r]
                pltpu.make_async_copy(_token_slice(h_ref, r), _token_slice(xs_ref, pos), sem).start(priority=kk)
        return 0

    lax.fori_loop(0, tm // DMA_UNROLL, issue, 0)
    for _ in range(TOP_K):
        pltpu.make_async_copy(h_ref, xs_ref.at[pl.ds(0, tm * TOKEN_ROWS)], sem).wait()


def _dispatch(pos, h2t, n_rows):
    tm = DISPATCH_TILE
    t = h2t.shape[0] // TOKEN_ROWS
    return pl.pallas_call(
        functools.partial(_dispatch_kernel, tm=tm),
        grid=(t // tm,),
        in_specs=[pl.BlockSpec((TOP_K * tm,), lambda i: (i,), memory_space=pltpu.SMEM),
                  pl.BlockSpec((tm * TOKEN_ROWS, LANES), lambda i: (i, 0))],
        out_specs=pl.BlockSpec(memory_space=pl.ANY),
        scratch_shapes=[pltpu.SemaphoreType.DMA],
        out_shape=jax.ShapeDtypeStruct((n_rows * TOKEN_ROWS, LANES), U32),
        compiler_params=_cparams(("arbitrary",)),
        name="moe_dispatch",
    )(_tile_major(pos, tm), h2t)


def _expert_kernel(tblk_ref, texp_ref, tn_ref, xs_ref, wgu_ref, wdn_ref, ys_ref, wgu_bf, wdn_bf, *, tm):
    i = pl.program_id(0)
    n_valid = tn_ref[i]
    new_expert = (i == 0) | (texp_ref[i] != texp_ref[jnp.maximum(i - 1, 0)])

    @pl.when(new_expert)
    def _():
        wgu_bf[...] = wgu_ref[0, 0].astype(BF16)
        wdn_bf[...] = wdn_ref[0, 0].astype(BF16)

    @pl.when(n_valid > 0)
    def _():
        rh = tm // EXPERT_CHAINS
        starts = [c * rh for c in range(EXPERT_CHAINS)]
        row = lax.broadcasted_iota(I32, (rh, 1), 0)
        gus = []
        for r0 in starts:
            x = jnp.where(row + r0 < n_valid, _from_token_tiles(xs_ref, rh, r0), 0.0).astype(BF16)
            gus.append(_dot(x, wgu_bf[...]))
        ys = []
        for gu in gus:
            hid = (gu[:, :EXPERT_HIDDEN] / (1.0 + jnp.exp(-gu[:, :EXPERT_HIDDEN])) * gu[:, EXPERT_HIDDEN:]).astype(BF16)
            ys.append(_dot(hid, wdn_bf[...]))
        for r0, y in zip(starts, ys):
            _to_token_tiles(ys_ref, y, rh, r0)


def _experts(tile_blk, tile_exp, tile_n, xs, w_gu, w_dn, layer):
    tm = EXPERT_TILE
    nt = xs.shape[0] // (tm * TOKEN_ROWS)
    gu_shape, dn_shape = w_gu.shape[2:], w_dn.shape[2:]
    return pl.pallas_call(
        functools.partial(_expert_kernel, tm=tm),
        grid_spec=pltpu.PrefetchScalarGridSpec(
            num_scalar_prefetch=3,
            grid=(nt,),
            in_specs=[pl.BlockSpec((tm * TOKEN_ROWS, LANES), lambda i, tb, te, tn: (tb[i], 0)),
                      pl.BlockSpec((1, 1) + gu_shape, lambda i, tb, te, tn: (layer, te[i], 0, 0)),
                      pl.BlockSpec((1, 1) + dn_shape, lambda i, tb, te, tn: (layer, te[i], 0, 0))],
            out_specs=pl.BlockSpec((tm * TOKEN_ROWS, LANES), lambda i, tb, te, tn: (tb[i], 0)),
            scratch_shapes=[pltpu.VMEM(gu_shape, BF16), pltpu.VMEM(dn_shape, BF16)]),
        out_shape=jax.ShapeDtypeStruct(xs.shape, U32),
        compiler_params=_cparams(("arbitrary",)),
        name="moe_experts",
    )(tile_blk, tile_exp, tile_n, xs, w_gu, w_dn)


def _moe_combine_tile(pos_cur, pos_nxt, mw_ref, x, ys_ref, ybuf, sems, i, n, tm):
    def gather(pos_ref, slot):
        def issue(g, _):
            for u in range(DMA_UNROLL):
                r = g * DMA_UNROLL + u
                for kk in range(TOP_K):
                    pos = pos_ref[kk * tm + r]
                    pltpu.make_async_copy(_token_slice(ys_ref, pos), _token_slice(ybuf.at[slot, kk], r),
                                          sems.at[slot]).start(priority=kk)
            return 0

        lax.fori_loop(0, tm // DMA_UNROLL, issue, 0)

    @pl.when(i == 0)
    def _():
        gather(pos_cur, 0)

    @pl.when(i + 1 < n)
    def _():
        gather(pos_nxt, (i + 1) % 2)

    slot = i % 2
    for kk in range(TOP_K):
        pltpu.make_async_copy(ys_ref.at[pl.ds(0, tm * TOKEN_ROWS)], ybuf.at[slot, kk], sems.at[slot]).wait()
    mw = mw_ref[...]
    return (x + mw[:, 0:1] * _from_token_tiles(ybuf.at[slot, 0], tm)
            + mw[:, 1:2] * _from_token_tiles(ybuf.at[slot, 1], tm))


def _combine_kernel(pos_cur, pos_nxt, mw_ref, x_ref, ys_ref, *rest, tm, final):
    if final:
        g_ref, o_ref, ybuf, sems = rest
    else:
        o_ref, ybuf, sems = rest
    x3 = _moe_combine_tile(pos_cur, pos_nxt, mw_ref, x_ref[...], ys_ref, ybuf, sems,
                           pl.program_id(0), pl.num_programs(0), tm)
    if final:
        x3 = _rms(x3, g_ref[...])
    o_ref[...] = x3


def _combine(moe, g_final):
    pos, meta_w, x2, ys = moe
    d = x2.shape[-1]
    x2d = x2.reshape(-1, d)
    t = x2d.shape[0]
    tm = COMBINE_TILE
    nt = t // tm
    final = g_final is not None
    in_specs = [pl.BlockSpec((TOP_K * tm,), lambda i: (i,), memory_space=pltpu.SMEM),
                pl.BlockSpec((TOP_K * tm,), lambda i: (jnp.minimum(i + 1, nt - 1),), memory_space=pltpu.SMEM),
                pl.BlockSpec((tm, LANES), lambda i: (i, 0)),
                pl.BlockSpec((tm, d), lambda i: (i, 0)),
                pl.BlockSpec(memory_space=pl.ANY)]
    pos_flat = _tile_major(pos, tm)
    args = [pos_flat, pos_flat, meta_w, x2d, ys]
    if final:
        in_specs.append(pl.BlockSpec((1, d), lambda i: (0, 0)))
        args.append(g_final.reshape(1, d))
    return pl.pallas_call(
        functools.partial(_combine_kernel, tm=tm, final=final),
        grid=(nt,),
        in_specs=in_specs,
        out_specs=pl.BlockSpec((tm, d), lambda i: (i, 0)),
        scratch_shapes=[pltpu.VMEM((2, TOP_K, tm * TOKEN_ROWS, LANES), U32), pltpu.SemaphoreType.DMA((2,))],
        out_shape=jax.ShapeDtypeStruct((t, d), F32),
        compiler_params=_cparams(("arbitrary",)),
        name="moe_combine",
    )(*args)


def _plan_kernel(cnt_ref, off_ref, blk_ref, exp_ref, nv_ref, *, tm, n_tiles):
    shift = tm.bit_length() - 1

    def per_expert(e, row0):
        c = cnt_ref[0, e]
        ntile = lax.shift_right_logical(c + (tm - 1), shift)
        off_ref[e] = row0
        t0 = lax.shift_right_logical(row0, shift)

        def fill(j, _):
            blk_ref[t0 + j] = t0 + j
            exp_ref[t0 + j] = e
            nv_ref[t0 + j] = jnp.minimum(c - j * tm, tm)
            return 0

        lax.fori_loop(0, ntile, fill, 0)
        return row0 + lax.shift_left(ntile, shift)

    total = lax.fori_loop(0, N_EXPERTS, per_expert, jnp.int32(0))
    used = lax.shift_right_logical(total, shift)
    last = jnp.maximum(used - 1, 0)

    def tail(i, _):
        blk_ref[i] = last
        exp_ref[i] = exp_ref[last]
        nv_ref[i] = 0
        return 0

    lax.fori_loop(used, n_tiles, tail, 0)


def _moe_plan(counts, n_tiles):
    tm = EXPERT_TILE
    assert tm & (tm - 1) == 0
    smem = pl.BlockSpec(memory_space=pltpu.SMEM)
    return pl.pallas_call(
        functools.partial(_plan_kernel, tm=tm, n_tiles=n_tiles),
        in_specs=[smem],
        out_specs=[smem, smem, smem, smem],
        out_shape=[jax.ShapeDtypeStruct((N_EXPERTS,), I32)] + [jax.ShapeDtypeStruct((n_tiles,), I32)] * 3,
        name="moe_plan",
    )(counts)


def _moe(x2, h2t, meta_i, meta_w, counts, w_gu, w_dn, layer):
    b, s, d = x2.shape
    assert d == 2 * TOKEN_ROWS * LANES
    n_rows = b * s * TOP_K + N_EXPERTS * EXPERT_TILE
    off, blk, exp, n_valid = _moe_plan(counts, n_rows // EXPERT_TILE)
    pos = _sorted_positions(off, meta_i)
    xs = _dispatch(pos, h2t, n_rows)
    ys = _experts(blk, exp, n_valid, xs, w_gu, w_dn, layer)
    return pos, meta_w, x2, ys


def _router_weights(w_group, b_group, w_router, b_router):
    d = w_group.shape[0]
    pad = LANES - N_EXPERTS - N_GROUPS
    w = jnp.concatenate([w_router, w_group, jnp.zeros((d, pad), F32)], axis=1)
    bias = jnp.concatenate([b_router, b_group, jnp.zeros((pad,), F32)]).reshape(1, LANES)
    w_hi = w.astype(BF16)
    w_lo = (w - w_hi.astype(F32)).astype(BF16)
    return jnp.concatenate([w_hi, w_lo], axis=1), bias


def kernel(x, mem, mem_norm, mem_w_kv, norm_mix, norm_xattn, norm_ffn, hyb_w_in, hyb_conv_w, diff_lambda, diff_subln, hyb_w_out, swa_w_qkv, swa_b_qkv, swa_sinks, swa_w_out, swa_b_out, xattn_w_q, xattn_w_o, moe_w_group, moe_b_group, moe_w_router, moe_b_router, moe_w_gate_up, moe_w_down, final_norm):
    b, s, d = x.shape
    m = mem.shape[1]
    depth = norm_mix.shape[0]
    mkv = _norm_proj(mem.reshape(b * m, d), mem_norm, mem_w_kv.astype(BF16), m).reshape(b, m, -1)

    scale = DIFF_DK ** -0.5 * LOG2E
    cq, suq, sdq, half = _rope_lane_tables(s, DIFF_DK, scale)
    ck, suk, sdk, _ = _rope_lane_tables(s, DIFF_DK, 1.0)
    tabs_q, tabs_k = (cq, suq, sdq), (ck, suk, sdk)

    moe = None
    for l in range(depth):
        if moe is not None:
            x = _combine(moe, None).reshape(b, s, d)
        if l % 2 == 0:
            e = l // 2
            lambda_init = 0.8 - 0.6 * math.exp(-0.3 * l)
            ya, q, k, v = _hyb_front(x, norm_mix[l], hyb_w_in[e].astype(BF16), hyb_conv_w[e], tabs_q, tabs_k, half)
            o = _diff_attn(q, k, v, diff_lambda[e], diff_subln[e], lambda_init)
            w_out = hyb_w_out[e].astype(BF16)
            a_list, wo_list, b_out = [ya, o], [w_out[:CONV_CH], w_out[CONV_CH:]], None
        else:
            e = l // 2
            order = jnp.asarray(_swa_head_order(), I32)
            nq = SWA_Q_HEADS * SWA_HEAD_DIM
            cols = (order[:, None] * SWA_HEAD_DIM + jnp.arange(SWA_HEAD_DIM, dtype=I32)[None, :]).reshape(-1)
            cols_all = jnp.concatenate([cols, jnp.arange(nq, swa_w_qkv.shape[2], dtype=I32)])
            w_qkv = jnp.take(swa_w_qkv[e], cols_all, axis=1).astype(BF16)
            b_qkv = jnp.take(swa_b_qkv[e], cols_all)
            q, kv = _swa_front(x, norm_mix[l], w_qkv, b_qkv, tabs_q, tabs_k, half)
            o = _swa_attn(q, kv, jnp.take(swa_sinks[e], order))
            a_list, wo_list, b_out = [o], [jnp.take(swa_w_out[e], cols, axis=0).astype(BF16)], swa_b_out[e]
        w_rt, b_rt = _router_weights(moe_w_group[l], moe_b_group[l], moe_w_router[l], moe_b_router[l])
        x2, h2, meta_i, meta_w, counts = _post_mixer(
            x, a_list, wo_list, b_out, norm_xattn[l], xattn_w_q[l].astype(BF16), mkv,
            xattn_w_o[l].astype(BF16), norm_ffn[l], w_rt, b_rt)
        moe = _moe(x2, h2, meta_i, meta_w, counts, moe_w_gate_up, moe_w_down, l)
    return _combine(moe, final_norm).reshape(b, s, d)
```

```python
import functools
import math

import jax
import jax.numpy as jnp
from jax import lax
from jax.experimental import pallas as pl
from jax.experimental.pallas import tpu as pltpu

F32 = jnp.float32
BF16 = jnp.bfloat16
I32 = jnp.int32
U32 = jnp.uint32

EPS = 1e-6
LANES = 128
SUBLANES = 8
TOKEN_ROWS = 4
DMA_UNROLL = 8
VMEM_LIMIT = 56 * 1024 * 1024

ROPE_THETA = 500000.0
ROPE_FRACTION = 4
BLOCK = 128
CONV_CH = 512
CONV_K = 3
DIFF_HEADS = 4
DIFF_DK = 64
DIFF_DV = 128
SWA_Q_HEADS = 16
SWA_KV_HEADS = 4
SWA_HEAD_DIM = 64
SWA_WINDOW = 128
XATTN_HEADS = 4
XATTN_HEAD_DIM = 128
N_GROUPS = 4
EXPERTS_PER_GROUP = 8
N_EXPERTS = N_GROUPS * EXPERTS_PER_GROUP
TOP_K = 2
EXPERT_HIDDEN = 512

ROW_TILE = 512
EXPERT_TILE = 512
DISPATCH_TILE = 1024
COMBINE_TILE = 512
POST_CHAINS = 2
EXPERT_CHAINS = 2
FRONT_CHAINS = 2
NEG_INF = float("-inf")
LOG2E = math.log2(math.e)


def _cparams(sem):
    return pltpu.CompilerParams(dimension_semantics=sem, vmem_limit_bytes=VMEM_LIMIT)


def _rms(x, g):
    return x * lax.rsqrt(jnp.mean(x * x, axis=-1, keepdims=True) + EPS) * g


def _dot(a, b):
    return jnp.dot(a, b, preferred_element_type=F32)


def _dot_nt(a, b):
    return lax.dot_general(a, b, (((1,), (1,)), ((), ())), preferred_element_type=F32)


def _rope_lane_tables(seq, head_dim, scale):
    rot = head_dim // ROPE_FRACTION
    half = rot // 2
    pos = jnp.arange(seq, dtype=F32)
    inv = ROPE_THETA ** (-jnp.arange(0, rot, 2, dtype=F32) / rot)
    ang = pos[:, None] * inv[None, :]
    cos, sin = jnp.cos(ang), jnp.sin(ang)
    idx = jnp.arange(LANES) % head_dim
    cl = jnp.take(cos, idx % half, axis=1)
    sl = jnp.take(sin, idx % half, axis=1)
    c = jnp.where(idx < rot, cl, 1.0) * scale
    s_up = jnp.where(idx < half, -sl, 0.0) * scale
    s_dn = jnp.where((idx >= half) & (idx < rot), sl, 0.0) * scale
    return c.astype(F32), s_up.astype(F32), s_dn.astype(F32), half


def _rope_chunk(xc, c, s_up, s_dn, half):
    return (xc * c + pltpu.roll(xc, LANES - half, 1) * s_up + pltpu.roll(xc, half, 1) * s_dn)


def _norm_proj_kernel(x_ref, g_ref, w_ref, o_ref):
    h = _rms(x_ref[...], g_ref[...]).astype(BF16)
    o_ref[...] = _dot(h, w_ref[...]).astype(o_ref.dtype)


def _norm_proj(x2d, g, w_bf16, tm):
    m, d = x2d.shape
    n = w_bf16.shape[1]
    return pl.pallas_call(
        _norm_proj_kernel,
        grid=(m // tm,),
        in_specs=[pl.BlockSpec((tm, d), lambda i: (i, 0)),
                  pl.BlockSpec((1, d), lambda i: (0, 0)),
                  pl.BlockSpec((d, n), lambda i: (0, 0))],
        out_specs=pl.BlockSpec((tm, n), lambda i: (i, 0)),
        out_shape=jax.ShapeDtypeStruct((m, n), BF16),
        compiler_params=_cparams(("arbitrary",)),
        name="mem_kv_proj",
    )(x2d, g.reshape(1, d), w_bf16)


def _hyb_front_kernel(x_ref, g_ref, w_ref, cw_ref, cq_ref, suq_ref, sdq_ref, ck_ref, suk_ref, sdk_ref,
                      ya_ref, q_ref, k_ref, v_ref, cbuf, *, tm, half):
    s = pl.program_id(1)
    c = CONV_CH
    base = 3 * c
    nq = DIFF_HEADS * 2 * DIFF_DK
    rh = tm // FRONT_CHAINS
    starts = [ch * rh for ch in range(FRONT_CHAINS)]
    cw = cw_ref[...]

    @pl.when(s == 0)
    def _():
        cbuf[0:8, :] = jnp.zeros((8, c), F32)

    hs = [_rms(x_ref[0, r0:r0 + rh, :], g_ref[...]).astype(BF16) for r0 in starts]
    for r0, h in zip(starts, hs):
        gate_b = _dot(h, w_ref[:, 0:c])
        cu = _dot(h, w_ref[:, c:2 * c]) * _dot(h, w_ref[:, 2 * c:3 * c])
        cbuf[8 + r0:8 + r0 + rh, :] = cu
        conv = (cw[0:1, :] * cbuf[6 + r0:6 + r0 + rh, :] + cw[1:2, :] * cbuf[7 + r0:7 + r0 + rh, :] + cw[2:3, :] * cu)
        ya_ref[0, r0:r0 + rh, :] = (gate_b * conv).astype(BF16)
    cbuf[0:8, :] = cbuf[tm:tm + 8, :]

    for r0, h in zip(starts, hs):
        cq, suq, sdq = cq_ref[r0:r0 + rh, :], suq_ref[r0:r0 + rh, :], sdq_ref[r0:r0 + rh, :]
        pq = _dot(h, w_ref[:, base:base + nq])
        for j in range(nq // LANES):
            q_ref[0, r0:r0 + rh, j * LANES:(j + 1) * LANES] = _rope_chunk(
                pq[:, j * LANES:(j + 1) * LANES], cq, suq, sdq, half).astype(BF16)
    for r0, h in zip(starts, hs):
        ck, suk, sdk = ck_ref[r0:r0 + rh, :], suk_ref[r0:r0 + rh, :], sdk_ref[r0:r0 + rh, :]
        pk = _dot(h, w_ref[:, base + nq:base + 2 * nq])
        for j in range(nq // LANES):
            k_ref[0, r0:r0 + rh, j * LANES:(j + 1) * LANES] = _rope_chunk(
                pk[:, j * LANES:(j + 1) * LANES], ck, suk, sdk, half).astype(BF16)
    for r0, h in zip(starts, hs):
        v_ref[0, r0:r0 + rh, :] = _dot(h, w_ref[:, base + 2 * nq:]).astype(BF16)


def _hyb_front(x, g, w_in, conv_w, tabs_q, tabs_k, half):
    b, s, d = x.shape
    tm = ROW_TILE
    n = w_in.shape[1]
    nq = DIFF_HEADS * 2 * DIFF_DK
    nv = DIFF_HEADS * DIFF_DV
    tab_spec = pl.BlockSpec((tm, LANES), lambda bi, si: (si, 0))
    row = lambda width: pl.BlockSpec((1, tm, width), lambda bi, si: (bi, si, 0))
    return pl.pallas_call(
        functools.partial(_hyb_front_kernel, tm=tm, half=half),
        grid=(b, s // tm),
        in_specs=[row(d),
                  pl.BlockSpec((1, d), lambda bi, si: (0, 0)),
                  pl.BlockSpec((d, n), lambda bi, si: (0, 0)),
                  pl.BlockSpec((CONV_K, CONV_CH), lambda bi, si: (0, 0)),
                  tab_spec, tab_spec, tab_spec, tab_spec, tab_spec, tab_spec],
        out_specs=[row(CONV_CH), row(nq), row(nq), row(nv)],
        out_shape=[jax.ShapeDtypeStruct((b, s, CONV_CH), BF16),
                   jax.ShapeDtypeStruct((b, s, nq), BF16),
                   jax.ShapeDtypeStruct((b, s, nq), BF16),
                   jax.ShapeDtypeStruct((b, s, nv), BF16)],
        scratch_shapes=[pltpu.VMEM((tm + 8, CONV_CH), F32)],
        compiler_params=_cparams(("arbitrary", "arbitrary")),
        name="hyb_front",
    )(x, g.reshape(1, d), w_in, conv_w, *tabs_q, *tabs_k)


def _lane_fold(x, op):
    r = x[:, 0:LANES]
    for c in range(1, x.shape[1] // LANES):
        r = op(r, x[:, c * LANES:(c + 1) * LANES])
    return r


def _diff_attn_kernel(q_ref, k_ref, v_ref, lam_ref, g_ref, o_ref, sbuf, stat, acc, *, tq, hp, lambda_init):
    i = pl.program_id(2)
    lane = lax.broadcasted_iota(I32, (1, LANES), 1)
    qs = []
    for h in range(hp):
        q = q_ref[0, :, h * LANES:(h + 1) * LANES]
        zero = jnp.zeros_like(q)
        qs += [jnp.where(lane < DIFF_DK, q, zero), jnp.where(lane >= DIFF_DK, q, zero)]
    nc = 2 * hp

    def scores(j, masked):
        for c in range(nc):
            h = c // 2
            kb = k_ref[0, pl.ds(pl.multiple_of(j * tq, tq), tq), h * LANES:(h + 1) * LANES]
            sc = _dot_nt(qs[c], kb)
            if masked:
                r = lax.broadcasted_iota(I32, (tq, tq), 0)
                cc = lax.broadcasted_iota(I32, (tq, tq), 1)
                sc = jnp.where(cc <= r, sc, NEG_INF)
            sbuf[c, j] = sc
            stat[c] = jnp.maximum(stat[c], _lane_fold(sc, jnp.maximum))

    stat[...] = jnp.full(stat.shape, NEG_INF, F32)

    def pass1(j, carry):
        scores(j, False)
        return carry

    lax.fori_loop(0, i, pass1, 0)
    scores(i, True)
    ms = [jnp.max(stat[c], axis=-1, keepdims=True) for c in range(nc)]

    ones = jnp.ones((tq, LANES), BF16)

    def pass2(j, first):
        for h in range(hp):
            vb = v_ref[0, pl.ds(pl.multiple_of(j * tq, tq), tq), h * LANES:(h + 1) * LANES]
            v_ext = jnp.concatenate([vb, ones], axis=1)
            p0 = jnp.exp2(sbuf[2 * h, j] - ms[2 * h])
            p1 = jnp.exp2(sbuf[2 * h + 1, j] - ms[2 * h + 1])
            upd = _dot(jnp.concatenate([p0, p1], axis=0).astype(BF16), v_ext)
            if first:
                acc[h] = upd
            else:
                acc[h] += upd
        return 0

    pass2(0, True)
    lax.fori_loop(1, i + 1, lambda j, c: pass2(j, False), 0)

    lf = lam_ref[...]
    lam = (jnp.exp(jnp.sum(lf[0:1] * lf[1:2], keepdims=True))
           - jnp.exp(jnp.sum(lf[2:3] * lf[3:4], keepdims=True)) + lambda_init)
    for h in range(hp):
        a0 = acc[h, 0:tq, :]
        a1 = acc[h, tq:2 * tq, :]
        o = a0[:, :LANES] / a0[:, LANES:] - lam * (a1[:, :LANES] / a1[:, LANES:])
        o_ref[0, :, h * LANES:(h + 1) * LANES] = (_rms(o, g_ref[...]) * (1.0 - lambda_init)).astype(BF16)


def _diff_attn(q, k, v, lam_vecs, subln_g, lambda_init):
    b, s, _ = q.shape
    tq = 512
    hp = 4
    blk = lambda bi, hi, qi: (bi, qi, hi)
    full = lambda bi, hi, qi: (bi, 0, hi)
    return pl.pallas_call(
        functools.partial(_diff_attn_kernel, tq=tq, hp=hp, lambda_init=lambda_init),
        grid=(b, DIFF_HEADS // hp, s // tq),
        in_specs=[pl.BlockSpec((1, tq, hp * LANES), blk),
                  pl.BlockSpec((1, s, hp * LANES), full),
                  pl.BlockSpec((1, s, hp * LANES), full),
                  pl.BlockSpec((4, DIFF_DK), lambda bi, hi, qi: (0, 0)),
                  pl.BlockSpec((1, DIFF_DV), lambda bi, hi, qi: (0, 0))],
        out_specs=pl.BlockSpec((1, tq, hp * LANES), blk),
        out_shape=jax.ShapeDtypeStruct((b, s, DIFF_HEADS * DIFF_DV), BF16),
        scratch_shapes=[pltpu.VMEM((2 * hp, s // tq, tq, tq), F32),
                        pltpu.VMEM((2 * hp, tq, LANES), F32),
                        pltpu.VMEM((hp, 2 * tq, DIFF_DV + LANES), F32)],
        compiler_params=_cparams(("arbitrary", "arbitrary", "arbitrary")),
        name="diff_attn",
    )(q, k, v, lam_vecs, subln_g.reshape(1, DIFF_DV))


def _swa_front_kernel(x_ref, g_ref, w_ref, b_ref, cq_ref, suq_ref, sdq_ref, ck_ref, suk_ref, sdk_ref,
                      q_ref, kv_ref, *, half):
    nq = SWA_Q_HEADS * SWA_HEAD_DIM
    nkv = SWA_KV_HEADS * SWA_HEAD_DIM
    tm = x_ref.shape[1]
    rh = tm // FRONT_CHAINS
    starts = [ch * rh for ch in range(FRONT_CHAINS)]
    hs = [_rms(x_ref[0, r0:r0 + rh, :], g_ref[...]).astype(BF16) for r0 in starts]
    for r0, h in zip(starts, hs):
        cq, suq, sdq = cq_ref[r0:r0 + rh, :], suq_ref[r0:r0 + rh, :], sdq_ref[r0:r0 + rh, :]
        pq = _dot(h, w_ref[:, 0:nq]) + b_ref[:, 0:nq]
        for j in range(nq // LANES):
            lo = j * LANES
            q_ref[0, r0:r0 + rh, lo:lo + LANES] = _rope_chunk(pq[:, lo:lo + LANES], cq, suq, sdq, half).astype(BF16)
    for r0, h in zip(starts, hs):
        ck, suk, sdk = ck_ref[r0:r0 + rh, :], suk_ref[r0:r0 + rh, :], sdk_ref[r0:r0 + rh, :]
        pkv = _dot(h, w_ref[:, nq:]) + b_ref[:, nq:]
        for j in range(nkv // LANES):
            lo = j * LANES
            kv_ref[0, r0:r0 + rh, lo:lo + LANES] = _rope_chunk(pkv[:, lo:lo + LANES], ck, suk, sdk, half).astype(BF16)
        kv_ref[0, r0:r0 + rh, nkv:] = pkv[:, nkv:].astype(BF16)


def _swa_front(x, g, w_qkv, b_qkv, tabs_q, tabs_k, half):
    b, s, d = x.shape
    tm = ROW_TILE
    n = w_qkv.shape[1]
    nq = SWA_Q_HEADS * SWA_HEAD_DIM
    nkv = SWA_KV_HEADS * SWA_HEAD_DIM
    tab_spec = pl.BlockSpec((tm, LANES), lambda bi, si: (si, 0))
    row = lambda width: pl.BlockSpec((1, tm, width), lambda bi, si: (bi, si, 0))
    return pl.pallas_call(
        functools.partial(_swa_front_kernel, half=half),
        grid=(b, s // tm),
        in_specs=[row(d),
                  pl.BlockSpec((1, d), lambda bi, si: (0, 0)),
                  pl.BlockSpec((d, n), lambda bi, si: (0, 0)),
                  pl.BlockSpec((1, n), lambda bi, si: (0, 0)),
                  tab_spec, tab_spec, tab_spec, tab_spec, tab_spec, tab_spec],
        out_specs=[row(nq), row(2 * nkv)],
        out_shape=[jax.ShapeDtypeStruct((b, s, nq), BF16),
                   jax.ShapeDtypeStruct((b, s, 2 * nkv), BF16)],
        compiler_params=_cparams(("arbitrary", "arbitrary")),
        name="swa_front",
    )(x, g.reshape(1, d), w_qkv, b_qkv.reshape(1, n), *tabs_q, *tabs_k)


def _swa_head_order():
    g_sz = SWA_Q_HEADS // SWA_KV_HEADS
    order = []
    for slab in range(SWA_Q_HEADS // 2):
        pair, j = slab // g_sz, slab % g_sz
        order += [(2 * pair) * g_sz + j, (2 * pair + 1) * g_sz + j]
    return order


def _swa_attn_kernel(sink_ref, q_ref, kvp_ref, kvc_ref, o_ref, *, nblk):
    i = pl.program_id(1)
    hd = SWA_HEAD_DIM
    nkv = SWA_KV_HEADS * hd
    g_sz = SWA_Q_HEADS // SWA_KV_HEADS
    kv = jnp.concatenate([kvp_ref[0], kvc_ref[0]], axis=0)
    r = lax.broadcasted_iota(I32, (BLOCK, 2 * BLOCK), 0)
    c = lax.broadcasted_iota(I32, (BLOCK, 2 * BLOCK), 1)
    rel = c - BLOCK - r
    in_win = (rel <= 0) & (rel > -SWA_WINDOW)
    lane = lax.broadcasted_iota(I32, (1, LANES), 1)
    lo_half = lane < hd
    ones = jnp.ones((2 * BLOCK, LANES), BF16)
    for n in range(nblk):
        mask = in_win & ((c >= BLOCK) | (i > 0)) if n == 0 else in_win
        keys = kv[n * BLOCK:(n + 2) * BLOCK]
        for pair in range(SWA_KV_HEADS // 2):
            k2 = keys[:, pair * LANES:(pair + 1) * LANES]
            v2 = keys[:, nkv + pair * LANES:nkv + (pair + 1) * LANES]
            v_ext = jnp.concatenate([v2, ones], axis=1)
            pieces = []
            for j in range(g_sz):
                slab = pair * g_sz + j
                qs = q_ref[0, n * BLOCK:(n + 1) * BLOCK, slab * LANES:(slab + 1) * LANES]
                zero = jnp.zeros_like(qs)
                pieces += [jnp.where(lo_half, qs, zero), jnp.where(lo_half, zero, qs)]
            sc = _dot_nt(jnp.concatenate(pieces, axis=0), k2)
            probs, tails = [], []
            for pc in range(2 * g_sz):
                scp = jnp.where(mask, sc[pc * BLOCK:(pc + 1) * BLOCK], NEG_INF)
                sink = sink_ref[2 * g_sz * pair + pc] * LOG2E
                m = jnp.maximum(jnp.max(scp, axis=-1, keepdims=True), sink)
                probs.append(jnp.exp2(scp - m).astype(BF16))
                tails.append(jnp.exp2(sink - m))
            pv = _dot(jnp.concatenate(probs, axis=0), v_ext)
            for j in range(g_sz):
                slab = pair * g_sz + j
                halves = []
                for hf in range(2):
                    pc = 2 * j + hf
                    blk = pv[pc * BLOCK:(pc + 1) * BLOCK]
                    halves.append(blk[:, :LANES] / (blk[:, LANES:] + tails[pc]))
                o_ref[0, n * BLOCK:(n + 1) * BLOCK, slab * LANES:(slab + 1) * LANES] = (
                    jnp.where(lo_half, halves[0], halves[1]).astype(BF16))


def _swa_attn(q, kv, sinks):
    b, s, nq = q.shape
    nblk = 2
    tq = nblk * BLOCK
    return pl.pallas_call(
        functools.partial(_swa_attn_kernel, nblk=nblk),
        grid_spec=pltpu.PrefetchScalarGridSpec(
            num_scalar_prefetch=1,
            grid=(b, s // tq),
            in_specs=[pl.BlockSpec((1, tq, nq), lambda bi, ni, sk: (bi, ni, 0)),
                      pl.BlockSpec((1, BLOCK, kv.shape[2]), lambda bi, ni, sk: (bi, jnp.maximum(ni * nblk - 1, 0), 0)),
                      pl.BlockSpec((1, tq, kv.shape[2]), lambda bi, ni, sk: (bi, ni, 0))],
            out_specs=pl.BlockSpec((1, tq, nq), lambda bi, ni, sk: (bi, ni, 0))),
        out_shape=jax.ShapeDtypeStruct((b, s, nq), BF16),
        compiler_params=_cparams(("arbitrary", "arbitrary")),
        name="swa_attn",
    )(sinks, q, kv, kv)


def _post_kernel(*refs, tm, n_a, has_bias):
    x_ref = refs[0]
    a_refs = refs[1:1 + n_a]
    k = 1 + n_a
    wo_refs = refs[k:k + n_a]
    k += n_a
    if has_bias:
        bo_ref = refs[k]
        k += 1
    (gx_ref, wq_ref, mkv_ref, wxo_ref, gf_ref, wr_ref, br_ref,
     x2_ref, h2_ref, mi_ref, mw_ref, cnt_ref, cnt_acc) = refs[k:]
    first = (pl.program_id(0) == 0) & (pl.program_id(1) == 0)

    @pl.when(first)
    def _():
        cnt_acc[...] = jnp.zeros_like(cnt_acc)

    xw = XATTN_HEADS * XATTN_HEAD_DIM
    ones = jnp.ones((mkv_ref.shape[1], LANES), BF16)

    rh = tm // POST_CHAINS

    def out_proj(r0):
        acc = _dot(a_refs[0][0, r0:r0 + rh, :], wo_refs[0][...])
        for a_ref, w_ref in zip(a_refs[1:], wo_refs[1:]):
            acc = acc + _dot(a_ref[0, r0:r0 + rh, :], w_ref[...])
        if has_bias:
            acc = acc + bo_ref[...]
        return x_ref[0, r0:r0 + rh, :] + acc

    def q_proj(x1):
        hx = _rms(x1, gx_ref[...]).astype(BF16)
        return (_dot(hx, wq_ref[...]) * (XATTN_HEAD_DIM ** -0.5 * LOG2E)).astype(BF16)

    def mem_attn(qx):
        outs = []
        for hh in range(XATTN_HEADS):
            lo = hh * XATTN_HEAD_DIM
            mk = mkv_ref[0, :, lo:lo + XATTN_HEAD_DIM]
            mv = jnp.concatenate([mkv_ref[0, :, xw + lo:xw + lo + XATTN_HEAD_DIM], ones], axis=1)
            sc = _dot_nt(qx[:, lo:lo + XATTN_HEAD_DIM], mk)
            m = jnp.max(sc, axis=-1, keepdims=True)
            pv = _dot(jnp.exp2(sc - m).astype(BF16), mv)
            outs.append((pv[:, :LANES] / pv[:, LANES:]).astype(BF16))
        return jnp.concatenate(outs, axis=-1)

    def o_proj(r0, x1, ox):
        x2 = x1 + _dot(ox, wxo_ref[...])
        x2_ref[0, r0:r0 + rh, :] = x2
        return x2

    def router(r0, x2):
        h2 = _rms(x2, gf_ref[...])
        _to_token_tiles(h2_ref, h2, rh, r0)
        h_hi = h2.astype(BF16)
        h_lo = (h2 - h_hi.astype(F32)).astype(BF16)
        prod = _dot(jnp.concatenate([h_hi, h_lo], axis=0), wr_ref[...])
        return prod[:rh, :LANES] + prod[:rh, LANES:] + prod[rh:, :LANES] + br_ref[...]

    starts = [c * rh for c in range(POST_CHAINS)]
    x1s = [out_proj(r0) for r0 in starts]
    qxs = [q_proj(x1) for x1 in x1s]
    oxs = [mem_attn(qx) for qx in qxs]
    x2s = [o_proj(r0, x1, ox) for r0, x1, ox in zip(starts, x1s, oxs)]
    logits = jnp.concatenate([router(r0, x2) for r0, x2 in zip(starts, x2s)], axis=0)
    lane = lax.broadcasted_iota(I32, (tm, LANES), 1).astype(F32)
    big = float(LANES)
    g_lo = float(N_EXPERTS)
    lg = jnp.where((lane >= g_lo) & (lane < g_lo + N_GROUPS), logits, NEG_INF)
    mg = jnp.max(lg, axis=-1, keepdims=True)
    g_lane = jnp.min(jnp.where(lg == mg, lane, big), axis=-1, keepdims=True)
    p_g = 1.0 / jnp.sum(jnp.exp(lg - mg), axis=-1, keepdims=True)
    e_lo = (g_lane - g_lo) * EXPERTS_PER_GROUP
    le = jnp.where((lane >= e_lo) & (lane < e_lo + EXPERTS_PER_GROUP), logits, NEG_INF)
    m1 = jnp.max(le, axis=-1, keepdims=True)
    i1 = jnp.min(jnp.where(le == m1, lane, big), axis=-1, keepdims=True)
    le2 = jnp.where(lane == i1, NEG_INF, le)
    m2 = jnp.max(le2, axis=-1, keepdims=True)
    i2 = jnp.min(jnp.where(le2 == m2, lane, big), axis=-1, keepdims=True)
    t = jnp.exp(m2 - m1)
    w1 = p_g / (1.0 + t)
    w2 = p_g * t / (1.0 + t)

    oh1 = lane == i1
    oh2 = lane == i2
    oh = jnp.where(oh1 | oh2, 1.0, 0.0)
    rr = lax.broadcasted_iota(I32, (tm, tm), 0)
    cc = lax.broadcasted_iota(I32, (tm, tm), 1)
    tri = jnp.where(cc < rr, 1.0, 0.0).astype(BF16)
    before = _dot(tri, oh.astype(BF16)) + cnt_acc[...]
    r1 = jnp.sum(jnp.where(oh1, before, 0.0), axis=-1, keepdims=True)
    r2 = jnp.sum(jnp.where(oh2, before, 0.0), axis=-1, keepdims=True)
    cnt_new = cnt_acc[...] + jnp.sum(oh, axis=0, keepdims=True)
    cnt_acc[...] = cnt_new
    cnt_ref[...] = cnt_new.astype(I32)

    mw_ref[...] = jnp.where(lane == 0, w1, jnp.where(lane == 1, w2, 0.0))
    ints = jnp.where(lane == 0, i1,
                     jnp.where(lane == 1, i2,
                               jnp.where(lane == 2, r1, jnp.where(lane == 3, r2, 0.0))))
    mi_ref[...] = ints.T[0:8, :].astype(I32)


def _post_mixer(x, a_list, wo_list, b_out, g_x, w_q, mkv, w_xo, g_f, w_rt, b_rt):
    b, s, d = x.shape
    tm = ROW_TILE
    ns = s // tm
    t = b * s
    n_a = len(a_list)
    has_bias = b_out is not None
    const2 = lambda bi, si: (0, 0)
    row = lambda width: pl.BlockSpec((1, tm, width), lambda bi, si: (bi, si, 0))
    in_specs = [row(d)] + [row(a.shape[2]) for a in a_list]
    in_specs += [pl.BlockSpec(w.shape, const2) for w in wo_list]
    args = [x, *a_list, *wo_list]
    if has_bias:
        in_specs.append(pl.BlockSpec((1, d), const2))
        args.append(b_out.reshape(1, d))
    xw = w_q.shape[1]
    in_specs += [pl.BlockSpec((1, d), const2),
                 pl.BlockSpec((d, xw), const2),
                 pl.BlockSpec((1, mkv.shape[1], mkv.shape[2]), lambda bi, si: (bi, 0, 0)),
                 pl.BlockSpec((xw, d), const2),
                 pl.BlockSpec((1, d), const2),
                 pl.BlockSpec((d, 2 * LANES), const2),
                 pl.BlockSpec((1, LANES), const2)]
    args += [g_x.reshape(1, d), w_q, mkv, w_xo, g_f.reshape(1, d), w_rt, b_rt]
    out_specs = [row(d),
                 pl.BlockSpec((tm * TOKEN_ROWS, LANES), lambda bi, si: (bi * ns + si, 0)),
                 pl.BlockSpec((8, tm), lambda bi, si: (0, bi * ns + si)),
                 pl.BlockSpec((tm, LANES), lambda bi, si: (bi * ns + si, 0)),
                 pl.BlockSpec((1, LANES), const2)]
    out_shape = [jax.ShapeDtypeStruct((b, s, d), F32),
                 jax.ShapeDtypeStruct((t * TOKEN_ROWS, LANES), U32),
                 jax.ShapeDtypeStruct((8, t), I32),
                 jax.ShapeDtypeStruct((t, LANES), F32),
                 jax.ShapeDtypeStruct((1, LANES), I32)]
    return pl.pallas_call(
        functools.partial(_post_kernel, tm=tm, n_a=n_a, has_bias=has_bias),
        grid=(b, ns),
        in_specs=in_specs,
        out_specs=out_specs,
        out_shape=out_shape,
        scratch_shapes=[pltpu.VMEM((1, LANES), F32)],
        compiler_params=_cparams(("arbitrary", "arbitrary")),
        name="post_mixer",
    )(*args)


def _to_token_tiles(dst_ref, val, rows, tok0=0):
    half = TOKEN_ROWS * LANES
    for j in range(TOKEN_ROWS):
        hi = val[:, j * LANES:(j + 1) * LANES].astype(BF16).astype(F32)
        lo = val[:, half + j * LANES:half + (j + 1) * LANES].astype(BF16).astype(F32)
        word = lax.bitcast_convert_type(hi, U32) | (lax.bitcast_convert_type(lo, U32) >> 16)
        dst_ref[pl.ds(tok0 * TOKEN_ROWS + j, rows, stride=TOKEN_ROWS), :] = word


def _from_token_tiles(src_ref, rows, tok0=0):
    his, los = [], []
    for j in range(TOKEN_ROWS):
        word = src_ref[pl.ds(tok0 * TOKEN_ROWS + j, rows, stride=TOKEN_ROWS), :]
        his.append(lax.bitcast_convert_type(word & jnp.uint32(0xFFFF0000), F32))
        los.append(lax.bitcast_convert_type(word << 16, F32))
    return jnp.concatenate(his + los, axis=1)


def _token_slice(ref, tok):
    return ref.at[pl.ds(pl.multiple_of(tok * TOKEN_ROWS, TOKEN_ROWS), TOKEN_ROWS)]


def _pos_kernel(off_ref, mi_ref, pos_ref):
    e = mi_ref[0:TOP_K, :]
    pos = mi_ref[TOP_K:2 * TOP_K, :]
    for j in range(N_EXPERTS):
        pos = pos + jnp.where(e == j, off_ref[j], 0)
    pos_ref[...] = pos


def _sorted_positions(off, meta_i):
    t = meta_i.shape[1]
    return pl.pallas_call(
        _pos_kernel,
        grid_spec=pltpu.PrefetchScalarGridSpec(
            num_scalar_prefetch=1,
            grid=(1,),
            in_specs=[pl.BlockSpec((8, t), lambda i, off: (0, 0))],
            out_specs=pl.BlockSpec((TOP_K, t), lambda i, off: (0, 0))),
        out_shape=jax.ShapeDtypeStruct((TOP_K, t), I32),
        compiler_params=_cparams(("arbitrary",)),
        name="moe_positions",
    )(off, meta_i)


def _tile_major(pos, tm):
    t = pos.shape[1]
    return pos.reshape(TOP_K, t // tm, tm).transpose(1, 0, 2).reshape(-1)


def _dispatch_kernel(pos_ref, h_ref, h_hbm, xs_ref, sems, *, tm):
    base = pl.program_id(0) * tm

    def issue(g, _):
        for u in range(DMA_UNROLL):
            r = g * DMA_UNROLL + u
            pltpu.make_async_copy(_token_slice(h_ref, r), _token_slice(xs_ref, pos_ref[r]), sems.at[0]).start()
            pltpu.make_async_copy(_token_slice(h_hbm, base + r), _token_slice(xs_ref, pos_ref[tm + r]),
                                  sems.at[1]).start()
        return 0

    lax.fori_loop(0, tm // DMA_UNROLL, issue, 0)
    whole = pl.ds(0, tm * TOKEN_ROWS)
    pltpu.make_async_copy(h_ref, xs_ref.at[whole], sems.at[0]).wait()
    pltpu.make_async_copy(h_hbm.at[whole], xs_ref.at[whole], sems.at[1]).wait()


def _dispatch(pos, h2t, n_rows):
    tm = DISPATCH_TILE
    t = h2t.shape[0] // TOKEN_ROWS
    return pl.pallas_call(
        functools.partial(_dispatch_kernel, tm=tm),
        grid=(t // tm,),
        in_specs=[pl.BlockSpec((TOP_K * tm,), lambda i: (i,), memory_space=pltpu.SMEM),
                  pl.BlockSpec((tm * TOKEN_ROWS, LANES), lambda i: (i, 0)),
                  pl.BlockSpec(memory_space=pl.ANY)],
        out_specs=pl.BlockSpec(memory_space=pl.ANY),
        scratch_shapes=[pltpu.SemaphoreType.DMA((2,))],
        out_shape=jax.ShapeDtypeStruct((n_rows * TOKEN_ROWS, LANES), U32),
        compiler_params=_cparams(("arbitrary",)),
        name="moe_dispatch",
    )(_tile_major(pos, tm), h2t, h2t)


def _expert_kernel(tblk_ref, texp_ref, tn_ref, xs_ref, wgu_ref, wdn_ref, ys_ref, wgu_bf, wdn_bf, *, tm):
    i = pl.program_id(0)
    n_valid = tn_ref[i]
    new_expert = (i == 0) | (texp_ref[i] != texp_ref[jnp.maximum(i - 1, 0)])

    @pl.when(new_expert)
    def _():
        wgu_bf[...] = wgu_ref[0, 0].astype(BF16)
        wdn_bf[...] = wdn_ref[0, 0].astype(BF16)

    @pl.when(n_valid > 0)
    def _():
        rh = tm // EXPERT_CHAINS
        starts = [c * rh for c in range(EXPERT_CHAINS)]
        row = lax.broadcasted_iota(I32, (rh, 1), 0)
        gus = []
        for r0 in starts:
            x = jnp.where(row + r0 < n_valid, _from_token_tiles(xs_ref, rh, r0), 0.0).astype(BF16)
            gus.append(_dot(x, wgu_bf[...]))
        ys = []
        for gu in gus:
            hid = (gu[:, :EXPERT_HIDDEN] / (1.0 + jnp.exp(-gu[:, :EXPERT_HIDDEN])) * gu[:, EXPERT_HIDDEN:]).astype(BF16)
            ys.append(_dot(hid, wdn_bf[...]))
        for r0, y in zip(starts, ys):
            _to_token_tiles(ys_ref, y, rh, r0)


def _experts(tile_blk, tile_exp, tile_n, xs, w_gu, w_dn, layer):
    tm = EXPERT_TILE
    nt = xs.shape[0] // (tm * TOKEN_ROWS)
    gu_shape, dn_shape = w_gu.shape[2:], w_dn.shape[2:]
    return pl.pallas_call(
        functools.partial(_expert_kernel, tm=tm),
        grid_spec=pltpu.PrefetchScalarGridSpec(
            num_scalar_prefetch=3,
            grid=(nt,),
            in_specs=[pl.BlockSpec((tm * TOKEN_ROWS, LANES), lambda i, tb, te, tn: (tb[i], 0)),
                      pl.BlockSpec((1, 1) + gu_shape, lambda i, tb, te, tn: (layer, te[i], 0, 0)),
                      pl.BlockSpec((1, 1) + dn_shape, lambda i, tb, te, tn: (layer, te[i], 0, 0))],
            out_specs=pl.BlockSpec((tm * TOKEN_ROWS, LANES), lambda i, tb, te, tn: (tb[i], 0)),
            scratch_shapes=[pltpu.VMEM(gu_shape, BF16), pltpu.VMEM(dn_shape, BF16)]),
        out_shape=jax.ShapeDtypeStruct(xs.shape, U32),
        compiler_params=_cparams(("arbitrary",)),
        name="moe_experts",
    )(tile_blk, tile_exp, tile_n, xs, w_gu, w_dn)


def _moe_combine_tile(pos_cur, pos_nxt, mw_ref, x, ys_ref, ybuf, sems, i, n, tm):
    def gather(pos_ref, slot):
        def issue(g, _):
            for u in range(DMA_UNROLL):
                r = g * DMA_UNROLL + u
                for kk in range(TOP_K):
                    pos = pos_ref[kk * tm + r]
                    pltpu.make_async_copy(_token_slice(ys_ref, pos), _token_slice(ybuf.at[slot, kk], r),
                                          sems.at[slot]).start(priority=kk)
            return 0

        lax.fori_loop(0, tm // DMA_UNROLL, issue, 0)

    @pl.when(i == 0)
    def _():
        gather(pos_cur, 0)

    @pl.when(i + 1 < n)
    def _():
        gather(pos_nxt, (i + 1) % 2)

    slot = i % 2
    for kk in range(TOP_K):
        pltpu.make_async_copy(ys_ref.at[pl.ds(0, tm * TOKEN_ROWS)], ybuf.at[slot, kk], sems.at[slot]).wait()
    mw = mw_ref[...]
    return (x + mw[:, 0:1] * _from_token_tiles(ybuf.at[slot, 0], tm)
            + mw[:, 1:2] * _from_token_tiles(ybuf.at[slot, 1], tm))


def _combine_kernel(pos_cur, pos_nxt, mw_ref, x_ref, ys_ref, *rest, tm, final):
    if final:
        g_ref, o_ref, ybuf, sems = rest
    else:
        o_ref, ybuf, sems = rest
    x3 = _moe_combine_tile(pos_cur, pos_nxt, mw_ref, x_ref[...], ys_ref, ybuf, sems,
                           pl.program_id(0), pl.num_programs(0), tm)
    if final:
        x3 = _rms(x3, g_ref[...])
    o_ref[...] = x3


def _combine(moe, g_final):
    pos, meta_w, x2, ys = moe
    d = x2.shape[-1]
    x2d = x2.reshape(-1, d)
    t = x2d.shape[0]
    tm = COMBINE_TILE
    nt = t // tm
    final = g_final is not None
    in_specs = [pl.BlockSpec((TOP_K * tm,), lambda i: (i,), memory_space=pltpu.SMEM),
                pl.BlockSpec((TOP_K * tm,), lambda i: (jnp.minimum(i + 1, nt - 1),), memory_space=pltpu.SMEM),
                pl.BlockSpec((tm, LANES), lambda i: (i, 0)),
                pl.BlockSpec((tm, d), lambda i: (i, 0)),
                pl.BlockSpec(memory_space=pl.ANY)]
    pos_flat = _tile_major(pos, tm)
    args = [pos_flat, pos_flat, meta_w, x2d, ys]
    if final:
        in_specs.append(pl.BlockSpec((1, d), lambda i: (0, 0)))
        args.append(g_final.reshape(1, d))
    return pl.pallas_call(
        functools.partial(_combine_kernel, tm=tm, final=final),
        grid=(nt,),
        in_specs=in_specs,
        out_specs=pl.BlockSpec((tm, d), lambda i: (i, 0)),
        scratch_shapes=[pltpu.VMEM((2, TOP_K, tm * TOKEN_ROWS, LANES), U32), pltpu.SemaphoreType.DMA((2,))],
        out_shape=jax.ShapeDtypeStruct((t, d), F32),
        compiler_params=_cparams(("arbitrary",)),
        name="moe_combine",
    )(*args)


def _plan_kernel(cnt_ref, off_ref, blk_ref, exp_ref, nv_ref, *, tm, n_tiles):
    shift = tm.bit_length() - 1

    def per_expert(e, row0):
        c = cnt_ref[0, e]
        ntile = lax.shift_right_logical(c + (tm - 1), shift)
        off_ref[e] = row0
        t0 = lax.shift_right_logical(row0, shift)

        def fill(j, _):
            blk_ref[t0 + j] = t0 + j
            exp_ref[t0 + j] = e
            nv_ref[t0 + j] = jnp.minimum(c - j * tm, tm)
            return 0

        lax.fori_loop(0, ntile, fill, 0)
        return row0 + lax.shift_left(ntile, shift)

    total = lax.fori_loop(0, N_EXPERTS, per_expert, jnp.int32(0))
    used = lax.shift_right_logical(total, shift)
    last = jnp.maximum(used - 1, 0)

    def tail(i, _):
        blk_ref[i] = last
        exp_ref[i] = exp_ref[last]
        nv_ref[i] = 0
        return 0

    lax.fori_loop(used, n_tiles, tail, 0)


def _moe_plan(counts, n_tiles):
    tm = EXPERT_TILE
    assert tm & (tm - 1) == 0
    smem = pl.BlockSpec(memory_space=pltpu.SMEM)
    return pl.pallas_call(
        functools.partial(_plan_kernel, tm=tm, n_tiles=n_tiles),
        in_specs=[smem],
        out_specs=[smem, smem, smem, smem],
        out_shape=[jax.ShapeDtypeStruct((N_EXPERTS,), I32)] + [jax.ShapeDtypeStruct((n_tiles,), I32)] * 3,
        name="moe_plan",
    )(counts)


def _moe(x2, h2t, meta_i, meta_w, counts, w_gu, w_dn, layer):
    b, s, d = x2.shape
    assert d == 2 * TOKEN_ROWS * LANES
    n_rows = b * s * TOP_K + N_EXPERTS * EXPERT_TILE
    off, blk, exp, n_valid = _moe_plan(counts, n_rows // EXPERT_TILE)
    pos = _sorted_positions(off, meta_i)
    xs = _dispatch(pos, h2t, n_rows)
    ys = _experts(blk, exp, n_valid, xs, w_gu, w_dn, layer)
    return pos, meta_w, x2, ys


def _router_weights(w_group, b_group, w_router, b_router):
    d = w_group.shape[0]
    pad = LANES - N_EXPERTS - N_GROUPS
    w = jnp.concatenate([w_router, w_group, jnp.zeros((d, pad), F32)], axis=1)
    bias = jnp.concatenate([b_router, b_group, jnp.zeros((pad,), F32)]).reshape(1, LANES)
    w_hi = w.astype(BF16)
    w_lo = (w - w_hi.astype(F32)).astype(BF16)
    return jnp.concatenate([w_hi, w_lo], axis=1), bias


def kernel(x, mem, mem_norm, mem_w_kv, norm_mix, norm_xattn, norm_ffn, hyb_w_in, hyb_conv_w, diff_lambda, diff_subln, hyb_w_out, swa_w_qkv, swa_b_qkv, swa_sinks, swa_w_out, swa_b_out, xattn_w_q, xattn_w_o, moe_w_group, moe_b_group, moe_w_router, moe_b_router, moe_w_gate_up, moe_w_down, final_norm):
    b, s, d = x.shape
    m = mem.shape[1]
    depth = norm_mix.shape[0]
    mkv = _norm_proj(mem.reshape(b * m, d), mem_norm, mem_w_kv.astype(BF16), m).reshape(b, m, -1)

    scale = DIFF_DK ** -0.5 * LOG2E
    cq, suq, sdq, half = _rope_lane_tables(s, DIFF_DK, scale)
    ck, suk, sdk, _ = _rope_lane_tables(s, DIFF_DK, 1.0)
    tabs_q, tabs_k = (cq, suq, sdq), (ck, suk, sdk)

    moe = None
    for l in range(depth):
        if moe is not None:
            x = _combine(moe, None).reshape(b, s, d)
        if l % 2 == 0:
            e = l // 2
            lambda_init = 0.8 - 0.6 * math.exp(-0.3 * l)
            ya, q, k, v = _hyb_front(x, norm_mix[l], hyb_w_in[e].astype(BF16), hyb_conv_w[e], tabs_q, tabs_k, half)
            o = _diff_attn(q, k, v, diff_lambda[e], diff_subln[e], lambda_init)
            w_out = hyb_w_out[e].astype(BF16)
            a_list, wo_list, b_out = [ya, o], [w_out[:CONV_CH], w_out[CONV_CH:]], None
        else:
            e = l // 2
            order = jnp.asarray(_swa_head_order(), I32)
            nq = SWA_Q_HEADS * SWA_HEAD_DIM
            cols = (order[:, None] * SWA_HEAD_DIM + jnp.arange(SWA_HEAD_DIM, dtype=I32)[None, :]).reshape(-1)
            cols_all = jnp.concatenate([cols, jnp.arange(nq, swa_w_qkv.shape[2], dtype=I32)])
            w_qkv = jnp.take(swa_w_qkv[e], cols_all, axis=1).astype(BF16)
            b_qkv = jnp.take(swa_b_qkv[e], cols_all)
            q, kv = _swa_front(x, norm_mix[l], w_qkv, b_qkv, tabs_q, tabs_k, half)
            o = _swa_attn(q, kv, jnp.take(swa_sinks[e], order))
            a_list, wo_list, b_out = [o], [jnp.take(swa_w_out[e], cols, axis=0).astype(BF16)], swa_b_out[e]
        w_rt, b_rt = _router_weights(moe_w_group[l], moe_b_group[l], moe_w_router[l], moe_b_router[l])
        x2, h2, meta_i, meta_w, counts = _post_mixer(
            x, a_list, wo_list, b_out, norm_xattn[l], xattn_w_q[l].astype(BF16), mkv,
            xattn_w_o[l].astype(BF16), norm_ffn[l], w_rt, b_rt)
        moe = _moe(x2, h2, meta_i, meta_w, counts, moe_w_gate_up, moe_w_down, l)
    return _combine(moe, final_norm).reshape(b, s, d)
```

```python
import functools
import math

import jax
import jax.numpy as jnp
from jax import lax
from jax.experimental import pallas as pl
from jax.experimental.pallas import tpu as pltpu

F32 = jnp.float32
BF16 = jnp.bfloat16
I32 = jnp.int32
U32 = jnp.uint32

EPS = 1e-6
LANES = 128
SUBLANES = 8
TOKEN_ROWS = 4
DMA_UNROLL = 8
VMEM_LIMIT = 56 * 1024 * 1024

ROPE_THETA = 500000.0
ROPE_FRACTION = 4
BLOCK = 128
CONV_CH = 512
CONV_K = 3
DIFF_HEADS = 4
DIFF_DK = 64
DIFF_DV = 128
SWA_Q_HEADS = 16
SWA_KV_HEADS = 4
SWA_HEAD_DIM = 64
SWA_WINDOW = 128
XATTN_HEADS = 4
XATTN_HEAD_DIM = 128
N_GROUPS = 4
EXPERTS_PER_GROUP = 8
N_EXPERTS = N_GROUPS * EXPERTS_PER_GROUP
TOP_K = 2
EXPERT_HIDDEN = 512

ROW_TILE = 512
EXPERT_TILE = 512
DISPATCH_TILE = 2048
COMBINE_TILE = 1024
POST_CHAINS = 2
EXPERT_CHAINS = 2
FRONT_CHAINS = 2
NEG_INF = float("-inf")
LOG2E = math.log2(math.e)


def _cparams(sem):
    return pltpu.CompilerParams(dimension_semantics=sem, vmem_limit_bytes=VMEM_LIMIT)


def _rms(x, g):
    return x * lax.rsqrt(jnp.mean(x * x, axis=-1, keepdims=True) + EPS) * g


def _dot(a, b):
    return jnp.dot(a, b, preferred_element_type=F32)


def _dot_nt(a, b):
    return lax.dot_general(a, b, (((1,), (1,)), ((), ())), preferred_element_type=F32)


def _rope_lane_tables(seq, head_dim, scale):
    rot = head_dim // ROPE_FRACTION
    half = rot // 2
    pos = jnp.arange(seq, dtype=F32)
    inv = ROPE_THETA ** (-jnp.arange(0, rot, 2, dtype=F32) / rot)
    ang = pos[:, None] * inv[None, :]
    cos, sin = jnp.cos(ang), jnp.sin(ang)
    idx = jnp.arange(LANES) % head_dim
    cl = jnp.take(cos, idx % half, axis=1)
    sl = jnp.take(sin, idx % half, axis=1)
    c = jnp.where(idx < rot, cl, 1.0) * scale
    s_up = jnp.where(idx < half, -sl, 0.0) * scale
    s_dn = jnp.where((idx >= half) & (idx < rot), sl, 0.0) * scale
    return c.astype(F32), s_up.astype(F32), s_dn.astype(F32), half


def _rope_chunk(xc, c, s_up, s_dn, half):
    return (xc * c + pltpu.roll(xc, LANES - half, 1) * s_up + pltpu.roll(xc, half, 1) * s_dn)


def _norm_proj_kernel(x_ref, g_ref, w_ref, o_ref):
    h = _rms(x_ref[...], g_ref[...]).astype(BF16)
    o_ref[...] = _dot(h, w_ref[...]).astype(o_ref.dtype)


def _norm_proj(x2d, g, w_bf16, tm):
    m, d = x2d.shape
    n = w_bf16.shape[1]
    return pl.pallas_call(
        _norm_proj_kernel,
        grid=(m // tm,),
        in_specs=[pl.BlockSpec((tm, d), lambda i: (i, 0)),
                  pl.BlockSpec((1, d), lambda i: (0, 0)),
                  pl.BlockSpec((d, n), lambda i: (0, 0))],
        out_specs=pl.BlockSpec((tm, n), lambda i: (i, 0)),
        out_shape=jax.ShapeDtypeStruct((m, n), BF16),
        compiler_params=_cparams(("arbitrary",)),
        name="mem_kv_proj",
    )(x2d, g.reshape(1, d), w_bf16)


def _hyb_front_kernel(x_ref, g_ref, w_ref, cw_ref, cq_ref, suq_ref, sdq_ref, ck_ref, suk_ref, sdk_ref,
                      ya_ref, q_ref, k_ref, v_ref, cbuf, *, tm, half):
    s = pl.program_id(1)
    c = CONV_CH
    base = 3 * c
    nq = DIFF_HEADS * 2 * DIFF_DK
    rh = tm // FRONT_CHAINS
    starts = [ch * rh for ch in range(FRONT_CHAINS)]
    cw = cw_ref[...]

    @pl.when(s == 0)
    def _():
        cbuf[0:8, :] = jnp.zeros((8, c), F32)

    hs = [_rms(x_ref[0, r0:r0 + rh, :], g_ref[...]).astype(BF16) for r0 in starts]
    for r0, h in zip(starts, hs):
        gate_b = _dot(h, w_ref[:, 0:c])
        cu = _dot(h, w_ref[:, c:2 * c]) * _dot(h, w_ref[:, 2 * c:3 * c])
        cbuf[8 + r0:8 + r0 + rh, :] = cu
        conv = (cw[0:1, :] * cbuf[6 + r0:6 + r0 + rh, :] + cw[1:2, :] * cbuf[7 + r0:7 + r0 + rh, :] + cw[2:3, :] * cu)
        ya_ref[0, r0:r0 + rh, :] = (gate_b * conv).astype(BF16)
    cbuf[0:8, :] = cbuf[tm:tm + 8, :]

    for r0, h in zip(starts, hs):
        cq, suq, sdq = cq_ref[r0:r0 + rh, :], suq_ref[r0:r0 + rh, :], sdq_ref[r0:r0 + rh, :]
        pq = _dot(h, w_ref[:, base:base + nq])
        for j in range(nq // LANES):
            q_ref[0, r0:r0 + rh, j * LANES:(j + 1) * LANES] = _rope_chunk(
                pq[:, j * LANES:(j + 1) * LANES], cq, suq, sdq, half).astype(BF16)
    for r0, h in zip(starts, hs):
        ck, suk, sdk = ck_ref[r0:r0 + rh, :], suk_ref[r0:r0 + rh, :], sdk_ref[r0:r0 + rh, :]
        pk = _dot(h, w_ref[:, base + nq:base + 2 * nq])
        for j in range(nq // LANES):
            k_ref[0, r0:r0 + rh, j * LANES:(j + 1) * LANES] = _rope_chunk(
                pk[:, j * LANES:(j + 1) * LANES], ck, suk, sdk, half).astype(BF16)
    for r0, h in zip(starts, hs):
        v_ref[0, r0:r0 + rh, :] = _dot(h, w_ref[:, base + 2 * nq:]).astype(BF16)


def _hyb_front(x, g, w_in, conv_w, tabs_q, tabs_k, half):
    b, s, d = x.shape
    tm = ROW_TILE
    n = w_in.shape[1]
    nq = DIFF_HEADS * 2 * DIFF_DK
    nv = DIFF_HEADS * DIFF_DV
    tab_spec = pl.BlockSpec((tm, LANES), lambda bi, si: (si, 0))
    row = lambda width: pl.BlockSpec((1, tm, width), lambda bi, si: (bi, si, 0))
    return pl.pallas_call(
        functools.partial(_hyb_front_kernel, tm=tm, half=half),
        grid=(b, s // tm),
        in_specs=[row(d),
                  pl.BlockSpec((1, d), lambda bi, si: (0, 0)),
                  pl.BlockSpec((d, n), lambda bi, si: (0, 0)),
                  pl.BlockSpec((CONV_K, CONV_CH), lambda bi, si: (0, 0)),
                  tab_spec, tab_spec, tab_spec, tab_spec, tab_spec, tab_spec],
        out_specs=[row(CONV_CH), row(nq), row(nq), row(nv)],
        out_shape=[jax.ShapeDtypeStruct((b, s, CONV_CH), BF16),
                   jax.ShapeDtypeStruct((b, s, nq), BF16),
                   jax.ShapeDtypeStruct((b, s, nq), BF16),
                   jax.ShapeDtypeStruct((b, s, nv), BF16)],
        scratch_shapes=[pltpu.VMEM((tm + 8, CONV_CH), F32)],
        compiler_params=_cparams(("arbitrary", "arbitrary")),
        name="hyb_front",
    )(x, g.reshape(1, d), w_in, conv_w, *tabs_q, *tabs_k)


def _lane_fold(x, op):
    r = x[:, 0:LANES]
    for c in range(1, x.shape[1] // LANES):
        r = op(r, x[:, c * LANES:(c + 1) * LANES])
    return r


def _diff_attn_kernel(q_ref, k_ref, v_ref, lam_ref, g_ref, o_ref, sbuf, stat, acc, *, tq, hp, lambda_init):
    i = pl.program_id(2)
    lane = lax.broadcasted_iota(I32, (1, LANES), 1)
    qs = []
    for h in range(hp):
        q = q_ref[0, :, h * LANES:(h + 1) * LANES]
        zero = jnp.zeros_like(q)
        qs += [jnp.where(lane < DIFF_DK, q, zero), jnp.where(lane >= DIFF_DK, q, zero)]
    nc = 2 * hp

    hq = tq // 2
    diag = pl.multiple_of(i * tq, tq)

    def pass1(j, carry):
        for c in range(nc):
            h = c // 2
            kb = k_ref[0, pl.ds(pl.multiple_of(j * tq, tq), tq), h * LANES:(h + 1) * LANES]
            sc = _dot_nt(qs[c], kb)
            sbuf[c, j] = sc
            stat[c] = jnp.maximum(stat[c], _lane_fold(sc, jnp.maximum))
        return carry

    stat[...] = jnp.full(stat.shape, NEG_INF, F32)
    lax.fori_loop(0, i, pass1, 0)

    r = lax.broadcasted_iota(I32, (hq, hq), 0)
    cc = lax.broadcasted_iota(I32, (hq, hq), 1)
    tri = cc <= r
    for c in range(nc):
        h = c // 2
        k_lo = k_ref[0, pl.ds(diag, hq), h * LANES:(h + 1) * LANES]
        k_hi = k_ref[0, pl.ds(diag + hq, hq), h * LANES:(h + 1) * LANES]
        s_tl = jnp.where(tri, _dot_nt(qs[c][:hq], k_lo), NEG_INF)
        s_bl = _dot_nt(qs[c][hq:], k_lo)
        s_br = jnp.where(tri, _dot_nt(qs[c][hq:], k_hi), NEG_INF)
        sbuf[c, i, 0:hq, 0:hq] = s_tl
        sbuf[c, i, hq:tq, 0:hq] = s_bl
        sbuf[c, i, hq:tq, hq:tq] = s_br
        stat[c, 0:hq, :] = jnp.maximum(stat[c, 0:hq, :], _lane_fold(s_tl, jnp.maximum))
        stat[c, hq:tq, :] = jnp.maximum(stat[c, hq:tq, :],
                                        jnp.maximum(_lane_fold(s_bl, jnp.maximum), _lane_fold(s_br, jnp.maximum)))
    ms = [jnp.max(stat[c], axis=-1, keepdims=True) for c in range(nc)]

    ones = jnp.ones((tq, LANES), BF16)
    for h in range(hp):
        vb = v_ref[0, pl.ds(diag, tq), h * LANES:(h + 1) * LANES]
        v_ext = jnp.concatenate([vb, ones], axis=1)
        tops, bots = [], []
        for c in (2 * h, 2 * h + 1):
            tops.append(jnp.exp2(sbuf[c, i, 0:hq, 0:hq] - ms[c][:hq]))
            bots.append(jnp.exp2(sbuf[c, i, hq:tq, :] - ms[c][hq:]))
        top = _dot(jnp.concatenate(tops, axis=0).astype(BF16), v_ext[:hq])
        bot = _dot(jnp.concatenate(bots, axis=0).astype(BF16), v_ext)
        acc[h, 0:hq, :] = top[:hq]
        acc[h, hq:tq, :] = bot[:hq]
        acc[h, tq:tq + hq, :] = top[hq:]
        acc[h, tq + hq:2 * tq, :] = bot[hq:]

    def pass2(j, carry):
        for h in range(hp):
            vb = v_ref[0, pl.ds(pl.multiple_of(j * tq, tq), tq), h * LANES:(h + 1) * LANES]
            v_ext = jnp.concatenate([vb, ones], axis=1)
            p0 = jnp.exp2(sbuf[2 * h, j] - ms[2 * h])
            p1 = jnp.exp2(sbuf[2 * h + 1, j] - ms[2 * h + 1])
            acc[h] += _dot(jnp.concatenate([p0, p1], axis=0).astype(BF16), v_ext)
        return carry

    lax.fori_loop(0, i, pass2, 0)

    lf = lam_ref[...]
    lam = (jnp.exp(jnp.sum(lf[0:1] * lf[1:2], keepdims=True))
           - jnp.exp(jnp.sum(lf[2:3] * lf[3:4], keepdims=True)) + lambda_init)
    for h in range(hp):
        a0 = acc[h, 0:tq, :]
        a1 = acc[h, tq:2 * tq, :]
        o = a0[:, :LANES] / a0[:, LANES:] - lam * (a1[:, :LANES] / a1[:, LANES:])
        o_ref[0, :, h * LANES:(h + 1) * LANES] = (_rms(o, g_ref[...]) * (1.0 - lambda_init)).astype(BF16)


def _diff_attn(q, k, v, lam_vecs, subln_g, lambda_init):
    b, s, _ = q.shape
    tq = 512
    hp = 4
    blk = lambda bi, hi, qi: (bi, qi, hi)
    full = lambda bi, hi, qi: (bi, 0, hi)
    return pl.pallas_call(
        functools.partial(_diff_attn_kernel, tq=tq, hp=hp, lambda_init=lambda_init),
        grid=(b, DIFF_HEADS // hp, s // tq),
        in_specs=[pl.BlockSpec((1, tq, hp * LANES), blk),
                  pl.BlockSpec((1, s, hp * LANES), full),
                  pl.BlockSpec((1, s, hp * LANES), full),
                  pl.BlockSpec((4, DIFF_DK), lambda bi, hi, qi: (0, 0)),
                  pl.BlockSpec((1, DIFF_DV), lambda bi, hi, qi: (0, 0))],
        out_specs=pl.BlockSpec((1, tq, hp * LANES), blk),
        out_shape=jax.ShapeDtypeStruct((b, s, DIFF_HEADS * DIFF_DV), BF16),
        scratch_shapes=[pltpu.VMEM((2 * hp, s // tq, tq, tq), F32),
                        pltpu.VMEM((2 * hp, tq, LANES), F32),
                        pltpu.VMEM((hp, 2 * tq, DIFF_DV + LANES), F32)],
        compiler_params=_cparams(("arbitrary", "arbitrary", "arbitrary")),
        name="diff_attn",
    )(q, k, v, lam_vecs, subln_g.reshape(1, DIFF_DV))


def _swa_front_kernel(x_ref, g_ref, w_ref, b_ref, cq_ref, suq_ref, sdq_ref, ck_ref, suk_ref, sdk_ref,
                      q_ref, kv_ref, *, half):
    nq = SWA_Q_HEADS * SWA_HEAD_DIM
    nkv = SWA_KV_HEADS * SWA_HEAD_DIM
    tm = x_ref.shape[1]
    rh = tm // FRONT_CHAINS
    starts = [ch * rh for ch in range(FRONT_CHAINS)]
    hs = [_rms(x_ref[0, r0:r0 + rh, :], g_ref[...]).astype(BF16) for r0 in starts]
    for r0, h in zip(starts, hs):
        cq, suq, sdq = cq_ref[r0:r0 + rh, :], suq_ref[r0:r0 + rh, :], sdq_ref[r0:r0 + rh, :]
        pq = _dot(h, w_ref[:, 0:nq]) + b_ref[:, 0:nq]
        for j in range(nq // LANES):
            lo = j * LANES
            q_ref[0, r0:r0 + rh, lo:lo + LANES] = _rope_chunk(pq[:, lo:lo + LANES], cq, suq, sdq, half).astype(BF16)
    for r0, h in zip(starts, hs):
        ck, suk, sdk = ck_ref[r0:r0 + rh, :], suk_ref[r0:r0 + rh, :], sdk_ref[r0:r0 + rh, :]
        pkv = _dot(h, w_ref[:, nq:]) + b_ref[:, nq:]
        for j in range(nkv // LANES):
            lo = j * LANES
            kv_ref[0, r0:r0 + rh, lo:lo + LANES] = _rope_chunk(pkv[:, lo:lo + LANES], ck, suk, sdk, half).astype(BF16)
        kv_ref[0, r0:r0 + rh, nkv:] = pkv[:, nkv:].astype(BF16)


def _swa_front(x, g, w_qkv, b_qkv, tabs_q, tabs_k, half):
    b, s, d = x.shape
    tm = ROW_TILE
    n = w_qkv.shape[1]
    nq = SWA_Q_HEADS * SWA_HEAD_DIM
    nkv = SWA_KV_HEADS * SWA_HEAD_DIM
    tab_spec = pl.BlockSpec((tm, LANES), lambda bi, si: (si, 0))
    row = lambda width: pl.BlockSpec((1, tm, width), lambda bi, si: (bi, si, 0))
    return pl.pallas_call(
        functools.partial(_swa_front_kernel, half=half),
        grid=(b, s // tm),
        in_specs=[row(d),
                  pl.BlockSpec((1, d), lambda bi, si: (0, 0)),
                  pl.BlockSpec((d, n), lambda bi, si: (0, 0)),
                  pl.BlockSpec((1, n), lambda bi, si: (0, 0)),
                  tab_spec, tab_spec, tab_spec, tab_spec, tab_spec, tab_spec],
        out_specs=[row(nq), row(2 * nkv)],
        out_shape=[jax.ShapeDtypeStruct((b, s, nq), BF16),
                   jax.ShapeDtypeStruct((b, s, 2 * nkv), BF16)],
        compiler_params=_cparams(("arbitrary", "arbitrary")),
        name="swa_front",
    )(x, g.reshape(1, d), w_qkv, b_qkv.reshape(1, n), *tabs_q, *tabs_k)


def _swa_head_order():
    g_sz = SWA_Q_HEADS // SWA_KV_HEADS
    order = []
    for slab in range(SWA_Q_HEADS // 2):
        pair, j = slab // g_sz, slab % g_sz
        order += [(2 * pair) * g_sz + j, (2 * pair + 1) * g_sz + j]
    return order


def _swa_attn_kernel(sink_ref, q_ref, kvp_ref, kvc_ref, o_ref, *, nblk):
    i = pl.program_id(1)
    hd = SWA_HEAD_DIM
    nkv = SWA_KV_HEADS * hd
    g_sz = SWA_Q_HEADS // SWA_KV_HEADS
    kv = jnp.concatenate([kvp_ref[0], kvc_ref[0]], axis=0)
    r = lax.broadcasted_iota(I32, (BLOCK, 2 * BLOCK), 0)
    c = lax.broadcasted_iota(I32, (BLOCK, 2 * BLOCK), 1)
    rel = c - BLOCK - r
    in_win = (rel <= 0) & (rel > -SWA_WINDOW)
    lane = lax.broadcasted_iota(I32, (1, LANES), 1)
    lo_half = lane < hd
    ones = jnp.ones((2 * BLOCK, LANES), BF16)
    for n in range(nblk):
        mask = in_win & ((c >= BLOCK) | (i > 0)) if n == 0 else in_win
        keys = kv[n * BLOCK:(n + 2) * BLOCK]
        for pair in range(SWA_KV_HEADS // 2):
            k2 = keys[:, pair * LANES:(pair + 1) * LANES]
            v2 = keys[:, nkv + pair * LANES:nkv + (pair + 1) * LANES]
            v_ext = jnp.concatenate([v2, ones], axis=1)
            pieces = []
            for j in range(g_sz):
                slab = pair * g_sz + j
                qs = q_ref[0, n * BLOCK:(n + 1) * BLOCK, slab * LANES:(slab + 1) * LANES]
                zero = jnp.zeros_like(qs)
                pieces += [jnp.where(lo_half, qs, zero), jnp.where(lo_half, zero, qs)]
            sc = _dot_nt(jnp.concatenate(pieces, axis=0), k2)
            probs, tails = [], []
            for pc in range(2 * g_sz):
                scp = jnp.where(mask, sc[pc * BLOCK:(pc + 1) * BLOCK], NEG_INF)
                sink = sink_ref[2 * g_sz * pair + pc] * LOG2E
                m = jnp.maximum(jnp.max(scp, axis=-1, keepdims=True), sink)
                probs.append(jnp.exp2(scp - m).astype(BF16))
                tails.append(jnp.exp2(sink - m))
            pv = _dot(jnp.concatenate(probs, axis=0), v_ext)
            for j in range(g_sz):
                slab = pair * g_sz + j
                halves = []
                for hf in range(2):
                    pc = 2 * j + hf
                    blk = pv[pc * BLOCK:(pc + 1) * BLOCK]
                    halves.append(blk[:, :LANES] / (blk[:, LANES:] + tails[pc]))
                o_ref[0, n * BLOCK:(n + 1) * BLOCK, slab * LANES:(slab + 1) * LANES] = (
                    jnp.where(lo_half, halves[0], halves[1]).astype(BF16))


def _swa_attn(q, kv, sinks):
    b, s, nq = q.shape
    nblk = 2
    tq = nblk * BLOCK
    return pl.pallas_call(
        functools.partial(_swa_attn_kernel, nblk=nblk),
        grid_spec=pltpu.PrefetchScalarGridSpec(
            num_scalar_prefetch=1,
            grid=(b, s // tq),
            in_specs=[pl.BlockSpec((1, tq, nq), lambda bi, ni, sk: (bi, ni, 0)),
                      pl.BlockSpec((1, BLOCK, kv.shape[2]), lambda bi, ni, sk: (bi, jnp.maximum(ni * nblk - 1, 0), 0)),
                      pl.BlockSpec((1, tq, kv.shape[2]), lambda bi, ni, sk: (bi, ni, 0))],
            out_specs=pl.BlockSpec((1, tq, nq), lambda bi, ni, sk: (bi, ni, 0))),
        out_shape=jax.ShapeDtypeStruct((b, s, nq), BF16),
        compiler_params=_cparams(("arbitrary", "arbitrary")),
        name="swa_attn",
    )(sinks, q, kv, kv)


def _post_kernel(*refs, tm, n_a, has_bias):
    x_ref = refs[0]
    a_refs = refs[1:1 + n_a]
    k = 1 + n_a
    wo_refs = refs[k:k + n_a]
    k += n_a
    if has_bias:
        bo_ref = refs[k]
        k += 1
    (gx_ref, wq_ref, mkv_ref, wxo_ref, gf_ref, wr_ref, br_ref,
     x2_ref, h2_ref, mi_ref, mw_ref, cnt_ref, cnt_acc) = refs[k:]
    first = (pl.program_id(0) == 0) & (pl.program_id(1) == 0)

    @pl.when(first)
    def _():
        cnt_acc[...] = jnp.zeros_like(cnt_acc)

    xw = XATTN_HEADS * XATTN_HEAD_DIM
    ones = jnp.ones((mkv_ref.shape[1], LANES), BF16)

    rh = tm // POST_CHAINS

    def out_proj(r0):
        acc = _dot(a_refs[0][0, r0:r0 + rh, :], wo_refs[0][...])
        for a_ref, w_ref in zip(a_refs[1:], wo_refs[1:]):
            acc = acc + _dot(a_ref[0, r0:r0 + rh, :], w_ref[...])
        if has_bias:
            acc = acc + bo_ref[...]
        return x_ref[0, r0:r0 + rh, :] + acc

    def q_proj(x1):
        hx = _rms(x1, gx_ref[...]).astype(BF16)
        return (_dot(hx, wq_ref[...]) * (XATTN_HEAD_DIM ** -0.5 * LOG2E)).astype(BF16)

    def mem_attn(qx):
        outs = []
        for hh in range(XATTN_HEADS):
            lo = hh * XATTN_HEAD_DIM
            mk = mkv_ref[0, :, lo:lo + XATTN_HEAD_DIM]
            mv = jnp.concatenate([mkv_ref[0, :, xw + lo:xw + lo + XATTN_HEAD_DIM], ones], axis=1)
            sc = _dot_nt(qx[:, lo:lo + XATTN_HEAD_DIM], mk)
            m = jnp.max(sc, axis=-1, keepdims=True)
            pv = _dot(jnp.exp2(sc - m).astype(BF16), mv)
            outs.append((pv[:, :LANES] / pv[:, LANES:]).astype(BF16))
        return jnp.concatenate(outs, axis=-1)

    def o_proj(r0, x1, ox):
        x2 = x1 + _dot(ox, wxo_ref[...])
        x2_ref[0, r0:r0 + rh, :] = x2
        return x2

    def router(r0, x2):
        h2 = _rms(x2, gf_ref[...])
        _to_token_tiles(h2_ref, h2, rh, r0)
        h_hi = h2.astype(BF16)
        h_lo = (h2 - h_hi.astype(F32)).astype(BF16)
        prod = _dot(jnp.concatenate([h_hi, h_lo], axis=0), wr_ref[...])
        return prod[:rh, :LANES] + prod[:rh, LANES:] + prod[rh:, :LANES] + br_ref[...]

    starts = [c * rh for c in range(POST_CHAINS)]
    x1s = [out_proj(r0) for r0 in starts]
    qxs = [q_proj(x1) for x1 in x1s]
    oxs = [mem_attn(qx) for qx in qxs]
    x2s = [o_proj(r0, x1, ox) for r0, x1, ox in zip(starts, x1s, oxs)]
    logits = jnp.concatenate([router(r0, x2) for r0, x2 in zip(starts, x2s)], axis=0)
    lane = lax.broadcasted_iota(I32, (tm, LANES), 1).astype(F32)
    big = float(LANES)
    g_lo = float(N_EXPERTS)
    lg = jnp.where((lane >= g_lo) & (lane < g_lo + N_GROUPS), logits, NEG_INF)
    mg = jnp.max(lg, axis=-1, keepdims=True)
    g_lane = jnp.min(jnp.where(lg == mg, lane, big), axis=-1, keepdims=True)
    p_g = 1.0 / jnp.sum(jnp.exp(lg - mg), axis=-1, keepdims=True)
    e_lo = (g_lane - g_lo) * EXPERTS_PER_GROUP
    le = jnp.where((lane >= e_lo) & (lane < e_lo + EXPERTS_PER_GROUP), logits, NEG_INF)
    m1 = jnp.max(le, axis=-1, keepdims=True)
    i1 = jnp.min(jnp.where(le == m1, lane, big), axis=-1, keepdims=True)
    le2 = jnp.where(lane == i1, NEG_INF, le)
    m2 = jnp.max(le2, axis=-1, keepdims=True)
    i2 = jnp.min(jnp.where(le2 == m2, lane, big), axis=-1, keepdims=True)
    t = jnp.exp(m2 - m1)
    w1 = p_g / (1.0 + t)
    w2 = p_g * t / (1.0 + t)

    oh1 = lane == i1
    oh2 = lane == i2
    oh = jnp.where(oh1 | oh2, 1.0, 0.0)
    rr = lax.broadcasted_iota(I32, (tm, tm), 0)
    cc = lax.broadcasted_iota(I32, (tm, tm), 1)
    tri = jnp.where(cc < rr, 1.0, 0.0).astype(BF16)
    before = _dot(tri, oh.astype(BF16)) + cnt_acc[...]
    r1 = jnp.sum(jnp.where(oh1, before, 0.0), axis=-1, keepdims=True)
    r2 = jnp.sum(jnp.where(oh2, before, 0.0), axis=-1, keepdims=True)
    cnt_new = cnt_acc[...] + jnp.sum(oh, axis=0, keepdims=True)
    cnt_acc[...] = cnt_new
    cnt_ref[...] = cnt_new.astype(I32)

    mw_ref[...] = jnp.where(lane == 0, w1, jnp.where(lane == 1, w2, 0.0))
    ints = jnp.where(lane == 0, i1,
                     jnp.where(lane == 1, i2,
                               jnp.where(lane == 2, r1, jnp.where(lane == 3, r2, 0.0))))
    mi_ref[...] = ints.T[0:8, :].astype(I32)


def _post_mixer(x, a_list, wo_list, b_out, g_x, w_q, mkv, w_xo, g_f, w_rt, b_rt):
    b, s, d = x.shape
    tm = ROW_TILE
    ns = s // tm
    t = b * s
    n_a = len(a_list)
    has_bias = b_out is not None
    const2 = lambda bi, si: (0, 0)
    row = lambda width: pl.BlockSpec((1, tm, width), lambda bi, si: (bi, si, 0))
    in_specs = [row(d)] + [row(a.shape[2]) for a in a_list]
    in_specs += [pl.BlockSpec(w.shape, const2) for w in wo_list]
    args = [x, *a_list, *wo_list]
    if has_bias:
        in_specs.append(pl.BlockSpec((1, d), const2))
        args.append(b_out.reshape(1, d))
    xw = w_q.shape[1]
    in_specs += [pl.BlockSpec((1, d), const2),
                 pl.BlockSpec((d, xw), const2),
                 pl.BlockSpec((1, mkv.shape[1], mkv.shape[2]), lambda bi, si: (bi, 0, 0)),
                 pl.BlockSpec((xw, d), const2),
                 pl.BlockSpec((1, d), const2),
                 pl.BlockSpec((d, 2 * LANES), const2),
                 pl.BlockSpec((1, LANES), const2)]
    args += [g_x.reshape(1, d), w_q, mkv, w_xo, g_f.reshape(1, d), w_rt, b_rt]
    out_specs = [row(d),
                 pl.BlockSpec((tm * TOKEN_ROWS, LANES), lambda bi, si: (bi * ns + si, 0)),
                 pl.BlockSpec((8, tm), lambda bi, si: (0, bi * ns + si)),
                 pl.BlockSpec((tm, LANES), lambda bi, si: (bi * ns + si, 0)),
                 pl.BlockSpec((1, LANES), const2)]
    out_shape = [jax.ShapeDtypeStruct((b, s, d), F32),
                 jax.ShapeDtypeStruct((t * TOKEN_ROWS, LANES), U32),
                 jax.ShapeDtypeStruct((8, t), I32),
                 jax.ShapeDtypeStruct((t, LANES), F32),
                 jax.ShapeDtypeStruct((1, LANES), I32)]
    return pl.pallas_call(
        functools.partial(_post_kernel, tm=tm, n_a=n_a, has_bias=has_bias),
        grid=(b, ns),
        in_specs=in_specs,
        out_specs=out_specs,
        out_shape=out_shape,
        scratch_shapes=[pltpu.VMEM((1, LANES), F32)],
        compiler_params=_cparams(("arbitrary", "arbitrary")),
        name="post_mixer",
    )(*args)


def _to_token_tiles(dst_ref, val, rows, tok0=0):
    half = TOKEN_ROWS * LANES
    for j in range(TOKEN_ROWS):
        hi = val[:, j * LANES:(j + 1) * LANES].astype(BF16).astype(F32)
        lo = val[:, half + j * LANES:half + (j + 1) * LANES].astype(BF16).astype(F32)
        word = lax.bitcast_convert_type(hi, U32) | (lax.bitcast_convert_type(lo, U32) >> 16)
        dst_ref[pl.ds(tok0 * TOKEN_ROWS + j, rows, stride=TOKEN_ROWS), :] = word


def _from_token_tiles(src_ref, rows, tok0=0):
    his, los = [], []
    for j in range(TOKEN_ROWS):
        word = src_ref[pl.ds(tok0 * TOKEN_ROWS + j, rows, stride=TOKEN_ROWS), :]
        his.append(lax.bitcast_convert_type(word & jnp.uint32(0xFFFF0000), F32))
        los.append(lax.bitcast_convert_type(word << 16, F32))
    return jnp.concatenate(his + los, axis=1)


def _token_slice(ref, tok):
    return ref.at[pl.ds(pl.multiple_of(tok * TOKEN_ROWS, TOKEN_ROWS), TOKEN_ROWS)]


def _pos_kernel(off_ref, mi_ref, pos_ref):
    e = mi_ref[0:TOP_K, :]
    pos = mi_ref[TOP_K:2 * TOP_K, :]
    for j in range(N_EXPERTS):
        pos = pos + jnp.where(e == j, off_ref[j], 0)
    pos_ref[...] = pos


def _sorted_positions(off, meta_i):
    t = meta_i.shape[1]
    return pl.pallas_call(
        _pos_kernel,
        grid_spec=pltpu.PrefetchScalarGridSpec(
            num_scalar_prefetch=1,
            grid=(1,),
            in_specs=[pl.BlockSpec((8, t), lambda i, off: (0, 0))],
            out_specs=pl.BlockSpec((TOP_K, t), lambda i, off: (0, 0))),
        out_shape=jax.ShapeDtypeStruct((TOP_K, t), I32),
        compiler_params=_cparams(("arbitrary",)),
        name="moe_positions",
    )(off, meta_i)


def _tile_major(pos, tm):
    t = pos.shape[1]
    return pos.reshape(TOP_K, t // tm, tm).transpose(1, 0, 2).reshape(-1)


def _dispatch_kernel(pos_ref, h_ref, xs_ref, sem, *, tm):
    def issue(g, _):
        for u in range(DMA_UNROLL):
            r = g * DMA_UNROLL + u
            for kk in range(TOP_K):
                pos = pos_ref[kk * tm + r]
                pltpu.make_async_copy(_token_slice(h_ref, r), _token_slice(xs_ref, pos), sem).start(priority=kk)
        return 0

    lax.fori_loop(0, tm // DMA_UNROLL, issue, 0)
    for _ in range(TOP_K):
        pltpu.make_async_copy(h_ref, xs_ref.at[pl.ds(0, tm * TOKEN_ROWS)], sem).wait()


def _dispatch(pos, h2t, n_rows):
    tm = DISPATCH_TILE
    t = h2t.shape[0] // TOKEN_ROWS
    return pl.pallas_call(
        functools.partial(_dispatch_kernel, tm=tm),
        grid=(t // tm,),
        in_specs=[pl.BlockSpec((TOP_K * tm,), lambda i: (i,), memory_space=pltpu.SMEM),
                  pl.BlockSpec((tm * TOKEN_ROWS, LANES), lambda i: (i, 0))],
        out_specs=pl.BlockSpec(memory_space=pl.ANY),
        scratch_shapes=[pltpu.SemaphoreType.DMA],
        out_shape=jax.ShapeDtypeStruct((n_rows * TOKEN_ROWS, LANES), U32),
        compiler_params=_cparams(("arbitrary",)),
        name="moe_dispatch",
    )(_tile_major(pos, tm), h2t)


def _expert_kernel(tblk_ref, texp_ref, tn_ref, xs_ref, wgu_ref, wdn_ref, ys_ref, wgu_bf, wdn_bf, *, tm):
    i = pl.program_id(0)
    n_valid = tn_ref[i]
    new_expert = (i == 0) | (texp_ref[i] != texp_ref[jnp.maximum(i - 1, 0)])

    @pl.when(new_expert)
    def _():
        wgu_bf[...] = wgu_ref[0, 0].astype(BF16)
        wdn_bf[...] = wdn_ref[0, 0].astype(BF16)

    @pl.when(n_valid > 0)
    def _():
        rh = tm // EXPERT_CHAINS
        starts = [c * rh for c in range(EXPERT_CHAINS)]
        row = lax.broadcasted_iota(I32, (rh, 1), 0)
        gus = []
        for r0 in starts:
            x = jnp.where(row + r0 < n_valid, _from_token_tiles(xs_ref, rh, r0), 0.0).astype(BF16)
            gus.append(_dot(x, wgu_bf[...]))
        ys = []
        for gu in gus:
            hid = (gu[:, :EXPERT_HIDDEN] / (1.0 + jnp.exp(-gu[:, :EXPERT_HIDDEN])) * gu[:, EXPERT_HIDDEN:]).astype(BF16)
            ys.append(_dot(hid, wdn_bf[...]))
        for r0, y in zip(starts, ys):
            _to_token_tiles(ys_ref, y, rh, r0)


def _experts(tile_blk, tile_exp, tile_n, xs, w_gu, w_dn, layer):
    tm = EXPERT_TILE
    nt = xs.shape[0] // (tm * TOKEN_ROWS)
    gu_shape, dn_shape = w_gu.shape[2:], w_dn.shape[2:]
    return pl.pallas_call(
        functools.partial(_expert_kernel, tm=tm),
        grid_spec=pltpu.PrefetchScalarGridSpec(
            num_scalar_prefetch=3,
            grid=(nt,),
            in_specs=[pl.BlockSpec((tm * TOKEN_ROWS, LANES), lambda i, tb, te, tn: (tb[i], 0)),
                      pl.BlockSpec((1, 1) + gu_shape, lambda i, tb, te, tn: (layer, te[i], 0, 0)),
                      pl.BlockSpec((1, 1) + dn_shape, lambda i, tb, te, tn: (layer, te[i], 0, 0))],
            out_specs=pl.BlockSpec((tm * TOKEN_ROWS, LANES), lambda i, tb, te, tn: (tb[i], 0)),
            scratch_shapes=[pltpu.VMEM(gu_shape, BF16), pltpu.VMEM(dn_shape, BF16)]),
        out_shape=jax.ShapeDtypeStruct(xs.shape, U32),
        compiler_params=_cparams(("arbitrary",)),
        name="moe_experts",
    )(tile_blk, tile_exp, tile_n, xs, w_gu, w_dn)


def _moe_combine_tile(pos_cur, pos_nxt, mw_ref, x, ys_ref, ybuf, sems, i, n, tm):
    def gather(pos_ref, slot):
        def issue(g, _):
            for u in range(DMA_UNROLL):
                r = g * DMA_UNROLL + u
                for kk in range(TOP_K):
                    pos = pos_ref[kk * tm + r]
                    pltpu.make_async_copy(_token_slice(ys_ref, pos), _token_slice(ybuf.at[slot, kk], r),
                                          sems.at[slot]).start(priority=kk)
            return 0

        lax.fori_loop(0, tm // DMA_UNROLL, issue, 0)

    @pl.when(i == 0)
    def _():
        gather(pos_cur, 0)

    @pl.when(i + 1 < n)
    def _():
        gather(pos_nxt, (i + 1) % 2)

    slot = i % 2
    for kk in range(TOP_K):
        pltpu.make_async_copy(ys_ref.at[pl.ds(0, tm * TOKEN_ROWS)], ybuf.at[slot, kk], sems.at[slot]).wait()
    mw = mw_ref[...]
    return (x + mw[:, 0:1] * _from_token_tiles(ybuf.at[slot, 0], tm)
            + mw[:, 1:2] * _from_token_tiles(ybuf.at[slot, 1], tm))


def _combine_kernel(pos_cur, pos_nxt, mw_ref, x_ref, ys_ref, *rest, tm, final):
    if final:
        g_ref, o_ref, ybuf, sems = rest
    else:
        o_ref, ybuf, sems = rest
    x3 = _moe_combine_tile(pos_cur, pos_nxt, mw_ref, x_ref[...], ys_ref, ybuf, sems,
                           pl.program_id(0), pl.num_programs(0), tm)
    if final:
        x3 = _rms(x3, g_ref[...])
    o_ref[...] = x3


def _combine(moe, g_final):
    pos, meta_w, x2, ys = moe
    d = x2.shape[-1]
    x2d = x2.reshape(-1, d)
    t = x2d.shape[0]
    tm = COMBINE_TILE
    nt = t // tm
    final = g_final is not None
    in_specs = [pl.BlockSpec((TOP_K * tm,), lambda i: (i,), memory_space=pltpu.SMEM),
                pl.BlockSpec((TOP_K * tm,), lambda i: (jnp.minimum(i + 1, nt - 1),), memory_space=pltpu.SMEM),
                pl.BlockSpec((tm, LANES), lambda i: (i, 0)),
                pl.BlockSpec((tm, d), lambda i: (i, 0)),
                pl.BlockSpec(memory_space=pl.ANY)]
    pos_flat = _tile_major(pos, tm)
    args = [pos_flat, pos_flat, meta_w, x2d, ys]
    if final:
        in_specs.append(pl.BlockSpec((1, d), lambda i: (0, 0)))
        args.append(g_final.reshape(1, d))
    return pl.pallas_call(
        functools.partial(_combine_kernel, tm=tm, final=final),
        grid=(nt,),
        in_specs=in_specs,
        out_specs=pl.BlockSpec((tm, d), lambda i: (i, 0)),
        scratch_shapes=[pltpu.VMEM((2, TOP_K, tm * TOKEN_ROWS, LANES), U32), pltpu.SemaphoreType.DMA((2,))],
        out_shape=jax.ShapeDtypeStruct((t, d), F32),
        compiler_params=_cparams(("arbitrary",)),
        name="moe_combine",
    )(*args)


def _plan_kernel(cnt_ref, off_ref, blk_ref, exp_ref, nv_ref, *, tm, n_tiles):
    shift = tm.bit_length() - 1

    def per_expert(e, row0):
        c = cnt_ref[0, e]
        ntile = lax.shift_right_logical(c + (tm - 1), shift)
        off_ref[e] = row0
        t0 = lax.shift_right_logical(row0, shift)

        def fill(j, _):
            blk_ref[t0 + j] = t0 + j
            exp_ref[t0 + j] = e
            nv_ref[t0 + j] = jnp.minimum(c - j * tm, tm)
            return 0

        lax.fori_loop(0, ntile, fill, 0)
        return row0 + lax.shift_left(ntile, shift)

    total = lax.fori_loop(0, N_EXPERTS, per_expert, jnp.int32(0))
    used = lax.shift_right_logical(total, shift)
    last = jnp.maximum(used - 1, 0)

    def tail(i, _):
        blk_ref[i] = last
        exp_ref[i] = exp_ref[last]
        nv_ref[i] = 0
        return 0

    lax.fori_loop(used, n_tiles, tail, 0)


def _moe_plan(counts, n_tiles):
    tm = EXPERT_TILE
    assert tm & (tm - 1) == 0
    smem = pl.BlockSpec(memory_space=pltpu.SMEM)
    return pl.pallas_call(
        functools.partial(_plan_kernel, tm=tm, n_tiles=n_tiles),
        in_specs=[smem],
        out_specs=[smem, smem, smem, smem],
        out_shape=[jax.ShapeDtypeStruct((N_EXPERTS,), I32)] + [jax.ShapeDtypeStruct((n_tiles,), I32)] * 3,
        name="moe_plan",
    )(counts)


def _moe(x2, h2t, meta_i, meta_w, counts, w_gu, w_dn, layer):
    b, s, d = x2.shape
    assert d == 2 * TOKEN_ROWS * LANES
    n_rows = b * s * TOP_K + N_EXPERTS * EXPERT_TILE
    off, blk, exp, n_valid = _moe_plan(counts, n_rows // EXPERT_TILE)
    pos = _sorted_positions(off, meta_i)
    xs = _dispatch(pos, h2t, n_rows)
    ys = _experts(blk, exp, n_valid, xs, w_gu, w_dn, layer)
    return pos, meta_w, x2, ys


def _router_weights(w_group, b_group, w_router, b_router):
    d = w_group.shape[0]
    pad = LANES - N_EXPERTS - N_GROUPS
    w = jnp.concatenate([w_router, w_group, jnp.zeros((d, pad), F32)], axis=1)
    bias = jnp.concatenate([b_router, b_group, jnp.zeros((pad,), F32)]).reshape(1, LANES)
    w_hi = w.astype(BF16)
    w_lo = (w - w_hi.astype(F32)).astype(BF16)
    return jnp.concatenate([w_hi, w_lo], axis=1), bias


def kernel(x, mem, mem_norm, mem_w_kv, norm_mix, norm_xattn, norm_ffn, hyb_w_in, hyb_conv_w, diff_lambda, diff_subln, hyb_w_out, swa_w_qkv, swa_b_qkv, swa_sinks, swa_w_out, swa_b_out, xattn_w_q, xattn_w_o, moe_w_group, moe_b_group, moe_w_router, moe_b_router, moe_w_gate_up, moe_w_down, final_norm):
    b, s, d = x.shape
    m = mem.shape[1]
    depth = norm_mix.shape[0]
    mkv = _norm_proj(mem.reshape(b * m, d), mem_norm, mem_w_kv.astype(BF16), m).reshape(b, m, -1)

    scale = DIFF_DK ** -0.5 * LOG2E
    cq, suq, sdq, half = _rope_lane_tables(s, DIFF_DK, scale)
    ck, suk, sdk, _ = _rope_lane_tables(s, DIFF_DK, 1.0)
    tabs_q, tabs_k = (cq, suq, sdq), (ck, suk, sdk)

    moe = None
    for l in range(depth):
        if moe is not None:
            x = _combine(moe, None).reshape(b, s, d)
        if l % 2 == 0:
            e = l // 2
            lambda_init = 0.8 - 0.6 * math.exp(-0.3 * l)
            ya, q, k, v = _hyb_front(x, norm_mix[l], hyb_w_in[e].astype(BF16), hyb_conv_w[e], tabs_q, tabs_k, half)
            o = _diff_attn(q, k, v, diff_lambda[e], diff_subln[e], lambda_init)
            w_out = hyb_w_out[e].astype(BF16)
            a_list, wo_list, b_out = [ya, o], [w_out[:CONV_CH], w_out[CONV_CH:]], None
        else:
            e = l // 2
            order = jnp.asarray(_swa_head_order(), I32)
            nq = SWA_Q_HEADS * SWA_HEAD_DIM
            cols = (order[:, None] * SWA_HEAD_DIM + jnp.arange(SWA_HEAD_DIM, dtype=I32)[None, :]).reshape(-1)
            cols_all = jnp.concatenate([cols, jnp.arange(nq, swa_w_qkv.shape[2], dtype=I32)])
            w_qkv = jnp.take(swa_w_qkv[e], cols_all, axis=1).astype(BF16)
            b_qkv = jnp.take(swa_b_qkv[e], cols_all)
            q, kv = _swa_front(x, norm_mix[l], w_qkv, b_qkv, tabs_q, tabs_k, half)
            o = _swa_attn(q, kv, jnp.take(swa_sinks[e], order))
            a_list, wo_list, b_out = [o], [jnp.take(swa_w_out[e], cols, axis=0).astype(BF16)], swa_b_out[e]
        w_rt, b_rt = _router_weights(moe_w_group[l], moe_b_group[l], moe_w_router[l], moe_b_router[l])
        x2, h2, meta_i, meta_w, counts = _post_mixer(
            x, a_list, wo_list, b_out, norm_xattn[l], xattn_w_q[l].astype(BF16), mkv,
            xattn_w_o[l].astype(BF16), norm_ffn[l], w_rt, b_rt)
        moe = _moe(x2, h2, meta_i, meta_w, counts, moe_w_gate_up, moe_w_down, l)
    return _combine(moe, final_norm).reshape(b, s, d)
```

```python
import functools
import math

import jax
import jax.numpy as jnp
from jax import lax
from jax.experimental import pallas as pl
from jax.experimental.pallas import tpu as pltpu

F32 = jnp.float32
BF16 = jnp.bfloat16
I32 = jnp.int32
U32 = jnp.uint32

EPS = 1e-6
LANES = 128
SUBLANES = 8
TOKEN_ROWS = 4
DMA_UNROLL = 8
VMEM_LIMIT = 56 * 1024 * 1024

ROPE_THETA = 500000.0
ROPE_FRACTION = 4
BLOCK = 128
CONV_CH = 512
CONV_K = 3
DIFF_HEADS = 4
DIFF_DK = 64
DIFF_DV = 128
SWA_Q_HEADS = 16
SWA_KV_HEADS = 4
SWA_HEAD_DIM = 64
SWA_WINDOW = 128
XATTN_HEADS = 4
XATTN_HEAD_DIM = 128
N_GROUPS = 4
EXPERTS_PER_GROUP = 8
N_EXPERTS = N_GROUPS * EXPERTS_PER_GROUP
TOP_K = 2
EXPERT_HIDDEN = 512

ROW_TILE = 512
EXPERT_TILE = 512
DISPATCH_TILE = 2048
COMBINE_TILE = 512
POST_CHAINS = 2
EXPERT_CHAINS = 2
FRONT_CHAINS = 2
NEG_INF = float("-inf")
LOG2E = math.log2(math.e)


def _cparams(sem):
    return pltpu.CompilerParams(dimension_semantics=sem, vmem_limit_bytes=VMEM_LIMIT)


def _rms(x, g):
    return x * lax.rsqrt(jnp.mean(x * x, axis=-1, keepdims=True) + EPS) * g


def _dot(a, b):
    return jnp.dot(a, b, preferred_element_type=F32)


def _dot_nt(a, b):
    return lax.dot_general(a, b, (((1,), (1,)), ((), ())), preferred_element_type=F32)


def _rope_lane_tables(seq, head_dim, scale):
    rot = head_dim // ROPE_FRACTION
    half = rot // 2
    pos = jnp.arange(seq, dtype=F32)
    inv = ROPE_THETA ** (-jnp.arange(0, rot, 2, dtype=F32) / rot)
    ang = pos[:, None] * inv[None, :]
    cos, sin = jnp.cos(ang), jnp.sin(ang)
    idx = jnp.arange(LANES) % head_dim
    cl = jnp.take(cos, idx % half, axis=1)
    sl = jnp.take(sin, idx % half, axis=1)
    c = jnp.where(idx < rot, cl, 1.0) * scale
    s_up = jnp.where(idx < half, -sl, 0.0) * scale
    s_dn = jnp.where((idx >= half) & (idx < rot), sl, 0.0) * scale
    return c.astype(F32), s_up.astype(F32), s_dn.astype(F32), half


def _rope_chunk(xc, c, s_up, s_dn, half):
    return (xc * c + pltpu.roll(xc, LANES - half, 1) * s_up + pltpu.roll(xc, half, 1) * s_dn)


def _norm_proj_kernel(x_ref, g_ref, w_ref, o_ref):
    h = _rms(x_ref[...], g_ref[...]).astype(BF16)
    o_ref[...] = _dot(h, w_ref[...]).astype(o_ref.dtype)


def _norm_proj(x2d, g, w_bf16, tm):
    m, d = x2d.shape
    n = w_bf16.shape[1]
    return pl.pallas_call(
        _norm_proj_kernel,
        grid=(m // tm,),
        in_specs=[pl.BlockSpec((tm, d), lambda i: (i, 0)),
                  pl.BlockSpec((1, d), lambda i: (0, 0)),
                  pl.BlockSpec((d, n), lambda i: (0, 0))],
        out_specs=pl.BlockSpec((tm, n), lambda i: (i, 0)),
        out_shape=jax.ShapeDtypeStruct((m, n), BF16),
        compiler_params=_cparams(("arbitrary",)),
        name="mem_kv_proj",
    )(x2d, g.reshape(1, d), w_bf16)


def _hyb_front_kernel(x_ref, g_ref, w_ref, cw_ref, cq_ref, suq_ref, sdq_ref, ck_ref, suk_ref, sdk_ref,
                      ya_ref, q_ref, k_ref, v_ref, cbuf, *, tm, half):
    s = pl.program_id(1)
    c = CONV_CH
    base = 3 * c
    nq = DIFF_HEADS * 2 * DIFF_DK
    rh = tm // FRONT_CHAINS
    starts = [ch * rh for ch in range(FRONT_CHAINS)]
    cw = cw_ref[...]

    @pl.when(s == 0)
    def _():
        cbuf[0:8, :] = jnp.zeros((8, c), F32)

    hs = [_rms(x_ref[0, r0:r0 + rh, :], g_ref[...]).astype(BF16) for r0 in starts]
    for r0, h in zip(starts, hs):
        gate_b = _dot(h, w_ref[:, 0:c])
        cu = _dot(h, w_ref[:, c:2 * c]) * _dot(h, w_ref[:, 2 * c:3 * c])
        cbuf[8 + r0:8 + r0 + rh, :] = cu
        conv = (cw[0:1, :] * cbuf[6 + r0:6 + r0 + rh, :] + cw[1:2, :] * cbuf[7 + r0:7 + r0 + rh, :] + cw[2:3, :] * cu)
        ya_ref[0, r0:r0 + rh, :] = (gate_b * conv).astype(BF16)
    cbuf[0:8, :] = cbuf[tm:tm + 8, :]

    for r0, h in zip(starts, hs):
        cq, suq, sdq = cq_ref[r0:r0 + rh, :], suq_ref[r0:r0 + rh, :], sdq_ref[r0:r0 + rh, :]
        pq = _dot(h, w_ref[:, base:base + nq])
        for j in range(nq // LANES):
            q_ref[0, r0:r0 + rh, j * LANES:(j + 1) * LANES] = _rope_chunk(
                pq[:, j * LANES:(j + 1) * LANES], cq, suq, sdq, half).astype(BF16)
    for r0, h in zip(starts, hs):
        ck, suk, sdk = ck_ref[r0:r0 + rh, :], suk_ref[r0:r0 + rh, :], sdk_ref[r0:r0 + rh, :]
        pk = _dot(h, w_ref[:, base + nq:base + 2 * nq])
        for j in range(nq // LANES):
            k_ref[0, r0:r0 + rh, j * LANES:(j + 1) * LANES] = _rope_chunk(
                pk[:, j * LANES:(j + 1) * LANES], ck, suk, sdk, half).astype(BF16)
    for r0, h in zip(starts, hs):
        v_ref[0, r0:r0 + rh, :] = _dot(h, w_ref[:, base + 2 * nq:]).astype(BF16)


def _hyb_front(x, g, w_in, conv_w, tabs_q, tabs_k, half):
    b, s, d = x.shape
    tm = ROW_TILE
    n = w_in.shape[1]
    nq = DIFF_HEADS * 2 * DIFF_DK
    nv = DIFF_HEADS * DIFF_DV
    tab_spec = pl.BlockSpec((tm, LANES), lambda bi, si: (si, 0))
    row = lambda width: pl.BlockSpec((1, tm, width), lambda bi, si: (bi, si, 0))
    return pl.pallas_call(
        functools.partial(_hyb_front_kernel, tm=tm, half=half),
        grid=(b, s // tm),
        in_specs=[row(d),
                  pl.BlockSpec((1, d), lambda bi, si: (0, 0)),
                  pl.BlockSpec((d, n), lambda bi, si: (0, 0)),
                  pl.BlockSpec((CONV_K, CONV_CH), lambda bi, si: (0, 0)),
                  tab_spec, tab_spec, tab_spec, tab_spec, tab_spec, tab_spec],
        out_specs=[row(CONV_CH), row(nq), row(nq), row(nv)],
        out_shape=[jax.ShapeDtypeStruct((b, s, CONV_CH), BF16),
                   jax.ShapeDtypeStruct((b, s, nq), BF16),
                   jax.ShapeDtypeStruct((b, s, nq), BF16),
                   jax.ShapeDtypeStruct((b, s, nv), BF16)],
        scratch_shapes=[pltpu.VMEM((tm + 8, CONV_CH), F32)],
        compiler_params=_cparams(("arbitrary", "arbitrary")),
        name="hyb_front",
    )(x, g.reshape(1, d), w_in, conv_w, *tabs_q, *tabs_k)


def _lane_fold(x, op):
    r = x[:, 0:LANES]
    for c in range(1, x.shape[1] // LANES):
        r = op(r, x[:, c * LANES:(c + 1) * LANES])
    return r


def _diff_attn_kernel(q_ref, k_ref, v_ref, lam_ref, g_ref, o_ref, sbuf, stat, acc, *, tq, hp, lambda_init):
    i = pl.program_id(2)
    lane = lax.broadcasted_iota(I32, (1, LANES), 1)
    qs = []
    for h in range(hp):
        q = q_ref[0, :, h * LANES:(h + 1) * LANES]
        zero = jnp.zeros_like(q)
        qs += [jnp.where(lane < DIFF_DK, q, zero), jnp.where(lane >= DIFF_DK, q, zero)]
    nc = 2 * hp

    hq = tq // 2
    diag = pl.multiple_of(i * tq, tq)

    def pass1(j, carry):
        for c in range(nc):
            h = c // 2
            kb = k_ref[0, pl.ds(pl.multiple_of(j * tq, tq), tq), h * LANES:(h + 1) * LANES]
            sc = _dot_nt(qs[c], kb)
            sbuf[c, j] = sc
            stat[c] = jnp.maximum(stat[c], _lane_fold(sc, jnp.maximum))
        return carry

    stat[...] = jnp.full(stat.shape, NEG_INF, F32)
    lax.fori_loop(0, i, pass1, 0)

    r = lax.broadcasted_iota(I32, (hq, hq), 0)
    cc = lax.broadcasted_iota(I32, (hq, hq), 1)
    tri = cc <= r
    for c in range(nc):
        h = c // 2
        k_lo = k_ref[0, pl.ds(diag, hq), h * LANES:(h + 1) * LANES]
        k_hi = k_ref[0, pl.ds(diag + hq, hq), h * LANES:(h + 1) * LANES]
        s_tl = jnp.where(tri, _dot_nt(qs[c][:hq], k_lo), NEG_INF)
        s_bl = _dot_nt(qs[c][hq:], k_lo)
        s_br = jnp.where(tri, _dot_nt(qs[c][hq:], k_hi), NEG_INF)
        sbuf[c, i, 0:hq, 0:hq] = s_tl
        sbuf[c, i, hq:tq, 0:hq] = s_bl
        sbuf[c, i, hq:tq, hq:tq] = s_br
        stat[c, 0:hq, :] = jnp.maximum(stat[c, 0:hq, :], _lane_fold(s_tl, jnp.maximum))
        stat[c, hq:tq, :] = jnp.maximum(stat[c, hq:tq, :],
                                        jnp.maximum(_lane_fold(s_bl, jnp.maximum), _lane_fold(s_br, jnp.maximum)))
    ms = [jnp.max(stat[c], axis=-1, keepdims=True) for c in range(nc)]

    ones = jnp.ones((tq, LANES), BF16)
    for h in range(hp):
        vb = v_ref[0, pl.ds(diag, tq), h * LANES:(h + 1) * LANES]
        v_ext = jnp.concatenate([vb, ones], axis=1)
        tops, bots = [], []
        for c in (2 * h, 2 * h + 1):
            tops.append(jnp.exp2(sbuf[c, i, 0:hq, 0:hq] - ms[c][:hq]))
            bots.append(jnp.exp2(sbuf[c, i, hq:tq, :] - ms[c][hq:]))
        top = _dot(jnp.concatenate(tops, axis=0).astype(BF16), v_ext[:hq])
        bot = _dot(jnp.concatenate(bots, axis=0).astype(BF16), v_ext)
        acc[h, 0:hq, :] = top[:hq]
        acc[h, hq:tq, :] = bot[:hq]
        acc[h, tq:tq + hq, :] = top[hq:]
        acc[h, tq + hq:2 * tq, :] = bot[hq:]

    def pass2(j, carry):
        for h in range(hp):
            vb = v_ref[0, pl.ds(pl.multiple_of(j * tq, tq), tq), h * LANES:(h + 1) * LANES]
            v_ext = jnp.concatenate([vb, ones], axis=1)
            p0 = jnp.exp2(sbuf[2 * h, j] - ms[2 * h])
            p1 = jnp.exp2(sbuf[2 * h + 1, j] - ms[2 * h + 1])
            acc[h] += _dot(jnp.concatenate([p0, p1], axis=0).astype(BF16), v_ext)
        return carry

    lax.fori_loop(0, i, pass2, 0)

    lf = lam_ref[...]
    lam = (jnp.exp(jnp.sum(lf[0:1] * lf[1:2], keepdims=True))
           - jnp.exp(jnp.sum(lf[2:3] * lf[3:4], keepdims=True)) + lambda_init)
    for h in range(hp):
        a0 = acc[h, 0:tq, :]
        a1 = acc[h, tq:2 * tq, :]
        o = a0[:, :LANES] / a0[:, LANES:] - lam * (a1[:, :LANES] / a1[:, LANES:])
        o_ref[0, :, h * LANES:(h + 1) * LANES] = (_rms(o, g_ref[...]) * (1.0 - lambda_init)).astype(BF16)


def _diff_attn(q, k, v, lam_vecs, subln_g, lambda_init):
    b, s, _ = q.shape
    tq = 512
    hp = 4
    blk = lambda bi, hi, qi: (bi, qi, hi)
    full = lambda bi, hi, qi: (bi, 0, hi)
    return pl.pallas_call(
        functools.partial(_diff_attn_kernel, tq=tq, hp=hp, lambda_init=lambda_init),
        grid=(b, DIFF_HEADS // hp, s // tq),
        in_specs=[pl.BlockSpec((1, tq, hp * LANES), blk),
                  pl.BlockSpec((1, s, hp * LANES), full),
                  pl.BlockSpec((1, s, hp * LANES), full),
                  pl.BlockSpec((4, DIFF_DK), lambda bi, hi, qi: (0, 0)),
                  pl.BlockSpec((1, DIFF_DV), lambda bi, hi, qi: (0, 0))],
        out_specs=pl.BlockSpec((1, tq, hp * LANES), blk),
        out_shape=jax.ShapeDtypeStruct((b, s, DIFF_HEADS * DIFF_DV), BF16),
        scratch_shapes=[pltpu.VMEM((2 * hp, s // tq, tq, tq), F32),
                        pltpu.VMEM((2 * hp, tq, LANES), F32),
                        pltpu.VMEM((hp, 2 * tq, DIFF_DV + LANES), F32)],
        compiler_params=_cparams(("arbitrary", "arbitrary", "arbitrary")),
        name="diff_attn",
    )(q, k, v, lam_vecs, subln_g.reshape(1, DIFF_DV))


def _swa_front_kernel(x_ref, g_ref, w_ref, b_ref, cq_ref, suq_ref, sdq_ref, ck_ref, suk_ref, sdk_ref,
                      q_ref, kv_ref, *, half):
    nq = SWA_Q_HEADS * SWA_HEAD_DIM
    nkv = SWA_KV_HEADS * SWA_HEAD_DIM
    tm = x_ref.shape[1]
    rh = tm // FRONT_CHAINS
    starts = [ch * rh for ch in range(FRONT_CHAINS)]
    hs = [_rms(x_ref[0, r0:r0 + rh, :], g_ref[...]).astype(BF16) for r0 in starts]
    for r0, h in zip(starts, hs):
        cq, suq, sdq = cq_ref[r0:r0 + rh, :], suq_ref[r0:r0 + rh, :], sdq_ref[r0:r0 + rh, :]
        pq = _dot(h, w_ref[:, 0:nq]) + b_ref[:, 0:nq]
        for j in range(nq // LANES):
            lo = j * LANES
            q_ref[0, r0:r0 + rh, lo:lo + LANES] = _rope_chunk(pq[:, lo:lo + LANES], cq, suq, sdq, half).astype(BF16)
    for r0, h in zip(starts, hs):
        ck, suk, sdk = ck_ref[r0:r0 + rh, :], suk_ref[r0:r0 + rh, :], sdk_ref[r0:r0 + rh, :]
        pkv = _dot(h, w_ref[:, nq:]) + b_ref[:, nq:]
        for j in range(nkv // LANES):
            lo = j * LANES
            kv_ref[0, r0:r0 + rh, lo:lo + LANES] = _rope_chunk(pkv[:, lo:lo + LANES], ck, suk, sdk, half).astype(BF16)
        kv_ref[0, r0:r0 + rh, nkv:] = pkv[:, nkv:].astype(BF16)


def _swa_front(x, g, w_qkv, b_qkv, tabs_q, tabs_k, half):
    b, s, d = x.shape
    tm = ROW_TILE
    n = w_qkv.shape[1]
    nq = SWA_Q_HEADS * SWA_HEAD_DIM
    nkv = SWA_KV_HEADS * SWA_HEAD_DIM
    tab_spec = pl.BlockSpec((tm, LANES), lambda bi, si: (si, 0))
    row = lambda width: pl.BlockSpec((1, tm, width), lambda bi, si: (bi, si, 0))
    return pl.pallas_call(
        functools.partial(_swa_front_kernel, half=half),
        grid=(b, s // tm),
        in_specs=[row(d),
                  pl.BlockSpec((1, d), lambda bi, si: (0, 0)),
                  pl.BlockSpec((d, n), lambda bi, si: (0, 0)),
                  pl.BlockSpec((1, n), lambda bi, si: (0, 0)),
                  tab_spec, tab_spec, tab_spec, tab_spec, tab_spec, tab_spec],
        out_specs=[row(nq), row(2 * nkv)],
        out_shape=[jax.ShapeDtypeStruct((b, s, nq), BF16),
                   jax.ShapeDtypeStruct((b, s, 2 * nkv), BF16)],
        compiler_params=_cparams(("arbitrary", "arbitrary")),
        name="swa_front",
    )(x, g.reshape(1, d), w_qkv, b_qkv.reshape(1, n), *tabs_q, *tabs_k)


def _swa_head_order():
    g_sz = SWA_Q_HEADS // SWA_KV_HEADS
    order = []
    for slab in range(SWA_Q_HEADS // 2):
        pair, j = slab // g_sz, slab % g_sz
        order += [(2 * pair) * g_sz + j, (2 * pair + 1) * g_sz + j]
    return order


def _swa_attn_kernel(sink_ref, q_ref, kvp_ref, kvc_ref, o_ref, *, nblk):
    i = pl.program_id(1)
    hd = SWA_HEAD_DIM
    nkv = SWA_KV_HEADS * hd
    g_sz = SWA_Q_HEADS // SWA_KV_HEADS
    kv = jnp.concatenate([kvp_ref[0], kvc_ref[0]], axis=0)
    r = lax.broadcasted_iota(I32, (BLOCK, 2 * BLOCK), 0)
    c = lax.broadcasted_iota(I32, (BLOCK, 2 * BLOCK), 1)
    rel = c - BLOCK - r
    in_win = (rel <= 0) & (rel > -SWA_WINDOW)
    lane = lax.broadcasted_iota(I32, (1, LANES), 1)
    lo_half = lane < hd
    ones = jnp.ones((2 * BLOCK, LANES), BF16)
    for n in range(nblk):
        mask = in_win & ((c >= BLOCK) | (i > 0)) if n == 0 else in_win
        keys = kv[n * BLOCK:(n + 2) * BLOCK]
        for pair in range(SWA_KV_HEADS // 2):
            k2 = keys[:, pair * LANES:(pair + 1) * LANES]
            v2 = keys[:, nkv + pair * LANES:nkv + (pair + 1) * LANES]
            v_ext = jnp.concatenate([v2, ones], axis=1)
            pieces = []
            for j in range(g_sz):
                slab = pair * g_sz + j
                qs = q_ref[0, n * BLOCK:(n + 1) * BLOCK, slab * LANES:(slab + 1) * LANES]
                zero = jnp.zeros_like(qs)
                pieces += [jnp.where(lo_half, qs, zero), jnp.where(lo_half, zero, qs)]
            sc = _dot_nt(jnp.concatenate(pieces, axis=0), k2)
            probs, tails = [], []
            for pc in range(2 * g_sz):
                scp = jnp.where(mask, sc[pc * BLOCK:(pc + 1) * BLOCK], NEG_INF)
                sink = sink_ref[2 * g_sz * pair + pc] * LOG2E
                m = jnp.maximum(jnp.max(scp, axis=-1, keepdims=True), sink)
                probs.append(jnp.exp2(scp - m).astype(BF16))
                tails.append(jnp.exp2(sink - m))
            pv = _dot(jnp.concatenate(probs, axis=0), v_ext)
            for j in range(g_sz):
                slab = pair * g_sz + j
                halves = []
                for hf in range(2):
                    pc = 2 * j + hf
                    blk = pv[pc * BLOCK:(pc + 1) * BLOCK]
                    halves.append(blk[:, :LANES] / (blk[:, LANES:] + tails[pc]))
                o_ref[0, n * BLOCK:(n + 1) * BLOCK, slab * LANES:(slab + 1) * LANES] = (
                    jnp.where(lo_half, halves[0], halves[1]).astype(BF16))


def _swa_attn(q, kv, sinks):
    b, s, nq = q.shape
    nblk = 2
    tq = nblk * BLOCK
    return pl.pallas_call(
        functools.partial(_swa_attn_kernel, nblk=nblk),
        grid_spec=pltpu.PrefetchScalarGridSpec(
            num_scalar_prefetch=1,
            grid=(b, s // tq),
            in_specs=[pl.BlockSpec((1, tq, nq), lambda bi, ni, sk: (bi, ni, 0)),
                      pl.BlockSpec((1, BLOCK, kv.shape[2]), lambda bi, ni, sk: (bi, jnp.maximum(ni * nblk - 1, 0), 0)),
                      pl.BlockSpec((1, tq, kv.shape[2]), lambda bi, ni, sk: (bi, ni, 0))],
            out_specs=pl.BlockSpec((1, tq, nq), lambda bi, ni, sk: (bi, ni, 0))),
        out_shape=jax.ShapeDtypeStruct((b, s, nq), BF16),
        compiler_params=_cparams(("arbitrary", "arbitrary")),
        name="swa_attn",
    )(sinks, q, kv, kv)


def _post_kernel(*refs, tm, n_a, has_bias):
    x_ref = refs[0]
    a_refs = refs[1:1 + n_a]
    k = 1 + n_a
    wo_refs = refs[k:k + n_a]
    k += n_a
    if has_bias:
        bo_ref = refs[k]
        k += 1
    (gx_ref, wq_ref, mkv_ref, wxo_ref, gf_ref, wr_ref, br_ref,
     x2_ref, h2_ref, mi_ref, mw_ref, cnt_ref, cnt_acc) = refs[k:]
    first = (pl.program_id(0) == 0) & (pl.program_id(1) == 0)

    @pl.when(first)
    def _():
        cnt_acc[...] = jnp.zeros_like(cnt_acc)

    xw = XATTN_HEADS * XATTN_HEAD_DIM
    ones = jnp.ones((mkv_ref.shape[1], LANES), BF16)

    rh = tm // POST_CHAINS

    def out_proj(r0):
        acc = _dot(a_refs[0][0, r0:r0 + rh, :], wo_refs[0][...])
        for a_ref, w_ref in zip(a_refs[1:], wo_refs[1:]):
            acc = acc + _dot(a_ref[0, r0:r0 + rh, :], w_ref[...])
        if has_bias:
            acc = acc + bo_ref[...]
        return x_ref[0, r0:r0 + rh, :] + acc

    def q_proj(x1):
        hx = _rms(x1, gx_ref[...]).astype(BF16)
        return (_dot(hx, wq_ref[...]) * (XATTN_HEAD_DIM ** -0.5 * LOG2E)).astype(BF16)

    def mem_attn(qx):
        outs = []
        for hh in range(XATTN_HEADS):
            lo = hh * XATTN_HEAD_DIM
            mk = mkv_ref[0, :, lo:lo + XATTN_HEAD_DIM]
            mv = jnp.concatenate([mkv_ref[0, :, xw + lo:xw + lo + XATTN_HEAD_DIM], ones], axis=1)
            sc = _dot_nt(qx[:, lo:lo + XATTN_HEAD_DIM], mk)
            m = jnp.max(sc, axis=-1, keepdims=True)
            pv = _dot(jnp.exp2(sc - m).astype(BF16), mv)
            outs.append((pv[:, :LANES] / pv[:, LANES:]).astype(BF16))
        return jnp.concatenate(outs, axis=-1)

    def o_proj(r0, x1, ox):
        x2 = x1 + _dot(ox, wxo_ref[...])
        x2_ref[0, r0:r0 + rh, :] = x2
        return x2

    def router(r0, x2):
        h2 = _rms(x2, gf_ref[...])
        _to_token_tiles(h2_ref, h2, rh, r0)
        h_hi = h2.astype(BF16)
        h_lo = (h2 - h_hi.astype(F32)).astype(BF16)
        prod = _dot(jnp.concatenate([h_hi, h_lo], axis=0), wr_ref[...])
        return prod[:rh, :LANES] + prod[:rh, LANES:] + prod[rh:, :LANES] + br_ref[...]

    starts = [c * rh for c in range(POST_CHAINS)]
    x1s = [out_proj(r0) for r0 in starts]
    qxs = [q_proj(x1) for x1 in x1s]
    oxs = [mem_attn(qx) for qx in qxs]
    x2s = [o_proj(r0, x1, ox) for r0, x1, ox in zip(starts, x1s, oxs)]
    logits = jnp.concatenate([router(r0, x2) for r0, x2 in zip(starts, x2s)], axis=0)
    lane = lax.broadcasted_iota(I32, (tm, LANES), 1).astype(F32)
    big = float(LANES)
    g_lo = float(N_EXPERTS)
    lg = jnp.where((lane >= g_lo) & (lane < g_lo + N_GROUPS), logits, NEG_INF)
    mg = jnp.max(lg, axis=-1, keepdims=True)
    g_lane = jnp.min(jnp.where(lg == mg, lane, big), axis=-1, keepdims=True)
    p_g = 1.0 / jnp.sum(jnp.exp(lg - mg), axis=-1, keepdims=True)
    e_lo = (g_lane - g_lo) * EXPERTS_PER_GROUP
    le = jnp.where((lane >= e_lo) & (lane < e_lo + EXPERTS_PER_GROUP), logits, NEG_INF)
    m1 = jnp.max(le, axis=-1, keepdims=True)
    i1 = jnp.min(jnp.where(le == m1, lane, big), axis=-1, keepdims=True)
    le2 = jnp.where(lane == i1, NEG_INF, le)
    m2 = jnp.max(le2, axis=-1, keepdims=True)
    i2 = jnp.min(jnp.where(le2 == m2, lane, big), axis=-1, keepdims=True)
    t = jnp.exp(m2 - m1)
    w1 = p_g / (1.0 + t)
    w2 = p_g * t / (1.0 + t)

    oh1 = lane == i1
    oh2 = lane == i2
    oh = jnp.where(oh1 | oh2, 1.0, 0.0)
    rr = lax.broadcasted_iota(I32, (tm, tm), 0)
    cc = lax.broadcasted_iota(I32, (tm, tm), 1)
    tri = jnp.where(cc < rr, 1.0, 0.0).astype(BF16)
    before = _dot(tri, oh.astype(BF16)) + cnt_acc[...]
    r1 = jnp.sum(jnp.where(oh1, before, 0.0), axis=-1, keepdims=True)
    r2 = jnp.sum(jnp.where(oh2, before, 0.0), axis=-1, keepdims=True)
    cnt_new = cnt_acc[...] + jnp.sum(oh, axis=0, keepdims=True)
    cnt_acc[...] = cnt_new
    cnt_ref[...] = cnt_new.astype(I32)

    mw_ref[...] = jnp.where(lane == 0, w1, jnp.where(lane == 1, w2, 0.0))
    ints = jnp.where(lane == 0, i1,
                     jnp.where(lane == 1, i2,
                               jnp.where(lane == 2, r1, jnp.where(lane == 3, r2, 0.0))))
    mi_ref[...] = ints.T[0:8, :].astype(I32)


def _post_mixer(x, a_list, wo_list, b_out, g_x, w_q, mkv, w_xo, g_f, w_rt, b_rt):
    b, s, d = x.shape
    tm = ROW_TILE
    ns = s // tm
    t = b * s
    n_a = len(a_list)
    has_bias = b_out is not None
    const2 = lambda bi, si: (0, 0)
    row = lambda width: pl.BlockSpec((1, tm, width), lambda bi, si: (bi, si, 0))
    in_specs = [row(d)] + [row(a.shape[2]) for a in a_list]
    in_specs += [pl.BlockSpec(w.shape, const2) for w in wo_list]
    args = [x, *a_list, *wo_list]
    if has_bias:
        in_specs.append(pl.BlockSpec((1, d), const2))
        args.append(b_out.reshape(1, d))
    xw = w_q.shape[1]
    in_specs += [pl.BlockSpec((1, d), const2),
                 pl.BlockSpec((d, xw), const2),
                 pl.BlockSpec((1, mkv.shape[1], mkv.shape[2]), lambda bi, si: (bi, 0, 0)),
                 pl.BlockSpec((xw, d), const2),
                 pl.BlockSpec((1, d), const2),
                 pl.BlockSpec((d, 2 * LANES), const2),
                 pl.BlockSpec((1, LANES), const2)]
    args += [g_x.reshape(1, d), w_q, mkv, w_xo, g_f.reshape(1, d), w_rt, b_rt]
    out_specs = [row(d),
                 pl.BlockSpec((tm * TOKEN_ROWS, LANES), lambda bi, si: (bi * ns + si, 0)),
                 pl.BlockSpec((8, tm), lambda bi, si: (0, bi * ns + si)),
                 pl.BlockSpec((tm, LANES), lambda bi, si: (bi * ns + si, 0)),
                 pl.BlockSpec((1, LANES), const2)]
    out_shape = [jax.ShapeDtypeStruct((b, s, d), F32),
                 jax.ShapeDtypeStruct((t * TOKEN_ROWS, LANES), U32),
                 jax.ShapeDtypeStruct((8, t), I32),
                 jax.ShapeDtypeStruct((t, LANES), F32),
                 jax.ShapeDtypeStruct((1, LANES), I32)]
    return pl.pallas_call(
        functools.partial(_post_kernel, tm=tm, n_a=n_a, has_bias=has_bias),
        grid=(b, ns),
        in_specs=in_specs,
        out_specs=out_specs,
        out_shape=out_shape,
        scratch_shapes=[pltpu.VMEM((1, LANES), F32)],
        compiler_params=_cparams(("arbitrary", "arbitrary")),
        name="post_mixer",
    )(*args)


def _to_token_tiles(dst_ref, val, rows, tok0=0):
    half = TOKEN_ROWS * LANES
    for j in range(TOKEN_ROWS):
        hi = val[:, j * LANES:(j + 1) * LANES].astype(BF16).astype(F32)
        lo = val[:, half + j * LANES:half + (j + 1) * LANES].astype(BF16).astype(F32)
        word = lax.bitcast_convert_type(hi, U32) | (lax.bitcast_convert_type(lo, U32) >> 16)
        dst_ref[pl.ds(tok0 * TOKEN_ROWS + j, rows, stride=TOKEN_ROWS), :] = word


def _from_token_tiles(src_ref, rows, tok0=0):
    his, los = [], []
    for j in range(TOKEN_ROWS):
        word = src_ref[pl.ds(tok0 * TOKEN_ROWS + j, rows, stride=TOKEN_ROWS), :]
        his.append(lax.bitcast_convert_type(word & jnp.uint32(0xFFFF0000), F32))
        los.append(lax.bitcast_convert_type(word << 16, F32))
    return jnp.concatenate(his + los, axis=1)


def _token_slice(ref, tok):
    return ref.at[pl.ds(pl.multiple_of(tok * TOKEN_ROWS, TOKEN_ROWS), TOKEN_ROWS)]


def _pos_kernel(off_ref, mi_ref, pos_ref):
    e = mi_ref[0:TOP_K, :]
    pos = mi_ref[TOP_K:2 * TOP_K, :]
    for j in range(N_EXPERTS):
        pos = pos + jnp.where(e == j, off_ref[j], 0)
    pos_ref[...] = pos


def _sorted_positions(off, meta_i):
    t = meta_i.shape[1]
    return pl.pallas_call(
        _pos_kernel,
        grid_spec=pltpu.PrefetchScalarGridSpec(
            num_scalar_prefetch=1,
            grid=(1,),
            in_specs=[pl.BlockSpec((8, t), lambda i, off: (0, 0))],
            out_specs=pl.BlockSpec((TOP_K, t), lambda i, off: (0, 0))),
        out_shape=jax.ShapeDtypeStruct((TOP_K, t), I32),
        compiler_params=_cparams(("arbitrary",)),
        name="moe_positions",
    )(off, meta_i)


def _tile_major(pos, tm):
    t = pos.shape[1]
    return pos.reshape(TOP_K, t // tm, tm).transpose(1, 0, 2).reshape(-1)


def _dispatch_kernel(pos_ref, h_ref, xs_ref, sem, *, tm):
    def issue(g, _):
        for u in range(DMA_UNROLL):
            r = g * DMA_UNROLL + u
            for kk in range(TOP_K):
                pos = pos_ref[kk * tm + r]
                pltpu.make_async_copy(_token_slice(h_ref, r), _token_slice(xs_ref, pos), sem).start(priority=kk)
        return 0

    lax.fori_loop(0, tm // DMA_UNROLL, issue, 0)
    for _ in range(TOP_K):
        pltpu.make_async_copy(h_ref, xs_ref.at[pl.ds(0, tm * TOKEN_ROWS)], sem).wait()


def _dispatch(pos, h2t, n_rows):
    tm = DISPATCH_TILE
    t = h2t.shape[0] // TOKEN_ROWS
    return pl.pallas_call(
        functools.partial(_dispatch_kernel, tm=tm),
        grid=(t // tm,),
        in_specs=[pl.BlockSpec((TOP_K * tm,), lambda i: (i,), memory_space=pltpu.SMEM),
                  pl.BlockSpec((tm * TOKEN_ROWS, LANES), lambda i: (i, 0))],
        out_specs=pl.BlockSpec(memory_space=pl.ANY),
        scratch_shapes=[pltpu.SemaphoreType.DMA],
        out_shape=jax.ShapeDtypeStruct((n_rows * TOKEN_ROWS, LANES), U32),
        compiler_params=_cparams(("arbitrary",)),
        name="moe_dispatch",
    )(_tile_major(pos, tm), h2t)


def _expert_kernel(tblk_ref, texp_ref, tn_ref, tnxt_ref, tpar_ref, xs_ref, wgu_hbm, wdn_hbm, ys_ref,
                   wgu_bf, wdn_bf, wgu_f, wdn_f, sems, *, tm, layer):
    i = pl.program_id(0)
    n_valid = tn_ref[i]
    expert = texp_ref[i]
    slot = tpar_ref[i]
    new_expert = (i == 0) | (expert != texp_ref[jnp.maximum(i - 1, 0)])

    def weight_copies(ex, sl):
        return (pltpu.make_async_copy(wgu_hbm.at[layer, ex], wgu_f.at[sl], sems.at[sl]),
                pltpu.make_async_copy(wdn_hbm.at[layer, ex], wdn_f.at[sl], sems.at[sl]))

    @pl.when(i == 0)
    def _():
        for cp in weight_copies(expert, slot):
            cp.start()

    @pl.when(new_expert)
    def _():
        @pl.when(tnxt_ref[i] >= 0)
        def _():
            for cp in weight_copies(tnxt_ref[i], 1 - slot):
                cp.start()

        for cp in weight_copies(expert, slot):
            cp.wait()
        wgu_bf[...] = wgu_f[slot].astype(BF16)
        wdn_bf[...] = wdn_f[slot].astype(BF16)

    @pl.when(n_valid > 0)
    def _():
        rh = tm // EXPERT_CHAINS
        starts = [c * rh for c in range(EXPERT_CHAINS)]
        row = lax.broadcasted_iota(I32, (rh, 1), 0)
        gus = []
        for r0 in starts:
            x = jnp.where(row + r0 < n_valid, _from_token_tiles(xs_ref, rh, r0), 0.0).astype(BF16)
            gus.append(_dot(x, wgu_bf[...]))
        ys = []
        for gu in gus:
            hid = (gu[:, :EXPERT_HIDDEN] / (1.0 + jnp.exp(-gu[:, :EXPERT_HIDDEN])) * gu[:, EXPERT_HIDDEN:]).astype(BF16)
            ys.append(_dot(hid, wdn_bf[...]))
        for r0, y in zip(starts, ys):
            _to_token_tiles(ys_ref, y, rh, r0)


def _experts(plan, xs, w_gu, w_dn, layer):
    tm = EXPERT_TILE
    nt = xs.shape[0] // (tm * TOKEN_ROWS)
    gu_shape, dn_shape = w_gu.shape[2:], w_dn.shape[2:]
    rows = lambda i, tb, te, tn, tx, tp: (tb[i], 0)
    return pl.pallas_call(
        functools.partial(_expert_kernel, tm=tm, layer=layer),
        grid_spec=pltpu.PrefetchScalarGridSpec(
            num_scalar_prefetch=5,
            grid=(nt,),
            in_specs=[pl.BlockSpec((tm * TOKEN_ROWS, LANES), rows),
                      pl.BlockSpec(memory_space=pl.ANY),
                      pl.BlockSpec(memory_space=pl.ANY)],
            out_specs=pl.BlockSpec((tm * TOKEN_ROWS, LANES), rows),
            scratch_shapes=[pltpu.VMEM(gu_shape, BF16), pltpu.VMEM(dn_shape, BF16),
                            pltpu.VMEM((2,) + gu_shape, F32), pltpu.VMEM((2,) + dn_shape, F32),
                            pltpu.SemaphoreType.DMA((2,))]),
        out_shape=jax.ShapeDtypeStruct(xs.shape, U32),
        compiler_params=_cparams(("arbitrary",)),
        name="moe_experts",
    )(*plan, xs, w_gu, w_dn)


def _moe_combine_tile(pos_cur, pos_nxt, mw_ref, x, ys_ref, ybuf, sems, i, n, tm):
    def gather(pos_ref, slot):
        def issue(g, _):
            for u in range(DMA_UNROLL):
                r = g * DMA_UNROLL + u
                for kk in range(TOP_K):
                    pos = pos_ref[kk * tm + r]
                    pltpu.make_async_copy(_token_slice(ys_ref, pos), _token_slice(ybuf.at[slot, kk], r),
                                          sems.at[slot]).start(priority=kk)
            return 0

        lax.fori_loop(0, tm // DMA_UNROLL, issue, 0)

    @pl.when(i == 0)
    def _():
        gather(pos_cur, 0)

    @pl.when(i + 1 < n)
    def _():
        gather(pos_nxt, (i + 1) % 2)

    slot = i % 2
    for kk in range(TOP_K):
        pltpu.make_async_copy(ys_ref.at[pl.ds(0, tm * TOKEN_ROWS)], ybuf.at[slot, kk], sems.at[slot]).wait()
    mw = mw_ref[...]
    return (x + mw[:, 0:1] * _from_token_tiles(ybuf.at[slot, 0], tm)
            + mw[:, 1:2] * _from_token_tiles(ybuf.at[slot, 1], tm))


def _combine_kernel(pos_cur, pos_nxt, mw_ref, x_ref, ys_ref, *rest, tm, final):
    if final:
        g_ref, o_ref, ybuf, sems = rest
    else:
        o_ref, ybuf, sems = rest
    x3 = _moe_combine_tile(pos_cur, pos_nxt, mw_ref, x_ref[...], ys_ref, ybuf, sems,
                           pl.program_id(0), pl.num_programs(0), tm)
    if final:
        x3 = _rms(x3, g_ref[...])
    o_ref[...] = x3


def _combine(moe, g_final):
    pos, meta_w, x2, ys = moe
    d = x2.shape[-1]
    x2d = x2.reshape(-1, d)
    t = x2d.shape[0]
    tm = COMBINE_TILE
    nt = t // tm
    final = g_final is not None
    in_specs = [pl.BlockSpec((TOP_K * tm,), lambda i: (i,), memory_space=pltpu.SMEM),
                pl.BlockSpec((TOP_K * tm,), lambda i: (jnp.minimum(i + 1, nt - 1),), memory_space=pltpu.SMEM),
                pl.BlockSpec((tm, LANES), lambda i: (i, 0)),
                pl.BlockSpec((tm, d), lambda i: (i, 0)),
                pl.BlockSpec(memory_space=pl.ANY)]
    pos_flat = _tile_major(pos, tm)
    args = [pos_flat, pos_flat, meta_w, x2d, ys]
    if final:
        in_specs.append(pl.BlockSpec((1, d), lambda i: (0, 0)))
        args.append(g_final.reshape(1, d))
    return pl.pallas_call(
        functools.partial(_combine_kernel, tm=tm, final=final),
        grid=(nt,),
        in_specs=in_specs,
        out_specs=pl.BlockSpec((tm, d), lambda i: (i, 0)),
        scratch_shapes=[pltpu.VMEM((2, TOP_K, tm * TOKEN_ROWS, LANES), U32), pltpu.SemaphoreType.DMA((2,))],
        out_shape=jax.ShapeDtypeStruct((t, d), F32),
        compiler_params=_cparams(("arbitrary",)),
        name="moe_combine",
    )(*args)


def _plan_kernel(cnt_ref, off_ref, blk_ref, exp_ref, nv_ref, nxt_ref, par_ref, *, tm, n_tiles):
    shift = tm.bit_length() - 1

    def clear(i, _):
        nxt_ref[i] = -1
        return 0

    lax.fori_loop(0, n_tiles, clear, 0)

    def per_expert(e, carry):
        row0, run, prev_first = carry
        c = cnt_ref[0, e]
        ntile = lax.shift_right_logical(c + (tm - 1), shift)
        off_ref[e] = row0
        t0 = lax.shift_right_logical(row0, shift)

        def fill(j, _):
            blk_ref[t0 + j] = t0 + j
            exp_ref[t0 + j] = e
            nv_ref[t0 + j] = jnp.minimum(c - j * tm, tm)
            par_ref[t0 + j] = run & 1
            return 0

        lax.fori_loop(0, ntile, fill, 0)
        has = ntile > 0

        @pl.when(has & (prev_first >= 0))
        def _():
            nxt_ref[prev_first] = e

        return (row0 + lax.shift_left(ntile, shift), run + has.astype(I32), jnp.where(has, t0, prev_first))

    total, _, _ = lax.fori_loop(0, N_EXPERTS, per_expert, (jnp.int32(0), jnp.int32(0), jnp.int32(-1)))
    used = lax.shift_right_logical(total, shift)
    last = jnp.maximum(used - 1, 0)

    def tail(i, _):
        blk_ref[i] = last
        exp_ref[i] = exp_ref[last]
        nv_ref[i] = 0
        par_ref[i] = par_ref[last]
        return 0

    lax.fori_loop(used, n_tiles, tail, 0)


def _moe_plan(counts, n_tiles):
    tm = EXPERT_TILE
    assert tm & (tm - 1) == 0
    smem = pl.BlockSpec(memory_space=pltpu.SMEM)
    return pl.pallas_call(
        functools.partial(_plan_kernel, tm=tm, n_tiles=n_tiles),
        in_specs=[smem],
        out_specs=[smem] * 6,
        out_shape=[jax.ShapeDtypeStruct((N_EXPERTS,), I32)] + [jax.ShapeDtypeStruct((n_tiles,), I32)] * 5,
        name="moe_plan",
    )(counts)


def _moe(x2, h2t, meta_i, meta_w, counts, w_gu, w_dn, layer):
    b, s, d = x2.shape
    assert d == 2 * TOKEN_ROWS * LANES
    n_rows = b * s * TOP_K + N_EXPERTS * EXPERT_TILE
    off, *tile_plan = _moe_plan(counts, n_rows // EXPERT_TILE)
    pos = _sorted_positions(off, meta_i)
    xs = _dispatch(pos, h2t, n_rows)
    ys = _experts(tile_plan, xs, w_gu, w_dn, layer)
    return pos, meta_w, x2, ys


def _router_weights(w_group, b_group, w_router, b_router):
    d = w_group.shape[0]
    pad = LANES - N_EXPERTS - N_GROUPS
    w = jnp.concatenate([w_router, w_group, jnp.zeros((d, pad), F32)], axis=1)
    bias = jnp.concatenate([b_router, b_group, jnp.zeros((pad,), F32)]).reshape(1, LANES)
    w_hi = w.astype(BF16)
    w_lo = (w - w_hi.astype(F32)).astype(BF16)
    return jnp.concatenate([w_hi, w_lo], axis=1), bias


def kernel(x, mem, mem_norm, mem_w_kv, norm_mix, norm_xattn, norm_ffn, hyb_w_in, hyb_conv_w, diff_lambda, diff_subln, hyb_w_out, swa_w_qkv, swa_b_qkv, swa_sinks, swa_w_out, swa_b_out, xattn_w_q, xattn_w_o, moe_w_group, moe_b_group, moe_w_router, moe_b_router, moe_w_gate_up, moe_w_down, final_norm):
    b, s, d = x.shape
    m = mem.shape[1]
    depth = norm_mix.shape[0]
    mkv = _norm_proj(mem.reshape(b * m, d), mem_norm, mem_w_kv.astype(BF16), m).reshape(b, m, -1)

    scale = DIFF_DK ** -0.5 * LOG2E
    cq, suq, sdq, half = _rope_lane_tables(s, DIFF_DK, scale)
    ck, suk, sdk, _ = _rope_lane_tables(s, DIFF_DK, 1.0)
    tabs_q, tabs_k = (cq, suq, sdq), (ck, suk, sdk)

    moe = None
    for l in range(depth):
        if moe is not None:
            x = _combine(moe, None).reshape(b, s, d)
        if l % 2 == 0:
            e = l // 2
            lambda_init = 0.8 - 0.6 * math.exp(-0.3 * l)
            ya, q, k, v = _hyb_front(x, norm_mix[l], hyb_w_in[e].astype(BF16), hyb_conv_w[e], tabs_q, tabs_k, half)
            o = _diff_attn(q, k, v, diff_lambda[e], diff_subln[e], lambda_init)
            w_out = hyb_w_out[e].astype(BF16)
            a_list, wo_list, b_out = [ya, o], [w_out[:CONV_CH], w_out[CONV_CH:]], None
        else:
            e = l // 2
            order = jnp.asarray(_swa_head_order(), I32)
            nq = SWA_Q_HEADS * SWA_HEAD_DIM
            cols = (order[:, None] * SWA_HEAD_DIM + jnp.arange(SWA_HEAD_DIM, dtype=I32)[None, :]).reshape(-1)
            cols_all = jnp.concatenate([cols, jnp.arange(nq, swa_w_qkv.shape[2], dtype=I32)])
            w_qkv = jnp.take(swa_w_qkv[e], cols_all, axis=1).astype(BF16)
            b_qkv = jnp.take(swa_b_qkv[e], cols_all)
            q, kv = _swa_front(x, norm_mix[l], w_qkv, b_qkv, tabs_q, tabs_k, half)
            o = _swa_attn(q, kv, jnp.take(swa_sinks[e], order))
            a_list, wo_list, b_out = [o], [jnp.take(swa_w_out[e], cols, axis=0).astype(BF16)], swa_b_out[e]
        w_rt, b_rt = _router_weights(moe_w_group[l], moe_b_group[l], moe_w_router[l], moe_b_router[l])
        x2, h2, meta_i, meta_w, counts = _post_mixer(
            x, a_list, wo_list, b_out, norm_xattn[l], xattn_w_q[l].astype(BF16), mkv,
            xattn_w_o[l].astype(BF16), norm_ffn[l], w_rt, b_rt)
        moe = _moe(x2, h2, meta_i, meta_w, counts, moe_w_gate_up, moe_w_down, l)
    return _combine(moe, final_norm).reshape(b, s, d)
```

```python
import functools
import math

import jax
import jax.numpy as jnp
from jax import lax
from jax.experimental import pallas as pl
from jax.experimental.pallas import tpu as pltpu

F32 = jnp.float32
BF16 = jnp.bfloat16
I32 = jnp.int32
U32 = jnp.uint32

EPS = 1e-6
LANES = 128
SUBLANES = 8
TOKEN_ROWS = 4
DMA_UNROLL = 8
VMEM_LIMIT = 56 * 1024 * 1024

ROPE_THETA = 500000.0
ROPE_FRACTION = 4
BLOCK = 128
CONV_CH = 512
CONV_K = 3
DIFF_HEADS = 4
DIFF_DK = 64
DIFF_DV = 128
SWA_Q_HEADS = 16
SWA_KV_HEADS = 4
SWA_HEAD_DIM = 64
SWA_WINDOW = 128
XATTN_HEADS = 4
XATTN_HEAD_DIM = 128
N_GROUPS = 4
EXPERTS_PER_GROUP = 8
N_EXPERTS = N_GROUPS * EXPERTS_PER_GROUP
TOP_K = 2
EXPERT_HIDDEN = 512

ROW_TILE = 512
EXPERT_TILE = 512
DISPATCH_TILE = 2048
COMBINE_TILE = 512
POST_CHAINS = 2
EXPERT_CHAINS = 2
FRONT_CHAINS = 2
NEG_INF = float("-inf")
LOG2E = math.log2(math.e)


def _cparams(sem):
    return pltpu.CompilerParams(dimension_semantics=sem, vmem_limit_bytes=VMEM_LIMIT)


def _rms(x, g):
    return x * lax.rsqrt(jnp.mean(x * x, axis=-1, keepdims=True) + EPS) * g


def _dot(a, b):
    return jnp.dot(a, b, preferred_element_type=F32)


def _dot_nt(a, b):
    return lax.dot_general(a, b, (((1,), (1,)), ((), ())), preferred_element_type=F32)


def _rope_lane_tables(seq, head_dim, scale):
    rot = head_dim // ROPE_FRACTION
    half = rot // 2
    pos = jnp.arange(seq, dtype=F32)
    inv = ROPE_THETA ** (-jnp.arange(0, rot, 2, dtype=F32) / rot)
    ang = pos[:, None] * inv[None, :]
    cos, sin = jnp.cos(ang), jnp.sin(ang)
    idx = jnp.arange(LANES) % head_dim
    cl = jnp.take(cos, idx % half, axis=1)
    sl = jnp.take(sin, idx % half, axis=1)
    c = jnp.where(idx < rot, cl, 1.0) * scale
    s_up = jnp.where(idx < half, -sl, 0.0) * scale
    s_dn = jnp.where((idx >= half) & (idx < rot), sl, 0.0) * scale
    return c.astype(F32), s_up.astype(F32), s_dn.astype(F32), half


def _rope_chunk(xc, c, s_up, s_dn, half):
    return (xc * c + pltpu.roll(xc, LANES - half, 1) * s_up + pltpu.roll(xc, half, 1) * s_dn)


def _norm_proj_kernel(x_ref, g_ref, w_ref, o_ref):
    h = _rms(x_ref[...], g_ref[...]).astype(BF16)
    o_ref[...] = _dot(h, w_ref[...]).astype(o_ref.dtype)


def _norm_proj(x2d, g, w_bf16, tm):
    m, d = x2d.shape
    n = w_bf16.shape[1]
    return pl.pallas_call(
        _norm_proj_kernel,
        grid=(m // tm,),
        in_specs=[pl.BlockSpec((tm, d), lambda i: (i, 0)),
                  pl.BlockSpec((1, d), lambda i: (0, 0)),
                  pl.BlockSpec((d, n), lambda i: (0, 0))],
        out_specs=pl.BlockSpec((tm, n), lambda i: (i, 0)),
        out_shape=jax.ShapeDtypeStruct((m, n), BF16),
        compiler_params=_cparams(("arbitrary",)),
        name="mem_kv_proj",
    )(x2d, g.reshape(1, d), w_bf16)


def _hyb_front_kernel(x_ref, g_ref, w_ref, cw_ref, cq_ref, suq_ref, sdq_ref, ck_ref, suk_ref, sdk_ref,
                      ya_ref, q_ref, k_ref, v_ref, cbuf, *, tm, half):
    s = pl.program_id(1)
    c = CONV_CH
    base = 3 * c
    nq = DIFF_HEADS * 2 * DIFF_DK
    rh = tm // FRONT_CHAINS
    starts = [ch * rh for ch in range(FRONT_CHAINS)]
    cw = cw_ref[...]

    @pl.when(s == 0)
    def _():
        cbuf[0:8, :] = jnp.zeros((8, c), F32)

    hs = [_rms(x_ref[0, r0:r0 + rh, :], g_ref[...]).astype(BF16) for r0 in starts]
    for r0, h in zip(starts, hs):
        gate_b = _dot(h, w_ref[:, 0:c])
        cu = _dot(h, w_ref[:, c:2 * c]) * _dot(h, w_ref[:, 2 * c:3 * c])
        cbuf[8 + r0:8 + r0 + rh, :] = cu
        conv = (cw[0:1, :] * cbuf[6 + r0:6 + r0 + rh, :] + cw[1:2, :] * cbuf[7 + r0:7 + r0 + rh, :] + cw[2:3, :] * cu)
        ya_ref[0, r0:r0 + rh, :] = (gate_b * conv).astype(BF16)
    cbuf[0:8, :] = cbuf[tm:tm + 8, :]

    for r0, h in zip(starts, hs):
        cq, suq, sdq = cq_ref[r0:r0 + rh, :], suq_ref[r0:r0 + rh, :], sdq_ref[r0:r0 + rh, :]
        pq = _dot(h, w_ref[:, base:base + nq])
        for j in range(nq // LANES):
            q_ref[0, r0:r0 + rh, j * LANES:(j + 1) * LANES] = _rope_chunk(
                pq[:, j * LANES:(j + 1) * LANES], cq, suq, sdq, half).astype(BF16)
    for r0, h in zip(starts, hs):
        ck, suk, sdk = ck_ref[r0:r0 + rh, :], suk_ref[r0:r0 + rh, :], sdk_ref[r0:r0 + rh, :]
        pk = _dot(h, w_ref[:, base + nq:base + 2 * nq])
        for j in range(nq // LANES):
            k_ref[0, r0:r0 + rh, j * LANES:(j + 1) * LANES] = _rope_chunk(
                pk[:, j * LANES:(j + 1) * LANES], ck, suk, sdk, half).astype(BF16)
    for r0, h in zip(starts, hs):
        v_ref[0, r0:r0 + rh, :] = _dot(h, w_ref[:, base + 2 * nq:]).astype(BF16)


def _hyb_front(x, g, w_in, conv_w, tabs_q, tabs_k, half):
    b, s, d = x.shape
    tm = ROW_TILE
    n = w_in.shape[1]
    nq = DIFF_HEADS * 2 * DIFF_DK
    nv = DIFF_HEADS * DIFF_DV
    tab_spec = pl.BlockSpec((tm, LANES), lambda bi, si: (si, 0))
    row = lambda width: pl.BlockSpec((1, tm, width), lambda bi, si: (bi, si, 0))
    return pl.pallas_call(
        functools.partial(_hyb_front_kernel, tm=tm, half=half),
        grid=(b, s // tm),
        in_specs=[row(d),
                  pl.BlockSpec((1, d), lambda bi, si: (0, 0)),
                  pl.BlockSpec((d, n), lambda bi, si: (0, 0)),
                  pl.BlockSpec((CONV_K, CONV_CH), lambda bi, si: (0, 0)),
                  tab_spec, tab_spec, tab_spec, tab_spec, tab_spec, tab_spec],
        out_specs=[row(CONV_CH), row(nq), row(nq), row(nv)],
        out_shape=[jax.ShapeDtypeStruct((b, s, CONV_CH), BF16),
                   jax.ShapeDtypeStruct((b, s, nq), BF16),
                   jax.ShapeDtypeStruct((b, s, nq), BF16),
                   jax.ShapeDtypeStruct((b, s, nv), BF16)],
        scratch_shapes=[pltpu.VMEM((tm + 8, CONV_CH), F32)],
        compiler_params=_cparams(("arbitrary", "arbitrary")),
        name="hyb_front",
    )(x, g.reshape(1, d), w_in, conv_w, *tabs_q, *tabs_k)


def _lane_fold(x, op):
    r = x[:, 0:LANES]
    for c in range(1, x.shape[1] // LANES):
        r = op(r, x[:, c * LANES:(c + 1) * LANES])
    return r


def _diff_attn_kernel(q_ref, k_ref, v_ref, lam_ref, g_ref, o_ref, sbuf, stat, acc, *, tq, hp, lambda_init):
    i = pl.program_id(2)
    lane = lax.broadcasted_iota(I32, (1, LANES), 1)
    qs = []
    for h in range(hp):
        q = q_ref[0, :, h * LANES:(h + 1) * LANES]
        zero = jnp.zeros_like(q)
        qs += [jnp.where(lane < DIFF_DK, q, zero), jnp.where(lane >= DIFF_DK, q, zero)]
    nc = 2 * hp

    hq = tq // 2
    diag = pl.multiple_of(i * tq, tq)

    def pass1(j, carry):
        for c in range(nc):
            h = c // 2
            kb = k_ref[0, pl.ds(pl.multiple_of(j * tq, tq), tq), h * LANES:(h + 1) * LANES]
            sc = _dot_nt(qs[c], kb)
            sbuf[c, j] = sc
            stat[c] = jnp.maximum(stat[c], _lane_fold(sc, jnp.maximum))
        return carry

    stat[...] = jnp.full(stat.shape, NEG_INF, F32)
    lax.fori_loop(0, i, pass1, 0)

    r = lax.broadcasted_iota(I32, (hq, hq), 0)
    cc = lax.broadcasted_iota(I32, (hq, hq), 1)
    tri = cc <= r
    for c in range(nc):
        h = c // 2
        k_lo = k_ref[0, pl.ds(diag, hq), h * LANES:(h + 1) * LANES]
        k_hi = k_ref[0, pl.ds(diag + hq, hq), h * LANES:(h + 1) * LANES]
        s_tl = jnp.where(tri, _dot_nt(qs[c][:hq], k_lo), NEG_INF)
        s_bl = _dot_nt(qs[c][hq:], k_lo)
        s_br = jnp.where(tri, _dot_nt(qs[c][hq:], k_hi), NEG_INF)
        sbuf[c, i, 0:hq, 0:hq] = s_tl
        sbuf[c, i, hq:tq, 0:hq] = s_bl
        sbuf[c, i, hq:tq, hq:tq] = s_br
        stat[c, 0:hq, :] = jnp.maximum(stat[c, 0:hq, :], _lane_fold(s_tl, jnp.maximum))
        stat[c, hq:tq, :] = jnp.maximum(stat[c, hq:tq, :],
                                        jnp.maximum(_lane_fold(s_bl, jnp.maximum), _lane_fold(s_br, jnp.maximum)))
    ms = [jnp.max(stat[c], axis=-1, keepdims=True) for c in range(nc)]

    ones = jnp.ones((tq, LANES), BF16)
    for h in range(hp):
        vb = v_ref[0, pl.ds(diag, tq), h * LANES:(h + 1) * LANES]
        v_ext = jnp.concatenate([vb, ones], axis=1)
        tops, bots = [], []
        for c in (2 * h, 2 * h + 1):
            tops.append(jnp.exp2(sbuf[c, i, 0:hq, 0:hq] - ms[c][:hq]))
            bots.append(jnp.exp2(sbuf[c, i, hq:tq, :] - ms[c][hq:]))
        top = _dot(jnp.concatenate(tops, axis=0).astype(BF16), v_ext[:hq])
        bot = _dot(jnp.concatenate(bots, axis=0).astype(BF16), v_ext)
        acc[h, 0:hq, :] = top[:hq]
        acc[h, hq:tq, :] = bot[:hq]
        acc[h, tq:tq + hq, :] = top[hq:]
        acc[h, tq + hq:2 * tq, :] = bot[hq:]

    def pass2(j, carry):
        for h in range(hp):
            vb = v_ref[0, pl.ds(pl.multiple_of(j * tq, tq), tq), h * LANES:(h + 1) * LANES]
            v_ext = jnp.concatenate([vb, ones], axis=1)
            p0 = jnp.exp2(sbuf[2 * h, j] - ms[2 * h])
            p1 = jnp.exp2(sbuf[2 * h + 1, j] - ms[2 * h + 1])
            acc[h] += _dot(jnp.concatenate([p0, p1], axis=0).astype(BF16), v_ext)
        return carry

    lax.fori_loop(0, i, pass2, 0)

    lf = lam_ref[...]
    lam = (jnp.exp(jnp.sum(lf[0:1] * lf[1:2], keepdims=True))
           - jnp.exp(jnp.sum(lf[2:3] * lf[3:4], keepdims=True)) + lambda_init)
    for h in range(hp):
        a0 = acc[h, 0:tq, :]
        a1 = acc[h, tq:2 * tq, :]
        o = a0[:, :LANES] / a0[:, LANES:] - lam * (a1[:, :LANES] / a1[:, LANES:])
        o_ref[0, :, h * LANES:(h + 1) * LANES] = (_rms(o, g_ref[...]) * (1.0 - lambda_init)).astype(BF16)


def _diff_attn(q, k, v, lam_vecs, subln_g, lambda_init):
    b, s, _ = q.shape
    tq = 512
    hp = 4
    blk = lambda bi, hi, qi: (bi, qi, hi)
    full = lambda bi, hi, qi: (bi, 0, hi)
    return pl.pallas_call(
        functools.partial(_diff_attn_kernel, tq=tq, hp=hp, lambda_init=lambda_init),
        grid=(b, DIFF_HEADS // hp, s // tq),
        in_specs=[pl.BlockSpec((1, tq, hp * LANES), blk),
                  pl.BlockSpec((1, s, hp * LANES), full),
                  pl.BlockSpec((1, s, hp * LANES), full),
                  pl.BlockSpec((4, DIFF_DK), lambda bi, hi, qi: (0, 0)),
                  pl.BlockSpec((1, DIFF_DV), lambda bi, hi, qi: (0, 0))],
        out_specs=pl.BlockSpec((1, tq, hp * LANES), blk),
        out_shape=jax.ShapeDtypeStruct((b, s, DIFF_HEADS * DIFF_DV), BF16),
        scratch_shapes=[pltpu.VMEM((2 * hp, s // tq, tq, tq), F32),
                        pltpu.VMEM((2 * hp, tq, LANES), F32),
                        pltpu.VMEM((hp, 2 * tq, DIFF_DV + LANES), F32)],
        compiler_params=_cparams(("arbitrary", "arbitrary", "arbitrary")),
        name="diff_attn",
    )(q, k, v, lam_vecs, subln_g.reshape(1, DIFF_DV))


def _swa_front_kernel(x_ref, g_ref, w_ref, b_ref, cq_ref, suq_ref, sdq_ref, ck_ref, suk_ref, sdk_ref,
                      q_ref, kv_ref, *, half):
    nq = SWA_Q_HEADS * SWA_HEAD_DIM
    nkv = SWA_KV_HEADS * SWA_HEAD_DIM
    tm = x_ref.shape[1]
    rh = tm // FRONT_CHAINS
    starts = [ch * rh for ch in range(FRONT_CHAINS)]
    hs = [_rms(x_ref[0, r0:r0 + rh, :], g_ref[...]).astype(BF16) for r0 in starts]
    for r0, h in zip(starts, hs):
        cq, suq, sdq = cq_ref[r0:r0 + rh, :], suq_ref[r0:r0 + rh, :], sdq_ref[r0:r0 + rh, :]
        pq = _dot(h, w_ref[:, 0:nq]) + b_ref[:, 0:nq]
        for j in range(nq // LANES):
            lo = j * LANES
            q_ref[0, r0:r0 + rh, lo:lo + LANES] = _rope_chunk(pq[:, lo:lo + LANES], cq, suq, sdq, half).astype(BF16)
    for r0, h in zip(starts, hs):
        ck, suk, sdk = ck_ref[r0:r0 + rh, :], suk_ref[r0:r0 + rh, :], sdk_ref[r0:r0 + rh, :]
        pkv = _dot(h, w_ref[:, nq:]) + b_ref[:, nq:]
        for j in range(nkv // LANES):
            lo = j * LANES
            kv_ref[0, r0:r0 + rh, lo:lo + LANES] = _rope_chunk(pkv[:, lo:lo + LANES], ck, suk, sdk, half).astype(BF16)
        kv_ref[0, r0:r0 + rh, nkv:] = pkv[:, nkv:].astype(BF16)


def _swa_front(x, g, w_qkv, b_qkv, tabs_q, tabs_k, half):
    b, s, d = x.shape
    tm = ROW_TILE
    n = w_qkv.shape[1]
    nq = SWA_Q_HEADS * SWA_HEAD_DIM
    nkv = SWA_KV_HEADS * SWA_HEAD_DIM
    tab_spec = pl.BlockSpec((tm, LANES), lambda bi, si: (si, 0))
    row = lambda width: pl.BlockSpec((1, tm, width), lambda bi, si: (bi, si, 0))
    return pl.pallas_call(
        functools.partial(_swa_front_kernel, half=half),
        grid=(b, s // tm),
        in_specs=[row(d),
                  pl.BlockSpec((1, d), lambda bi, si: (0, 0)),
                  pl.BlockSpec((d, n), lambda bi, si: (0, 0)),
                  pl.BlockSpec((1, n), lambda bi, si: (0, 0)),
                  tab_spec, tab_spec, tab_spec, tab_spec, tab_spec, tab_spec],
        out_specs=[row(nq), row(2 * nkv)],
        out_shape=[jax.ShapeDtypeStruct((b, s, nq), BF16),
                   jax.ShapeDtypeStruct((b, s, 2 * nkv), BF16)],
        compiler_params=_cparams(("arbitrary", "arbitrary")),
        name="swa_front",
    )(x, g.reshape(1, d), w_qkv, b_qkv.reshape(1, n), *tabs_q, *tabs_k)


def _swa_head_order():
    g_sz = SWA_Q_HEADS // SWA_KV_HEADS
    order = []
    for slab in range(SWA_Q_HEADS // 2):
        pair, j = slab // g_sz, slab % g_sz
        order += [(2 * pair) * g_sz + j, (2 * pair + 1) * g_sz + j]
    return order


def _swa_attn_kernel(sink_ref, q_ref, kvp_ref, kvc_ref, o_ref, *, nblk):
    i = pl.program_id(1)
    hd = SWA_HEAD_DIM
    nkv = SWA_KV_HEADS * hd
    g_sz = SWA_Q_HEADS // SWA_KV_HEADS
    kv = jnp.concatenate([kvp_ref[0], kvc_ref[0]], axis=0)
    r = lax.broadcasted_iota(I32, (BLOCK, 2 * BLOCK), 0)
    c = lax.broadcasted_iota(I32, (BLOCK, 2 * BLOCK), 1)
    rel = c - BLOCK - r
    in_win = (rel <= 0) & (rel > -SWA_WINDOW)
    lane = lax.broadcasted_iota(I32, (1, LANES), 1)
    lo_half = lane < hd
    ones = jnp.ones((2 * BLOCK, LANES), BF16)
    for n in range(nblk):
        mask = in_win & ((c >= BLOCK) | (i > 0)) if n == 0 else in_win
        keys = kv[n * BLOCK:(n + 2) * BLOCK]
        for pair in range(SWA_KV_HEADS // 2):
            k2 = keys[:, pair * LANES:(pair + 1) * LANES]
            v2 = keys[:, nkv + pair * LANES:nkv + (pair + 1) * LANES]
            v_ext = jnp.concatenate([v2, ones], axis=1)
            pieces = []
            for j in range(g_sz):
                slab = pair * g_sz + j
                qs = q_ref[0, n * BLOCK:(n + 1) * BLOCK, slab * LANES:(slab + 1) * LANES]
                zero = jnp.zeros_like(qs)
                pieces += [jnp.where(lo_half, qs, zero), jnp.where(lo_half, zero, qs)]
            sc = _dot_nt(jnp.concatenate(pieces, axis=0), k2)
            probs, tails = [], []
            for pc in range(2 * g_sz):
                scp = jnp.where(mask, sc[pc * BLOCK:(pc + 1) * BLOCK], NEG_INF)
                sink = sink_ref[2 * g_sz * pair + pc] * LOG2E
                m = jnp.maximum(jnp.max(scp, axis=-1, keepdims=True), sink)
                probs.append(jnp.exp2(scp - m).astype(BF16))
                tails.append(jnp.exp2(sink - m))
            pv = _dot(jnp.concatenate(probs, axis=0), v_ext)
            for j in range(g_sz):
                slab = pair * g_sz + j
                halves = []
                for hf in range(2):
                    pc = 2 * j + hf
                    blk = pv[pc * BLOCK:(pc + 1) * BLOCK]
                    halves.append(blk[:, :LANES] / (blk[:, LANES:] + tails[pc]))
                o_ref[0, n * BLOCK:(n + 1) * BLOCK, slab * LANES:(slab + 1) * LANES] = (
                    jnp.where(lo_half, halves[0], halves[1]).astype(BF16))


def _swa_attn(q, kv, sinks):
    b, s, nq = q.shape
    nblk = 4
    tq = nblk * BLOCK
    return pl.pallas_call(
        functools.partial(_swa_attn_kernel, nblk=nblk),
        grid_spec=pltpu.PrefetchScalarGridSpec(
            num_scalar_prefetch=1,
            grid=(b, s // tq),
            in_specs=[pl.BlockSpec((1, tq, nq), lambda bi, ni, sk: (bi, ni, 0)),
                      pl.BlockSpec((1, BLOCK, kv.shape[2]), lambda bi, ni, sk: (bi, jnp.maximum(ni * nblk - 1, 0), 0)),
                      pl.BlockSpec((1, tq, kv.shape[2]), lambda bi, ni, sk: (bi, ni, 0))],
            out_specs=pl.BlockSpec((1, tq, nq), lambda bi, ni, sk: (bi, ni, 0))),
        out_shape=jax.ShapeDtypeStruct((b, s, nq), BF16),
        compiler_params=_cparams(("arbitrary", "arbitrary")),
        name="swa_attn",
    )(sinks, q, kv, kv)


def _post_kernel(*refs, tm, n_a, has_bias):
    x_ref = refs[0]
    a_refs = refs[1:1 + n_a]
    k = 1 + n_a
    wo_refs = refs[k:k + n_a]
    k += n_a
    if has_bias:
        bo_ref = refs[k]
        k += 1
    (gx_ref, wq_ref, mkv_ref, wxo_ref, gf_ref, wr_ref, br_ref,
     x2_ref, h2_ref, mi_ref, mw_ref, cnt_ref, cnt_acc) = refs[k:]
    first = (pl.program_id(0) == 0) & (pl.program_id(1) == 0)

    @pl.when(first)
    def _():
        cnt_acc[...] = jnp.zeros_like(cnt_acc)

    xw = XATTN_HEADS * XATTN_HEAD_DIM
    ones = jnp.ones((mkv_ref.shape[1], LANES), BF16)

    rh = tm // POST_CHAINS

    def out_proj(r0):
        acc = _dot(a_refs[0][0, r0:r0 + rh, :], wo_refs[0][...])
        for a_ref, w_ref in zip(a_refs[1:], wo_refs[1:]):
            acc = acc + _dot(a_ref[0, r0:r0 + rh, :], w_ref[...])
        if has_bias:
            acc = acc + bo_ref[...]
        return x_ref[0, r0:r0 + rh, :] + acc

    def q_proj(x1):
        hx = _rms(x1, gx_ref[...]).astype(BF16)
        return (_dot(hx, wq_ref[...]) * (XATTN_HEAD_DIM ** -0.5 * LOG2E)).astype(BF16)

    def mem_attn(qx):
        outs = []
        for hh in range(XATTN_HEADS):
            lo = hh * XATTN_HEAD_DIM
            mk = mkv_ref[0, :, lo:lo + XATTN_HEAD_DIM]
            mv = jnp.concatenate([mkv_ref[0, :, xw + lo:xw + lo + XATTN_HEAD_DIM], ones], axis=1)
            sc = _dot_nt(qx[:, lo:lo + XATTN_HEAD_DIM], mk)
            m = jnp.max(sc, axis=-1, keepdims=True)
            pv = _dot(jnp.exp2(sc - m).astype(BF16), mv)
            outs.append((pv[:, :LANES] / pv[:, LANES:]).astype(BF16))
        return jnp.concatenate(outs, axis=-1)

    def o_proj(r0, x1, ox):
        x2 = x1 + _dot(ox, wxo_ref[...])
        x2_ref[0, r0:r0 + rh, :] = x2
        return x2

    def router(r0, x2):
        h2 = _rms(x2, gf_ref[...])
        _to_token_tiles(h2_ref, h2, rh, r0)
        return _dot(h2.astype(BF16), wr_ref[...]) + br_ref[...]

    starts = [c * rh for c in range(POST_CHAINS)]
    x1s = [out_proj(r0) for r0 in starts]
    qxs = [q_proj(x1) for x1 in x1s]
    oxs = [mem_attn(qx) for qx in qxs]
    x2s = [o_proj(r0, x1, ox) for r0, x1, ox in zip(starts, x1s, oxs)]
    logits = jnp.concatenate([router(r0, x2) for r0, x2 in zip(starts, x2s)], axis=0)
    lane = lax.broadcasted_iota(I32, (tm, LANES), 1).astype(F32)
    big = float(LANES)
    g_lo = float(N_EXPERTS)
    lg = jnp.where((lane >= g_lo) & (lane < g_lo + N_GROUPS), logits, NEG_INF)
    mg = jnp.max(lg, axis=-1, keepdims=True)
    g_lane = jnp.min(jnp.where(lg == mg, lane, big), axis=-1, keepdims=True)
    p_g = 1.0 / jnp.sum(jnp.exp(lg - mg), axis=-1, keepdims=True)
    e_lo = (g_lane - g_lo) * EXPERTS_PER_GROUP
    le = jnp.where((lane >= e_lo) & (lane < e_lo + EXPERTS_PER_GROUP), logits, NEG_INF)
    m1 = jnp.max(le, axis=-1, keepdims=True)
    i1 = jnp.min(jnp.where(le == m1, lane, big), axis=-1, keepdims=True)
    le2 = jnp.where(lane == i1, NEG_INF, le)
    m2 = jnp.max(le2, axis=-1, keepdims=True)
    i2 = jnp.min(jnp.where(le2 == m2, lane, big), axis=-1, keepdims=True)
    t = jnp.exp(m2 - m1)
    w1 = p_g / (1.0 + t)
    w2 = p_g * t / (1.0 + t)

    oh1 = lane == i1
    oh2 = lane == i2
    oh = jnp.where(oh1 | oh2, 1.0, 0.0)
    rr = lax.broadcasted_iota(I32, (tm, tm), 0)
    cc = lax.broadcasted_iota(I32, (tm, tm), 1)
    tri = jnp.where(cc < rr, 1.0, 0.0).astype(BF16)
    before = _dot(tri, oh.astype(BF16)) + cnt_acc[...]
    r1 = jnp.sum(jnp.where(oh1, before, 0.0), axis=-1, keepdims=True)
    r2 = jnp.sum(jnp.where(oh2, before, 0.0), axis=-1, keepdims=True)
    cnt_new = cnt_acc[...] + jnp.sum(oh, axis=0, keepdims=True)
    cnt_acc[...] = cnt_new
    cnt_ref[...] = cnt_new.astype(I32)

    mw_ref[...] = jnp.where(lane == 0, w1, jnp.where(lane == 1, w2, 0.0))
    ints = jnp.where(lane == 0, i1,
                     jnp.where(lane == 1, i2,
                               jnp.where(lane == 2, r1, jnp.where(lane == 3, r2, 0.0))))
    mi_ref[...] = ints.T[0:8, :].astype(I32)


def _post_mixer(x, a_list, wo_list, b_out, g_x, w_q, mkv, w_xo, g_f, w_rt, b_rt):
    b, s, d = x.shape
    tm = ROW_TILE
    ns = s // tm
    t = b * s
    n_a = len(a_list)
    has_bias = b_out is not None
    const2 = lambda bi, si: (0, 0)
    row = lambda width: pl.BlockSpec((1, tm, width), lambda bi, si: (bi, si, 0))
    in_specs = [row(d)] + [row(a.shape[2]) for a in a_list]
    in_specs += [pl.BlockSpec(w.shape, const2) for w in wo_list]
    args = [x, *a_list, *wo_list]
    if has_bias:
        in_specs.append(pl.BlockSpec((1, d), const2))
        args.append(b_out.reshape(1, d))
    xw = w_q.shape[1]
    in_specs += [pl.BlockSpec((1, d), const2),
                 pl.BlockSpec((d, xw), const2),
                 pl.BlockSpec((1, mkv.shape[1], mkv.shape[2]), lambda bi, si: (bi, 0, 0)),
                 pl.BlockSpec((xw, d), const2),
                 pl.BlockSpec((1, d), const2),
                 pl.BlockSpec((d, LANES), const2),
                 pl.BlockSpec((1, LANES), const2)]
    args += [g_x.reshape(1, d), w_q, mkv, w_xo, g_f.reshape(1, d), w_rt, b_rt]
    out_specs = [row(d),
                 pl.BlockSpec((tm * TOKEN_ROWS, LANES), lambda bi, si: (bi * ns + si, 0)),
                 pl.BlockSpec((8, tm), lambda bi, si: (0, bi * ns + si)),
                 pl.BlockSpec((tm, LANES), lambda bi, si: (bi * ns + si, 0)),
                 pl.BlockSpec((1, LANES), const2)]
    out_shape = [jax.ShapeDtypeStruct((b, s, d), F32),
                 jax.ShapeDtypeStruct((t * TOKEN_ROWS, LANES), U32),
                 jax.ShapeDtypeStruct((8, t), I32),
                 jax.ShapeDtypeStruct((t, LANES), F32),
                 jax.ShapeDtypeStruct((1, LANES), I32)]
    return pl.pallas_call(
        functools.partial(_post_kernel, tm=tm, n_a=n_a, has_bias=has_bias),
        grid=(b, ns),
        in_specs=in_specs,
        out_specs=out_specs,
        out_shape=out_shape,
        scratch_shapes=[pltpu.VMEM((1, LANES), F32)],
        compiler_params=_cparams(("arbitrary", "arbitrary")),
        name="post_mixer",
    )(*args)


def _to_token_tiles(dst_ref, val, rows, tok0=0):
    half = TOKEN_ROWS * LANES
    for j in range(TOKEN_ROWS):
        hi = val[:, j * LANES:(j + 1) * LANES].astype(BF16).astype(F32)
        lo = val[:, half + j * LANES:half + (j + 1) * LANES].astype(BF16).astype(F32)
        word = lax.bitcast_convert_type(hi, U32) | (lax.bitcast_convert_type(lo, U32) >> 16)
        dst_ref[pl.ds(tok0 * TOKEN_ROWS + j, rows, stride=TOKEN_ROWS), :] = word


def _from_token_tiles(src_ref, rows, tok0=0):
    his, los = [], []
    for j in range(TOKEN_ROWS):
        word = src_ref[pl.ds(tok0 * TOKEN_ROWS + j, rows, stride=TOKEN_ROWS), :]
        his.append(lax.bitcast_convert_type(word & jnp.uint32(0xFFFF0000), F32))
        los.append(lax.bitcast_convert_type(word << 16, F32))
    return jnp.concatenate(his + los, axis=1)


def _token_slice(ref, tok):
    return ref.at[pl.ds(pl.multiple_of(tok * TOKEN_ROWS, TOKEN_ROWS), TOKEN_ROWS)]


def _pos_kernel(off_ref, mi_ref, pos_ref):
    e = mi_ref[0:TOP_K, :]
    pos = mi_ref[TOP_K:2 * TOP_K, :]
    for j in range(N_EXPERTS):
        pos = pos + jnp.where(e == j, off_ref[j], 0)
    pos_ref[...] = pos


def _sorted_positions(off, meta_i):
    t = meta_i.shape[1]
    return pl.pallas_call(
        _pos_kernel,
        grid_spec=pltpu.PrefetchScalarGridSpec(
            num_scalar_prefetch=1,
            grid=(1,),
            in_specs=[pl.BlockSpec((8, t), lambda i, off: (0, 0))],
            out_specs=pl.BlockSpec((TOP_K, t), lambda i, off: (0, 0))),
        out_shape=jax.ShapeDtypeStruct((TOP_K, t), I32),
        compiler_params=_cparams(("arbitrary",)),
        name="moe_positions",
    )(off, meta_i)


def _tile_major(pos, tm):
    t = pos.shape[1]
    return pos.reshape(TOP_K, t // tm, tm).transpose(1, 0, 2).reshape(-1)


def _dispatch_kernel(pos_ref, h_ref, xs_ref, sem, *, tm):
    def issue(g, _):
        for u in range(DMA_UNROLL):
            r = g * DMA_UNROLL + u
            for kk in range(TOP_K):
                pos = pos_ref[kk * tm + r]
                pltpu.make_async_copy(_token_slice(h_ref, r), _token_slice(xs_ref, pos), sem).start(priority=kk)
        return 0

    lax.fori_loop(0, tm // DMA_UNROLL, issue, 0)
    for _ in range(TOP_K):
        pltpu.make_async_copy(h_ref, xs_ref.at[pl.ds(0, tm * TOKEN_ROWS)], sem).wait()


def _dispatch(pos, h2t, n_rows):
    tm = DISPATCH_TILE
    t = h2t.shape[0] // TOKEN_ROWS
    return pl.pallas_call(
        functools.partial(_dispatch_kernel, tm=tm),
        grid=(t // tm,),
        in_specs=[pl.BlockSpec((TOP_K * tm,), lambda i: (i,), memory_space=pltpu.SMEM),
                  pl.BlockSpec((tm * TOKEN_ROWS, LANES), lambda i: (i, 0))],
        out_specs=pl.BlockSpec(memory_space=pl.ANY),
        scratch_shapes=[pltpu.SemaphoreType.DMA],
        out_shape=jax.ShapeDtypeStruct((n_rows * TOKEN_ROWS, LANES), U32),
        compiler_params=_cparams(("arbitrary",)),
        name="moe_dispatch",
    )(_tile_major(pos, tm), h2t)


def _expert_kernel(tblk_ref, texp_ref, tn_ref, tnxt_ref, tpar_ref, xs_ref, wgu_hbm, wdn_hbm, ys_ref,
                   wgu_bf, wdn_bf, wgu_f, wdn_f, sems, *, tm, layer):
    i = pl.program_id(0)
    n_valid = tn_ref[i]
    expert = texp_ref[i]
    slot = tpar_ref[i]
    new_expert = (i == 0) | (expert != texp_ref[jnp.maximum(i - 1, 0)])

    def weight_copies(ex, sl):
        return (pltpu.make_async_copy(wgu_hbm.at[layer, ex], wgu_f.at[sl], sems.at[sl]),
                pltpu.make_async_copy(wdn_hbm.at[layer, ex], wdn_f.at[sl], sems.at[sl]))

    @pl.when(i == 0)
    def _():
        for cp in weight_copies(expert, slot):
            cp.start()

    @pl.when(new_expert)
    def _():
        @pl.when(tnxt_ref[i] >= 0)
        def _():
            for cp in weight_copies(tnxt_ref[i], 1 - slot):
                cp.start()

        for cp in weight_copies(expert, slot):
            cp.wait()
        wgu_bf[...] = wgu_f[slot].astype(BF16)
        wdn_bf[...] = wdn_f[slot].astype(BF16)

    @pl.when(n_valid > 0)
    def _():
        rh = tm // EXPERT_CHAINS
        starts = [c * rh for c in range(EXPERT_CHAINS)]
        row = lax.broadcasted_iota(I32, (rh, 1), 0)
        gus = []
        for r0 in starts:
            x = jnp.where(row + r0 < n_valid, _from_token_tiles(xs_ref, rh, r0), 0.0).astype(BF16)
            gus.append(_dot(x, wgu_bf[...]))
        ys = []
        for gu in gus:
            hid = (gu[:, :EXPERT_HIDDEN] / (1.0 + jnp.exp(-gu[:, :EXPERT_HIDDEN])) * gu[:, EXPERT_HIDDEN:]).astype(BF16)
            ys.append(_dot(hid, wdn_bf[...]))
        for r0, y in zip(starts, ys):
            _to_token_tiles(ys_ref, y, rh, r0)


def _experts(plan, xs, w_gu, w_dn, layer):
    tm = EXPERT_TILE
    nt = xs.shape[0] // (tm * TOKEN_ROWS)
    gu_shape, dn_shape = w_gu.shape[2:], w_dn.shape[2:]
    rows = lambda i, tb, te, tn, tx, tp: (tb[i], 0)
    return pl.pallas_call(
        functools.partial(_expert_kernel, tm=tm, layer=layer),
        grid_spec=pltpu.PrefetchScalarGridSpec(
            num_scalar_prefetch=5,
            grid=(nt,),
            in_specs=[pl.BlockSpec((tm * TOKEN_ROWS, LANES), rows),
                      pl.BlockSpec(memory_space=pl.ANY),
                      pl.BlockSpec(memory_space=pl.ANY)],
            out_specs=pl.BlockSpec((tm * TOKEN_ROWS, LANES), rows),
            scratch_shapes=[pltpu.VMEM(gu_shape, BF16), pltpu.VMEM(dn_shape, BF16),
                            pltpu.VMEM((2,) + gu_shape, F32), pltpu.VMEM((2,) + dn_shape, F32),
                            pltpu.SemaphoreType.DMA((2,))]),
        out_shape=jax.ShapeDtypeStruct(xs.shape, U32),
        compiler_params=_cparams(("arbitrary",)),
        name="moe_experts",
    )(*plan, xs, w_gu, w_dn)


def _moe_combine_tile(pos_cur, pos_nxt, mw_ref, x, ys_ref, ybuf, sems, i, n, tm):
    def gather(pos_ref, slot):
        def issue(g, _):
            for u in range(DMA_UNROLL):
                r = g * DMA_UNROLL + u
                for kk in range(TOP_K):
                    pos = pos_ref[kk * tm + r]
                    pltpu.make_async_copy(_token_slice(ys_ref, pos), _token_slice(ybuf.at[slot, kk], r),
                                          sems.at[slot]).start(priority=kk)
            return 0

        lax.fori_loop(0, tm // DMA_UNROLL, issue, 0)

    @pl.when(i == 0)
    def _():
        gather(pos_cur, 0)

    @pl.when(i + 1 < n)
    def _():
        gather(pos_nxt, (i + 1) % 2)

    slot = i % 2
    for kk in range(TOP_K):
        pltpu.make_async_copy(ys_ref.at[pl.ds(0, tm * TOKEN_ROWS)], ybuf.at[slot, kk], sems.at[slot]).wait()
    mw = mw_ref[...]
    return (x + mw[:, 0:1] * _from_token_tiles(ybuf.at[slot, 0], tm)
            + mw[:, 1:2] * _from_token_tiles(ybuf.at[slot, 1], tm))


def _combine_kernel(pos_cur, pos_nxt, mw_ref, x_ref, ys_ref, *rest, tm, final):
    if final:
        g_ref, o_ref, ybuf, sems = rest
    else:
        o_ref, ybuf, sems = rest
    x3 = _moe_combine_tile(pos_cur, pos_nxt, mw_ref, x_ref[...], ys_ref, ybuf, sems,
                           pl.program_id(0), pl.num_programs(0), tm)
    if final:
        x3 = _rms(x3, g_ref[...])
    o_ref[...] = x3


def _combine(moe, g_final):
    pos, meta_w, x2, ys = moe
    d = x2.shape[-1]
    x2d = x2.reshape(-1, d)
    t = x2d.shape[0]
    tm = COMBINE_TILE
    nt = t // tm
    final = g_final is not None
    in_specs = [pl.BlockSpec((TOP_K * tm,), lambda i: (i,), memory_space=pltpu.SMEM),
                pl.BlockSpec((TOP_K * tm,), lambda i: (jnp.minimum(i + 1, nt - 1),), memory_space=pltpu.SMEM),
                pl.BlockSpec((tm, LANES), lambda i: (i, 0)),
                pl.BlockSpec((tm, d), lambda i: (i, 0)),
                pl.BlockSpec(memory_space=pl.ANY)]
    pos_flat = _tile_major(pos, tm)
    args = [pos_flat, pos_flat, meta_w, x2d, ys]
    if final:
        in_specs.append(pl.BlockSpec((1, d), lambda i: (0, 0)))
        args.append(g_final.reshape(1, d))
    return pl.pallas_call(
        functools.partial(_combine_kernel, tm=tm, final=final),
        grid=(nt,),
        in_specs=in_specs,
        out_specs=pl.BlockSpec((tm, d), lambda i: (i, 0)),
        scratch_shapes=[pltpu.VMEM((2, TOP_K, tm * TOKEN_ROWS, LANES), U32), pltpu.SemaphoreType.DMA((2,))],
        out_shape=jax.ShapeDtypeStruct((t, d), F32),
        compiler_params=_cparams(("arbitrary",)),
        name="moe_combine",
    )(*args)


def _plan_kernel(cnt_ref, off_ref, blk_ref, exp_ref, nv_ref, nxt_ref, par_ref, *, tm, n_tiles):
    shift = tm.bit_length() - 1

    def clear(i, _):
        nxt_ref[i] = -1
        return 0

    lax.fori_loop(0, n_tiles, clear, 0)

    def per_expert(e, carry):
        row0, run, prev_first = carry
        c = cnt_ref[0, e]
        ntile = lax.shift_right_logical(c + (tm - 1), shift)
        off_ref[e] = row0
        t0 = lax.shift_right_logical(row0, shift)

        def fill(j, _):
            blk_ref[t0 + j] = t0 + j
            exp_ref[t0 + j] = e
            nv_ref[t0 + j] = jnp.minimum(c - j * tm, tm)
            par_ref[t0 + j] = run & 1
            return 0

        lax.fori_loop(0, ntile, fill, 0)
        has = ntile > 0

        @pl.when(has & (prev_first >= 0))
        def _():
            nxt_ref[prev_first] = e

        return (row0 + lax.shift_left(ntile, shift), run + has.astype(I32), jnp.where(has, t0, prev_first))

    total, _, _ = lax.fori_loop(0, N_EXPERTS, per_expert, (jnp.int32(0), jnp.int32(0), jnp.int32(-1)))
    used = lax.shift_right_logical(total, shift)
    last = jnp.maximum(used - 1, 0)

    def tail(i, _):
        blk_ref[i] = last
        exp_ref[i] = exp_ref[last]
        nv_ref[i] = 0
        par_ref[i] = par_ref[last]
        return 0

    lax.fori_loop(used, n_tiles, tail, 0)


def _moe_plan(counts, n_tiles):
    tm = EXPERT_TILE
    assert tm & (tm - 1) == 0
    smem = pl.BlockSpec(memory_space=pltpu.SMEM)
    return pl.pallas_call(
        functools.partial(_plan_kernel, tm=tm, n_tiles=n_tiles),
        in_specs=[smem],
        out_specs=[smem] * 6,
        out_shape=[jax.ShapeDtypeStruct((N_EXPERTS,), I32)] + [jax.ShapeDtypeStruct((n_tiles,), I32)] * 5,
        name="moe_plan",
    )(counts)


def _moe(x2, h2t, meta_i, meta_w, counts, w_gu, w_dn, layer):
    b, s, d = x2.shape
    assert d == 2 * TOKEN_ROWS * LANES
    n_rows = b * s * TOP_K + N_EXPERTS * EXPERT_TILE
    off, *tile_plan = _moe_plan(counts, n_rows // EXPERT_TILE)
    pos = _sorted_positions(off, meta_i)
    xs = _dispatch(pos, h2t, n_rows)
    ys = _experts(tile_plan, xs, w_gu, w_dn, layer)
    return pos, meta_w, x2, ys


def _router_weights(w_group, b_group, w_router, b_router):
    d = w_group.shape[0]
    pad = LANES - N_EXPERTS - N_GROUPS
    w = jnp.concatenate([w_router, w_group, jnp.zeros((d, pad), F32)], axis=1)
    bias = jnp.concatenate([b_router, b_group, jnp.zeros((pad,), F32)]).reshape(1, LANES)
    return w.astype(BF16), bias


def kernel(x, mem, mem_norm, mem_w_kv, norm_mix, norm_xattn, norm_ffn, hyb_w_in, hyb_conv_w, diff_lambda, diff_subln, hyb_w_out, swa_w_qkv, swa_b_qkv, swa_sinks, swa_w_out, swa_b_out, xattn_w_q, xattn_w_o, moe_w_group, moe_b_group, moe_w_router, moe_b_router, moe_w_gate_up, moe_w_down, final_norm):
    b, s, d = x.shape
    m = mem.shape[1]
    depth = norm_mix.shape[0]
    mkv = _norm_proj(mem.reshape(b * m, d), mem_norm, mem_w_kv.astype(BF16), m).reshape(b, m, -1)

    scale = DIFF_DK ** -0.5 * LOG2E
    cq, suq, sdq, half = _rope_lane_tables(s, DIFF_DK, scale)
    ck, suk, sdk, _ = _rope_lane_tables(s, DIFF_DK, 1.0)
    tabs_q, tabs_k = (cq, suq, sdq), (ck, suk, sdk)

    moe = None
    for l in range(depth):
        if moe is not None:
            x = _combine(moe, None).reshape(b, s, d)
        if l % 2 == 0:
            e = l // 2
            lambda_init = 0.8 - 0.6 * math.exp(-0.3 * l)
            ya, q, k, v = _hyb_front(x, norm_mix[l], hyb_w_in[e].astype(BF16), hyb_conv_w[e], tabs_q, tabs_k, half)
            o = _diff_attn(q, k, v, diff_lambda[e], diff_subln[e], lambda_init)
            w_out = hyb_w_out[e].astype(BF16)
            a_list, wo_list, b_out = [ya, o], [w_out[:CONV_CH], w_out[CONV_CH:]], None
        else:
            e = l // 2
            order = _swa_head_order()
            hd, nq = SWA_HEAD_DIM, SWA_Q_HEADS * SWA_HEAD_DIM
            heads = lambda a, axis: [lax.slice_in_dim(a, h * hd, (h + 1) * hd, axis=axis) for h in order]
            w_all = swa_w_qkv[e].astype(BF16)
            w_qkv = jnp.concatenate(heads(w_all, 1) + [w_all[:, nq:]], axis=1)
            b_qkv = jnp.concatenate(heads(swa_b_qkv[e], 0) + [swa_b_qkv[e][nq:]])
            sinks = jnp.stack([swa_sinks[e][h] for h in order])
            q, kv = _swa_front(x, norm_mix[l], w_qkv, b_qkv, tabs_q, tabs_k, half)
            o = _swa_attn(q, kv, sinks)
            w_out = jnp.concatenate(heads(swa_w_out[e].astype(BF16), 0), axis=0)
            a_list, wo_list, b_out = [o], [w_out], swa_b_out[e]
        w_rt, b_rt = _router_weights(moe_w_group[l], moe_b_group[l], moe_w_router[l], moe_b_router[l])
        x2, h2, meta_i, meta_w, counts = _post_mixer(
            x, a_list, wo_list, b_out, norm_xattn[l], xattn_w_q[l].astype(BF16), mkv,
            xattn_w_o[l].astype(BF16), norm_ffn[l], w_rt, b_rt)
        moe = _moe(x2, h2, meta_i, meta_w, counts, moe_w_gate_up, moe_w_down, l)
    return _combine(moe, final_norm).reshape(b, s, d)
```

```python
import functools
import math

import jax
import jax.numpy as jnp
from jax import lax
from jax.experimental import pallas as pl
from jax.experimental.pallas import tpu as pltpu

F32 = jnp.float32
BF16 = jnp.bfloat16
I32 = jnp.int32
U32 = jnp.uint32

EPS = 1e-6
LANES = 128
SUBLANES = 8
TOKEN_ROWS = 4
DMA_UNROLL = 8
VMEM_LIMIT = 56 * 1024 * 1024

ROPE_THETA = 500000.0
ROPE_FRACTION = 4
BLOCK = 128
CONV_CH = 512
CONV_K = 3
DIFF_HEADS = 4
DIFF_DK = 64
DIFF_DV = 128
SWA_Q_HEADS = 16
SWA_KV_HEADS = 4
SWA_HEAD_DIM = 64
SWA_WINDOW = 128
XATTN_HEADS = 4
XATTN_HEAD_DIM = 128
N_GROUPS = 4
EXPERTS_PER_GROUP = 8
N_EXPERTS = N_GROUPS * EXPERTS_PER_GROUP
TOP_K = 2
EXPERT_HIDDEN = 512

ROW_TILE = 512
EXPERT_TILE = 512
DISPATCH_TILE = 2048
COMBINE_TILE = 512
POST_CHAINS = 2
EXPERT_CHAINS = 4
FRONT_CHAINS = 2
NEG_INF = float("-inf")
LOG2E = math.log2(math.e)


def _cparams(sem):
    return pltpu.CompilerParams(dimension_semantics=sem, vmem_limit_bytes=VMEM_LIMIT)


def _rms(x, g):
    return x * lax.rsqrt(jnp.mean(x * x, axis=-1, keepdims=True) + EPS) * g


def _dot(a, b):
    return jnp.dot(a, b, preferred_element_type=F32)


def _dot_nt(a, b):
    return lax.dot_general(a, b, (((1,), (1,)), ((), ())), preferred_element_type=F32)


def _rope_lane_tables(seq, head_dim, scale):
    rot = head_dim // ROPE_FRACTION
    half = rot // 2
    pos = jnp.arange(seq, dtype=F32)
    inv = ROPE_THETA ** (-jnp.arange(0, rot, 2, dtype=F32) / rot)
    ang = pos[:, None] * inv[None, :]
    cos, sin = jnp.cos(ang), jnp.sin(ang)
    idx = jnp.arange(LANES) % head_dim
    cl = jnp.take(cos, idx % half, axis=1)
    sl = jnp.take(sin, idx % half, axis=1)
    c = jnp.where(idx < rot, cl, 1.0) * scale
    s_up = jnp.where(idx < half, -sl, 0.0) * scale
    s_dn = jnp.where((idx >= half) & (idx < rot), sl, 0.0) * scale
    return c.astype(F32), s_up.astype(F32), s_dn.astype(F32), half


def _rope_chunk(xc, c, s_up, s_dn, half):
    return (xc * c + pltpu.roll(xc, LANES - half, 1) * s_up + pltpu.roll(xc, half, 1) * s_dn)


def _norm_proj_kernel(x_ref, g_ref, w_ref, o_ref):
    h = _rms(x_ref[...], g_ref[...]).astype(BF16)
    o_ref[...] = _dot(h, w_ref[...]).astype(o_ref.dtype)


def _norm_proj(x2d, g, w_bf16, tm):
    m, d = x2d.shape
    n = w_bf16.shape[1]
    return pl.pallas_call(
        _norm_proj_kernel,
        grid=(m // tm,),
        in_specs=[pl.BlockSpec((tm, d), lambda i: (i, 0)),
                  pl.BlockSpec((1, d), lambda i: (0, 0)),
                  pl.BlockSpec((d, n), lambda i: (0, 0))],
        out_specs=pl.BlockSpec((tm, n), lambda i: (i, 0)),
        out_shape=jax.ShapeDtypeStruct((m, n), BF16),
        compiler_params=_cparams(("arbitrary",)),
        name="mem_kv_proj",
    )(x2d, g.reshape(1, d), w_bf16)


def _hyb_front_kernel(x_ref, g_ref, w_ref, cw_ref, cq_ref, suq_ref, sdq_ref, ck_ref, suk_ref, sdk_ref,
                      ya_ref, q_ref, k_ref, v_ref, cbuf, *, tm, half):
    s = pl.program_id(1)
    c = CONV_CH
    base = 3 * c
    nq = DIFF_HEADS * 2 * DIFF_DK
    rh = tm // FRONT_CHAINS
    starts = [ch * rh for ch in range(FRONT_CHAINS)]
    cw = cw_ref[...]

    @pl.when(s == 0)
    def _():
        cbuf[0:8, :] = jnp.zeros((8, c), F32)

    hs = [_rms(x_ref[0, r0:r0 + rh, :], g_ref[...]).astype(BF16) for r0 in starts]
    for r0, h in zip(starts, hs):
        gate_b = _dot(h, w_ref[:, 0:c])
        cu = _dot(h, w_ref[:, c:2 * c]) * _dot(h, w_ref[:, 2 * c:3 * c])
        cbuf[8 + r0:8 + r0 + rh, :] = cu
        conv = (cw[0:1, :] * cbuf[6 + r0:6 + r0 + rh, :] + cw[1:2, :] * cbuf[7 + r0:7 + r0 + rh, :] + cw[2:3, :] * cu)
        ya_ref[0, r0:r0 + rh, :] = (gate_b * conv).astype(BF16)
    cbuf[0:8, :] = cbuf[tm:tm + 8, :]

    for r0, h in zip(starts, hs):
        cq, suq, sdq = cq_ref[r0:r0 + rh, :], suq_ref[r0:r0 + rh, :], sdq_ref[r0:r0 + rh, :]
        pq = _dot(h, w_ref[:, base:base + nq])
        for j in range(nq // LANES):
            q_ref[0, r0:r0 + rh, j * LANES:(j + 1) * LANES] = _rope_chunk(
                pq[:, j * LANES:(j + 1) * LANES], cq, suq, sdq, half).astype(BF16)
    for r0, h in zip(starts, hs):
        ck, suk, sdk = ck_ref[r0:r0 + rh, :], suk_ref[r0:r0 + rh, :], sdk_ref[r0:r0 + rh, :]
        pk = _dot(h, w_ref[:, base + nq:base + 2 * nq])
        for j in range(nq // LANES):
            k_ref[0, r0:r0 + rh, j * LANES:(j + 1) * LANES] = _rope_chunk(
                pk[:, j * LANES:(j + 1) * LANES], ck, suk, sdk, half).astype(BF16)
    for r0, h in zip(starts, hs):
        v_ref[0, r0:r0 + rh, :] = _dot(h, w_ref[:, base + 2 * nq:]).astype(BF16)


def _hyb_front(x, g, w_in, conv_w, tabs_q, tabs_k, half):
    b, s, d = x.shape
    tm = ROW_TILE
    n = w_in.shape[1]
    nq = DIFF_HEADS * 2 * DIFF_DK
    nv = DIFF_HEADS * DIFF_DV
    tab_spec = pl.BlockSpec((tm, LANES), lambda bi, si: (si, 0))
    row = lambda width: pl.BlockSpec((1, tm, width), lambda bi, si: (bi, si, 0))
    return pl.pallas_call(
        functools.partial(_hyb_front_kernel, tm=tm, half=half),
        grid=(b, s // tm),
        in_specs=[row(d),
                  pl.BlockSpec((1, d), lambda bi, si: (0, 0)),
                  pl.BlockSpec((d, n), lambda bi, si: (0, 0)),
                  pl.BlockSpec((CONV_K, CONV_CH), lambda bi, si: (0, 0)),
                  tab_spec, tab_spec, tab_spec, tab_spec, tab_spec, tab_spec],
        out_specs=[row(CONV_CH), row(nq), row(nq), row(nv)],
        out_shape=[jax.ShapeDtypeStruct((b, s, CONV_CH), BF16),
                   jax.ShapeDtypeStruct((b, s, nq), BF16),
                   jax.ShapeDtypeStruct((b, s, nq), BF16),
                   jax.ShapeDtypeStruct((b, s, nv), BF16)],
        scratch_shapes=[pltpu.VMEM((tm + 8, CONV_CH), F32)],
        compiler_params=_cparams(("arbitrary", "arbitrary")),
        name="hyb_front",
    )(x, g.reshape(1, d), w_in, conv_w, *tabs_q, *tabs_k)


def _lane_fold(x, op):
    r = x[:, 0:LANES]
    for c in range(1, x.shape[1] // LANES):
        r = op(r, x[:, c * LANES:(c + 1) * LANES])
    return r


def _diff_attn_kernel(q_ref, k_ref, v_ref, lam_ref, g_ref, o_ref, sbuf, stat, acc, *, tq, hp, lambda_init):
    i = pl.program_id(2)
    lane = lax.broadcasted_iota(I32, (1, LANES), 1)
    qs = []
    for h in range(hp):
        q = q_ref[0, :, h * LANES:(h + 1) * LANES]
        zero = jnp.zeros_like(q)
        qs += [jnp.where(lane < DIFF_DK, q, zero), jnp.where(lane >= DIFF_DK, q, zero)]
    nc = 2 * hp

    hq = tq // 2
    diag = pl.multiple_of(i * tq, tq)

    def pass1(j, carry):
        for c in range(nc):
            h = c // 2
            kb = k_ref[0, pl.ds(pl.multiple_of(j * tq, tq), tq), h * LANES:(h + 1) * LANES]
            sc = _dot_nt(qs[c], kb)
            sbuf[c, j] = sc
            stat[c] = jnp.maximum(stat[c], _lane_fold(sc, jnp.maximum))
        return carry

    stat[...] = jnp.full(stat.shape, NEG_INF, F32)
    lax.fori_loop(0, i, pass1, 0)

    r = lax.broadcasted_iota(I32, (hq, hq), 0)
    cc = lax.broadcasted_iota(I32, (hq, hq), 1)
    tri = cc <= r
    for c in range(nc):
        h = c // 2
        k_lo = k_ref[0, pl.ds(diag, hq), h * LANES:(h + 1) * LANES]
        k_hi = k_ref[0, pl.ds(diag + hq, hq), h * LANES:(h + 1) * LANES]
        s_tl = jnp.where(tri, _dot_nt(qs[c][:hq], k_lo), NEG_INF)
        s_bl = _dot_nt(qs[c][hq:], k_lo)
        s_br = jnp.where(tri, _dot_nt(qs[c][hq:], k_hi), NEG_INF)
        sbuf[c, i, 0:hq, 0:hq] = s_tl
        sbuf[c, i, hq:tq, 0:hq] = s_bl
        sbuf[c, i, hq:tq, hq:tq] = s_br
        stat[c, 0:hq, :] = jnp.maximum(stat[c, 0:hq, :], _lane_fold(s_tl, jnp.maximum))
        stat[c, hq:tq, :] = jnp.maximum(stat[c, hq:tq, :],
                                        jnp.maximum(_lane_fold(s_bl, jnp.maximum), _lane_fold(s_br, jnp.maximum)))
    ms = [jnp.max(stat[c], axis=-1, keepdims=True) for c in range(nc)]

    ones = jnp.ones((tq, LANES), BF16)
    for h in range(hp):
        vb = v_ref[0, pl.ds(diag, tq), h * LANES:(h + 1) * LANES]
        v_ext = jnp.concatenate([vb, ones], axis=1)
        tops, bots = [], []
        for c in (2 * h, 2 * h + 1):
            tops.append(jnp.exp2(sbuf[c, i, 0:hq, 0:hq] - ms[c][:hq]))
            bots.append(jnp.exp2(sbuf[c, i, hq:tq, :] - ms[c][hq:]))
        top = _dot(jnp.concatenate(tops, axis=0).astype(BF16), v_ext[:hq])
        bot = _dot(jnp.concatenate(bots, axis=0).astype(BF16), v_ext)
        acc[h, 0:hq, :] = top[:hq]
        acc[h, hq:tq, :] = bot[:hq]
        acc[h, tq:tq + hq, :] = top[hq:]
        acc[h, tq + hq:2 * tq, :] = bot[hq:]

    def pass2(j, carry):
        for h in range(hp):
            vb = v_ref[0, pl.ds(pl.multiple_of(j * tq, tq), tq), h * LANES:(h + 1) * LANES]
            v_ext = jnp.concatenate([vb, ones], axis=1)
            p0 = jnp.exp2(sbuf[2 * h, j] - ms[2 * h])
            p1 = jnp.exp2(sbuf[2 * h + 1, j] - ms[2 * h + 1])
            acc[h] += _dot(jnp.concatenate([p0, p1], axis=0).astype(BF16), v_ext)
        return carry

    lax.fori_loop(0, i, pass2, 0)

    lf = lam_ref[...]
    lam = (jnp.exp(jnp.sum(lf[0:1] * lf[1:2], keepdims=True))
           - jnp.exp(jnp.sum(lf[2:3] * lf[3:4], keepdims=True)) + lambda_init)
    for h in range(hp):
        a0 = acc[h, 0:tq, :]
        a1 = acc[h, tq:2 * tq, :]
        o = a0[:, :LANES] / a0[:, LANES:] - lam * (a1[:, :LANES] / a1[:, LANES:])
        o_ref[0, :, h * LANES:(h + 1) * LANES] = (_rms(o, g_ref[...]) * (1.0 - lambda_init)).astype(BF16)


def _diff_attn(q, k, v, lam_vecs, subln_g, lambda_init):
    b, s, _ = q.shape
    tq = 512
    hp = 4
    blk = lambda bi, hi, qi: (bi, qi, hi)
    full = lambda bi, hi, qi: (bi, 0, hi)
    return pl.pallas_call(
        functools.partial(_diff_attn_kernel, tq=tq, hp=hp, lambda_init=lambda_init),
        grid=(b, DIFF_HEADS // hp, s // tq),
        in_specs=[pl.BlockSpec((1, tq, hp * LANES), blk),
                  pl.BlockSpec((1, s, hp * LANES), full),
                  pl.BlockSpec((1, s, hp * LANES), full),
                  pl.BlockSpec((4, DIFF_DK), lambda bi, hi, qi: (0, 0)),
                  pl.BlockSpec((1, DIFF_DV), lambda bi, hi, qi: (0, 0))],
        out_specs=pl.BlockSpec((1, tq, hp * LANES), blk),
        out_shape=jax.ShapeDtypeStruct((b, s, DIFF_HEADS * DIFF_DV), BF16),
        scratch_shapes=[pltpu.VMEM((2 * hp, s // tq, tq, tq), F32),
                        pltpu.VMEM((2 * hp, tq, LANES), F32),
                        pltpu.VMEM((hp, 2 * tq, DIFF_DV + LANES), F32)],
        compiler_params=_cparams(("arbitrary", "arbitrary", "arbitrary")),
        name="diff_attn",
    )(q, k, v, lam_vecs, subln_g.reshape(1, DIFF_DV))


def _swa_front_kernel(x_ref, g_ref, w_ref, b_ref, cq_ref, suq_ref, sdq_ref, ck_ref, suk_ref, sdk_ref,
                      q_ref, kv_ref, *, half):
    nq = SWA_Q_HEADS * SWA_HEAD_DIM
    nkv = SWA_KV_HEADS * SWA_HEAD_DIM
    tm = x_ref.shape[1]
    rh = tm // FRONT_CHAINS
    starts = [ch * rh for ch in range(FRONT_CHAINS)]
    hs = [_rms(x_ref[0, r0:r0 + rh, :], g_ref[...]).astype(BF16) for r0 in starts]
    for r0, h in zip(starts, hs):
        cq, suq, sdq = cq_ref[r0:r0 + rh, :], suq_ref[r0:r0 + rh, :], sdq_ref[r0:r0 + rh, :]
        pq = _dot(h, w_ref[:, 0:nq]) + b_ref[:, 0:nq]
        for j in range(nq // LANES):
            lo = j * LANES
            q_ref[0, r0:r0 + rh, lo:lo + LANES] = _rope_chunk(pq[:, lo:lo + LANES], cq, suq, sdq, half).astype(BF16)
    for r0, h in zip(starts, hs):
        ck, suk, sdk = ck_ref[r0:r0 + rh, :], suk_ref[r0:r0 + rh, :], sdk_ref[r0:r0 + rh, :]
        pkv = _dot(h, w_ref[:, nq:]) + b_ref[:, nq:]
        for j in range(nkv // LANES):
            lo = j * LANES
            kv_ref[0, r0:r0 + rh, lo:lo + LANES] = _rope_chunk(pkv[:, lo:lo + LANES], ck, suk, sdk, half).astype(BF16)
        kv_ref[0, r0:r0 + rh, nkv:] = pkv[:, nkv:].astype(BF16)


def _swa_front(x, g, w_qkv, b_qkv, tabs_q, tabs_k, half):
    b, s, d = x.shape
    tm = ROW_TILE
    n = w_qkv.shape[1]
    nq = SWA_Q_HEADS * SWA_HEAD_DIM
    nkv = SWA_KV_HEADS * SWA_HEAD_DIM
    tab_spec = pl.BlockSpec((tm, LANES), lambda bi, si: (si, 0))
    row = lambda width: pl.BlockSpec((1, tm, width), lambda bi, si: (bi, si, 0))
    return pl.pallas_call(
        functools.partial(_swa_front_kernel, half=half),
        grid=(b, s // tm),
        in_specs=[row(d),
                  pl.BlockSpec((1, d), lambda bi, si: (0, 0)),
                  pl.BlockSpec((d, n), lambda bi, si: (0, 0)),
                  pl.BlockSpec((1, n), lambda bi, si: (0, 0)),
                  tab_spec, tab_spec, tab_spec, tab_spec, tab_spec, tab_spec],
        out_specs=[row(nq), row(2 * nkv)],
        out_shape=[jax.ShapeDtypeStruct((b, s, nq), BF16),
                   jax.ShapeDtypeStruct((b, s, 2 * nkv), BF16)],
        compiler_params=_cparams(("arbitrary", "arbitrary")),
        name="swa_front",
    )(x, g.reshape(1, d), w_qkv, b_qkv.reshape(1, n), *tabs_q, *tabs_k)


def _swa_head_order():
    g_sz = SWA_Q_HEADS // SWA_KV_HEADS
    order = []
    for slab in range(SWA_Q_HEADS // 2):
        pair, j = slab // g_sz, slab % g_sz
        order += [(2 * pair) * g_sz + j, (2 * pair + 1) * g_sz + j]
    return order


def _swa_attn_kernel(sink_ref, q_ref, kvp_ref, kvc_ref, o_ref, *, nblk):
    i = pl.program_id(1)
    hd = SWA_HEAD_DIM
    nkv = SWA_KV_HEADS * hd
    g_sz = SWA_Q_HEADS // SWA_KV_HEADS
    kv = jnp.concatenate([kvp_ref[0], kvc_ref[0]], axis=0)
    r = lax.broadcasted_iota(I32, (BLOCK, 2 * BLOCK), 0)
    c = lax.broadcasted_iota(I32, (BLOCK, 2 * BLOCK), 1)
    rel = c - BLOCK - r
    in_win = (rel <= 0) & (rel > -SWA_WINDOW)
    lane = lax.broadcasted_iota(I32, (1, LANES), 1)
    lo_half = lane < hd
    ones = jnp.ones((2 * BLOCK, LANES), BF16)
    for n in range(nblk):
        mask = in_win & ((c >= BLOCK) | (i > 0)) if n == 0 else in_win
        keys = kv[n * BLOCK:(n + 2) * BLOCK]
        for pair in range(SWA_KV_HEADS // 2):
            k2 = keys[:, pair * LANES:(pair + 1) * LANES]
            v2 = keys[:, nkv + pair * LANES:nkv + (pair + 1) * LANES]
            v_ext = jnp.concatenate([v2, ones], axis=1)
            pieces = []
            for j in range(g_sz):
                slab = pair * g_sz + j
                qs = q_ref[0, n * BLOCK:(n + 1) * BLOCK, slab * LANES:(slab + 1) * LANES]
                zero = jnp.zeros_like(qs)
                pieces += [jnp.where(lo_half, qs, zero), jnp.where(lo_half, zero, qs)]
            sc = _dot_nt(jnp.concatenate(pieces, axis=0), k2)
            probs, tails = [], []
            for pc in range(2 * g_sz):
                scp = jnp.where(mask, sc[pc * BLOCK:(pc + 1) * BLOCK], NEG_INF)
                sink = sink_ref[2 * g_sz * pair + pc] * LOG2E
                m = jnp.maximum(jnp.max(scp, axis=-1, keepdims=True), sink)
                probs.append(jnp.exp2(scp - m).astype(BF16))
                tails.append(jnp.exp2(sink - m))
            pv = _dot(jnp.concatenate(probs, axis=0), v_ext)
            for j in range(g_sz):
                slab = pair * g_sz + j
                halves = []
                for hf in range(2):
                    pc = 2 * j + hf
                    blk = pv[pc * BLOCK:(pc + 1) * BLOCK]
                    halves.append(blk[:, :LANES] / (blk[:, LANES:] + tails[pc]))
                o_ref[0, n * BLOCK:(n + 1) * BLOCK, slab * LANES:(slab + 1) * LANES] = (
                    jnp.where(lo_half, halves[0], halves[1]).astype(BF16))


def _swa_attn(q, kv, sinks):
    b, s, nq = q.shape
    nblk = 4
    tq = nblk * BLOCK
    return pl.pallas_call(
        functools.partial(_swa_attn_kernel, nblk=nblk),
        grid_spec=pltpu.PrefetchScalarGridSpec(
            num_scalar_prefetch=1,
            grid=(b, s // tq),
            in_specs=[pl.BlockSpec((1, tq, nq), lambda bi, ni, sk: (bi, ni, 0)),
                      pl.BlockSpec((1, BLOCK, kv.shape[2]), lambda bi, ni, sk: (bi, jnp.maximum(ni * nblk - 1, 0), 0)),
                      pl.BlockSpec((1, tq, kv.shape[2]), lambda bi, ni, sk: (bi, ni, 0))],
            out_specs=pl.BlockSpec((1, tq, nq), lambda bi, ni, sk: (bi, ni, 0))),
        out_shape=jax.ShapeDtypeStruct((b, s, nq), BF16),
        compiler_params=_cparams(("arbitrary", "arbitrary")),
        name="swa_attn",
    )(sinks, q, kv, kv)


def _post_kernel(*refs, tm, n_a, has_bias):
    x_ref = refs[0]
    a_refs = refs[1:1 + n_a]
    k = 1 + n_a
    wo_refs = refs[k:k + n_a]
    k += n_a
    if has_bias:
        bo_ref = refs[k]
        k += 1
    (gx_ref, wq_ref, mkv_ref, wxo_ref, gf_ref, wr_ref, br_ref,
     x2_ref, h2_ref, mi_ref, mw_ref, cnt_ref, cnt_acc) = refs[k:]
    first = (pl.program_id(0) == 0) & (pl.program_id(1) == 0)

    @pl.when(first)
    def _():
        cnt_acc[...] = jnp.zeros_like(cnt_acc)

    xw = XATTN_HEADS * XATTN_HEAD_DIM
    ones = jnp.ones((mkv_ref.shape[1], LANES), BF16)

    rh = tm // POST_CHAINS

    def out_proj(r0):
        acc = _dot(a_refs[0][0, r0:r0 + rh, :], wo_refs[0][...])
        for a_ref, w_ref in zip(a_refs[1:], wo_refs[1:]):
            acc = acc + _dot(a_ref[0, r0:r0 + rh, :], w_ref[...])
        if has_bias:
            acc = acc + bo_ref[...]
        return x_ref[0, r0:r0 + rh, :] + acc

    def q_proj(x1):
        hx = _rms(x1, gx_ref[...]).astype(BF16)
        return (_dot(hx, wq_ref[...]) * (XATTN_HEAD_DIM ** -0.5 * LOG2E)).astype(BF16)

    def mem_attn(qx):
        outs = []
        for hh in range(XATTN_HEADS):
            lo = hh * XATTN_HEAD_DIM
            mk = mkv_ref[0, :, lo:lo + XATTN_HEAD_DIM]
            mv = jnp.concatenate([mkv_ref[0, :, xw + lo:xw + lo + XATTN_HEAD_DIM], ones], axis=1)
            sc = _dot_nt(qx[:, lo:lo + XATTN_HEAD_DIM], mk)
            m = jnp.max(sc, axis=-1, keepdims=True)
            pv = _dot(jnp.exp2(sc - m).astype(BF16), mv)
            outs.append((pv[:, :LANES] / pv[:, LANES:]).astype(BF16))
        return jnp.concatenate(outs, axis=-1)

    def o_proj(r0, x1, ox):
        x2 = x1 + _dot(ox, wxo_ref[...])
        x2_ref[0, r0:r0 + rh, :] = x2
        return x2

    def router(r0, x2):
        h2 = _rms(x2, gf_ref[...])
        _to_token_tiles(h2_ref, h2, rh, r0)
        return _dot(h2.astype(BF16), wr_ref[...]) + br_ref[...]

    starts = [c * rh for c in range(POST_CHAINS)]
    x1s = [out_proj(r0) for r0 in starts]
    qxs = [q_proj(x1) for x1 in x1s]
    oxs = [mem_attn(qx) for qx in qxs]
    x2s = [o_proj(r0, x1, ox) for r0, x1, ox in zip(starts, x1s, oxs)]
    logits = jnp.concatenate([router(r0, x2) for r0, x2 in zip(starts, x2s)], axis=0)
    lane = lax.broadcasted_iota(I32, (tm, LANES), 1).astype(F32)
    big = float(LANES)
    g_lo = float(N_EXPERTS)
    lg = jnp.where((lane >= g_lo) & (lane < g_lo + N_GROUPS), logits, NEG_INF)
    mg = jnp.max(lg, axis=-1, keepdims=True)
    g_lane = jnp.min(jnp.where(lg == mg, lane, big), axis=-1, keepdims=True)
    p_g = 1.0 / jnp.sum(jnp.exp(lg - mg), axis=-1, keepdims=True)
    e_lo = (g_lane - g_lo) * EXPERTS_PER_GROUP
    le = jnp.where((lane >= e_lo) & (lane < e_lo + EXPERTS_PER_GROUP), logits, NEG_INF)
    m1 = jnp.max(le, axis=-1, keepdims=True)
    i1 = jnp.min(jnp.where(le == m1, lane, big), axis=-1, keepdims=True)
    le2 = jnp.where(lane == i1, NEG_INF, le)
    m2 = jnp.max(le2, axis=-1, keepdims=True)
    i2 = jnp.min(jnp.where(le2 == m2, lane, big), axis=-1, keepdims=True)
    t = jnp.exp(m2 - m1)
    w1 = p_g / (1.0 + t)
    w2 = p_g * t / (1.0 + t)

    oh1 = lane == i1
    oh2 = lane == i2
    oh = jnp.where(oh1 | oh2, 1.0, 0.0)
    rr = lax.broadcasted_iota(I32, (tm, tm), 0)
    cc = lax.broadcasted_iota(I32, (tm, tm), 1)
    tri = jnp.where(cc < rr, 1.0, 0.0).astype(BF16)
    before = _dot(tri, oh.astype(BF16)) + cnt_acc[...]
    r1 = jnp.sum(jnp.where(oh1, before, 0.0), axis=-1, keepdims=True)
    r2 = jnp.sum(jnp.where(oh2, before, 0.0), axis=-1, keepdims=True)
    cnt_new = cnt_acc[...] + jnp.sum(oh, axis=0, keepdims=True)
    cnt_acc[...] = cnt_new
    cnt_ref[...] = cnt_new.astype(I32)

    mw_ref[...] = jnp.where(lane == 0, w1, jnp.where(lane == 1, w2, 0.0))
    ints = jnp.where(lane == 0, i1,
                     jnp.where(lane == 1, i2,
                               jnp.where(lane == 2, r1, jnp.where(lane == 3, r2, 0.0))))
    mi_ref[...] = ints.T[0:8, :].astype(I32)


def _post_mixer(x, a_list, wo_list, b_out, g_x, w_q, mkv, w_xo, g_f, w_rt, b_rt):
    b, s, d = x.shape
    tm = ROW_TILE
    ns = s // tm
    t = b * s
    n_a = len(a_list)
    has_bias = b_out is not None
    const2 = lambda bi, si: (0, 0)
    row = lambda width: pl.BlockSpec((1, tm, width), lambda bi, si: (bi, si, 0))
    in_specs = [row(d)] + [row(a.shape[2]) for a in a_list]
    in_specs += [pl.BlockSpec(w.shape, const2) for w in wo_list]
    args = [x, *a_list, *wo_list]
    if has_bias:
        in_specs.append(pl.BlockSpec((1, d), const2))
        args.append(b_out.reshape(1, d))
    xw = w_q.shape[1]
    in_specs += [pl.BlockSpec((1, d), const2),
                 pl.BlockSpec((d, xw), const2),
                 pl.BlockSpec((1, mkv.shape[1], mkv.shape[2]), lambda bi, si: (bi, 0, 0)),
                 pl.BlockSpec((xw, d), const2),
                 pl.BlockSpec((1, d), const2),
                 pl.BlockSpec((d, LANES), const2),
                 pl.BlockSpec((1, LANES), const2)]
    args += [g_x.reshape(1, d), w_q, mkv, w_xo, g_f.reshape(1, d), w_rt, b_rt]
    out_specs = [row(d),
                 pl.BlockSpec((tm * TOKEN_ROWS, LANES), lambda bi, si: (bi * ns + si, 0)),
                 pl.BlockSpec((8, tm), lambda bi, si: (0, bi * ns + si)),
                 pl.BlockSpec((tm, LANES), lambda bi, si: (bi * ns + si, 0)),
                 pl.BlockSpec((1, LANES), const2)]
    out_shape = [jax.ShapeDtypeStruct((b, s, d), F32),
                 jax.ShapeDtypeStruct((t * TOKEN_ROWS, LANES), U32),
                 jax.ShapeDtypeStruct((8, t), I32),
                 jax.ShapeDtypeStruct((t, LANES), F32),
                 jax.ShapeDtypeStruct((1, LANES), I32)]
    return pl.pallas_call(
        functools.partial(_post_kernel, tm=tm, n_a=n_a, has_bias=has_bias),
        grid=(b, ns),
        in_specs=in_specs,
        out_specs=out_specs,
        out_shape=out_shape,
        scratch_shapes=[pltpu.VMEM((1, LANES), F32)],
        compiler_params=_cparams(("arbitrary", "arbitrary")),
        name="post_mixer",
    )(*args)


def _to_token_tiles(dst_ref, val, rows, tok0=0):
    half = TOKEN_ROWS * LANES
    for j in range(TOKEN_ROWS):
        hi = val[:, j * LANES:(j + 1) * LANES].astype(BF16).astype(F32)
        lo = val[:, half + j * LANES:half + (j + 1) * LANES].astype(BF16).astype(F32)
        word = lax.bitcast_convert_type(hi, U32) | (lax.bitcast_convert_type(lo, U32) >> 16)
        dst_ref[pl.ds(tok0 * TOKEN_ROWS + j, rows, stride=TOKEN_ROWS), :] = word


def _from_token_tiles(src_ref, rows, tok0=0):
    his, los = [], []
    for j in range(TOKEN_ROWS):
        word = src_ref[pl.ds(tok0 * TOKEN_ROWS + j, rows, stride=TOKEN_ROWS), :]
        his.append(lax.bitcast_convert_type(word & jnp.uint32(0xFFFF0000), F32))
        los.append(lax.bitcast_convert_type(word << 16, F32))
    return jnp.concatenate(his + los, axis=1)


def _token_slice(ref, tok):
    return ref.at[pl.ds(pl.multiple_of(tok * TOKEN_ROWS, TOKEN_ROWS), TOKEN_ROWS)]


def _pos_kernel(off_ref, mi_ref, pos_ref):
    e = mi_ref[0:TOP_K, :]
    pos = mi_ref[TOP_K:2 * TOP_K, :]
    for j in range(N_EXPERTS):
        pos = pos + jnp.where(e == j, off_ref[j], 0)
    pos_ref[...] = pos


def _sorted_positions(off, meta_i):
    t = meta_i.shape[1]
    return pl.pallas_call(
        _pos_kernel,
        grid_spec=pltpu.PrefetchScalarGridSpec(
            num_scalar_prefetch=1,
            grid=(1,),
            in_specs=[pl.BlockSpec((8, t), lambda i, off: (0, 0))],
            out_specs=pl.BlockSpec((TOP_K, t), lambda i, off: (0, 0))),
        out_shape=jax.ShapeDtypeStruct((TOP_K, t), I32),
        compiler_params=_cparams(("arbitrary",)),
        name="moe_positions",
    )(off, meta_i)


def _tile_major(pos, tm):
    t = pos.shape[1]
    return pos.reshape(TOP_K, t // tm, tm).transpose(1, 0, 2).reshape(-1)


def _dispatch_kernel(pos_ref, h_ref, xs_ref, sem, *, tm):
    def issue(g, _):
        for u in range(DMA_UNROLL):
            r = g * DMA_UNROLL + u
            for kk in range(TOP_K):
                pos = pos_ref[kk * tm + r]
                pltpu.make_async_copy(_token_slice(h_ref, r), _token_slice(xs_ref, pos), sem).start(priority=kk)
        return 0

    lax.fori_loop(0, tm // DMA_UNROLL, issue, 0)
    for _ in range(TOP_K):
        pltpu.make_async_copy(h_ref, xs_ref.at[pl.ds(0, tm * TOKEN_ROWS)], sem).wait()


def _dispatch(pos, h2t, n_rows):
    tm = DISPATCH_TILE
    t = h2t.shape[0] // TOKEN_ROWS
    return pl.pallas_call(
        functools.partial(_dispatch_kernel, tm=tm),
        grid=(t // tm,),
        in_specs=[pl.BlockSpec((TOP_K * tm,), lambda i: (i,), memory_space=pltpu.SMEM),
                  pl.BlockSpec((tm * TOKEN_ROWS, LANES), lambda i: (i, 0))],
        out_specs=pl.BlockSpec(memory_space=pl.ANY),
        scratch_shapes=[pltpu.SemaphoreType.DMA],
        out_shape=jax.ShapeDtypeStruct((n_rows * TOKEN_ROWS, LANES), U32),
        compiler_params=_cparams(("arbitrary",)),
        name="moe_dispatch",
    )(_tile_major(pos, tm), h2t)


def _expert_kernel(tblk_ref, texp_ref, tn_ref, tnxt_ref, tpar_ref, xs_ref, wgu_hbm, wdn_hbm, ys_ref,
                   wgu_bf, wdn_bf, wgu_f, wdn_f, sems, *, tm, layer):
    i = pl.program_id(0)
    n_valid = tn_ref[i]
    expert = texp_ref[i]
    slot = tpar_ref[i]
    new_expert = (i == 0) | (expert != texp_ref[jnp.maximum(i - 1, 0)])

    def weight_copies(ex, sl):
        return (pltpu.make_async_copy(wgu_hbm.at[layer, ex], wgu_f.at[sl], sems.at[sl]),
                pltpu.make_async_copy(wdn_hbm.at[layer, ex], wdn_f.at[sl], sems.at[sl]))

    @pl.when(i == 0)
    def _():
        for cp in weight_copies(expert, slot):
            cp.start()

    @pl.when(new_expert)
    def _():
        @pl.when(tnxt_ref[i] >= 0)
        def _():
            for cp in weight_copies(tnxt_ref[i], 1 - slot):
                cp.start()

        for cp in weight_copies(expert, slot):
            cp.wait()
        wgu_bf[...] = wgu_f[slot].astype(BF16)
        wdn_bf[...] = wdn_f[slot].astype(BF16)

    rh = tm // EXPERT_CHAINS

    def run_chains(n_chains):
        starts = [c * rh for c in range(n_chains)]
        row = lax.broadcasted_iota(I32, (rh, 1), 0)
        gus = []
        for r0 in starts:
            x = jnp.where(row + r0 < n_valid, _from_token_tiles(xs_ref, rh, r0), 0.0).astype(BF16)
            gus.append(_dot(x, wgu_bf[...]))
        ys = []
        for gu in gus:
            hid = (gu[:, :EXPERT_HIDDEN] / (1.0 + jnp.exp(-gu[:, :EXPERT_HIDDEN])) * gu[:, EXPERT_HIDDEN:]).astype(BF16)
            ys.append(_dot(hid, wdn_bf[...]))
        for r0, y in zip(starts, ys):
            _to_token_tiles(ys_ref, y, rh, r0)

    for n_chains in range(1, EXPERT_CHAINS + 1):
        lo = (n_chains - 1) * rh
        hi = n_chains * rh if n_chains < EXPERT_CHAINS else tm
        pl.when((n_valid > lo) & (n_valid <= hi))(functools.partial(run_chains, n_chains))


def _experts(plan, xs, w_gu, w_dn, layer):
    tm = EXPERT_TILE
    nt = xs.shape[0] // (tm * TOKEN_ROWS)
    gu_shape, dn_shape = w_gu.shape[2:], w_dn.shape[2:]
    rows = lambda i, tb, te, tn, tx, tp: (tb[i], 0)
    return pl.pallas_call(
        functools.partial(_expert_kernel, tm=tm, layer=layer),
        grid_spec=pltpu.PrefetchScalarGridSpec(
            num_scalar_prefetch=5,
            grid=(nt,),
            in_specs=[pl.BlockSpec((tm * TOKEN_ROWS, LANES), rows),
                      pl.BlockSpec(memory_space=pl.ANY),
                      pl.BlockSpec(memory_space=pl.ANY)],
            out_specs=pl.BlockSpec((tm * TOKEN_ROWS, LANES), rows),
            scratch_shapes=[pltpu.VMEM(gu_shape, BF16), pltpu.VMEM(dn_shape, BF16),
                            pltpu.VMEM((2,) + gu_shape, F32), pltpu.VMEM((2,) + dn_shape, F32),
                            pltpu.SemaphoreType.DMA((2,))]),
        out_shape=jax.ShapeDtypeStruct(xs.shape, U32),
        compiler_params=_cparams(("arbitrary",)),
        name="moe_experts",
    )(*plan, xs, w_gu, w_dn)


def _moe_combine_tile(pos_cur, pos_nxt, mw_ref, x, ys_ref, ybuf, sems, i, n, tm):
    def gather(pos_ref, slot):
        def issue(g, _):
            for u in range(DMA_UNROLL):
                r = g * DMA_UNROLL + u
                for kk in range(TOP_K):
                    pos = pos_ref[kk * tm + r]
                    pltpu.make_async_copy(_token_slice(ys_ref, pos), _token_slice(ybuf.at[slot, kk], r),
                                          sems.at[slot]).start(priority=kk)
            return 0

        lax.fori_loop(0, tm // DMA_UNROLL, issue, 0)

    @pl.when(i == 0)
    def _():
        gather(pos_cur, 0)

    @pl.when(i + 1 < n)
    def _():
        gather(pos_nxt, (i + 1) % 2)

    slot = i % 2
    for kk in range(TOP_K):
        pltpu.make_async_copy(ys_ref.at[pl.ds(0, tm * TOKEN_ROWS)], ybuf.at[slot, kk], sems.at[slot]).wait()
    mw = mw_ref[...]
    return (x + mw[:, 0:1] * _from_token_tiles(ybuf.at[slot, 0], tm)
            + mw[:, 1:2] * _from_token_tiles(ybuf.at[slot, 1], tm))


def _combine_kernel(pos_cur, pos_nxt, mw_ref, x_ref, ys_ref, *rest, tm, final):
    if final:
        g_ref, o_ref, ybuf, sems = rest
    else:
        o_ref, ybuf, sems = rest
    x3 = _moe_combine_tile(pos_cur, pos_nxt, mw_ref, x_ref[...], ys_ref, ybuf, sems,
                           pl.program_id(0), pl.num_programs(0), tm)
    if final:
        x3 = _rms(x3, g_ref[...])
    o_ref[...] = x3


def _combine(moe, g_final):
    pos, meta_w, x2, ys = moe
    d = x2.shape[-1]
    x2d = x2.reshape(-1, d)
    t = x2d.shape[0]
    tm = COMBINE_TILE
    nt = t // tm
    final = g_final is not None
    in_specs = [pl.BlockSpec((TOP_K * tm,), lambda i: (i,), memory_space=pltpu.SMEM),
                pl.BlockSpec((TOP_K * tm,), lambda i: (jnp.minimum(i + 1, nt - 1),), memory_space=pltpu.SMEM),
                pl.BlockSpec((tm, LANES), lambda i: (i, 0)),
                pl.BlockSpec((tm, d), lambda i: (i, 0)),
                pl.BlockSpec(memory_space=pl.ANY)]
    pos_flat = _tile_major(pos, tm)
    args = [pos_flat, pos_flat, meta_w, x2d, ys]
    if final:
        in_specs.append(pl.BlockSpec((1, d), lambda i: (0, 0)))
        args.append(g_final.reshape(1, d))
    return pl.pallas_call(
        functools.partial(_combine_kernel, tm=tm, final=final),
        grid=(nt,),
        in_specs=in_specs,
        out_specs=pl.BlockSpec((tm, d), lambda i: (i, 0)),
        scratch_shapes=[pltpu.VMEM((2, TOP_K, tm * TOKEN_ROWS, LANES), U32), pltpu.SemaphoreType.DMA((2,))],
        out_shape=jax.ShapeDtypeStruct((t, d), F32),
        compiler_params=_cparams(("arbitrary",)),
        name="moe_combine",
    )(*args)


def _plan_kernel(cnt_ref, off_ref, blk_ref, exp_ref, nv_ref, nxt_ref, par_ref, *, tm, n_tiles):
    shift = tm.bit_length() - 1

    def clear(i, _):
        nxt_ref[i] = -1
        return 0

    lax.fori_loop(0, n_tiles, clear, 0)

    def per_expert(e, carry):
        row0, run, prev_first = carry
        c = cnt_ref[0, e]
        ntile = lax.shift_right_logical(c + (tm - 1), shift)
        off_ref[e] = row0
        t0 = lax.shift_right_logical(row0, shift)

        def fill(j, _):
            blk_ref[t0 + j] = t0 + j
            exp_ref[t0 + j] = e
            nv_ref[t0 + j] = jnp.minimum(c - j * tm, tm)
            par_ref[t0 + j] = run & 1
            return 0

        lax.fori_loop(0, ntile, fill, 0)
        has = ntile > 0

        @pl.when(has & (prev_first >= 0))
        def _():
            nxt_ref[prev_first] = e

        return (row0 + lax.shift_left(ntile, shift), run + has.astype(I32), jnp.where(has, t0, prev_first))

    total, _, _ = lax.fori_loop(0, N_EXPERTS, per_expert, (jnp.int32(0), jnp.int32(0), jnp.int32(-1)))
    used = lax.shift_right_logical(total, shift)
    last = jnp.maximum(used - 1, 0)

    def tail(i, _):
        blk_ref[i] = last
        exp_ref[i] = exp_ref[last]
        nv_ref[i] = 0
        par_ref[i] = par_ref[last]
        return 0

    lax.fori_loop(used, n_tiles, tail, 0)


def _moe_plan(counts, n_tiles):
    tm = EXPERT_TILE
    assert tm & (tm - 1) == 0
    smem = pl.BlockSpec(memory_space=pltpu.SMEM)
    return pl.pallas_call(
        functools.partial(_plan_kernel, tm=tm, n_tiles=n_tiles),
        in_specs=[smem],
        out_specs=[smem] * 6,
        out_shape=[jax.ShapeDtypeStruct((N_EXPERTS,), I32)] + [jax.ShapeDtypeStruct((n_tiles,), I32)] * 5,
        name="moe_plan",
    )(counts)


def _moe(x2, h2t, meta_i, meta_w, counts, w_gu, w_dn, layer):
    b, s, d = x2.shape
    assert d == 2 * TOKEN_ROWS * LANES
    n_rows = b * s * TOP_K + N_EXPERTS * EXPERT_TILE
    off, *tile_plan = _moe_plan(counts, n_rows // EXPERT_TILE)
    pos = _sorted_positions(off, meta_i)
    xs = _dispatch(pos, h2t, n_rows)
    ys = _experts(tile_plan, xs, w_gu, w_dn, layer)
    return pos, meta_w, x2, ys


def _router_weights(w_group, b_group, w_router, b_router):
    d = w_group.shape[0]
    pad = LANES - N_EXPERTS - N_GROUPS
    w = jnp.concatenate([w_router, w_group, jnp.zeros((d, pad), F32)], axis=1)
    bias = jnp.concatenate([b_router, b_group, jnp.zeros((pad,), F32)]).reshape(1, LANES)
    return w.astype(BF16), bias


def kernel(x, mem, mem_norm, mem_w_kv, norm_mix, norm_xattn, norm_ffn, hyb_w_in, hyb_conv_w, diff_lambda, diff_subln, hyb_w_out, swa_w_qkv, swa_b_qkv, swa_sinks, swa_w_out, swa_b_out, xattn_w_q, xattn_w_o, moe_w_group, moe_b_group, moe_w_router, moe_b_router, moe_w_gate_up, moe_w_down, final_norm):
    b, s, d = x.shape
    m = mem.shape[1]
    depth = norm_mix.shape[0]
    mkv = _norm_proj(mem.reshape(b * m, d), mem_norm, mem_w_kv.astype(BF16), m).reshape(b, m, -1)

    scale = DIFF_DK ** -0.5 * LOG2E
    cq, suq, sdq, half = _rope_lane_tables(s, DIFF_DK, scale)
    ck, suk, sdk, _ = _rope_lane_tables(s, DIFF_DK, 1.0)
    tabs_q, tabs_k = (cq, suq, sdq), (ck, suk, sdk)

    moe = None
    for l in range(depth):
        if moe is not None:
            x = _combine(moe, None).reshape(b, s, d)
        if l % 2 == 0:
            e = l // 2
            lambda_init = 0.8 - 0.6 * math.exp(-0.3 * l)
            ya, q, k, v = _hyb_front(x, norm_mix[l], hyb_w_in[e].astype(BF16), hyb_conv_w[e], tabs_q, tabs_k, half)
            o = _diff_attn(q, k, v, diff_lambda[e], diff_subln[e], lambda_init)
            w_out = hyb_w_out[e].astype(BF16)
            a_list, wo_list, b_out = [ya, o], [w_out[:CONV_CH], w_out[CONV_CH:]], None
        else:
            e = l // 2
            order = _swa_head_order()
            hd, nq = SWA_HEAD_DIM, SWA_Q_HEADS * SWA_HEAD_DIM
            heads = lambda a, axis: [lax.slice_in_dim(a, h * hd, (h + 1) * hd, axis=axis) for h in order]
            w_all = swa_w_qkv[e].astype(BF16)
            w_qkv = jnp.concatenate(heads(w_all, 1) + [w_all[:, nq:]], axis=1)
            b_qkv = jnp.concatenate(heads(swa_b_qkv[e], 0) + [swa_b_qkv[e][nq:]])
            sinks = jnp.stack([swa_sinks[e][h] for h in order])
            q, kv = _swa_front(x, norm_mix[l], w_qkv, b_qkv, tabs_q, tabs_k, half)
            o = _swa_attn(q, kv, sinks)
            w_out = jnp.concatenate(heads(swa_w_out[e].astype(BF16), 0), axis=0)
            a_list, wo_list, b_out = [o], [w_out], swa_b_out[e]
        w_rt, b_rt = _router_weights(moe_w_group[l], moe_b_group[l], moe_w_router[l], moe_b_router[l])
        x2, h2, meta_i, meta_w, counts = _post_mixer(
            x, a_list, wo_list, b_out, norm_xattn[l], xattn_w_q[l].astype(BF16), mkv,
            xattn_w_o[l].astype(BF16), norm_ffn[l], w_rt, b_rt)
        moe = _moe(x2, h2, meta_i, meta_w, counts, moe_w_gate_up, moe_w_down, l)
    return _combine(moe, final_norm).reshape(b, s, d)
```

```python
import functools
import math

import jax
import jax.numpy as jnp
from jax import lax
from jax.experimental import pallas as pl
from jax.experimental.pallas import tpu as pltpu

F32 = jnp.float32
BF16 = jnp.bfloat16
I32 = jnp.int32
U32 = jnp.uint32

EPS = 1e-6
LANES = 128
SUBLANES = 8
TOKEN_ROWS = 4
DMA_UNROLL = 8
VMEM_LIMIT = 56 * 1024 * 1024

ROPE_THETA = 500000.0
ROPE_FRACTION = 4
BLOCK = 128
CONV_CH = 512
CONV_K = 3
DIFF_HEADS = 4
DIFF_DK = 64
DIFF_DV = 128
SWA_Q_HEADS = 16
SWA_KV_HEADS = 4
SWA_HEAD_DIM = 64
SWA_WINDOW = 128
XATTN_HEADS = 4
XATTN_HEAD_DIM = 128
N_GROUPS = 4
EXPERTS_PER_GROUP = 8
N_EXPERTS = N_GROUPS * EXPERTS_PER_GROUP
TOP_K = 2
EXPERT_HIDDEN = 512

ROW_TILE = 512
EXPERT_TILE = 512
DISPATCH_TILE = 2048
COMBINE_TILE = 512
POST_CHAINS = 2
EXPERT_CHAINS = 2
FRONT_CHAINS = 2
NEG_INF = float("-inf")
LOG2E = math.log2(math.e)


def _cparams(sem):
    return pltpu.CompilerParams(dimension_semantics=sem, vmem_limit_bytes=VMEM_LIMIT)


def _rms(x, g):
    return x * lax.rsqrt(jnp.mean(x * x, axis=-1, keepdims=True) + EPS) * g


def _dot(a, b):
    return jnp.dot(a, b, preferred_element_type=F32)


def _dot_nt(a, b):
    return lax.dot_general(a, b, (((1,), (1,)), ((), ())), preferred_element_type=F32)


def _rope_lane_tables(seq, head_dim, scale):
    rot = head_dim // ROPE_FRACTION
    half = rot // 2
    pos = jnp.arange(seq, dtype=F32)
    inv = ROPE_THETA ** (-jnp.arange(0, rot, 2, dtype=F32) / rot)
    ang = pos[:, None] * inv[None, :]
    cos, sin = jnp.cos(ang), jnp.sin(ang)
    idx = jnp.arange(LANES) % head_dim
    cl = jnp.take(cos, idx % half, axis=1)
    sl = jnp.take(sin, idx % half, axis=1)
    c = jnp.where(idx < rot, cl, 1.0) * scale
    s_up = jnp.where(idx < half, -sl, 0.0) * scale
    s_dn = jnp.where((idx >= half) & (idx < rot), sl, 0.0) * scale
    return c.astype(F32), s_up.astype(F32), s_dn.astype(F32), half


def _rope_chunk(xc, c, s_up, s_dn, half):
    return (xc * c + pltpu.roll(xc, LANES - half, 1) * s_up + pltpu.roll(xc, half, 1) * s_dn)


def _norm_proj_kernel(x_ref, g_ref, w_ref, o_ref):
    h = _rms(x_ref[...], g_ref[...]).astype(BF16)
    o_ref[...] = _dot(h, w_ref[...]).astype(o_ref.dtype)


def _norm_proj(x2d, g, w_bf16, tm):
    m, d = x2d.shape
    n = w_bf16.shape[1]
    return pl.pallas_call(
        _norm_proj_kernel,
        grid=(m // tm,),
        in_specs=[pl.BlockSpec((tm, d), lambda i: (i, 0)),
                  pl.BlockSpec((1, d), lambda i: (0, 0)),
                  pl.BlockSpec((d, n), lambda i: (0, 0))],
        out_specs=pl.BlockSpec((tm, n), lambda i: (i, 0)),
        out_shape=jax.ShapeDtypeStruct((m, n), BF16),
        compiler_params=_cparams(("arbitrary",)),
        name="mem_kv_proj",
    )(x2d, g.reshape(1, d), w_bf16)


def _hyb_front_kernel(x_ref, g_ref, w_ref, cw_ref, cq_ref, suq_ref, sdq_ref, ck_ref, suk_ref, sdk_ref,
                      ya_ref, q_ref, k_ref, v_ref, cbuf, *, tm, half):
    s = pl.program_id(1)
    c = CONV_CH
    base = 3 * c
    nq = DIFF_HEADS * 2 * DIFF_DK
    rh = tm // FRONT_CHAINS
    starts = [ch * rh for ch in range(FRONT_CHAINS)]
    cw = cw_ref[...]

    @pl.when(s == 0)
    def _():
        cbuf[0:8, :] = jnp.zeros((8, c), F32)

    hs = [_rms(x_ref[0, r0:r0 + rh, :], g_ref[...]).astype(BF16) for r0 in starts]
    for r0, h in zip(starts, hs):
        gate_b = _dot(h, w_ref[:, 0:c])
        cu = _dot(h, w_ref[:, c:2 * c]) * _dot(h, w_ref[:, 2 * c:3 * c])
        cbuf[8 + r0:8 + r0 + rh, :] = cu
        conv = (cw[0:1, :] * cbuf[6 + r0:6 + r0 + rh, :] + cw[1:2, :] * cbuf[7 + r0:7 + r0 + rh, :] + cw[2:3, :] * cu)
        ya_ref[0, r0:r0 + rh, :] = (gate_b * conv).astype(BF16)
    cbuf[0:8, :] = cbuf[tm:tm + 8, :]

    for r0, h in zip(starts, hs):
        cq, suq, sdq = cq_ref[r0:r0 + rh, :], suq_ref[r0:r0 + rh, :], sdq_ref[r0:r0 + rh, :]
        pq = _dot(h, w_ref[:, base:base + nq])
        for j in range(nq // LANES):
            q_ref[0, r0:r0 + rh, j * LANES:(j + 1) * LANES] = _rope_chunk(
                pq[:, j * LANES:(j + 1) * LANES], cq, suq, sdq, half).astype(BF16)
    for r0, h in zip(starts, hs):
        ck, suk, sdk = ck_ref[r0:r0 + rh, :], suk_ref[r0:r0 + rh, :], sdk_ref[r0:r0 + rh, :]
        pk = _dot(h, w_ref[:, base + nq:base + 2 * nq])
        for j in range(nq // LANES):
            k_ref[0, r0:r0 + rh, j * LANES:(j + 1) * LANES] = _rope_chunk(
                pk[:, j * LANES:(j + 1) * LANES], ck, suk, sdk, half).astype(BF16)
    for r0, h in zip(starts, hs):
        v_ref[0, r0:r0 + rh, :] = _dot(h, w_ref[:, base + 2 * nq:]).astype(BF16)


def _hyb_front(x, g, w_in, conv_w, tabs_q, tabs_k, half):
    b, s, d = x.shape
    tm = ROW_TILE
    n = w_in.shape[1]
    nq = DIFF_HEADS * 2 * DIFF_DK
    nv = DIFF_HEADS * DIFF_DV
    tab_spec = pl.BlockSpec((tm, LANES), lambda bi, si: (si, 0))
    row = lambda width: pl.BlockSpec((1, tm, width), lambda bi, si: (bi, si, 0))
    return pl.pallas_call(
        functools.partial(_hyb_front_kernel, tm=tm, half=half),
        grid=(b, s // tm),
        in_specs=[row(d),
                  pl.BlockSpec((1, d), lambda bi, si: (0, 0)),
                  pl.BlockSpec((d, n), lambda bi, si: (0, 0)),
                  pl.BlockSpec((CONV_K, CONV_CH), lambda bi, si: (0, 0)),
                  tab_spec, tab_spec, tab_spec, tab_spec, tab_spec, tab_spec],
        out_specs=[row(CONV_CH), row(nq), row(nq), row(nv)],
        out_shape=[jax.ShapeDtypeStruct((b, s, CONV_CH), BF16),
                   jax.ShapeDtypeStruct((b, s, nq), BF16),
                   jax.ShapeDtypeStruct((b, s, nq), BF16),
                   jax.ShapeDtypeStruct((b, s, nv), BF16)],
        scratch_shapes=[pltpu.VMEM((tm + 8, CONV_CH), F32)],
        compiler_params=_cparams(("arbitrary", "arbitrary")),
        name="hyb_front",
    )(x, g.reshape(1, d), w_in, conv_w, *tabs_q, *tabs_k)


def _lane_fold(x, op):
    r = x[:, 0:LANES]
    for c in range(1, x.shape[1] // LANES):
        r = op(r, x[:, c * LANES:(c + 1) * LANES])
    return r


def _diff_attn_kernel(q_ref, k_ref, v_ref, lam_ref, g_ref, o_ref, sbuf, stat, acc, *, tq, hp, lambda_init):
    i = pl.program_id(2)
    lane = lax.broadcasted_iota(I32, (1, LANES), 1)
    qs = []
    for h in range(hp):
        q = q_ref[0, :, h * LANES:(h + 1) * LANES]
        zero = jnp.zeros_like(q)
        qs += [jnp.where(lane < DIFF_DK, q, zero), jnp.where(lane >= DIFF_DK, q, zero)]
    nc = 2 * hp

    hq = tq // 2
    diag = pl.multiple_of(i * tq, tq)

    def pass1(j, carry):
        for c in range(nc):
            h = c // 2
            kb = k_ref[0, pl.ds(pl.multiple_of(j * tq, tq), tq), h * LANES:(h + 1) * LANES]
            sc = _dot_nt(qs[c], kb)
            sbuf[c, j] = sc
            stat[c] = jnp.maximum(stat[c], _lane_fold(sc, jnp.maximum))
        return carry

    stat[...] = jnp.full(stat.shape, NEG_INF, F32)
    lax.fori_loop(0, i, pass1, 0)

    r = lax.broadcasted_iota(I32, (hq, hq), 0)
    cc = lax.broadcasted_iota(I32, (hq, hq), 1)
    tri = cc <= r
    for c in range(nc):
        h = c // 2
        k_lo = k_ref[0, pl.ds(diag, hq), h * LANES:(h + 1) * LANES]
        k_hi = k_ref[0, pl.ds(diag + hq, hq), h * LANES:(h + 1) * LANES]
        s_tl = jnp.where(tri, _dot_nt(qs[c][:hq], k_lo), NEG_INF)
        s_bl = _dot_nt(qs[c][hq:], k_lo)
        s_br = jnp.where(tri, _dot_nt(qs[c][hq:], k_hi), NEG_INF)
        sbuf[c, i, 0:hq, 0:hq] = s_tl
        sbuf[c, i, hq:tq, 0:hq] = s_bl
        sbuf[c, i, hq:tq, hq:tq] = s_br
        stat[c, 0:hq, :] = jnp.maximum(stat[c, 0:hq, :], _lane_fold(s_tl, jnp.maximum))
        stat[c, hq:tq, :] = jnp.maximum(stat[c, hq:tq, :],
                                        jnp.maximum(_lane_fold(s_bl, jnp.maximum), _lane_fold(s_br, jnp.maximum)))
    ms = [jnp.max(stat[c], axis=-1, keepdims=True) for c in range(nc)]

    ones = jnp.ones((tq, LANES), BF16)
    for h in range(hp):
        vb = v_ref[0, pl.ds(diag, tq), h * LANES:(h + 1) * LANES]
        v_ext = jnp.concatenate([vb, ones], axis=1)
        tops, bots = [], []
        for c in (2 * h, 2 * h + 1):
            tops.append(jnp.exp2(sbuf[c, i, 0:hq, 0:hq] - ms[c][:hq]))
            bots.append(jnp.exp2(sbuf[c, i, hq:tq, :] - ms[c][hq:]))
        top = _dot(jnp.concatenate(tops, axis=0).astype(BF16), v_ext[:hq])
        bot = _dot(jnp.concatenate(bots, axis=0).astype(BF16), v_ext)
        acc[h, 0:hq, :] = top[:hq]
        acc[h, hq:tq, :] = bot[:hq]
        acc[h, tq:tq + hq, :] = top[hq:]
        acc[h, tq + hq:2 * tq, :] = bot[hq:]

    def pass2(j, carry):
        for h in range(hp):
            vb = v_ref[0, pl.ds(pl.multiple_of(j * tq, tq), tq), h * LANES:(h + 1) * LANES]
            v_ext = jnp.concatenate([vb, ones], axis=1)
            p0 = jnp.exp2(sbuf[2 * h, j] - ms[2 * h])
            p1 = jnp.exp2(sbuf[2 * h + 1, j] - ms[2 * h + 1])
            acc[h] += _dot(jnp.concatenate([p0, p1], axis=0).astype(BF16), v_ext)
        return carry

    lax.fori_loop(0, i, pass2, 0)

    lf = lam_ref[...]
    lam = (jnp.exp(jnp.sum(lf[0:1] * lf[1:2], keepdims=True))
           - jnp.exp(jnp.sum(lf[2:3] * lf[3:4], keepdims=True)) + lambda_init)
    for h in range(hp):
        a0 = acc[h, 0:tq, :]
        a1 = acc[h, tq:2 * tq, :]
        o = a0[:, :LANES] / a0[:, LANES:] - lam * (a1[:, :LANES] / a1[:, LANES:])
        o_ref[0, :, h * LANES:(h + 1) * LANES] = (_rms(o, g_ref[...]) * (1.0 - lambda_init)).astype(BF16)


def _diff_attn(q, k, v, lam_vecs, subln_g, lambda_init):
    b, s, _ = q.shape
    tq = 512
    hp = 4
    blk = lambda bi, hi, qi: (bi, qi, hi)
    full = lambda bi, hi, qi: (bi, 0, hi)
    return pl.pallas_call(
        functools.partial(_diff_attn_kernel, tq=tq, hp=hp, lambda_init=lambda_init),
        grid=(b, DIFF_HEADS // hp, s // tq),
        in_specs=[pl.BlockSpec((1, tq, hp * LANES), blk),
                  pl.BlockSpec((1, s, hp * LANES), full),
                  pl.BlockSpec((1, s, hp * LANES), full),
                  pl.BlockSpec((4, DIFF_DK), lambda bi, hi, qi: (0, 0)),
                  pl.BlockSpec((1, DIFF_DV), lambda bi, hi, qi: (0, 0))],
        out_specs=pl.BlockSpec((1, tq, hp * LANES), blk),
        out_shape=jax.ShapeDtypeStruct((b, s, DIFF_HEADS * DIFF_DV), BF16),
        scratch_shapes=[pltpu.VMEM((2 * hp, s // tq, tq, tq), F32),
                        pltpu.VMEM((2 * hp, tq, LANES), F32),
                        pltpu.VMEM((hp, 2 * tq, DIFF_DV + LANES), F32)],
        compiler_params=_cparams(("arbitrary", "arbitrary", "arbitrary")),
        name="diff_attn",
    )(q, k, v, lam_vecs, subln_g.reshape(1, DIFF_DV))


def _swa_front_kernel(x_ref, g_ref, w_ref, b_ref, cq_ref, suq_ref, sdq_ref, ck_ref, suk_ref, sdk_ref,
                      q_ref, kv_ref, *, half):
    nq = SWA_Q_HEADS * SWA_HEAD_DIM
    nkv = SWA_KV_HEADS * SWA_HEAD_DIM
    tm = x_ref.shape[1]
    rh = tm // FRONT_CHAINS
    starts = [ch * rh for ch in range(FRONT_CHAINS)]
    hs = [_rms(x_ref[0, r0:r0 + rh, :], g_ref[...]).astype(BF16) for r0 in starts]
    for r0, h in zip(starts, hs):
        cq, suq, sdq = cq_ref[r0:r0 + rh, :], suq_ref[r0:r0 + rh, :], sdq_ref[r0:r0 + rh, :]
        pq = _dot(h, w_ref[:, 0:nq]) + b_ref[:, 0:nq]
        for j in range(nq // LANES):
            lo = j * LANES
            q_ref[0, r0:r0 + rh, lo:lo + LANES] = _rope_chunk(pq[:, lo:lo + LANES], cq, suq, sdq, half).astype(BF16)
    for r0, h in zip(starts, hs):
        ck, suk, sdk = ck_ref[r0:r0 + rh, :], suk_ref[r0:r0 + rh, :], sdk_ref[r0:r0 + rh, :]
        pkv = _dot(h, w_ref[:, nq:]) + b_ref[:, nq:]
        for j in range(nkv // LANES):
            lo = j * LANES
            kv_ref[0, r0:r0 + rh, lo:lo + LANES] = _rope_chunk(pkv[:, lo:lo + LANES], ck, suk, sdk, half).astype(BF16)
        kv_ref[0, r0:r0 + rh, nkv:] = pkv[:, nkv:].astype(BF16)


def _swa_front(x, g, w_qkv, b_qkv, tabs_q, tabs_k, half):
    b, s, d = x.shape
    tm = ROW_TILE
    n = w_qkv.shape[1]
    nq = SWA_Q_HEADS * SWA_HEAD_DIM
    nkv = SWA_KV_HEADS * SWA_HEAD_DIM
    tab_spec = pl.BlockSpec((tm, LANES), lambda bi, si: (si, 0))
    row = lambda width: pl.BlockSpec((1, tm, width), lambda bi, si: (bi, si, 0))
    return pl.pallas_call(
        functools.partial(_swa_front_kernel, half=half),
        grid=(b, s // tm),
        in_specs=[row(d),
                  pl.BlockSpec((1, d), lambda bi, si: (0, 0)),
                  pl.BlockSpec((d, n), lambda bi, si: (0, 0)),
                  pl.BlockSpec((1, n), lambda bi, si: (0, 0)),
                  tab_spec, tab_spec, tab_spec, tab_spec, tab_spec, tab_spec],
        out_specs=[row(nq), row(2 * nkv)],
        out_shape=[jax.ShapeDtypeStruct((b, s, nq), BF16),
                   jax.ShapeDtypeStruct((b, s, 2 * nkv), BF16)],
        compiler_params=_cparams(("arbitrary", "arbitrary")),
        name="swa_front",
    )(x, g.reshape(1, d), w_qkv, b_qkv.reshape(1, n), *tabs_q, *tabs_k)


def _swa_head_order():
    g_sz = SWA_Q_HEADS // SWA_KV_HEADS
    order = []
    for slab in range(SWA_Q_HEADS // 2):
        pair, j = slab // g_sz, slab % g_sz
        order += [(2 * pair) * g_sz + j, (2 * pair + 1) * g_sz + j]
    return order


def _swa_attn_kernel(sink_ref, q_ref, kvp_ref, kvc_ref, o_ref, *, nblk):
    i = pl.program_id(1)
    hd = SWA_HEAD_DIM
    nkv = SWA_KV_HEADS * hd
    g_sz = SWA_Q_HEADS // SWA_KV_HEADS
    kv = jnp.concatenate([kvp_ref[0], kvc_ref[0]], axis=0)
    r = lax.broadcasted_iota(I32, (BLOCK, 2 * BLOCK), 0)
    c = lax.broadcasted_iota(I32, (BLOCK, 2 * BLOCK), 1)
    rel = c - BLOCK - r
    in_win = (rel <= 0) & (rel > -SWA_WINDOW)
    lane = lax.broadcasted_iota(I32, (1, LANES), 1)
    lo_half = lane < hd
    ones = jnp.ones((2 * BLOCK, LANES), BF16)
    for n in range(nblk):
        mask = in_win & ((c >= BLOCK) | (i > 0)) if n == 0 else in_win
        keys = kv[n * BLOCK:(n + 2) * BLOCK]
        for pair in range(SWA_KV_HEADS // 2):
            k2 = keys[:, pair * LANES:(pair + 1) * LANES]
            v2 = keys[:, nkv + pair * LANES:nkv + (pair + 1) * LANES]
            v_ext = jnp.concatenate([v2, ones], axis=1)
            pieces = []
            for j in range(g_sz):
                slab = pair * g_sz + j
                qs = q_ref[0, n * BLOCK:(n + 1) * BLOCK, slab * LANES:(slab + 1) * LANES]
                zero = jnp.zeros_like(qs)
                pieces += [jnp.where(lo_half, qs, zero), jnp.where(lo_half, zero, qs)]
            sc = _dot_nt(jnp.concatenate(pieces, axis=0), k2)
            probs, tails = [], []
            for pc in range(2 * g_sz):
                scp = jnp.where(mask, sc[pc * BLOCK:(pc + 1) * BLOCK], NEG_INF)
                sink = sink_ref[2 * g_sz * pair + pc] * LOG2E
                m = jnp.maximum(jnp.max(scp, axis=-1, keepdims=True), sink)
                probs.append(jnp.exp2(scp - m).astype(BF16))
                tails.append(jnp.exp2(sink - m))
            pv = _dot(jnp.concatenate(probs, axis=0), v_ext)
            for j in range(g_sz):
                slab = pair * g_sz + j
                halves = []
                for hf in range(2):
                    pc = 2 * j + hf
                    blk = pv[pc * BLOCK:(pc + 1) * BLOCK]
                    halves.append(blk[:, :LANES] / (blk[:, LANES:] + tails[pc]))
                o_ref[0, n * BLOCK:(n + 1) * BLOCK, slab * LANES:(slab + 1) * LANES] = (
                    jnp.where(lo_half, halves[0], halves[1]).astype(BF16))


def _swa_attn(q, kv, sinks):
    b, s, nq = q.shape
    nblk = 4
    tq = nblk * BLOCK
    return pl.pallas_call(
        functools.partial(_swa_attn_kernel, nblk=nblk),
        grid_spec=pltpu.PrefetchScalarGridSpec(
            num_scalar_prefetch=1,
            grid=(b, s // tq),
            in_specs=[pl.BlockSpec((1, tq, nq), lambda bi, ni, sk: (bi, ni, 0)),
                      pl.BlockSpec((1, BLOCK, kv.shape[2]), lambda bi, ni, sk: (bi, jnp.maximum(ni * nblk - 1, 0), 0)),
                      pl.BlockSpec((1, tq, kv.shape[2]), lambda bi, ni, sk: (bi, ni, 0))],
            out_specs=pl.BlockSpec((1, tq, nq), lambda bi, ni, sk: (bi, ni, 0))),
        out_shape=jax.ShapeDtypeStruct((b, s, nq), BF16),
        compiler_params=_cparams(("arbitrary", "arbitrary")),
        name="swa_attn",
    )(sinks, q, kv, kv)


def _post_kernel(*refs, tm, n_a, has_bias):
    x_ref = refs[0]
    a_refs = refs[1:1 + n_a]
    k = 1 + n_a
    wo_refs = refs[k:k + n_a]
    k += n_a
    if has_bias:
        bo_ref = refs[k]
        k += 1
    (gx_ref, wq_ref, mkv_ref, wxo_ref, gf_ref, wr_ref, br_ref,
     x2_ref, h2_ref, mi_ref, mw_ref, cnt_ref, cnt_acc, cnt_col) = refs[k:]
    first = (pl.program_id(0) == 0) & (pl.program_id(1) == 0)

    @pl.when(first)
    def _():
        cnt_acc[...] = jnp.zeros_like(cnt_acc)
        cnt_col[...] = jnp.zeros_like(cnt_col)

    xw = XATTN_HEADS * XATTN_HEAD_DIM
    ones = jnp.ones((mkv_ref.shape[1], LANES), BF16)

    rh = tm // POST_CHAINS

    def out_proj(r0):
        acc = _dot(a_refs[0][0, r0:r0 + rh, :], wo_refs[0][...])
        for a_ref, w_ref in zip(a_refs[1:], wo_refs[1:]):
            acc = acc + _dot(a_ref[0, r0:r0 + rh, :], w_ref[...])
        if has_bias:
            acc = acc + bo_ref[...]
        return x_ref[0, r0:r0 + rh, :] + acc

    def q_proj(x1):
        hx = _rms(x1, gx_ref[...]).astype(BF16)
        return (_dot(hx, wq_ref[...]) * (XATTN_HEAD_DIM ** -0.5 * LOG2E)).astype(BF16)

    def mem_attn(qx):
        outs = []
        for hh in range(XATTN_HEADS):
            lo = hh * XATTN_HEAD_DIM
            mk = mkv_ref[0, :, lo:lo + XATTN_HEAD_DIM]
            mv = jnp.concatenate([mkv_ref[0, :, xw + lo:xw + lo + XATTN_HEAD_DIM], ones], axis=1)
            sc = _dot_nt(qx[:, lo:lo + XATTN_HEAD_DIM], mk)
            m = jnp.max(sc, axis=-1, keepdims=True)
            pv = _dot(jnp.exp2(sc - m).astype(BF16), mv)
            outs.append((pv[:, :LANES] / pv[:, LANES:]).astype(BF16))
        return jnp.concatenate(outs, axis=-1)

    def o_proj(r0, x1, ox):
        x2 = x1 + _dot(ox, wxo_ref[...])
        x2_ref[0, r0:r0 + rh, :] = x2
        return x2

    def router(r0, x2):
        h2 = _rms(x2, gf_ref[...])
        _to_token_tiles(h2_ref, h2, rh, r0)
        return _dot(h2.astype(BF16), wr_ref[...]) + br_ref[...]

    starts = [c * rh for c in range(POST_CHAINS)]
    x1s = [out_proj(r0) for r0 in starts]
    qxs = [q_proj(x1) for x1 in x1s]
    oxs = [mem_attn(qx) for qx in qxs]
    x2s = [o_proj(r0, x1, ox) for r0, x1, ox in zip(starts, x1s, oxs)]
    logits = jnp.concatenate([router(r0, x2) for r0, x2 in zip(starts, x2s)], axis=0)

    lt = logits.T
    ex = lt[0:N_EXPERTS]
    grp = lt[N_EXPERTS:N_EXPERTS + SUBLANES]
    grow = lax.broadcasted_iota(I32, grp.shape, 0).astype(F32)
    lg = jnp.where(grow < N_GROUPS, grp, NEG_INF)
    mg = jnp.max(lg, axis=0, keepdims=True)
    g_sel = jnp.min(jnp.where(lg == mg, grow, float(SUBLANES)), axis=0, keepdims=True)
    p_g = 1.0 / jnp.sum(jnp.exp(lg - mg), axis=0, keepdims=True)
    erow = lax.broadcasted_iota(I32, ex.shape, 0).astype(F32)
    e_lo = g_sel * EXPERTS_PER_GROUP
    le = jnp.where((erow >= e_lo) & (erow < e_lo + EXPERTS_PER_GROUP), ex, NEG_INF)
    big = float(N_EXPERTS)
    m1 = jnp.max(le, axis=0, keepdims=True)
    i1 = jnp.min(jnp.where(le == m1, erow, big), axis=0, keepdims=True)
    le2 = jnp.where(erow == i1, NEG_INF, le)
    m2 = jnp.max(le2, axis=0, keepdims=True)
    i2 = jnp.min(jnp.where(le2 == m2, erow, big), axis=0, keepdims=True)
    t = jnp.exp(m2 - m1)
    w1 = p_g / (1.0 + t)
    w2 = p_g * t / (1.0 + t)

    oh1 = erow == i1
    oh2 = erow == i2
    oh = jnp.where(oh1 | oh2, 1.0, 0.0)
    rr = lax.broadcasted_iota(I32, (tm, tm), 0)
    cc = lax.broadcasted_iota(I32, (tm, tm), 1)
    earlier = jnp.where(rr < cc, 1.0, 0.0).astype(BF16)
    before = _dot(oh.astype(BF16), earlier) + cnt_col[...]
    r1 = jnp.sum(jnp.where(oh1, before, 0.0), axis=0, keepdims=True)
    r2 = jnp.sum(jnp.where(oh2, before, 0.0), axis=0, keepdims=True)
    cnt_col[...] = cnt_col[...] + jnp.sum(oh, axis=1, keepdims=True)
    oh_pad = jnp.concatenate([oh, jnp.zeros((LANES - N_EXPERTS, tm), F32)], axis=0).astype(BF16)
    cnt_new = cnt_acc[...] + _dot_nt(jnp.ones((SUBLANES, tm), BF16), oh_pad)[0:1]
    cnt_acc[...] = cnt_new
    cnt_ref[...] = cnt_new.astype(I32)

    zrow = jnp.zeros((1, tm), F32)
    mi_ref[...] = jnp.concatenate([i1, i2, r1, r2, zrow, zrow, zrow, zrow], axis=0).astype(I32)
    wt = jnp.concatenate([w1, w2, jnp.zeros((LANES - TOP_K, tm), F32)], axis=0)
    mw_ref[...] = wt.T


def _post_mixer(x, a_list, wo_list, b_out, g_x, w_q, mkv, w_xo, g_f, w_rt, b_rt):
    b, s, d = x.shape
    tm = ROW_TILE
    ns = s // tm
    t = b * s
    n_a = len(a_list)
    has_bias = b_out is not None
    const2 = lambda bi, si: (0, 0)
    row = lambda width: pl.BlockSpec((1, tm, width), lambda bi, si: (bi, si, 0))
    in_specs = [row(d)] + [row(a.shape[2]) for a in a_list]
    in_specs += [pl.BlockSpec(w.shape, const2) for w in wo_list]
    args = [x, *a_list, *wo_list]
    if has_bias:
        in_specs.append(pl.BlockSpec((1, d), const2))
        args.append(b_out.reshape(1, d))
    xw = w_q.shape[1]
    in_specs += [pl.BlockSpec((1, d), const2),
                 pl.BlockSpec((d, xw), const2),
                 pl.BlockSpec((1, mkv.shape[1], mkv.shape[2]), lambda bi, si: (bi, 0, 0)),
                 pl.BlockSpec((xw, d), const2),
                 pl.BlockSpec((1, d), const2),
                 pl.BlockSpec((d, LANES), const2),
                 pl.BlockSpec((1, LANES), const2)]
    args += [g_x.reshape(1, d), w_q, mkv, w_xo, g_f.reshape(1, d), w_rt, b_rt]
    out_specs = [row(d),
                 pl.BlockSpec((tm * TOKEN_ROWS, LANES), lambda bi, si: (bi * ns + si, 0)),
                 pl.BlockSpec((8, tm), lambda bi, si: (0, bi * ns + si)),
                 pl.BlockSpec((tm, LANES), lambda bi, si: (bi * ns + si, 0)),
                 pl.BlockSpec((1, LANES), const2)]
    out_shape = [jax.ShapeDtypeStruct((b, s, d), F32),
                 jax.ShapeDtypeStruct((t * TOKEN_ROWS, LANES), U32),
                 jax.ShapeDtypeStruct((8, t), I32),
                 jax.ShapeDtypeStruct((t, LANES), F32),
                 jax.ShapeDtypeStruct((1, LANES), I32)]
    return pl.pallas_call(
        functools.partial(_post_kernel, tm=tm, n_a=n_a, has_bias=has_bias),
        grid=(b, ns),
        in_specs=in_specs,
        out_specs=out_specs,
        out_shape=out_shape,
        scratch_shapes=[pltpu.VMEM((1, LANES), F32), pltpu.VMEM((N_EXPERTS, 1), F32)],
        compiler_params=_cparams(("arbitrary", "arbitrary")),
        name="post_mixer",
    )(*args)


def _to_token_tiles(dst_ref, val, rows, tok0=0):
    half = TOKEN_ROWS * LANES
    for j in range(TOKEN_ROWS):
        hi = val[:, j * LANES:(j + 1) * LANES].astype(BF16).astype(F32)
        lo = val[:, half + j * LANES:half + (j + 1) * LANES].astype(BF16).astype(F32)
        word = lax.bitcast_convert_type(hi, U32) | (lax.bitcast_convert_type(lo, U32) >> 16)
        dst_ref[pl.ds(tok0 * TOKEN_ROWS + j, rows, stride=TOKEN_ROWS), :] = word


def _from_token_tiles(src_ref, rows, tok0=0):
    his, los = [], []
    for j in range(TOKEN_ROWS):
        word = src_ref[pl.ds(tok0 * TOKEN_ROWS + j, rows, stride=TOKEN_ROWS), :]
        his.append(lax.bitcast_convert_type(word & jnp.uint32(0xFFFF0000), F32))
        los.append(lax.bitcast_convert_type(word << 16, F32))
    return jnp.concatenate(his + los, axis=1)


def _token_slice(ref, tok):
    return ref.at[pl.ds(pl.multiple_of(tok * TOKEN_ROWS, TOKEN_ROWS), TOKEN_ROWS)]


def _pos_kernel(off_ref, mi_ref, pos_ref):
    e = mi_ref[0:TOP_K, :]
    pos = mi_ref[TOP_K:2 * TOP_K, :]
    for j in range(N_EXPERTS):
        pos = pos + jnp.where(e == j, off_ref[j], 0)
    pos_ref[...] = pos


def _sorted_positions(off, meta_i):
    t = meta_i.shape[1]
    return pl.pallas_call(
        _pos_kernel,
        grid_spec=pltpu.PrefetchScalarGridSpec(
            num_scalar_prefetch=1,
            grid=(1,),
            in_specs=[pl.BlockSpec((8, t), lambda i, off: (0, 0))],
            out_specs=pl.BlockSpec((TOP_K, t), lambda i, off: (0, 0))),
        out_shape=jax.ShapeDtypeStruct((TOP_K, t), I32),
        compiler_params=_cparams(("arbitrary",)),
        name="moe_positions",
    )(off, meta_i)


def _tile_major(pos, tm):
    t = pos.shape[1]
    return pos.reshape(TOP_K, t // tm, tm).transpose(1, 0, 2).reshape(-1)


def _dispatch_kernel(pos_ref, h_ref, xs_ref, sem, *, tm):
    def issue(g, _):
        for u in range(DMA_UNROLL):
            r = g * DMA_UNROLL + u
            for kk in range(TOP_K):
                pos = pos_ref[kk * tm + r]
                pltpu.make_async_copy(_token_slice(h_ref, r), _token_slice(xs_ref, pos), sem).start(priority=kk)
        return 0

    lax.fori_loop(0, tm // DMA_UNROLL, issue, 0)
    for _ in range(TOP_K):
        pltpu.make_async_copy(h_ref, xs_ref.at[pl.ds(0, tm * TOKEN_ROWS)], sem).wait()


def _dispatch(pos, h2t, n_rows):
    tm = DISPATCH_TILE
    t = h2t.shape[0] // TOKEN_ROWS
    return pl.pallas_call(
        functools.partial(_dispatch_kernel, tm=tm),
        grid=(t // tm,),
        in_specs=[pl.BlockSpec((TOP_K * tm,), lambda i: (i,), memory_space=pltpu.SMEM),
                  pl.BlockSpec((tm * TOKEN_ROWS, LANES), lambda i: (i, 0))],
        out_specs=pl.BlockSpec(memory_space=pl.ANY),
        scratch_shapes=[pltpu.SemaphoreType.DMA],
        out_shape=jax.ShapeDtypeStruct((n_rows * TOKEN_ROWS, LANES), U32),
        compiler_params=_cparams(("arbitrary",)),
        name="moe_dispatch",
    )(_tile_major(pos, tm), h2t)


def _expert_kernel(tblk_ref, texp_ref, tn_ref, tnxt_ref, tpar_ref, xs_ref, wgu_hbm, wdn_hbm, ys_ref,
                   wgu_bf, wdn_bf, wgu_f, wdn_f, sems, *, tm, layer):
    i = pl.program_id(0)
    n_valid = tn_ref[i]
    expert = texp_ref[i]
    slot = tpar_ref[i]
    new_expert = (i == 0) | (expert != texp_ref[jnp.maximum(i - 1, 0)])

    def weight_copies(ex, sl):
        return (pltpu.make_async_copy(wgu_hbm.at[layer, ex], wgu_f.at[sl], sems.at[sl]),
                pltpu.make_async_copy(wdn_hbm.at[layer, ex], wdn_f.at[sl], sems.at[sl]))

    @pl.when(i == 0)
    def _():
        for cp in weight_copies(expert, slot):
            cp.start()

    @pl.when(new_expert)
    def _():
        @pl.when(tnxt_ref[i] >= 0)
        def _():
            for cp in weight_copies(tnxt_ref[i], 1 - slot):
                cp.start()

        for cp in weight_copies(expert, slot):
            cp.wait()
        wgu_bf[...] = wgu_f[slot].astype(BF16)
        wdn_bf[...] = wdn_f[slot].astype(BF16)

    @pl.when(n_valid > 0)
    def _():
        rh = tm // EXPERT_CHAINS
        starts = [c * rh for c in range(EXPERT_CHAINS)]
        row = lax.broadcasted_iota(I32, (rh, 1), 0)
        gus = []
        for r0 in starts:
            x = jnp.where(row + r0 < n_valid, _from_token_tiles(xs_ref, rh, r0), 0.0).astype(BF16)
            gus.append(_dot(x, wgu_bf[...]))
        ys = []
        for gu in gus:
            hid = (gu[:, :EXPERT_HIDDEN] / (1.0 + jnp.exp(-gu[:, :EXPERT_HIDDEN])) * gu[:, EXPERT_HIDDEN:]).astype(BF16)
            ys.append(_dot(hid, wdn_bf[...]))
        for r0, y in zip(starts, ys):
            _to_token_tiles(ys_ref, y, rh, r0)


def _experts(plan, xs, w_gu, w_dn, layer):
    tm = EXPERT_TILE
    nt = xs.shape[0] // (tm * TOKEN_ROWS)
    gu_shape, dn_shape = w_gu.shape[2:], w_dn.shape[2:]
    rows = lambda i, tb, te, tn, tx, tp: (tb[i], 0)
    return pl.pallas_call(
        functools.partial(_expert_kernel, tm=tm, layer=layer),
        grid_spec=pltpu.PrefetchScalarGridSpec(
            num_scalar_prefetch=5,
            grid=(nt,),
            in_specs=[pl.BlockSpec((tm * TOKEN_ROWS, LANES), rows),
                      pl.BlockSpec(memory_space=pl.ANY),
                      pl.BlockSpec(memory_space=pl.ANY)],
            out_specs=pl.BlockSpec((tm * TOKEN_ROWS, LANES), rows),
            scratch_shapes=[pltpu.VMEM(gu_shape, BF16), pltpu.VMEM(dn_shape, BF16),
                            pltpu.VMEM((2,) + gu_shape, F32), pltpu.VMEM((2,) + dn_shape, F32),
                            pltpu.SemaphoreType.DMA((2,))]),
        out_shape=jax.ShapeDtypeStruct(xs.shape, U32),
        compiler_params=_cparams(("arbitrary",)),
        name="moe_experts",
    )(*plan, xs, w_gu, w_dn)


def _moe_combine_tile(pos_cur, pos_nxt, mw_ref, x, ys_ref, ybuf, sems, i, n, tm):
    def gather(pos_ref, slot):
        def issue(g, _):
            for u in range(DMA_UNROLL):
                r = g * DMA_UNROLL + u
                for kk in range(TOP_K):
                    pos = pos_ref[kk * tm + r]
                    pltpu.make_async_copy(_token_slice(ys_ref, pos), _token_slice(ybuf.at[slot, kk], r),
                                          sems.at[slot]).start(priority=kk)
            return 0

        lax.fori_loop(0, tm // DMA_UNROLL, issue, 0)

    @pl.when(i == 0)
    def _():
        gather(pos_cur, 0)

    @pl.when(i + 1 < n)
    def _():
        gather(pos_nxt, (i + 1) % 2)

    slot = i % 2
    for kk in range(TOP_K):
        pltpu.make_async_copy(ys_ref.at[pl.ds(0, tm * TOKEN_ROWS)], ybuf.at[slot, kk], sems.at[slot]).wait()
    mw = mw_ref[...]
    return (x + mw[:, 0:1] * _from_token_tiles(ybuf.at[slot, 0], tm)
            + mw[:, 1:2] * _from_token_tiles(ybuf.at[slot, 1], tm))


def _combine_kernel(pos_cur, pos_nxt, mw_ref, x_ref, ys_ref, *rest, tm, final):
    if final:
        g_ref, o_ref, ybuf, sems = rest
    else:
        o_ref, ybuf, sems = rest
    x3 = _moe_combine_tile(pos_cur, pos_nxt, mw_ref, x_ref[...], ys_ref, ybuf, sems,
                           pl.program_id(0), pl.num_programs(0), tm)
    if final:
        x3 = _rms(x3, g_ref[...])
    o_ref[...] = x3


def _combine(moe, g_final):
    pos, meta_w, x2, ys = moe
    d = x2.shape[-1]
    x2d = x2.reshape(-1, d)
    t = x2d.shape[0]
    tm = COMBINE_TILE
    nt = t // tm
    final = g_final is not None
    in_specs = [pl.BlockSpec((TOP_K * tm,), lambda i: (i,), memory_space=pltpu.SMEM),
                pl.BlockSpec((TOP_K * tm,), lambda i: (jnp.minimum(i + 1, nt - 1),), memory_space=pltpu.SMEM),
                pl.BlockSpec((tm, LANES), lambda i: (i, 0)),
                pl.BlockSpec((tm, d), lambda i: (i, 0)),
                pl.BlockSpec(memory_space=pl.ANY)]
    pos_flat = _tile_major(pos, tm)
    args = [pos_flat, pos_flat, meta_w, x2d, ys]
    if final:
        in_specs.append(pl.BlockSpec((1, d), lambda i: (0, 0)))
        args.append(g_final.reshape(1, d))
    return pl.pallas_call(
        functools.partial(_combine_kernel, tm=tm, final=final),
        grid=(nt,),
        in_specs=in_specs,
        out_specs=pl.BlockSpec((tm, d), lambda i: (i, 0)),
        scratch_shapes=[pltpu.VMEM((2, TOP_K, tm * TOKEN_ROWS, LANES), U32), pltpu.SemaphoreType.DMA((2,))],
        out_shape=jax.ShapeDtypeStruct((t, d), F32),
        compiler_params=_cparams(("arbitrary",)),
        name="moe_combine",
    )(*args)


def _plan_kernel(cnt_ref, off_ref, blk_ref, exp_ref, nv_ref, nxt_ref, par_ref, *, tm, n_tiles):
    shift = tm.bit_length() - 1

    def clear(i, _):
        nxt_ref[i] = -1
        return 0

    lax.fori_loop(0, n_tiles, clear, 0)

    def per_expert(e, carry):
        row0, run, prev_first = carry
        c = cnt_ref[0, e]
        ntile = lax.shift_right_logical(c + (tm - 1), shift)
        off_ref[e] = row0
        t0 = lax.shift_right_logical(row0, shift)

        def fill(j, _):
            blk_ref[t0 + j] = t0 + j
            exp_ref[t0 + j] = e
            nv_ref[t0 + j] = jnp.minimum(c - j * tm, tm)
            par_ref[t0 + j] = run & 1
            return 0

        lax.fori_loop(0, ntile, fill, 0)
        has = ntile > 0

        @pl.when(has & (prev_first >= 0))
        def _():
            nxt_ref[prev_first] = e

        return (row0 + lax.shift_left(ntile, shift), run + has.astype(I32), jnp.where(has, t0, prev_first))

    total, _, _ = lax.fori_loop(0, N_EXPERTS, per_expert, (jnp.int32(0), jnp.int32(0), jnp.int32(-1)))
    used = lax.shift_right_logical(total, shift)
    last = jnp.maximum(used - 1, 0)

    def tail(i, _):
        blk_ref[i] = last
        exp_ref[i] = exp_ref[last]
        nv_ref[i] = 0
        par_ref[i] = par_ref[last]
        return 0

    lax.fori_loop(used, n_tiles, tail, 0)


def _moe_plan(counts, n_tiles):
    tm = EXPERT_TILE
    assert tm & (tm - 1) == 0
    smem = pl.BlockSpec(memory_space=pltpu.SMEM)
    return pl.pallas_call(
        functools.partial(_plan_kernel, tm=tm, n_tiles=n_tiles),
        in_specs=[smem],
        out_specs=[smem] * 6,
        out_shape=[jax.ShapeDtypeStruct((N_EXPERTS,), I32)] + [jax.ShapeDtypeStruct((n_tiles,), I32)] * 5,
        name="moe_plan",
    )(counts)


def _moe(x2, h2t, meta_i, meta_w, counts, w_gu, w_dn, layer):
    b, s, d = x2.shape
    assert d == 2 * TOKEN_ROWS * LANES
    n_rows = b * s * TOP_K + N_EXPERTS * EXPERT_TILE
    off, *tile_plan = _moe_plan(counts, n_rows // EXPERT_TILE)
    pos = _sorted_positions(off, meta_i)
    xs = _dispatch(pos, h2t, n_rows)
    ys = _experts(tile_plan, xs, w_gu, w_dn, layer)
    return pos, meta_w, x2, ys


def _router_weights(w_group, b_group, w_router, b_router):
    d = w_group.shape[0]
    pad = LANES - N_EXPERTS - N_GROUPS
    w = jnp.concatenate([w_router, w_group, jnp.zeros((d, pad), F32)], axis=1)
    bias = jnp.concatenate([b_router, b_group, jnp.zeros((pad,), F32)]).reshape(1, LANES)
    return w.astype(BF16), bias


def kernel(x, mem, mem_norm, mem_w_kv, norm_mix, norm_xattn, norm_ffn, hyb_w_in, hyb_conv_w, diff_lambda, diff_subln, hyb_w_out, swa_w_qkv, swa_b_qkv, swa_sinks, swa_w_out, swa_b_out, xattn_w_q, xattn_w_o, moe_w_group, moe_b_group, moe_w_router, moe_b_router, moe_w_gate_up, moe_w_down, final_norm):
    b, s, d = x.shape
    m = mem.shape[1]
    depth = norm_mix.shape[0]
    mkv = _norm_proj(mem.reshape(b * m, d), mem_norm, mem_w_kv.astype(BF16), m).reshape(b, m, -1)

    scale = DIFF_DK ** -0.5 * LOG2E
    cq, suq, sdq, half = _rope_lane_tables(s, DIFF_DK, scale)
    ck, suk, sdk, _ = _rope_lane_tables(s, DIFF_DK, 1.0)
    tabs_q, tabs_k = (cq, suq, sdq), (ck, suk, sdk)

    moe = None
    for l in range(depth):
        if moe is not None:
            x = _combine(moe, None).reshape(b, s, d)
        if l % 2 == 0:
            e = l // 2
            lambda_init = 0.8 - 0.6 * math.exp(-0.3 * l)
            ya, q, k, v = _hyb_front(x, norm_mix[l], hyb_w_in[e].astype(BF16), hyb_conv_w[e], tabs_q, tabs_k, half)
            o = _diff_attn(q, k, v, diff_lambda[e], diff_subln[e], lambda_init)
            w_out = hyb_w_out[e].astype(BF16)
            a_list, wo_list, b_out = [ya, o], [w_out[:CONV_CH], w_out[CONV_CH:]], None
        else:
            e = l // 2
            order = _swa_head_order()
            hd, nq = SWA_HEAD_DIM, SWA_Q_HEADS * SWA_HEAD_DIM
            heads = lambda a, axis: [lax.slice_in_dim(a, h * hd, (h + 1) * hd, axis=axis) for h in order]
            w_all = swa_w_qkv[e].astype(BF16)
            w_qkv = jnp.concatenate(heads(w_all, 1) + [w_all[:, nq:]], axis=1)
            b_qkv = jnp.concatenate(heads(swa_b_qkv[e], 0) + [swa_b_qkv[e][nq:]])
            sinks = jnp.stack([swa_sinks[e][h] for h in order])
            q, kv = _swa_front(x, norm_mix[l], w_qkv, b_qkv, tabs_q, tabs_k, half)
            o = _swa_attn(q, kv, sinks)
            w_out = jnp.concatenate(heads(swa_w_out[e].astype(BF16), 0), axis=0)
            a_list, wo_list, b_out = [o], [w_out], swa_b_out[e]
        w_rt, b_rt = _router_weights(moe_w_group[l], moe_b_group[l], moe_w_router[l], moe_b_router[l])
        x2, h2, meta_i, meta_w, counts = _post_mixer(
            x, a_list, wo_list, b_out, norm_xattn[l], xattn_w_q[l].astype(BF16), mkv,
            xattn_w_o[l].astype(BF16), norm_ffn[l], w_rt, b_rt)
        moe = _moe(x2, h2, meta_i, meta_w, counts, moe_w_gate_up, moe_w_down, l)
    return _combine(moe, final_norm).reshape(b, s, d)
```

```python
import functools
import math

import jax
import jax.numpy as jnp
from jax import lax
from jax.experimental import pallas as pl
from jax.experimental.pallas import tpu as pltpu

F32 = jnp.float32
BF16 = jnp.bfloat16
I32 = jnp.int32
U32 = jnp.uint32

EPS = 1e-6
LANES = 128
SUBLANES = 8
TOKEN_ROWS = 4
DMA_UNROLL = 8
VMEM_LIMIT = 56 * 1024 * 1024

ROPE_THETA = 500000.0
ROPE_FRACTION = 4
BLOCK = 128
CONV_CH = 512
CONV_K = 3
DIFF_HEADS = 4
DIFF_DK = 64
DIFF_DV = 128
SWA_Q_HEADS = 16
SWA_KV_HEADS = 4
SWA_HEAD_DIM = 64
SWA_WINDOW = 128
XATTN_HEADS = 4
XATTN_HEAD_DIM = 128
N_GROUPS = 4
EXPERTS_PER_GROUP = 8
N_EXPERTS = N_GROUPS * EXPERTS_PER_GROUP
TOP_K = 2
EXPERT_HIDDEN = 512

ROW_TILE = 1024
EXPERT_TILE = 512
DISPATCH_TILE = 2048
COMBINE_TILE = 512
POST_CHAINS = 2
EXPERT_CHAINS = 2
FRONT_CHAINS = 2
NEG_INF = float("-inf")
LOG2E = math.log2(math.e)


def _cparams(sem):
    return pltpu.CompilerParams(dimension_semantics=sem, vmem_limit_bytes=VMEM_LIMIT)


def _rms(x, g):
    return x * lax.rsqrt(jnp.mean(x * x, axis=-1, keepdims=True) + EPS) * g


def _dot(a, b):
    return jnp.dot(a, b, preferred_element_type=F32)


def _dot_nt(a, b):
    return lax.dot_general(a, b, (((1,), (1,)), ((), ())), preferred_element_type=F32)


def _rope_lane_tables(seq, head_dim, scale):
    rot = head_dim // ROPE_FRACTION
    half = rot // 2
    pos = jnp.arange(seq, dtype=F32)
    inv = ROPE_THETA ** (-jnp.arange(0, rot, 2, dtype=F32) / rot)
    ang = pos[:, None] * inv[None, :]
    cos, sin = jnp.cos(ang), jnp.sin(ang)
    idx = jnp.arange(LANES) % head_dim
    cl = jnp.take(cos, idx % half, axis=1)
    sl = jnp.take(sin, idx % half, axis=1)
    c = jnp.where(idx < rot, cl, 1.0) * scale
    s_up = jnp.where(idx < half, -sl, 0.0) * scale
    s_dn = jnp.where((idx >= half) & (idx < rot), sl, 0.0) * scale
    return c.astype(F32), s_up.astype(F32), s_dn.astype(F32), half


def _rope_chunk(xc, c, s_up, s_dn, half):
    return (xc * c + pltpu.roll(xc, LANES - half, 1) * s_up + pltpu.roll(xc, half, 1) * s_dn)


def _norm_proj_kernel(x_ref, g_ref, w_ref, o_ref):
    h = _rms(x_ref[...], g_ref[...]).astype(BF16)
    o_ref[...] = _dot(h, w_ref[...]).astype(o_ref.dtype)


def _norm_proj(x2d, g, w_bf16, tm):
    m, d = x2d.shape
    n = w_bf16.shape[1]
    return pl.pallas_call(
        _norm_proj_kernel,
        grid=(m // tm,),
        in_specs=[pl.BlockSpec((tm, d), lambda i: (i, 0)),
                  pl.BlockSpec((1, d), lambda i: (0, 0)),
                  pl.BlockSpec((d, n), lambda i: (0, 0))],
        out_specs=pl.BlockSpec((tm, n), lambda i: (i, 0)),
        out_shape=jax.ShapeDtypeStruct((m, n), BF16),
        compiler_params=_cparams(("arbitrary",)),
        name="mem_kv_proj",
    )(x2d, g.reshape(1, d), w_bf16)


def _hyb_front_kernel(x_ref, g_ref, w_ref, cw_ref, cq_ref, suq_ref, sdq_ref, ck_ref, suk_ref, sdk_ref,
                      ya_ref, q_ref, k_ref, v_ref, cbuf, *, tm, half):
    s = pl.program_id(1)
    c = CONV_CH
    base = 3 * c
    nq = DIFF_HEADS * 2 * DIFF_DK
    rh = tm // FRONT_CHAINS
    starts = [ch * rh for ch in range(FRONT_CHAINS)]
    cw = cw_ref[...]

    @pl.when(s == 0)
    def _():
        cbuf[0:8, :] = jnp.zeros((8, c), F32)

    hs = [_rms(x_ref[0, r0:r0 + rh, :], g_ref[...]).astype(BF16) for r0 in starts]
    for r0, h in zip(starts, hs):
        gate_b = _dot(h, w_ref[:, 0:c])
        cu = _dot(h, w_ref[:, c:2 * c]) * _dot(h, w_ref[:, 2 * c:3 * c])
        cbuf[8 + r0:8 + r0 + rh, :] = cu
        conv = (cw[0:1, :] * cbuf[6 + r0:6 + r0 + rh, :] + cw[1:2, :] * cbuf[7 + r0:7 + r0 + rh, :] + cw[2:3, :] * cu)
        ya_ref[0, r0:r0 + rh, :] = (gate_b * conv).astype(BF16)
    cbuf[0:8, :] = cbuf[tm:tm + 8, :]

    for r0, h in zip(starts, hs):
        cq, suq, sdq = cq_ref[r0:r0 + rh, :], suq_ref[r0:r0 + rh, :], sdq_ref[r0:r0 + rh, :]
        pq = _dot(h, w_ref[:, base:base + nq])
        for j in range(nq // LANES):
            q_ref[0, r0:r0 + rh, j * LANES:(j + 1) * LANES] = _rope_chunk(
                pq[:, j * LANES:(j + 1) * LANES], cq, suq, sdq, half).astype(BF16)
    for r0, h in zip(starts, hs):
        ck, suk, sdk = ck_ref[r0:r0 + rh, :], suk_ref[r0:r0 + rh, :], sdk_ref[r0:r0 + rh, :]
        pk = _dot(h, w_ref[:, base + nq:base + 2 * nq])
        for j in range(nq // LANES):
            k_ref[0, r0:r0 + rh, j * LANES:(j + 1) * LANES] = _rope_chunk(
                pk[:, j * LANES:(j + 1) * LANES], ck, suk, sdk, half).astype(BF16)
    for r0, h in zip(starts, hs):
        v_ref[0, r0:r0 + rh, :] = _dot(h, w_ref[:, base + 2 * nq:]).astype(BF16)


def _hyb_front(x, g, w_in, conv_w, tabs_q, tabs_k, half):
    b, s, d = x.shape
    tm = ROW_TILE
    n = w_in.shape[1]
    nq = DIFF_HEADS * 2 * DIFF_DK
    nv = DIFF_HEADS * DIFF_DV
    tab_spec = pl.BlockSpec((tm, LANES), lambda bi, si: (si, 0))
    row = lambda width: pl.BlockSpec((1, tm, width), lambda bi, si: (bi, si, 0))
    return pl.pallas_call(
        functools.partial(_hyb_front_kernel, tm=tm, half=half),
        grid=(b, s // tm),
        in_specs=[row(d),
                  pl.BlockSpec((1, d), lambda bi, si: (0, 0)),
                  pl.BlockSpec((d, n), lambda bi, si: (0, 0)),
                  pl.BlockSpec((CONV_K, CONV_CH), lambda bi, si: (0, 0)),
                  tab_spec, tab_spec, tab_spec, tab_spec, tab_spec, tab_spec],
        out_specs=[row(CONV_CH), row(nq), row(nq), row(nv)],
        out_shape=[jax.ShapeDtypeStruct((b, s, CONV_CH), BF16),
                   jax.ShapeDtypeStruct((b, s, nq), BF16),
                   jax.ShapeDtypeStruct((b, s, nq), BF16),
                   jax.ShapeDtypeStruct((b, s, nv), BF16)],
        scratch_shapes=[pltpu.VMEM((tm + 8, CONV_CH), F32)],
        compiler_params=_cparams(("arbitrary", "arbitrary")),
        name="hyb_front",
    )(x, g.reshape(1, d), w_in, conv_w, *tabs_q, *tabs_k)


def _lane_fold(x, op):
    r = x[:, 0:LANES]
    for c in range(1, x.shape[1] // LANES):
        r = op(r, x[:, c * LANES:(c + 1) * LANES])
    return r


def _diff_attn_kernel(q_ref, k_ref, v_ref, lam_ref, g_ref, o_ref, sbuf, stat, acc, *, tq, hp, lambda_init):
    i = pl.program_id(2)
    lane = lax.broadcasted_iota(I32, (1, LANES), 1)
    qs = []
    for h in range(hp):
        q = q_ref[0, :, h * LANES:(h + 1) * LANES]
        zero = jnp.zeros_like(q)
        qs += [jnp.where(lane < DIFF_DK, q, zero), jnp.where(lane >= DIFF_DK, q, zero)]
    nc = 2 * hp

    hq = tq // 2
    diag = pl.multiple_of(i * tq, tq)

    def pass1(j, carry):
        for c in range(nc):
            h = c // 2
            kb = k_ref[0, pl.ds(pl.multiple_of(j * tq, tq), tq), h * LANES:(h + 1) * LANES]
            sc = _dot_nt(qs[c], kb)
            sbuf[c, j] = sc
            stat[c] = jnp.maximum(stat[c], _lane_fold(sc, jnp.maximum))
        return carry

    stat[...] = jnp.full(stat.shape, NEG_INF, F32)
    lax.fori_loop(0, i, pass1, 0)

    r = lax.broadcasted_iota(I32, (hq, hq), 0)
    cc = lax.broadcasted_iota(I32, (hq, hq), 1)
    tri = cc <= r
    for c in range(nc):
        h = c // 2
        k_lo = k_ref[0, pl.ds(diag, hq), h * LANES:(h + 1) * LANES]
        k_hi = k_ref[0, pl.ds(diag + hq, hq), h * LANES:(h + 1) * LANES]
        s_tl = jnp.where(tri, _dot_nt(qs[c][:hq], k_lo), NEG_INF)
        s_bl = _dot_nt(qs[c][hq:], k_lo)
        s_br = jnp.where(tri, _dot_nt(qs[c][hq:], k_hi), NEG_INF)
        sbuf[c, i, 0:hq, 0:hq] = s_tl
        sbuf[c, i, hq:tq, 0:hq] = s_bl
        sbuf[c, i, hq:tq, hq:tq] = s_br
        stat[c, 0:hq, :] = jnp.maximum(stat[c, 0:hq, :], _lane_fold(s_tl, jnp.maximum))
        stat[c, hq:tq, :] = jnp.maximum(stat[c, hq:tq, :],
                                        jnp.maximum(_lane_fold(s_bl, jnp.maximum), _lane_fold(s_br, jnp.maximum)))
    ms = [jnp.max(stat[c], axis=-1, keepdims=True) for c in range(nc)]

    ones = jnp.ones((tq, LANES), BF16)
    for h in range(hp):
        vb = v_ref[0, pl.ds(diag, tq), h * LANES:(h + 1) * LANES]
        v_ext = jnp.concatenate([vb, ones], axis=1)
        tops, bots = [], []
        for c in (2 * h, 2 * h + 1):
            tops.append(jnp.exp2(sbuf[c, i, 0:hq, 0:hq] - ms[c][:hq]))
            bots.append(jnp.exp2(sbuf[c, i, hq:tq, :] - ms[c][hq:]))
        top = _dot(jnp.concatenate(tops, axis=0).astype(BF16), v_ext[:hq])
        bot = _dot(jnp.concatenate(bots, axis=0).astype(BF16), v_ext)
        acc[h, 0:hq, :] = top[:hq]
        acc[h, hq:tq, :] = bot[:hq]
        acc[h, tq:tq + hq, :] = top[hq:]
        acc[h, tq + hq:2 * tq, :] = bot[hq:]

    def pass2(j, carry):
        for h in range(hp):
            vb = v_ref[0, pl.ds(pl.multiple_of(j * tq, tq), tq), h * LANES:(h + 1) * LANES]
            v_ext = jnp.concatenate([vb, ones], axis=1)
            p0 = jnp.exp2(sbuf[2 * h, j] - ms[2 * h])
            p1 = jnp.exp2(sbuf[2 * h + 1, j] - ms[2 * h + 1])
            acc[h] += _dot(jnp.concatenate([p0, p1], axis=0).astype(BF16), v_ext)
        return carry

    lax.fori_loop(0, i, pass2, 0)

    lf = lam_ref[...]
    lam = (jnp.exp(jnp.sum(lf[0:1] * lf[1:2], keepdims=True))
           - jnp.exp(jnp.sum(lf[2:3] * lf[3:4], keepdims=True)) + lambda_init)
    for h in range(hp):
        a0 = acc[h, 0:tq, :]
        a1 = acc[h, tq:2 * tq, :]
        o = a0[:, :LANES] / a0[:, LANES:] - lam * (a1[:, :LANES] / a1[:, LANES:])
        o_ref[0, :, h * LANES:(h + 1) * LANES] = (_rms(o, g_ref[...]) * (1.0 - lambda_init)).astype(BF16)


def _diff_attn(q, k, v, lam_vecs, subln_g, lambda_init):
    b, s, _ = q.shape
    tq = 512
    hp = 4
    blk = lambda bi, hi, qi: (bi, qi, hi)
    full = lambda bi, hi, qi: (bi, 0, hi)
    return pl.pallas_call(
        functools.partial(_diff_attn_kernel, tq=tq, hp=hp, lambda_init=lambda_init),
        grid=(b, DIFF_HEADS // hp, s // tq),
        in_specs=[pl.BlockSpec((1, tq, hp * LANES), blk),
                  pl.BlockSpec((1, s, hp * LANES), full),
                  pl.BlockSpec((1, s, hp * LANES), full),
                  pl.BlockSpec((4, DIFF_DK), lambda bi, hi, qi: (0, 0)),
                  pl.BlockSpec((1, DIFF_DV), lambda bi, hi, qi: (0, 0))],
        out_specs=pl.BlockSpec((1, tq, hp * LANES), blk),
        out_shape=jax.ShapeDtypeStruct((b, s, DIFF_HEADS * DIFF_DV), BF16),
        scratch_shapes=[pltpu.VMEM((2 * hp, s // tq, tq, tq), F32),
                        pltpu.VMEM((2 * hp, tq, LANES), F32),
                        pltpu.VMEM((hp, 2 * tq, DIFF_DV + LANES), F32)],
        compiler_params=_cparams(("arbitrary", "arbitrary", "arbitrary")),
        name="diff_attn",
    )(q, k, v, lam_vecs, subln_g.reshape(1, DIFF_DV))


def _swa_front_kernel(x_ref, g_ref, w_ref, b_ref, cq_ref, suq_ref, sdq_ref, ck_ref, suk_ref, sdk_ref,
                      q_ref, kv_ref, *, half):
    nq = SWA_Q_HEADS * SWA_HEAD_DIM
    nkv = SWA_KV_HEADS * SWA_HEAD_DIM
    tm = x_ref.shape[1]
    rh = tm // FRONT_CHAINS
    starts = [ch * rh for ch in range(FRONT_CHAINS)]
    hs = [_rms(x_ref[0, r0:r0 + rh, :], g_ref[...]).astype(BF16) for r0 in starts]
    for r0, h in zip(starts, hs):
        cq, suq, sdq = cq_ref[r0:r0 + rh, :], suq_ref[r0:r0 + rh, :], sdq_ref[r0:r0 + rh, :]
        pq = _dot(h, w_ref[:, 0:nq]) + b_ref[:, 0:nq]
        for j in range(nq // LANES):
            lo = j * LANES
            q_ref[0, r0:r0 + rh, lo:lo + LANES] = _rope_chunk(pq[:, lo:lo + LANES], cq, suq, sdq, half).astype(BF16)
    for r0, h in zip(starts, hs):
        ck, suk, sdk = ck_ref[r0:r0 + rh, :], suk_ref[r0:r0 + rh, :], sdk_ref[r0:r0 + rh, :]
        pkv = _dot(h, w_ref[:, nq:]) + b_ref[:, nq:]
        for j in range(nkv // LANES):
            lo = j * LANES
            kv_ref[0, r0:r0 + rh, lo:lo + LANES] = _rope_chunk(pkv[:, lo:lo + LANES], ck, suk, sdk, half).astype(BF16)
        kv_ref[0, r0:r0 + rh, nkv:] = pkv[:, nkv:].astype(BF16)


def _swa_front(x, g, w_qkv, b_qkv, tabs_q, tabs_k, half):
    b, s, d = x.shape
    tm = ROW_TILE
    n = w_qkv.shape[1]
    nq = SWA_Q_HEADS * SWA_HEAD_DIM
    nkv = SWA_KV_HEADS * SWA_HEAD_DIM
    tab_spec = pl.BlockSpec((tm, LANES), lambda bi, si: (si, 0))
    row = lambda width: pl.BlockSpec((1, tm, width), lambda bi, si: (bi, si, 0))
    return pl.pallas_call(
        functools.partial(_swa_front_kernel, half=half),
        grid=(b, s // tm),
        in_specs=[row(d),
                  pl.BlockSpec((1, d), lambda bi, si: (0, 0)),
                  pl.BlockSpec((d, n), lambda bi, si: (0, 0)),
                  pl.BlockSpec((1, n), lambda bi, si: (0, 0)),
                  tab_spec, tab_spec, tab_spec, tab_spec, tab_spec, tab_spec],
        out_specs=[row(nq), row(2 * nkv)],
        out_shape=[jax.ShapeDtypeStruct((b, s, nq), BF16),
                   jax.ShapeDtypeStruct((b, s, 2 * nkv), BF16)],
        compiler_params=_cparams(("arbitrary", "arbitrary")),
        name="swa_front",
    )(x, g.reshape(1, d), w_qkv, b_qkv.reshape(1, n), *tabs_q, *tabs_k)


def _swa_head_order():
    g_sz = SWA_Q_HEADS // SWA_KV_HEADS
    order = []
    for slab in range(SWA_Q_HEADS // 2):
        pair, j = slab // g_sz, slab % g_sz
        order += [(2 * pair) * g_sz + j, (2 * pair + 1) * g_sz + j]
    return order


def _swa_attn_kernel(sink_ref, q_ref, kvp_ref, kvc_ref, o_ref, *, nblk):
    i = pl.program_id(1)
    hd = SWA_HEAD_DIM
    nkv = SWA_KV_HEADS * hd
    g_sz = SWA_Q_HEADS // SWA_KV_HEADS
    kv = jnp.concatenate([kvp_ref[0], kvc_ref[0]], axis=0)
    r = lax.broadcasted_iota(I32, (BLOCK, 2 * BLOCK), 0)
    c = lax.broadcasted_iota(I32, (BLOCK, 2 * BLOCK), 1)
    rel = c - BLOCK - r
    in_win = (rel <= 0) & (rel > -SWA_WINDOW)
    lane = lax.broadcasted_iota(I32, (1, LANES), 1)
    lo_half = lane < hd
    ones = jnp.ones((2 * BLOCK, LANES), BF16)
    for n in range(nblk):
        mask = in_win & ((c >= BLOCK) | (i > 0)) if n == 0 else in_win
        keys = kv[n * BLOCK:(n + 2) * BLOCK]
        for pair in range(SWA_KV_HEADS // 2):
            k2 = keys[:, pair * LANES:(pair + 1) * LANES]
            v2 = keys[:, nkv + pair * LANES:nkv + (pair + 1) * LANES]
            v_ext = jnp.concatenate([v2, ones], axis=1)
            pieces = []
            for j in range(g_sz):
                slab = pair * g_sz + j
                qs = q_ref[0, n * BLOCK:(n + 1) * BLOCK, slab * LANES:(slab + 1) * LANES]
                zero = jnp.zeros_like(qs)
                pieces += [jnp.where(lo_half, qs, zero), jnp.where(lo_half, zero, qs)]
            sc = _dot_nt(jnp.concatenate(pieces, axis=0), k2)
            probs, tails = [], []
            for pc in range(2 * g_sz):
                scp = jnp.where(mask, sc[pc * BLOCK:(pc + 1) * BLOCK], NEG_INF)
                sink = sink_ref[2 * g_sz * pair + pc] * LOG2E
                m = jnp.maximum(jnp.max(scp, axis=-1, keepdims=True), sink)
                probs.append(jnp.exp2(scp - m).astype(BF16))
                tails.append(jnp.exp2(sink - m))
            pv = _dot(jnp.concatenate(probs, axis=0), v_ext)
            for j in range(g_sz):
                slab = pair * g_sz + j
                halves = []
                for hf in range(2):
                    pc = 2 * j + hf
                    blk = pv[pc * BLOCK:(pc + 1) * BLOCK]
                    halves.append(blk[:, :LANES] / (blk[:, LANES:] + tails[pc]))
                o_ref[0, n * BLOCK:(n + 1) * BLOCK, slab * LANES:(slab + 1) * LANES] = (
                    jnp.where(lo_half, halves[0], halves[1]).astype(BF16))


def _swa_attn(q, kv, sinks):
    b, s, nq = q.shape
    nblk = 4
    tq = nblk * BLOCK
    return pl.pallas_call(
        functools.partial(_swa_attn_kernel, nblk=nblk),
        grid_spec=pltpu.PrefetchScalarGridSpec(
            num_scalar_prefetch=1,
            grid=(b, s // tq),
            in_specs=[pl.BlockSpec((1, tq, nq), lambda bi, ni, sk: (bi, ni, 0)),
                      pl.BlockSpec((1, BLOCK, kv.shape[2]), lambda bi, ni, sk: (bi, jnp.maximum(ni * nblk - 1, 0), 0)),
                      pl.BlockSpec((1, tq, kv.shape[2]), lambda bi, ni, sk: (bi, ni, 0))],
            out_specs=pl.BlockSpec((1, tq, nq), lambda bi, ni, sk: (bi, ni, 0))),
        out_shape=jax.ShapeDtypeStruct((b, s, nq), BF16),
        compiler_params=_cparams(("arbitrary", "arbitrary")),
        name="swa_attn",
    )(sinks, q, kv, kv)


def _post_kernel(*refs, tm, n_a, has_bias):
    x_ref = refs[0]
    a_refs = refs[1:1 + n_a]
    k = 1 + n_a
    wo_refs = refs[k:k + n_a]
    k += n_a
    if has_bias:
        bo_ref = refs[k]
        k += 1
    (gx_ref, wq_ref, mkv_ref, wxo_ref, gf_ref, wr_ref, br_ref,
     x2_ref, h2_ref, mi_ref, mw_ref, cnt_ref, cnt_acc, cnt_col) = refs[k:]
    first = (pl.program_id(0) == 0) & (pl.program_id(1) == 0)

    @pl.when(first)
    def _():
        cnt_acc[...] = jnp.zeros_like(cnt_acc)
        cnt_col[...] = jnp.zeros_like(cnt_col)

    xw = XATTN_HEADS * XATTN_HEAD_DIM
    ones = jnp.ones((mkv_ref.shape[1], LANES), BF16)

    rh = tm // POST_CHAINS

    def out_proj(r0):
        acc = _dot(a_refs[0][0, r0:r0 + rh, :], wo_refs[0][...])
        for a_ref, w_ref in zip(a_refs[1:], wo_refs[1:]):
            acc = acc + _dot(a_ref[0, r0:r0 + rh, :], w_ref[...])
        if has_bias:
            acc = acc + bo_ref[...]
        return x_ref[0, r0:r0 + rh, :] + acc

    def q_proj(x1):
        hx = _rms(x1, gx_ref[...]).astype(BF16)
        return (_dot(hx, wq_ref[...]) * (XATTN_HEAD_DIM ** -0.5 * LOG2E)).astype(BF16)

    def mem_attn(qx):
        outs = []
        for hh in range(XATTN_HEADS):
            lo = hh * XATTN_HEAD_DIM
            mk = mkv_ref[0, :, lo:lo + XATTN_HEAD_DIM]
            mv = jnp.concatenate([mkv_ref[0, :, xw + lo:xw + lo + XATTN_HEAD_DIM], ones], axis=1)
            sc = _dot_nt(qx[:, lo:lo + XATTN_HEAD_DIM], mk)
            m = jnp.max(sc, axis=-1, keepdims=True)
            pv = _dot(jnp.exp2(sc - m).astype(BF16), mv)
            outs.append((pv[:, :LANES] / pv[:, LANES:]).astype(BF16))
        return jnp.concatenate(outs, axis=-1)

    def o_proj(r0, x1, ox):
        x2 = x1 + _dot(ox, wxo_ref[...])
        x2_ref[0, r0:r0 + rh, :] = x2
        return x2

    def router(r0, x2):
        h2 = _rms(x2, gf_ref[...])
        _to_token_tiles(h2_ref, h2, rh, r0)
        return _dot(h2.astype(BF16), wr_ref[...]) + br_ref[...]

    starts = [c * rh for c in range(POST_CHAINS)]
    x1s = [out_proj(r0) for r0 in starts]
    qxs = [q_proj(x1) for x1 in x1s]
    oxs = [mem_attn(qx) for qx in qxs]
    x2s = [o_proj(r0, x1, ox) for r0, x1, ox in zip(starts, x1s, oxs)]
    logits = jnp.concatenate([router(r0, x2) for r0, x2 in zip(starts, x2s)], axis=0)

    lt = logits.T
    ex = lt[0:N_EXPERTS]
    grp = lt[N_EXPERTS:N_EXPERTS + SUBLANES]
    grow = lax.broadcasted_iota(I32, grp.shape, 0).astype(F32)
    lg = jnp.where(grow < N_GROUPS, grp, NEG_INF)
    mg = jnp.max(lg, axis=0, keepdims=True)
    g_sel = jnp.min(jnp.where(lg == mg, grow, float(SUBLANES)), axis=0, keepdims=True)
    p_g = 1.0 / jnp.sum(jnp.exp(lg - mg), axis=0, keepdims=True)
    erow = lax.broadcasted_iota(I32, ex.shape, 0).astype(F32)
    e_lo = g_sel * EXPERTS_PER_GROUP
    le = jnp.where((erow >= e_lo) & (erow < e_lo + EXPERTS_PER_GROUP), ex, NEG_INF)
    big = float(N_EXPERTS)
    m1 = jnp.max(le, axis=0, keepdims=True)
    i1 = jnp.min(jnp.where(le == m1, erow, big), axis=0, keepdims=True)
    le2 = jnp.where(erow == i1, NEG_INF, le)
    m2 = jnp.max(le2, axis=0, keepdims=True)
    i2 = jnp.min(jnp.where(le2 == m2, erow, big), axis=0, keepdims=True)
    t = jnp.exp(m2 - m1)
    w1 = p_g / (1.0 + t)
    w2 = p_g * t / (1.0 + t)

    oh1 = erow == i1
    oh2 = erow == i2
    oh = jnp.where(oh1 | oh2, 1.0, 0.0)
    rr = lax.broadcasted_iota(I32, (tm, tm), 0)
    cc = lax.broadcasted_iota(I32, (tm, tm), 1)
    earlier = jnp.where(rr < cc, 1.0, 0.0).astype(BF16)
    before = _dot(oh.astype(BF16), earlier) + cnt_col[...]
    r1 = jnp.sum(jnp.where(oh1, before, 0.0), axis=0, keepdims=True)
    r2 = jnp.sum(jnp.where(oh2, before, 0.0), axis=0, keepdims=True)
    cnt_col[...] = cnt_col[...] + jnp.sum(oh, axis=1, keepdims=True)
    oh_pad = jnp.concatenate([oh, jnp.zeros((LANES - N_EXPERTS, tm), F32)], axis=0).astype(BF16)
    cnt_new = cnt_acc[...] + _dot_nt(jnp.ones((SUBLANES, tm), BF16), oh_pad)[0:1]
    cnt_acc[...] = cnt_new
    cnt_ref[...] = cnt_new.astype(I32)

    zrow = jnp.zeros((1, tm), F32)
    mi_ref[...] = jnp.concatenate([i1, i2, r1, r2, zrow, zrow, zrow, zrow], axis=0).astype(I32)
    wt = jnp.concatenate([w1, w2, jnp.zeros((LANES - TOP_K, tm), F32)], axis=0)
    mw_ref[...] = wt.T


def _post_mixer(x, a_list, wo_list, b_out, g_x, w_q, mkv, w_xo, g_f, w_rt, b_rt):
    b, s, d = x.shape
    tm = ROW_TILE
    ns = s // tm
    t = b * s
    n_a = len(a_list)
    has_bias = b_out is not None
    const2 = lambda bi, si: (0, 0)
    row = lambda width: pl.BlockSpec((1, tm, width), lambda bi, si: (bi, si, 0))
    in_specs = [row(d)] + [row(a.shape[2]) for a in a_list]
    in_specs += [pl.BlockSpec(w.shape, const2) for w in wo_list]
    args = [x, *a_list, *wo_list]
    if has_bias:
        in_specs.append(pl.BlockSpec((1, d), const2))
        args.append(b_out.reshape(1, d))
    xw = w_q.shape[1]
    in_specs += [pl.BlockSpec((1, d), const2),
                 pl.BlockSpec((d, xw), const2),
                 pl.BlockSpec((1, mkv.shape[1], mkv.shape[2]), lambda bi, si: (bi, 0, 0)),
                 pl.BlockSpec((xw, d), const2),
                 pl.BlockSpec((1, d), const2),
                 pl.BlockSpec((d, LANES), const2),
                 pl.BlockSpec((1, LANES), const2)]
    args += [g_x.reshape(1, d), w_q, mkv, w_xo, g_f.reshape(1, d), w_rt, b_rt]
    out_specs = [row(d),
                 pl.BlockSpec((tm * TOKEN_ROWS, LANES), lambda bi, si: (bi * ns + si, 0)),
                 pl.BlockSpec((8, tm), lambda bi, si: (0, bi * ns + si)),
                 pl.BlockSpec((tm, LANES), lambda bi, si: (bi * ns + si, 0)),
                 pl.BlockSpec((1, LANES), const2)]
    out_shape = [jax.ShapeDtypeStruct((b, s, d), F32),
                 jax.ShapeDtypeStruct((t * TOKEN_ROWS, LANES), U32),
                 jax.ShapeDtypeStruct((8, t), I32),
                 jax.ShapeDtypeStruct((t, LANES), F32),
                 jax.ShapeDtypeStruct((1, LANES), I32)]
    return pl.pallas_call(
        functools.partial(_post_kernel, tm=tm, n_a=n_a, has_bias=has_bias),
        grid=(b, ns),
        in_specs=in_specs,
        out_specs=out_specs,
        out_shape=out_shape,
        scratch_shapes=[pltpu.VMEM((1, LANES), F32), pltpu.VMEM((N_EXPERTS, 1), F32)],
        compiler_params=_cparams(("arbitrary", "arbitrary")),
        name="post_mixer",
    )(*args)


def _to_token_tiles(dst_ref, val, rows, tok0=0):
    half = TOKEN_ROWS * LANES
    for j in range(TOKEN_ROWS):
        hi = val[:, j * LANES:(j + 1) * LANES].astype(BF16).astype(F32)
        lo = val[:, half + j * LANES:half + (j + 1) * LANES].astype(BF16).astype(F32)
        word = lax.bitcast_convert_type(hi, U32) | (lax.bitcast_convert_type(lo, U32) >> 16)
        dst_ref[pl.ds(tok0 * TOKEN_ROWS + j, rows, stride=TOKEN_ROWS), :] = word


def _from_token_tiles(src_ref, rows, tok0=0):
    his, los = [], []
    for j in range(TOKEN_ROWS):
        word = src_ref[pl.ds(tok0 * TOKEN_ROWS + j, rows, stride=TOKEN_ROWS), :]
        his.append(lax.bitcast_convert_type(word & jnp.uint32(0xFFFF0000), F32))
        los.append(lax.bitcast_convert_type(word << 16, F32))
    return jnp.concatenate(his + los, axis=1)


def _token_slice(ref, tok):
    return ref.at[pl.ds(pl.multiple_of(tok * TOKEN_ROWS, TOKEN_ROWS), TOKEN_ROWS)]


def _pos_kernel(off_ref, mi_ref, pos_ref):
    e = mi_ref[0:TOP_K, :]
    pos = mi_ref[TOP_K:2 * TOP_K, :]
    for j in range(N_EXPERTS):
        pos = pos + jnp.where(e == j, off_ref[j], 0)
    pos_ref[...] = pos


def _sorted_positions(off, meta_i):
    t = meta_i.shape[1]
    return pl.pallas_call(
        _pos_kernel,
        grid_spec=pltpu.PrefetchScalarGridSpec(
            num_scalar_prefetch=1,
            grid=(1,),
            in_specs=[pl.BlockSpec((8, t), lambda i, off: (0, 0))],
            out_specs=pl.BlockSpec((TOP_K, t), lambda i, off: (0, 0))),
        out_shape=jax.ShapeDtypeStruct((TOP_K, t), I32),
        compiler_params=_cparams(("arbitrary",)),
        name="moe_positions",
    )(off, meta_i)


def _tile_major(pos, tm):
    t = pos.shape[1]
    return pos.reshape(TOP_K, t // tm, tm).transpose(1, 0, 2).reshape(-1)


def _dispatch_kernel(pos_ref, h_ref, xs_ref, sem, *, tm):
    def issue(g, _):
        for u in range(DMA_UNROLL):
            r = g * DMA_UNROLL + u
            for kk in range(TOP_K):
                pos = pos_ref[kk * tm + r]
                pltpu.make_async_copy(_token_slice(h_ref, r), _token_slice(xs_ref, pos), sem).start(priority=kk)
        return 0

    lax.fori_loop(0, tm // DMA_UNROLL, issue, 0)
    for _ in range(TOP_K):
        pltpu.make_async_copy(h_ref, xs_ref.at[pl.ds(0, tm * TOKEN_ROWS)], sem).wait()


def _dispatch(pos, h2t, n_rows):
    tm = DISPATCH_TILE
    t = h2t.shape[0] // TOKEN_ROWS
    return pl.pallas_call(
        functools.partial(_dispatch_kernel, tm=tm),
        grid=(t // tm,),
        in_specs=[pl.BlockSpec((TOP_K * tm,), lambda i: (i,), memory_space=pltpu.SMEM),
                  pl.BlockSpec((tm * TOKEN_ROWS, LANES), lambda i: (i, 0))],
        out_specs=pl.BlockSpec(memory_space=pl.ANY),
        scratch_shapes=[pltpu.SemaphoreType.DMA],
        out_shape=jax.ShapeDtypeStruct((n_rows * TOKEN_ROWS, LANES), U32),
        compiler_params=_cparams(("arbitrary",)),
        name="moe_dispatch",
    )(_tile_major(pos, tm), h2t)


def _expert_kernel(tblk_ref, texp_ref, tn_ref, tnxt_ref, tpar_ref, xs_ref, wgu_hbm, wdn_hbm, ys_ref,
                   wgu_bf, wdn_bf, wgu_f, wdn_f, sems, *, tm, layer):
    i = pl.program_id(0)
    n_valid = tn_ref[i]
    expert = texp_ref[i]
    slot = tpar_ref[i]
    new_expert = (i == 0) | (expert != texp_ref[jnp.maximum(i - 1, 0)])

    def weight_copies(ex, sl):
        return (pltpu.make_async_copy(wgu_hbm.at[layer, ex], wgu_f.at[sl], sems.at[sl]),
                pltpu.make_async_copy(wdn_hbm.at[layer, ex], wdn_f.at[sl], sems.at[sl]))

    @pl.when(i == 0)
    def _():
        for cp in weight_copies(expert, slot):
            cp.start()

    @pl.when(new_expert)
    def _():
        @pl.when(tnxt_ref[i] >= 0)
        def _():
            for cp in weight_copies(tnxt_ref[i], 1 - slot):
                cp.start()

        for cp in weight_copies(expert, slot):
            cp.wait()
        wgu_bf[...] = wgu_f[slot].astype(BF16)
        wdn_bf[...] = wdn_f[slot].astype(BF16)

    @pl.when(n_valid > 0)
    def _():
        rh = tm // EXPERT_CHAINS
        starts = [c * rh for c in range(EXPERT_CHAINS)]
        row = lax.broadcasted_iota(I32, (rh, 1), 0)
        gus = []
        for r0 in starts:
            x = jnp.where(row + r0 < n_valid, _from_token_tiles(xs_ref, rh, r0), 0.0).astype(BF16)
            gus.append(_dot(x, wgu_bf[...]))
        ys = []
        for gu in gus:
            hid = (gu[:, :EXPERT_HIDDEN] / (1.0 + jnp.exp(-gu[:, :EXPERT_HIDDEN])) * gu[:, EXPERT_HIDDEN:]).astype(BF16)
            ys.append(_dot(hid, wdn_bf[...]))
        for r0, y in zip(starts, ys):
            _to_token_tiles(ys_ref, y, rh, r0)


def _experts(plan, xs, w_gu, w_dn, layer):
    tm = EXPERT_TILE
    nt = xs.shape[0] // (tm * TOKEN_ROWS)
    gu_shape, dn_shape = w_gu.shape[2:], w_dn.shape[2:]
    rows = lambda i, tb, te, tn, tx, tp: (tb[i], 0)
    return pl.pallas_call(
        functools.partial(_expert_kernel, tm=tm, layer=layer),
        grid_spec=pltpu.PrefetchScalarGridSpec(
            num_scalar_prefetch=5,
            grid=(nt,),
            in_specs=[pl.BlockSpec((tm * TOKEN_ROWS, LANES), rows),
                      pl.BlockSpec(memory_space=pl.ANY),
                      pl.BlockSpec(memory_space=pl.ANY)],
            out_specs=pl.BlockSpec((tm * TOKEN_ROWS, LANES), rows),
            scratch_shapes=[pltpu.VMEM(gu_shape, BF16), pltpu.VMEM(dn_shape, BF16),
                            pltpu.VMEM((2,) + gu_shape, F32), pltpu.VMEM((2,) + dn_shape, F32),
                            pltpu.SemaphoreType.DMA((2,))]),
        out_shape=jax.ShapeDtypeStruct(xs.shape, U32),
        compiler_params=_cparams(("arbitrary",)),
        name="moe_experts",
    )(*plan, xs, w_gu, w_dn)


def _moe_combine_tile(pos_cur, pos_nxt, mw_ref, x, ys_ref, ybuf, sems, i, n, tm):
    def gather(pos_ref, slot):
        def issue(g, _):
            for u in range(DMA_UNROLL):
                r = g * DMA_UNROLL + u
                for kk in range(TOP_K):
                    pos = pos_ref[kk * tm + r]
                    pltpu.make_async_copy(_token_slice(ys_ref, pos), _token_slice(ybuf.at[slot, kk], r),
                                          sems.at[slot]).start(priority=kk)
            return 0

        lax.fori_loop(0, tm // DMA_UNROLL, issue, 0)

    @pl.when(i == 0)
    def _():
        gather(pos_cur, 0)

    @pl.when(i + 1 < n)
    def _():
        gather(pos_nxt, (i + 1) % 2)

    slot = i % 2
    for kk in range(TOP_K):
        pltpu.make_async_copy(ys_ref.at[pl.ds(0, tm * TOKEN_ROWS)], ybuf.at[slot, kk], sems.at[slot]).wait()
    mw = mw_ref[...]
    return (x + mw[:, 0:1] * _from_token_tiles(ybuf.at[slot, 0], tm)
            + mw[:, 1:2] * _from_token_tiles(ybuf.at[slot, 1], tm))


def _combine_kernel(pos_cur, pos_nxt, mw_ref, x_ref, ys_ref, *rest, tm, final):
    if final:
        g_ref, o_ref, ybuf, sems = rest
    else:
        o_ref, ybuf, sems = rest
    x3 = _moe_combine_tile(pos_cur, pos_nxt, mw_ref, x_ref[...], ys_ref, ybuf, sems,
                           pl.program_id(0), pl.num_programs(0), tm)
    if final:
        x3 = _rms(x3, g_ref[...])
    o_ref[...] = x3


def _combine(moe, g_final):
    pos, meta_w, x2, ys = moe
    d = x2.shape[-1]
    x2d = x2.reshape(-1, d)
    t = x2d.shape[0]
    tm = COMBINE_TILE
    nt = t // tm
    final = g_final is not None
    in_specs = [pl.BlockSpec((TOP_K * tm,), lambda i: (i,), memory_space=pltpu.SMEM),
                pl.BlockSpec((TOP_K * tm,), lambda i: (jnp.minimum(i + 1, nt - 1),), memory_space=pltpu.SMEM),
                pl.BlockSpec((tm, LANES), lambda i: (i, 0)),
                pl.BlockSpec((tm, d), lambda i: (i, 0)),
                pl.BlockSpec(memory_space=pl.ANY)]
    pos_flat = _tile_major(pos, tm)
    args = [pos_flat, pos_flat, meta_w, x2d, ys]
    if final:
        in_specs.append(pl.BlockSpec((1, d), lambda i: (0, 0)))
        args.append(g_final.reshape(1, d))
    return pl.pallas_call(
        functools.partial(_combine_kernel, tm=tm, final=final),
        grid=(nt,),
        in_specs=in_specs,
        out_specs=pl.BlockSpec((tm, d), lambda i: (i, 0)),
        scratch_shapes=[pltpu.VMEM((2, TOP_K, tm * TOKEN_ROWS, LANES), U32), pltpu.SemaphoreType.DMA((2,))],
        out_shape=jax.ShapeDtypeStruct((t, d), F32),
        compiler_params=_cparams(("arbitrary",)),
        name="moe_combine",
    )(*args)


def _plan_kernel(cnt_ref, off_ref, blk_ref, exp_ref, nv_ref, nxt_ref, par_ref, *, tm, n_tiles):
    shift = tm.bit_length() - 1

    def clear(i, _):
        nxt_ref[i] = -1
        return 0

    lax.fori_loop(0, n_tiles, clear, 0)

    def per_expert(e, carry):
        row0, run, prev_first = carry
        c = cnt_ref[0, e]
        ntile = lax.shift_right_logical(c + (tm - 1), shift)
        off_ref[e] = row0
        t0 = lax.shift_right_logical(row0, shift)

        def fill(j, _):
            blk_ref[t0 + j] = t0 + j
            exp_ref[t0 + j] = e
            nv_ref[t0 + j] = jnp.minimum(c - j * tm, tm)
            par_ref[t0 + j] = run & 1
            return 0

        lax.fori_loop(0, ntile, fill, 0)
        has = ntile > 0

        @pl.when(has & (prev_first >= 0))
        def _():
            nxt_ref[prev_first] = e

        return (row0 + lax.shift_left(ntile, shift), run + has.astype(I32), jnp.where(has, t0, prev_first))

    total, _, _ = lax.fori_loop(0, N_EXPERTS, per_expert, (jnp.int32(0), jnp.int32(0), jnp.int32(-1)))
    used = lax.shift_right_logical(total, shift)
    last = jnp.maximum(used - 1, 0)

    def tail(i, _):
        blk_ref[i] = last
        exp_ref[i] = exp_ref[last]
        nv_ref[i] = 0
        par_ref[i] = par_ref[last]
        return 0

    lax.fori_loop(used, n_tiles, tail, 0)


def _moe_plan(counts, n_tiles):
    tm = EXPERT_TILE
    assert tm & (tm - 1) == 0
    smem = pl.BlockSpec(memory_space=pltpu.SMEM)
    return pl.pallas_call(
        functools.partial(_plan_kernel, tm=tm, n_tiles=n_tiles),
        in_specs=[smem],
        out_specs=[smem] * 6,
        out_shape=[jax.ShapeDtypeStruct((N_EXPERTS,), I32)] + [jax.ShapeDtypeStruct((n_tiles,), I32)] * 5,
        name="moe_plan",
    )(counts)


def _moe(x2, h2t, meta_i, meta_w, counts, w_gu, w_dn, layer):
    b, s, d = x2.shape
    assert d == 2 * TOKEN_ROWS * LANES
    n_rows = b * s * TOP_K + N_EXPERTS * EXPERT_TILE
    off, *tile_plan = _moe_plan(counts, n_rows // EXPERT_TILE)
    pos = _sorted_positions(off, meta_i)
    xs = _dispatch(pos, h2t, n_rows)
    ys = _experts(tile_plan, xs, w_gu, w_dn, layer)
    return pos, meta_w, x2, ys


def _router_weights(w_group, b_group, w_router, b_router):
    d = w_group.shape[0]
    pad = LANES - N_EXPERTS - N_GROUPS
    w = jnp.concatenate([w_router, w_group, jnp.zeros((d, pad), F32)], axis=1)
    bias = jnp.concatenate([b_router, b_group, jnp.zeros((pad,), F32)]).reshape(1, LANES)
    return w.astype(BF16), bias


def kernel(x, mem, mem_norm, mem_w_kv, norm_mix, norm_xattn, norm_ffn, hyb_w_in, hyb_conv_w, diff_lambda, diff_subln, hyb_w_out, swa_w_qkv, swa_b_qkv, swa_sinks, swa_w_out, swa_b_out, xattn_w_q, xattn_w_o, moe_w_group, moe_b_group, moe_w_router, moe_b_router, moe_w_gate_up, moe_w_down, final_norm):
    b, s, d = x.shape
    m = mem.shape[1]
    depth = norm_mix.shape[0]
    mkv = _norm_proj(mem.reshape(b * m, d), mem_norm, mem_w_kv.astype(BF16), m).reshape(b, m, -1)

    scale = DIFF_DK ** -0.5 * LOG2E
    cq, suq, sdq, half = _rope_lane_tables(s, DIFF_DK, scale)
    ck, suk, sdk, _ = _rope_lane_tables(s, DIFF_DK, 1.0)
    tabs_q, tabs_k = (cq, suq, sdq), (ck, suk, sdk)

    moe = None
    for l in range(depth):
        if moe is not None:
            x = _combine(moe, None).reshape(b, s, d)
        if l % 2 == 0:
            e = l // 2
            lambda_init = 0.8 - 0.6 * math.exp(-0.3 * l)
            ya, q, k, v = _hyb_front(x, norm_mix[l], hyb_w_in[e].astype(BF16), hyb_conv_w[e], tabs_q, tabs_k, half)
            o = _diff_attn(q, k, v, diff_lambda[e], diff_subln[e], lambda_init)
            w_out = hyb_w_out[e].astype(BF16)
            a_list, wo_list, b_out = [ya, o], [w_out[:CONV_CH], w_out[CONV_CH:]], None
        else:
            e = l // 2
            order = _swa_head_order()
            hd, nq = SWA_HEAD_DIM, SWA_Q_HEADS * SWA_HEAD_DIM
            heads = lambda a, axis: [lax.slice_in_dim(a, h * hd, (h + 1) * hd, axis=axis) for h in order]
            w_all = swa_w_qkv[e].astype(BF16)
            w_qkv = jnp.concatenate(heads(w_all, 1) + [w_all[:, nq:]], axis=1)
            b_qkv = jnp.concatenate(heads(swa_b_qkv[e], 0) + [swa_b_qkv[e][nq:]])
            sinks = jnp.stack([swa_sinks[e][h] for h in order])
            q, kv = _swa_front(x, norm_mix[l], w_qkv, b_qkv, tabs_q, tabs_k, half)
            o = _swa_attn(q, kv, sinks)
            w_out = jnp.concatenate(heads(swa_w_out[e].astype(BF16), 0), axis=0)
            a_list, wo_list, b_out = [o], [w_out], swa_b_out[e]
        w_rt, b_rt = _router_weights(moe_w_group[l], moe_b_group[l], moe_w_router[l], moe_b_router[l])
        x2, h2, meta_i, meta_w, counts = _post_mixer(
            x, a_list, wo_list, b_out, norm_xattn[l], xattn_w_q[l].astype(BF16), mkv,
            xattn_w_o[l].astype(BF16), norm_ffn[l], w_rt, b_rt)
        moe = _moe(x2, h2, meta_i, meta_w, counts, moe_w_gate_up, moe_w_down, l)
    return _combine(moe, final_norm).reshape(b, s, d)
```

```python
import functools
import math

import jax
import jax.numpy as jnp
from jax import lax
from jax.experimental import pallas as pl
from jax.experimental.pallas import tpu as pltpu

F32 = jnp.float32
BF16 = jnp.bfloat16
I32 = jnp.int32
U32 = jnp.uint32

EPS = 1e-6
LANES = 128
SUBLANES = 8
TOKEN_ROWS = 4
DMA_UNROLL = 16
VMEM_LIMIT = 56 * 1024 * 1024

ROPE_THETA = 500000.0
ROPE_FRACTION = 4
BLOCK = 128
CONV_CH = 512
CONV_K = 3
DIFF_HEADS = 4
DIFF_DK = 64
DIFF_DV = 128
SWA_Q_HEADS = 16
SWA_KV_HEADS = 4
SWA_HEAD_DIM = 64
SWA_WINDOW = 128
XATTN_HEADS = 4
XATTN_HEAD_DIM = 128
N_GROUPS = 4
EXPERTS_PER_GROUP = 8
N_EXPERTS = N_GROUPS * EXPERTS_PER_GROUP
TOP_K = 2
EXPERT_HIDDEN = 512

ROW_TILE = 1024
EXPERT_TILE = 512
DISPATCH_TILE = 2048
COMBINE_TILE = 512
POST_CHAINS = 2
EXPERT_CHAINS = 2
FRONT_CHAINS = 2
NEG_INF = float("-inf")
LOG2E = math.log2(math.e)


def _cparams(sem):
    return pltpu.CompilerParams(dimension_semantics=sem, vmem_limit_bytes=VMEM_LIMIT)


def _rms(x, g):
    return x * lax.rsqrt(jnp.mean(x * x, axis=-1, keepdims=True) + EPS) * g


def _dot(a, b):
    return jnp.dot(a, b, preferred_element_type=F32)


def _dot_nt(a, b):
    return lax.dot_general(a, b, (((1,), (1,)), ((), ())), preferred_element_type=F32)


def _rope_lane_tables(seq, head_dim, scale):
    rot = head_dim // ROPE_FRACTION
    half = rot // 2
    pos = jnp.arange(seq, dtype=F32)
    inv = ROPE_THETA ** (-jnp.arange(0, rot, 2, dtype=F32) / rot)
    ang = pos[:, None] * inv[None, :]
    cos, sin = jnp.cos(ang), jnp.sin(ang)
    idx = jnp.arange(LANES) % head_dim
    cl = jnp.take(cos, idx % half, axis=1)
    sl = jnp.take(sin, idx % half, axis=1)
    c = jnp.where(idx < rot, cl, 1.0) * scale
    s_up = jnp.where(idx < half, -sl, 0.0) * scale
    s_dn = jnp.where((idx >= half) & (idx < rot), sl, 0.0) * scale
    return c.astype(F32), s_up.astype(F32), s_dn.astype(F32), half


def _rope_chunk(xc, c, s_up, s_dn, half):
    return (xc * c + pltpu.roll(xc, LANES - half, 1) * s_up + pltpu.roll(xc, half, 1) * s_dn)


def _norm_proj_kernel(x_ref, g_ref, w_ref, o_ref):
    h = _rms(x_ref[...], g_ref[...]).astype(BF16)
    o_ref[...] = _dot(h, w_ref[...]).astype(o_ref.dtype)


def _norm_proj(x2d, g, w_bf16, tm):
    m, d = x2d.shape
    n = w_bf16.shape[1]
    return pl.pallas_call(
        _norm_proj_kernel,
        grid=(m // tm,),
        in_specs=[pl.BlockSpec((tm, d), lambda i: (i, 0)),
                  pl.BlockSpec((1, d), lambda i: (0, 0)),
                  pl.BlockSpec((d, n), lambda i: (0, 0))],
        out_specs=pl.BlockSpec((tm, n), lambda i: (i, 0)),
        out_shape=jax.ShapeDtypeStruct((m, n), BF16),
        compiler_params=_cparams(("arbitrary",)),
        name="mem_kv_proj",
    )(x2d, g.reshape(1, d), w_bf16)


def _hyb_front_kernel(x_ref, g_ref, w_ref, cw_ref, cq_ref, suq_ref, sdq_ref, ck_ref, suk_ref, sdk_ref,
                      ya_ref, q_ref, k_ref, v_ref, cbuf, *, tm, half):
    s = pl.program_id(1)
    c = CONV_CH
    base = 3 * c
    nq = DIFF_HEADS * 2 * DIFF_DK
    rh = tm // FRONT_CHAINS
    starts = [ch * rh for ch in range(FRONT_CHAINS)]
    cw = cw_ref[...]

    @pl.when(s == 0)
    def _():
        cbuf[0:8, :] = jnp.zeros((8, c), F32)

    hs = [_rms(x_ref[0, r0:r0 + rh, :], g_ref[...]).astype(BF16) for r0 in starts]
    for r0, h in zip(starts, hs):
        gate_b = _dot(h, w_ref[:, 0:c])
        cu = _dot(h, w_ref[:, c:2 * c]) * _dot(h, w_ref[:, 2 * c:3 * c])
        cbuf[8 + r0:8 + r0 + rh, :] = cu
        conv = (cw[0:1, :] * cbuf[6 + r0:6 + r0 + rh, :] + cw[1:2, :] * cbuf[7 + r0:7 + r0 + rh, :] + cw[2:3, :] * cu)
        ya_ref[0, r0:r0 + rh, :] = (gate_b * conv).astype(BF16)
    cbuf[0:8, :] = cbuf[tm:tm + 8, :]

    for r0, h in zip(starts, hs):
        cq, suq, sdq = cq_ref[r0:r0 + rh, :], suq_ref[r0:r0 + rh, :], sdq_ref[r0:r0 + rh, :]
        pq = _dot(h, w_ref[:, base:base + nq])
        for j in range(nq // LANES):
            q_ref[0, r0:r0 + rh, j * LANES:(j + 1) * LANES] = _rope_chunk(
                pq[:, j * LANES:(j + 1) * LANES], cq, suq, sdq, half).astype(BF16)
    for r0, h in zip(starts, hs):
        ck, suk, sdk = ck_ref[r0:r0 + rh, :], suk_ref[r0:r0 + rh, :], sdk_ref[r0:r0 + rh, :]
        pk = _dot(h, w_ref[:, base + nq:base + 2 * nq])
        for j in range(nq // LANES):
            k_ref[0, r0:r0 + rh, j * LANES:(j + 1) * LANES] = _rope_chunk(
                pk[:, j * LANES:(j + 1) * LANES], ck, suk, sdk, half).astype(BF16)
    for r0, h in zip(starts, hs):
        v_ref[0, r0:r0 + rh, :] = _dot(h, w_ref[:, base + 2 * nq:]).astype(BF16)


def _hyb_front(x, g, w_in, conv_w, tabs_q, tabs_k, half):
    b, s, d = x.shape
    tm = ROW_TILE
    n = w_in.shape[1]
    nq = DIFF_HEADS * 2 * DIFF_DK
    nv = DIFF_HEADS * DIFF_DV
    tab_spec = pl.BlockSpec((tm, LANES), lambda bi, si: (si, 0))
    row = lambda width: pl.BlockSpec((1, tm, width), lambda bi, si: (bi, si, 0))
    return pl.pallas_call(
        functools.partial(_hyb_front_kernel, tm=tm, half=half),
        grid=(b, s // tm),
        in_specs=[row(d),
                  pl.BlockSpec((1, d), lambda bi, si: (0, 0)),
                  pl.BlockSpec((d, n), lambda bi, si: (0, 0)),
                  pl.BlockSpec((CONV_K, CONV_CH), lambda bi, si: (0, 0)),
                  tab_spec, tab_spec, tab_spec, tab_spec, tab_spec, tab_spec],
        out_specs=[row(CONV_CH), row(nq), row(nq), row(nv)],
        out_shape=[jax.ShapeDtypeStruct((b, s, CONV_CH), BF16),
                   jax.ShapeDtypeStruct((b, s, nq), BF16),
                   jax.ShapeDtypeStruct((b, s, nq), BF16),
                   jax.ShapeDtypeStruct((b, s, nv), BF16)],
        scratch_shapes=[pltpu.VMEM((tm + 8, CONV_CH), F32)],
        compiler_params=_cparams(("arbitrary", "arbitrary")),
        name="hyb_front",
    )(x, g.reshape(1, d), w_in, conv_w, *tabs_q, *tabs_k)


def _lane_fold(x, op):
    r = x[:, 0:LANES]
    for c in range(1, x.shape[1] // LANES):
        r = op(r, x[:, c * LANES:(c + 1) * LANES])
    return r


def _diff_attn_kernel(q_ref, k_ref, v_ref, lam_ref, g_ref, o_ref, sbuf, stat, acc, *, tq, hp, lambda_init):
    i = pl.program_id(2)
    lane = lax.broadcasted_iota(I32, (1, LANES), 1)
    qs = []
    for h in range(hp):
        q = q_ref[0, :, h * LANES:(h + 1) * LANES]
        zero = jnp.zeros_like(q)
        qs += [jnp.where(lane < DIFF_DK, q, zero), jnp.where(lane >= DIFF_DK, q, zero)]
    nc = 2 * hp

    hq = tq // 2
    diag = pl.multiple_of(i * tq, tq)

    def pass1(j, carry):
        for c in range(nc):
            h = c // 2
            kb = k_ref[0, pl.ds(pl.multiple_of(j * tq, tq), tq), h * LANES:(h + 1) * LANES]
            sc = _dot_nt(qs[c], kb)
            sbuf[c, j] = sc
            stat[c] = jnp.maximum(stat[c], _lane_fold(sc, jnp.maximum))
        return carry

    stat[...] = jnp.full(stat.shape, NEG_INF, F32)
    lax.fori_loop(0, i, pass1, 0)

    r = lax.broadcasted_iota(I32, (hq, hq), 0)
    cc = lax.broadcasted_iota(I32, (hq, hq), 1)
    tri = cc <= r
    for c in range(nc):
        h = c // 2
        k_lo = k_ref[0, pl.ds(diag, hq), h * LANES:(h + 1) * LANES]
        k_hi = k_ref[0, pl.ds(diag + hq, hq), h * LANES:(h + 1) * LANES]
        s_tl = jnp.where(tri, _dot_nt(qs[c][:hq], k_lo), NEG_INF)
        s_bl = _dot_nt(qs[c][hq:], k_lo)
        s_br = jnp.where(tri, _dot_nt(qs[c][hq:], k_hi), NEG_INF)
        sbuf[c, i, 0:hq, 0:hq] = s_tl
        sbuf[c, i, hq:tq, 0:hq] = s_bl
        sbuf[c, i, hq:tq, hq:tq] = s_br
        stat[c, 0:hq, :] = jnp.maximum(stat[c, 0:hq, :], _lane_fold(s_tl, jnp.maximum))
        stat[c, hq:tq, :] = jnp.maximum(stat[c, hq:tq, :],
                                        jnp.maximum(_lane_fold(s_bl, jnp.maximum), _lane_fold(s_br, jnp.maximum)))
    ms = [jnp.max(stat[c], axis=-1, keepdims=True) for c in range(nc)]

    ones = jnp.ones((tq, LANES), BF16)
    for h in range(hp):
        vb = v_ref[0, pl.ds(diag, tq), h * LANES:(h + 1) * LANES]
        v_ext = jnp.concatenate([vb, ones], axis=1)
        tops, bots = [], []
        for c in (2 * h, 2 * h + 1):
            tops.append(jnp.exp2(sbuf[c, i, 0:hq, 0:hq] - ms[c][:hq]))
            bots.append(jnp.exp2(sbuf[c, i, hq:tq, :] - ms[c][hq:]))
        top = _dot(jnp.concatenate(tops, axis=0).astype(BF16), v_ext[:hq])
        bot = _dot(jnp.concatenate(bots, axis=0).astype(BF16), v_ext)
        acc[h, 0:hq, :] = top[:hq]
        acc[h, hq:tq, :] = bot[:hq]
        acc[h, tq:tq + hq, :] = top[hq:]
        acc[h, tq + hq:2 * tq, :] = bot[hq:]

    def pass2(j, carry):
        for h in range(hp):
            vb = v_ref[0, pl.ds(pl.multiple_of(j * tq, tq), tq), h * LANES:(h + 1) * LANES]
            v_ext = jnp.concatenate([vb, ones], axis=1)
            p0 = jnp.exp2(sbuf[2 * h, j] - ms[2 * h])
            p1 = jnp.exp2(sbuf[2 * h + 1, j] - ms[2 * h + 1])
            acc[h] += _dot(jnp.concatenate([p0, p1], axis=0).astype(BF16), v_ext)
        return carry

    lax.fori_loop(0, i, pass2, 0)

    lf = lam_ref[...]
    lam = (jnp.exp(jnp.sum(lf[0:1] * lf[1:2], keepdims=True))
           - jnp.exp(jnp.sum(lf[2:3] * lf[3:4], keepdims=True)) + lambda_init)
    for h in range(hp):
        a0 = acc[h, 0:tq, :]
        a1 = acc[h, tq:2 * tq, :]
        o = a0[:, :LANES] / a0[:, LANES:] - lam * (a1[:, :LANES] / a1[:, LANES:])
        o_ref[0, :, h * LANES:(h + 1) * LANES] = (_rms(o, g_ref[...]) * (1.0 - lambda_init)).astype(BF16)


def _diff_attn(q, k, v, lam_vecs, subln_g, lambda_init):
    b, s, _ = q.shape
    tq = 512
    hp = 4
    blk = lambda bi, hi, qi: (bi, qi, hi)
    full = lambda bi, hi, qi: (bi, 0, hi)
    return pl.pallas_call(
        functools.partial(_diff_attn_kernel, tq=tq, hp=hp, lambda_init=lambda_init),
        grid=(b, DIFF_HEADS // hp, s // tq),
        in_specs=[pl.BlockSpec((1, tq, hp * LANES), blk),
                  pl.BlockSpec((1, s, hp * LANES), full),
                  pl.BlockSpec((1, s, hp * LANES), full),
                  pl.BlockSpec((4, DIFF_DK), lambda bi, hi, qi: (0, 0)),
                  pl.BlockSpec((1, DIFF_DV), lambda bi, hi, qi: (0, 0))],
        out_specs=pl.BlockSpec((1, tq, hp * LANES), blk),
        out_shape=jax.ShapeDtypeStruct((b, s, DIFF_HEADS * DIFF_DV), BF16),
        scratch_shapes=[pltpu.VMEM((2 * hp, s // tq, tq, tq), F32),
                        pltpu.VMEM((2 * hp, tq, LANES), F32),
                        pltpu.VMEM((hp, 2 * tq, DIFF_DV + LANES), F32)],
        compiler_params=_cparams(("arbitrary", "arbitrary", "arbitrary")),
        name="diff_attn",
    )(q, k, v, lam_vecs, subln_g.reshape(1, DIFF_DV))


def _swa_front_kernel(x_ref, g_ref, w_ref, b_ref, cq_ref, suq_ref, sdq_ref, ck_ref, suk_ref, sdk_ref,
                      q_ref, kv_ref, *, half):
    nq = SWA_Q_HEADS * SWA_HEAD_DIM
    nkv = SWA_KV_HEADS * SWA_HEAD_DIM
    tm = x_ref.shape[1]
    rh = tm // FRONT_CHAINS
    starts = [ch * rh for ch in range(FRONT_CHAINS)]
    hs = [_rms(x_ref[0, r0:r0 + rh, :], g_ref[...]).astype(BF16) for r0 in starts]
    for r0, h in zip(starts, hs):
        cq, suq, sdq = cq_ref[r0:r0 + rh, :], suq_ref[r0:r0 + rh, :], sdq_ref[r0:r0 + rh, :]
        pq = _dot(h, w_ref[:, 0:nq]) + b_ref[:, 0:nq]
        for j in range(nq // LANES):
            lo = j * LANES
            q_ref[0, r0:r0 + rh, lo:lo + LANES] = _rope_chunk(pq[:, lo:lo + LANES], cq, suq, sdq, half).astype(BF16)
    for r0, h in zip(starts, hs):
        ck, suk, sdk = ck_ref[r0:r0 + rh, :], suk_ref[r0:r0 + rh, :], sdk_ref[r0:r0 + rh, :]
        pkv = _dot(h, w_ref[:, nq:]) + b_ref[:, nq:]
        for j in range(nkv // LANES):
            lo = j * LANES
            kv_ref[0, r0:r0 + rh, lo:lo + LANES] = _rope_chunk(pkv[:, lo:lo + LANES], ck, suk, sdk, half).astype(BF16)
        kv_ref[0, r0:r0 + rh, nkv:] = pkv[:, nkv:].astype(BF16)


def _swa_front(x, g, w_qkv, b_qkv, tabs_q, tabs_k, half):
    b, s, d = x.shape
    tm = ROW_TILE
    n = w_qkv.shape[1]
    nq = SWA_Q_HEADS * SWA_HEAD_DIM
    nkv = SWA_KV_HEADS * SWA_HEAD_DIM
    tab_spec = pl.BlockSpec((tm, LANES), lambda bi, si: (si, 0))
    row = lambda width: pl.BlockSpec((1, tm, width), lambda bi, si: (bi, si, 0))
    return pl.pallas_call(
        functools.partial(_swa_front_kernel, half=half),
        grid=(b, s // tm),
        in_specs=[row(d),
                  pl.BlockSpec((1, d), lambda bi, si: (0, 0)),
                  pl.BlockSpec((d, n), lambda bi, si: (0, 0)),
                  pl.BlockSpec((1, n), lambda bi, si: (0, 0)),
                  tab_spec, tab_spec, tab_spec, tab_spec, tab_spec, tab_spec],
        out_specs=[row(nq), row(2 * nkv)],
        out_shape=[jax.ShapeDtypeStruct((b, s, nq), BF16),
                   jax.ShapeDtypeStruct((b, s, 2 * nkv), BF16)],
        compiler_params=_cparams(("arbitrary", "arbitrary")),
        name="swa_front",
    )(x, g.reshape(1, d), w_qkv, b_qkv.reshape(1, n), *tabs_q, *tabs_k)


def _swa_head_order():
    g_sz = SWA_Q_HEADS // SWA_KV_HEADS
    order = []
    for slab in range(SWA_Q_HEADS // 2):
        pair, j = slab // g_sz, slab % g_sz
        order += [(2 * pair) * g_sz + j, (2 * pair + 1) * g_sz + j]
    return order


def _swa_attn_kernel(sink_ref, q_ref, kvp_ref, kvc_ref, o_ref, *, nblk):
    i = pl.program_id(1)
    hd = SWA_HEAD_DIM
    nkv = SWA_KV_HEADS * hd
    g_sz = SWA_Q_HEADS // SWA_KV_HEADS
    kv = jnp.concatenate([kvp_ref[0], kvc_ref[0]], axis=0)
    r = lax.broadcasted_iota(I32, (BLOCK, 2 * BLOCK), 0)
    c = lax.broadcasted_iota(I32, (BLOCK, 2 * BLOCK), 1)
    rel = c - BLOCK - r
    in_win = (rel <= 0) & (rel > -SWA_WINDOW)
    lane = lax.broadcasted_iota(I32, (1, LANES), 1)
    lo_half = lane < hd
    ones = jnp.ones((2 * BLOCK, LANES), BF16)
    for n in range(nblk):
        mask = in_win & ((c >= BLOCK) | (i > 0)) if n == 0 else in_win
        keys = kv[n * BLOCK:(n + 2) * BLOCK]
        for pair in range(SWA_KV_HEADS // 2):
            k2 = keys[:, pair * LANES:(pair + 1) * LANES]
            v2 = keys[:, nkv + pair * LANES:nkv + (pair + 1) * LANES]
            v_ext = jnp.concatenate([v2, ones], axis=1)
            pieces = []
            for j in range(g_sz):
                slab = pair * g_sz + j
                qs = q_ref[0, n * BLOCK:(n + 1) * BLOCK, slab * LANES:(slab + 1) * LANES]
                zero = jnp.zeros_like(qs)
                pieces += [jnp.where(lo_half, qs, zero), jnp.where(lo_half, zero, qs)]
            sc = _dot_nt(jnp.concatenate(pieces, axis=0), k2)
            probs, tails = [], []
            for pc in range(2 * g_sz):
                scp = jnp.where(mask, sc[pc * BLOCK:(pc + 1) * BLOCK], NEG_INF)
                sink = sink_ref[2 * g_sz * pair + pc] * LOG2E
                m = jnp.maximum(jnp.max(scp, axis=-1, keepdims=True), sink)
                probs.append(jnp.exp2(scp - m).astype(BF16))
                tails.append(jnp.exp2(sink - m))
            pv = _dot(jnp.concatenate(probs, axis=0), v_ext)
            for j in range(g_sz):
                slab = pair * g_sz + j
                halves = []
                for hf in range(2):
                    pc = 2 * j + hf
                    blk = pv[pc * BLOCK:(pc + 1) * BLOCK]
                    halves.append(blk[:, :LANES] / (blk[:, LANES:] + tails[pc]))
                o_ref[0, n * BLOCK:(n + 1) * BLOCK, slab * LANES:(slab + 1) * LANES] = (
                    jnp.where(lo_half, halves[0], halves[1]).astype(BF16))


def _swa_attn(q, kv, sinks):
    b, s, nq = q.shape
    nblk = 4
    tq = nblk * BLOCK
    return pl.pallas_call(
        functools.partial(_swa_attn_kernel, nblk=nblk),
        grid_spec=pltpu.PrefetchScalarGridSpec(
            num_scalar_prefetch=1,
            grid=(b, s // tq),
            in_specs=[pl.BlockSpec((1, tq, nq), lambda bi, ni, sk: (bi, ni, 0)),
                      pl.BlockSpec((1, BLOCK, kv.shape[2]), lambda bi, ni, sk: (bi, jnp.maximum(ni * nblk - 1, 0), 0)),
                      pl.BlockSpec((1, tq, kv.shape[2]), lambda bi, ni, sk: (bi, ni, 0))],
            out_specs=pl.BlockSpec((1, tq, nq), lambda bi, ni, sk: (bi, ni, 0))),
        out_shape=jax.ShapeDtypeStruct((b, s, nq), BF16),
        compiler_params=_cparams(("arbitrary", "arbitrary")),
        name="swa_attn",
    )(sinks, q, kv, kv)


def _post_kernel(*refs, tm, n_a, has_bias):
    x_ref = refs[0]
    a_refs = refs[1:1 + n_a]
    k = 1 + n_a
    wo_refs = refs[k:k + n_a]
    k += n_a
    if has_bias:
        bo_ref = refs[k]
        k += 1
    (gx_ref, wq_ref, mkv_ref, wxo_ref, gf_ref, wr_ref, br_ref,
     x2_ref, h2_ref, mi_ref, mw_ref, cnt_ref, cnt_acc, cnt_col) = refs[k:]
    first = (pl.program_id(0) == 0) & (pl.program_id(1) == 0)

    @pl.when(first)
    def _():
        cnt_acc[...] = jnp.zeros_like(cnt_acc)
        cnt_col[...] = jnp.zeros_like(cnt_col)

    xw = XATTN_HEADS * XATTN_HEAD_DIM
    ones = jnp.ones((mkv_ref.shape[1], LANES), BF16)

    rh = tm // POST_CHAINS

    def out_proj(r0):
        acc = _dot(a_refs[0][0, r0:r0 + rh, :], wo_refs[0][...])
        for a_ref, w_ref in zip(a_refs[1:], wo_refs[1:]):
            acc = acc + _dot(a_ref[0, r0:r0 + rh, :], w_ref[...])
        if has_bias:
            acc = acc + bo_ref[...]
        return x_ref[0, r0:r0 + rh, :] + acc

    def q_proj(x1):
        hx = _rms(x1, gx_ref[...]).astype(BF16)
        return (_dot(hx, wq_ref[...]) * (XATTN_HEAD_DIM ** -0.5 * LOG2E)).astype(BF16)

    def mem_attn(qx):
        outs = []
        for hh in range(XATTN_HEADS):
            lo = hh * XATTN_HEAD_DIM
            mk = mkv_ref[0, :, lo:lo + XATTN_HEAD_DIM]
            mv = jnp.concatenate([mkv_ref[0, :, xw + lo:xw + lo + XATTN_HEAD_DIM], ones], axis=1)
            sc = _dot_nt(qx[:, lo:lo + XATTN_HEAD_DIM], mk)
            m = jnp.max(sc, axis=-1, keepdims=True)
            pv = _dot(jnp.exp2(sc - m).astype(BF16), mv)
            outs.append((pv[:, :LANES] / pv[:, LANES:]).astype(BF16))
        return jnp.concatenate(outs, axis=-1)

    def o_proj(r0, x1, ox):
        x2 = x1 + _dot(ox, wxo_ref[...])
        x2_ref[0, r0:r0 + rh, :] = x2
        return x2

    def router(r0, x2):
        h2 = _rms(x2, gf_ref[...])
        _to_token_tiles(h2_ref, h2, rh, r0)
        return _dot(h2.astype(BF16), wr_ref[...]) + br_ref[...]

    starts = [c * rh for c in range(POST_CHAINS)]
    x1s = [out_proj(r0) for r0 in starts]
    qxs = [q_proj(x1) for x1 in x1s]
    oxs = [mem_attn(qx) for qx in qxs]
    x2s = [o_proj(r0, x1, ox) for r0, x1, ox in zip(starts, x1s, oxs)]
    logits = jnp.concatenate([router(r0, x2) for r0, x2 in zip(starts, x2s)], axis=0)

    lt = logits.T
    ex = lt[0:N_EXPERTS]
    grp = lt[N_EXPERTS:N_EXPERTS + SUBLANES]
    grow = lax.broadcasted_iota(I32, grp.shape, 0).astype(F32)
    lg = jnp.where(grow < N_GROUPS, grp, NEG_INF)
    mg = jnp.max(lg, axis=0, keepdims=True)
    g_sel = jnp.min(jnp.where(lg == mg, grow, float(SUBLANES)), axis=0, keepdims=True)
    p_g = 1.0 / jnp.sum(jnp.exp(lg - mg), axis=0, keepdims=True)
    erow = lax.broadcasted_iota(I32, ex.shape, 0).astype(F32)
    e_lo = g_sel * EXPERTS_PER_GROUP
    le = jnp.where((erow >= e_lo) & (erow < e_lo + EXPERTS_PER_GROUP), ex, NEG_INF)
    big = float(N_EXPERTS)
    m1 = jnp.max(le, axis=0, keepdims=True)
    i1 = jnp.min(jnp.where(le == m1, erow, big), axis=0, keepdims=True)
    le2 = jnp.where(erow == i1, NEG_INF, le)
    m2 = jnp.max(le2, axis=0, keepdims=True)
    i2 = jnp.min(jnp.where(le2 == m2, erow, big), axis=0, keepdims=True)
    t = jnp.exp(m2 - m1)
    w1 = p_g / (1.0 + t)
    w2 = p_g * t / (1.0 + t)

    oh1 = erow == i1
    oh2 = erow == i2
    oh = jnp.where(oh1 | oh2, 1.0, 0.0)
    rr = lax.broadcasted_iota(I32, (tm, tm), 0)
    cc = lax.broadcasted_iota(I32, (tm, tm), 1)
    earlier = jnp.where(rr < cc, 1.0, 0.0).astype(BF16)
    before = _dot(oh.astype(BF16), earlier) + cnt_col[...]
    r1 = jnp.sum(jnp.where(oh1, before, 0.0), axis=0, keepdims=True)
    r2 = jnp.sum(jnp.where(oh2, before, 0.0), axis=0, keepdims=True)
    cnt_col[...] = cnt_col[...] + jnp.sum(oh, axis=1, keepdims=True)
    oh_pad = jnp.concatenate([oh, jnp.zeros((LANES - N_EXPERTS, tm), F32)], axis=0).astype(BF16)
    cnt_new = cnt_acc[...] + _dot_nt(jnp.ones((SUBLANES, tm), BF16), oh_pad)[0:1]
    cnt_acc[...] = cnt_new
    cnt_ref[...] = cnt_new.astype(I32)

    zrow = jnp.zeros((1, tm), F32)
    mi_ref[...] = jnp.concatenate([i1, i2, r1, r2, zrow, zrow, zrow, zrow], axis=0).astype(I32)
    wt = jnp.concatenate([w1, w2, jnp.zeros((LANES - TOP_K, tm), F32)], axis=0)
    mw_ref[...] = wt.T


def _post_mixer(x, a_list, wo_list, b_out, g_x, w_q, mkv, w_xo, g_f, w_rt, b_rt):
    b, s, d = x.shape
    tm = ROW_TILE
    ns = s // tm
    t = b * s
    n_a = len(a_list)
    has_bias = b_out is not None
    const2 = lambda bi, si: (0, 0)
    row = lambda width: pl.BlockSpec((1, tm, width), lambda bi, si: (bi, si, 0))
    in_specs = [row(d)] + [row(a.shape[2]) for a in a_list]
    in_specs += [pl.BlockSpec(w.shape, const2) for w in wo_list]
    args = [x, *a_list, *wo_list]
    if has_bias:
        in_specs.append(pl.BlockSpec((1, d), const2))
        args.append(b_out.reshape(1, d))
    xw = w_q.shape[1]
    in_specs += [pl.BlockSpec((1, d), const2),
                 pl.BlockSpec((d, xw), const2),
                 pl.BlockSpec((1, mkv.shape[1], mkv.shape[2]), lambda bi, si: (bi, 0, 0)),
                 pl.BlockSpec((xw, d), const2),
                 pl.BlockSpec((1, d), const2),
                 pl.BlockSpec((d, LANES), const2),
                 pl.BlockSpec((1, LANES), const2)]
    args += [g_x.reshape(1, d), w_q, mkv, w_xo, g_f.reshape(1, d), w_rt, b_rt]
    out_specs = [row(d),
                 pl.BlockSpec((tm * TOKEN_ROWS, LANES), lambda bi, si: (bi * ns + si, 0)),
                 pl.BlockSpec((8, tm), lambda bi, si: (0, bi * ns + si)),
                 pl.BlockSpec((tm, LANES), lambda bi, si: (bi * ns + si, 0)),
                 pl.BlockSpec((1, LANES), const2)]
    out_shape = [jax.ShapeDtypeStruct((b, s, d), F32),
                 jax.ShapeDtypeStruct((t * TOKEN_ROWS, LANES), U32),
                 jax.ShapeDtypeStruct((8, t), I32),
                 jax.ShapeDtypeStruct((t, LANES), F32),
                 jax.ShapeDtypeStruct((1, LANES), I32)]
    return pl.pallas_call(
        functools.partial(_post_kernel, tm=tm, n_a=n_a, has_bias=has_bias),
        grid=(b, ns),
        in_specs=in_specs,
        out_specs=out_specs,
        out_shape=out_shape,
        scratch_shapes=[pltpu.VMEM((1, LANES), F32), pltpu.VMEM((N_EXPERTS, 1), F32)],
        compiler_params=_cparams(("arbitrary", "arbitrary")),
        name="post_mixer",
    )(*args)


def _to_token_tiles(dst_ref, val, rows, tok0=0):
    half = TOKEN_ROWS * LANES
    for j in range(TOKEN_ROWS):
        hi = val[:, j * LANES:(j + 1) * LANES].astype(BF16).astype(F32)
        lo = val[:, half + j * LANES:half + (j + 1) * LANES].astype(BF16).astype(F32)
        word = lax.bitcast_convert_type(hi, U32) | (lax.bitcast_convert_type(lo, U32) >> 16)
        dst_ref[pl.ds(tok0 * TOKEN_ROWS + j, rows, stride=TOKEN_ROWS), :] = word


def _from_token_tiles(src_ref, rows, tok0=0):
    his, los = [], []
    for j in range(TOKEN_ROWS):
        word = src_ref[pl.ds(tok0 * TOKEN_ROWS + j, rows, stride=TOKEN_ROWS), :]
        his.append(lax.bitcast_convert_type(word & jnp.uint32(0xFFFF0000), F32))
        los.append(lax.bitcast_convert_type(word << 16, F32))
    return jnp.concatenate(his + los, axis=1)


def _token_slice(ref, tok):
    return ref.at[pl.ds(pl.multiple_of(tok * TOKEN_ROWS, TOKEN_ROWS), TOKEN_ROWS)]


def _pos_kernel(off_ref, mi_ref, pos_ref):
    e = mi_ref[0:TOP_K, :]
    pos = mi_ref[TOP_K:2 * TOP_K, :]
    for j in range(N_EXPERTS):
        pos = pos + jnp.where(e == j, off_ref[j], 0)
    pos_ref[...] = pos


def _sorted_positions(off, meta_i):
    t = meta_i.shape[1]
    return pl.pallas_call(
        _pos_kernel,
        grid_spec=pltpu.PrefetchScalarGridSpec(
            num_scalar_prefetch=1,
            grid=(1,),
            in_specs=[pl.BlockSpec((8, t), lambda i, off: (0, 0))],
            out_specs=pl.BlockSpec((TOP_K, t), lambda i, off: (0, 0))),
        out_shape=jax.ShapeDtypeStruct((TOP_K, t), I32),
        compiler_params=_cparams(("arbitrary",)),
        name="moe_positions",
    )(off, meta_i)


def _tile_major(pos, tm):
    t = pos.shape[1]
    return pos.reshape(TOP_K, t // tm, tm).transpose(1, 0, 2).reshape(-1)


def _dispatch_kernel(pos_ref, h_ref, xs_ref, sem, *, tm):
    def issue(g, _):
        for u in range(DMA_UNROLL):
            r = g * DMA_UNROLL + u
            for kk in range(TOP_K):
                pos = pos_ref[kk * tm + r]
                pltpu.make_async_copy(_token_slice(h_ref, r), _token_slice(xs_ref, pos), sem).start(priority=kk)
        return 0

    lax.fori_loop(0, tm // DMA_UNROLL, issue, 0)
    for _ in range(TOP_K):
        pltpu.make_async_copy(h_ref, xs_ref.at[pl.ds(0, tm * TOKEN_ROWS)], sem).wait()


def _dispatch(pos, h2t, n_rows):
    tm = DISPATCH_TILE
    t = h2t.shape[0] // TOKEN_ROWS
    return pl.pallas_call(
        functools.partial(_dispatch_kernel, tm=tm),
        grid=(t // tm,),
        in_specs=[pl.BlockSpec((TOP_K * tm,), lambda i: (i,), memory_space=pltpu.SMEM),
                  pl.BlockSpec((tm * TOKEN_ROWS, LANES), lambda i: (i, 0))],
        out_specs=pl.BlockSpec(memory_space=pl.ANY),
        scratch_shapes=[pltpu.SemaphoreType.DMA],
        out_shape=jax.ShapeDtypeStruct((n_rows * TOKEN_ROWS, LANES), U32),
        compiler_params=_cparams(("arbitrary",)),
        name="moe_dispatch",
    )(_tile_major(pos, tm), h2t)


def _expert_kernel(tblk_ref, texp_ref, tn_ref, tnxt_ref, tpar_ref, xs_ref, wgu_hbm, wdn_hbm, ys_ref,
                   wgu_bf, wdn_bf, wgu_f, wdn_f, sems, *, tm, layer):
    i = pl.program_id(0)
    n_valid = tn_ref[i]
    expert = texp_ref[i]
    slot = tpar_ref[i]
    new_expert = (i == 0) | (expert != texp_ref[jnp.maximum(i - 1, 0)])

    def weight_copies(ex, sl):
        return (pltpu.make_async_copy(wgu_hbm.at[layer, ex], wgu_f.at[sl], sems.at[sl]),
                pltpu.make_async_copy(wdn_hbm.at[layer, ex], wdn_f.at[sl], sems.at[sl]))

    @pl.when(i == 0)
    def _():
        for cp in weight_copies(expert, slot):
            cp.start()

    @pl.when(new_expert)
    def _():
        @pl.when(tnxt_ref[i] >= 0)
        def _():
            for cp in weight_copies(tnxt_ref[i], 1 - slot):
                cp.start()

        for cp in weight_copies(expert, slot):
            cp.wait()
        wgu_bf[...] = wgu_f[slot].astype(BF16)
        wdn_bf[...] = wdn_f[slot].astype(BF16)

    @pl.when(n_valid > 0)
    def _():
        rh = tm // EXPERT_CHAINS
        starts = [c * rh for c in range(EXPERT_CHAINS)]
        row = lax.broadcasted_iota(I32, (rh, 1), 0)
        gus = []
        for r0 in starts:
            x = jnp.where(row + r0 < n_valid, _from_token_tiles(xs_ref, rh, r0), 0.0).astype(BF16)
            gus.append(_dot(x, wgu_bf[...]))
        ys = []
        for gu in gus:
            hid = (gu[:, :EXPERT_HIDDEN] / (1.0 + jnp.exp(-gu[:, :EXPERT_HIDDEN])) * gu[:, EXPERT_HIDDEN:]).astype(BF16)
            ys.append(_dot(hid, wdn_bf[...]))
        for r0, y in zip(starts, ys):
            _to_token_tiles(ys_ref, y, rh, r0)


def _experts(plan, xs, w_gu, w_dn, layer):
    tm = EXPERT_TILE
    nt = xs.shape[0] // (tm * TOKEN_ROWS)
    gu_shape, dn_shape = w_gu.shape[2:], w_dn.shape[2:]
    rows = lambda i, tb, te, tn, tx, tp: (tb[i], 0)
    return pl.pallas_call(
        functools.partial(_expert_kernel, tm=tm, layer=layer),
        grid_spec=pltpu.PrefetchScalarGridSpec(
            num_scalar_prefetch=5,
            grid=(nt,),
            in_specs=[pl.BlockSpec((tm * TOKEN_ROWS, LANES), rows),
                      pl.BlockSpec(memory_space=pl.ANY),
                      pl.BlockSpec(memory_space=pl.ANY)],
            out_specs=pl.BlockSpec((tm * TOKEN_ROWS, LANES), rows),
            scratch_shapes=[pltpu.VMEM(gu_shape, BF16), pltpu.VMEM(dn_shape, BF16),
                            pltpu.VMEM((2,) + gu_shape, F32), pltpu.VMEM((2,) + dn_shape, F32),
                            pltpu.SemaphoreType.DMA((2,))]),
        out_shape=jax.ShapeDtypeStruct(xs.shape, U32),
        compiler_params=_cparams(("arbitrary",)),
        name="moe_experts",
    )(*plan, xs, w_gu, w_dn)


def _moe_combine_tile(pos_cur, pos_nxt, mw_ref, x, ys_ref, ybuf, sems, i, n, tm):
    def gather(pos_ref, slot):
        def issue(g, _):
            for u in range(DMA_UNROLL):
                r = g * DMA_UNROLL + u
                for kk in range(TOP_K):
                    pos = pos_ref[kk * tm + r]
                    pltpu.make_async_copy(_token_slice(ys_ref, pos), _token_slice(ybuf.at[slot, kk], r),
                                          sems.at[slot]).start(priority=kk)
            return 0

        lax.fori_loop(0, tm // DMA_UNROLL, issue, 0)

    @pl.when(i == 0)
    def _():
        gather(pos_cur, 0)

    @pl.when(i + 1 < n)
    def _():
        gather(pos_nxt, (i + 1) % 2)

    slot = i % 2
    for kk in range(TOP_K):
        pltpu.make_async_copy(ys_ref.at[pl.ds(0, tm * TOKEN_ROWS)], ybuf.at[slot, kk], sems.at[slot]).wait()
    mw = mw_ref[...]
    return (x + mw[:, 0:1] * _from_token_tiles(ybuf.at[slot, 0], tm)
            + mw[:, 1:2] * _from_token_tiles(ybuf.at[slot, 1], tm))


def _combine_kernel(pos_cur, pos_nxt, mw_ref, x_ref, ys_ref, *rest, tm, final):
    if final:
        g_ref, o_ref, ybuf, sems = rest
    else:
        o_ref, ybuf, sems = rest
    x3 = _moe_combine_tile(pos_cur, pos_nxt, mw_ref, x_ref[...], ys_ref, ybuf, sems,
                           pl.program_id(0), pl.num_programs(0), tm)
    if final:
        x3 = _rms(x3, g_ref[...])
    o_ref[...] = x3


def _combine(moe, g_final):
    pos, meta_w, x2, ys = moe
    d = x2.shape[-1]
    x2d = x2.reshape(-1, d)
    t = x2d.shape[0]
    tm = COMBINE_TILE
    nt = t // tm
    final = g_final is not None
    in_specs = [pl.BlockSpec((TOP_K * tm,), lambda i: (i,), memory_space=pltpu.SMEM),
                pl.BlockSpec((TOP_K * tm,), lambda i: (jnp.minimum(i + 1, nt - 1),), memory_space=pltpu.SMEM),
                pl.BlockSpec((tm, LANES), lambda i: (i, 0)),
                pl.BlockSpec((tm, d), lambda i: (i, 0)),
                pl.BlockSpec(memory_space=pl.ANY)]
    pos_flat = _tile_major(pos, tm)
    args = [pos_flat, pos_flat, meta_w, x2d, ys]
    if final:
        in_specs.append(pl.BlockSpec((1, d), lambda i: (0, 0)))
        args.append(g_final.reshape(1, d))
    return pl.pallas_call(
        functools.partial(_combine_kernel, tm=tm, final=final),
        grid=(nt,),
        in_specs=in_specs,
        out_specs=pl.BlockSpec((tm, d), lambda i: (i, 0)),
        scratch_shapes=[pltpu.VMEM((2, TOP_K, tm * TOKEN_ROWS, LANES), U32), pltpu.SemaphoreType.DMA((2,))],
        out_shape=jax.ShapeDtypeStruct((t, d), F32),
        compiler_params=_cparams(("arbitrary",)),
        name="moe_combine",
    )(*args)


def _plan_kernel(cnt_ref, off_ref, blk_ref, exp_ref, nv_ref, nxt_ref, par_ref, *, tm, n_tiles):
    shift = tm.bit_length() - 1

    def clear(i, _):
        nxt_ref[i] = -1
        return 0

    lax.fori_loop(0, n_tiles, clear, 0)

    def per_expert(e, carry):
        row0, run, prev_first = carry
        c = cnt_ref[0, e]
        ntile = lax.shift_right_logical(c + (tm - 1), shift)
        off_ref[e] = row0
        t0 = lax.shift_right_logical(row0, shift)

        def fill(j, _):
            blk_ref[t0 + j] = t0 + j
            exp_ref[t0 + j] = e
            nv_ref[t0 + j] = jnp.minimum(c - j * tm, tm)
            par_ref[t0 + j] = run & 1
            return 0

        lax.fori_loop(0, ntile, fill, 0)
        has = ntile > 0

        @pl.when(has & (prev_first >= 0))
        def _():
            nxt_ref[prev_first] = e

        return (row0 + lax.shift_left(ntile, shift), run + has.astype(I32), jnp.where(has, t0, prev_first))

    total, _, _ = lax.fori_loop(0, N_EXPERTS, per_expert, (jnp.int32(0), jnp.int32(0), jnp.int32(-1)))
    used = lax.shift_right_logical(total, shift)
    last = jnp.maximum(used - 1, 0)

    def tail(i, _):
        blk_ref[i] = last
        exp_ref[i] = exp_ref[last]
        nv_ref[i] = 0
        par_ref[i] = par_ref[last]
        return 0

    lax.fori_loop(used, n_tiles, tail, 0)


def _moe_plan(counts, n_tiles):
    tm = EXPERT_TILE
    assert tm & (tm - 1) == 0
    smem = pl.BlockSpec(memory_space=pltpu.SMEM)
    return pl.pallas_call(
        functools.partial(_plan_kernel, tm=tm, n_tiles=n_tiles),
        in_specs=[smem],
        out_specs=[smem] * 6,
        out_shape=[jax.ShapeDtypeStruct((N_EXPERTS,), I32)] + [jax.ShapeDtypeStruct((n_tiles,), I32)] * 5,
        name="moe_plan",
    )(counts)


def _moe(x2, h2t, meta_i, meta_w, counts, w_gu, w_dn, layer):
    b, s, d = x2.shape
    assert d == 2 * TOKEN_ROWS * LANES
    n_rows = b * s * TOP_K + N_EXPERTS * EXPERT_TILE
    off, *tile_plan = _moe_plan(counts, n_rows // EXPERT_TILE)
    pos = _sorted_positions(off, meta_i)
    xs = _dispatch(pos, h2t, n_rows)
    ys = _experts(tile_plan, xs, w_gu, w_dn, layer)
    return pos, meta_w, x2, ys


def _router_weights(w_group, b_group, w_router, b_router):
    d = w_group.shape[0]
    pad = LANES - N_EXPERTS - N_GROUPS
    w = jnp.concatenate([w_router, w_group, jnp.zeros((d, pad), F32)], axis=1)
    bias = jnp.concatenate([b_router, b_group, jnp.zeros((pad,), F32)]).reshape(1, LANES)
    return w.astype(BF16), bias


def kernel(x, mem, mem_norm, mem_w_kv, norm_mix, norm_xattn, norm_ffn, hyb_w_in, hyb_conv_w, diff_lambda, diff_subln, hyb_w_out, swa_w_qkv, swa_b_qkv, swa_sinks, swa_w_out, swa_b_out, xattn_w_q, xattn_w_o, moe_w_group, moe_b_group, moe_w_router, moe_b_router, moe_w_gate_up, moe_w_down, final_norm):
    b, s, d = x.shape
    m = mem.shape[1]
    depth = norm_mix.shape[0]
    assert d == 2 * TOKEN_ROWS * LANES and s % ROW_TILE == 0 and s % (4 * BLOCK) == 0
    assert (b * s) % DISPATCH_TILE == 0 and (b * s) % COMBINE_TILE == 0 and DISPATCH_TILE % DMA_UNROLL == 0
    mkv = _norm_proj(mem.reshape(b * m, d), mem_norm, mem_w_kv.astype(BF16), m).reshape(b, m, -1)

    scale = DIFF_DK ** -0.5 * LOG2E
    cq, suq, sdq, half = _rope_lane_tables(s, DIFF_DK, scale)
    ck, suk, sdk, _ = _rope_lane_tables(s, DIFF_DK, 1.0)
    tabs_q, tabs_k = (cq, suq, sdq), (ck, suk, sdk)

    moe = None
    for l in range(depth):
        if moe is not None:
            x = _combine(moe, None).reshape(b, s, d)
        if l % 2 == 0:
            e = l // 2
            lambda_init = 0.8 - 0.6 * math.exp(-0.3 * l)
            ya, q, k, v = _hyb_front(x, norm_mix[l], hyb_w_in[e].astype(BF16), hyb_conv_w[e], tabs_q, tabs_k, half)
            o = _diff_attn(q, k, v, diff_lambda[e], diff_subln[e], lambda_init)
            w_out = hyb_w_out[e].astype(BF16)
            a_list, wo_list, b_out = [ya, o], [w_out[:CONV_CH], w_out[CONV_CH:]], None
        else:
            e = l // 2
            order = _swa_head_order()
            hd, nq = SWA_HEAD_DIM, SWA_Q_HEADS * SWA_HEAD_DIM
            heads = lambda a, axis: [lax.slice_in_dim(a, h * hd, (h + 1) * hd, axis=axis) for h in order]
            w_all = swa_w_qkv[e].astype(BF16)
            w_qkv = jnp.concatenate(heads(w_all, 1) + [w_all[:, nq:]], axis=1)
            b_qkv = jnp.concatenate(heads(swa_b_qkv[e], 0) + [swa_b_qkv[e][nq:]])
            sinks = jnp.stack([swa_sinks[e][h] for h in order])
            q, kv = _swa_front(x, norm_mix[l], w_qkv, b_qkv, tabs_q, tabs_k, half)
            o = _swa_attn(q, kv, sinks)
            w_out = jnp.concatenate(heads(swa_w_out[e].astype(BF16), 0), axis=0)
            a_list, wo_list, b_out = [o], [w_out], swa_b_out[e]
        w_rt, b_rt = _router_weights(moe_w_group[l], moe_b_group[l], moe_w_router[l], moe_b_router[l])
        x2, h2, meta_i, meta_w, counts = _post_mixer(
            x, a_list, wo_list, b_out, norm_xattn[l], xattn_w_q[l].astype(BF16), mkv,
            xattn_w_o[l].astype(BF16), norm_ffn[l], w_rt, b_rt)
        moe = _moe(x2, h2, meta_i, meta_w, counts, moe_w_gate_up, moe_w_down, l)
    return _combine(moe, final_norm).reshape(b, s, d)
```

```python
import functools
import math

import jax
import jax.numpy as jnp
from jax import lax
from jax.experimental import pallas as pl
from jax.experimental.pallas import tpu as pltpu

F32 = jnp.float32
BF16 = jnp.bfloat16
I32 = jnp.int32
U32 = jnp.uint32

EPS = 1e-6
LANES = 128
SUBLANES = 8
TOKEN_ROWS = 4
DMA_UNROLL = 16
VMEM_LIMIT = 56 * 1024 * 1024

ROPE_THETA = 500000.0
ROPE_FRACTION = 4
BLOCK = 128
CONV_CH = 512
CONV_K = 3
DIFF_HEADS = 4
DIFF_DK = 64
DIFF_DV = 128
SWA_Q_HEADS = 16
SWA_KV_HEADS = 4
SWA_HEAD_DIM = 64
SWA_WINDOW = 128
XATTN_HEADS = 4
XATTN_HEAD_DIM = 128
N_GROUPS = 4
EXPERTS_PER_GROUP = 8
N_EXPERTS = N_GROUPS * EXPERTS_PER_GROUP
TOP_K = 2
EXPERT_HIDDEN = 512

ROW_TILE = 1024
EXPERT_TILE = 512
DISPATCH_TILE = 4096
COMBINE_TILE = 256
POST_CHAINS = 2
EXPERT_CHAINS = 2
FRONT_CHAINS = 2
NEG_INF = float("-inf")
LOG2E = math.log2(math.e)


def _cparams(sem):
    return pltpu.CompilerParams(dimension_semantics=sem, vmem_limit_bytes=VMEM_LIMIT)


def _rms(x, g):
    return x * lax.rsqrt(jnp.mean(x * x, axis=-1, keepdims=True) + EPS) * g


def _dot(a, b):
    return jnp.dot(a, b, preferred_element_type=F32)


def _dot_nt(a, b):
    return lax.dot_general(a, b, (((1,), (1,)), ((), ())), preferred_element_type=F32)


def _rope_lane_tables(seq, head_dim, scale):
    rot = head_dim // ROPE_FRACTION
    half = rot // 2
    pos = jnp.arange(seq, dtype=F32)
    inv = ROPE_THETA ** (-jnp.arange(0, rot, 2, dtype=F32) / rot)
    ang = pos[:, None] * inv[None, :]
    cos, sin = jnp.cos(ang), jnp.sin(ang)
    idx = jnp.arange(LANES) % head_dim
    cl = jnp.take(cos, idx % half, axis=1)
    sl = jnp.take(sin, idx % half, axis=1)
    c = jnp.where(idx < rot, cl, 1.0) * scale
    s_up = jnp.where(idx < half, -sl, 0.0) * scale
    s_dn = jnp.where((idx >= half) & (idx < rot), sl, 0.0) * scale
    return c.astype(F32), s_up.astype(F32), s_dn.astype(F32), half


def _rope_chunk(xc, c, s_up, s_dn, half):
    return (xc * c + pltpu.roll(xc, LANES - half, 1) * s_up + pltpu.roll(xc, half, 1) * s_dn)


def _norm_proj_kernel(x_ref, g_ref, w_ref, o_ref):
    h = _rms(x_ref[...], g_ref[...]).astype(BF16)
    o_ref[...] = _dot(h, w_ref[...]).astype(o_ref.dtype)


def _norm_proj(x2d, g, w_bf16, tm):
    m, d = x2d.shape
    n = w_bf16.shape[1]
    return pl.pallas_call(
        _norm_proj_kernel,
        grid=(m // tm,),
        in_specs=[pl.BlockSpec((tm, d), lambda i: (i, 0)),
                  pl.BlockSpec((1, d), lambda i: (0, 0)),
                  pl.BlockSpec((d, n), lambda i: (0, 0))],
        out_specs=pl.BlockSpec((tm, n), lambda i: (i, 0)),
        out_shape=jax.ShapeDtypeStruct((m, n), BF16),
        compiler_params=_cparams(("arbitrary",)),
        name="mem_kv_proj",
    )(x2d, g.reshape(1, d), w_bf16)


def _hyb_front_kernel(x_ref, g_ref, w_ref, cw_ref, cq_ref, suq_ref, sdq_ref, ck_ref, suk_ref, sdk_ref,
                      ya_ref, q_ref, k_ref, v_ref, cbuf, *, tm, half):
    s = pl.program_id(1)
    c = CONV_CH
    base = 3 * c
    nq = DIFF_HEADS * 2 * DIFF_DK
    rh = tm // FRONT_CHAINS
    starts = [ch * rh for ch in range(FRONT_CHAINS)]
    cw = cw_ref[...]

    @pl.when(s == 0)
    def _():
        cbuf[0:8, :] = jnp.zeros((8, c), F32)

    hs = [_rms(x_ref[0, r0:r0 + rh, :], g_ref[...]).astype(BF16) for r0 in starts]
    for r0, h in zip(starts, hs):
        gate_b = _dot(h, w_ref[:, 0:c])
        cu = _dot(h, w_ref[:, c:2 * c]) * _dot(h, w_ref[:, 2 * c:3 * c])
        cbuf[8 + r0:8 + r0 + rh, :] = cu
        conv = (cw[0:1, :] * cbuf[6 + r0:6 + r0 + rh, :] + cw[1:2, :] * cbuf[7 + r0:7 + r0 + rh, :] + cw[2:3, :] * cu)
        ya_ref[0, r0:r0 + rh, :] = (gate_b * conv).astype(BF16)
    cbuf[0:8, :] = cbuf[tm:tm + 8, :]

    for r0, h in zip(starts, hs):
        cq, suq, sdq = cq_ref[r0:r0 + rh, :], suq_ref[r0:r0 + rh, :], sdq_ref[r0:r0 + rh, :]
        pq = _dot(h, w_ref[:, base:base + nq])
        for j in range(nq // LANES):
            q_ref[0, r0:r0 + rh, j * LANES:(j + 1) * LANES] = _rope_chunk(
                pq[:, j * LANES:(j + 1) * LANES], cq, suq, sdq, half).astype(BF16)
    for r0, h in zip(starts, hs):
        ck, suk, sdk = ck_ref[r0:r0 + rh, :], suk_ref[r0:r0 + rh, :], sdk_ref[r0:r0 + rh, :]
        pk = _dot(h, w_ref[:, base + nq:base + 2 * nq])
        for j in range(nq // LANES):
            k_ref[0, r0:r0 + rh, j * LANES:(j + 1) * LANES] = _rope_chunk(
                pk[:, j * LANES:(j + 1) * LANES], ck, suk, sdk, half).astype(BF16)
    for r0, h in zip(starts, hs):
        v_ref[0, r0:r0 + rh, :] = _dot(h, w_ref[:, base + 2 * nq:]).astype(BF16)


def _hyb_front(x, g, w_in, conv_w, tabs_q, tabs_k, half):
    b, s, d = x.shape
    tm = ROW_TILE
    n = w_in.shape[1]
    nq = DIFF_HEADS * 2 * DIFF_DK
    nv = DIFF_HEADS * DIFF_DV
    tab_spec = pl.BlockSpec((tm, LANES), lambda bi, si: (si, 0))
    row = lambda width: pl.BlockSpec((1, tm, width), lambda bi, si: (bi, si, 0))
    return pl.pallas_call(
        functools.partial(_hyb_front_kernel, tm=tm, half=half),
        grid=(b, s // tm),
        in_specs=[row(d),
                  pl.BlockSpec((1, d), lambda bi, si: (0, 0)),
                  pl.BlockSpec((d, n), lambda bi, si: (0, 0)),
                  pl.BlockSpec((CONV_K, CONV_CH), lambda bi, si: (0, 0)),
                  tab_spec, tab_spec, tab_spec, tab_spec, tab_spec, tab_spec],
        out_specs=[row(CONV_CH), row(nq), row(nq), row(nv)],
        out_shape=[jax.ShapeDtypeStruct((b, s, CONV_CH), BF16),
                   jax.ShapeDtypeStruct((b, s, nq), BF16),
                   jax.ShapeDtypeStruct((b, s, nq), BF16),
                   jax.ShapeDtypeStruct((b, s, nv), BF16)],
        scratch_shapes=[pltpu.VMEM((tm + 8, CONV_CH), F32)],
        compiler_params=_cparams(("arbitrary", "arbitrary")),
        name="hyb_front",
    )(x, g.reshape(1, d), w_in, conv_w, *tabs_q, *tabs_k)


def _lane_fold(x, op):
    r = x[:, 0:LANES]
    for c in range(1, x.shape[1] // LANES):
        r = op(r, x[:, c * LANES:(c + 1) * LANES])
    return r


def _diff_attn_kernel(q_ref, k_ref, v_ref, lam_ref, g_ref, o_ref, sbuf, stat, acc, *, tq, hp, lambda_init):
    i = pl.program_id(2)
    lane = lax.broadcasted_iota(I32, (1, LANES), 1)
    qs = []
    for h in range(hp):
        q = q_ref[0, :, h * LANES:(h + 1) * LANES]
        zero = jnp.zeros_like(q)
        qs += [jnp.where(lane < DIFF_DK, q, zero), jnp.where(lane >= DIFF_DK, q, zero)]
    nc = 2 * hp

    hq = tq // 2
    diag = pl.multiple_of(i * tq, tq)

    def pass1(j, carry):
        for c in range(nc):
            h = c // 2
            kb = k_ref[0, pl.ds(pl.multiple_of(j * tq, tq), tq), h * LANES:(h + 1) * LANES]
            sc = _dot_nt(qs[c], kb)
            sbuf[c, j] = sc
            stat[c] = jnp.maximum(stat[c], _lane_fold(sc, jnp.maximum))
        return carry

    stat[...] = jnp.full(stat.shape, NEG_INF, F32)
    lax.fori_loop(0, i, pass1, 0)

    r = lax.broadcasted_iota(I32, (hq, hq), 0)
    cc = lax.broadcasted_iota(I32, (hq, hq), 1)
    tri = cc <= r
    for c in range(nc):
        h = c // 2
        k_lo = k_ref[0, pl.ds(diag, hq), h * LANES:(h + 1) * LANES]
        k_hi = k_ref[0, pl.ds(diag + hq, hq), h * LANES:(h + 1) * LANES]
        s_tl = jnp.where(tri, _dot_nt(qs[c][:hq], k_lo), NEG_INF)
        s_bl = _dot_nt(qs[c][hq:], k_lo)
        s_br = jnp.where(tri, _dot_nt(qs[c][hq:], k_hi), NEG_INF)
        sbuf[c, i, 0:hq, 0:hq] = s_tl
        sbuf[c, i, hq:tq, 0:hq] = s_bl
        sbuf[c, i, hq:tq, hq:tq] = s_br
        stat[c, 0:hq, :] = jnp.maximum(stat[c, 0:hq, :], _lane_fold(s_tl, jnp.maximum))
        stat[c, hq:tq, :] = jnp.maximum(stat[c, hq:tq, :],
                                        jnp.maximum(_lane_fold(s_bl, jnp.maximum), _lane_fold(s_br, jnp.maximum)))
    ms = [jnp.max(stat[c], axis=-1, keepdims=True) for c in range(nc)]

    ones = jnp.ones((tq, LANES), BF16)
    for h in range(hp):
        vb = v_ref[0, pl.ds(diag, tq), h * LANES:(h + 1) * LANES]
        v_ext = jnp.concatenate([vb, ones], axis=1)
        tops, bots = [], []
        for c in (2 * h, 2 * h + 1):
            tops.append(jnp.exp2(sbuf[c, i, 0:hq, 0:hq] - ms[c][:hq]))
            bots.append(jnp.exp2(sbuf[c, i, hq:tq, :] - ms[c][hq:]))
        top = _dot(jnp.concatenate(tops, axis=0).astype(BF16), v_ext[:hq])
        bot = _dot(jnp.concatenate(bots, axis=0).astype(BF16), v_ext)
        acc[h, 0:hq, :] = top[:hq]
        acc[h, hq:tq, :] = bot[:hq]
        acc[h, tq:tq + hq, :] = top[hq:]
        acc[h, tq + hq:2 * tq, :] = bot[hq:]

    def pass2(j, carry):
        for h in range(hp):
            vb = v_ref[0, pl.ds(pl.multiple_of(j * tq, tq), tq), h * LANES:(h + 1) * LANES]
            v_ext = jnp.concatenate([vb, ones], axis=1)
            p0 = jnp.exp2(sbuf[2 * h, j] - ms[2 * h])
            p1 = jnp.exp2(sbuf[2 * h + 1, j] - ms[2 * h + 1])
            acc[h] += _dot(jnp.concatenate([p0, p1], axis=0).astype(BF16), v_ext)
        return carry

    lax.fori_loop(0, i, pass2, 0)

    lf = lam_ref[...]
    lam = (jnp.exp(jnp.sum(lf[0:1] * lf[1:2], keepdims=True))
           - jnp.exp(jnp.sum(lf[2:3] * lf[3:4], keepdims=True)) + lambda_init)
    for h in range(hp):
        a0 = acc[h, 0:tq, :]
        a1 = acc[h, tq:2 * tq, :]
        o = a0[:, :LANES] / a0[:, LANES:] - lam * (a1[:, :LANES] / a1[:, LANES:])
        o_ref[0, :, h * LANES:(h + 1) * LANES] = (_rms(o, g_ref[...]) * (1.0 - lambda_init)).astype(BF16)


def _diff_attn(q, k, v, lam_vecs, subln_g, lambda_init):
    b, s, _ = q.shape
    tq = 512
    hp = 4
    blk = lambda bi, hi, qi: (bi, qi, hi)
    full = lambda bi, hi, qi: (bi, 0, hi)
    return pl.pallas_call(
        functools.partial(_diff_attn_kernel, tq=tq, hp=hp, lambda_init=lambda_init),
        grid=(b, DIFF_HEADS // hp, s // tq),
        in_specs=[pl.BlockSpec((1, tq, hp * LANES), blk),
                  pl.BlockSpec((1, s, hp * LANES), full),
                  pl.BlockSpec((1, s, hp * LANES), full),
                  pl.BlockSpec((4, DIFF_DK), lambda bi, hi, qi: (0, 0)),
                  pl.BlockSpec((1, DIFF_DV), lambda bi, hi, qi: (0, 0))],
        out_specs=pl.BlockSpec((1, tq, hp * LANES), blk),
        out_shape=jax.ShapeDtypeStruct((b, s, DIFF_HEADS * DIFF_DV), BF16),
        scratch_shapes=[pltpu.VMEM((2 * hp, s // tq, tq, tq), F32),
                        pltpu.VMEM((2 * hp, tq, LANES), F32),
                        pltpu.VMEM((hp, 2 * tq, DIFF_DV + LANES), F32)],
        compiler_params=_cparams(("arbitrary", "arbitrary", "arbitrary")),
        name="diff_attn",
    )(q, k, v, lam_vecs, subln_g.reshape(1, DIFF_DV))


def _swa_front_kernel(x_ref, g_ref, w_ref, b_ref, cq_ref, suq_ref, sdq_ref, ck_ref, suk_ref, sdk_ref,
                      q_ref, kv_ref, *, half):
    nq = SWA_Q_HEADS * SWA_HEAD_DIM
    nkv = SWA_KV_HEADS * SWA_HEAD_DIM
    tm = x_ref.shape[1]
    rh = tm // FRONT_CHAINS
    starts = [ch * rh for ch in range(FRONT_CHAINS)]
    hs = [_rms(x_ref[0, r0:r0 + rh, :], g_ref[...]).astype(BF16) for r0 in starts]
    for r0, h in zip(starts, hs):
        cq, suq, sdq = cq_ref[r0:r0 + rh, :], suq_ref[r0:r0 + rh, :], sdq_ref[r0:r0 + rh, :]
        pq = _dot(h, w_ref[:, 0:nq]) + b_ref[:, 0:nq]
        for j in range(nq // LANES):
            lo = j * LANES
            q_ref[0, r0:r0 + rh, lo:lo + LANES] = _rope_chunk(pq[:, lo:lo + LANES], cq, suq, sdq, half).astype(BF16)
    for r0, h in zip(starts, hs):
        ck, suk, sdk = ck_ref[r0:r0 + rh, :], suk_ref[r0:r0 + rh, :], sdk_ref[r0:r0 + rh, :]
        pkv = _dot(h, w_ref[:, nq:]) + b_ref[:, nq:]
        for j in range(nkv // LANES):
            lo = j * LANES
            kv_ref[0, r0:r0 + rh, lo:lo + LANES] = _rope_chunk(pkv[:, lo:lo + LANES], ck, suk, sdk, half).astype(BF16)
        kv_ref[0, r0:r0 + rh, nkv:] = pkv[:, nkv:].astype(BF16)


def _swa_front(x, g, w_qkv, b_qkv, tabs_q, tabs_k, half):
    b, s, d = x.shape
    tm = ROW_TILE
    n = w_qkv.shape[1]
    nq = SWA_Q_HEADS * SWA_HEAD_DIM
    nkv = SWA_KV_HEADS * SWA_HEAD_DIM
    tab_spec = pl.BlockSpec((tm, LANES), lambda bi, si: (si, 0))
    row = lambda width: pl.BlockSpec((1, tm, width), lambda bi, si: (bi, si, 0))
    return pl.pallas_call(
        functools.partial(_swa_front_kernel, half=half),
        grid=(b, s // tm),
        in_specs=[row(d),
                  pl.BlockSpec((1, d), lambda bi, si: (0, 0)),
                  pl.BlockSpec((d, n), lambda bi, si: (0, 0)),
                  pl.BlockSpec((1, n), lambda bi, si: (0, 0)),
                  tab_spec, tab_spec, tab_spec, tab_spec, tab_spec, tab_spec],
        out_specs=[row(nq), row(2 * nkv)],
        out_shape=[jax.ShapeDtypeStruct((b, s, nq), BF16),
                   jax.ShapeDtypeStruct((b, s, 2 * nkv), BF16)],
        compiler_params=_cparams(("arbitrary", "arbitrary")),
        name="swa_front",
    )(x, g.reshape(1, d), w_qkv, b_qkv.reshape(1, n), *tabs_q, *tabs_k)


def _swa_head_order():
    g_sz = SWA_Q_HEADS // SWA_KV_HEADS
    order = []
    for slab in range(SWA_Q_HEADS // 2):
        pair, j = slab // g_sz, slab % g_sz
        order += [(2 * pair) * g_sz + j, (2 * pair + 1) * g_sz + j]
    return order


def _swa_attn_kernel(sink_ref, q_ref, kvp_ref, kvc_ref, o_ref, *, nblk):
    i = pl.program_id(1)
    hd = SWA_HEAD_DIM
    nkv = SWA_KV_HEADS * hd
    g_sz = SWA_Q_HEADS // SWA_KV_HEADS
    kv = jnp.concatenate([kvp_ref[0], kvc_ref[0]], axis=0)
    r = lax.broadcasted_iota(I32, (BLOCK, 2 * BLOCK), 0)
    c = lax.broadcasted_iota(I32, (BLOCK, 2 * BLOCK), 1)
    rel = c - BLOCK - r
    in_win = (rel <= 0) & (rel > -SWA_WINDOW)
    lane = lax.broadcasted_iota(I32, (1, LANES), 1)
    lo_half = lane < hd
    ones = jnp.ones((2 * BLOCK, LANES), BF16)
    for n in range(nblk):
        mask = in_win & ((c >= BLOCK) | (i > 0)) if n == 0 else in_win
        keys = kv[n * BLOCK:(n + 2) * BLOCK]
        for pair in range(SWA_KV_HEADS // 2):
            k2 = keys[:, pair * LANES:(pair + 1) * LANES]
            v2 = keys[:, nkv + pair * LANES:nkv + (pair + 1) * LANES]
            v_ext = jnp.concatenate([v2, ones], axis=1)
            pieces = []
            for j in range(g_sz):
                slab = pair * g_sz + j
                qs = q_ref[0, n * BLOCK:(n + 1) * BLOCK, slab * LANES:(slab + 1) * LANES]
                zero = jnp.zeros_like(qs)
                pieces += [jnp.where(lo_half, qs, zero), jnp.where(lo_half, zero, qs)]
            sc = _dot_nt(jnp.concatenate(pieces, axis=0), k2)
            probs, tails = [], []
            for pc in range(2 * g_sz):
                scp = jnp.where(mask, sc[pc * BLOCK:(pc + 1) * BLOCK], NEG_INF)
                sink = sink_ref[2 * g_sz * pair + pc] * LOG2E
                m = jnp.maximum(jnp.max(scp, axis=-1, keepdims=True), sink)
                probs.append(jnp.exp2(scp - m).astype(BF16))
                tails.append(jnp.exp2(sink - m))
            pv = _dot(jnp.concatenate(probs, axis=0), v_ext)
            for j in range(g_sz):
                slab = pair * g_sz + j
                halves = []
                for hf in range(2):
                    pc = 2 * j + hf
                    blk = pv[pc * BLOCK:(pc + 1) * BLOCK]
                    halves.append(blk[:, :LANES] / (blk[:, LANES:] + tails[pc]))
                o_ref[0, n * BLOCK:(n + 1) * BLOCK, slab * LANES:(slab + 1) * LANES] = (
                    jnp.where(lo_half, halves[0], halves[1]).astype(BF16))


def _swa_attn(q, kv, sinks):
    b, s, nq = q.shape
    nblk = 4
    tq = nblk * BLOCK
    return pl.pallas_call(
        functools.partial(_swa_attn_kernel, nblk=nblk),
        grid_spec=pltpu.PrefetchScalarGridSpec(
            num_scalar_prefetch=1,
            grid=(b, s // tq),
            in_specs=[pl.BlockSpec((1, tq, nq), lambda bi, ni, sk: (bi, ni, 0)),
                      pl.BlockSpec((1, BLOCK, kv.shape[2]), lambda bi, ni, sk: (bi, jnp.maximum(ni * nblk - 1, 0), 0)),
                      pl.BlockSpec((1, tq, kv.shape[2]), lambda bi, ni, sk: (bi, ni, 0))],
            out_specs=pl.BlockSpec((1, tq, nq), lambda bi, ni, sk: (bi, ni, 0))),
        out_shape=jax.ShapeDtypeStruct((b, s, nq), BF16),
        compiler_params=_cparams(("arbitrary", "arbitrary")),
        name="swa_attn",
    )(sinks, q, kv, kv)


def _post_kernel(*refs, tm, n_a, has_bias):
    x_ref = refs[0]
    a_refs = refs[1:1 + n_a]
    k = 1 + n_a
    wo_refs = refs[k:k + n_a]
    k += n_a
    if has_bias:
        bo_ref = refs[k]
        k += 1
    (gx_ref, wq_ref, mkv_ref, wxo_ref, gf_ref, wr_ref, br_ref,
     x2_ref, h2_ref, mi_ref, mw_ref, cnt_ref, cnt_acc, cnt_col) = refs[k:]
    first = (pl.program_id(0) == 0) & (pl.program_id(1) == 0)

    @pl.when(first)
    def _():
        cnt_acc[...] = jnp.zeros_like(cnt_acc)
        cnt_col[...] = jnp.zeros_like(cnt_col)

    xw = XATTN_HEADS * XATTN_HEAD_DIM
    ones = jnp.ones((mkv_ref.shape[1], LANES), BF16)

    rh = tm // POST_CHAINS

    def out_proj(r0):
        acc = _dot(a_refs[0][0, r0:r0 + rh, :], wo_refs[0][...])
        for a_ref, w_ref in zip(a_refs[1:], wo_refs[1:]):
            acc = acc + _dot(a_ref[0, r0:r0 + rh, :], w_ref[...])
        if has_bias:
            acc = acc + bo_ref[...]
        return x_ref[0, r0:r0 + rh, :] + acc

    def q_proj(x1):
        hx = _rms(x1, gx_ref[...]).astype(BF16)
        return (_dot(hx, wq_ref[...]) * (XATTN_HEAD_DIM ** -0.5 * LOG2E)).astype(BF16)

    def mem_attn(qx):
        outs = []
        for hh in range(XATTN_HEADS):
            lo = hh * XATTN_HEAD_DIM
            mk = mkv_ref[0, :, lo:lo + XATTN_HEAD_DIM]
            mv = jnp.concatenate([mkv_ref[0, :, xw + lo:xw + lo + XATTN_HEAD_DIM], ones], axis=1)
            sc = _dot_nt(qx[:, lo:lo + XATTN_HEAD_DIM], mk)
            m = jnp.max(sc, axis=-1, keepdims=True)
            pv = _dot(jnp.exp2(sc - m).astype(BF16), mv)
            outs.append((pv[:, :LANES] / pv[:, LANES:]).astype(BF16))
        return jnp.concatenate(outs, axis=-1)

    def o_proj(r0, x1, ox):
        x2 = x1 + _dot(ox, wxo_ref[...])
        x2_ref[0, r0:r0 + rh, :] = x2
        return x2

    def router(r0, x2):
        h2 = _rms(x2, gf_ref[...])
        _to_token_tiles(h2_ref, h2, rh, r0)
        return _dot(h2.astype(BF16), wr_ref[...]) + br_ref[...]

    starts = [c * rh for c in range(POST_CHAINS)]
    x1s = [out_proj(r0) for r0 in starts]
    qxs = [q_proj(x1) for x1 in x1s]
    oxs = [mem_attn(qx) for qx in qxs]
    x2s = [o_proj(r0, x1, ox) for r0, x1, ox in zip(starts, x1s, oxs)]
    logits = jnp.concatenate([router(r0, x2) for r0, x2 in zip(starts, x2s)], axis=0)

    lt = logits.T
    ex = lt[0:N_EXPERTS]
    grp = lt[N_EXPERTS:N_EXPERTS + SUBLANES]
    grow = lax.broadcasted_iota(I32, grp.shape, 0).astype(F32)
    lg = jnp.where(grow < N_GROUPS, grp, NEG_INF)
    mg = jnp.max(lg, axis=0, keepdims=True)
    g_sel = jnp.min(jnp.where(lg == mg, grow, float(SUBLANES)), axis=0, keepdims=True)
    p_g = 1.0 / jnp.sum(jnp.exp(lg - mg), axis=0, keepdims=True)
    erow = lax.broadcasted_iota(I32, ex.shape, 0).astype(F32)
    e_lo = g_sel * EXPERTS_PER_GROUP
    le = jnp.where((erow >= e_lo) & (erow < e_lo + EXPERTS_PER_GROUP), ex, NEG_INF)
    big = float(N_EXPERTS)
    m1 = jnp.max(le, axis=0, keepdims=True)
    i1 = jnp.min(jnp.where(le == m1, erow, big), axis=0, keepdims=True)
    le2 = jnp.where(erow == i1, NEG_INF, le)
    m2 = jnp.max(le2, axis=0, keepdims=True)
    i2 = jnp.min(jnp.where(le2 == m2, erow, big), axis=0, keepdims=True)
    t = jnp.exp(m2 - m1)
    w1 = p_g / (1.0 + t)
    w2 = p_g * t / (1.0 + t)

    oh1 = erow == i1
    oh2 = erow == i2
    oh = jnp.where(oh1 | oh2, 1.0, 0.0)
    rr = lax.broadcasted_iota(I32, (tm, tm), 0)
    cc = lax.broadcasted_iota(I32, (tm, tm), 1)
    earlier = jnp.where(rr < cc, 1.0, 0.0).astype(BF16)
    before = _dot(oh.astype(BF16), earlier) + cnt_col[...]
    r1 = jnp.sum(jnp.where(oh1, before, 0.0), axis=0, keepdims=True)
    r2 = jnp.sum(jnp.where(oh2, before, 0.0), axis=0, keepdims=True)
    cnt_col[...] = cnt_col[...] + jnp.sum(oh, axis=1, keepdims=True)
    oh_pad = jnp.concatenate([oh, jnp.zeros((LANES - N_EXPERTS, tm), F32)], axis=0).astype(BF16)
    cnt_new = cnt_acc[...] + _dot_nt(jnp.ones((SUBLANES, tm), BF16), oh_pad)[0:1]
    cnt_acc[...] = cnt_new
    cnt_ref[...] = cnt_new.astype(I32)

    zrow = jnp.zeros((1, tm), F32)
    mi_ref[...] = jnp.concatenate([i1, i2, r1, r2, zrow, zrow, zrow, zrow], axis=0).astype(I32)
    wt = jnp.concatenate([w1, w2, jnp.zeros((LANES - TOP_K, tm), F32)], axis=0)
    mw_ref[...] = wt.T


def _post_mixer(x, a_list, wo_list, b_out, g_x, w_q, mkv, w_xo, g_f, w_rt, b_rt):
    b, s, d = x.shape
    tm = ROW_TILE
    ns = s // tm
    t = b * s
    n_a = len(a_list)
    has_bias = b_out is not None
    const2 = lambda bi, si: (0, 0)
    row = lambda width: pl.BlockSpec((1, tm, width), lambda bi, si: (bi, si, 0))
    in_specs = [row(d)] + [row(a.shape[2]) for a in a_list]
    in_specs += [pl.BlockSpec(w.shape, const2) for w in wo_list]
    args = [x, *a_list, *wo_list]
    if has_bias:
        in_specs.append(pl.BlockSpec((1, d), const2))
        args.append(b_out.reshape(1, d))
    xw = w_q.shape[1]
    in_specs += [pl.BlockSpec((1, d), const2),
                 pl.BlockSpec((d, xw), const2),
                 pl.BlockSpec((1, mkv.shape[1], mkv.shape[2]), lambda bi, si: (bi, 0, 0)),
                 pl.BlockSpec((xw, d), const2),
                 pl.BlockSpec((1, d), const2),
                 pl.BlockSpec((d, LANES), const2),
                 pl.BlockSpec((1, LANES), const2)]
    args += [g_x.reshape(1, d), w_q, mkv, w_xo, g_f.reshape(1, d), w_rt, b_rt]
    out_specs = [row(d),
                 pl.BlockSpec((tm * TOKEN_ROWS, LANES), lambda bi, si: (bi * ns + si, 0)),
                 pl.BlockSpec((8, tm), lambda bi, si: (0, bi * ns + si)),
                 pl.BlockSpec((tm, LANES), lambda bi, si: (bi * ns + si, 0)),
                 pl.BlockSpec((1, LANES), const2)]
    out_shape = [jax.ShapeDtypeStruct((b, s, d), F32),
                 jax.ShapeDtypeStruct((t * TOKEN_ROWS, LANES), U32),
                 jax.ShapeDtypeStruct((8, t), I32),
                 jax.ShapeDtypeStruct((t, LANES), F32),
                 jax.ShapeDtypeStruct((1, LANES), I32)]
    return pl.pallas_call(
        functools.partial(_post_kernel, tm=tm, n_a=n_a, has_bias=has_bias),
        grid=(b, ns),
        in_specs=in_specs,
        out_specs=out_specs,
        out_shape=out_shape,
        scratch_shapes=[pltpu.VMEM((1, LANES), F32), pltpu.VMEM((N_EXPERTS, 1), F32)],
        compiler_params=_cparams(("arbitrary", "arbitrary")),
        name="post_mixer",
    )(*args)


def _to_token_tiles(dst_ref, val, rows, tok0=0):
    half = TOKEN_ROWS * LANES
    for j in range(TOKEN_ROWS):
        hi = val[:, j * LANES:(j + 1) * LANES].astype(BF16).astype(F32)
        lo = val[:, half + j * LANES:half + (j + 1) * LANES].astype(BF16).astype(F32)
        word = lax.bitcast_convert_type(hi, U32) | (lax.bitcast_convert_type(lo, U32) >> 16)
        dst_ref[pl.ds(tok0 * TOKEN_ROWS + j, rows, stride=TOKEN_ROWS), :] = word


def _from_token_tiles(src_ref, rows, tok0=0):
    his, los = [], []
    for j in range(TOKEN_ROWS):
        word = src_ref[pl.ds(tok0 * TOKEN_ROWS + j, rows, stride=TOKEN_ROWS), :]
        his.append(lax.bitcast_convert_type(word & jnp.uint32(0xFFFF0000), F32))
        los.append(lax.bitcast_convert_type(word << 16, F32))
    return jnp.concatenate(his + los, axis=1)


def _token_slice(ref, tok):
    return ref.at[pl.ds(pl.multiple_of(tok * TOKEN_ROWS, TOKEN_ROWS), TOKEN_ROWS)]


def _pos_kernel(off_ref, mi_ref, pos_ref):
    e = mi_ref[0:TOP_K, :]
    pos = mi_ref[TOP_K:2 * TOP_K, :]
    for j in range(N_EXPERTS):
        pos = pos + jnp.where(e == j, off_ref[j], 0)
    pos_ref[...] = pos


def _sorted_positions(off, meta_i):
    t = meta_i.shape[1]
    return pl.pallas_call(
        _pos_kernel,
        grid_spec=pltpu.PrefetchScalarGridSpec(
            num_scalar_prefetch=1,
            grid=(1,),
            in_specs=[pl.BlockSpec((8, t), lambda i, off: (0, 0))],
            out_specs=pl.BlockSpec((TOP_K, t), lambda i, off: (0, 0))),
        out_shape=jax.ShapeDtypeStruct((TOP_K, t), I32),
        compiler_params=_cparams(("arbitrary",)),
        name="moe_positions",
    )(off, meta_i)


def _tile_major(pos, tm):
    t = pos.shape[1]
    return pos.reshape(TOP_K, t // tm, tm).transpose(1, 0, 2).reshape(-1)


def _dispatch_kernel(pos_ref, h_ref, xs_ref, sem, *, tm):
    def issue(g, _):
        for u in range(DMA_UNROLL):
            r = g * DMA_UNROLL + u
            for kk in range(TOP_K):
                pos = pos_ref[kk * tm + r]
                pltpu.make_async_copy(_token_slice(h_ref, r), _token_slice(xs_ref, pos), sem).start(priority=kk)
        return 0

    lax.fori_loop(0, tm // DMA_UNROLL, issue, 0)
    for _ in range(TOP_K):
        pltpu.make_async_copy(h_ref, xs_ref.at[pl.ds(0, tm * TOKEN_ROWS)], sem).wait()


def _dispatch(pos, h2t, n_rows):
    tm = DISPATCH_TILE
    t = h2t.shape[0] // TOKEN_ROWS
    return pl.pallas_call(
        functools.partial(_dispatch_kernel, tm=tm),
        grid=(t // tm,),
        in_specs=[pl.BlockSpec((TOP_K * tm,), lambda i: (i,), memory_space=pltpu.SMEM),
                  pl.BlockSpec((tm * TOKEN_ROWS, LANES), lambda i: (i, 0))],
        out_specs=pl.BlockSpec(memory_space=pl.ANY),
        scratch_shapes=[pltpu.SemaphoreType.DMA],
        out_shape=jax.ShapeDtypeStruct((n_rows * TOKEN_ROWS, LANES), U32),
        compiler_params=_cparams(("arbitrary",)),
        name="moe_dispatch",
    )(_tile_major(pos, tm), h2t)


def _expert_kernel(tblk_ref, texp_ref, tn_ref, tnxt_ref, tpar_ref, xs_ref, wgu_hbm, wdn_hbm, ys_ref,
                   wgu_bf, wdn_bf, wgu_f, wdn_f, sems, *, tm, layer):
    i = pl.program_id(0)
    n_valid = tn_ref[i]
    expert = texp_ref[i]
    slot = tpar_ref[i]
    new_expert = (i == 0) | (expert != texp_ref[jnp.maximum(i - 1, 0)])

    def weight_copies(ex, sl):
        return (pltpu.make_async_copy(wgu_hbm.at[layer, ex], wgu_f.at[sl], sems.at[sl]),
                pltpu.make_async_copy(wdn_hbm.at[layer, ex], wdn_f.at[sl], sems.at[sl]))

    @pl.when(i == 0)
    def _():
        for cp in weight_copies(expert, slot):
            cp.start()

    @pl.when(new_expert)
    def _():
        @pl.when(tnxt_ref[i] >= 0)
        def _():
            for cp in weight_copies(tnxt_ref[i], 1 - slot):
                cp.start()

        for cp in weight_copies(expert, slot):
            cp.wait()
        wgu_bf[...] = wgu_f[slot].astype(BF16)
        wdn_bf[...] = wdn_f[slot].astype(BF16)

    @pl.when(n_valid > 0)
    def _():
        rh = tm // EXPERT_CHAINS
        starts = [c * rh for c in range(EXPERT_CHAINS)]
        row = lax.broadcasted_iota(I32, (rh, 1), 0)
        gus = []
        for r0 in starts:
            x = jnp.where(row + r0 < n_valid, _from_token_tiles(xs_ref, rh, r0), 0.0).astype(BF16)
            gus.append(_dot(x, wgu_bf[...]))
        ys = []
        for gu in gus:
            hid = (gu[:, :EXPERT_HIDDEN] / (1.0 + jnp.exp(-gu[:, :EXPERT_HIDDEN])) * gu[:, EXPERT_HIDDEN:]).astype(BF16)
            ys.append(_dot(hid, wdn_bf[...]))
        for r0, y in zip(starts, ys):
            _to_token_tiles(ys_ref, y, rh, r0)


def _experts(plan, xs, w_gu, w_dn, layer):
    tm = EXPERT_TILE
    nt = xs.shape[0] // (tm * TOKEN_ROWS)
    gu_shape, dn_shape = w_gu.shape[2:], w_dn.shape[2:]
    rows = lambda i, tb, te, tn, tx, tp: (tb[i], 0)
    return pl.pallas_call(
        functools.partial(_expert_kernel, tm=tm, layer=layer),
        grid_spec=pltpu.PrefetchScalarGridSpec(
            num_scalar_prefetch=5,
            grid=(nt,),
            in_specs=[pl.BlockSpec((tm * TOKEN_ROWS, LANES), rows),
                      pl.BlockSpec(memory_space=pl.ANY),
                      pl.BlockSpec(memory_space=pl.ANY)],
            out_specs=pl.BlockSpec((tm * TOKEN_ROWS, LANES), rows),
            scratch_shapes=[pltpu.VMEM(gu_shape, BF16), pltpu.VMEM(dn_shape, BF16),
                            pltpu.VMEM((2,) + gu_shape, F32), pltpu.VMEM((2,) + dn_shape, F32),
                            pltpu.SemaphoreType.DMA((2,))]),
        out_shape=jax.ShapeDtypeStruct(xs.shape, U32),
        compiler_params=_cparams(("arbitrary",)),
        name="moe_experts",
    )(*plan, xs, w_gu, w_dn)


def _moe_combine_tile(pos_cur, pos_nxt, mw_ref, x, ys_ref, ybuf, sems, i, n, tm):
    def gather(pos_ref, slot):
        def issue(g, _):
            for u in range(DMA_UNROLL):
                r = g * DMA_UNROLL + u
                for kk in range(TOP_K):
                    pos = pos_ref[kk * tm + r]
                    pltpu.make_async_copy(_token_slice(ys_ref, pos), _token_slice(ybuf.at[slot, kk], r),
                                          sems.at[slot]).start(priority=kk)
            return 0

        lax.fori_loop(0, tm // DMA_UNROLL, issue, 0)

    @pl.when(i == 0)
    def _():
        gather(pos_cur, 0)

    @pl.when(i + 1 < n)
    def _():
        gather(pos_nxt, (i + 1) % 2)

    slot = i % 2
    for kk in range(TOP_K):
        pltpu.make_async_copy(ys_ref.at[pl.ds(0, tm * TOKEN_ROWS)], ybuf.at[slot, kk], sems.at[slot]).wait()
    mw = mw_ref[...]
    return (x + mw[:, 0:1] * _from_token_tiles(ybuf.at[slot, 0], tm)
            + mw[:, 1:2] * _from_token_tiles(ybuf.at[slot, 1], tm))


def _combine_kernel(pos_cur, pos_nxt, mw_ref, x_ref, ys_ref, *rest, tm, final):
    if final:
        g_ref, o_ref, ybuf, sems = rest
    else:
        o_ref, ybuf, sems = rest
    x3 = _moe_combine_tile(pos_cur, pos_nxt, mw_ref, x_ref[...], ys_ref, ybuf, sems,
                           pl.program_id(0), pl.num_programs(0), tm)
    if final:
        x3 = _rms(x3, g_ref[...])
    o_ref[...] = x3


def _combine(moe, g_final):
    pos, meta_w, x2, ys = moe
    d = x2.shape[-1]
    x2d = x2.reshape(-1, d)
    t = x2d.shape[0]
    tm = COMBINE_TILE
    nt = t // tm
    final = g_final is not None
    in_specs = [pl.BlockSpec((TOP_K * tm,), lambda i: (i,), memory_space=pltpu.SMEM),
                pl.BlockSpec((TOP_K * tm,), lambda i: (jnp.minimum(i + 1, nt - 1),), memory_space=pltpu.SMEM),
                pl.BlockSpec((tm, LANES), lambda i: (i, 0)),
                pl.BlockSpec((tm, d), lambda i: (i, 0)),
                pl.BlockSpec(memory_space=pl.ANY)]
    pos_flat = _tile_major(pos, tm)
    args = [pos_flat, pos_flat, meta_w, x2d, ys]
    if final:
        in_specs.append(pl.BlockSpec((1, d), lambda i: (0, 0)))
        args.append(g_final.reshape(1, d))
    return pl.pallas_call(
        functools.partial(_combine_kernel, tm=tm, final=final),
        grid=(nt,),
        in_specs=in_specs,
        out_specs=pl.BlockSpec((tm, d), lambda i: (i, 0)),
        scratch_shapes=[pltpu.VMEM((2, TOP_K, tm * TOKEN_ROWS, LANES), U32), pltpu.SemaphoreType.DMA((2,))],
        out_shape=jax.ShapeDtypeStruct((t, d), F32),
        compiler_params=_cparams(("arbitrary",)),
        name="moe_combine",
    )(*args)


def _plan_kernel(cnt_ref, off_ref, blk_ref, exp_ref, nv_ref, nxt_ref, par_ref, *, tm, n_tiles):
    shift = tm.bit_length() - 1

    def clear(i, _):
        nxt_ref[i] = -1
        return 0

    lax.fori_loop(0, n_tiles, clear, 0)

    def per_expert(e, carry):
        row0, run, prev_first = carry
        c = cnt_ref[0, e]
        ntile = lax.shift_right_logical(c + (tm - 1), shift)
        off_ref[e] = row0
        t0 = lax.shift_right_logical(row0, shift)

        def fill(j, _):
            blk_ref[t0 + j] = t0 + j
            exp_ref[t0 + j] = e
            nv_ref[t0 + j] = jnp.minimum(c - j * tm, tm)
            par_ref[t0 + j] = run & 1
            return 0

        lax.fori_loop(0, ntile, fill, 0)
        has = ntile > 0

        @pl.when(has & (prev_first >= 0))
        def _():
            nxt_ref[prev_first] = e

        return (row0 + lax.shift_left(ntile, shift), run + has.astype(I32), jnp.where(has, t0, prev_first))

    total, _, _ = lax.fori_loop(0, N_EXPERTS, per_expert, (jnp.int32(0), jnp.int32(0), jnp.int32(-1)))
    used = lax.shift_right_logical(total, shift)
    last = jnp.maximum(used - 1, 0)

    def tail(i, _):
        blk_ref[i] = last
        exp_ref[i] = exp_ref[last]
        nv_ref[i] = 0
        par_ref[i] = par_ref[last]
        return 0

    lax.fori_loop(used, n_tiles, tail, 0)


def _moe_plan(counts, n_tiles):
    tm = EXPERT_TILE
    assert tm & (tm - 1) == 0
    smem = pl.BlockSpec(memory_space=pltpu.SMEM)
    return pl.pallas_call(
        functools.partial(_plan_kernel, tm=tm, n_tiles=n_tiles),
        in_specs=[smem],
        out_specs=[smem] * 6,
        out_shape=[jax.ShapeDtypeStruct((N_EXPERTS,), I32)] + [jax.ShapeDtypeStruct((n_tiles,), I32)] * 5,
        name="moe_plan",
    )(counts)


def _moe(x2, h2t, meta_i, meta_w, counts, w_gu, w_dn, layer):
    b, s, d = x2.shape
    assert d == 2 * TOKEN_ROWS * LANES
    n_rows = b * s * TOP_K + N_EXPERTS * EXPERT_TILE
    off, *tile_plan = _moe_plan(counts, n_rows // EXPERT_TILE)
    pos = _sorted_positions(off, meta_i)
    xs = _dispatch(pos, h2t, n_rows)
    ys = _experts(tile_plan, xs, w_gu, w_dn, layer)
    return pos, meta_w, x2, ys


def _router_weights(w_group, b_group, w_router, b_router):
    d = w_group.shape[0]
    pad = LANES - N_EXPERTS - N_GROUPS
    w = jnp.concatenate([w_router, w_group, jnp.zeros((d, pad), F32)], axis=1)
    bias = jnp.concatenate([b_router, b_group, jnp.zeros((pad,), F32)]).reshape(1, LANES)
    return w.astype(BF16), bias


def kernel(x, mem, mem_norm, mem_w_kv, norm_mix, norm_xattn, norm_ffn, hyb_w_in, hyb_conv_w, diff_lambda, diff_subln, hyb_w_out, swa_w_qkv, swa_b_qkv, swa_sinks, swa_w_out, swa_b_out, xattn_w_q, xattn_w_o, moe_w_group, moe_b_group, moe_w_router, moe_b_router, moe_w_gate_up, moe_w_down, final_norm):
    b, s, d = x.shape
    m = mem.shape[1]
    depth = norm_mix.shape[0]
    assert d == 2 * TOKEN_ROWS * LANES and s % ROW_TILE == 0 and s % (4 * BLOCK) == 0
    assert (b * s) % DISPATCH_TILE == 0 and (b * s) % COMBINE_TILE == 0 and DISPATCH_TILE % DMA_UNROLL == 0
    mkv = _norm_proj(mem.reshape(b * m, d), mem_norm, mem_w_kv.astype(BF16), m).reshape(b, m, -1)

    scale = DIFF_DK ** -0.5 * LOG2E
    cq, suq, sdq, half = _rope_lane_tables(s, DIFF_DK, scale)
    ck, suk, sdk, _ = _rope_lane_tables(s, DIFF_DK, 1.0)
    tabs_q, tabs_k = (cq, suq, sdq), (ck, suk, sdk)

    moe = None
    for l in range(depth):
        if moe is not None:
            x = _combine(moe, None).reshape(b, s, d)
        if l % 2 == 0:
            e = l // 2
            lambda_init = 0.8 - 0.6 * math.exp(-0.3 * l)
            ya, q, k, v = _hyb_front(x, norm_mix[l], hyb_w_in[e].astype(BF16), hyb_conv_w[e], tabs_q, tabs_k, half)
            o = _diff_attn(q, k, v, diff_lambda[e], diff_subln[e], lambda_init)
            w_out = hyb_w_out[e].astype(BF16)
            a_list, wo_list, b_out = [ya, o], [w_out[:CONV_CH], w_out[CONV_CH:]], None
        else:
            e = l // 2
            order = _swa_head_order()
            hd, nq = SWA_HEAD_DIM, SWA_Q_HEADS * SWA_HEAD_DIM
            heads = lambda a, axis: [lax.slice_in_dim(a, h * hd, (h + 1) * hd, axis=axis) for h in order]
            w_all = swa_w_qkv[e].astype(BF16)
            w_qkv = jnp.concatenate(heads(w_all, 1) + [w_all[:, nq:]], axis=1)
            b_qkv = jnp.concatenate(heads(swa_b_qkv[e], 0) + [swa_b_qkv[e][nq:]])
            sinks = jnp.stack([swa_sinks[e][h] for h in order])
            q, kv = _swa_front(x, norm_mix[l], w_qkv, b_qkv, tabs_q, tabs_k, half)
            o = _swa_attn(q, kv, sinks)
            w_out = jnp.concatenate(heads(swa_w_out[e].astype(BF16), 0), axis=0)
            a_list, wo_list, b_out = [o], [w_out], swa_b_out[e]
        w_rt, b_rt = _router_weights(moe_w_group[l], moe_b_group[l], moe_w_router[l], moe_b_router[l])
        x2, h2, meta_i, meta_w, counts = _post_mixer(
            x, a_list, wo_list, b_out, norm_xattn[l], xattn_w_q[l].astype(BF16), mkv,
            xattn_w_o[l].astype(BF16), norm_ffn[l], w_rt, b_rt)
        moe = _moe(x2, h2, meta_i, meta_w, counts, moe_w_gate_up, moe_w_down, l)
    return _combine(moe, final_norm).reshape(b, s, d)
```

```python
import functools
import math

import jax
import jax.numpy as jnp
from jax import lax
from jax.experimental import pallas as pl
from jax.experimental.pallas import tpu as pltpu

F32 = jnp.float32
BF16 = jnp.bfloat16
I32 = jnp.int32
U32 = jnp.uint32

EPS = 1e-6
LANES = 128
SUBLANES = 8
TOKEN_ROWS = 4
DMA_UNROLL = 16
VMEM_LIMIT = 56 * 1024 * 1024

ROPE_THETA = 500000.0
ROPE_FRACTION = 4
BLOCK = 128
CONV_CH = 512
CONV_K = 3
DIFF_HEADS = 4
DIFF_DK = 64
DIFF_DV = 128
SWA_Q_HEADS = 16
SWA_KV_HEADS = 4
SWA_HEAD_DIM = 64
SWA_WINDOW = 128
XATTN_HEADS = 4
XATTN_HEAD_DIM = 128
N_GROUPS = 4
EXPERTS_PER_GROUP = 8
N_EXPERTS = N_GROUPS * EXPERTS_PER_GROUP
TOP_K = 2
EXPERT_HIDDEN = 512

ROW_TILE = 1024
EXPERT_TILE = 512
DISPATCH_TILE = 4096
COMBINE_TILE = 512
POST_CHAINS = 2
EXPERT_CHAINS = 2
FRONT_CHAINS = 2
NEG_INF = float("-inf")
LOG2E = math.log2(math.e)


def _cparams(sem):
    return pltpu.CompilerParams(dimension_semantics=sem, vmem_limit_bytes=VMEM_LIMIT)


def _rms(x, g):
    return x * lax.rsqrt(jnp.mean(x * x, axis=-1, keepdims=True) + EPS) * g


def _dot(a, b):
    return jnp.dot(a, b, preferred_element_type=F32)


def _dot_nt(a, b):
    return lax.dot_general(a, b, (((1,), (1,)), ((), ())), preferred_element_type=F32)


def _rope_lane_tables(seq, head_dim, scale):
    rot = head_dim // ROPE_FRACTION
    half = rot // 2
    pos = jnp.arange(seq, dtype=F32)
    inv = ROPE_THETA ** (-jnp.arange(0, rot, 2, dtype=F32) / rot)
    ang = pos[:, None] * inv[None, :]
    cos, sin = jnp.cos(ang), jnp.sin(ang)
    idx = jnp.arange(LANES) % head_dim
    cl = jnp.take(cos, idx % half, axis=1)
    sl = jnp.take(sin, idx % half, axis=1)
    c = jnp.where(idx < rot, cl, 1.0) * scale
    s_up = jnp.where(idx < half, -sl, 0.0) * scale
    s_dn = jnp.where((idx >= half) & (idx < rot), sl, 0.0) * scale
    return c.astype(F32), s_up.astype(F32), s_dn.astype(F32), half


def _rope_chunk(xc, c, s_up, s_dn, half):
    return (xc * c + pltpu.roll(xc, LANES - half, 1) * s_up + pltpu.roll(xc, half, 1) * s_dn)


def _norm_proj_kernel(x_ref, g_ref, w_ref, o_ref):
    h = _rms(x_ref[...], g_ref[...]).astype(BF16)
    o_ref[...] = _dot(h, w_ref[...]).astype(o_ref.dtype)


def _norm_proj(x2d, g, w_bf16, tm):
    m, d = x2d.shape
    n = w_bf16.shape[1]
    return pl.pallas_call(
        _norm_proj_kernel,
        grid=(m // tm,),
        in_specs=[pl.BlockSpec((tm, d), lambda i: (i, 0)),
                  pl.BlockSpec((1, d), lambda i: (0, 0)),
                  pl.BlockSpec((d, n), lambda i: (0, 0))],
        out_specs=pl.BlockSpec((tm, n), lambda i: (i, 0)),
        out_shape=jax.ShapeDtypeStruct((m, n), BF16),
        compiler_params=_cparams(("arbitrary",)),
        name="mem_kv_proj",
    )(x2d, g.reshape(1, d), w_bf16)


def _hyb_front_kernel(x_ref, g_ref, w_ref, cw_ref, cq_ref, suq_ref, sdq_ref, ck_ref, suk_ref, sdk_ref,
                      ya_ref, q_ref, k_ref, v_ref, cbuf, *, tm, half):
    s = pl.program_id(1)
    c = CONV_CH
    base = 3 * c
    nq = DIFF_HEADS * 2 * DIFF_DK
    rh = tm // FRONT_CHAINS
    starts = [ch * rh for ch in range(FRONT_CHAINS)]
    cw = cw_ref[...]

    @pl.when(s == 0)
    def _():
        cbuf[0:8, :] = jnp.zeros((8, c), F32)

    hs = [_rms(x_ref[0, r0:r0 + rh, :], g_ref[...]).astype(BF16) for r0 in starts]
    for r0, h in zip(starts, hs):
        gate_b = _dot(h, w_ref[:, 0:c])
        cu = _dot(h, w_ref[:, c:2 * c]) * _dot(h, w_ref[:, 2 * c:3 * c])
        cbuf[8 + r0:8 + r0 + rh, :] = cu
        conv = (cw[0:1, :] * cbuf[6 + r0:6 + r0 + rh, :] + cw[1:2, :] * cbuf[7 + r0:7 + r0 + rh, :] + cw[2:3, :] * cu)
        ya_ref[0, r0:r0 + rh, :] = (gate_b * conv).astype(BF16)
    cbuf[0:8, :] = cbuf[tm:tm + 8, :]

    for r0, h in zip(starts, hs):
        cq, suq, sdq = cq_ref[r0:r0 + rh, :], suq_ref[r0:r0 + rh, :], sdq_ref[r0:r0 + rh, :]
        pq = _dot(h, w_ref[:, base:base + nq])
        for j in range(nq // LANES):
            q_ref[0, r0:r0 + rh, j * LANES:(j + 1) * LANES] = _rope_chunk(
                pq[:, j * LANES:(j + 1) * LANES], cq, suq, sdq, half).astype(BF16)
    for r0, h in zip(starts, hs):
        ck, suk, sdk = ck_ref[r0:r0 + rh, :], suk_ref[r0:r0 + rh, :], sdk_ref[r0:r0 + rh, :]
        pk = _dot(h, w_ref[:, base + nq:base + 2 * nq])
        for j in range(nq // LANES):
            k_ref[0, r0:r0 + rh, j * LANES:(j + 1) * LANES] = _rope_chunk(
                pk[:, j * LANES:(j + 1) * LANES], ck, suk, sdk, half).astype(BF16)
    for r0, h in zip(starts, hs):
        v_ref[0, r0:r0 + rh, :] = _dot(h, w_ref[:, base + 2 * nq:]).astype(BF16)


def _hyb_front(x, g, w_in, conv_w, tabs_q, tabs_k, half):
    b, s, d = x.shape
    tm = ROW_TILE
    n = w_in.shape[1]
    nq = DIFF_HEADS * 2 * DIFF_DK
    nv = DIFF_HEADS * DIFF_DV
    tab_spec = pl.BlockSpec((tm, LANES), lambda bi, si: (si, 0))
    row = lambda width: pl.BlockSpec((1, tm, width), lambda bi, si: (bi, si, 0))
    return pl.pallas_call(
        functools.partial(_hyb_front_kernel, tm=tm, half=half),
        grid=(b, s // tm),
        in_specs=[row(d),
                  pl.BlockSpec((1, d), lambda bi, si: (0, 0)),
                  pl.BlockSpec((d, n), lambda bi, si: (0, 0)),
                  pl.BlockSpec((CONV_K, CONV_CH), lambda bi, si: (0, 0)),
                  tab_spec, tab_spec, tab_spec, tab_spec, tab_spec, tab_spec],
        out_specs=[row(CONV_CH), row(nq), row(nq), row(nv)],
        out_shape=[jax.ShapeDtypeStruct((b, s, CONV_CH), BF16),
                   jax.ShapeDtypeStruct((b, s, nq), BF16),
                   jax.ShapeDtypeStruct((b, s, nq), BF16),
                   jax.ShapeDtypeStruct((b, s, nv), BF16)],
        scratch_shapes=[pltpu.VMEM((tm + 8, CONV_CH), F32)],
        compiler_params=_cparams(("arbitrary", "arbitrary")),
        name="hyb_front",
    )(x, g.reshape(1, d), w_in, conv_w, *tabs_q, *tabs_k)


def _lane_fold(x, op):
    r = x[:, 0:LANES]
    for c in range(1, x.shape[1] // LANES):
        r = op(r, x[:, c * LANES:(c + 1) * LANES])
    return r


def _diff_attn_kernel(q_ref, k_ref, v_ref, lam_ref, g_ref, o_ref, sbuf, stat, acc, *, tq, hp, lambda_init):
    i = pl.program_id(2)
    lane = lax.broadcasted_iota(I32, (1, LANES), 1)
    qs = []
    for h in range(hp):
        q = q_ref[0, :, h * LANES:(h + 1) * LANES]
        zero = jnp.zeros_like(q)
        qs += [jnp.where(lane < DIFF_DK, q, zero), jnp.where(lane >= DIFF_DK, q, zero)]
    nc = 2 * hp

    hq = tq // 2
    diag = pl.multiple_of(i * tq, tq)

    def pass1(j, carry):
        for c in range(nc):
            h = c // 2
            kb = k_ref[0, pl.ds(pl.multiple_of(j * tq, tq), tq), h * LANES:(h + 1) * LANES]
            sc = _dot_nt(qs[c], kb)
            sbuf[c, j] = sc
            stat[c] = jnp.maximum(stat[c], _lane_fold(sc, jnp.maximum))
        return carry

    stat[...] = jnp.full(stat.shape, NEG_INF, F32)
    lax.fori_loop(0, i, pass1, 0)

    r = lax.broadcasted_iota(I32, (hq, hq), 0)
    cc = lax.broadcasted_iota(I32, (hq, hq), 1)
    tri = cc <= r
    for c in range(nc):
        h = c // 2
        k_lo = k_ref[0, pl.ds(diag, hq), h * LANES:(h + 1) * LANES]
        k_hi = k_ref[0, pl.ds(diag + hq, hq), h * LANES:(h + 1) * LANES]
        s_tl = jnp.where(tri, _dot_nt(qs[c][:hq], k_lo), NEG_INF)
        s_bl = _dot_nt(qs[c][hq:], k_lo)
        s_br = jnp.where(tri, _dot_nt(qs[c][hq:], k_hi), NEG_INF)
        sbuf[c, i, 0:hq, 0:hq] = s_tl
        sbuf[c, i, hq:tq, 0:hq] = s_bl
        sbuf[c, i, hq:tq, hq:tq] = s_br
        stat[c, 0:hq, :] = jnp.maximum(stat[c, 0:hq, :], _lane_fold(s_tl, jnp.maximum))
        stat[c, hq:tq, :] = jnp.maximum(stat[c, hq:tq, :],
                                        jnp.maximum(_lane_fold(s_bl, jnp.maximum), _lane_fold(s_br, jnp.maximum)))
    ms = [jnp.max(stat[c], axis=-1, keepdims=True) for c in range(nc)]

    ones = jnp.ones((tq, LANES), BF16)
    for h in range(hp):
        vb = v_ref[0, pl.ds(diag, tq), h * LANES:(h + 1) * LANES]
        v_ext = jnp.concatenate([vb, ones], axis=1)
        tops, bots = [], []
        for c in (2 * h, 2 * h + 1):
            tops.append(jnp.exp2(sbuf[c, i, 0:hq, 0:hq] - ms[c][:hq]))
            bots.append(jnp.exp2(sbuf[c, i, hq:tq, :] - ms[c][hq:]))
        top = _dot(jnp.concatenate(tops, axis=0).astype(BF16), v_ext[:hq])
        bot = _dot(jnp.concatenate(bots, axis=0).astype(BF16), v_ext)
        acc[h, 0:hq, :] = top[:hq]
        acc[h, hq:tq, :] = bot[:hq]
        acc[h, tq:tq + hq, :] = top[hq:]
        acc[h, tq + hq:2 * tq, :] = bot[hq:]

    def pass2(j, carry):
        for h in range(hp):
            vb = v_ref[0, pl.ds(pl.multiple_of(j * tq, tq), tq), h * LANES:(h + 1) * LANES]
            v_ext = jnp.concatenate([vb, ones], axis=1)
            p0 = jnp.exp2(sbuf[2 * h, j] - ms[2 * h])
            p1 = jnp.exp2(sbuf[2 * h + 1, j] - ms[2 * h + 1])
            acc[h] += _dot(jnp.concatenate([p0, p1], axis=0).astype(BF16), v_ext)
        return carry

    lax.fori_loop(0, i, pass2, 0)

    lf = lam_ref[...]
    lam = (jnp.exp(jnp.sum(lf[0:1] * lf[1:2], keepdims=True))
           - jnp.exp(jnp.sum(lf[2:3] * lf[3:4], keepdims=True)) + lambda_init)
    for h in range(hp):
        a0 = acc[h, 0:tq, :]
        a1 = acc[h, tq:2 * tq, :]
        o = a0[:, :LANES] / a0[:, LANES:] - lam * (a1[:, :LANES] / a1[:, LANES:])
        o_ref[0, :, h * LANES:(h + 1) * LANES] = (_rms(o, g_ref[...]) * (1.0 - lambda_init)).astype(BF16)


def _diff_attn(q, k, v, lam_vecs, subln_g, lambda_init):
    b, s, _ = q.shape
    tq = 512
    hp = 4
    blk = lambda bi, hi, qi: (bi, qi, hi)
    full = lambda bi, hi, qi: (bi, 0, hi)
    return pl.pallas_call(
        functools.partial(_diff_attn_kernel, tq=tq, hp=hp, lambda_init=lambda_init),
        grid=(b, DIFF_HEADS // hp, s // tq),
        in_specs=[pl.BlockSpec((1, tq, hp * LANES), blk),
                  pl.BlockSpec((1, s, hp * LANES), full),
                  pl.BlockSpec((1, s, hp * LANES), full),
                  pl.BlockSpec((4, DIFF_DK), lambda bi, hi, qi: (0, 0)),
                  pl.BlockSpec((1, DIFF_DV), lambda bi, hi, qi: (0, 0))],
        out_specs=pl.BlockSpec((1, tq, hp * LANES), blk),
        out_shape=jax.ShapeDtypeStruct((b, s, DIFF_HEADS * DIFF_DV), BF16),
        scratch_shapes=[pltpu.VMEM((2 * hp, s // tq, tq, tq), F32),
                        pltpu.VMEM((2 * hp, tq, LANES), F32),
                        pltpu.VMEM((hp, 2 * tq, DIFF_DV + LANES), F32)],
        compiler_params=_cparams(("arbitrary", "arbitrary", "arbitrary")),
        name="diff_attn",
    )(q, k, v, lam_vecs, subln_g.reshape(1, DIFF_DV))


def _swa_front_kernel(x_ref, g_ref, w_ref, b_ref, cq_ref, suq_ref, sdq_ref, ck_ref, suk_ref, sdk_ref,
                      q_ref, kv_ref, *, half):
    nq = SWA_Q_HEADS * SWA_HEAD_DIM
    nkv = SWA_KV_HEADS * SWA_HEAD_DIM
    tm = x_ref.shape[1]
    rh = tm // FRONT_CHAINS
    starts = [ch * rh for ch in range(FRONT_CHAINS)]
    hs = [_rms(x_ref[0, r0:r0 + rh, :], g_ref[...]).astype(BF16) for r0 in starts]
    for r0, h in zip(starts, hs):
        cq, suq, sdq = cq_ref[r0:r0 + rh, :], suq_ref[r0:r0 + rh, :], sdq_ref[r0:r0 + rh, :]
        pq = _dot(h, w_ref[:, 0:nq]) + b_ref[:, 0:nq]
        for j in range(nq // LANES):
            lo = j * LANES
            q_ref[0, r0:r0 + rh, lo:lo + LANES] = _rope_chunk(pq[:, lo:lo + LANES], cq, suq, sdq, half).astype(BF16)
    for r0, h in zip(starts, hs):
        ck, suk, sdk = ck_ref[r0:r0 + rh, :], suk_ref[r0:r0 + rh, :], sdk_ref[r0:r0 + rh, :]
        pkv = _dot(h, w_ref[:, nq:]) + b_ref[:, nq:]
        for j in range(nkv // LANES):
            lo = j * LANES
            kv_ref[0, r0:r0 + rh, lo:lo + LANES] = _rope_chunk(pkv[:, lo:lo + LANES], ck, suk, sdk, half).astype(BF16)
        kv_ref[0, r0:r0 + rh, nkv:] = pkv[:, nkv:].astype(BF16)


def _swa_front(x, g, w_qkv, b_qkv, tabs_q, tabs_k, half):
    b, s, d = x.shape
    tm = ROW_TILE
    n = w_qkv.shape[1]
    nq = SWA_Q_HEADS * SWA_HEAD_DIM
    nkv = SWA_KV_HEADS * SWA_HEAD_DIM
    tab_spec = pl.BlockSpec((tm, LANES), lambda bi, si: (si, 0))
    row = lambda width: pl.BlockSpec((1, tm, width), lambda bi, si: (bi, si, 0))
    return pl.pallas_call(
        functools.partial(_swa_front_kernel, half=half),
        grid=(b, s // tm),
        in_specs=[row(d),
                  pl.BlockSpec((1, d), lambda bi, si: (0, 0)),
                  pl.BlockSpec((d, n), lambda bi, si: (0, 0)),
                  pl.BlockSpec((1, n), lambda bi, si: (0, 0)),
                  tab_spec, tab_spec, tab_spec, tab_spec, tab_spec, tab_spec],
        out_specs=[row(nq), row(2 * nkv)],
        out_shape=[jax.ShapeDtypeStruct((b, s, nq), BF16),
                   jax.ShapeDtypeStruct((b, s, 2 * nkv), BF16)],
        compiler_params=_cparams(("arbitrary", "arbitrary")),
        name="swa_front",
    )(x, g.reshape(1, d), w_qkv, b_qkv.reshape(1, n), *tabs_q, *tabs_k)


def _swa_head_order():
    g_sz = SWA_Q_HEADS // SWA_KV_HEADS
    order = []
    for slab in range(SWA_Q_HEADS // 2):
        pair, j = slab // g_sz, slab % g_sz
        order += [(2 * pair) * g_sz + j, (2 * pair + 1) * g_sz + j]
    return order


def _swa_attn_kernel(sink_ref, q_ref, kvp_ref, kvc_ref, o_ref, *, nblk):
    i = pl.program_id(1)
    hd = SWA_HEAD_DIM
    nkv = SWA_KV_HEADS * hd
    g_sz = SWA_Q_HEADS // SWA_KV_HEADS
    kv = jnp.concatenate([kvp_ref[0], kvc_ref[0]], axis=0)
    r = lax.broadcasted_iota(I32, (BLOCK, 2 * BLOCK), 0)
    c = lax.broadcasted_iota(I32, (BLOCK, 2 * BLOCK), 1)
    rel = c - BLOCK - r
    in_win = (rel <= 0) & (rel > -SWA_WINDOW)
    lane = lax.broadcasted_iota(I32, (1, LANES), 1)
    lo_half = lane < hd
    ones = jnp.ones((2 * BLOCK, LANES), BF16)
    for n in range(nblk):
        mask = in_win & ((c >= BLOCK) | (i > 0)) if n == 0 else in_win
        keys = kv[n * BLOCK:(n + 2) * BLOCK]
        for pair in range(SWA_KV_HEADS // 2):
            k2 = keys[:, pair * LANES:(pair + 1) * LANES]
            v2 = keys[:, nkv + pair * LANES:nkv + (pair + 1) * LANES]
            v_ext = jnp.concatenate([v2, ones], axis=1)
            pieces = []
            for j in range(g_sz):
                slab = pair * g_sz + j
                qs = q_ref[0, n * BLOCK:(n + 1) * BLOCK, slab * LANES:(slab + 1) * LANES]
                zero = jnp.zeros_like(qs)
                pieces += [jnp.where(lo_half, qs, zero), jnp.where(lo_half, zero, qs)]
            sc = _dot_nt(jnp.concatenate(pieces, axis=0), k2)
            probs, tails = [], []
            for pc in range(2 * g_sz):
                scp = jnp.where(mask, sc[pc * BLOCK:(pc + 1) * BLOCK], NEG_INF)
                sink = sink_ref[2 * g_sz * pair + pc] * LOG2E
                m = jnp.maximum(jnp.max(scp, axis=-1, keepdims=True), sink)
                probs.append(jnp.exp2(scp - m).astype(BF16))
                tails.append(jnp.exp2(sink - m))
            pv = _dot(jnp.concatenate(probs, axis=0), v_ext)
            for j in range(g_sz):
                slab = pair * g_sz + j
                halves = []
                for hf in range(2):
                    pc = 2 * j + hf
                    blk = pv[pc * BLOCK:(pc + 1) * BLOCK]
                    halves.append(blk[:, :LANES] / (blk[:, LANES:] + tails[pc]))
                o_ref[0, n * BLOCK:(n + 1) * BLOCK, slab * LANES:(slab + 1) * LANES] = (
                    jnp.where(lo_half, halves[0], halves[1]).astype(BF16))


def _swa_attn(q, kv, sinks):
    b, s, nq = q.shape
    nblk = 4
    tq = nblk * BLOCK
    return pl.pallas_call(
        functools.partial(_swa_attn_kernel, nblk=nblk),
        grid_spec=pltpu.PrefetchScalarGridSpec(
            num_scalar_prefetch=1,
            grid=(b, s // tq),
            in_specs=[pl.BlockSpec((1, tq, nq), lambda bi, ni, sk: (bi, ni, 0)),
                      pl.BlockSpec((1, BLOCK, kv.shape[2]), lambda bi, ni, sk: (bi, jnp.maximum(ni * nblk - 1, 0), 0)),
                      pl.BlockSpec((1, tq, kv.shape[2]), lambda bi, ni, sk: (bi, ni, 0))],
            out_specs=pl.BlockSpec((1, tq, nq), lambda bi, ni, sk: (bi, ni, 0))),
        out_shape=jax.ShapeDtypeStruct((b, s, nq), BF16),
        compiler_params=_cparams(("arbitrary", "arbitrary")),
        name="swa_attn",
    )(sinks, q, kv, kv)


def _post_kernel(*refs, tm, n_a, has_bias):
    x_ref = refs[0]
    a_refs = refs[1:1 + n_a]
    k = 1 + n_a
    wo_refs = refs[k:k + n_a]
    k += n_a
    if has_bias:
        bo_ref = refs[k]
        k += 1
    (gx_ref, wq_ref, mkv_ref, wxo_ref, gf_ref, wr_ref, br_ref,
     x2_ref, h2_ref, mi_ref, mw_ref, cnt_ref, cnt_acc, cnt_col) = refs[k:]
    first = (pl.program_id(0) == 0) & (pl.program_id(1) == 0)

    @pl.when(first)
    def _():
        cnt_acc[...] = jnp.zeros_like(cnt_acc)
        cnt_col[...] = jnp.zeros_like(cnt_col)

    xw = XATTN_HEADS * XATTN_HEAD_DIM
    ones = jnp.ones((mkv_ref.shape[1], LANES), BF16)

    rh = tm // POST_CHAINS

    def out_proj(r0):
        acc = _dot(a_refs[0][0, r0:r0 + rh, :], wo_refs[0][...])
        for a_ref, w_ref in zip(a_refs[1:], wo_refs[1:]):
            acc = acc + _dot(a_ref[0, r0:r0 + rh, :], w_ref[...])
        if has_bias:
            acc = acc + bo_ref[...]
        return x_ref[0, r0:r0 + rh, :] + acc

    def q_proj(x1):
        hx = _rms(x1, gx_ref[...]).astype(BF16)
        return (_dot(hx, wq_ref[...]) * (XATTN_HEAD_DIM ** -0.5 * LOG2E)).astype(BF16)

    def mem_attn(qx):
        outs = []
        for hh in range(XATTN_HEADS):
            lo = hh * XATTN_HEAD_DIM
            mk = mkv_ref[0, :, lo:lo + XATTN_HEAD_DIM]
            mv = jnp.concatenate([mkv_ref[0, :, xw + lo:xw + lo + XATTN_HEAD_DIM], ones], axis=1)
            sc = _dot_nt(qx[:, lo:lo + XATTN_HEAD_DIM], mk)
            m = jnp.max(sc, axis=-1, keepdims=True)
            pv = _dot(jnp.exp2(sc - m).astype(BF16), mv)
            outs.append((pv[:, :LANES] / pv[:, LANES:]).astype(BF16))
        return jnp.concatenate(outs, axis=-1)

    def o_proj(r0, x1, ox):
        x2 = x1 + _dot(ox, wxo_ref[...])
        x2_ref[0, r0:r0 + rh, :] = x2
        return x2

    def router(r0, x2):
        h2 = _rms(x2, gf_ref[...])
        _to_token_tiles(h2_ref, h2, rh, r0)
        return _dot(h2.astype(BF16), wr_ref[...]) + br_ref[...]

    starts = [c * rh for c in range(POST_CHAINS)]
    x1s = [out_proj(r0) for r0 in starts]
    qxs = [q_proj(x1) for x1 in x1s]
    oxs = [mem_attn(qx) for qx in qxs]
    x2s = [o_proj(r0, x1, ox) for r0, x1, ox in zip(starts, x1s, oxs)]
    logits = jnp.concatenate([router(r0, x2) for r0, x2 in zip(starts, x2s)], axis=0)

    lt = logits.T
    ex = lt[0:N_EXPERTS]
    grp = lt[N_EXPERTS:N_EXPERTS + SUBLANES]
    grow = lax.broadcasted_iota(I32, grp.shape, 0).astype(F32)
    lg = jnp.where(grow < N_GROUPS, grp, NEG_INF)
    mg = jnp.max(lg, axis=0, keepdims=True)
    g_sel = jnp.min(jnp.where(lg == mg, grow, float(SUBLANES)), axis=0, keepdims=True)
    p_g = 1.0 / jnp.sum(jnp.exp(lg - mg), axis=0, keepdims=True)
    erow = lax.broadcasted_iota(I32, ex.shape, 0).astype(F32)
    e_lo = g_sel * EXPERTS_PER_GROUP
    le = jnp.where((erow >= e_lo) & (erow < e_lo + EXPERTS_PER_GROUP), ex, NEG_INF)
    big = float(N_EXPERTS)
    m1 = jnp.max(le, axis=0, keepdims=True)
    i1 = jnp.min(jnp.where(le == m1, erow, big), axis=0, keepdims=True)
    le2 = jnp.where(erow == i1, NEG_INF, le)
    m2 = jnp.max(le2, axis=0, keepdims=True)
    i2 = jnp.min(jnp.where(le2 == m2, erow, big), axis=0, keepdims=True)
    t = jnp.exp(m2 - m1)
    w1 = p_g / (1.0 + t)
    w2 = p_g * t / (1.0 + t)

    oh1 = erow == i1
    oh2 = erow == i2
    oh = jnp.where(oh1 | oh2, 1.0, 0.0)
    rr = lax.broadcasted_iota(I32, (tm, tm), 0)
    cc = lax.broadcasted_iota(I32, (tm, tm), 1)
    earlier = jnp.where(rr < cc, 1.0, 0.0).astype(BF16)
    before = _dot(oh.astype(BF16), earlier) + cnt_col[...]
    r1 = jnp.sum(jnp.where(oh1, before, 0.0), axis=0, keepdims=True)
    r2 = jnp.sum(jnp.where(oh2, before, 0.0), axis=0, keepdims=True)
    cnt_col[...] = cnt_col[...] + jnp.sum(oh, axis=1, keepdims=True)
    oh_pad = jnp.concatenate([oh, jnp.zeros((LANES - N_EXPERTS, tm), F32)], axis=0).astype(BF16)
    cnt_new = cnt_acc[...] + _dot_nt(jnp.ones((SUBLANES, tm), BF16), oh_pad)[0:1]
    cnt_acc[...] = cnt_new
    cnt_ref[...] = cnt_new.astype(I32)

    zrow = jnp.zeros((1, tm), F32)
    mi_ref[...] = jnp.concatenate([i1, i2, r1, r2, zrow, zrow, zrow, zrow], axis=0).astype(I32)
    wt = jnp.concatenate([w1, w2, jnp.zeros((LANES - TOP_K, tm), F32)], axis=0)
    mw_ref[...] = wt.T


def _post_mixer(x, a_list, wo_list, b_out, g_x, w_q, mkv, w_xo, g_f, w_rt, b_rt):
    b, s, d = x.shape
    tm = ROW_TILE
    ns = s // tm
    t = b * s
    n_a = len(a_list)
    has_bias = b_out is not None
    const2 = lambda bi, si: (0, 0)
    row = lambda width: pl.BlockSpec((1, tm, width), lambda bi, si: (bi, si, 0))
    in_specs = [row(d)] + [row(a.shape[2]) for a in a_list]
    in_specs += [pl.BlockSpec(w.shape, const2) for w in wo_list]
    args = [x, *a_list, *wo_list]
    if has_bias:
        in_specs.append(pl.BlockSpec((1, d), const2))
        args.append(b_out.reshape(1, d))
    xw = w_q.shape[1]
    in_specs += [pl.BlockSpec((1, d), const2),
                 pl.BlockSpec((d, xw), const2),
                 pl.BlockSpec((1, mkv.shape[1], mkv.shape[2]), lambda bi, si: (bi, 0, 0)),
                 pl.BlockSpec((xw, d), const2),
                 pl.BlockSpec((1, d), const2),
                 pl.BlockSpec((d, LANES), const2),
                 pl.BlockSpec((1, LANES), const2)]
    args += [g_x.reshape(1, d), w_q, mkv, w_xo, g_f.reshape(1, d), w_rt, b_rt]
    out_specs = [row(d),
                 pl.BlockSpec((tm * TOKEN_ROWS, LANES), lambda bi, si: (bi * ns + si, 0)),
                 pl.BlockSpec((8, tm), lambda bi, si: (0, bi * ns + si)),
                 pl.BlockSpec((tm, LANES), lambda bi, si: (bi * ns + si, 0)),
                 pl.BlockSpec((1, LANES), const2)]
    out_shape = [jax.ShapeDtypeStruct((b, s, d), F32),
                 jax.ShapeDtypeStruct((t * TOKEN_ROWS, LANES), U32),
                 jax.ShapeDtypeStruct((8, t), I32),
                 jax.ShapeDtypeStruct((t, LANES), F32),
                 jax.ShapeDtypeStruct((1, LANES), I32)]
    return pl.pallas_call(
        functools.partial(_post_kernel, tm=tm, n_a=n_a, has_bias=has_bias),
        grid=(b, ns),
        in_specs=in_specs,
        out_specs=out_specs,
        out_shape=out_shape,
        scratch_shapes=[pltpu.VMEM((1, LANES), F32), pltpu.VMEM((N_EXPERTS, 1), F32)],
        compiler_params=_cparams(("arbitrary", "arbitrary")),
        name="post_mixer",
    )(*args)


def _to_token_tiles(dst_ref, val, rows, tok0=0):
    half = TOKEN_ROWS * LANES
    for j in range(TOKEN_ROWS):
        hi = val[:, j * LANES:(j + 1) * LANES].astype(BF16).astype(F32)
        lo = val[:, half + j * LANES:half + (j + 1) * LANES].astype(BF16).astype(F32)
        word = lax.bitcast_convert_type(hi, U32) | (lax.bitcast_convert_type(lo, U32) >> 16)
        dst_ref[pl.ds(tok0 * TOKEN_ROWS + j, rows, stride=TOKEN_ROWS), :] = word


def _from_token_tiles(src_ref, rows, tok0=0):
    his, los = [], []
    for j in range(TOKEN_ROWS):
        word = src_ref[pl.ds(tok0 * TOKEN_ROWS + j, rows, stride=TOKEN_ROWS), :]
        his.append(lax.bitcast_convert_type(word & jnp.uint32(0xFFFF0000), F32))
        los.append(lax.bitcast_convert_type(word << 16, F32))
    return jnp.concatenate(his + los, axis=1)


def _token_slice(ref, tok):
    return ref.at[pl.ds(pl.multiple_of(tok * TOKEN_ROWS, TOKEN_ROWS), TOKEN_ROWS)]


def _pos_kernel(off_ref, mi_ref, pos_ref):
    e = mi_ref[0:TOP_K, :]
    pos = mi_ref[TOP_K:2 * TOP_K, :]
    for j in range(N_EXPERTS):
        pos = pos + jnp.where(e == j, off_ref[j], 0)
    pos_ref[...] = pos


def _sorted_positions(off, meta_i):
    t = meta_i.shape[1]
    return pl.pallas_call(
        _pos_kernel,
        grid_spec=pltpu.PrefetchScalarGridSpec(
            num_scalar_prefetch=1,
            grid=(1,),
            in_specs=[pl.BlockSpec((8, t), lambda i, off: (0, 0))],
            out_specs=pl.BlockSpec((TOP_K, t), lambda i, off: (0, 0))),
        out_shape=jax.ShapeDtypeStruct((TOP_K, t), I32),
        compiler_params=_cparams(("arbitrary",)),
        name="moe_positions",
    )(off, meta_i)


def _tile_major(pos, tm):
    t = pos.shape[1]
    return pos.reshape(TOP_K, t // tm, tm).transpose(1, 0, 2).reshape(-1)


def _dispatch_kernel(pos_ref, h_ref, xs_ref, sem, *, tm):
    def issue(g, _):
        for u in range(DMA_UNROLL):
            r = g * DMA_UNROLL + u
            for kk in range(TOP_K):
                pos = pos_ref[kk * tm + r]
                pltpu.make_async_copy(_token_slice(h_ref, r), _token_slice(xs_ref, pos), sem).start(priority=kk)
        return 0

    lax.fori_loop(0, tm // DMA_UNROLL, issue, 0)
    for _ in range(TOP_K):
        pltpu.make_async_copy(h_ref, xs_ref.at[pl.ds(0, tm * TOKEN_ROWS)], sem).wait()


def _dispatch(pos, h2t, n_rows):
    tm = DISPATCH_TILE
    t = h2t.shape[0] // TOKEN_ROWS
    return pl.pallas_call(
        functools.partial(_dispatch_kernel, tm=tm),
        grid=(t // tm,),
        in_specs=[pl.BlockSpec((TOP_K * tm,), lambda i: (i,), memory_space=pltpu.SMEM),
                  pl.BlockSpec((tm * TOKEN_ROWS, LANES), lambda i: (i, 0))],
        out_specs=pl.BlockSpec(memory_space=pl.ANY),
        scratch_shapes=[pltpu.SemaphoreType.DMA],
        out_shape=jax.ShapeDtypeStruct((n_rows * TOKEN_ROWS, LANES), U32),
        compiler_params=_cparams(("arbitrary",)),
        name="moe_dispatch",
    )(_tile_major(pos, tm), h2t)


def _expert_kernel(tblk_ref, texp_ref, tn_ref, tnxt_ref, tpar_ref, xs_ref, wgu_hbm, wdn_hbm, ys_ref,
                   wgu_bf, wdn_bf, wgu_f, wdn_f, sems, *, tm, layer):
    i = pl.program_id(0)
    n_valid = tn_ref[i]
    expert = texp_ref[i]
    slot = tpar_ref[i]
    new_expert = (i == 0) | (expert != texp_ref[jnp.maximum(i - 1, 0)])

    def weight_copies(ex, sl):
        return (pltpu.make_async_copy(wgu_hbm.at[layer, ex], wgu_f.at[sl], sems.at[sl]),
                pltpu.make_async_copy(wdn_hbm.at[layer, ex], wdn_f.at[sl], sems.at[sl]))

    @pl.when(i == 0)
    def _():
        for cp in weight_copies(expert, slot):
            cp.start()

    @pl.when(new_expert)
    def _():
        @pl.when(tnxt_ref[i] >= 0)
        def _():
            for cp in weight_copies(tnxt_ref[i], 1 - slot):
                cp.start()

        for cp in weight_copies(expert, slot):
            cp.wait()
        wgu_bf[...] = wgu_f[slot].astype(BF16)
        wdn_bf[...] = wdn_f[slot].astype(BF16)

    @pl.when(n_valid > 0)
    def _():
        rh = tm // EXPERT_CHAINS
        starts = [c * rh for c in range(EXPERT_CHAINS)]
        row = lax.broadcasted_iota(I32, (rh, 1), 0)
        gus = []
        for r0 in starts:
            x = jnp.where(row + r0 < n_valid, _from_token_tiles(xs_ref, rh, r0), 0.0).astype(BF16)
            gus.append(_dot(x, wgu_bf[...]))
        ys = []
        for gu in gus:
            hid = (gu[:, :EXPERT_HIDDEN] / (1.0 + jnp.exp(-gu[:, :EXPERT_HIDDEN])) * gu[:, EXPERT_HIDDEN:]).astype(BF16)
            ys.append(_dot(hid, wdn_bf[...]))
        for r0, y in zip(starts, ys):
            _to_token_tiles(ys_ref, y, rh, r0)


def _experts(plan, xs, w_gu, w_dn, layer):
    tm = EXPERT_TILE
    nt = xs.shape[0] // (tm * TOKEN_ROWS)
    gu_shape, dn_shape = w_gu.shape[2:], w_dn.shape[2:]
    rows = lambda i, tb, te, tn, tx, tp: (tb[i], 0)
    return pl.pallas_call(
        functools.partial(_expert_kernel, tm=tm, layer=layer),
        grid_spec=pltpu.PrefetchScalarGridSpec(
            num_scalar_prefetch=5,
            grid=(nt,),
            in_specs=[pl.BlockSpec((tm * TOKEN_ROWS, LANES), rows),
                      pl.BlockSpec(memory_space=pl.ANY),
                      pl.BlockSpec(memory_space=pl.ANY)],
            out_specs=pl.BlockSpec((tm * TOKEN_ROWS, LANES), rows),
            scratch_shapes=[pltpu.VMEM(gu_shape, BF16), pltpu.VMEM(dn_shape, BF16),
                            pltpu.VMEM((2,) + gu_shape, F32), pltpu.VMEM((2,) + dn_shape, F32),
                            pltpu.SemaphoreType.DMA((2,))]),
        out_shape=jax.ShapeDtypeStruct(xs.shape, U32),
        compiler_params=_cparams(("arbitrary",)),
        name="moe_experts",
    )(*plan, xs, w_gu, w_dn)


def _moe_combine_tile(pos_cur, pos_nxt, mw_ref, x, ys_ref, ybuf, sems, i, n, tm):
    def gather(pos_ref, slot):
        def issue(g, _):
            for u in range(DMA_UNROLL):
                r = g * DMA_UNROLL + u
                for kk in range(TOP_K):
                    pos = pos_ref[kk * tm + r]
                    pltpu.make_async_copy(_token_slice(ys_ref, pos), _token_slice(ybuf.at[slot, kk], r),
                                          sems.at[slot]).start(priority=kk)
            return 0

        lax.fori_loop(0, tm // DMA_UNROLL, issue, 0)

    @pl.when(i == 0)
    def _():
        gather(pos_cur, 0)

    @pl.when(i + 1 < n)
    def _():
        gather(pos_nxt, (i + 1) % 2)

    slot = i % 2
    for kk in range(TOP_K):
        pltpu.make_async_copy(ys_ref.at[pl.ds(0, tm * TOKEN_ROWS)], ybuf.at[slot, kk], sems.at[slot]).wait()
    mw = mw_ref[...]
    return (x + mw[:, 0:1] * _from_token_tiles(ybuf.at[slot, 0], tm)
            + mw[:, 1:2] * _from_token_tiles(ybuf.at[slot, 1], tm))


def _combine_kernel(pos_cur, pos_nxt, mw_ref, x_ref, ys_ref, *rest, tm, final):
    if final:
        g_ref, o_ref, ybuf, sems = rest
    else:
        o_ref, ybuf, sems = rest
    x3 = _moe_combine_tile(pos_cur, pos_nxt, mw_ref, x_ref[...], ys_ref, ybuf, sems,
                           pl.program_id(0), pl.num_programs(0), tm)
    if final:
        x3 = _rms(x3, g_ref[...])
    o_ref[...] = x3


def _combine(moe, g_final):
    pos, meta_w, x2, ys = moe
    d = x2.shape[-1]
    x2d = x2.reshape(-1, d)
    t = x2d.shape[0]
    tm = COMBINE_TILE
    nt = t // tm
    final = g_final is not None
    in_specs = [pl.BlockSpec((TOP_K * tm,), lambda i: (i,), memory_space=pltpu.SMEM),
                pl.BlockSpec((TOP_K * tm,), lambda i: (jnp.minimum(i + 1, nt - 1),), memory_space=pltpu.SMEM),
                pl.BlockSpec((tm, LANES), lambda i: (i, 0)),
                pl.BlockSpec((tm, d), lambda i: (i, 0)),
                pl.BlockSpec(memory_space=pl.ANY)]
    pos_flat = _tile_major(pos, tm)
    args = [pos_flat, pos_flat, meta_w, x2d, ys]
    if final:
        in_specs.append(pl.BlockSpec((1, d), lambda i: (0, 0)))
        args.append(g_final.reshape(1, d))
    return pl.pallas_call(
        functools.partial(_combine_kernel, tm=tm, final=final),
        grid=(nt,),
        in_specs=in_specs,
        out_specs=pl.BlockSpec((tm, d), lambda i: (i, 0)),
        scratch_shapes=[pltpu.VMEM((2, TOP_K, tm * TOKEN_ROWS, LANES), U32), pltpu.SemaphoreType.DMA((2,))],
        out_shape=jax.ShapeDtypeStruct((t, d), F32),
        compiler_params=_cparams(("arbitrary",)),
        name="moe_combine",
    )(*args)


def _plan_kernel(cnt_ref, off_ref, blk_ref, exp_ref, nv_ref, nxt_ref, par_ref, *, tm, n_tiles):
    shift = tm.bit_length() - 1

    def clear(i, _):
        nxt_ref[i] = -1
        return 0

    lax.fori_loop(0, n_tiles, clear, 0)

    def per_expert(e, carry):
        row0, run, prev_first = carry
        c = cnt_ref[0, e]
        ntile = lax.shift_right_logical(c + (tm - 1), shift)
        off_ref[e] = row0
        t0 = lax.shift_right_logical(row0, shift)

        def fill(j, _):
            blk_ref[t0 + j] = t0 + j
            exp_ref[t0 + j] = e
            nv_ref[t0 + j] = jnp.minimum(c - j * tm, tm)
            par_ref[t0 + j] = run & 1
            return 0

        lax.fori_loop(0, ntile, fill, 0)
        has = ntile > 0

        @pl.when(has & (prev_first >= 0))
        def _():
            nxt_ref[prev_first] = e

        return (row0 + lax.shift_left(ntile, shift), run + has.astype(I32), jnp.where(has, t0, prev_first))

    total, _, _ = lax.fori_loop(0, N_EXPERTS, per_expert, (jnp.int32(0), jnp.int32(0), jnp.int32(-1)))
    used = lax.shift_right_logical(total, shift)
    last = jnp.maximum(used - 1, 0)

    def tail(i, _):
        blk_ref[i] = last
        exp_ref[i] = exp_ref[last]
        nv_ref[i] = 0
        par_ref[i] = par_ref[last]
        return 0

    lax.fori_loop(used, n_tiles, tail, 0)


def _moe_plan(counts, n_tiles):
    tm = EXPERT_TILE
    assert tm & (tm - 1) == 0
    smem = pl.BlockSpec(memory_space=pltpu.SMEM)
    return pl.pallas_call(
        functools.partial(_plan_kernel, tm=tm, n_tiles=n_tiles),
        in_specs=[smem],
        out_specs=[smem] * 6,
        out_shape=[jax.ShapeDtypeStruct((N_EXPERTS,), I32)] + [jax.ShapeDtypeStruct((n_tiles,), I32)] * 5,
        name="moe_plan",
    )(counts)


def _moe(x2, h2t, meta_i, meta_w, counts, w_gu, w_dn, layer):
    b, s, d = x2.shape
    assert d == 2 * TOKEN_ROWS * LANES
    n_rows = b * s * TOP_K + N_EXPERTS * EXPERT_TILE
    off, *tile_plan = _moe_plan(counts, n_rows // EXPERT_TILE)
    pos = _sorted_positions(off, meta_i)
    xs = _dispatch(pos, h2t, n_rows)
    ys = _experts(tile_plan, xs, w_gu, w_dn, layer)
    return pos, meta_w, x2, ys


def _router_weights(w_group, b_group, w_router, b_router):
    d = w_group.shape[0]
    pad = LANES - N_EXPERTS - N_GROUPS
    w = jnp.concatenate([w_router, w_group, jnp.zeros((d, pad), F32)], axis=1)
    bias = jnp.concatenate([b_router, b_group, jnp.zeros((pad,), F32)]).reshape(1, LANES)
    return w.astype(BF16), bias


def kernel(x, mem, mem_norm, mem_w_kv, norm_mix, norm_xattn, norm_ffn, hyb_w_in, hyb_conv_w, diff_lambda, diff_subln, hyb_w_out, swa_w_qkv, swa_b_qkv, swa_sinks, swa_w_out, swa_b_out, xattn_w_q, xattn_w_o, moe_w_group, moe_b_group, moe_w_router, moe_b_router, moe_w_gate_up, moe_w_down, final_norm):
    b, s, d = x.shape
    m = mem.shape[1]
    depth = norm_mix.shape[0]
    assert d == 2 * TOKEN_ROWS * LANES and s % ROW_TILE == 0 and s % (4 * BLOCK) == 0
    assert (b * s) % DISPATCH_TILE == 0 and (b * s) % COMBINE_TILE == 0 and DISPATCH_TILE % DMA_UNROLL == 0
    mem_tile = math.gcd(b * m, ROW_TILE)
    mkv = _norm_proj(mem.reshape(b * m, d), mem_norm, mem_w_kv.astype(BF16), mem_tile).reshape(b, m, -1)

    scale = DIFF_DK ** -0.5 * LOG2E
    cq, suq, sdq, half = _rope_lane_tables(s, DIFF_DK, scale)
    ck, suk, sdk, _ = _rope_lane_tables(s, DIFF_DK, 1.0)
    tabs_q, tabs_k = (cq, suq, sdq), (ck, suk, sdk)

    moe = None
    for l in range(depth):
        if moe is not None:
            x = _combine(moe, None).reshape(b, s, d)
        if l % 2 == 0:
            e = l // 2
            lambda_init = 0.8 - 0.6 * math.exp(-0.3 * l)
            ya, q, k, v = _hyb_front(x, norm_mix[l], hyb_w_in[e].astype(BF16), hyb_conv_w[e], tabs_q, tabs_k, half)
            o = _diff_attn(q, k, v, diff_lambda[e], diff_subln[e], lambda_init)
            w_out = hyb_w_out[e].astype(BF16)
            a_list, wo_list, b_out = [ya, o], [w_out[:CONV_CH], w_out[CONV_CH:]], None
        else:
            e = l // 2
            order = _swa_head_order()
            hd, nq = SWA_HEAD_DIM, SWA_Q_HEADS * SWA_HEAD_DIM
            heads = lambda a, axis: [lax.slice_in_dim(a, h * hd, (h + 1) * hd, axis=axis) for h in order]
            w_all = swa_w_qkv[e].astype(BF16)
            w_qkv = jnp.concatenate(heads(w_all, 1) + [w_all[:, nq:]], axis=1)
            b_qkv = jnp.concatenate(heads(swa_b_qkv[e], 0) + [swa_b_qkv[e][nq:]])
            sinks = jnp.stack([swa_sinks[e][h] for h in order])
            q, kv = _swa_front(x, norm_mix[l], w_qkv, b_qkv, tabs_q, tabs_k, half)
            o = _swa_attn(q, kv, sinks)
            w_out = jnp.concatenate(heads(swa_w_out[e].astype(BF16), 0), axis=0)
            a_list, wo_list, b_out = [o], [w_out], swa_b_out[e]
        w_rt, b_rt = _router_weights(moe_w_group[l], moe_b_group[l], moe_w_router[l], moe_b_router[l])
        x2, h2, meta_i, meta_w, counts = _post_mixer(
            x, a_list, wo_list, b_out, norm_xattn[l], xattn_w_q[l].astype(BF16), mkv,
            xattn_w_o[l].astype(BF16), norm_ffn[l], w_rt, b_rt)
        moe = _moe(x2, h2, meta_i, meta_w, counts, moe_w_gate_up, moe_w_down, l)
    return _combine(moe, final_norm).reshape(b, s, d)
```

```python
import functools
import math

import jax
import jax.numpy as jnp
from jax import lax
from jax.experimental import pallas as pl
from jax.experimental.pallas import tpu as pltpu

F32 = jnp.float32
BF16 = jnp.bfloat16
I32 = jnp.int32
U32 = jnp.uint32

EPS = 1e-6
LANES = 128
SUBLANES = 8
TOKEN_ROWS = 4
DMA_UNROLL = 16
VMEM_LIMIT = 56 * 1024 * 1024

ROPE_THETA = 500000.0
ROPE_FRACTION = 4
BLOCK = 128
CONV_CH = 512
CONV_K = 3
DIFF_HEADS = 4
DIFF_DK = 64
DIFF_DV = 128
SWA_Q_HEADS = 16
SWA_KV_HEADS = 4
SWA_HEAD_DIM = 64
SWA_WINDOW = 128
XATTN_HEADS = 4
XATTN_HEAD_DIM = 128
N_GROUPS = 4
EXPERTS_PER_GROUP = 8
N_EXPERTS = N_GROUPS * EXPERTS_PER_GROUP
TOP_K = 2
EXPERT_HIDDEN = 512

ROW_TILE = 1024
EXPERT_TILE = 512
DISPATCH_TILE = 4096
COMBINE_TILE = 512
POST_CHAINS = 2
EXPERT_CHAINS = 2
FRONT_CHAINS = 2
NEG_INF = float("-inf")
LOG2E = math.log2(math.e)


def _cparams(sem):
    return pltpu.CompilerParams(dimension_semantics=sem, vmem_limit_bytes=VMEM_LIMIT)


def _rms(x, g):
    return x * lax.rsqrt(jnp.mean(x * x, axis=-1, keepdims=True) + EPS) * g


def _dot(a, b):
    return jnp.dot(a, b, preferred_element_type=F32)


def _dot_nt(a, b):
    return lax.dot_general(a, b, (((1,), (1,)), ((), ())), preferred_element_type=F32)


def _rope_lane_tables(seq, head_dim, scale):
    rot = head_dim // ROPE_FRACTION
    half = rot // 2
    pos = jnp.arange(seq, dtype=F32)
    inv = ROPE_THETA ** (-jnp.arange(0, rot, 2, dtype=F32) / rot)
    ang = pos[:, None] * inv[None, :]
    cos, sin = jnp.cos(ang), jnp.sin(ang)
    idx = jnp.arange(LANES) % head_dim
    cl = jnp.take(cos, idx % half, axis=1)
    sl = jnp.take(sin, idx % half, axis=1)
    c = jnp.where(idx < rot, cl, 1.0) * scale
    s_up = jnp.where(idx < half, -sl, 0.0) * scale
    s_dn = jnp.where((idx >= half) & (idx < rot), sl, 0.0) * scale
    return c.astype(F32), s_up.astype(F32), s_dn.astype(F32), half


def _rope_chunk(xc, c, s_up, s_dn, half):
    return (xc * c + pltpu.roll(xc, LANES - half, 1) * s_up + pltpu.roll(xc, half, 1) * s_dn)


def _norm_proj_kernel(x_ref, g_ref, w_ref, o_ref):
    h = _rms(x_ref[...], g_ref[...]).astype(BF16)
    o_ref[...] = _dot(h, w_ref[...]).astype(o_ref.dtype)


def _norm_proj(x2d, g, w_bf16, tm):
    m, d = x2d.shape
    n = w_bf16.shape[1]
    return pl.pallas_call(
        _norm_proj_kernel,
        grid=(m // tm,),
        in_specs=[pl.BlockSpec((tm, d), lambda i: (i, 0)),
                  pl.BlockSpec((1, d), lambda i: (0, 0)),
                  pl.BlockSpec((d, n), lambda i: (0, 0))],
        out_specs=pl.BlockSpec((tm, n), lambda i: (i, 0)),
        out_shape=jax.ShapeDtypeStruct((m, n), BF16),
        compiler_params=_cparams(("arbitrary",)),
        name="mem_kv_proj",
    )(x2d, g.reshape(1, d), w_bf16)


def _hyb_front_kernel(x_ref, g_ref, w_ref, cw_ref, cq_ref, suq_ref, sdq_ref, ck_ref, suk_ref, sdk_ref,
                      ya_ref, q_ref, k_ref, v_ref, cbuf, *, tm, half):
    s = pl.program_id(1)
    c = CONV_CH
    base = 3 * c
    nq = DIFF_HEADS * 2 * DIFF_DK
    rh = tm // FRONT_CHAINS
    starts = [ch * rh for ch in range(FRONT_CHAINS)]
    cw = cw_ref[...]

    @pl.when(s == 0)
    def _():
        cbuf[0:8, :] = jnp.zeros((8, c), F32)

    hs = [_rms(x_ref[0, r0:r0 + rh, :], g_ref[...]).astype(BF16) for r0 in starts]
    for r0, h in zip(starts, hs):
        gate_b = _dot(h, w_ref[:, 0:c])
        cu = _dot(h, w_ref[:, c:2 * c]) * _dot(h, w_ref[:, 2 * c:3 * c])
        cbuf[8 + r0:8 + r0 + rh, :] = cu
        conv = (cw[0:1, :] * cbuf[6 + r0:6 + r0 + rh, :] + cw[1:2, :] * cbuf[7 + r0:7 + r0 + rh, :] + cw[2:3, :] * cu)
        ya_ref[0, r0:r0 + rh, :] = (gate_b * conv).astype(BF16)
    cbuf[0:8, :] = cbuf[tm:tm + 8, :]

    for r0, h in zip(starts, hs):
        cq, suq, sdq = cq_ref[r0:r0 + rh, :], suq_ref[r0:r0 + rh, :], sdq_ref[r0:r0 + rh, :]
        pq = _dot(h, w_ref[:, base:base + nq])
        for j in range(nq // LANES):
            q_ref[0, r0:r0 + rh, j * LANES:(j + 1) * LANES] = _rope_chunk(
                pq[:, j * LANES:(j + 1) * LANES], cq, suq, sdq, half).astype(BF16)
    for r0, h in zip(starts, hs):
        ck, suk, sdk = ck_ref[r0:r0 + rh, :], suk_ref[r0:r0 + rh, :], sdk_ref[r0:r0 + rh, :]
        pk = _dot(h, w_ref[:, base + nq:base + 2 * nq])
        for j in range(nq // LANES):
            k_ref[0, r0:r0 + rh, j * LANES:(j + 1) * LANES] = _rope_chunk(
                pk[:, j * LANES:(j + 1) * LANES], ck, suk, sdk, half).astype(BF16)
    for r0, h in zip(starts, hs):
        v_ref[0, r0:r0 + rh, :] = _dot(h, w_ref[:, base + 2 * nq:]).astype(BF16)


def _hyb_front(x, g, w_in, conv_w, tabs_q, tabs_k, half):
    b, s, d = x.shape
    tm = ROW_TILE
    n = w_in.shape[1]
    nq = DIFF_HEADS * 2 * DIFF_DK
    nv = DIFF_HEADS * DIFF_DV
    tab_spec = pl.BlockSpec((tm, LANES), lambda bi, si: (si, 0))
    row = lambda width: pl.BlockSpec((1, tm, width), lambda bi, si: (bi, si, 0))
    return pl.pallas_call(
        functools.partial(_hyb_front_kernel, tm=tm, half=half),
        grid=(b, s // tm),
        in_specs=[row(d),
                  pl.BlockSpec((1, d), lambda bi, si: (0, 0)),
                  pl.BlockSpec((d, n), lambda bi, si: (0, 0)),
                  pl.BlockSpec((CONV_K, CONV_CH), lambda bi, si: (0, 0)),
                  tab_spec, tab_spec, tab_spec, tab_spec, tab_spec, tab_spec],
        out_specs=[row(CONV_CH), row(nq), row(nq), row(nv)],
        out_shape=[jax.ShapeDtypeStruct((b, s, CONV_CH), BF16),
                   jax.ShapeDtypeStruct((b, s, nq), BF16),
                   jax.ShapeDtypeStruct((b, s, nq), BF16),
                   jax.ShapeDtypeStruct((b, s, nv), BF16)],
        scratch_shapes=[pltpu.VMEM((tm + 8, CONV_CH), F32)],
        compiler_params=_cparams(("arbitrary", "arbitrary")),
        name="hyb_front",
    )(x, g.reshape(1, d), w_in, conv_w, *tabs_q, *tabs_k)


def _lane_fold(x, op):
    r = x[:, 0:LANES]
    for c in range(1, x.shape[1] // LANES):
        r = op(r, x[:, c * LANES:(c + 1) * LANES])
    return r


def _diff_attn_kernel(q_ref, k_ref, v_ref, lam_ref, g_ref, o_ref, sbuf, stat, acc, *, tq, hp, lambda_init):
    i = pl.program_id(2)
    lane = lax.broadcasted_iota(I32, (1, LANES), 1)
    qs = []
    for h in range(hp):
        q = q_ref[0, :, h * LANES:(h + 1) * LANES]
        zero = jnp.zeros_like(q)
        qs += [jnp.where(lane < DIFF_DK, q, zero), jnp.where(lane >= DIFF_DK, q, zero)]
    nc = 2 * hp

    hq = tq // 2
    diag = pl.multiple_of(i * tq, tq)

    def pass1(j, carry):
        for c in range(nc):
            h = c // 2
            kb = k_ref[0, pl.ds(pl.multiple_of(j * tq, tq), tq), h * LANES:(h + 1) * LANES]
            sc = _dot_nt(qs[c], kb)
            sbuf[c, j] = sc
            stat[c] = jnp.maximum(stat[c], _lane_fold(sc, jnp.maximum))
        return carry

    stat[...] = jnp.full(stat.shape, NEG_INF, F32)
    lax.fori_loop(0, i, pass1, 0)

    r = lax.broadcasted_iota(I32, (hq, hq), 0)
    cc = lax.broadcasted_iota(I32, (hq, hq), 1)
    tri = cc <= r
    for c in range(nc):
        h = c // 2
        k_lo = k_ref[0, pl.ds(diag, hq), h * LANES:(h + 1) * LANES]
        k_hi = k_ref[0, pl.ds(diag + hq, hq), h * LANES:(h + 1) * LANES]
        s_tl = jnp.where(tri, _dot_nt(qs[c][:hq], k_lo), NEG_INF)
        s_bl = _dot_nt(qs[c][hq:], k_lo)
        s_br = jnp.where(tri, _dot_nt(qs[c][hq:], k_hi), NEG_INF)
        sbuf[c, i, 0:hq, 0:hq] = s_tl
        sbuf[c, i, hq:tq, 0:hq] = s_bl
        sbuf[c, i, hq:tq, hq:tq] = s_br
        stat[c, 0:hq, :] = jnp.maximum(stat[c, 0:hq, :], _lane_fold(s_tl, jnp.maximum))
        stat[c, hq:tq, :] = jnp.maximum(stat[c, hq:tq, :],
                                        jnp.maximum(_lane_fold(s_bl, jnp.maximum), _lane_fold(s_br, jnp.maximum)))
    ms = [jnp.max(stat[c], axis=-1, keepdims=True) for c in range(nc)]

    ones = jnp.ones((tq, LANES), BF16)
    for h in range(hp):
        vb = v_ref[0, pl.ds(diag, tq), h * LANES:(h + 1) * LANES]
        v_ext = jnp.concatenate([vb, ones], axis=1)
        tops, bots = [], []
        for c in (2 * h, 2 * h + 1):
            tops.append(jnp.exp2(sbuf[c, i, 0:hq, 0:hq] - ms[c][:hq]))
            bots.append(jnp.exp2(sbuf[c, i, hq:tq, :] - ms[c][hq:]))
        top = _dot(jnp.concatenate(tops, axis=0).astype(BF16), v_ext[:hq])
        bot = _dot(jnp.concatenate(bots, axis=0).astype(BF16), v_ext)
        acc[h, 0:hq, :] = top[:hq]
        acc[h, hq:tq, :] = bot[:hq]
        acc[h, tq:tq + hq, :] = top[hq:]
        acc[h, tq + hq:2 * tq, :] = bot[hq:]

    def pass2(j, carry):
        for h in range(hp):
            vb = v_ref[0, pl.ds(pl.multiple_of(j * tq, tq), tq), h * LANES:(h + 1) * LANES]
            v_ext = jnp.concatenate([vb, ones], axis=1)
            p0 = jnp.exp2(sbuf[2 * h, j] - ms[2 * h])
            p1 = jnp.exp2(sbuf[2 * h + 1, j] - ms[2 * h + 1])
            acc[h] += _dot(jnp.concatenate([p0, p1], axis=0).astype(BF16), v_ext)
        return carry

    lax.fori_loop(0, i, pass2, 0)

    lf = lam_ref[...]
    lam = (jnp.exp(jnp.sum(lf[0:1] * lf[1:2], keepdims=True))
           - jnp.exp(jnp.sum(lf[2:3] * lf[3:4], keepdims=True)) + lambda_init)
    for h in range(hp):
        a0 = acc[h, 0:tq, :]
        a1 = acc[h, tq:2 * tq, :]
        o = a0[:, :LANES] / a0[:, LANES:] - lam * (a1[:, :LANES] / a1[:, LANES:])
        o_ref[0, :, h * LANES:(h + 1) * LANES] = (_rms(o, g_ref[...]) * (1.0 - lambda_init)).astype(BF16)


def _diff_attn(q, k, v, lam_vecs, subln_g, lambda_init):
    b, s, _ = q.shape
    tq = 512
    hp = 4
    blk = lambda bi, hi, qi: (bi, qi, hi)
    full = lambda bi, hi, qi: (bi, 0, hi)
    return pl.pallas_call(
        functools.partial(_diff_attn_kernel, tq=tq, hp=hp, lambda_init=lambda_init),
        grid=(b, DIFF_HEADS // hp, s // tq),
        in_specs=[pl.BlockSpec((1, tq, hp * LANES), blk),
                  pl.BlockSpec((1, s, hp * LANES), full),
                  pl.BlockSpec((1, s, hp * LANES), full),
                  pl.BlockSpec((4, DIFF_DK), lambda bi, hi, qi: (0, 0)),
                  pl.BlockSpec((1, DIFF_DV), lambda bi, hi, qi: (0, 0))],
        out_specs=pl.BlockSpec((1, tq, hp * LANES), blk),
        out_shape=jax.ShapeDtypeStruct((b, s, DIFF_HEADS * DIFF_DV), BF16),
        scratch_shapes=[pltpu.VMEM((2 * hp, s // tq, tq, tq), F32),
                        pltpu.VMEM((2 * hp, tq, LANES), F32),
                        pltpu.VMEM((hp, 2 * tq, DIFF_DV + LANES), F32)],
        compiler_params=_cparams(("arbitrary", "arbitrary", "arbitrary")),
        name="diff_attn",
    )(q, k, v, lam_vecs, subln_g.reshape(1, DIFF_DV))


def _swa_front_kernel(x_ref, g_ref, w_ref, b_ref, cq_ref, suq_ref, sdq_ref, ck_ref, suk_ref, sdk_ref,
                      q_ref, kv_ref, *, half):
    nq = SWA_Q_HEADS * SWA_HEAD_DIM
    nkv = SWA_KV_HEADS * SWA_HEAD_DIM
    tm = x_ref.shape[1]
    rh = tm // FRONT_CHAINS
    starts = [ch * rh for ch in range(FRONT_CHAINS)]
    hs = [_rms(x_ref[0, r0:r0 + rh, :], g_ref[...]).astype(BF16) for r0 in starts]
    for r0, h in zip(starts, hs):
        cq, suq, sdq = cq_ref[r0:r0 + rh, :], suq_ref[r0:r0 + rh, :], sdq_ref[r0:r0 + rh, :]
        pq = _dot(h, w_ref[:, 0:nq]) + b_ref[:, 0:nq]
        for j in range(nq // LANES):
            lo = j * LANES
            q_ref[0, r0:r0 + rh, lo:lo + LANES] = _rope_chunk(pq[:, lo:lo + LANES], cq, suq, sdq, half).astype(BF16)
    for r0, h in zip(starts, hs):
        ck, suk, sdk = ck_ref[r0:r0 + rh, :], suk_ref[r0:r0 + rh, :], sdk_ref[r0:r0 + rh, :]
        pkv = _dot(h, w_ref[:, nq:]) + b_ref[:, nq:]
        for j in range(nkv // LANES):
            lo = j * LANES
            kv_ref[0, r0:r0 + rh, lo:lo + LANES] = _rope_chunk(pkv[:, lo:lo + LANES], ck, suk, sdk, half).astype(BF16)
        kv_ref[0, r0:r0 + rh, nkv:] = pkv[:, nkv:].astype(BF16)


def _swa_front(x, g, w_qkv, b_qkv, tabs_q, tabs_k, half):
    b, s, d = x.shape
    tm = ROW_TILE
    n = w_qkv.shape[1]
    nq = SWA_Q_HEADS * SWA_HEAD_DIM
    nkv = SWA_KV_HEADS * SWA_HEAD_DIM
    tab_spec = pl.BlockSpec((tm, LANES), lambda bi, si: (si, 0))
    row = lambda width: pl.BlockSpec((1, tm, width), lambda bi, si: (bi, si, 0))
    return pl.pallas_call(
        functools.partial(_swa_front_kernel, half=half),
        grid=(b, s // tm),
        in_specs=[row(d),
                  pl.BlockSpec((1, d), lambda bi, si: (0, 0)),
                  pl.BlockSpec((d, n), lambda bi, si: (0, 0)),
                  pl.BlockSpec((1, n), lambda bi, si: (0, 0)),
                  tab_spec, tab_spec, tab_spec, tab_spec, tab_spec, tab_spec],
        out_specs=[row(nq), row(2 * nkv)],
        out_shape=[jax.ShapeDtypeStruct((b, s, nq), BF16),
                   jax.ShapeDtypeStruct((b, s, 2 * nkv), BF16)],
        compiler_params=_cparams(("arbitrary", "arbitrary")),
        name="swa_front",
    )(x, g.reshape(1, d), w_qkv, b_qkv.reshape(1, n), *tabs_q, *tabs_k)


def _swa_head_order():
    g_sz = SWA_Q_HEADS // SWA_KV_HEADS
    order = []
    for slab in range(SWA_Q_HEADS // 2):
        pair, j = slab // g_sz, slab % g_sz
        order += [(2 * pair) * g_sz + j, (2 * pair + 1) * g_sz + j]
    return order


def _swa_attn_kernel(sink_ref, q_ref, kvp_ref, kvc_ref, o_ref, *, nblk):
    i = pl.program_id(1)
    hd = SWA_HEAD_DIM
    nkv = SWA_KV_HEADS * hd
    g_sz = SWA_Q_HEADS // SWA_KV_HEADS
    kv = jnp.concatenate([kvp_ref[0], kvc_ref[0]], axis=0)
    r = lax.broadcasted_iota(I32, (BLOCK, 2 * BLOCK), 0)
    c = lax.broadcasted_iota(I32, (BLOCK, 2 * BLOCK), 1)
    rel = c - BLOCK - r
    in_win = (rel <= 0) & (rel > -SWA_WINDOW)
    lane = lax.broadcasted_iota(I32, (1, LANES), 1)
    lo_half = lane < hd
    ones = jnp.ones((2 * BLOCK, LANES), BF16)
    for n in range(nblk):
        mask = in_win & ((c >= BLOCK) | (i > 0)) if n == 0 else in_win
        keys = kv[n * BLOCK:(n + 2) * BLOCK]
        for pair in range(SWA_KV_HEADS // 2):
            k2 = keys[:, pair * LANES:(pair + 1) * LANES]
            v2 = keys[:, nkv + pair * LANES:nkv + (pair + 1) * LANES]
            v_ext = jnp.concatenate([v2, ones], axis=1)
            pieces = []
            for j in range(g_sz):
                slab = pair * g_sz + j
                qs = q_ref[0, n * BLOCK:(n + 1) * BLOCK, slab * LANES:(slab + 1) * LANES]
                zero = jnp.zeros_like(qs)
                pieces += [jnp.where(lo_half, qs, zero), jnp.where(lo_half, zero, qs)]
            sc = _dot_nt(jnp.concatenate(pieces, axis=0), k2)
            probs, tails = [], []
            for pc in range(2 * g_sz):
                scp = jnp.where(mask, sc[pc * BLOCK:(pc + 1) * BLOCK], NEG_INF)
                sink = sink_ref[2 * g_sz * pair + pc] * LOG2E
                m = jnp.maximum(jnp.max(scp, axis=-1, keepdims=True), sink)
                probs.append(jnp.exp2(scp - m).astype(BF16))
                tails.append(jnp.exp2(sink - m))
            pv = _dot(jnp.concatenate(probs, axis=0), v_ext)
            for j in range(g_sz):
                slab = pair * g_sz + j
                halves = []
                for hf in range(2):
                    pc = 2 * j + hf
                    blk = pv[pc * BLOCK:(pc + 1) * BLOCK]
                    halves.append(blk[:, :LANES] / (blk[:, LANES:] + tails[pc]))
                o_ref[0, n * BLOCK:(n + 1) * BLOCK, slab * LANES:(slab + 1) * LANES] = (
                    jnp.where(lo_half, halves[0], halves[1]).astype(BF16))


def _swa_attn(q, kv, sinks):
    b, s, nq = q.shape
    nblk = 8
    tq = nblk * BLOCK
    return pl.pallas_call(
        functools.partial(_swa_attn_kernel, nblk=nblk),
        grid_spec=pltpu.PrefetchScalarGridSpec(
            num_scalar_prefetch=1,
            grid=(b, s // tq),
            in_specs=[pl.BlockSpec((1, tq, nq), lambda bi, ni, sk: (bi, ni, 0)),
                      pl.BlockSpec((1, BLOCK, kv.shape[2]), lambda bi, ni, sk: (bi, jnp.maximum(ni * nblk - 1, 0), 0)),
                      pl.BlockSpec((1, tq, kv.shape[2]), lambda bi, ni, sk: (bi, ni, 0))],
            out_specs=pl.BlockSpec((1, tq, nq), lambda bi, ni, sk: (bi, ni, 0))),
        out_shape=jax.ShapeDtypeStruct((b, s, nq), BF16),
        compiler_params=_cparams(("arbitrary", "arbitrary")),
        name="swa_attn",
    )(sinks, q, kv, kv)


def _post_kernel(*refs, tm, n_a, has_bias):
    x_ref = refs[0]
    a_refs = refs[1:1 + n_a]
    k = 1 + n_a
    wo_refs = refs[k:k + n_a]
    k += n_a
    if has_bias:
        bo_ref = refs[k]
        k += 1
    (gx_ref, wq_ref, mkv_ref, wxo_ref, gf_ref, wr_ref, br_ref,
     x2_ref, h2_ref, mi_ref, mw_ref, cnt_ref, cnt_acc, cnt_col) = refs[k:]
    first = (pl.program_id(0) == 0) & (pl.program_id(1) == 0)

    @pl.when(first)
    def _():
        cnt_acc[...] = jnp.zeros_like(cnt_acc)
        cnt_col[...] = jnp.zeros_like(cnt_col)

    xw = XATTN_HEADS * XATTN_HEAD_DIM
    ones = jnp.ones((mkv_ref.shape[1], LANES), BF16)

    rh = tm // POST_CHAINS

    def out_proj(r0):
        acc = _dot(a_refs[0][0, r0:r0 + rh, :], wo_refs[0][...])
        for a_ref, w_ref in zip(a_refs[1:], wo_refs[1:]):
            acc = acc + _dot(a_ref[0, r0:r0 + rh, :], w_ref[...])
        if has_bias:
            acc = acc + bo_ref[...]
        return x_ref[0, r0:r0 + rh, :] + acc

    def q_proj(x1):
        hx = _rms(x1, gx_ref[...]).astype(BF16)
        return (_dot(hx, wq_ref[...]) * (XATTN_HEAD_DIM ** -0.5 * LOG2E)).astype(BF16)

    def mem_attn(qx):
        outs = []
        for hh in range(XATTN_HEADS):
            lo = hh * XATTN_HEAD_DIM
            mk = mkv_ref[0, :, lo:lo + XATTN_HEAD_DIM]
            mv = jnp.concatenate([mkv_ref[0, :, xw + lo:xw + lo + XATTN_HEAD_DIM], ones], axis=1)
            sc = _dot_nt(qx[:, lo:lo + XATTN_HEAD_DIM], mk)
            m = jnp.max(sc, axis=-1, keepdims=True)
            pv = _dot(jnp.exp2(sc - m).astype(BF16), mv)
            outs.append((pv[:, :LANES] / pv[:, LANES:]).astype(BF16))
        return jnp.concatenate(outs, axis=-1)

    def o_proj(r0, x1, ox):
        x2 = x1 + _dot(ox, wxo_ref[...])
        x2_ref[0, r0:r0 + rh, :] = x2
        return x2

    def router(r0, x2):
        h2 = _rms(x2, gf_ref[...])
        _to_token_tiles(h2_ref, h2, rh, r0)
        return _dot(h2.astype(BF16), wr_ref[...]) + br_ref[...]

    starts = [c * rh for c in range(POST_CHAINS)]
    x1s = [out_proj(r0) for r0 in starts]
    qxs = [q_proj(x1) for x1 in x1s]
    oxs = [mem_attn(qx) for qx in qxs]
    x2s = [o_proj(r0, x1, ox) for r0, x1, ox in zip(starts, x1s, oxs)]
    logits = jnp.concatenate([router(r0, x2) for r0, x2 in zip(starts, x2s)], axis=0)

    lt = logits.T
    ex = lt[0:N_EXPERTS]
    grp = lt[N_EXPERTS:N_EXPERTS + SUBLANES]
    grow = lax.broadcasted_iota(I32, grp.shape, 0).astype(F32)
    lg = jnp.where(grow < N_GROUPS, grp, NEG_INF)
    mg = jnp.max(lg, axis=0, keepdims=True)
    g_sel = jnp.min(jnp.where(lg == mg, grow, float(SUBLANES)), axis=0, keepdims=True)
    p_g = 1.0 / jnp.sum(jnp.exp(lg - mg), axis=0, keepdims=True)
    erow = lax.broadcasted_iota(I32, ex.shape, 0).astype(F32)
    e_lo = g_sel * EXPERTS_PER_GROUP
    le = jnp.where((erow >= e_lo) & (erow < e_lo + EXPERTS_PER_GROUP), ex, NEG_INF)
    big = float(N_EXPERTS)
    m1 = jnp.max(le, axis=0, keepdims=True)
    i1 = jnp.min(jnp.where(le == m1, erow, big), axis=0, keepdims=True)
    le2 = jnp.where(erow == i1, NEG_INF, le)
    m2 = jnp.max(le2, axis=0, keepdims=True)
    i2 = jnp.min(jnp.where(le2 == m2, erow, big), axis=0, keepdims=True)
    t = jnp.exp(m2 - m1)
    w1 = p_g / (1.0 + t)
    w2 = p_g * t / (1.0 + t)

    oh1 = erow == i1
    oh2 = erow == i2
    oh = jnp.where(oh1 | oh2, 1.0, 0.0)
    rr = lax.broadcasted_iota(I32, (tm, tm), 0)
    cc = lax.broadcasted_iota(I32, (tm, tm), 1)
    earlier = jnp.where(rr < cc, 1.0, 0.0).astype(BF16)
    before = _dot(oh.astype(BF16), earlier) + cnt_col[...]
    r1 = jnp.sum(jnp.where(oh1, before, 0.0), axis=0, keepdims=True)
    r2 = jnp.sum(jnp.where(oh2, before, 0.0), axis=0, keepdims=True)
    cnt_col[...] = cnt_col[...] + jnp.sum(oh, axis=1, keepdims=True)
    oh_pad = jnp.concatenate([oh, jnp.zeros((LANES - N_EXPERTS, tm), F32)], axis=0).astype(BF16)
    cnt_new = cnt_acc[...] + _dot_nt(jnp.ones((SUBLANES, tm), BF16), oh_pad)[0:1]
    cnt_acc[...] = cnt_new
    cnt_ref[...] = cnt_new.astype(I32)

    zrow = jnp.zeros((1, tm), F32)
    mi_ref[...] = jnp.concatenate([i1, i2, r1, r2, zrow, zrow, zrow, zrow], axis=0).astype(I32)
    wt = jnp.concatenate([w1, w2, jnp.zeros((LANES - TOP_K, tm), F32)], axis=0)
    mw_ref[...] = wt.T


def _post_mixer(x, a_list, wo_list, b_out, g_x, w_q, mkv, w_xo, g_f, w_rt, b_rt):
    b, s, d = x.shape
    tm = ROW_TILE
    ns = s // tm
    t = b * s
    n_a = len(a_list)
    has_bias = b_out is not None
    const2 = lambda bi, si: (0, 0)
    row = lambda width: pl.BlockSpec((1, tm, width), lambda bi, si: (bi, si, 0))
    in_specs = [row(d)] + [row(a.shape[2]) for a in a_list]
    in_specs += [pl.BlockSpec(w.shape, const2) for w in wo_list]
    args = [x, *a_list, *wo_list]
    if has_bias:
        in_specs.append(pl.BlockSpec((1, d), const2))
        args.append(b_out.reshape(1, d))
    xw = w_q.shape[1]
    in_specs += [pl.BlockSpec((1, d), const2),
                 pl.BlockSpec((d, xw), const2),
                 pl.BlockSpec((1, mkv.shape[1], mkv.shape[2]), lambda bi, si: (bi, 0, 0)),
                 pl.BlockSpec((xw, d), const2),
                 pl.BlockSpec((1, d), const2),
                 pl.BlockSpec((d, LANES), const2),
                 pl.BlockSpec((1, LANES), const2)]
    args += [g_x.reshape(1, d), w_q, mkv, w_xo, g_f.reshape(1, d), w_rt, b_rt]
    out_specs = [row(d),
                 pl.BlockSpec((tm * TOKEN_ROWS, LANES), lambda bi, si: (bi * ns + si, 0)),
                 pl.BlockSpec((8, tm), lambda bi, si: (0, bi * ns + si)),
                 pl.BlockSpec((tm, LANES), lambda bi, si: (bi * ns + si, 0)),
                 pl.BlockSpec((1, LANES), const2)]
    out_shape = [jax.ShapeDtypeStruct((b, s, d), F32),
                 jax.ShapeDtypeStruct((t * TOKEN_ROWS, LANES), U32),
                 jax.ShapeDtypeStruct((8, t), I32),
                 jax.ShapeDtypeStruct((t, LANES), F32),
                 jax.ShapeDtypeStruct((1, LANES), I32)]
    return pl.pallas_call(
        functools.partial(_post_kernel, tm=tm, n_a=n_a, has_bias=has_bias),
        grid=(b, ns),
        in_specs=in_specs,
        out_specs=out_specs,
        out_shape=out_shape,
        scratch_shapes=[pltpu.VMEM((1, LANES), F32), pltpu.VMEM((N_EXPERTS, 1), F32)],
        compiler_params=_cparams(("arbitrary", "arbitrary")),
        name="post_mixer",
    )(*args)


def _to_token_tiles(dst_ref, val, rows, tok0=0):
    half = TOKEN_ROWS * LANES
    for j in range(TOKEN_ROWS):
        hi = val[:, j * LANES:(j + 1) * LANES].astype(BF16).astype(F32)
        lo = val[:, half + j * LANES:half + (j + 1) * LANES].astype(BF16).astype(F32)
        word = lax.bitcast_convert_type(hi, U32) | (lax.bitcast_convert_type(lo, U32) >> 16)
        dst_ref[pl.ds(tok0 * TOKEN_ROWS + j, rows, stride=TOKEN_ROWS), :] = word


def _from_token_tiles(src_ref, rows, tok0=0):
    his, los = [], []
    for j in range(TOKEN_ROWS):
        word = src_ref[pl.ds(tok0 * TOKEN_ROWS + j, rows, stride=TOKEN_ROWS), :]
        his.append(lax.bitcast_convert_type(word & jnp.uint32(0xFFFF0000), F32))
        los.append(lax.bitcast_convert_type(word << 16, F32))
    return jnp.concatenate(his + los, axis=1)


def _token_slice(ref, tok):
    return ref.at[pl.ds(pl.multiple_of(tok * TOKEN_ROWS, TOKEN_ROWS), TOKEN_ROWS)]


def _pos_kernel(off_ref, mi_ref, pos_ref):
    e = mi_ref[0:TOP_K, :]
    pos = mi_ref[TOP_K:2 * TOP_K, :]
    for j in range(N_EXPERTS):
        pos = pos + jnp.where(e == j, off_ref[j], 0)
    pos_ref[...] = pos


def _sorted_positions(off, meta_i):
    t = meta_i.shape[1]
    return pl.pallas_call(
        _pos_kernel,
        grid_spec=pltpu.PrefetchScalarGridSpec(
            num_scalar_prefetch=1,
            grid=(1,),
            in_specs=[pl.BlockSpec((8, t), lambda i, off: (0, 0))],
            out_specs=pl.BlockSpec((TOP_K, t), lambda i, off: (0, 0))),
        out_shape=jax.ShapeDtypeStruct((TOP_K, t), I32),
        compiler_params=_cparams(("arbitrary",)),
        name="moe_positions",
    )(off, meta_i)


def _tile_major(pos, tm):
    t = pos.shape[1]
    return pos.reshape(TOP_K, t // tm, tm).transpose(1, 0, 2).reshape(-1)


def _dispatch_kernel(pos_ref, h_ref, xs_ref, sem, *, tm):
    def issue(g, _):
        for u in range(DMA_UNROLL):
            r = g * DMA_UNROLL + u
            for kk in range(TOP_K):
                pos = pos_ref[kk * tm + r]
                pltpu.make_async_copy(_token_slice(h_ref, r), _token_slice(xs_ref, pos), sem).start(priority=kk)
        return 0

    lax.fori_loop(0, tm // DMA_UNROLL, issue, 0)
    for _ in range(TOP_K):
        pltpu.make_async_copy(h_ref, xs_ref.at[pl.ds(0, tm * TOKEN_ROWS)], sem).wait()


def _dispatch(pos, h2t, n_rows):
    tm = DISPATCH_TILE
    t = h2t.shape[0] // TOKEN_ROWS
    return pl.pallas_call(
        functools.partial(_dispatch_kernel, tm=tm),
        grid=(t // tm,),
        in_specs=[pl.BlockSpec((TOP_K * tm,), lambda i: (i,), memory_space=pltpu.SMEM),
                  pl.BlockSpec((tm * TOKEN_ROWS, LANES), lambda i: (i, 0))],
        out_specs=pl.BlockSpec(memory_space=pl.ANY),
        scratch_shapes=[pltpu.SemaphoreType.DMA],
        out_shape=jax.ShapeDtypeStruct((n_rows * TOKEN_ROWS, LANES), U32),
        compiler_params=_cparams(("arbitrary",)),
        name="moe_dispatch",
    )(_tile_major(pos, tm), h2t)


def _expert_kernel(tblk_ref, texp_ref, tn_ref, tnxt_ref, tpar_ref, xs_ref, wgu_hbm, wdn_hbm, ys_ref,
                   wgu_bf, wdn_bf, wgu_f, wdn_f, sems, *, tm, layer):
    i = pl.program_id(0)
    n_valid = tn_ref[i]
    expert = texp_ref[i]
    slot = tpar_ref[i]
    new_expert = (i == 0) | (expert != texp_ref[jnp.maximum(i - 1, 0)])

    def weight_copies(ex, sl):
        return (pltpu.make_async_copy(wgu_hbm.at[layer, ex], wgu_f.at[sl], sems.at[sl]),
                pltpu.make_async_copy(wdn_hbm.at[layer, ex], wdn_f.at[sl], sems.at[sl]))

    @pl.when(i == 0)
    def _():
        for cp in weight_copies(expert, slot):
            cp.start()

    @pl.when(new_expert)
    def _():
        @pl.when(tnxt_ref[i] >= 0)
        def _():
            for cp in weight_copies(tnxt_ref[i], 1 - slot):
                cp.start()

        for cp in weight_copies(expert, slot):
            cp.wait()
        wgu_bf[...] = wgu_f[slot].astype(BF16)
        wdn_bf[...] = wdn_f[slot].astype(BF16)

    @pl.when(n_valid > 0)
    def _():
        rh = tm // EXPERT_CHAINS
        starts = [c * rh for c in range(EXPERT_CHAINS)]
        row = lax.broadcasted_iota(I32, (rh, 1), 0)
        gus = []
        for r0 in starts:
            x = jnp.where(row + r0 < n_valid, _from_token_tiles(xs_ref, rh, r0), 0.0).astype(BF16)
            gus.append(_dot(x, wgu_bf[...]))
        ys = []
        for gu in gus:
            hid = (gu[:, :EXPERT_HIDDEN] / (1.0 + jnp.exp(-gu[:, :EXPERT_HIDDEN])) * gu[:, EXPERT_HIDDEN:]).astype(BF16)
            ys.append(_dot(hid, wdn_bf[...]))
        for r0, y in zip(starts, ys):
            _to_token_tiles(ys_ref, y, rh, r0)


def _experts(plan, xs, w_gu, w_dn, layer):
    tm = EXPERT_TILE
    nt = xs.shape[0] // (tm * TOKEN_ROWS)
    gu_shape, dn_shape = w_gu.shape[2:], w_dn.shape[2:]
    rows = lambda i, tb, te, tn, tx, tp: (tb[i], 0)
    return pl.pallas_call(
        functools.partial(_expert_kernel, tm=tm, layer=layer),
        grid_spec=pltpu.PrefetchScalarGridSpec(
            num_scalar_prefetch=5,
            grid=(nt,),
            in_specs=[pl.BlockSpec((tm * TOKEN_ROWS, LANES), rows),
                      pl.BlockSpec(memory_space=pl.ANY),
                      pl.BlockSpec(memory_space=pl.ANY)],
            out_specs=pl.BlockSpec((tm * TOKEN_ROWS, LANES), rows),
            scratch_shapes=[pltpu.VMEM(gu_shape, BF16), pltpu.VMEM(dn_shape, BF16),
                            pltpu.VMEM((2,) + gu_shape, F32), pltpu.VMEM((2,) + dn_shape, F32),
                            pltpu.SemaphoreType.DMA((2,))]),
        out_shape=jax.ShapeDtypeStruct(xs.shape, U32),
        compiler_params=_cparams(("arbitrary",)),
        name="moe_experts",
    )(*plan, xs, w_gu, w_dn)


def _moe_combine_tile(pos_cur, pos_nxt, mw_ref, x, ys_ref, ybuf, sems, i, n, tm):
    def gather(pos_ref, slot):
        def issue(g, _):
            for u in range(DMA_UNROLL):
                r = g * DMA_UNROLL + u
                for kk in range(TOP_K):
                    pos = pos_ref[kk * tm + r]
                    pltpu.make_async_copy(_token_slice(ys_ref, pos), _token_slice(ybuf.at[slot, kk], r),
                                          sems.at[slot]).start(priority=kk)
            return 0

        lax.fori_loop(0, tm // DMA_UNROLL, issue, 0)

    @pl.when(i == 0)
    def _():
        gather(pos_cur, 0)

    @pl.when(i + 1 < n)
    def _():
        gather(pos_nxt, (i + 1) % 2)

    slot = i % 2
    for kk in range(TOP_K):
        pltpu.make_async_copy(ys_ref.at[pl.ds(0, tm * TOKEN_ROWS)], ybuf.at[slot, kk], sems.at[slot]).wait()
    mw = mw_ref[...]
    return (x + mw[:, 0:1] * _from_token_tiles(ybuf.at[slot, 0], tm)
            + mw[:, 1:2] * _from_token_tiles(ybuf.at[slot, 1], tm))


def _combine_kernel(pos_cur, pos_nxt, mw_ref, x_ref, ys_ref, *rest, tm, final):
    if final:
        g_ref, o_ref, ybuf, sems = rest
    else:
        o_ref, ybuf, sems = rest
    x3 = _moe_combine_tile(pos_cur, pos_nxt, mw_ref, x_ref[...], ys_ref, ybuf, sems,
                           pl.program_id(0), pl.num_programs(0), tm)
    if final:
        x3 = _rms(x3, g_ref[...])
    o_ref[...] = x3


def _combine(moe, g_final):
    pos, meta_w, x2, ys = moe
    d = x2.shape[-1]
    x2d = x2.reshape(-1, d)
    t = x2d.shape[0]
    tm = COMBINE_TILE
    nt = t // tm
    final = g_final is not None
    in_specs = [pl.BlockSpec((TOP_K * tm,), lambda i: (i,), memory_space=pltpu.SMEM),
                pl.BlockSpec((TOP_K * tm,), lambda i: (jnp.minimum(i + 1, nt - 1),), memory_space=pltpu.SMEM),
                pl.BlockSpec((tm, LANES), lambda i: (i, 0)),
                pl.BlockSpec((tm, d), lambda i: (i, 0)),
                pl.BlockSpec(memory_space=pl.ANY)]
    pos_flat = _tile_major(pos, tm)
    args = [pos_flat, pos_flat, meta_w, x2d, ys]
    if final:
        in_specs.append(pl.BlockSpec((1, d), lambda i: (0, 0)))
        args.append(g_final.reshape(1, d))
    return pl.pallas_call(
        functools.partial(_combine_kernel, tm=tm, final=final),
        grid=(nt,),
        in_specs=in_specs,
        out_specs=pl.BlockSpec((tm, d), lambda i: (i, 0)),
        scratch_shapes=[pltpu.VMEM((2, TOP_K, tm * TOKEN_ROWS, LANES), U32), pltpu.SemaphoreType.DMA((2,))],
        out_shape=jax.ShapeDtypeStruct((t, d), F32),
        compiler_params=_cparams(("arbitrary",)),
        name="moe_combine",
    )(*args)


def _plan_kernel(cnt_ref, off_ref, blk_ref, exp_ref, nv_ref, nxt_ref, par_ref, *, tm, n_tiles):
    shift = tm.bit_length() - 1

    def clear(i, _):
        nxt_ref[i] = -1
        return 0

    lax.fori_loop(0, n_tiles, clear, 0)

    def per_expert(e, carry):
        row0, run, prev_first = carry
        c = cnt_ref[0, e]
        ntile = lax.shift_right_logical(c + (tm - 1), shift)
        off_ref[e] = row0
        t0 = lax.shift_right_logical(row0, shift)

        def fill(j, _):
            blk_ref[t0 + j] = t0 + j
            exp_ref[t0 + j] = e
            nv_ref[t0 + j] = jnp.minimum(c - j * tm, tm)
            par_ref[t0 + j] = run & 1
            return 0

        lax.fori_loop(0, ntile, fill, 0)
        has = ntile > 0

        @pl.when(has & (prev_first >= 0))
        def _():
            nxt_ref[prev_first] = e

        return (row0 + lax.shift_left(ntile, shift), run + has.astype(I32), jnp.where(has, t0, prev_first))

    total, _, _ = lax.fori_loop(0, N_EXPERTS, per_expert, (jnp.int32(0), jnp.int32(0), jnp.int32(-1)))
    used = lax.shift_right_logical(total, shift)
    last = jnp.maximum(used - 1, 0)

    def tail(i, _):
        blk_ref[i] = last
        exp_ref[i] = exp_ref[last]
        nv_ref[i] = 0
        par_ref[i] = par_ref[last]
        return 0

    lax.fori_loop(used, n_tiles, tail, 0)


def _moe_plan(counts, n_tiles):
    tm = EXPERT_TILE
    assert tm & (tm - 1) == 0
    smem = pl.BlockSpec(memory_space=pltpu.SMEM)
    return pl.pallas_call(
        functools.partial(_plan_kernel, tm=tm, n_tiles=n_tiles),
        in_specs=[smem],
        out_specs=[smem] * 6,
        out_shape=[jax.ShapeDtypeStruct((N_EXPERTS,), I32)] + [jax.ShapeDtypeStruct((n_tiles,), I32)] * 5,
        name="moe_plan",
    )(counts)


def _moe(x2, h2t, meta_i, meta_w, counts, w_gu, w_dn, layer):
    b, s, d = x2.shape
    assert d == 2 * TOKEN_ROWS * LANES
    n_rows = b * s * TOP_K + N_EXPERTS * EXPERT_TILE
    off, *tile_plan = _moe_plan(counts, n_rows // EXPERT_TILE)
    pos = _sorted_positions(off, meta_i)
    xs = _dispatch(pos, h2t, n_rows)
    ys = _experts(tile_plan, xs, w_gu, w_dn, layer)
    return pos, meta_w, x2, ys


def _router_weights(w_group, b_group, w_router, b_router):
    d = w_group.shape[0]
    pad = LANES - N_EXPERTS - N_GROUPS
    w = jnp.concatenate([w_router, w_group, jnp.zeros((d, pad), F32)], axis=1)
    bias = jnp.concatenate([b_router, b_group, jnp.zeros((pad,), F32)]).reshape(1, LANES)
    return w.astype(BF16), bias


def kernel(x, mem, mem_norm, mem_w_kv, norm_mix, norm_xattn, norm_ffn, hyb_w_in, hyb_conv_w, diff_lambda, diff_subln, hyb_w_out, swa_w_qkv, swa_b_qkv, swa_sinks, swa_w_out, swa_b_out, xattn_w_q, xattn_w_o, moe_w_group, moe_b_group, moe_w_router, moe_b_router, moe_w_gate_up, moe_w_down, final_norm):
    b, s, d = x.shape
    m = mem.shape[1]
    depth = norm_mix.shape[0]
    assert d == 2 * TOKEN_ROWS * LANES and s % ROW_TILE == 0 and s % (4 * BLOCK) == 0
    assert (b * s) % DISPATCH_TILE == 0 and (b * s) % COMBINE_TILE == 0 and DISPATCH_TILE % DMA_UNROLL == 0
    mem_tile = math.gcd(b * m, ROW_TILE)
    mkv = _norm_proj(mem.reshape(b * m, d), mem_norm, mem_w_kv.astype(BF16), mem_tile).reshape(b, m, -1)

    scale = DIFF_DK ** -0.5 * LOG2E
    cq, suq, sdq, half = _rope_lane_tables(s, DIFF_DK, scale)
    ck, suk, sdk, _ = _rope_lane_tables(s, DIFF_DK, 1.0)
    tabs_q, tabs_k = (cq, suq, sdq), (ck, suk, sdk)

    moe = None
    for l in range(depth):
        if moe is not None:
            x = _combine(moe, None).reshape(b, s, d)
        if l % 2 == 0:
            e = l // 2
            lambda_init = 0.8 - 0.6 * math.exp(-0.3 * l)
            ya, q, k, v = _hyb_front(x, norm_mix[l], hyb_w_in[e].astype(BF16), hyb_conv_w[e], tabs_q, tabs_k, half)
            o = _diff_attn(q, k, v, diff_lambda[e], diff_subln[e], lambda_init)
            w_out = hyb_w_out[e].astype(BF16)
            a_list, wo_list, b_out = [ya, o], [w_out[:CONV_CH], w_out[CONV_CH:]], None
        else:
            e = l // 2
            order = _swa_head_order()
            hd, nq = SWA_HEAD_DIM, SWA_Q_HEADS * SWA_HEAD_DIM
            heads = lambda a, axis: [lax.slice_in_dim(a, h * hd, (h + 1) * hd, axis=axis) for h in order]
            w_all = swa_w_qkv[e].astype(BF16)
            w_qkv = jnp.concatenate(heads(w_all, 1) + [w_all[:, nq:]], axis=1)
            b_qkv = jnp.concatenate(heads(swa_b_qkv[e], 0) + [swa_b_qkv[e][nq:]])
            sinks = jnp.stack([swa_sinks[e][h] for h in order])
            q, kv = _swa_front(x, norm_mix[l], w_qkv, b_qkv, tabs_q, tabs_k, half)
            o = _swa_attn(q, kv, sinks)
            w_out = jnp.concatenate(heads(swa_w_out[e].astype(BF16), 0), axis=0)
            a_list, wo_list, b_out = [o], [w_out], swa_b_out[e]
        w_rt, b_rt = _router_weights(moe_w_group[l], moe_b_group[l], moe_w_router[l], moe_b_router[l])
        x2, h2, meta_i, meta_w, counts = _post_mixer(
            x, a_list, wo_list, b_out, norm_xattn[l], xattn_w_q[l].astype(BF16), mkv,
            xattn_w_o[l].astype(BF16), norm_ffn[l], w_rt, b_rt)
        moe = _moe(x2, h2, meta_i, meta_w, counts, moe_w_gate_up, moe_w_down, l)
    return _combine(moe, final_norm).reshape(b, s, d)
```

```python
import functools
import math

import jax
import jax.numpy as jnp
from jax import lax
from jax.experimental import pallas as pl
from jax.experimental.pallas import tpu as pltpu

F32 = jnp.float32
BF16 = jnp.bfloat16
I32 = jnp.int32
U32 = jnp.uint32

EPS = 1e-6
LANES = 128
SUBLANES = 8
TOKEN_ROWS = 4
DMA_UNROLL = 16
VMEM_LIMIT = 56 * 1024 * 1024

ROPE_THETA = 500000.0
ROPE_FRACTION = 4
BLOCK = 128
CONV_CH = 512
CONV_K = 3
DIFF_HEADS = 4
DIFF_DK = 64
DIFF_DV = 128
SWA_Q_HEADS = 16
SWA_KV_HEADS = 4
SWA_HEAD_DIM = 64
SWA_WINDOW = 128
XATTN_HEADS = 4
XATTN_HEAD_DIM = 128
N_GROUPS = 4
EXPERTS_PER_GROUP = 8
N_EXPERTS = N_GROUPS * EXPERTS_PER_GROUP
TOP_K = 2
EXPERT_HIDDEN = 512

ROW_TILE = 1024
EXPERT_TILE = 512
DISPATCH_TILE = 4096
COMBINE_TILE = 512
POST_CHAINS = 4
EXPERT_CHAINS = 2
FRONT_CHAINS = 4
NEG_INF = float("-inf")
LOG2E = math.log2(math.e)


def _cparams(sem):
    return pltpu.CompilerParams(dimension_semantics=sem, vmem_limit_bytes=VMEM_LIMIT)


def _rms(x, g):
    return x * lax.rsqrt(jnp.mean(x * x, axis=-1, keepdims=True) + EPS) * g


def _dot(a, b):
    return jnp.dot(a, b, preferred_element_type=F32)


def _dot_nt(a, b):
    return lax.dot_general(a, b, (((1,), (1,)), ((), ())), preferred_element_type=F32)


def _rope_lane_tables(seq, head_dim, scale):
    rot = head_dim // ROPE_FRACTION
    half = rot // 2
    pos = jnp.arange(seq, dtype=F32)
    inv = ROPE_THETA ** (-jnp.arange(0, rot, 2, dtype=F32) / rot)
    ang = pos[:, None] * inv[None, :]
    cos, sin = jnp.cos(ang), jnp.sin(ang)
    idx = jnp.arange(LANES) % head_dim
    cl = jnp.take(cos, idx % half, axis=1)
    sl = jnp.take(sin, idx % half, axis=1)
    c = jnp.where(idx < rot, cl, 1.0) * scale
    s_up = jnp.where(idx < half, -sl, 0.0) * scale
    s_dn = jnp.where((idx >= half) & (idx < rot), sl, 0.0) * scale
    return c.astype(F32), s_up.astype(F32), s_dn.astype(F32), half


def _rope_chunk(xc, c, s_up, s_dn, half):
    return (xc * c + pltpu.roll(xc, LANES - half, 1) * s_up + pltpu.roll(xc, half, 1) * s_dn)


def _norm_proj_kernel(x_ref, g_ref, w_ref, o_ref):
    h = _rms(x_ref[...], g_ref[...]).astype(BF16)
    o_ref[...] = _dot(h, w_ref[...]).astype(o_ref.dtype)


def _norm_proj(x2d, g, w_bf16, tm):
    m, d = x2d.shape
    n = w_bf16.shape[1]
    return pl.pallas_call(
        _norm_proj_kernel,
        grid=(m // tm,),
        in_specs=[pl.BlockSpec((tm, d), lambda i: (i, 0)),
                  pl.BlockSpec((1, d), lambda i: (0, 0)),
                  pl.BlockSpec((d, n), lambda i: (0, 0))],
        out_specs=pl.BlockSpec((tm, n), lambda i: (i, 0)),
        out_shape=jax.ShapeDtypeStruct((m, n), BF16),
        compiler_params=_cparams(("arbitrary",)),
        name="mem_kv_proj",
    )(x2d, g.reshape(1, d), w_bf16)


def _hyb_front_kernel(x_ref, g_ref, w_ref, cw_ref, cq_ref, suq_ref, sdq_ref, ck_ref, suk_ref, sdk_ref,
                      ya_ref, q_ref, k_ref, v_ref, cbuf, *, tm, half):
    s = pl.program_id(1)
    c = CONV_CH
    base = 3 * c
    nq = DIFF_HEADS * 2 * DIFF_DK
    rh = tm // FRONT_CHAINS
    starts = [ch * rh for ch in range(FRONT_CHAINS)]
    cw = cw_ref[...]

    @pl.when(s == 0)
    def _():
        cbuf[0:8, :] = jnp.zeros((8, c), F32)

    hs = [_rms(x_ref[0, r0:r0 + rh, :], g_ref[...]).astype(BF16) for r0 in starts]
    for r0, h in zip(starts, hs):
        gate_b = _dot(h, w_ref[:, 0:c])
        cu = _dot(h, w_ref[:, c:2 * c]) * _dot(h, w_ref[:, 2 * c:3 * c])
        cbuf[8 + r0:8 + r0 + rh, :] = cu
        conv = (cw[0:1, :] * cbuf[6 + r0:6 + r0 + rh, :] + cw[1:2, :] * cbuf[7 + r0:7 + r0 + rh, :] + cw[2:3, :] * cu)
        ya_ref[0, r0:r0 + rh, :] = (gate_b * conv).astype(BF16)
    cbuf[0:8, :] = cbuf[tm:tm + 8, :]

    for r0, h in zip(starts, hs):
        cq, suq, sdq = cq_ref[r0:r0 + rh, :], suq_ref[r0:r0 + rh, :], sdq_ref[r0:r0 + rh, :]
        pq = _dot(h, w_ref[:, base:base + nq])
        for j in range(nq // LANES):
            q_ref[0, r0:r0 + rh, j * LANES:(j + 1) * LANES] = _rope_chunk(
                pq[:, j * LANES:(j + 1) * LANES], cq, suq, sdq, half).astype(BF16)
    for r0, h in zip(starts, hs):
        ck, suk, sdk = ck_ref[r0:r0 + rh, :], suk_ref[r0:r0 + rh, :], sdk_ref[r0:r0 + rh, :]
        pk = _dot(h, w_ref[:, base + nq:base + 2 * nq])
        for j in range(nq // LANES):
            k_ref[0, r0:r0 + rh, j * LANES:(j + 1) * LANES] = _rope_chunk(
                pk[:, j * LANES:(j + 1) * LANES], ck, suk, sdk, half).astype(BF16)
    for r0, h in zip(starts, hs):
        v_ref[0, r0:r0 + rh, :] = _dot(h, w_ref[:, base + 2 * nq:]).astype(BF16)


def _hyb_front(x, g, w_in, conv_w, tabs_q, tabs_k, half):
    b, s, d = x.shape
    tm = ROW_TILE
    n = w_in.shape[1]
    nq = DIFF_HEADS * 2 * DIFF_DK
    nv = DIFF_HEADS * DIFF_DV
    tab_spec = pl.BlockSpec((tm, LANES), lambda bi, si: (si, 0))
    row = lambda width: pl.BlockSpec((1, tm, width), lambda bi, si: (bi, si, 0))
    return pl.pallas_call(
        functools.partial(_hyb_front_kernel, tm=tm, half=half),
        grid=(b, s // tm),
        in_specs=[row(d),
                  pl.BlockSpec((1, d), lambda bi, si: (0, 0)),
                  pl.BlockSpec((d, n), lambda bi, si: (0, 0)),
                  pl.BlockSpec((CONV_K, CONV_CH), lambda bi, si: (0, 0)),
                  tab_spec, tab_spec, tab_spec, tab_spec, tab_spec, tab_spec],
        out_specs=[row(CONV_CH), row(nq), row(nq), row(nv)],
        out_shape=[jax.ShapeDtypeStruct((b, s, CONV_CH), BF16),
                   jax.ShapeDtypeStruct((b, s, nq), BF16),
                   jax.ShapeDtypeStruct((b, s, nq), BF16),
                   jax.ShapeDtypeStruct((b, s, nv), BF16)],
        scratch_shapes=[pltpu.VMEM((tm + 8, CONV_CH), F32)],
        compiler_params=_cparams(("arbitrary", "arbitrary")),
        name="hyb_front",
    )(x, g.reshape(1, d), w_in, conv_w, *tabs_q, *tabs_k)


def _lane_fold(x, op):
    r = x[:, 0:LANES]
    for c in range(1, x.shape[1] // LANES):
        r = op(r, x[:, c * LANES:(c + 1) * LANES])
    return r


def _diff_attn_kernel(q_ref, k_ref, v_ref, lam_ref, g_ref, o_ref, sbuf, stat, acc, *, tq, hp, lambda_init):
    i = pl.program_id(2)
    lane = lax.broadcasted_iota(I32, (1, LANES), 1)
    qs = []
    for h in range(hp):
        q = q_ref[0, :, h * LANES:(h + 1) * LANES]
        zero = jnp.zeros_like(q)
        qs += [jnp.where(lane < DIFF_DK, q, zero), jnp.where(lane >= DIFF_DK, q, zero)]
    nc = 2 * hp

    hq = tq // 2
    diag = pl.multiple_of(i * tq, tq)

    def pass1(j, carry):
        for c in range(nc):
            h = c // 2
            kb = k_ref[0, pl.ds(pl.multiple_of(j * tq, tq), tq), h * LANES:(h + 1) * LANES]
            sc = _dot_nt(qs[c], kb)
            sbuf[c, j] = sc
            stat[c] = jnp.maximum(stat[c], _lane_fold(sc, jnp.maximum))
        return carry

    stat[...] = jnp.full(stat.shape, NEG_INF, F32)
    lax.fori_loop(0, i, pass1, 0)

    r = lax.broadcasted_iota(I32, (hq, hq), 0)
    cc = lax.broadcasted_iota(I32, (hq, hq), 1)
    tri = cc <= r
    for c in range(nc):
        h = c // 2
        k_lo = k_ref[0, pl.ds(diag, hq), h * LANES:(h + 1) * LANES]
        k_hi = k_ref[0, pl.ds(diag + hq, hq), h * LANES:(h + 1) * LANES]
        s_tl = jnp.where(tri, _dot_nt(qs[c][:hq], k_lo), NEG_INF)
        s_bl = _dot_nt(qs[c][hq:], k_lo)
        s_br = jnp.where(tri, _dot_nt(qs[c][hq:], k_hi), NEG_INF)
        sbuf[c, i, 0:hq, 0:hq] = s_tl
        sbuf[c, i, hq:tq, 0:hq] = s_bl
        sbuf[c, i, hq:tq, hq:tq] = s_br
        stat[c, 0:hq, :] = jnp.maximum(stat[c, 0:hq, :], _lane_fold(s_tl, jnp.maximum))
        stat[c, hq:tq, :] = jnp.maximum(stat[c, hq:tq, :],
                                        jnp.maximum(_lane_fold(s_bl, jnp.maximum), _lane_fold(s_br, jnp.maximum)))
    ms = [jnp.max(stat[c], axis=-1, keepdims=True) for c in range(nc)]

    ones = jnp.ones((tq, LANES), BF16)
    for h in range(hp):
        vb = v_ref[0, pl.ds(diag, tq), h * LANES:(h + 1) * LANES]
        v_ext = jnp.concatenate([vb, ones], axis=1)
        tops, bots = [], []
        for c in (2 * h, 2 * h + 1):
            tops.append(jnp.exp2(sbuf[c, i, 0:hq, 0:hq] - ms[c][:hq]))
            bots.append(jnp.exp2(sbuf[c, i, hq:tq, :] - ms[c][hq:]))
        top = _dot(jnp.concatenate(tops, axis=0).astype(BF16), v_ext[:hq])
        bot = _dot(jnp.concatenate(bots, axis=0).astype(BF16), v_ext)
        acc[h, 0:hq, :] = top[:hq]
        acc[h, hq:tq, :] = bot[:hq]
        acc[h, tq:tq + hq, :] = top[hq:]
        acc[h, tq + hq:2 * tq, :] = bot[hq:]

    def pass2(j, carry):
        for h in range(hp):
            vb = v_ref[0, pl.ds(pl.multiple_of(j * tq, tq), tq), h * LANES:(h + 1) * LANES]
            v_ext = jnp.concatenate([vb, ones], axis=1)
            p0 = jnp.exp2(sbuf[2 * h, j] - ms[2 * h])
            p1 = jnp.exp2(sbuf[2 * h + 1, j] - ms[2 * h + 1])
            acc[h] += _dot(jnp.concatenate([p0, p1], axis=0).astype(BF16), v_ext)
        return carry

    lax.fori_loop(0, i, pass2, 0)

    lf = lam_ref[...]
    lam = (jnp.exp(jnp.sum(lf[0:1] * lf[1:2], keepdims=True))
           - jnp.exp(jnp.sum(lf[2:3] * lf[3:4], keepdims=True)) + lambda_init)
    for h in range(hp):
        a0 = acc[h, 0:tq, :]
        a1 = acc[h, tq:2 * tq, :]
        o = a0[:, :LANES] / a0[:, LANES:] - lam * (a1[:, :LANES] / a1[:, LANES:])
        o_ref[0, :, h * LANES:(h + 1) * LANES] = (_rms(o, g_ref[...]) * (1.0 - lambda_init)).astype(BF16)


def _diff_attn(q, k, v, lam_vecs, subln_g, lambda_init):
    b, s, _ = q.shape
    tq = 512
    hp = 4
    blk = lambda bi, hi, qi: (bi, qi, hi)
    full = lambda bi, hi, qi: (bi, 0, hi)
    return pl.pallas_call(
        functools.partial(_diff_attn_kernel, tq=tq, hp=hp, lambda_init=lambda_init),
        grid=(b, DIFF_HEADS // hp, s // tq),
        in_specs=[pl.BlockSpec((1, tq, hp * LANES), blk),
                  pl.BlockSpec((1, s, hp * LANES), full),
                  pl.BlockSpec((1, s, hp * LANES), full),
                  pl.BlockSpec((4, DIFF_DK), lambda bi, hi, qi: (0, 0)),
                  pl.BlockSpec((1, DIFF_DV), lambda bi, hi, qi: (0, 0))],
        out_specs=pl.BlockSpec((1, tq, hp * LANES), blk),
        out_shape=jax.ShapeDtypeStruct((b, s, DIFF_HEADS * DIFF_DV), BF16),
        scratch_shapes=[pltpu.VMEM((2 * hp, s // tq, tq, tq), F32),
                        pltpu.VMEM((2 * hp, tq, LANES), F32),
                        pltpu.VMEM((hp, 2 * tq, DIFF_DV + LANES), F32)],
        compiler_params=_cparams(("arbitrary", "arbitrary", "arbitrary")),
        name="diff_attn",
    )(q, k, v, lam_vecs, subln_g.reshape(1, DIFF_DV))


def _swa_front_kernel(x_ref, g_ref, w_ref, b_ref, cq_ref, suq_ref, sdq_ref, ck_ref, suk_ref, sdk_ref,
                      q_ref, kv_ref, *, half):
    nq = SWA_Q_HEADS * SWA_HEAD_DIM
    nkv = SWA_KV_HEADS * SWA_HEAD_DIM
    tm = x_ref.shape[1]
    rh = tm // FRONT_CHAINS
    starts = [ch * rh for ch in range(FRONT_CHAINS)]
    hs = [_rms(x_ref[0, r0:r0 + rh, :], g_ref[...]).astype(BF16) for r0 in starts]
    for r0, h in zip(starts, hs):
        cq, suq, sdq = cq_ref[r0:r0 + rh, :], suq_ref[r0:r0 + rh, :], sdq_ref[r0:r0 + rh, :]
        pq = _dot(h, w_ref[:, 0:nq]) + b_ref[:, 0:nq]
        for j in range(nq // LANES):
            lo = j * LANES
            q_ref[0, r0:r0 + rh, lo:lo + LANES] = _rope_chunk(pq[:, lo:lo + LANES], cq, suq, sdq, half).astype(BF16)
    for r0, h in zip(starts, hs):
        ck, suk, sdk = ck_ref[r0:r0 + rh, :], suk_ref[r0:r0 + rh, :], sdk_ref[r0:r0 + rh, :]
        pkv = _dot(h, w_ref[:, nq:]) + b_ref[:, nq:]
        for j in range(nkv // LANES):
            lo = j * LANES
            kv_ref[0, r0:r0 + rh, lo:lo + LANES] = _rope_chunk(pkv[:, lo:lo + LANES], ck, suk, sdk, half).astype(BF16)
        kv_ref[0, r0:r0 + rh, nkv:] = pkv[:, nkv:].astype(BF16)


def _swa_front(x, g, w_qkv, b_qkv, tabs_q, tabs_k, half):
    b, s, d = x.shape
    tm = ROW_TILE
    n = w_qkv.shape[1]
    nq = SWA_Q_HEADS * SWA_HEAD_DIM
    nkv = SWA_KV_HEADS * SWA_HEAD_DIM
    tab_spec = pl.BlockSpec((tm, LANES), lambda bi, si: (si, 0))
    row = lambda width: pl.BlockSpec((1, tm, width), lambda bi, si: (bi, si, 0))
    return pl.pallas_call(
        functools.partial(_swa_front_kernel, half=half),
        grid=(b, s // tm),
        in_specs=[row(d),
                  pl.BlockSpec((1, d), lambda bi, si: (0, 0)),
                  pl.BlockSpec((d, n), lambda bi, si: (0, 0)),
                  pl.BlockSpec((1, n), lambda bi, si: (0, 0)),
                  tab_spec, tab_spec, tab_spec, tab_spec, tab_spec, tab_spec],
        out_specs=[row(nq), row(2 * nkv)],
        out_shape=[jax.ShapeDtypeStruct((b, s, nq), BF16),
                   jax.ShapeDtypeStruct((b, s, 2 * nkv), BF16)],
        compiler_params=_cparams(("arbitrary", "arbitrary")),
        name="swa_front",
    )(x, g.reshape(1, d), w_qkv, b_qkv.reshape(1, n), *tabs_q, *tabs_k)


def _swa_head_order():
    g_sz = SWA_Q_HEADS // SWA_KV_HEADS
    order = []
    for slab in range(SWA_Q_HEADS // 2):
        pair, j = slab // g_sz, slab % g_sz
        order += [(2 * pair) * g_sz + j, (2 * pair + 1) * g_sz + j]
    return order


def _swa_attn_kernel(sink_ref, q_ref, kvp_ref, kvc_ref, o_ref, *, nblk):
    i = pl.program_id(1)
    hd = SWA_HEAD_DIM
    nkv = SWA_KV_HEADS * hd
    g_sz = SWA_Q_HEADS // SWA_KV_HEADS
    kv = jnp.concatenate([kvp_ref[0], kvc_ref[0]], axis=0)
    r = lax.broadcasted_iota(I32, (BLOCK, 2 * BLOCK), 0)
    c = lax.broadcasted_iota(I32, (BLOCK, 2 * BLOCK), 1)
    rel = c - BLOCK - r
    in_win = (rel <= 0) & (rel > -SWA_WINDOW)
    lane = lax.broadcasted_iota(I32, (1, LANES), 1)
    lo_half = lane < hd
    ones = jnp.ones((2 * BLOCK, LANES), BF16)
    for n in range(nblk):
        mask = in_win & ((c >= BLOCK) | (i > 0)) if n == 0 else in_win
        keys = kv[n * BLOCK:(n + 2) * BLOCK]
        for pair in range(SWA_KV_HEADS // 2):
            k2 = keys[:, pair * LANES:(pair + 1) * LANES]
            v2 = keys[:, nkv + pair * LANES:nkv + (pair + 1) * LANES]
            v_ext = jnp.concatenate([v2, ones], axis=1)
            pieces = []
            for j in range(g_sz):
                slab = pair * g_sz + j
                qs = q_ref[0, n * BLOCK:(n + 1) * BLOCK, slab * LANES:(slab + 1) * LANES]
                zero = jnp.zeros_like(qs)
                pieces += [jnp.where(lo_half, qs, zero), jnp.where(lo_half, zero, qs)]
            sc = _dot_nt(jnp.concatenate(pieces, axis=0), k2)
            probs, tails = [], []
            for pc in range(2 * g_sz):
                scp = jnp.where(mask, sc[pc * BLOCK:(pc + 1) * BLOCK], NEG_INF)
                sink = sink_ref[2 * g_sz * pair + pc] * LOG2E
                m = jnp.maximum(jnp.max(scp, axis=-1, keepdims=True), sink)
                probs.append(jnp.exp2(scp - m).astype(BF16))
                tails.append(jnp.exp2(sink - m))
            pv = _dot(jnp.concatenate(probs, axis=0), v_ext)
            for j in range(g_sz):
                slab = pair * g_sz + j
                halves = []
                for hf in range(2):
                    pc = 2 * j + hf
                    blk = pv[pc * BLOCK:(pc + 1) * BLOCK]
                    halves.append(blk[:, :LANES] / (blk[:, LANES:] + tails[pc]))
                o_ref[0, n * BLOCK:(n + 1) * BLOCK, slab * LANES:(slab + 1) * LANES] = (
                    jnp.where(lo_half, halves[0], halves[1]).astype(BF16))


def _swa_attn(q, kv, sinks):
    b, s, nq = q.shape
    nblk = 8
    tq = nblk * BLOCK
    return pl.pallas_call(
        functools.partial(_swa_attn_kernel, nblk=nblk),
        grid_spec=pltpu.PrefetchScalarGridSpec(
            num_scalar_prefetch=1,
            grid=(b, s // tq),
            in_specs=[pl.BlockSpec((1, tq, nq), lambda bi, ni, sk: (bi, ni, 0)),
                      pl.BlockSpec((1, BLOCK, kv.shape[2]), lambda bi, ni, sk: (bi, jnp.maximum(ni * nblk - 1, 0), 0)),
                      pl.BlockSpec((1, tq, kv.shape[2]), lambda bi, ni, sk: (bi, ni, 0))],
            out_specs=pl.BlockSpec((1, tq, nq), lambda bi, ni, sk: (bi, ni, 0))),
        out_shape=jax.ShapeDtypeStruct((b, s, nq), BF16),
        compiler_params=_cparams(("arbitrary", "arbitrary")),
        name="swa_attn",
    )(sinks, q, kv, kv)


def _post_kernel(*refs, tm, n_a, has_bias):
    x_ref = refs[0]
    a_refs = refs[1:1 + n_a]
    k = 1 + n_a
    wo_refs = refs[k:k + n_a]
    k += n_a
    if has_bias:
        bo_ref = refs[k]
        k += 1
    (gx_ref, wq_ref, mkv_ref, wxo_ref, gf_ref, wr_ref, br_ref,
     x2_ref, h2_ref, mi_ref, mw_ref, cnt_ref, cnt_acc, cnt_col) = refs[k:]
    first = (pl.program_id(0) == 0) & (pl.program_id(1) == 0)

    @pl.when(first)
    def _():
        cnt_acc[...] = jnp.zeros_like(cnt_acc)
        cnt_col[...] = jnp.zeros_like(cnt_col)

    xw = XATTN_HEADS * XATTN_HEAD_DIM
    ones = jnp.ones((mkv_ref.shape[1], LANES), BF16)

    rh = tm // POST_CHAINS

    def out_proj(r0):
        acc = _dot(a_refs[0][0, r0:r0 + rh, :], wo_refs[0][...])
        for a_ref, w_ref in zip(a_refs[1:], wo_refs[1:]):
            acc = acc + _dot(a_ref[0, r0:r0 + rh, :], w_ref[...])
        if has_bias:
            acc = acc + bo_ref[...]
        return x_ref[0, r0:r0 + rh, :] + acc

    def q_proj(x1):
        hx = _rms(x1, gx_ref[...]).astype(BF16)
        return (_dot(hx, wq_ref[...]) * (XATTN_HEAD_DIM ** -0.5 * LOG2E)).astype(BF16)

    def mem_attn(qx):
        outs = []
        for hh in range(XATTN_HEADS):
            lo = hh * XATTN_HEAD_DIM
            mk = mkv_ref[0, :, lo:lo + XATTN_HEAD_DIM]
            mv = jnp.concatenate([mkv_ref[0, :, xw + lo:xw + lo + XATTN_HEAD_DIM], ones], axis=1)
            sc = _dot_nt(qx[:, lo:lo + XATTN_HEAD_DIM], mk)
            m = jnp.max(sc, axis=-1, keepdims=True)
            pv = _dot(jnp.exp2(sc - m).astype(BF16), mv)
            outs.append((pv[:, :LANES] / pv[:, LANES:]).astype(BF16))
        return jnp.concatenate(outs, axis=-1)

    def o_proj(r0, x1, ox):
        x2 = x1 + _dot(ox, wxo_ref[...])
        x2_ref[0, r0:r0 + rh, :] = x2
        return x2

    def router(r0, x2):
        h2 = _rms(x2, gf_ref[...])
        _to_token_tiles(h2_ref, h2, rh, r0)
        return _dot(h2.astype(BF16), wr_ref[...]) + br_ref[...]

    starts = [c * rh for c in range(POST_CHAINS)]
    x1s = [out_proj(r0) for r0 in starts]
    qxs = [q_proj(x1) for x1 in x1s]
    oxs = [mem_attn(qx) for qx in qxs]
    x2s = [o_proj(r0, x1, ox) for r0, x1, ox in zip(starts, x1s, oxs)]
    logits = jnp.concatenate([router(r0, x2) for r0, x2 in zip(starts, x2s)], axis=0)

    lt = logits.T
    ex = lt[0:N_EXPERTS]
    grp = lt[N_EXPERTS:N_EXPERTS + SUBLANES]
    grow = lax.broadcasted_iota(I32, grp.shape, 0).astype(F32)
    lg = jnp.where(grow < N_GROUPS, grp, NEG_INF)
    mg = jnp.max(lg, axis=0, keepdims=True)
    g_sel = jnp.min(jnp.where(lg == mg, grow, float(SUBLANES)), axis=0, keepdims=True)
    p_g = 1.0 / jnp.sum(jnp.exp(lg - mg), axis=0, keepdims=True)
    erow = lax.broadcasted_iota(I32, ex.shape, 0).astype(F32)
    e_lo = g_sel * EXPERTS_PER_GROUP
    le = jnp.where((erow >= e_lo) & (erow < e_lo + EXPERTS_PER_GROUP), ex, NEG_INF)
    big = float(N_EXPERTS)
    m1 = jnp.max(le, axis=0, keepdims=True)
    i1 = jnp.min(jnp.where(le == m1, erow, big), axis=0, keepdims=True)
    le2 = jnp.where(erow == i1, NEG_INF, le)
    m2 = jnp.max(le2, axis=0, keepdims=True)
    i2 = jnp.min(jnp.where(le2 == m2, erow, big), axis=0, keepdims=True)
    t = jnp.exp(m2 - m1)
    w1 = p_g / (1.0 + t)
    w2 = p_g * t / (1.0 + t)

    oh1 = erow == i1
    oh2 = erow == i2
    oh = jnp.where(oh1 | oh2, 1.0, 0.0)
    rr = lax.broadcasted_iota(I32, (tm, tm), 0)
    cc = lax.broadcasted_iota(I32, (tm, tm), 1)
    earlier = jnp.where(rr < cc, 1.0, 0.0).astype(BF16)
    before = _dot(oh.astype(BF16), earlier) + cnt_col[...]
    r1 = jnp.sum(jnp.where(oh1, before, 0.0), axis=0, keepdims=True)
    r2 = jnp.sum(jnp.where(oh2, before, 0.0), axis=0, keepdims=True)
    cnt_col[...] = cnt_col[...] + jnp.sum(oh, axis=1, keepdims=True)
    oh_pad = jnp.concatenate([oh, jnp.zeros((LANES - N_EXPERTS, tm), F32)], axis=0).astype(BF16)
    cnt_new = cnt_acc[...] + _dot_nt(jnp.ones((SUBLANES, tm), BF16), oh_pad)[0:1]
    cnt_acc[...] = cnt_new
    cnt_ref[...] = cnt_new.astype(I32)

    zrow = jnp.zeros((1, tm), F32)
    mi_ref[...] = jnp.concatenate([i1, i2, r1, r2, zrow, zrow, zrow, zrow], axis=0).astype(I32)
    wt = jnp.concatenate([w1, w2, jnp.zeros((LANES - TOP_K, tm), F32)], axis=0)
    mw_ref[...] = wt.T


def _post_mixer(x, a_list, wo_list, b_out, g_x, w_q, mkv, w_xo, g_f, w_rt, b_rt):
    b, s, d = x.shape
    tm = ROW_TILE
    ns = s // tm
    t = b * s
    n_a = len(a_list)
    has_bias = b_out is not None
    const2 = lambda bi, si: (0, 0)
    row = lambda width: pl.BlockSpec((1, tm, width), lambda bi, si: (bi, si, 0))
    in_specs = [row(d)] + [row(a.shape[2]) for a in a_list]
    in_specs += [pl.BlockSpec(w.shape, const2) for w in wo_list]
    args = [x, *a_list, *wo_list]
    if has_bias:
        in_specs.append(pl.BlockSpec((1, d), const2))
        args.append(b_out.reshape(1, d))
    xw = w_q.shape[1]
    in_specs += [pl.BlockSpec((1, d), const2),
                 pl.BlockSpec((d, xw), const2),
                 pl.BlockSpec((1, mkv.shape[1], mkv.shape[2]), lambda bi, si: (bi, 0, 0)),
                 pl.BlockSpec((xw, d), const2),
                 pl.BlockSpec((1, d), const2),
                 pl.BlockSpec((d, LANES), const2),
                 pl.BlockSpec((1, LANES), const2)]
    args += [g_x.reshape(1, d), w_q, mkv, w_xo, g_f.reshape(1, d), w_rt, b_rt]
    out_specs = [row(d),
                 pl.BlockSpec((tm * TOKEN_ROWS, LANES), lambda bi, si: (bi * ns + si, 0)),
                 pl.BlockSpec((8, tm), lambda bi, si: (0, bi * ns + si)),
                 pl.BlockSpec((tm, LANES), lambda bi, si: (bi * ns + si, 0)),
                 pl.BlockSpec((1, LANES), const2)]
    out_shape = [jax.ShapeDtypeStruct((b, s, d), F32),
                 jax.ShapeDtypeStruct((t * TOKEN_ROWS, LANES), U32),
                 jax.ShapeDtypeStruct((8, t), I32),
                 jax.ShapeDtypeStruct((t, LANES), F32),
                 jax.ShapeDtypeStruct((1, LANES), I32)]
    return pl.pallas_call(
        functools.partial(_post_kernel, tm=tm, n_a=n_a, has_bias=has_bias),
        grid=(b, ns),
        in_specs=in_specs,
        out_specs=out_specs,
        out_shape=out_shape,
        scratch_shapes=[pltpu.VMEM((1, LANES), F32), pltpu.VMEM((N_EXPERTS, 1), F32)],
        compiler_params=_cparams(("arbitrary", "arbitrary")),
        name="post_mixer",
    )(*args)


def _to_token_tiles(dst_ref, val, rows, tok0=0):
    half = TOKEN_ROWS * LANES
    for j in range(TOKEN_ROWS):
        hi = val[:, j * LANES:(j + 1) * LANES].astype(BF16).astype(F32)
        lo = val[:, half + j * LANES:half + (j + 1) * LANES].astype(BF16).astype(F32)
        word = lax.bitcast_convert_type(hi, U32) | (lax.bitcast_convert_type(lo, U32) >> 16)
        dst_ref[pl.ds(tok0 * TOKEN_ROWS + j, rows, stride=TOKEN_ROWS), :] = word


def _from_token_tiles(src_ref, rows, tok0=0):
    his, los = [], []
    for j in range(TOKEN_ROWS):
        word = src_ref[pl.ds(tok0 * TOKEN_ROWS + j, rows, stride=TOKEN_ROWS), :]
        his.append(lax.bitcast_convert_type(word & jnp.uint32(0xFFFF0000), F32))
        los.append(lax.bitcast_convert_type(word << 16, F32))
    return jnp.concatenate(his + los, axis=1)


def _token_slice(ref, tok):
    return ref.at[pl.ds(pl.multiple_of(tok * TOKEN_ROWS, TOKEN_ROWS), TOKEN_ROWS)]


def _pos_kernel(off_ref, mi_ref, pos_ref):
    e = mi_ref[0:TOP_K, :]
    pos = mi_ref[TOP_K:2 * TOP_K, :]
    for j in range(N_EXPERTS):
        pos = pos + jnp.where(e == j, off_ref[j], 0)
    pos_ref[...] = pos


def _sorted_positions(off, meta_i):
    t = meta_i.shape[1]
    return pl.pallas_call(
        _pos_kernel,
        grid_spec=pltpu.PrefetchScalarGridSpec(
            num_scalar_prefetch=1,
            grid=(1,),
            in_specs=[pl.BlockSpec((8, t), lambda i, off: (0, 0))],
            out_specs=pl.BlockSpec((TOP_K, t), lambda i, off: (0, 0))),
        out_shape=jax.ShapeDtypeStruct((TOP_K, t), I32),
        compiler_params=_cparams(("arbitrary",)),
        name="moe_positions",
    )(off, meta_i)


def _tile_major(pos, tm):
    t = pos.shape[1]
    return pos.reshape(TOP_K, t // tm, tm).transpose(1, 0, 2).reshape(-1)


def _dispatch_kernel(pos_ref, h_ref, xs_ref, sem, *, tm):
    def issue(g, _):
        for u in range(DMA_UNROLL):
            r = g * DMA_UNROLL + u
            for kk in range(TOP_K):
                pos = pos_ref[kk * tm + r]
                pltpu.make_async_copy(_token_slice(h_ref, r), _token_slice(xs_ref, pos), sem).start(priority=kk)
        return 0

    lax.fori_loop(0, tm // DMA_UNROLL, issue, 0)
    for _ in range(TOP_K):
        pltpu.make_async_copy(h_ref, xs_ref.at[pl.ds(0, tm * TOKEN_ROWS)], sem).wait()


def _dispatch(pos, h2t, n_rows):
    tm = DISPATCH_TILE
    t = h2t.shape[0] // TOKEN_ROWS
    return pl.pallas_call(
        functools.partial(_dispatch_kernel, tm=tm),
        grid=(t // tm,),
        in_specs=[pl.BlockSpec((TOP_K * tm,), lambda i: (i,), memory_space=pltpu.SMEM),
                  pl.BlockSpec((tm * TOKEN_ROWS, LANES), lambda i: (i, 0))],
        out_specs=pl.BlockSpec(memory_space=pl.ANY),
        scratch_shapes=[pltpu.SemaphoreType.DMA],
        out_shape=jax.ShapeDtypeStruct((n_rows * TOKEN_ROWS, LANES), U32),
        compiler_params=_cparams(("arbitrary",)),
        name="moe_dispatch",
    )(_tile_major(pos, tm), h2t)


def _expert_kernel(tblk_ref, texp_ref, tn_ref, tnxt_ref, tpar_ref, xs_ref, wgu_hbm, wdn_hbm, ys_ref,
                   wgu_bf, wdn_bf, wgu_f, wdn_f, sems, *, tm, layer):
    i = pl.program_id(0)
    n_valid = tn_ref[i]
    expert = texp_ref[i]
    slot = tpar_ref[i]
    new_expert = (i == 0) | (expert != texp_ref[jnp.maximum(i - 1, 0)])

    def weight_copies(ex, sl):
        return (pltpu.make_async_copy(wgu_hbm.at[layer, ex], wgu_f.at[sl], sems.at[sl]),
                pltpu.make_async_copy(wdn_hbm.at[layer, ex], wdn_f.at[sl], sems.at[sl]))

    @pl.when(i == 0)
    def _():
        for cp in weight_copies(expert, slot):
            cp.start()

    @pl.when(new_expert)
    def _():
        @pl.when(tnxt_ref[i] >= 0)
        def _():
            for cp in weight_copies(tnxt_ref[i], 1 - slot):
                cp.start()

        for cp in weight_copies(expert, slot):
            cp.wait()
        wgu_bf[...] = wgu_f[slot].astype(BF16)
        wdn_bf[...] = wdn_f[slot].astype(BF16)

    @pl.when(n_valid > 0)
    def _():
        rh = tm // EXPERT_CHAINS
        starts = [c * rh for c in range(EXPERT_CHAINS)]
        row = lax.broadcasted_iota(I32, (rh, 1), 0)
        gus = []
        for r0 in starts:
            x = jnp.where(row + r0 < n_valid, _from_token_tiles(xs_ref, rh, r0), 0.0).astype(BF16)
            gus.append(_dot(x, wgu_bf[...]))
        ys = []
        for gu in gus:
            hid = (gu[:, :EXPERT_HIDDEN] / (1.0 + jnp.exp(-gu[:, :EXPERT_HIDDEN])) * gu[:, EXPERT_HIDDEN:]).astype(BF16)
            ys.append(_dot(hid, wdn_bf[...]))
        for r0, y in zip(starts, ys):
            _to_token_tiles(ys_ref, y, rh, r0)


def _experts(plan, xs, w_gu, w_dn, layer):
    tm = EXPERT_TILE
    nt = xs.shape[0] // (tm * TOKEN_ROWS)
    gu_shape, dn_shape = w_gu.shape[2:], w_dn.shape[2:]
    rows = lambda i, tb, te, tn, tx, tp: (tb[i], 0)
    return pl.pallas_call(
        functools.partial(_expert_kernel, tm=tm, layer=layer),
        grid_spec=pltpu.PrefetchScalarGridSpec(
            num_scalar_prefetch=5,
            grid=(nt,),
            in_specs=[pl.BlockSpec((tm * TOKEN_ROWS, LANES), rows),
                      pl.BlockSpec(memory_space=pl.ANY),
                      pl.BlockSpec(memory_space=pl.ANY)],
            out_specs=pl.BlockSpec((tm * TOKEN_ROWS, LANES), rows),
            scratch_shapes=[pltpu.VMEM(gu_shape, BF16), pltpu.VMEM(dn_shape, BF16),
                            pltpu.VMEM((2,) + gu_shape, F32), pltpu.VMEM((2,) + dn_shape, F32),
                            pltpu.SemaphoreType.DMA((2,))]),
        out_shape=jax.ShapeDtypeStruct(xs.shape, U32),
        compiler_params=_cparams(("arbitrary",)),
        name="moe_experts",
    )(*plan, xs, w_gu, w_dn)


def _moe_combine_tile(pos_cur, pos_nxt, mw_ref, x, ys_ref, ybuf, sems, i, n, tm):
    def gather(pos_ref, slot):
        def issue(g, _):
            for u in range(DMA_UNROLL):
                r = g * DMA_UNROLL + u
                for kk in range(TOP_K):
                    pos = pos_ref[kk * tm + r]
                    pltpu.make_async_copy(_token_slice(ys_ref, pos), _token_slice(ybuf.at[slot, kk], r),
                                          sems.at[slot]).start(priority=kk)
            return 0

        lax.fori_loop(0, tm // DMA_UNROLL, issue, 0)

    @pl.when(i == 0)
    def _():
        gather(pos_cur, 0)

    @pl.when(i + 1 < n)
    def _():
        gather(pos_nxt, (i + 1) % 2)

    slot = i % 2
    for kk in range(TOP_K):
        pltpu.make_async_copy(ys_ref.at[pl.ds(0, tm * TOKEN_ROWS)], ybuf.at[slot, kk], sems.at[slot]).wait()
    mw = mw_ref[...]
    return (x + mw[:, 0:1] * _from_token_tiles(ybuf.at[slot, 0], tm)
            + mw[:, 1:2] * _from_token_tiles(ybuf.at[slot, 1], tm))


def _combine_kernel(pos_cur, pos_nxt, mw_ref, x_ref, ys_ref, *rest, tm, final):
    if final:
        g_ref, o_ref, ybuf, sems = rest
    else:
        o_ref, ybuf, sems = rest
    x3 = _moe_combine_tile(pos_cur, pos_nxt, mw_ref, x_ref[...], ys_ref, ybuf, sems,
                           pl.program_id(0), pl.num_programs(0), tm)
    if final:
        x3 = _rms(x3, g_ref[...])
    o_ref[...] = x3


def _combine(moe, g_final):
    pos, meta_w, x2, ys = moe
    d = x2.shape[-1]
    x2d = x2.reshape(-1, d)
    t = x2d.shape[0]
    tm = COMBINE_TILE
    nt = t // tm
    final = g_final is not None
    in_specs = [pl.BlockSpec((TOP_K * tm,), lambda i: (i,), memory_space=pltpu.SMEM),
                pl.BlockSpec((TOP_K * tm,), lambda i: (jnp.minimum(i + 1, nt - 1),), memory_space=pltpu.SMEM),
                pl.BlockSpec((tm, LANES), lambda i: (i, 0)),
                pl.BlockSpec((tm, d), lambda i: (i, 0)),
                pl.BlockSpec(memory_space=pl.ANY)]
    pos_flat = _tile_major(pos, tm)
    args = [pos_flat, pos_flat, meta_w, x2d, ys]
    if final:
        in_specs.append(pl.BlockSpec((1, d), lambda i: (0, 0)))
        args.append(g_final.reshape(1, d))
    return pl.pallas_call(
        functools.partial(_combine_kernel, tm=tm, final=final),
        grid=(nt,),
        in_specs=in_specs,
        out_specs=pl.BlockSpec((tm, d), lambda i: (i, 0)),
        scratch_shapes=[pltpu.VMEM((2, TOP_K, tm * TOKEN_ROWS, LANES), U32), pltpu.SemaphoreType.DMA((2,))],
        out_shape=jax.ShapeDtypeStruct((t, d), F32),
        compiler_params=_cparams(("arbitrary",)),
        name="moe_combine",
    )(*args)


def _plan_kernel(cnt_ref, off_ref, blk_ref, exp_ref, nv_ref, nxt_ref, par_ref, *, tm, n_tiles):
    shift = tm.bit_length() - 1

    def clear(i, _):
        nxt_ref[i] = -1
        return 0

    lax.fori_loop(0, n_tiles, clear, 0)

    def per_expert(e, carry):
        row0, run, prev_first = carry
        c = cnt_ref[0, e]
        ntile = lax.shift_right_logical(c + (tm - 1), shift)
        off_ref[e] = row0
        t0 = lax.shift_right_logical(row0, shift)

        def fill(j, _):
            blk_ref[t0 + j] = t0 + j
            exp_ref[t0 + j] = e
            nv_ref[t0 + j] = jnp.minimum(c - j * tm, tm)
            par_ref[t0 + j] = run & 1
            return 0

        lax.fori_loop(0, ntile, fill, 0)
        has = ntile > 0

        @pl.when(has & (prev_first >= 0))
        def _():
            nxt_ref[prev_first] = e

        return (row0 + lax.shift_left(ntile, shift), run + has.astype(I32), jnp.where(has, t0, prev_first))

    total, _, _ = lax.fori_loop(0, N_EXPERTS, per_expert, (jnp.int32(0), jnp.int32(0), jnp.int32(-1)))
    used = lax.shift_right_logical(total, shift)
    last = jnp.maximum(used - 1, 0)

    def tail(i, _):
        blk_ref[i] = last
        exp_ref[i] = exp_ref[last]
        nv_ref[i] = 0
        par_ref[i] = par_ref[last]
        return 0

    lax.fori_loop(used, n_tiles, tail, 0)


def _moe_plan(counts, n_tiles):
    tm = EXPERT_TILE
    assert tm & (tm - 1) == 0
    smem = pl.BlockSpec(memory_space=pltpu.SMEM)
    return pl.pallas_call(
        functools.partial(_plan_kernel, tm=tm, n_tiles=n_tiles),
        in_specs=[smem],
        out_specs=[smem] * 6,
        out_shape=[jax.ShapeDtypeStruct((N_EXPERTS,), I32)] + [jax.ShapeDtypeStruct((n_tiles,), I32)] * 5,
        name="moe_plan",
    )(counts)


def _moe(x2, h2t, meta_i, meta_w, counts, w_gu, w_dn, layer):
    b, s, d = x2.shape
    assert d == 2 * TOKEN_ROWS * LANES
    n_rows = b * s * TOP_K + N_EXPERTS * EXPERT_TILE
    off, *tile_plan = _moe_plan(counts, n_rows // EXPERT_TILE)
    pos = _sorted_positions(off, meta_i)
    xs = _dispatch(pos, h2t, n_rows)
    ys = _experts(tile_plan, xs, w_gu, w_dn, layer)
    return pos, meta_w, x2, ys


def _router_weights(w_group, b_group, w_router, b_router):
    d = w_group.shape[0]
    pad = LANES - N_EXPERTS - N_GROUPS
    w = jnp.concatenate([w_router, w_group, jnp.zeros((d, pad), F32)], axis=1)
    bias = jnp.concatenate([b_router, b_group, jnp.zeros((pad,), F32)]).reshape(1, LANES)
    return w.astype(BF16), bias


def kernel(x, mem, mem_norm, mem_w_kv, norm_mix, norm_xattn, norm_ffn, hyb_w_in, hyb_conv_w, diff_lambda, diff_subln, hyb_w_out, swa_w_qkv, swa_b_qkv, swa_sinks, swa_w_out, swa_b_out, xattn_w_q, xattn_w_o, moe_w_group, moe_b_group, moe_w_router, moe_b_router, moe_w_gate_up, moe_w_down, final_norm):
    b, s, d = x.shape
    m = mem.shape[1]
    depth = norm_mix.shape[0]
    assert d == 2 * TOKEN_ROWS * LANES and s % ROW_TILE == 0 and s % (4 * BLOCK) == 0
    assert (b * s) % DISPATCH_TILE == 0 and (b * s) % COMBINE_TILE == 0 and DISPATCH_TILE % DMA_UNROLL == 0
    mem_tile = math.gcd(b * m, ROW_TILE)
    mkv = _norm_proj(mem.reshape(b * m, d), mem_norm, mem_w_kv.astype(BF16), mem_tile).reshape(b, m, -1)

    scale = DIFF_DK ** -0.5 * LOG2E
    cq, suq, sdq, half = _rope_lane_tables(s, DIFF_DK, scale)
    ck, suk, sdk, _ = _rope_lane_tables(s, DIFF_DK, 1.0)
    tabs_q, tabs_k = (cq, suq, sdq), (ck, suk, sdk)

    moe = None
    for l in range(depth):
        if moe is not None:
            x = _combine(moe, None).reshape(b, s, d)
        if l % 2 == 0:
            e = l // 2
            lambda_init = 0.8 - 0.6 * math.exp(-0.3 * l)
            ya, q, k, v = _hyb_front(x, norm_mix[l], hyb_w_in[e].astype(BF16), hyb_conv_w[e], tabs_q, tabs_k, half)
            o = _diff_attn(q, k, v, diff_lambda[e], diff_subln[e], lambda_init)
            w_out = hyb_w_out[e].astype(BF16)
            a_list, wo_list, b_out = [ya, o], [w_out[:CONV_CH], w_out[CONV_CH:]], None
        else:
            e = l // 2
            order = _swa_head_order()
            hd, nq = SWA_HEAD_DIM, SWA_Q_HEADS * SWA_HEAD_DIM
            heads = lambda a, axis: [lax.slice_in_dim(a, h * hd, (h + 1) * hd, axis=axis) for h in order]
            w_all = swa_w_qkv[e].astype(BF16)
            w_qkv = jnp.concatenate(heads(w_all, 1) + [w_all[:, nq:]], axis=1)
            b_qkv = jnp.concatenate(heads(swa_b_qkv[e], 0) + [swa_b_qkv[e][nq:]])
            sinks = jnp.stack([swa_sinks[e][h] for h in order])
            q, kv = _swa_front(x, norm_mix[l], w_qkv, b_qkv, tabs_q, tabs_k, half)
            o = _swa_attn(q, kv, sinks)
            w_out = jnp.concatenate(heads(swa_w_out[e].astype(BF16), 0), axis=0)
            a_list, wo_list, b_out = [o], [w_out], swa_b_out[e]
        w_rt, b_rt = _router_weights(moe_w_group[l], moe_b_group[l], moe_w_router[l], moe_b_router[l])
        x2, h2, meta_i, meta_w, counts = _post_mixer(
            x, a_list, wo_list, b_out, norm_xattn[l], xattn_w_q[l].astype(BF16), mkv,
            xattn_w_o[l].astype(BF16), norm_ffn[l], w_rt, b_rt)
        moe = _moe(x2, h2, meta_i, meta_w, counts, moe_w_gate_up, moe_w_down, l)
    return _combine(moe, final_norm).reshape(b, s, d)
```

```python
import functools
import math

import jax
import jax.numpy as jnp
from jax import lax
from jax.experimental import pallas as pl
from jax.experimental.pallas import tpu as pltpu

F32 = jnp.float32
BF16 = jnp.bfloat16
I32 = jnp.int32
U32 = jnp.uint32

EPS = 1e-6
LANES = 128
SUBLANES = 8
TOKEN_ROWS = 4
DMA_UNROLL = 16
VMEM_LIMIT = 56 * 1024 * 1024

ROPE_THETA = 500000.0
ROPE_FRACTION = 4
BLOCK = 128
CONV_CH = 512
CONV_K = 3
DIFF_HEADS = 4
DIFF_DK = 64
DIFF_DV = 128
SWA_Q_HEADS = 16
SWA_KV_HEADS = 4
SWA_HEAD_DIM = 64
SWA_WINDOW = 128
XATTN_HEADS = 4
XATTN_HEAD_DIM = 128
N_GROUPS = 4
EXPERTS_PER_GROUP = 8
N_EXPERTS = N_GROUPS * EXPERTS_PER_GROUP
TOP_K = 2
EXPERT_HIDDEN = 512

ROW_TILE = 1024
EXPERT_TILE = 512
DISPATCH_TILE = 4096
COMBINE_TILE = 512
SWA_BLOCKS_PER_STEP = 8
POST_CHAINS = 2
EXPERT_CHAINS = 2
FRONT_CHAINS = 2
NEG_INF = float("-inf")
LOG2E = math.log2(math.e)


def _cparams(sem):
    return pltpu.CompilerParams(dimension_semantics=sem, vmem_limit_bytes=VMEM_LIMIT)


def _rms(x, g):
    return x * lax.rsqrt(jnp.mean(x * x, axis=-1, keepdims=True) + EPS) * g


def _dot(a, b):
    return jnp.dot(a, b, preferred_element_type=F32)


def _dot_nt(a, b):
    return lax.dot_general(a, b, (((1,), (1,)), ((), ())), preferred_element_type=F32)


def _rope_lane_tables(seq, head_dim, scale):
    rot = head_dim // ROPE_FRACTION
    half = rot // 2
    pos = jnp.arange(seq, dtype=F32)
    inv = ROPE_THETA ** (-jnp.arange(0, rot, 2, dtype=F32) / rot)
    ang = pos[:, None] * inv[None, :]
    cos, sin = jnp.cos(ang), jnp.sin(ang)
    idx = jnp.arange(LANES) % head_dim
    cl = jnp.take(cos, idx % half, axis=1)
    sl = jnp.take(sin, idx % half, axis=1)
    c = jnp.where(idx < rot, cl, 1.0) * scale
    s_up = jnp.where(idx < half, -sl, 0.0) * scale
    s_dn = jnp.where((idx >= half) & (idx < rot), sl, 0.0) * scale
    return c.astype(F32), s_up.astype(F32), s_dn.astype(F32), half


def _rope_chunk(xc, c, s_up, s_dn, half):
    return (xc * c + pltpu.roll(xc, LANES - half, 1) * s_up + pltpu.roll(xc, half, 1) * s_dn)


def _norm_proj_kernel(x_ref, g_ref, w_ref, o_ref):
    h = _rms(x_ref[...], g_ref[...]).astype(BF16)
    o_ref[...] = _dot(h, w_ref[...]).astype(o_ref.dtype)


def _norm_proj(x2d, g, w_bf16, tm):
    m, d = x2d.shape
    n = w_bf16.shape[1]
    return pl.pallas_call(
        _norm_proj_kernel,
        grid=(m // tm,),
        in_specs=[pl.BlockSpec((tm, d), lambda i: (i, 0)),
                  pl.BlockSpec((1, d), lambda i: (0, 0)),
                  pl.BlockSpec((d, n), lambda i: (0, 0))],
        out_specs=pl.BlockSpec((tm, n), lambda i: (i, 0)),
        out_shape=jax.ShapeDtypeStruct((m, n), BF16),
        compiler_params=_cparams(("arbitrary",)),
        name="mem_kv_proj",
    )(x2d, g.reshape(1, d), w_bf16)


def _hyb_front_kernel(x_ref, g_ref, w_ref, cw_ref, cq_ref, suq_ref, sdq_ref, ck_ref, suk_ref, sdk_ref,
                      ya_ref, q_ref, k_ref, v_ref, cbuf, *, tm, half):
    s = pl.program_id(1)
    c = CONV_CH
    base = 3 * c
    nq = DIFF_HEADS * 2 * DIFF_DK
    rh = tm // FRONT_CHAINS
    starts = [ch * rh for ch in range(FRONT_CHAINS)]
    cw = cw_ref[...]

    @pl.when(s == 0)
    def _():
        cbuf[0:8, :] = jnp.zeros((8, c), F32)

    hs = [_rms(x_ref[0, r0:r0 + rh, :], g_ref[...]).astype(BF16) for r0 in starts]
    for r0, h in zip(starts, hs):
        gate_b = _dot(h, w_ref[:, 0:c])
        cu = _dot(h, w_ref[:, c:2 * c]) * _dot(h, w_ref[:, 2 * c:3 * c])
        cbuf[8 + r0:8 + r0 + rh, :] = cu
        conv = (cw[0:1, :] * cbuf[6 + r0:6 + r0 + rh, :] + cw[1:2, :] * cbuf[7 + r0:7 + r0 + rh, :] + cw[2:3, :] * cu)
        ya_ref[0, r0:r0 + rh, :] = (gate_b * conv).astype(BF16)
    cbuf[0:8, :] = cbuf[tm:tm + 8, :]

    for r0, h in zip(starts, hs):
        cq, suq, sdq = cq_ref[r0:r0 + rh, :], suq_ref[r0:r0 + rh, :], sdq_ref[r0:r0 + rh, :]
        pq = _dot(h, w_ref[:, base:base + nq])
        for j in range(nq // LANES):
            q_ref[0, r0:r0 + rh, j * LANES:(j + 1) * LANES] = _rope_chunk(
                pq[:, j * LANES:(j + 1) * LANES], cq, suq, sdq, half).astype(BF16)
    for r0, h in zip(starts, hs):
        ck, suk, sdk = ck_ref[r0:r0 + rh, :], suk_ref[r0:r0 + rh, :], sdk_ref[r0:r0 + rh, :]
        pk = _dot(h, w_ref[:, base + nq:base + 2 * nq])
        for j in range(nq // LANES):
            k_ref[0, r0:r0 + rh, j * LANES:(j + 1) * LANES] = _rope_chunk(
                pk[:, j * LANES:(j + 1) * LANES], ck, suk, sdk, half).astype(BF16)
    for r0, h in zip(starts, hs):
        v_ref[0, r0:r0 + rh, :] = _dot(h, w_ref[:, base + 2 * nq:]).astype(BF16)


def _hyb_front(x, g, w_in, conv_w, tabs_q, tabs_k, half):
    b, s, d = x.shape
    tm = ROW_TILE
    n = w_in.shape[1]
    nq = DIFF_HEADS * 2 * DIFF_DK
    nv = DIFF_HEADS * DIFF_DV
    tab_spec = pl.BlockSpec((tm, LANES), lambda bi, si: (si, 0))
    row = lambda width: pl.BlockSpec((1, tm, width), lambda bi, si: (bi, si, 0))
    return pl.pallas_call(
        functools.partial(_hyb_front_kernel, tm=tm, half=half),
        grid=(b, s // tm),
        in_specs=[row(d),
                  pl.BlockSpec((1, d), lambda bi, si: (0, 0)),
                  pl.BlockSpec((d, n), lambda bi, si: (0, 0)),
                  pl.BlockSpec((CONV_K, CONV_CH), lambda bi, si: (0, 0)),
                  tab_spec, tab_spec, tab_spec, tab_spec, tab_spec, tab_spec],
        out_specs=[row(CONV_CH), row(nq), row(nq), row(nv)],
        out_shape=[jax.ShapeDtypeStruct((b, s, CONV_CH), BF16),
                   jax.ShapeDtypeStruct((b, s, nq), BF16),
                   jax.ShapeDtypeStruct((b, s, nq), BF16),
                   jax.ShapeDtypeStruct((b, s, nv), BF16)],
        scratch_shapes=[pltpu.VMEM((tm + 8, CONV_CH), F32)],
        compiler_params=_cparams(("arbitrary", "arbitrary")),
        name="hyb_front",
    )(x, g.reshape(1, d), w_in, conv_w, *tabs_q, *tabs_k)


def _lane_fold(x, op):
    r = x[:, 0:LANES]
    for c in range(1, x.shape[1] // LANES):
        r = op(r, x[:, c * LANES:(c + 1) * LANES])
    return r


def _diff_attn_kernel(q_ref, k_ref, v_ref, lam_ref, g_ref, o_ref, sbuf, stat, acc, *, tq, hp, lambda_init):
    i = pl.program_id(2)
    lane = lax.broadcasted_iota(I32, (1, LANES), 1)
    qs = []
    for h in range(hp):
        q = q_ref[0, :, h * LANES:(h + 1) * LANES]
        zero = jnp.zeros_like(q)
        qs += [jnp.where(lane < DIFF_DK, q, zero), jnp.where(lane >= DIFF_DK, q, zero)]
    nc = 2 * hp

    hq = tq // 2
    diag = pl.multiple_of(i * tq, tq)

    def pass1(j, carry):
        for c in range(nc):
            h = c // 2
            kb = k_ref[0, pl.ds(pl.multiple_of(j * tq, tq), tq), h * LANES:(h + 1) * LANES]
            sc = _dot_nt(qs[c], kb)
            sbuf[c, j] = sc
            stat[c] = jnp.maximum(stat[c], _lane_fold(sc, jnp.maximum))
        return carry

    stat[...] = jnp.full(stat.shape, NEG_INF, F32)
    lax.fori_loop(0, i, pass1, 0)

    r = lax.broadcasted_iota(I32, (hq, hq), 0)
    cc = lax.broadcasted_iota(I32, (hq, hq), 1)
    tri = cc <= r
    for c in range(nc):
        h = c // 2
        k_lo = k_ref[0, pl.ds(diag, hq), h * LANES:(h + 1) * LANES]
        k_hi = k_ref[0, pl.ds(diag + hq, hq), h * LANES:(h + 1) * LANES]
        s_tl = jnp.where(tri, _dot_nt(qs[c][:hq], k_lo), NEG_INF)
        s_bl = _dot_nt(qs[c][hq:], k_lo)
        s_br = jnp.where(tri, _dot_nt(qs[c][hq:], k_hi), NEG_INF)
        sbuf[c, i, 0:hq, 0:hq] = s_tl
        sbuf[c, i, hq:tq, 0:hq] = s_bl
        sbuf[c, i, hq:tq, hq:tq] = s_br
        stat[c, 0:hq, :] = jnp.maximum(stat[c, 0:hq, :], _lane_fold(s_tl, jnp.maximum))
        stat[c, hq:tq, :] = jnp.maximum(stat[c, hq:tq, :],
                                        jnp.maximum(_lane_fold(s_bl, jnp.maximum), _lane_fold(s_br, jnp.maximum)))
    ms = [jnp.max(stat[c], axis=-1, keepdims=True) for c in range(nc)]

    ones = jnp.ones((tq, LANES), BF16)
    for h in range(hp):
        vb = v_ref[0, pl.ds(diag, tq), h * LANES:(h + 1) * LANES]
        v_ext = jnp.concatenate([vb, ones], axis=1)
        tops, bots = [], []
        for c in (2 * h, 2 * h + 1):
            tops.append(jnp.exp2(sbuf[c, i, 0:hq, 0:hq] - ms[c][:hq]))
            bots.append(jnp.exp2(sbuf[c, i, hq:tq, :] - ms[c][hq:]))
        top = _dot(jnp.concatenate(tops, axis=0).astype(BF16), v_ext[:hq])
        bot = _dot(jnp.concatenate(bots, axis=0).astype(BF16), v_ext)
        acc[h, 0:hq, :] = top[:hq]
        acc[h, hq:tq, :] = bot[:hq]
        acc[h, tq:tq + hq, :] = top[hq:]
        acc[h, tq + hq:2 * tq, :] = bot[hq:]

    def pass2(j, carry):
        for h in range(hp):
            vb = v_ref[0, pl.ds(pl.multiple_of(j * tq, tq), tq), h * LANES:(h + 1) * LANES]
            v_ext = jnp.concatenate([vb, ones], axis=1)
            p0 = jnp.exp2(sbuf[2 * h, j] - ms[2 * h])
            p1 = jnp.exp2(sbuf[2 * h + 1, j] - ms[2 * h + 1])
            acc[h] += _dot(jnp.concatenate([p0, p1], axis=0).astype(BF16), v_ext)
        return carry

    lax.fori_loop(0, i, pass2, 0)

    lf = lam_ref[...]
    lam = (jnp.exp(jnp.sum(lf[0:1] * lf[1:2], keepdims=True))
           - jnp.exp(jnp.sum(lf[2:3] * lf[3:4], keepdims=True)) + lambda_init)
    for h in range(hp):
        a0 = acc[h, 0:tq, :]
        a1 = acc[h, tq:2 * tq, :]
        o = a0[:, :LANES] / a0[:, LANES:] - lam * (a1[:, :LANES] / a1[:, LANES:])
        o_ref[0, :, h * LANES:(h + 1) * LANES] = (_rms(o, g_ref[...]) * (1.0 - lambda_init)).astype(BF16)


def _diff_attn(q, k, v, lam_vecs, subln_g, lambda_init):
    b, s, _ = q.shape
    tq = 512
    hp = 4
    blk = lambda bi, hi, qi: (bi, qi, hi)
    full = lambda bi, hi, qi: (bi, 0, hi)
    return pl.pallas_call(
        functools.partial(_diff_attn_kernel, tq=tq, hp=hp, lambda_init=lambda_init),
        grid=(b, DIFF_HEADS // hp, s // tq),
        in_specs=[pl.BlockSpec((1, tq, hp * LANES), blk),
                  pl.BlockSpec((1, s, hp * LANES), full),
                  pl.BlockSpec((1, s, hp * LANES), full),
                  pl.BlockSpec((4, DIFF_DK), lambda bi, hi, qi: (0, 0)),
                  pl.BlockSpec((1, DIFF_DV), lambda bi, hi, qi: (0, 0))],
        out_specs=pl.BlockSpec((1, tq, hp * LANES), blk),
        out_shape=jax.ShapeDtypeStruct((b, s, DIFF_HEADS * DIFF_DV), BF16),
        scratch_shapes=[pltpu.VMEM((2 * hp, s // tq, tq, tq), F32),
                        pltpu.VMEM((2 * hp, tq, LANES), F32),
                        pltpu.VMEM((hp, 2 * tq, DIFF_DV + LANES), F32)],
        compiler_params=_cparams(("arbitrary", "arbitrary", "arbitrary")),
        name="diff_attn",
    )(q, k, v, lam_vecs, subln_g.reshape(1, DIFF_DV))


def _swa_front_kernel(x_ref, g_ref, w_ref, b_ref, cq_ref, suq_ref, sdq_ref, ck_ref, suk_ref, sdk_ref,
                      q_ref, kv_ref, *, half):
    nq = SWA_Q_HEADS * SWA_HEAD_DIM
    nkv = SWA_KV_HEADS * SWA_HEAD_DIM
    tm = x_ref.shape[1]
    rh = tm // FRONT_CHAINS
    starts = [ch * rh for ch in range(FRONT_CHAINS)]
    hs = [_rms(x_ref[0, r0:r0 + rh, :], g_ref[...]).astype(BF16) for r0 in starts]
    for r0, h in zip(starts, hs):
        cq, suq, sdq = cq_ref[r0:r0 + rh, :], suq_ref[r0:r0 + rh, :], sdq_ref[r0:r0 + rh, :]
        pq = _dot(h, w_ref[:, 0:nq]) + b_ref[:, 0:nq]
        for j in range(nq // LANES):
            lo = j * LANES
            q_ref[0, r0:r0 + rh, lo:lo + LANES] = _rope_chunk(pq[:, lo:lo + LANES], cq, suq, sdq, half).astype(BF16)
    for r0, h in zip(starts, hs):
        ck, suk, sdk = ck_ref[r0:r0 + rh, :], suk_ref[r0:r0 + rh, :], sdk_ref[r0:r0 + rh, :]
        pkv = _dot(h, w_ref[:, nq:]) + b_ref[:, nq:]
        for j in range(nkv // LANES):
            lo = j * LANES
            kv_ref[0, r0:r0 + rh, lo:lo + LANES] = _rope_chunk(pkv[:, lo:lo + LANES], ck, suk, sdk, half).astype(BF16)
        kv_ref[0, r0:r0 + rh, nkv:] = pkv[:, nkv:].astype(BF16)


def _swa_front(x, g, w_qkv, b_qkv, tabs_q, tabs_k, half):
    b, s, d = x.shape
    tm = ROW_TILE
    n = w_qkv.shape[1]
    nq = SWA_Q_HEADS * SWA_HEAD_DIM
    nkv = SWA_KV_HEADS * SWA_HEAD_DIM
    tab_spec = pl.BlockSpec((tm, LANES), lambda bi, si: (si, 0))
    row = lambda width: pl.BlockSpec((1, tm, width), lambda bi, si: (bi, si, 0))
    return pl.pallas_call(
        functools.partial(_swa_front_kernel, half=half),
        grid=(b, s // tm),
        in_specs=[row(d),
                  pl.BlockSpec((1, d), lambda bi, si: (0, 0)),
                  pl.BlockSpec((d, n), lambda bi, si: (0, 0)),
                  pl.BlockSpec((1, n), lambda bi, si: (0, 0)),
                  tab_spec, tab_spec, tab_spec, tab_spec, tab_spec, tab_spec],
        out_specs=[row(nq), row(2 * nkv)],
        out_shape=[jax.ShapeDtypeStruct((b, s, nq), BF16),
                   jax.ShapeDtypeStruct((b, s, 2 * nkv), BF16)],
        compiler_params=_cparams(("arbitrary", "arbitrary")),
        name="swa_front",
    )(x, g.reshape(1, d), w_qkv, b_qkv.reshape(1, n), *tabs_q, *tabs_k)


def _swa_head_order():
    g_sz = SWA_Q_HEADS // SWA_KV_HEADS
    order = []
    for slab in range(SWA_Q_HEADS // 2):
        pair, j = slab // g_sz, slab % g_sz
        order += [(2 * pair) * g_sz + j, (2 * pair + 1) * g_sz + j]
    return order


def _swa_attn_kernel(sink_ref, q_ref, kvp_ref, kvc_ref, o_ref, *, nblk):
    i = pl.program_id(1)
    hd = SWA_HEAD_DIM
    nkv = SWA_KV_HEADS * hd
    g_sz = SWA_Q_HEADS // SWA_KV_HEADS
    kv = jnp.concatenate([kvp_ref[0], kvc_ref[0]], axis=0)
    r = lax.broadcasted_iota(I32, (BLOCK, 2 * BLOCK), 0)
    c = lax.broadcasted_iota(I32, (BLOCK, 2 * BLOCK), 1)
    rel = c - BLOCK - r
    in_win = (rel <= 0) & (rel > -SWA_WINDOW)
    lane = lax.broadcasted_iota(I32, (1, LANES), 1)
    lo_half = lane < hd
    ones = jnp.ones((2 * BLOCK, LANES), BF16)
    for n in range(nblk):
        mask = in_win & ((c >= BLOCK) | (i > 0)) if n == 0 else in_win
        keys = kv[n * BLOCK:(n + 2) * BLOCK]
        for pair in range(SWA_KV_HEADS // 2):
            k2 = keys[:, pair * LANES:(pair + 1) * LANES]
            v2 = keys[:, nkv + pair * LANES:nkv + (pair + 1) * LANES]
            v_ext = jnp.concatenate([v2, ones], axis=1)
            pieces = []
            for j in range(g_sz):
                slab = pair * g_sz + j
                qs = q_ref[0, n * BLOCK:(n + 1) * BLOCK, slab * LANES:(slab + 1) * LANES]
                zero = jnp.zeros_like(qs)
                pieces += [jnp.where(lo_half, qs, zero), jnp.where(lo_half, zero, qs)]
            sc = _dot_nt(jnp.concatenate(pieces, axis=0), k2)
            probs, tails = [], []
            for pc in range(2 * g_sz):
                scp = jnp.where(mask, sc[pc * BLOCK:(pc + 1) * BLOCK], NEG_INF)
                sink = sink_ref[2 * g_sz * pair + pc] * LOG2E
                m = jnp.maximum(jnp.max(scp, axis=-1, keepdims=True), sink)
                probs.append(jnp.exp2(scp - m).astype(BF16))
                tails.append(jnp.exp2(sink - m))
            pv = _dot(jnp.concatenate(probs, axis=0), v_ext)
            for j in range(g_sz):
                slab = pair * g_sz + j
                halves = []
                for hf in range(2):
                    pc = 2 * j + hf
                    blk = pv[pc * BLOCK:(pc + 1) * BLOCK]
                    halves.append(blk[:, :LANES] / (blk[:, LANES:] + tails[pc]))
                o_ref[0, n * BLOCK:(n + 1) * BLOCK, slab * LANES:(slab + 1) * LANES] = (
                    jnp.where(lo_half, halves[0], halves[1]).astype(BF16))


def _swa_attn(q, kv, sinks):
    b, s, nq = q.shape
    nblk = SWA_BLOCKS_PER_STEP
    tq = nblk * BLOCK
    return pl.pallas_call(
        functools.partial(_swa_attn_kernel, nblk=nblk),
        grid_spec=pltpu.PrefetchScalarGridSpec(
            num_scalar_prefetch=1,
            grid=(b, s // tq),
            in_specs=[pl.BlockSpec((1, tq, nq), lambda bi, ni, sk: (bi, ni, 0)),
                      pl.BlockSpec((1, BLOCK, kv.shape[2]), lambda bi, ni, sk: (bi, jnp.maximum(ni * nblk - 1, 0), 0)),
                      pl.BlockSpec((1, tq, kv.shape[2]), lambda bi, ni, sk: (bi, ni, 0))],
            out_specs=pl.BlockSpec((1, tq, nq), lambda bi, ni, sk: (bi, ni, 0))),
        out_shape=jax.ShapeDtypeStruct((b, s, nq), BF16),
        compiler_params=_cparams(("arbitrary", "arbitrary")),
        name="swa_attn",
    )(sinks, q, kv, kv)


def _post_kernel(*refs, tm, n_a, has_bias):
    x_ref = refs[0]
    a_refs = refs[1:1 + n_a]
    k = 1 + n_a
    wo_refs = refs[k:k + n_a]
    k += n_a
    if has_bias:
        bo_ref = refs[k]
        k += 1
    (gx_ref, wq_ref, mkv_ref, wxo_ref, gf_ref, wr_ref, br_ref,
     x2_ref, h2_ref, mi_ref, mw_ref, cnt_ref, cnt_acc, cnt_col) = refs[k:]
    first = (pl.program_id(0) == 0) & (pl.program_id(1) == 0)

    @pl.when(first)
    def _():
        cnt_acc[...] = jnp.zeros_like(cnt_acc)
        cnt_col[...] = jnp.zeros_like(cnt_col)

    xw = XATTN_HEADS * XATTN_HEAD_DIM
    ones = jnp.ones((mkv_ref.shape[1], LANES), BF16)

    rh = tm // POST_CHAINS

    def out_proj(r0):
        acc = _dot(a_refs[0][0, r0:r0 + rh, :], wo_refs[0][...])
        for a_ref, w_ref in zip(a_refs[1:], wo_refs[1:]):
            acc = acc + _dot(a_ref[0, r0:r0 + rh, :], w_ref[...])
        if has_bias:
            acc = acc + bo_ref[...]
        return x_ref[0, r0:r0 + rh, :] + acc

    def q_proj(x1):
        hx = _rms(x1, gx_ref[...]).astype(BF16)
        return (_dot(hx, wq_ref[...]) * (XATTN_HEAD_DIM ** -0.5 * LOG2E)).astype(BF16)

    def mem_attn(qx):
        outs = []
        for hh in range(XATTN_HEADS):
            lo = hh * XATTN_HEAD_DIM
            mk = mkv_ref[0, :, lo:lo + XATTN_HEAD_DIM]
            mv = jnp.concatenate([mkv_ref[0, :, xw + lo:xw + lo + XATTN_HEAD_DIM], ones], axis=1)
            sc = _dot_nt(qx[:, lo:lo + XATTN_HEAD_DIM], mk)
            m = jnp.max(sc, axis=-1, keepdims=True)
            pv = _dot(jnp.exp2(sc - m).astype(BF16), mv)
            outs.append((pv[:, :LANES] / pv[:, LANES:]).astype(BF16))
        return jnp.concatenate(outs, axis=-1)

    def o_proj(r0, x1, ox):
        x2 = x1 + _dot(ox, wxo_ref[...])
        x2_ref[0, r0:r0 + rh, :] = x2
        return x2

    def router(r0, x2):
        h2 = _rms(x2, gf_ref[...])
        _to_token_tiles(h2_ref, h2, rh, r0)
        return _dot(h2.astype(BF16), wr_ref[...]) + br_ref[...]

    starts = [c * rh for c in range(POST_CHAINS)]
    x1s = [out_proj(r0) for r0 in starts]
    qxs = [q_proj(x1) for x1 in x1s]
    oxs = [mem_attn(qx) for qx in qxs]
    x2s = [o_proj(r0, x1, ox) for r0, x1, ox in zip(starts, x1s, oxs)]
    logits = jnp.concatenate([router(r0, x2) for r0, x2 in zip(starts, x2s)], axis=0)

    lt = logits.T
    ex = lt[0:N_EXPERTS]
    grp = lt[N_EXPERTS:N_EXPERTS + SUBLANES]
    grow = lax.broadcasted_iota(I32, grp.shape, 0).astype(F32)
    lg = jnp.where(grow < N_GROUPS, grp, NEG_INF)
    mg = jnp.max(lg, axis=0, keepdims=True)
    g_sel = jnp.min(jnp.where(lg == mg, grow, float(SUBLANES)), axis=0, keepdims=True)
    p_g = 1.0 / jnp.sum(jnp.exp(lg - mg), axis=0, keepdims=True)
    erow = lax.broadcasted_iota(I32, ex.shape, 0).astype(F32)
    e_lo = g_sel * EXPERTS_PER_GROUP
    le = jnp.where((erow >= e_lo) & (erow < e_lo + EXPERTS_PER_GROUP), ex, NEG_INF)
    big = float(N_EXPERTS)
    m1 = jnp.max(le, axis=0, keepdims=True)
    i1 = jnp.min(jnp.where(le == m1, erow, big), axis=0, keepdims=True)
    le2 = jnp.where(erow == i1, NEG_INF, le)
    m2 = jnp.max(le2, axis=0, keepdims=True)
    i2 = jnp.min(jnp.where(le2 == m2, erow, big), axis=0, keepdims=True)
    t = jnp.exp(m2 - m1)
    w1 = p_g / (1.0 + t)
    w2 = p_g * t / (1.0 + t)

    oh1 = erow == i1
    oh2 = erow == i2
    oh = jnp.where(oh1 | oh2, 1.0, 0.0)
    rr = lax.broadcasted_iota(I32, (tm, tm), 0)
    cc = lax.broadcasted_iota(I32, (tm, tm), 1)
    earlier = jnp.where(rr < cc, 1.0, 0.0).astype(BF16)
    before = _dot(oh.astype(BF16), earlier) + cnt_col[...]
    r1 = jnp.sum(jnp.where(oh1, before, 0.0), axis=0, keepdims=True)
    r2 = jnp.sum(jnp.where(oh2, before, 0.0), axis=0, keepdims=True)
    cnt_col[...] = cnt_col[...] + jnp.sum(oh, axis=1, keepdims=True)
    oh_pad = jnp.concatenate([oh, jnp.zeros((LANES - N_EXPERTS, tm), F32)], axis=0).astype(BF16)
    cnt_new = cnt_acc[...] + _dot_nt(jnp.ones((SUBLANES, tm), BF16), oh_pad)[0:1]
    cnt_acc[...] = cnt_new
    cnt_ref[...] = cnt_new.astype(I32)

    zrow = jnp.zeros((1, tm), F32)
    mi_ref[...] = jnp.concatenate([i1, i2, r1, r2, zrow, zrow, zrow, zrow], axis=0).astype(I32)
    wt = jnp.concatenate([w1, w2, jnp.zeros((LANES - TOP_K, tm), F32)], axis=0)
    mw_ref[...] = wt.T


def _post_mixer(x, a_list, wo_list, b_out, g_x, w_q, mkv, w_xo, g_f, w_rt, b_rt):
    b, s, d = x.shape
    tm = ROW_TILE
    ns = s // tm
    t = b * s
    n_a = len(a_list)
    has_bias = b_out is not None
    const2 = lambda bi, si: (0, 0)
    row = lambda width: pl.BlockSpec((1, tm, width), lambda bi, si: (bi, si, 0))
    in_specs = [row(d)] + [row(a.shape[2]) for a in a_list]
    in_specs += [pl.BlockSpec(w.shape, const2) for w in wo_list]
    args = [x, *a_list, *wo_list]
    if has_bias:
        in_specs.append(pl.BlockSpec((1, d), const2))
        args.append(b_out.reshape(1, d))
    xw = w_q.shape[1]
    in_specs += [pl.BlockSpec((1, d), const2),
                 pl.BlockSpec((d, xw), const2),
                 pl.BlockSpec((1, mkv.shape[1], mkv.shape[2]), lambda bi, si: (bi, 0, 0)),
                 pl.BlockSpec((xw, d), const2),
                 pl.BlockSpec((1, d), const2),
                 pl.BlockSpec((d, LANES), const2),
                 pl.BlockSpec((1, LANES), const2)]
    args += [g_x.reshape(1, d), w_q, mkv, w_xo, g_f.reshape(1, d), w_rt, b_rt]
    out_specs = [row(d),
                 pl.BlockSpec((tm * TOKEN_ROWS, LANES), lambda bi, si: (bi * ns + si, 0)),
                 pl.BlockSpec((8, tm), lambda bi, si: (0, bi * ns + si)),
                 pl.BlockSpec((tm, LANES), lambda bi, si: (bi * ns + si, 0)),
                 pl.BlockSpec((1, LANES), const2)]
    out_shape = [jax.ShapeDtypeStruct((b, s, d), F32),
                 jax.ShapeDtypeStruct((t * TOKEN_ROWS, LANES), U32),
                 jax.ShapeDtypeStruct((8, t), I32),
                 jax.ShapeDtypeStruct((t, LANES), F32),
                 jax.ShapeDtypeStruct((1, LANES), I32)]
    return pl.pallas_call(
        functools.partial(_post_kernel, tm=tm, n_a=n_a, has_bias=has_bias),
        grid=(b, ns),
        in_specs=in_specs,
        out_specs=out_specs,
        out_shape=out_shape,
        scratch_shapes=[pltpu.VMEM((1, LANES), F32), pltpu.VMEM((N_EXPERTS, 1), F32)],
        compiler_params=_cparams(("arbitrary", "arbitrary")),
        name="post_mixer",
    )(*args)


def _to_token_tiles(dst_ref, val, rows, tok0=0):
    half = TOKEN_ROWS * LANES
    for j in range(TOKEN_ROWS):
        hi = val[:, j * LANES:(j + 1) * LANES].astype(BF16).astype(F32)
        lo = val[:, half + j * LANES:half + (j + 1) * LANES].astype(BF16).astype(F32)
        word = lax.bitcast_convert_type(hi, U32) | (lax.bitcast_convert_type(lo, U32) >> 16)
        dst_ref[pl.ds(tok0 * TOKEN_ROWS + j, rows, stride=TOKEN_ROWS), :] = word


def _from_token_tiles(src_ref, rows, tok0=0):
    his, los = [], []
    for j in range(TOKEN_ROWS):
        word = src_ref[pl.ds(tok0 * TOKEN_ROWS + j, rows, stride=TOKEN_ROWS), :]
        his.append(lax.bitcast_convert_type(word & jnp.uint32(0xFFFF0000), F32))
        los.append(lax.bitcast_convert_type(word << 16, F32))
    return jnp.concatenate(his + los, axis=1)


def _token_slice(ref, tok):
    return ref.at[pl.ds(pl.multiple_of(tok * TOKEN_ROWS, TOKEN_ROWS), TOKEN_ROWS)]


def _pos_kernel(off_ref, mi_ref, pos_ref):
    e = mi_ref[0:TOP_K, :]
    pos = mi_ref[TOP_K:2 * TOP_K, :]
    for j in range(N_EXPERTS):
        pos = pos + jnp.where(e == j, off_ref[j], 0)
    pos_ref[...] = pos


def _sorted_positions(off, meta_i):
    t = meta_i.shape[1]
    return pl.pallas_call(
        _pos_kernel,
        grid_spec=pltpu.PrefetchScalarGridSpec(
            num_scalar_prefetch=1,
            grid=(1,),
            in_specs=[pl.BlockSpec((8, t), lambda i, off: (0, 0))],
            out_specs=pl.BlockSpec((TOP_K, t), lambda i, off: (0, 0))),
        out_shape=jax.ShapeDtypeStruct((TOP_K, t), I32),
        compiler_params=_cparams(("arbitrary",)),
        name="moe_positions",
    )(off, meta_i)


def _tile_major(pos, tm):
    t = pos.shape[1]
    return pos.reshape(TOP_K, t // tm, tm).transpose(1, 0, 2).reshape(-1)


def _dispatch_kernel(pos_ref, h_ref, xs_ref, sem, *, tm):
    def issue(g, _):
        for u in range(DMA_UNROLL):
            r = g * DMA_UNROLL + u
            for kk in range(TOP_K):
                pos = pos_ref[kk * tm + r]
                pltpu.make_async_copy(_token_slice(h_ref, r), _token_slice(xs_ref, pos), sem).start(priority=kk)
        return 0

    lax.fori_loop(0, tm // DMA_UNROLL, issue, 0)
    for _ in range(TOP_K):
        pltpu.make_async_copy(h_ref, xs_ref.at[pl.ds(0, tm * TOKEN_ROWS)], sem).wait()


def _dispatch(pos, h2t, n_rows):
    tm = DISPATCH_TILE
    t = h2t.shape[0] // TOKEN_ROWS
    return pl.pallas_call(
        functools.partial(_dispatch_kernel, tm=tm),
        grid=(t // tm,),
        in_specs=[pl.BlockSpec((TOP_K * tm,), lambda i: (i,), memory_space=pltpu.SMEM),
                  pl.BlockSpec((tm * TOKEN_ROWS, LANES), lambda i: (i, 0))],
        out_specs=pl.BlockSpec(memory_space=pl.ANY),
        scratch_shapes=[pltpu.SemaphoreType.DMA],
        out_shape=jax.ShapeDtypeStruct((n_rows * TOKEN_ROWS, LANES), U32),
        compiler_params=_cparams(("arbitrary",)),
        name="moe_dispatch",
    )(_tile_major(pos, tm), h2t)


def _expert_kernel(tblk_ref, texp_ref, tn_ref, tnxt_ref, tpar_ref, xs_ref, wgu_hbm, wdn_hbm, ys_ref,
                   wgu_bf, wdn_bf, wgu_f, wdn_f, sems, *, tm, layer):
    i = pl.program_id(0)
    n_valid = tn_ref[i]
    expert = texp_ref[i]
    slot = tpar_ref[i]
    new_expert = (i == 0) | (expert != texp_ref[jnp.maximum(i - 1, 0)])

    def weight_copies(ex, sl):
        return (pltpu.make_async_copy(wgu_hbm.at[layer, ex], wgu_f.at[sl], sems.at[sl]),
                pltpu.make_async_copy(wdn_hbm.at[layer, ex], wdn_f.at[sl], sems.at[sl]))

    @pl.when(i == 0)
    def _():
        for cp in weight_copies(expert, slot):
            cp.start()

    @pl.when(new_expert)
    def _():
        @pl.when(tnxt_ref[i] >= 0)
        def _():
            for cp in weight_copies(tnxt_ref[i], 1 - slot):
                cp.start()

        for cp in weight_copies(expert, slot):
            cp.wait()
        wgu_bf[...] = wgu_f[slot].astype(BF16)
        wdn_bf[...] = wdn_f[slot].astype(BF16)

    @pl.when(n_valid > 0)
    def _():
        rh = tm // EXPERT_CHAINS
        starts = [c * rh for c in range(EXPERT_CHAINS)]
        row = lax.broadcasted_iota(I32, (rh, 1), 0)
        gus = []
        for r0 in starts:
            x = jnp.where(row + r0 < n_valid, _from_token_tiles(xs_ref, rh, r0), 0.0).astype(BF16)
            gus.append(_dot(x, wgu_bf[...]))
        ys = []
        for gu in gus:
            hid = (gu[:, :EXPERT_HIDDEN] / (1.0 + jnp.exp(-gu[:, :EXPERT_HIDDEN])) * gu[:, EXPERT_HIDDEN:]).astype(BF16)
            ys.append(_dot(hid, wdn_bf[...]))
        for r0, y in zip(starts, ys):
            _to_token_tiles(ys_ref, y, rh, r0)


def _experts(plan, xs, w_gu, w_dn, layer):
    tm = EXPERT_TILE
    nt = xs.shape[0] // (tm * TOKEN_ROWS)
    gu_shape, dn_shape = w_gu.shape[2:], w_dn.shape[2:]
    rows = lambda i, tb, te, tn, tx, tp: (tb[i], 0)
    return pl.pallas_call(
        functools.partial(_expert_kernel, tm=tm, layer=layer),
        grid_spec=pltpu.PrefetchScalarGridSpec(
            num_scalar_prefetch=5,
            grid=(nt,),
            in_specs=[pl.BlockSpec((tm * TOKEN_ROWS, LANES), rows),
                      pl.BlockSpec(memory_space=pl.ANY),
                      pl.BlockSpec(memory_space=pl.ANY)],
            out_specs=pl.BlockSpec((tm * TOKEN_ROWS, LANES), rows),
            scratch_shapes=[pltpu.VMEM(gu_shape, BF16), pltpu.VMEM(dn_shape, BF16),
                            pltpu.VMEM((2,) + gu_shape, F32), pltpu.VMEM((2,) + dn_shape, F32),
                            pltpu.SemaphoreType.DMA((2,))]),
        out_shape=jax.ShapeDtypeStruct(xs.shape, U32),
        compiler_params=_cparams(("arbitrary",)),
        name="moe_experts",
    )(*plan, xs, w_gu, w_dn)


def _moe_combine_tile(pos_cur, pos_nxt, mw_ref, x, ys_ref, ybuf, sems, i, n, tm):
    def gather(pos_ref, slot):
        def issue(g, _):
            for u in range(DMA_UNROLL):
                r = g * DMA_UNROLL + u
                for kk in range(TOP_K):
                    pos = pos_ref[kk * tm + r]
                    pltpu.make_async_copy(_token_slice(ys_ref, pos), _token_slice(ybuf.at[slot, kk], r),
                                          sems.at[slot]).start(priority=kk)
            return 0

        lax.fori_loop(0, tm // DMA_UNROLL, issue, 0)

    @pl.when(i == 0)
    def _():
        gather(pos_cur, 0)

    @pl.when(i + 1 < n)
    def _():
        gather(pos_nxt, (i + 1) % 2)

    slot = i % 2
    for kk in range(TOP_K):
        pltpu.make_async_copy(ys_ref.at[pl.ds(0, tm * TOKEN_ROWS)], ybuf.at[slot, kk], sems.at[slot]).wait()
    mw = mw_ref[...]
    return (x + mw[:, 0:1] * _from_token_tiles(ybuf.at[slot, 0], tm)
            + mw[:, 1:2] * _from_token_tiles(ybuf.at[slot, 1], tm))


def _combine_kernel(pos_cur, pos_nxt, mw_ref, x_ref, ys_ref, *rest, tm, final):
    if final:
        g_ref, o_ref, ybuf, sems = rest
    else:
        o_ref, ybuf, sems = rest
    x3 = _moe_combine_tile(pos_cur, pos_nxt, mw_ref, x_ref[...], ys_ref, ybuf, sems,
                           pl.program_id(0), pl.num_programs(0), tm)
    if final:
        x3 = _rms(x3, g_ref[...])
    o_ref[...] = x3


def _combine(moe, g_final):
    pos, meta_w, x2, ys = moe
    d = x2.shape[-1]
    x2d = x2.reshape(-1, d)
    t = x2d.shape[0]
    tm = COMBINE_TILE
    nt = t // tm
    final = g_final is not None
    in_specs = [pl.BlockSpec((TOP_K * tm,), lambda i: (i,), memory_space=pltpu.SMEM),
                pl.BlockSpec((TOP_K * tm,), lambda i: (jnp.minimum(i + 1, nt - 1),), memory_space=pltpu.SMEM),
                pl.BlockSpec((tm, LANES), lambda i: (i, 0)),
                pl.BlockSpec((tm, d), lambda i: (i, 0)),
                pl.BlockSpec(memory_space=pl.ANY)]
    pos_flat = _tile_major(pos, tm)
    args = [pos_flat, pos_flat, meta_w, x2d, ys]
    if final:
        in_specs.append(pl.BlockSpec((1, d), lambda i: (0, 0)))
        args.append(g_final.reshape(1, d))
    return pl.pallas_call(
        functools.partial(_combine_kernel, tm=tm, final=final),
        grid=(nt,),
        in_specs=in_specs,
        out_specs=pl.BlockSpec((tm, d), lambda i: (i, 0)),
        scratch_shapes=[pltpu.VMEM((2, TOP_K, tm * TOKEN_ROWS, LANES), U32), pltpu.SemaphoreType.DMA((2,))],
        out_shape=jax.ShapeDtypeStruct((t, d), F32),
        compiler_params=_cparams(("arbitrary",)),
        name="moe_combine",
    )(*args)


def _plan_kernel(cnt_ref, off_ref, blk_ref, exp_ref, nv_ref, nxt_ref, par_ref, *, tm, n_tiles):
    shift = tm.bit_length() - 1

    def clear(i, _):
        nxt_ref[i] = -1
        return 0

    lax.fori_loop(0, n_tiles, clear, 0)

    def per_expert(e, carry):
        row0, run, prev_first = carry
        c = cnt_ref[0, e]
        ntile = lax.shift_right_logical(c + (tm - 1), shift)
        off_ref[e] = row0
        t0 = lax.shift_right_logical(row0, shift)

        def fill(j, _):
            blk_ref[t0 + j] = t0 + j
            exp_ref[t0 + j] = e
            nv_ref[t0 + j] = jnp.minimum(c - j * tm, tm)
            par_ref[t0 + j] = run & 1
            return 0

        lax.fori_loop(0, ntile, fill, 0)
        has = ntile > 0

        @pl.when(has & (prev_first >= 0))
        def _():
            nxt_ref[prev_first] = e

        return (row0 + lax.shift_left(ntile, shift), run + has.astype(I32), jnp.where(has, t0, prev_first))

    total, _, _ = lax.fori_loop(0, N_EXPERTS, per_expert, (jnp.int32(0), jnp.int32(0), jnp.int32(-1)))
    used = lax.shift_right_logical(total, shift)
    last = jnp.maximum(used - 1, 0)

    def tail(i, _):
        blk_ref[i] = last
        exp_ref[i] = exp_ref[last]
        nv_ref[i] = 0
        par_ref[i] = par_ref[last]
        return 0

    lax.fori_loop(used, n_tiles, tail, 0)


def _moe_plan(counts, n_tiles):
    tm = EXPERT_TILE
    assert tm & (tm - 1) == 0
    smem = pl.BlockSpec(memory_space=pltpu.SMEM)
    return pl.pallas_call(
        functools.partial(_plan_kernel, tm=tm, n_tiles=n_tiles),
        in_specs=[smem],
        out_specs=[smem] * 6,
        out_shape=[jax.ShapeDtypeStruct((N_EXPERTS,), I32)] + [jax.ShapeDtypeStruct((n_tiles,), I32)] * 5,
        name="moe_plan",
    )(counts)


def _moe(x2, h2t, meta_i, meta_w, counts, w_gu, w_dn, layer):
    b, s, d = x2.shape
    assert d == 2 * TOKEN_ROWS * LANES
    n_rows = b * s * TOP_K + N_EXPERTS * EXPERT_TILE
    off, *tile_plan = _moe_plan(counts, n_rows // EXPERT_TILE)
    pos = _sorted_positions(off, meta_i)
    xs = _dispatch(pos, h2t, n_rows)
    ys = _experts(tile_plan, xs, w_gu, w_dn, layer)
    return pos, meta_w, x2, ys


def _router_weights(w_group, b_group, w_router, b_router):
    d = w_group.shape[0]
    pad = LANES - N_EXPERTS - N_GROUPS
    w = jnp.concatenate([w_router, w_group, jnp.zeros((d, pad), F32)], axis=1)
    bias = jnp.concatenate([b_router, b_group, jnp.zeros((pad,), F32)]).reshape(1, LANES)
    return w.astype(BF16), bias


def kernel(x, mem, mem_norm, mem_w_kv, norm_mix, norm_xattn, norm_ffn, hyb_w_in, hyb_conv_w, diff_lambda, diff_subln, hyb_w_out, swa_w_qkv, swa_b_qkv, swa_sinks, swa_w_out, swa_b_out, xattn_w_q, xattn_w_o, moe_w_group, moe_b_group, moe_w_router, moe_b_router, moe_w_gate_up, moe_w_down, final_norm):
    b, s, d = x.shape
    m = mem.shape[1]
    depth = norm_mix.shape[0]
    assert d == 2 * TOKEN_ROWS * LANES and s % ROW_TILE == 0 and s % (SWA_BLOCKS_PER_STEP * BLOCK) == 0
    assert (b * s) % DISPATCH_TILE == 0 and (b * s) % COMBINE_TILE == 0 and DISPATCH_TILE % DMA_UNROLL == 0
    mem_tile = math.gcd(b * m, ROW_TILE)
    mkv = _norm_proj(mem.reshape(b * m, d), mem_norm, mem_w_kv.astype(BF16), mem_tile).reshape(b, m, -1)

    scale = DIFF_DK ** -0.5 * LOG2E
    cq, suq, sdq, half = _rope_lane_tables(s, DIFF_DK, scale)
    ck, suk, sdk, _ = _rope_lane_tables(s, DIFF_DK, 1.0)
    tabs_q, tabs_k = (cq, suq, sdq), (ck, suk, sdk)

    moe = None
    for l in range(depth):
        if moe is not None:
            x = _combine(moe, None).reshape(b, s, d)
        if l % 2 == 0:
            e = l // 2
            lambda_init = 0.8 - 0.6 * math.exp(-0.3 * l)
            ya, q, k, v = _hyb_front(x, norm_mix[l], hyb_w_in[e].astype(BF16), hyb_conv_w[e], tabs_q, tabs_k, half)
            o = _diff_attn(q, k, v, diff_lambda[e], diff_subln[e], lambda_init)
            w_out = hyb_w_out[e].astype(BF16)
            a_list, wo_list, b_out = [ya, o], [w_out[:CONV_CH], w_out[CONV_CH:]], None
        else:
            e = l // 2
            order = _swa_head_order()
            hd, nq = SWA_HEAD_DIM, SWA_Q_HEADS * SWA_HEAD_DIM
            heads = lambda a, axis: [lax.slice_in_dim(a, h * hd, (h + 1) * hd, axis=axis) for h in order]
            w_all = swa_w_qkv[e].astype(BF16)
            w_qkv = jnp.concatenate(heads(w_all, 1) + [w_all[:, nq:]], axis=1)
            b_qkv = jnp.concatenate(heads(swa_b_qkv[e], 0) + [swa_b_qkv[e][nq:]])
            sinks = jnp.stack([swa_sinks[e][h] for h in order])
            q, kv = _swa_front(x, norm_mix[l], w_qkv, b_qkv, tabs_q, tabs_k, half)
            o = _swa_attn(q, kv, sinks)
            w_out = jnp.concatenate(heads(swa_w_out[e].astype(BF16), 0), axis=0)
            a_list, wo_list, b_out = [o], [w_out], swa_b_out[e]
        w_rt, b_rt = _router_weights(moe_w_group[l], moe_b_group[l], moe_w_router[l], moe_b_router[l])
        x2, h2, meta_i, meta_w, counts = _post_mixer(
            x, a_list, wo_list, b_out, norm_xattn[l], xattn_w_q[l].astype(BF16), mkv,
            xattn_w_o[l].astype(BF16), norm_ffn[l], w_rt, b_rt)
        moe = _moe(x2, h2, meta_i, meta_w, counts, moe_w_gate_up, moe_w_down, l)
    return _combine(moe, final_norm).reshape(b, s, d)
```

```python
import functools
import math

import jax
import jax.numpy as jnp
from jax import lax
from jax.experimental import pallas as pl
from jax.experimental.pallas import tpu as pltpu

F32 = jnp.float32
BF16 = jnp.bfloat16
I32 = jnp.int32
U32 = jnp.uint32

EPS = 1e-6
LANES = 128
SUBLANES = 8
TOKEN_ROWS = 4
DMA_UNROLL = 16
VMEM_LIMIT = 56 * 1024 * 1024

ROPE_THETA = 500000.0
ROPE_FRACTION = 4
BLOCK = 128
CONV_CH = 512
CONV_K = 3
DIFF_HEADS = 4
DIFF_DK = 64
DIFF_DV = 128
SWA_Q_HEADS = 16
SWA_KV_HEADS = 4
SWA_HEAD_DIM = 64
SWA_WINDOW = 128
XATTN_HEADS = 4
XATTN_HEAD_DIM = 128
N_GROUPS = 4
EXPERTS_PER_GROUP = 8
N_EXPERTS = N_GROUPS * EXPERTS_PER_GROUP
TOP_K = 2
EXPERT_HIDDEN = 512

ROW_TILE = 1024
EXPERT_TILE = 512
DISPATCH_TILE = 4096
COMBINE_TILE = 512
COMBINE_SLOTS = 3
SWA_BLOCKS_PER_STEP = 8
POST_CHAINS = 2
EXPERT_CHAINS = 2
FRONT_CHAINS = 2
NEG_INF = float("-inf")
LOG2E = math.log2(math.e)


def _cparams(sem):
    return pltpu.CompilerParams(dimension_semantics=sem, vmem_limit_bytes=VMEM_LIMIT)


def _rms(x, g):
    return x * lax.rsqrt(jnp.mean(x * x, axis=-1, keepdims=True) + EPS) * g


def _dot(a, b):
    return jnp.dot(a, b, preferred_element_type=F32)


def _dot_nt(a, b):
    return lax.dot_general(a, b, (((1,), (1,)), ((), ())), preferred_element_type=F32)


def _rope_lane_tables(seq, head_dim, scale):
    rot = head_dim // ROPE_FRACTION
    half = rot // 2
    pos = jnp.arange(seq, dtype=F32)
    inv = ROPE_THETA ** (-jnp.arange(0, rot, 2, dtype=F32) / rot)
    ang = pos[:, None] * inv[None, :]
    cos, sin = jnp.cos(ang), jnp.sin(ang)
    idx = jnp.arange(LANES) % head_dim
    cl = jnp.take(cos, idx % half, axis=1)
    sl = jnp.take(sin, idx % half, axis=1)
    c = jnp.where(idx < rot, cl, 1.0) * scale
    s_up = jnp.where(idx < half, -sl, 0.0) * scale
    s_dn = jnp.where((idx >= half) & (idx < rot), sl, 0.0) * scale
    return c.astype(F32), s_up.astype(F32), s_dn.astype(F32), half


def _rope_chunk(xc, c, s_up, s_dn, half):
    return (xc * c + pltpu.roll(xc, LANES - half, 1) * s_up + pltpu.roll(xc, half, 1) * s_dn)


def _norm_proj_kernel(x_ref, g_ref, w_ref, o_ref):
    h = _rms(x_ref[...], g_ref[...]).astype(BF16)
    o_ref[...] = _dot(h, w_ref[...]).astype(o_ref.dtype)


def _norm_proj(x2d, g, w_bf16, tm):
    m, d = x2d.shape
    n = w_bf16.shape[1]
    return pl.pallas_call(
        _norm_proj_kernel,
        grid=(m // tm,),
        in_specs=[pl.BlockSpec((tm, d), lambda i: (i, 0)),
                  pl.BlockSpec((1, d), lambda i: (0, 0)),
                  pl.BlockSpec((d, n), lambda i: (0, 0))],
        out_specs=pl.BlockSpec((tm, n), lambda i: (i, 0)),
        out_shape=jax.ShapeDtypeStruct((m, n), BF16),
        compiler_params=_cparams(("arbitrary",)),
        name="mem_kv_proj",
    )(x2d, g.reshape(1, d), w_bf16)


def _hyb_front_kernel(x_ref, g_ref, w_ref, cw_ref, cq_ref, suq_ref, sdq_ref, ck_ref, suk_ref, sdk_ref,
                      ya_ref, q_ref, k_ref, v_ref, cbuf, *, tm, half):
    s = pl.program_id(1)
    c = CONV_CH
    base = 3 * c
    nq = DIFF_HEADS * 2 * DIFF_DK
    rh = tm // FRONT_CHAINS
    starts = [ch * rh for ch in range(FRONT_CHAINS)]
    cw = cw_ref[...]

    @pl.when(s == 0)
    def _():
        cbuf[0:8, :] = jnp.zeros((8, c), F32)

    hs = [_rms(x_ref[0, r0:r0 + rh, :], g_ref[...]).astype(BF16) for r0 in starts]
    for r0, h in zip(starts, hs):
        gate_b = _dot(h, w_ref[:, 0:c])
        cu = _dot(h, w_ref[:, c:2 * c]) * _dot(h, w_ref[:, 2 * c:3 * c])
        cbuf[8 + r0:8 + r0 + rh, :] = cu
        conv = (cw[0:1, :] * cbuf[6 + r0:6 + r0 + rh, :] + cw[1:2, :] * cbuf[7 + r0:7 + r0 + rh, :] + cw[2:3, :] * cu)
        ya_ref[0, r0:r0 + rh, :] = (gate_b * conv).astype(BF16)
    cbuf[0:8, :] = cbuf[tm:tm + 8, :]

    for r0, h in zip(starts, hs):
        cq, suq, sdq = cq_ref[r0:r0 + rh, :], suq_ref[r0:r0 + rh, :], sdq_ref[r0:r0 + rh, :]
        pq = _dot(h, w_ref[:, base:base + nq])
        for j in range(nq // LANES):
            q_ref[0, r0:r0 + rh, j * LANES:(j + 1) * LANES] = _rope_chunk(
                pq[:, j * LANES:(j + 1) * LANES], cq, suq, sdq, half).astype(BF16)
    for r0, h in zip(starts, hs):
        ck, suk, sdk = ck_ref[r0:r0 + rh, :], suk_ref[r0:r0 + rh, :], sdk_ref[r0:r0 + rh, :]
        pk = _dot(h, w_ref[:, base + nq:base + 2 * nq])
        for j in range(nq // LANES):
            k_ref[0, r0:r0 + rh, j * LANES:(j + 1) * LANES] = _rope_chunk(
                pk[:, j * LANES:(j + 1) * LANES], ck, suk, sdk, half).astype(BF16)
    for r0, h in zip(starts, hs):
        v_ref[0, r0:r0 + rh, :] = _dot(h, w_ref[:, base + 2 * nq:]).astype(BF16)


def _hyb_front(x, g, w_in, conv_w, tabs_q, tabs_k, half):
    b, s, d = x.shape
    tm = ROW_TILE
    n = w_in.shape[1]
    nq = DIFF_HEADS * 2 * DIFF_DK
    nv = DIFF_HEADS * DIFF_DV
    tab_spec = pl.BlockSpec((tm, LANES), lambda bi, si: (si, 0))
    row = lambda width: pl.BlockSpec((1, tm, width), lambda bi, si: (bi, si, 0))
    return pl.pallas_call(
        functools.partial(_hyb_front_kernel, tm=tm, half=half),
        grid=(b, s // tm),
        in_specs=[row(d),
                  pl.BlockSpec((1, d), lambda bi, si: (0, 0)),
                  pl.BlockSpec((d, n), lambda bi, si: (0, 0)),
                  pl.BlockSpec((CONV_K, CONV_CH), lambda bi, si: (0, 0)),
                  tab_spec, tab_spec, tab_spec, tab_spec, tab_spec, tab_spec],
        out_specs=[row(CONV_CH), row(nq), row(nq), row(nv)],
        out_shape=[jax.ShapeDtypeStruct((b, s, CONV_CH), BF16),
                   jax.ShapeDtypeStruct((b, s, nq), BF16),
                   jax.ShapeDtypeStruct((b, s, nq), BF16),
                   jax.ShapeDtypeStruct((b, s, nv), BF16)],
        scratch_shapes=[pltpu.VMEM((tm + 8, CONV_CH), F32)],
        compiler_params=_cparams(("arbitrary", "arbitrary")),
        name="hyb_front",
    )(x, g.reshape(1, d), w_in, conv_w, *tabs_q, *tabs_k)


def _lane_fold(x, op):
    r = x[:, 0:LANES]
    for c in range(1, x.shape[1] // LANES):
        r = op(r, x[:, c * LANES:(c + 1) * LANES])
    return r


def _diff_attn_kernel(q_ref, k_ref, v_ref, lam_ref, g_ref, o_ref, sbuf, stat, acc, *, tq, hp, lambda_init):
    i = pl.program_id(2)
    lane = lax.broadcasted_iota(I32, (1, LANES), 1)
    qs = []
    for h in range(hp):
        q = q_ref[0, :, h * LANES:(h + 1) * LANES]
        zero = jnp.zeros_like(q)
        qs += [jnp.where(lane < DIFF_DK, q, zero), jnp.where(lane >= DIFF_DK, q, zero)]
    nc = 2 * hp

    hq = tq // 2
    diag = pl.multiple_of(i * tq, tq)

    def pass1(j, carry):
        for c in range(nc):
            h = c // 2
            kb = k_ref[0, pl.ds(pl.multiple_of(j * tq, tq), tq), h * LANES:(h + 1) * LANES]
            sc = _dot_nt(qs[c], kb)
            sbuf[c, j] = sc
            stat[c] = jnp.maximum(stat[c], _lane_fold(sc, jnp.maximum))
        return carry

    stat[...] = jnp.full(stat.shape, NEG_INF, F32)
    lax.fori_loop(0, i, pass1, 0)

    r = lax.broadcasted_iota(I32, (hq, hq), 0)
    cc = lax.broadcasted_iota(I32, (hq, hq), 1)
    tri = cc <= r
    for c in range(nc):
        h = c // 2
        k_lo = k_ref[0, pl.ds(diag, hq), h * LANES:(h + 1) * LANES]
        k_hi = k_ref[0, pl.ds(diag + hq, hq), h * LANES:(h + 1) * LANES]
        s_tl = jnp.where(tri, _dot_nt(qs[c][:hq], k_lo), NEG_INF)
        s_bl = _dot_nt(qs[c][hq:], k_lo)
        s_br = jnp.where(tri, _dot_nt(qs[c][hq:], k_hi), NEG_INF)
        sbuf[c, i, 0:hq, 0:hq] = s_tl
        sbuf[c, i, hq:tq, 0:hq] = s_bl
        sbuf[c, i, hq:tq, hq:tq] = s_br
        stat[c, 0:hq, :] = jnp.maximum(stat[c, 0:hq, :], _lane_fold(s_tl, jnp.maximum))
        stat[c, hq:tq, :] = jnp.maximum(stat[c, hq:tq, :],
                                        jnp.maximum(_lane_fold(s_bl, jnp.maximum), _lane_fold(s_br, jnp.maximum)))
    ms = [jnp.max(stat[c], axis=-1, keepdims=True) for c in range(nc)]

    ones = jnp.ones((tq, LANES), BF16)
    for h in range(hp):
        vb = v_ref[0, pl.ds(diag, tq), h * LANES:(h + 1) * LANES]
        v_ext = jnp.concatenate([vb, ones], axis=1)
        tops, bots = [], []
        for c in (2 * h, 2 * h + 1):
            tops.append(jnp.exp2(sbuf[c, i, 0:hq, 0:hq] - ms[c][:hq]))
            bots.append(jnp.exp2(sbuf[c, i, hq:tq, :] - ms[c][hq:]))
        top = _dot(jnp.concatenate(tops, axis=0).astype(BF16), v_ext[:hq])
        bot = _dot(jnp.concatenate(bots, axis=0).astype(BF16), v_ext)
        acc[h, 0:hq, :] = top[:hq]
        acc[h, hq:tq, :] = bot[:hq]
        acc[h, tq:tq + hq, :] = top[hq:]
        acc[h, tq + hq:2 * tq, :] = bot[hq:]

    def pass2(j, carry):
        for h in range(hp):
            vb = v_ref[0, pl.ds(pl.multiple_of(j * tq, tq), tq), h * LANES:(h + 1) * LANES]
            v_ext = jnp.concatenate([vb, ones], axis=1)
            p0 = jnp.exp2(sbuf[2 * h, j] - ms[2 * h])
            p1 = jnp.exp2(sbuf[2 * h + 1, j] - ms[2 * h + 1])
            acc[h] += _dot(jnp.concatenate([p0, p1], axis=0).astype(BF16), v_ext)
        return carry

    lax.fori_loop(0, i, pass2, 0)

    lf = lam_ref[...]
    lam = (jnp.exp(jnp.sum(lf[0:1] * lf[1:2], keepdims=True))
           - jnp.exp(jnp.sum(lf[2:3] * lf[3:4], keepdims=True)) + lambda_init)
    for h in range(hp):
        a0 = acc[h, 0:tq, :]
        a1 = acc[h, tq:2 * tq, :]
        o = a0[:, :LANES] / a0[:, LANES:] - lam * (a1[:, :LANES] / a1[:, LANES:])
        o_ref[0, :, h * LANES:(h + 1) * LANES] = (_rms(o, g_ref[...]) * (1.0 - lambda_init)).astype(BF16)


def _diff_attn(q, k, v, lam_vecs, subln_g, lambda_init):
    b, s, _ = q.shape
    tq = 512
    hp = 4
    blk = lambda bi, hi, qi: (bi, qi, hi)
    full = lambda bi, hi, qi: (bi, 0, hi)
    return pl.pallas_call(
        functools.partial(_diff_attn_kernel, tq=tq, hp=hp, lambda_init=lambda_init),
        grid=(b, DIFF_HEADS // hp, s // tq),
        in_specs=[pl.BlockSpec((1, tq, hp * LANES), blk),
                  pl.BlockSpec((1, s, hp * LANES), full),
                  pl.BlockSpec((1, s, hp * LANES), full),
                  pl.BlockSpec((4, DIFF_DK), lambda bi, hi, qi: (0, 0)),
                  pl.BlockSpec((1, DIFF_DV), lambda bi, hi, qi: (0, 0))],
        out_specs=pl.BlockSpec((1, tq, hp * LANES), blk),
        out_shape=jax.ShapeDtypeStruct((b, s, DIFF_HEADS * DIFF_DV), BF16),
        scratch_shapes=[pltpu.VMEM((2 * hp, s // tq, tq, tq), F32),
                        pltpu.VMEM((2 * hp, tq, LANES), F32),
                        pltpu.VMEM((hp, 2 * tq, DIFF_DV + LANES), F32)],
        compiler_params=_cparams(("arbitrary", "arbitrary", "arbitrary")),
        name="diff_attn",
    )(q, k, v, lam_vecs, subln_g.reshape(1, DIFF_DV))


def _swa_front_kernel(x_ref, g_ref, w_ref, b_ref, cq_ref, suq_ref, sdq_ref, ck_ref, suk_ref, sdk_ref,
                      q_ref, kv_ref, *, half):
    nq = SWA_Q_HEADS * SWA_HEAD_DIM
    nkv = SWA_KV_HEADS * SWA_HEAD_DIM
    tm = x_ref.shape[1]
    rh = tm // FRONT_CHAINS
    starts = [ch * rh for ch in range(FRONT_CHAINS)]
    hs = [_rms(x_ref[0, r0:r0 + rh, :], g_ref[...]).astype(BF16) for r0 in starts]
    for r0, h in zip(starts, hs):
        cq, suq, sdq = cq_ref[r0:r0 + rh, :], suq_ref[r0:r0 + rh, :], sdq_ref[r0:r0 + rh, :]
        pq = _dot(h, w_ref[:, 0:nq]) + b_ref[:, 0:nq]
        for j in range(nq // LANES):
            lo = j * LANES
            q_ref[0, r0:r0 + rh, lo:lo + LANES] = _rope_chunk(pq[:, lo:lo + LANES], cq, suq, sdq, half).astype(BF16)
    for r0, h in zip(starts, hs):
        ck, suk, sdk = ck_ref[r0:r0 + rh, :], suk_ref[r0:r0 + rh, :], sdk_ref[r0:r0 + rh, :]
        pkv = _dot(h, w_ref[:, nq:]) + b_ref[:, nq:]
        for j in range(nkv // LANES):
            lo = j * LANES
            kv_ref[0, r0:r0 + rh, lo:lo + LANES] = _rope_chunk(pkv[:, lo:lo + LANES], ck, suk, sdk, half).astype(BF16)
        kv_ref[0, r0:r0 + rh, nkv:] = pkv[:, nkv:].astype(BF16)


def _swa_front(x, g, w_qkv, b_qkv, tabs_q, tabs_k, half):
    b, s, d = x.shape
    tm = ROW_TILE
    n = w_qkv.shape[1]
    nq = SWA_Q_HEADS * SWA_HEAD_DIM
    nkv = SWA_KV_HEADS * SWA_HEAD_DIM
    tab_spec = pl.BlockSpec((tm, LANES), lambda bi, si: (si, 0))
    row = lambda width: pl.BlockSpec((1, tm, width), lambda bi, si: (bi, si, 0))
    return pl.pallas_call(
        functools.partial(_swa_front_kernel, half=half),
        grid=(b, s // tm),
        in_specs=[row(d),
                  pl.BlockSpec((1, d), lambda bi, si: (0, 0)),
                  pl.BlockSpec((d, n), lambda bi, si: (0, 0)),
                  pl.BlockSpec((1, n), lambda bi, si: (0, 0)),
                  tab_spec, tab_spec, tab_spec, tab_spec, tab_spec, tab_spec],
        out_specs=[row(nq), row(2 * nkv)],
        out_shape=[jax.ShapeDtypeStruct((b, s, nq), BF16),
                   jax.ShapeDtypeStruct((b, s, 2 * nkv), BF16)],
        compiler_params=_cparams(("arbitrary", "arbitrary")),
        name="swa_front",
    )(x, g.reshape(1, d), w_qkv, b_qkv.reshape(1, n), *tabs_q, *tabs_k)


def _swa_head_order():
    g_sz = SWA_Q_HEADS // SWA_KV_HEADS
    order = []
    for slab in range(SWA_Q_HEADS // 2):
        pair, j = slab // g_sz, slab % g_sz
        order += [(2 * pair) * g_sz + j, (2 * pair + 1) * g_sz + j]
    return order


def _swa_attn_kernel(sink_ref, q_ref, kvp_ref, kvc_ref, o_ref, *, nblk):
    i = pl.program_id(1)
    hd = SWA_HEAD_DIM
    nkv = SWA_KV_HEADS * hd
    g_sz = SWA_Q_HEADS // SWA_KV_HEADS
    kv = jnp.concatenate([kvp_ref[0], kvc_ref[0]], axis=0)
    r = lax.broadcasted_iota(I32, (BLOCK, 2 * BLOCK), 0)
    c = lax.broadcasted_iota(I32, (BLOCK, 2 * BLOCK), 1)
    rel = c - BLOCK - r
    in_win = (rel <= 0) & (rel > -SWA_WINDOW)
    lane = lax.broadcasted_iota(I32, (1, LANES), 1)
    lo_half = lane < hd
    ones = jnp.ones((2 * BLOCK, LANES), BF16)
    for n in range(nblk):
        mask = in_win & ((c >= BLOCK) | (i > 0)) if n == 0 else in_win
        keys = kv[n * BLOCK:(n + 2) * BLOCK]
        for pair in range(SWA_KV_HEADS // 2):
            k2 = keys[:, pair * LANES:(pair + 1) * LANES]
            v2 = keys[:, nkv + pair * LANES:nkv + (pair + 1) * LANES]
            v_ext = jnp.concatenate([v2, ones], axis=1)
            pieces = []
            for j in range(g_sz):
                slab = pair * g_sz + j
                qs = q_ref[0, n * BLOCK:(n + 1) * BLOCK, slab * LANES:(slab + 1) * LANES]
                zero = jnp.zeros_like(qs)
                pieces += [jnp.where(lo_half, qs, zero), jnp.where(lo_half, zero, qs)]
            sc = _dot_nt(jnp.concatenate(pieces, axis=0), k2)
            probs, tails = [], []
            for pc in range(2 * g_sz):
                scp = jnp.where(mask, sc[pc * BLOCK:(pc + 1) * BLOCK], NEG_INF)
                sink = sink_ref[2 * g_sz * pair + pc] * LOG2E
                m = jnp.maximum(jnp.max(scp, axis=-1, keepdims=True), sink)
                probs.append(jnp.exp2(scp - m).astype(BF16))
                tails.append(jnp.exp2(sink - m))
            pv = _dot(jnp.concatenate(probs, axis=0), v_ext)
            for j in range(g_sz):
                slab = pair * g_sz + j
                halves = []
                for hf in range(2):
                    pc = 2 * j + hf
                    blk = pv[pc * BLOCK:(pc + 1) * BLOCK]
                    halves.append(blk[:, :LANES] / (blk[:, LANES:] + tails[pc]))
                o_ref[0, n * BLOCK:(n + 1) * BLOCK, slab * LANES:(slab + 1) * LANES] = (
                    jnp.where(lo_half, halves[0], halves[1]).astype(BF16))


def _swa_attn(q, kv, sinks):
    b, s, nq = q.shape
    nblk = SWA_BLOCKS_PER_STEP
    tq = nblk * BLOCK
    return pl.pallas_call(
        functools.partial(_swa_attn_kernel, nblk=nblk),
        grid_spec=pltpu.PrefetchScalarGridSpec(
            num_scalar_prefetch=1,
            grid=(b, s // tq),
            in_specs=[pl.BlockSpec((1, tq, nq), lambda bi, ni, sk: (bi, ni, 0)),
                      pl.BlockSpec((1, BLOCK, kv.shape[2]), lambda bi, ni, sk: (bi, jnp.maximum(ni * nblk - 1, 0), 0)),
                      pl.BlockSpec((1, tq, kv.shape[2]), lambda bi, ni, sk: (bi, ni, 0))],
            out_specs=pl.BlockSpec((1, tq, nq), lambda bi, ni, sk: (bi, ni, 0))),
        out_shape=jax.ShapeDtypeStruct((b, s, nq), BF16),
        compiler_params=_cparams(("arbitrary", "arbitrary")),
        name="swa_attn",
    )(sinks, q, kv, kv)


def _post_kernel(*refs, tm, n_a, has_bias):
    x_ref = refs[0]
    a_refs = refs[1:1 + n_a]
    k = 1 + n_a
    wo_refs = refs[k:k + n_a]
    k += n_a
    if has_bias:
        bo_ref = refs[k]
        k += 1
    (gx_ref, wq_ref, mkv_ref, wxo_ref, gf_ref, wr_ref, br_ref,
     x2_ref, h2_ref, mi_ref, mw_ref, cnt_ref, cnt_acc, cnt_col) = refs[k:]
    first = (pl.program_id(0) == 0) & (pl.program_id(1) == 0)

    @pl.when(first)
    def _():
        cnt_acc[...] = jnp.zeros_like(cnt_acc)
        cnt_col[...] = jnp.zeros_like(cnt_col)

    xw = XATTN_HEADS * XATTN_HEAD_DIM
    ones = jnp.ones((mkv_ref.shape[1], LANES), BF16)

    rh = tm // POST_CHAINS

    def out_proj(r0):
        acc = _dot(a_refs[0][0, r0:r0 + rh, :], wo_refs[0][...])
        for a_ref, w_ref in zip(a_refs[1:], wo_refs[1:]):
            acc = acc + _dot(a_ref[0, r0:r0 + rh, :], w_ref[...])
        if has_bias:
            acc = acc + bo_ref[...]
        return x_ref[0, r0:r0 + rh, :] + acc

    def q_proj(x1):
        hx = _rms(x1, gx_ref[...]).astype(BF16)
        return (_dot(hx, wq_ref[...]) * (XATTN_HEAD_DIM ** -0.5 * LOG2E)).astype(BF16)

    def mem_attn(qx):
        outs = []
        for hh in range(XATTN_HEADS):
            lo = hh * XATTN_HEAD_DIM
            mk = mkv_ref[0, :, lo:lo + XATTN_HEAD_DIM]
            mv = jnp.concatenate([mkv_ref[0, :, xw + lo:xw + lo + XATTN_HEAD_DIM], ones], axis=1)
            sc = _dot_nt(qx[:, lo:lo + XATTN_HEAD_DIM], mk)
            m = jnp.max(sc, axis=-1, keepdims=True)
            pv = _dot(jnp.exp2(sc - m).astype(BF16), mv)
            outs.append((pv[:, :LANES] / pv[:, LANES:]).astype(BF16))
        return jnp.concatenate(outs, axis=-1)

    def o_proj(r0, x1, ox):
        x2 = x1 + _dot(ox, wxo_ref[...])
        x2_ref[0, r0:r0 + rh, :] = x2
        return x2

    def router(r0, x2):
        h2 = _rms(x2, gf_ref[...])
        _to_token_tiles(h2_ref, h2, rh, r0)
        return _dot(h2.astype(BF16), wr_ref[...]) + br_ref[...]

    starts = [c * rh for c in range(POST_CHAINS)]
    x1s = [out_proj(r0) for r0 in starts]
    qxs = [q_proj(x1) for x1 in x1s]
    oxs = [mem_attn(qx) for qx in qxs]
    x2s = [o_proj(r0, x1, ox) for r0, x1, ox in zip(starts, x1s, oxs)]
    logits = jnp.concatenate([router(r0, x2) for r0, x2 in zip(starts, x2s)], axis=0)

    lt = logits.T
    ex = lt[0:N_EXPERTS]
    grp = lt[N_EXPERTS:N_EXPERTS + SUBLANES]
    grow = lax.broadcasted_iota(I32, grp.shape, 0).astype(F32)
    lg = jnp.where(grow < N_GROUPS, grp, NEG_INF)
    mg = jnp.max(lg, axis=0, keepdims=True)
    g_sel = jnp.min(jnp.where(lg == mg, grow, float(SUBLANES)), axis=0, keepdims=True)
    p_g = 1.0 / jnp.sum(jnp.exp(lg - mg), axis=0, keepdims=True)
    erow = lax.broadcasted_iota(I32, ex.shape, 0).astype(F32)
    e_lo = g_sel * EXPERTS_PER_GROUP
    le = jnp.where((erow >= e_lo) & (erow < e_lo + EXPERTS_PER_GROUP), ex, NEG_INF)
    big = float(N_EXPERTS)
    m1 = jnp.max(le, axis=0, keepdims=True)
    i1 = jnp.min(jnp.where(le == m1, erow, big), axis=0, keepdims=True)
    le2 = jnp.where(erow == i1, NEG_INF, le)
    m2 = jnp.max(le2, axis=0, keepdims=True)
    i2 = jnp.min(jnp.where(le2 == m2, erow, big), axis=0, keepdims=True)
    t = jnp.exp(m2 - m1)
    w1 = p_g / (1.0 + t)
    w2 = p_g * t / (1.0 + t)

    oh1 = erow == i1
    oh2 = erow == i2
    oh = jnp.where(oh1 | oh2, 1.0, 0.0)
    rr = lax.broadcasted_iota(I32, (tm, tm), 0)
    cc = lax.broadcasted_iota(I32, (tm, tm), 1)
    earlier = jnp.where(rr < cc, 1.0, 0.0).astype(BF16)
    before = _dot(oh.astype(BF16), earlier) + cnt_col[...]
    r1 = jnp.sum(jnp.where(oh1, before, 0.0), axis=0, keepdims=True)
    r2 = jnp.sum(jnp.where(oh2, before, 0.0), axis=0, keepdims=True)
    cnt_col[...] = cnt_col[...] + jnp.sum(oh, axis=1, keepdims=True)
    oh_pad = jnp.concatenate([oh, jnp.zeros((LANES - N_EXPERTS, tm), F32)], axis=0).astype(BF16)
    cnt_new = cnt_acc[...] + _dot_nt(jnp.ones((SUBLANES, tm), BF16), oh_pad)[0:1]
    cnt_acc[...] = cnt_new
    cnt_ref[...] = cnt_new.astype(I32)

    zrow = jnp.zeros((1, tm), F32)
    mi_ref[...] = jnp.concatenate([i1, i2, r1, r2, zrow, zrow, zrow, zrow], axis=0).astype(I32)
    wt = jnp.concatenate([w1, w2, jnp.zeros((LANES - TOP_K, tm), F32)], axis=0)
    mw_ref[...] = wt.T


def _post_mixer(x, a_list, wo_list, b_out, g_x, w_q, mkv, w_xo, g_f, w_rt, b_rt):
    b, s, d = x.shape
    tm = ROW_TILE
    ns = s // tm
    t = b * s
    n_a = len(a_list)
    has_bias = b_out is not None
    const2 = lambda bi, si: (0, 0)
    row = lambda width: pl.BlockSpec((1, tm, width), lambda bi, si: (bi, si, 0))
    in_specs = [row(d)] + [row(a.shape[2]) for a in a_list]
    in_specs += [pl.BlockSpec(w.shape, const2) for w in wo_list]
    args = [x, *a_list, *wo_list]
    if has_bias:
        in_specs.append(pl.BlockSpec((1, d), const2))
        args.append(b_out.reshape(1, d))
    xw = w_q.shape[1]
    in_specs += [pl.BlockSpec((1, d), const2),
                 pl.BlockSpec((d, xw), const2),
                 pl.BlockSpec((1, mkv.shape[1], mkv.shape[2]), lambda bi, si: (bi, 0, 0)),
                 pl.BlockSpec((xw, d), const2),
                 pl.BlockSpec((1, d), const2),
                 pl.BlockSpec((d, LANES), const2),
                 pl.BlockSpec((1, LANES), const2)]
    args += [g_x.reshape(1, d), w_q, mkv, w_xo, g_f.reshape(1, d), w_rt, b_rt]
    out_specs = [row(d),
                 pl.BlockSpec((tm * TOKEN_ROWS, LANES), lambda bi, si: (bi * ns + si, 0)),
                 pl.BlockSpec((8, tm), lambda bi, si: (0, bi * ns + si)),
                 pl.BlockSpec((tm, LANES), lambda bi, si: (bi * ns + si, 0)),
                 pl.BlockSpec((1, LANES), const2)]
    out_shape = [jax.ShapeDtypeStruct((b, s, d), F32),
                 jax.ShapeDtypeStruct((t * TOKEN_ROWS, LANES), U32),
                 jax.ShapeDtypeStruct((8, t), I32),
                 jax.ShapeDtypeStruct((t, LANES), F32),
                 jax.ShapeDtypeStruct((1, LANES), I32)]
    return pl.pallas_call(
        functools.partial(_post_kernel, tm=tm, n_a=n_a, has_bias=has_bias),
        grid=(b, ns),
        in_specs=in_specs,
        out_specs=out_specs,
        out_shape=out_shape,
        scratch_shapes=[pltpu.VMEM((1, LANES), F32), pltpu.VMEM((N_EXPERTS, 1), F32)],
        compiler_params=_cparams(("arbitrary", "arbitrary")),
        name="post_mixer",
    )(*args)


def _to_token_tiles(dst_ref, val, rows, tok0=0):
    half = TOKEN_ROWS * LANES
    for j in range(TOKEN_ROWS):
        hi = val[:, j * LANES:(j + 1) * LANES].astype(BF16).astype(F32)
        lo = val[:, half + j * LANES:half + (j + 1) * LANES].astype(BF16).astype(F32)
        word = lax.bitcast_convert_type(hi, U32) | (lax.bitcast_convert_type(lo, U32) >> 16)
        dst_ref[pl.ds(tok0 * TOKEN_ROWS + j, rows, stride=TOKEN_ROWS), :] = word


def _from_token_tiles(src_ref, rows, tok0=0):
    his, los = [], []
    for j in range(TOKEN_ROWS):
        word = src_ref[pl.ds(tok0 * TOKEN_ROWS + j, rows, stride=TOKEN_ROWS), :]
        his.append(lax.bitcast_convert_type(word & jnp.uint32(0xFFFF0000), F32))
        los.append(lax.bitcast_convert_type(word << 16, F32))
    return jnp.concatenate(his + los, axis=1)


def _token_slice(ref, tok):
    return ref.at[pl.ds(pl.multiple_of(tok * TOKEN_ROWS, TOKEN_ROWS), TOKEN_ROWS)]


def _pos_kernel(off_ref, mi_ref, pos_ref):
    e = mi_ref[0:TOP_K, :]
    pos = mi_ref[TOP_K:2 * TOP_K, :]
    for j in range(N_EXPERTS):
        pos = pos + jnp.where(e == j, off_ref[j], 0)
    pos_ref[...] = pos


def _sorted_positions(off, meta_i):
    t = meta_i.shape[1]
    return pl.pallas_call(
        _pos_kernel,
        grid_spec=pltpu.PrefetchScalarGridSpec(
            num_scalar_prefetch=1,
            grid=(1,),
            in_specs=[pl.BlockSpec((8, t), lambda i, off: (0, 0))],
            out_specs=pl.BlockSpec((TOP_K, t), lambda i, off: (0, 0))),
        out_shape=jax.ShapeDtypeStruct((TOP_K, t), I32),
        compiler_params=_cparams(("arbitrary",)),
        name="moe_positions",
    )(off, meta_i)


def _tile_major(pos, tm):
    t = pos.shape[1]
    return pos.reshape(TOP_K, t // tm, tm).transpose(1, 0, 2).reshape(-1)


def _dispatch_kernel(pos_ref, h_ref, xs_ref, sem, *, tm):
    def issue(g, _):
        for u in range(DMA_UNROLL):
            r = g * DMA_UNROLL + u
            for kk in range(TOP_K):
                pos = pos_ref[kk * tm + r]
                pltpu.make_async_copy(_token_slice(h_ref, r), _token_slice(xs_ref, pos), sem).start(priority=kk)
        return 0

    lax.fori_loop(0, tm // DMA_UNROLL, issue, 0)
    for _ in range(TOP_K):
        pltpu.make_async_copy(h_ref, xs_ref.at[pl.ds(0, tm * TOKEN_ROWS)], sem).wait()


def _dispatch(pos, h2t, n_rows):
    tm = DISPATCH_TILE
    t = h2t.shape[0] // TOKEN_ROWS
    return pl.pallas_call(
        functools.partial(_dispatch_kernel, tm=tm),
        grid=(t // tm,),
        in_specs=[pl.BlockSpec((TOP_K * tm,), lambda i: (i,), memory_space=pltpu.SMEM),
                  pl.BlockSpec((tm * TOKEN_ROWS, LANES), lambda i: (i, 0))],
        out_specs=pl.BlockSpec(memory_space=pl.ANY),
        scratch_shapes=[pltpu.SemaphoreType.DMA],
        out_shape=jax.ShapeDtypeStruct((n_rows * TOKEN_ROWS, LANES), U32),
        compiler_params=_cparams(("arbitrary",)),
        name="moe_dispatch",
    )(_tile_major(pos, tm), h2t)


def _expert_kernel(tblk_ref, texp_ref, tn_ref, tnxt_ref, tpar_ref, xs_ref, wgu_hbm, wdn_hbm, ys_ref,
                   wgu_bf, wdn_bf, wgu_f, wdn_f, sems, *, tm, layer):
    i = pl.program_id(0)
    n_valid = tn_ref[i]
    expert = texp_ref[i]
    slot = tpar_ref[i]
    new_expert = (i == 0) | (expert != texp_ref[jnp.maximum(i - 1, 0)])

    def weight_copies(ex, sl):
        return (pltpu.make_async_copy(wgu_hbm.at[layer, ex], wgu_f.at[sl], sems.at[sl]),
                pltpu.make_async_copy(wdn_hbm.at[layer, ex], wdn_f.at[sl], sems.at[sl]))

    @pl.when(i == 0)
    def _():
        for cp in weight_copies(expert, slot):
            cp.start()

    @pl.when(new_expert)
    def _():
        @pl.when(tnxt_ref[i] >= 0)
        def _():
            for cp in weight_copies(tnxt_ref[i], 1 - slot):
                cp.start()

        for cp in weight_copies(expert, slot):
            cp.wait()
        wgu_bf[...] = wgu_f[slot].astype(BF16)
        wdn_bf[...] = wdn_f[slot].astype(BF16)

    @pl.when(n_valid > 0)
    def _():
        rh = tm // EXPERT_CHAINS
        starts = [c * rh for c in range(EXPERT_CHAINS)]
        row = lax.broadcasted_iota(I32, (rh, 1), 0)
        gus = []
        for r0 in starts:
            x = jnp.where(row + r0 < n_valid, _from_token_tiles(xs_ref, rh, r0), 0.0).astype(BF16)
            gus.append(_dot(x, wgu_bf[...]))
        ys = []
        for gu in gus:
            hid = (gu[:, :EXPERT_HIDDEN] / (1.0 + jnp.exp(-gu[:, :EXPERT_HIDDEN])) * gu[:, EXPERT_HIDDEN:]).astype(BF16)
            ys.append(_dot(hid, wdn_bf[...]))
        for r0, y in zip(starts, ys):
            _to_token_tiles(ys_ref, y, rh, r0)


def _experts(plan, xs, w_gu, w_dn, layer):
    tm = EXPERT_TILE
    nt = xs.shape[0] // (tm * TOKEN_ROWS)
    gu_shape, dn_shape = w_gu.shape[2:], w_dn.shape[2:]
    rows = lambda i, tb, te, tn, tx, tp: (tb[i], 0)
    return pl.pallas_call(
        functools.partial(_expert_kernel, tm=tm, layer=layer),
        grid_spec=pltpu.PrefetchScalarGridSpec(
            num_scalar_prefetch=5,
            grid=(nt,),
            in_specs=[pl.BlockSpec((tm * TOKEN_ROWS, LANES), rows),
                      pl.BlockSpec(memory_space=pl.ANY),
                      pl.BlockSpec(memory_space=pl.ANY)],
            out_specs=pl.BlockSpec((tm * TOKEN_ROWS, LANES), rows),
            scratch_shapes=[pltpu.VMEM(gu_shape, BF16), pltpu.VMEM(dn_shape, BF16),
                            pltpu.VMEM((2,) + gu_shape, F32), pltpu.VMEM((2,) + dn_shape, F32),
                            pltpu.SemaphoreType.DMA((2,))]),
        out_shape=jax.ShapeDtypeStruct(xs.shape, U32),
        compiler_params=_cparams(("arbitrary",)),
        name="moe_experts",
    )(*plan, xs, w_gu, w_dn)


def _moe_combine_tile(pos_cur, pos_nxt, pos_nx2, mw_ref, x, ys_ref, ybuf, sems, i, n, tm):
    def gather(pos_ref, slot):
        def issue(g, _):
            for u in range(DMA_UNROLL):
                r = g * DMA_UNROLL + u
                for kk in range(TOP_K):
                    pos = pos_ref[kk * tm + r]
                    pltpu.make_async_copy(_token_slice(ys_ref, pos), _token_slice(ybuf.at[slot, kk], r),
                                          sems.at[slot]).start(priority=kk)
            return 0

        lax.fori_loop(0, tm // DMA_UNROLL, issue, 0)

    @pl.when(i == 0)
    def _():
        gather(pos_cur, 0)

        @pl.when(n > 1)
        def _():
            gather(pos_nxt, 1)

    @pl.when(i + 2 < n)
    def _():
        gather(pos_nx2, (i + 2) % COMBINE_SLOTS)

    slot = i % COMBINE_SLOTS
    for kk in range(TOP_K):
        pltpu.make_async_copy(ys_ref.at[pl.ds(0, tm * TOKEN_ROWS)], ybuf.at[slot, kk], sems.at[slot]).wait()
    mw = mw_ref[...]
    return (x + mw[:, 0:1] * _from_token_tiles(ybuf.at[slot, 0], tm)
            + mw[:, 1:2] * _from_token_tiles(ybuf.at[slot, 1], tm))


def _combine_kernel(pos_cur, pos_nxt, pos_nx2, mw_ref, x_ref, ys_ref, *rest, tm, final):
    if final:
        g_ref, o_ref, ybuf, sems = rest
    else:
        o_ref, ybuf, sems = rest
    x3 = _moe_combine_tile(pos_cur, pos_nxt, pos_nx2, mw_ref, x_ref[...], ys_ref, ybuf, sems,
                           pl.program_id(0), pl.num_programs(0), tm)
    if final:
        x3 = _rms(x3, g_ref[...])
    o_ref[...] = x3


def _combine(moe, g_final):
    pos, meta_w, x2, ys = moe
    d = x2.shape[-1]
    x2d = x2.reshape(-1, d)
    t = x2d.shape[0]
    tm = COMBINE_TILE
    nt = t // tm
    final = g_final is not None
    in_specs = [pl.BlockSpec((TOP_K * tm,), lambda i: (i,), memory_space=pltpu.SMEM),
                pl.BlockSpec((TOP_K * tm,), lambda i: (jnp.minimum(i + 1, nt - 1),), memory_space=pltpu.SMEM),
                pl.BlockSpec((TOP_K * tm,), lambda i: (jnp.minimum(i + 2, nt - 1),), memory_space=pltpu.SMEM),
                pl.BlockSpec((tm, LANES), lambda i: (i, 0)),
                pl.BlockSpec((tm, d), lambda i: (i, 0)),
                pl.BlockSpec(memory_space=pl.ANY)]
    pos_flat = _tile_major(pos, tm)
    args = [pos_flat, pos_flat, pos_flat, meta_w, x2d, ys]
    if final:
        in_specs.append(pl.BlockSpec((1, d), lambda i: (0, 0)))
        args.append(g_final.reshape(1, d))
    return pl.pallas_call(
        functools.partial(_combine_kernel, tm=tm, final=final),
        grid=(nt,),
        in_specs=in_specs,
        out_specs=pl.BlockSpec((tm, d), lambda i: (i, 0)),
        scratch_shapes=[pltpu.VMEM((COMBINE_SLOTS, TOP_K, tm * TOKEN_ROWS, LANES), U32),
                        pltpu.SemaphoreType.DMA((COMBINE_SLOTS,))],
        out_shape=jax.ShapeDtypeStruct((t, d), F32),
        compiler_params=_cparams(("arbitrary",)),
        name="moe_combine",
    )(*args)


def _plan_kernel(cnt_ref, off_ref, blk_ref, exp_ref, nv_ref, nxt_ref, par_ref, *, tm, n_tiles):
    shift = tm.bit_length() - 1

    def clear(i, _):
        nxt_ref[i] = -1
        return 0

    lax.fori_loop(0, n_tiles, clear, 0)

    def per_expert(e, carry):
        row0, run, prev_first = carry
        c = cnt_ref[0, e]
        ntile = lax.shift_right_logical(c + (tm - 1), shift)
        off_ref[e] = row0
        t0 = lax.shift_right_logical(row0, shift)

        def fill(j, _):
            blk_ref[t0 + j] = t0 + j
            exp_ref[t0 + j] = e
            nv_ref[t0 + j] = jnp.minimum(c - j * tm, tm)
            par_ref[t0 + j] = run & 1
            return 0

        lax.fori_loop(0, ntile, fill, 0)
        has = ntile > 0

        @pl.when(has & (prev_first >= 0))
        def _():
            nxt_ref[prev_first] = e

        return (row0 + lax.shift_left(ntile, shift), run + has.astype(I32), jnp.where(has, t0, prev_first))

    total, _, _ = lax.fori_loop(0, N_EXPERTS, per_expert, (jnp.int32(0), jnp.int32(0), jnp.int32(-1)))
    used = lax.shift_right_logical(total, shift)
    last = jnp.maximum(used - 1, 0)

    def tail(i, _):
        blk_ref[i] = last
        exp_ref[i] = exp_ref[last]
        nv_ref[i] = 0
        par_ref[i] = par_ref[last]
        return 0

    lax.fori_loop(used, n_tiles, tail, 0)


def _moe_plan(counts, n_tiles):
    tm = EXPERT_TILE
    assert tm & (tm - 1) == 0
    smem = pl.BlockSpec(memory_space=pltpu.SMEM)
    return pl.pallas_call(
        functools.partial(_plan_kernel, tm=tm, n_tiles=n_tiles),
        in_specs=[smem],
        out_specs=[smem] * 6,
        out_shape=[jax.ShapeDtypeStruct((N_EXPERTS,), I32)] + [jax.ShapeDtypeStruct((n_tiles,), I32)] * 5,
        name="moe_plan",
    )(counts)


def _moe(x2, h2t, meta_i, meta_w, counts, w_gu, w_dn, layer):
    b, s, d = x2.shape
    assert d == 2 * TOKEN_ROWS * LANES
    n_rows = b * s * TOP_K + N_EXPERTS * EXPERT_TILE
    off, *tile_plan = _moe_plan(counts, n_rows // EXPERT_TILE)
    pos = _sorted_positions(off, meta_i)
    xs = _dispatch(pos, h2t, n_rows)
    ys = _experts(tile_plan, xs, w_gu, w_dn, layer)
    return pos, meta_w, x2, ys


def _router_weights(w_group, b_group, w_router, b_router):
    d = w_group.shape[0]
    pad = LANES - N_EXPERTS - N_GROUPS
    w = jnp.concatenate([w_router, w_group, jnp.zeros((d, pad), F32)], axis=1)
    bias = jnp.concatenate([b_router, b_group, jnp.zeros((pad,), F32)]).reshape(1, LANES)
    return w.astype(BF16), bias


def kernel(x, mem, mem_norm, mem_w_kv, norm_mix, norm_xattn, norm_ffn, hyb_w_in, hyb_conv_w, diff_lambda, diff_subln, hyb_w_out, swa_w_qkv, swa_b_qkv, swa_sinks, swa_w_out, swa_b_out, xattn_w_q, xattn_w_o, moe_w_group, moe_b_group, moe_w_router, moe_b_router, moe_w_gate_up, moe_w_down, final_norm):
    b, s, d = x.shape
    m = mem.shape[1]
    depth = norm_mix.shape[0]
    assert d == 2 * TOKEN_ROWS * LANES and s % ROW_TILE == 0 and s % (SWA_BLOCKS_PER_STEP * BLOCK) == 0
    assert (b * s) % DISPATCH_TILE == 0 and (b * s) % COMBINE_TILE == 0 and DISPATCH_TILE % DMA_UNROLL == 0
    mem_tile = math.gcd(b * m, ROW_TILE)
    mkv = _norm_proj(mem.reshape(b * m, d), mem_norm, mem_w_kv.astype(BF16), mem_tile).reshape(b, m, -1)

    scale = DIFF_DK ** -0.5 * LOG2E
    cq, suq, sdq, half = _rope_lane_tables(s, DIFF_DK, scale)
    ck, suk, sdk, _ = _rope_lane_tables(s, DIFF_DK, 1.0)
    tabs_q, tabs_k = (cq, suq, sdq), (ck, suk, sdk)

    moe = None
    for l in range(depth):
        if moe is not None:
            x = _combine(moe, None).reshape(b, s, d)
        if l % 2 == 0:
            e = l // 2
            lambda_init = 0.8 - 0.6 * math.exp(-0.3 * l)
            ya, q, k, v = _hyb_front(x, norm_mix[l], hyb_w_in[e].astype(BF16), hyb_conv_w[e], tabs_q, tabs_k, half)
            o = _diff_attn(q, k, v, diff_lambda[e], diff_subln[e], lambda_init)
            w_out = hyb_w_out[e].astype(BF16)
            a_list, wo_list, b_out = [ya, o], [w_out[:CONV_CH], w_out[CONV_CH:]], None
        else:
            e = l // 2
            order = _swa_head_order()
            hd, nq = SWA_HEAD_DIM, SWA_Q_HEADS * SWA_HEAD_DIM
            heads = lambda a, axis: [lax.slice_in_dim(a, h * hd, (h + 1) * hd, axis=axis) for h in order]
            w_all = swa_w_qkv[e].astype(BF16)
            w_qkv = jnp.concatenate(heads(w_all, 1) + [w_all[:, nq:]], axis=1)
            b_qkv = jnp.concatenate(heads(swa_b_qkv[e], 0) + [swa_b_qkv[e][nq:]])
            sinks = jnp.stack([swa_sinks[e][h] for h in order])
            q, kv = _swa_front(x, norm_mix[l], w_qkv, b_qkv, tabs_q, tabs_k, half)
            o = _swa_attn(q, kv, sinks)
            w_out = jnp.concatenate(heads(swa_w_out[e].astype(BF16), 0), axis=0)
            a_list, wo_list, b_out = [o], [w_out], swa_b_out[e]
        w_rt, b_rt = _router_weights(moe_w_group[l], moe_b_group[l], moe_w_router[l], moe_b_router[l])
        x2, h2, meta_i, meta_w, counts = _post_mixer(
            x, a_list, wo_list, b_out, norm_xattn[l], xattn_w_q[l].astype(BF16), mkv,
            xattn_w_o[l].astype(BF16), norm_ffn[l], w_rt, b_rt)
        moe = _moe(x2, h2, meta_i, meta_w, counts, moe_w_gate_up, moe_w_down, l)
    return _combine(moe, final_norm).reshape(b, s, d)
```
